```python
import math
import jax, jax.numpy as jnp
from jax import lax
import numpy as np

D_MODEL = 2048
BATCH = 8
SEQ = 4096
DEPTH = 1

CHUNK = 64
Q_BLOCK = 128
EPS = 1e-6
ROPE_THETA = 10000.0

RET_HEADS = 8
RET_QK_DIM = 128
RET_V_DIM = 256
RET_QK_W = RET_HEADS * RET_QK_DIM
RET_V_W = RET_HEADS * RET_V_DIM

MLA_HEADS = 16
Q_LORA = 512
KV_LORA = 512
QK_NOPE = 128
QK_ROPE = 64
V_HEAD = 128
MLA_QK_DIM = QK_NOPE + QK_ROPE
MLA_V_W = MLA_HEADS * V_HEAD

D_FF = 4 * D_MODEL

IN_SPLITS = (RET_QK_W, RET_QK_W, RET_V_W, RET_V_W, Q_LORA, KV_LORA, QK_ROPE, D_MODEL, D_MODEL)
N_IN = sum(IN_SPLITS)

kernel_name = "hybrid_retention_mla_gated_encoder"


def _rmsnorm(x, g):
    x32 = x.astype(jnp.float32)
    y = x32 * lax.rsqrt(jnp.mean(x32 * x32, axis=-1, keepdims=True) + EPS)
    return (y * g.astype(jnp.float32)).astype(x.dtype)


def _rope(t, positions):
    half = t.shape[-1] // 2
    inv = ROPE_THETA ** (-jnp.arange(half, dtype=jnp.float32) / half)
    ang = positions.astype(jnp.float32)[..., None] * inv
    cos = jnp.cos(ang)[:, :, None, :]
    sin = jnp.sin(ang)[:, :, None, :]
    t32 = t.astype(jnp.float32)
    t1, t2 = t32[..., :half], t32[..., half:]
    out = jnp.concatenate([t1 * cos - t2 * sin, t2 * cos + t1 * sin], axis=-1)
    return out.astype(t.dtype)


def _retention(q, k, v):
    B, S, H, dk = q.shape
    dv = v.shape[-1]
    nc = S // CHUNK
    log_gamma = jnp.log(1.0 - 2.0 ** (-5.0 - jnp.arange(H, dtype=jnp.float32)))

    def to_chunks(t):
        return t.reshape(B, nc, CHUNK, H, t.shape[-1]).transpose(1, 0, 3, 2, 4)

    pos = jnp.arange(CHUNK, dtype=jnp.float32)
    lg = log_gamma[:, None]
    intra = jnp.exp(lg[..., None] * jnp.abs(pos[:, None] - pos[None, :]))
    q_dec = jnp.exp(lg * (pos + 1.0))
    k_dec = jnp.exp(lg * (CHUNK - 1.0 - pos))
    c_dec = jnp.exp(log_gamma * CHUNK)

    def step(state, inp):
        qi, ki, vi = inp
        s = jnp.einsum('bhnd,bhmd->bhnm', qi, ki) * intra
        o = (jnp.einsum('bhnm,bhme->bhne', s, vi)
             + jnp.einsum('bhnd,bhde->bhne', qi * q_dec[..., None], state))
        state = state * c_dec[:, None, None] + jnp.einsum(
            'bhmd,bhme->bhde', ki * k_dec[..., None], vi)
        return state, o

    state0 = jnp.zeros((B, H, dk, dv), jnp.float32)
    _, o = lax.scan(step, state0, (to_chunks(q), to_chunks(k), to_chunks(v)))
    return o.transpose(1, 0, 3, 2, 4).reshape(B, S, H, dv)


def _block_causal_attention(q, k, v, scale):
    B, S, H, dqk = q.shape
    nb = S // Q_BLOCK
    qb = q.reshape(B, nb, Q_BLOCK, H, dqk).transpose(1, 0, 2, 3, 4)
    key_chunk = jnp.arange(S) // CHUNK

    def one(args):
        qi, bi = args
        q_chunk = (bi * Q_BLOCK + jnp.arange(Q_BLOCK)) // CHUNK
        mask = key_chunk[None, :] <= q_chunk[:, None]
        s = jnp.einsum('bqhd,bkhd->bhqk', qi, k,
                       preferred_element_type=jnp.float32) * scale
        s = jnp.where(mask, s, -jnp.inf)
        p = jax.nn.softmax(s, axis=-1).astype(v.dtype)
        return jnp.einsum('bhqk,bkhe->bqhe', p, v)

    o = lax.map(one, (qb, jnp.arange(nb)))
    return o.transpose(1, 0, 2, 3, 4).reshape(B, S, H, v.shape[-1])


def _mixer(u, positions, w_in, ret_norm_g, w_ret_o, q_a_norm_g, w_q_b,
           kv_a_norm_g, w_kv_b, w_mla_o, w_out):
    B, S, _ = u.shape
    proj = u @ w_in
    offs = []
    acc = 0
    for w in IN_SPLITS[:-1]:
        acc += w
        offs.append(acc)
    (r_q, r_k, r_v, r_g, c_q, c_kv, k_pe, g_ret, g_mla) = jnp.split(proj, offs, axis=-1)

    rq = _rope(r_q.reshape(B, S, RET_HEADS, RET_QK_DIM), positions).astype(jnp.float32)
    rk = _rope(r_k.reshape(B, S, RET_HEADS, RET_QK_DIM), positions).astype(jnp.float32)
    rk = rk * (RET_QK_DIM ** -0.5)
    rv = r_v.reshape(B, S, RET_HEADS, RET_V_DIM).astype(jnp.float32)
    ry = _retention(rq, rk, rv)
    mu = jnp.mean(ry, axis=-1, keepdims=True)
    var = jnp.mean(jnp.square(ry - mu), axis=-1, keepdims=True)
    ry = ((ry - mu) * lax.rsqrt(var + EPS)).reshape(B, S, RET_V_W) * ret_norm_g.astype(jnp.float32)
    ry = ry.astype(u.dtype) * jax.nn.silu(r_g)
    y_ret = ry @ w_ret_o

    q = (_rmsnorm(c_q, q_a_norm_g) @ w_q_b).reshape(B, S, MLA_HEADS, MLA_QK_DIM)
    q_nope, q_pe = q[..., :QK_NOPE], _rope(q[..., QK_NOPE:], positions)
    kv = (_rmsnorm(c_kv, kv_a_norm_g) @ w_kv_b).reshape(B, S, MLA_HEADS, QK_NOPE + V_HEAD)
    k_nope, v = kv[..., :QK_NOPE], kv[..., QK_NOPE:]
    k_pe = _rope(k_pe.reshape(B, S, 1, QK_ROPE), positions)
    qf = jnp.concatenate([q_nope, q_pe], axis=-1)
    kf = jnp.concatenate([k_nope, jnp.broadcast_to(k_pe, (B, S, MLA_HEADS, QK_ROPE))], axis=-1)
    my = _block_causal_attention(qf, kf, v, MLA_QK_DIM ** -0.5)
    y_mla = my.reshape(B, S, MLA_V_W) @ w_mla_o

    merged = jax.nn.sigmoid(g_ret) * y_ret + jax.nn.sigmoid(g_mla) * y_mla
    return merged @ w_out


def _fwd_setup_inputs(seed: int = 0) -> dict:
    key = jax.random.key(seed)
    ks = jax.random.split(key, 20)
    f32 = jnp.float32

    def nrm(k, shape, fan_in):
        return jax.random.normal(k, shape, f32) * (fan_in ** -0.5)

    def gain(k, shape):
        return 1.0 + 0.02 * jax.random.normal(k, shape, f32)

    x = jax.random.normal(ks[0], (BATCH, SEQ, D_MODEL), f32)
    start = jax.random.randint(ks[1], (BATCH, 1), 0, 4096, dtype=jnp.int32)
    positions = (start + jnp.arange(SEQ, dtype=jnp.int32)[None, :]).astype(jnp.int32)
    return {
        "x": x,
        "positions": positions,
        "norm_mix_g": gain(ks[2], (DEPTH, D_MODEL)),
        "w_in": nrm(ks[3], (DEPTH, D_MODEL, N_IN), D_MODEL),
        "ret_norm_g": gain(ks[4], (DEPTH, RET_V_W)),
        "w_ret_o": nrm(ks[5], (DEPTH, RET_V_W, D_MODEL), RET_V_W),
        "q_a_norm_g": gain(ks[6], (DEPTH, Q_LORA)),
        "w_q_b": nrm(ks[7], (DEPTH, Q_LORA, MLA_HEADS * MLA_QK_DIM), Q_LORA),
        "kv_a_norm_g": gain(ks[8], (DEPTH, KV_LORA)),
        "w_kv_b": nrm(ks[9], (DEPTH, KV_LORA, MLA_HEADS * (QK_NOPE + V_HEAD)), KV_LORA),
        "w_mla_o": nrm(ks[10], (DEPTH, MLA_V_W, D_MODEL), MLA_V_W),
        "w_out": nrm(ks[11], (DEPTH, D_MODEL, D_MODEL), D_MODEL),
        "norm_mlp_g": gain(ks[12], (DEPTH, D_MODEL)),
        "w_up": nrm(ks[13], (DEPTH, D_MODEL, D_FF), D_MODEL),
        "w_down": nrm(ks[14], (DEPTH, D_FF, D_MODEL), D_FF),
        "norm_f_g": gain(ks[15], (D_MODEL,)),
    }


def _fwd_reference(x, positions, norm_mix_g, w_in, ret_norm_g, w_ret_o, q_a_norm_g, w_q_b,
              kv_a_norm_g, w_kv_b, w_mla_o, w_out, norm_mlp_g, w_up, w_down, norm_f_g):
    h = x
    for l in range(DEPTH):
        u = _rmsnorm(h, norm_mix_g[l])
        h = h + _mixer(u, positions, w_in[l], ret_norm_g[l], w_ret_o[l], q_a_norm_g[l],
                       w_q_b[l], kv_a_norm_g[l], w_kv_b[l], w_mla_o[l], w_out[l])
        z = _rmsnorm(h, norm_mlp_g[l]) @ w_up[l]
        h = h + jnp.square(jax.nn.relu(z)) @ w_down[l]
    return _rmsnorm(h, norm_f_g)


import jax as _jax
import jax.numpy as _jnp

TWIN_FORMAT = 'train_step'
FWD_PARAMS = ['x', 'positions', 'norm_mix_g', 'w_in', 'ret_norm_g', 'w_ret_o', 'q_a_norm_g', 'w_q_b', 'kv_a_norm_g', 'w_kv_b', 'w_mla_o', 'w_out', 'norm_mlp_g', 'w_up', 'w_down', 'norm_f_g']
TWIN_WEIGHTS = ['norm_mix_g', 'w_in', 'ret_norm_g', 'w_ret_o', 'q_a_norm_g', 'w_q_b', 'kv_a_norm_g', 'w_kv_b', 'w_mla_o', 'w_out', 'norm_mlp_g', 'w_up', 'w_down', 'norm_f_g']
TWIN_DIFF_INPUT = 'x'
TWIN_INPUTS = ['x', 'positions', 'norm_mix_g', 'w_in', 'ret_norm_g', 'w_ret_o', 'q_a_norm_g', 'w_q_b', 'kv_a_norm_g', 'w_kv_b', 'w_mla_o', 'w_out', 'norm_mlp_g', 'w_up', 'w_down', 'norm_f_g', 'loss_target', 'm_norm_mix_g', 'm_w_in', 'm_ret_norm_g', 'm_w_ret_o', 'm_q_a_norm_g', 'm_w_q_b', 'm_kv_a_norm_g', 'm_w_kv_b', 'm_w_mla_o', 'm_w_out', 'm_norm_mlp_g', 'm_w_up', 'm_w_down', 'm_norm_f_g', 'v_norm_mix_g', 'v_w_in', 'v_ret_norm_g', 'v_w_ret_o', 'v_q_a_norm_g', 'v_w_q_b', 'v_kv_a_norm_g', 'v_w_kv_b', 'v_w_mla_o', 'v_w_out', 'v_norm_mlp_g', 'v_w_up', 'v_w_down', 'v_norm_f_g']
TWIN_OUTPUTS = ['loss', 'grad_x', 'grad_norm_mix_g', 'grad_w_in', 'grad_ret_norm_g', 'grad_w_ret_o', 'grad_q_a_norm_g', 'grad_w_q_b', 'grad_kv_a_norm_g', 'grad_w_kv_b', 'grad_w_mla_o', 'grad_w_out', 'grad_norm_mlp_g', 'grad_w_up', 'grad_w_down', 'grad_norm_f_g', 'delta_norm_mix_g', 'delta_w_in', 'delta_ret_norm_g', 'delta_w_ret_o', 'delta_q_a_norm_g', 'delta_w_q_b', 'delta_kv_a_norm_g', 'delta_w_kv_b', 'delta_w_mla_o', 'delta_w_out', 'delta_norm_mlp_g', 'delta_w_up', 'delta_w_down', 'delta_norm_f_g', 'new_m_norm_mix_g', 'new_m_w_in', 'new_m_ret_norm_g', 'new_m_w_ret_o', 'new_m_q_a_norm_g', 'new_m_w_q_b', 'new_m_kv_a_norm_g', 'new_m_w_kv_b', 'new_m_w_mla_o', 'new_m_w_out', 'new_m_norm_mlp_g', 'new_m_w_up', 'new_m_w_down', 'new_m_norm_f_g', 'new_v_norm_mix_g', 'new_v_w_in', 'new_v_ret_norm_g', 'new_v_w_ret_o', 'new_v_q_a_norm_g', 'new_v_w_q_b', 'new_v_kv_a_norm_g', 'new_v_w_kv_b', 'new_v_w_mla_o', 'new_v_w_out', 'new_v_norm_mlp_g', 'new_v_w_up', 'new_v_w_down', 'new_v_norm_f_g']
TWIN_LEAF_KINDS = {'loss': 'loss', 'grad_x': 'grad_x', 'grad_norm_mix_g': 'grad_w', 'grad_w_in': 'grad_w', 'grad_ret_norm_g': 'grad_w', 'grad_w_ret_o': 'grad_w', 'grad_q_a_norm_g': 'grad_w', 'grad_w_q_b': 'grad_w', 'grad_kv_a_norm_g': 'grad_w', 'grad_w_kv_b': 'grad_w', 'grad_w_mla_o': 'grad_w', 'grad_w_out': 'grad_w', 'grad_norm_mlp_g': 'grad_w', 'grad_w_up': 'grad_w', 'grad_w_down': 'grad_w', 'grad_norm_f_g': 'grad_w', 'delta_norm_mix_g': 'delta_w', 'delta_w_in': 'delta_w', 'delta_ret_norm_g': 'delta_w', 'delta_w_ret_o': 'delta_w', 'delta_q_a_norm_g': 'delta_w', 'delta_w_q_b': 'delta_w', 'delta_kv_a_norm_g': 'delta_w', 'delta_w_kv_b': 'delta_w', 'delta_w_mla_o': 'delta_w', 'delta_w_out': 'delta_w', 'delta_norm_mlp_g': 'delta_w', 'delta_w_up': 'delta_w', 'delta_w_down': 'delta_w', 'delta_norm_f_g': 'delta_w', 'new_m_norm_mix_g': 'new_m', 'new_m_w_in': 'new_m', 'new_m_ret_norm_g': 'new_m', 'new_m_w_ret_o': 'new_m', 'new_m_q_a_norm_g': 'new_m', 'new_m_w_q_b': 'new_m', 'new_m_kv_a_norm_g': 'new_m', 'new_m_w_kv_b': 'new_m', 'new_m_w_mla_o': 'new_m', 'new_m_w_out': 'new_m', 'new_m_norm_mlp_g': 'new_m', 'new_m_w_up': 'new_m', 'new_m_w_down': 'new_m', 'new_m_norm_f_g': 'new_m', 'new_v_norm_mix_g': 'new_v', 'new_v_w_in': 'new_v', 'new_v_ret_norm_g': 'new_v', 'new_v_w_ret_o': 'new_v', 'new_v_q_a_norm_g': 'new_v', 'new_v_w_q_b': 'new_v', 'new_v_kv_a_norm_g': 'new_v', 'new_v_w_kv_b': 'new_v', 'new_v_w_mla_o': 'new_v', 'new_v_w_out': 'new_v', 'new_v_norm_mlp_g': 'new_v', 'new_v_w_up': 'new_v', 'new_v_w_down': 'new_v', 'new_v_norm_f_g': 'new_v'}


def _forward(args):
    return _fwd_reference(*[args[k] for k in FWD_PARAMS])


def _output_shape():
    def fwd():
        inp = _fwd_setup_inputs(0)
        return _fwd_reference(*[inp[k] for k in FWD_PARAMS])
    out = _jax.eval_shape(fwd)
    return out.shape, out.dtype

N_MICROBATCH = 1
ADAM_LR = 0.001
ADAM_B1 = 0.9
ADAM_B2 = 0.999
ADAM_EPS = 1e-08
ADAM_WD = 0.01
ADAM_STEP = 10
PER_EXAMPLE_BATCH_AXIS = {'x': 0, 'positions': 0, 'loss_target': 0}
SHARED_INPUTS = []
_WEIGHT_DTYPES = {'norm_mix_g': _jnp.float32, 'w_in': _jnp.float32, 'ret_norm_g': _jnp.float32, 'w_ret_o': _jnp.float32, 'q_a_norm_g': _jnp.float32, 'w_q_b': _jnp.float32, 'kv_a_norm_g': _jnp.float32, 'w_kv_b': _jnp.float32, 'w_mla_o': _jnp.float32, 'w_out': _jnp.float32, 'norm_mlp_g': _jnp.float32, 'w_up': _jnp.float32, 'w_down': _jnp.float32, 'norm_f_g': _jnp.float32}
MOMENT_SCALE = {'norm_mix_g': 6.327475e-02, 'w_in': 2.663001e-02, 'ret_norm_g': 3.036342e-02, 'w_ret_o': 2.976409e-02, 'q_a_norm_g': 1.440096e-02, 'w_q_b': 5.819209e-03, 'kv_a_norm_g': 1.956597e-02, 'w_kv_b': 6.743288e-03, 'w_mla_o': 7.469276e-03, 'w_out': 3.060693e-02, 'norm_mlp_g': 7.968762e-02, 'w_up': 3.888038e-02, 'w_down': 7.848618e-02, 'norm_f_g': 1.614702e+01}


def _to_microbatches(a, axis):
    t = _jnp.moveaxis(a, axis, 0)
    t = t.reshape((N_MICROBATCH, t.shape[0] // N_MICROBATCH) + t.shape[1:])
    return _jnp.moveaxis(t, 1, axis + 1)


def setup_inputs(seed: int = 0) -> dict:
    inp = _fwd_setup_inputs(seed)
    key = _jax.random.fold_in(_jax.random.key(seed), 7919)
    shape, _ = _output_shape()
    out = dict(inp)
    out["loss_target"] = _jax.random.normal(_jax.random.fold_in(key, 0), shape, _jnp.float32)
    for i, name in enumerate(TWIN_WEIGHTS):
        w = inp[name].astype(_jnp.float32)
        if MOMENT_SCALE is None:
            s = _jnp.sqrt(_jnp.mean(_jnp.square(w)) + 1e-30)
        else:
            s = MOMENT_SCALE[name]
        km, kv = _jax.random.split(_jax.random.fold_in(key, i + 1))
        out[name] = w
        out["m_" + name] = s * _jax.random.normal(km, w.shape, _jnp.float32)
        out["v_" + name] = (s * s) * _jax.random.uniform(kv, w.shape, _jnp.float32, 0.5, 1.5)
    if N_MICROBATCH > 1:
        for name, axis in PER_EXAMPLE_BATCH_AXIS.items():
            out[name] = _to_microbatches(out[name], axis)
    return {'x': out['x'], 'positions': out['positions'], 'norm_mix_g': out['norm_mix_g'], 'w_in': out['w_in'], 'ret_norm_g': out['ret_norm_g'], 'w_ret_o': out['w_ret_o'], 'q_a_norm_g': out['q_a_norm_g'], 'w_q_b': out['w_q_b'], 'kv_a_norm_g': out['kv_a_norm_g'], 'w_kv_b': out['w_kv_b'], 'w_mla_o': out['w_mla_o'], 'w_out': out['w_out'], 'norm_mlp_g': out['norm_mlp_g'], 'w_up': out['w_up'], 'w_down': out['w_down'], 'norm_f_g': out['norm_f_g'], 'loss_target': out['loss_target'], 'm_norm_mix_g': out['m_norm_mix_g'], 'm_w_in': out['m_w_in'], 'm_ret_norm_g': out['m_ret_norm_g'], 'm_w_ret_o': out['m_w_ret_o'], 'm_q_a_norm_g': out['m_q_a_norm_g'], 'm_w_q_b': out['m_w_q_b'], 'm_kv_a_norm_g': out['m_kv_a_norm_g'], 'm_w_kv_b': out['m_w_kv_b'], 'm_w_mla_o': out['m_w_mla_o'], 'm_w_out': out['m_w_out'], 'm_norm_mlp_g': out['m_norm_mlp_g'], 'm_w_up': out['m_w_up'], 'm_w_down': out['m_w_down'], 'm_norm_f_g': out['m_norm_f_g'], 'v_norm_mix_g': out['v_norm_mix_g'], 'v_w_in': out['v_w_in'], 'v_ret_norm_g': out['v_ret_norm_g'], 'v_w_ret_o': out['v_w_ret_o'], 'v_q_a_norm_g': out['v_q_a_norm_g'], 'v_w_q_b': out['v_w_q_b'], 'v_kv_a_norm_g': out['v_kv_a_norm_g'], 'v_w_kv_b': out['v_w_kv_b'], 'v_w_mla_o': out['v_w_mla_o'], 'v_w_out': out['v_w_out'], 'v_norm_mlp_g': out['v_norm_mlp_g'], 'v_w_up': out['v_w_up'], 'v_w_down': out['v_w_down'], 'v_norm_f_g': out['v_norm_f_g']}


def _loss(weights, diff, rest, loss_target):
    with _jax.named_scope("forward"):
        args = {**rest, TWIN_DIFF_INPUT: diff, **{k: w.astype(_WEIGHT_DTYPES[k]) for k, w in weights.items()}}
        y = _forward(args)
    with _jax.named_scope("loss_head"):
        err = _jnp.square(y.astype(_jnp.float32) - loss_target)
        return 0.5 * _jnp.sum(_jnp.mean(err, axis=-1)) if err.ndim else 0.5 * err


def _adamw(w, g, m, v):
    m = ADAM_B1 * m + (1.0 - ADAM_B1) * g
    v = ADAM_B2 * v + (1.0 - ADAM_B2) * _jnp.square(g)
    m_hat = m / (1.0 - ADAM_B1 ** ADAM_STEP)
    v_hat = v / (1.0 - ADAM_B2 ** ADAM_STEP)
    delta = -ADAM_LR * (m_hat / (_jnp.sqrt(v_hat) + ADAM_EPS) + ADAM_WD * w)
    return delta, m, v


def reference(x, positions, norm_mix_g, w_in, ret_norm_g, w_ret_o, q_a_norm_g, w_q_b, kv_a_norm_g, w_kv_b, w_mla_o, w_out, norm_mlp_g, w_up, w_down, norm_f_g, loss_target, m_norm_mix_g, m_w_in, m_ret_norm_g, m_w_ret_o, m_q_a_norm_g, m_w_q_b, m_kv_a_norm_g, m_w_kv_b, m_w_mla_o, m_w_out, m_norm_mlp_g, m_w_up, m_w_down, m_norm_f_g, v_norm_mix_g, v_w_in, v_ret_norm_g, v_w_ret_o, v_q_a_norm_g, v_w_q_b, v_kv_a_norm_g, v_w_kv_b, v_w_mla_o, v_w_out, v_norm_mlp_g, v_w_up, v_w_down, v_norm_f_g):
    given = dict(x=x, positions=positions, norm_mix_g=norm_mix_g, w_in=w_in, ret_norm_g=ret_norm_g, w_ret_o=w_ret_o, q_a_norm_g=q_a_norm_g, w_q_b=w_q_b, kv_a_norm_g=kv_a_norm_g, w_kv_b=w_kv_b, w_mla_o=w_mla_o, w_out=w_out, norm_mlp_g=norm_mlp_g, w_up=w_up, w_down=w_down, norm_f_g=norm_f_g, loss_target=loss_target, m_norm_mix_g=m_norm_mix_g, m_w_in=m_w_in, m_ret_norm_g=m_ret_norm_g, m_w_ret_o=m_w_ret_o, m_q_a_norm_g=m_q_a_norm_g, m_w_q_b=m_w_q_b, m_kv_a_norm_g=m_kv_a_norm_g, m_w_kv_b=m_w_kv_b, m_w_mla_o=m_w_mla_o, m_w_out=m_w_out, m_norm_mlp_g=m_norm_mlp_g, m_w_up=m_w_up, m_w_down=m_w_down, m_norm_f_g=m_norm_f_g, v_norm_mix_g=v_norm_mix_g, v_w_in=v_w_in, v_ret_norm_g=v_ret_norm_g, v_w_ret_o=v_w_ret_o, v_q_a_norm_g=v_q_a_norm_g, v_w_q_b=v_w_q_b, v_kv_a_norm_g=v_kv_a_norm_g, v_w_kv_b=v_w_kv_b, v_w_mla_o=v_w_mla_o, v_w_out=v_w_out, v_norm_mlp_g=v_norm_mlp_g, v_w_up=v_w_up, v_w_down=v_w_down, v_norm_f_g=v_norm_f_g)
    weights = {n: given[n] for n in TWIN_WEIGHTS}
    shared = {n: given[n] for n in SHARED_INPUTS}
    per_example = {n: given[n] for n in ['x', 'positions']}
    grad_fn = _jax.value_and_grad(_loss, argnums=(0, 1))

    def one_microbatch(ex, loss_target):
        ex = dict(ex)
        diff = ex.pop(TWIN_DIFF_INPUT)
        return grad_fn(weights, diff, {**shared, **ex}, loss_target)

    if N_MICROBATCH == 1:
        loss, (grad_w, grad_x) = one_microbatch(per_example, given["loss_target"])
    else:
        def body(carry, xs):
            loss_sum, grad_sum = carry
            l_k, (gw_k, gx_k) = one_microbatch(xs[0], xs[1])
            with _jax.named_scope("update"):
                return (loss_sum + l_k, _jax.tree.map(_jnp.add, grad_sum, gw_k)), gx_k

        init = (_jnp.zeros((), _jnp.float32), _jax.tree.map(_jnp.zeros_like, weights))
        (loss, grad_w), grad_x = _jax.lax.scan(body, init, (per_example, given["loss_target"]))
    with _jax.named_scope("update"):
        delta_w, new_m, new_v = {}, {}, {}
        for n in TWIN_WEIGHTS:
            delta_w[n], new_m[n], new_v[n] = _adamw(weights[n], grad_w[n], given["m_" + n], given["v_" + n])
    return (loss, grad_x, *[grad_w[n] for n in TWIN_WEIGHTS], *[delta_w[n] for n in TWIN_WEIGHTS],
            *[new_m[n] for n in TWIN_WEIGHTS], *[new_v[n] for n in TWIN_WEIGHTS])
```

```python
import functools
import math

import jax
import jax.numpy as jnp
from jax import lax
from jax.experimental import pallas as pl
from jax.experimental.pallas import tpu as pltpu

F32 = jnp.float32
BF16 = jnp.bfloat16

EPS = 1e-6
ROPE_THETA = 10000.0
CHUNK = 64
RET_QK = 128
RET_V = 256
QK_NOPE = 128
QK_ROPE = 64
V_HEAD = 128
LANES = 128

ADAM_LR = 0.001
ADAM_B1 = 0.9
ADAM_B2 = 0.999
ADAM_EPS = 1e-08
ADAM_WD = 0.01
ADAM_STEP = 10

N_CHIPS = 4
VMEM_LIMIT = 56 * 1024 * 1024
MESH = pl.DeviceIdType.MESH
NEG = -1e30


def _pallas(body, **kw):
    return pl.pallas_call(body, **kw)


def _params(sem=None):
    return pltpu.CompilerParams(dimension_semantics=sem, vmem_limit_bytes=VMEM_LIMIT)


def _tile(n, want):
    t = min(n, want)
    while n % t:
        t //= 2
    return t


def _mm(a, b, *, mode, outs, name, epi=None, extras=(), tm=1024, tn=1024, tk=2048):
    if mode == "nn":
        (M, K), (_, N) = a.shape, b.shape
    elif mode == "nt":
        (M, K), (N, _) = a.shape, b.shape
    else:
        (K, M), (_, N) = a.shape, b.shape
    tm, tn, tk = _tile(M, tm), _tile(N, tn), _tile(K, tk)
    nk = K // tk
    if mode == "nn":
        a_spec = pl.BlockSpec((tm, tk), lambda i, j, k: (i, k))
        b_spec = pl.BlockSpec((tk, tn), lambda i, j, k: (k, j))
        dims = (((1,), (0,)), ((), ()))
    elif mode == "nt":
        a_spec = pl.BlockSpec((tm, tk), lambda i, j, k: (i, k))
        b_spec = pl.BlockSpec((tn, tk), lambda i, j, k: (j, k))
        dims = (((1,), (1,)), ((), ()))
    else:
        a_spec = pl.BlockSpec((tk, tm), lambda i, j, k: (k, i))
        b_spec = pl.BlockSpec((tk, tn), lambda i, j, k: (k, j))
        dims = (((0,), (0,)), ((), ()))
    o_spec = pl.BlockSpec((tm, tn), lambda i, j, k: (i, j))
    n_ex, n_out = len(extras), len(outs)
    if epi is None:
        epi = lambda acc: (acc,)

    def body(*refs):
        a_ref, b_ref = refs[0], refs[1]
        ex_refs = refs[2:2 + n_ex]
        o_refs = refs[2 + n_ex:2 + n_ex + n_out]
        part = lax.dot_general(a_ref[...].astype(BF16), b_ref[...].astype(BF16), dims,
                               preferred_element_type=F32)

        def finish(acc):
            vals = epi(acc, *[r[...] for r in ex_refs])
            for r, v in zip(o_refs, vals):
                r[...] = v.astype(r.dtype)

        if nk == 1:
            finish(part)
        else:
            acc_ref = refs[-1]
            k = pl.program_id(2)

            @pl.when(k == 0)
            def _():
                acc_ref[...] = part

            @pl.when(k > 0)
            def _():
                acc_ref[...] += part

            @pl.when(k == nk - 1)
            def _():
                finish(acc_ref[...])

    res = _pallas(
        body, name=name, grid=(M // tm, N // tn, nk),
        in_specs=[a_spec, b_spec] + [o_spec] * n_ex,
        out_specs=[o_spec] * n_out,
        out_shape=[jax.ShapeDtypeStruct((M, N), d) for d in outs],
        scratch_shapes=[pltpu.VMEM((tm, tn), F32)] if nk > 1 else [],
        compiler_params=_params(("parallel", "parallel", "arbitrary")),
    )(a, b, *extras)
    return res[0] if n_out == 1 else res


def _rmsnorm_fwd(x, g, *, name, width=None, col=0, tr=256):
    S = x.shape[0]
    W = x.shape[1] if width is None else width
    tr = _tile(S, tr)

    def body(x_ref, g_ref, y_ref, r_ref):
        xv = x_ref[...]
        rstd = lax.rsqrt(jnp.mean(xv * xv, axis=-1, keepdims=True) + EPS)
        y_ref[...] = (xv * rstd * g_ref[...]).astype(BF16)
        r_ref[...] = rstd

    return _pallas(
        body, name=name, grid=(S // tr,),
        in_specs=[pl.BlockSpec((tr, W), lambda i: (i, col)), pl.BlockSpec((1, W), lambda i: (0, 0))],
        out_specs=[pl.BlockSpec((tr, W), lambda i: (i, 0)), pl.BlockSpec((tr, 1), lambda i: (i, 0))],
        out_shape=[jax.ShapeDtypeStruct((S, W), BF16), jax.ShapeDtypeStruct((S, 1), F32)],
        compiler_params=_params(("parallel",)),
    )(x, g)


def _rmsnorm_bwd(dy, x, rstd, g, *, name, res=None, out_dtype=F32, width=None, col=0, tr=256):
    S = x.shape[0]
    W = x.shape[1] if width is None else width
    tr = _tile(S, tr)
    has_res = res is not None

    def body(*refs):
        dy_ref, x_ref, r_ref, g_ref = refs[:4]
        dx_ref, dg_ref = refs[-2], refs[-1]
        rstd_v = r_ref[...]
        xhat = x_ref[...] * rstd_v
        dyv = dy_ref[...].astype(F32)
        dyg = dyv * g_ref[...]
        dx = rstd_v * (dyg - xhat * jnp.mean(dyg * xhat, axis=-1, keepdims=True))
        if has_res:
            dx = dx + refs[4][...]
        dx_ref[...] = dx.astype(dx_ref.dtype)
        part = jnp.sum(dyv * xhat, axis=0, keepdims=True)

        @pl.when(pl.program_id(0) == 0)
        def _():
            dg_ref[...] = part

        @pl.when(pl.program_id(0) > 0)
        def _():
            dg_ref[...] += part

    row = pl.BlockSpec((tr, W), lambda i: (i, 0))
    ins = [dy, x, rstd, g] + ([res] if has_res else [])
    in_specs = [row, pl.BlockSpec((tr, W), lambda i: (i, col)), pl.BlockSpec((tr, 1), lambda i: (i, 0)),
                pl.BlockSpec((1, W), lambda i: (0, 0))] + ([row] if has_res else [])
    return _pallas(
        body, name=name, grid=(S // tr,), in_specs=in_specs,
        out_specs=[row, pl.BlockSpec((1, W), lambda i: (0, 0))],
        out_shape=[jax.ShapeDtypeStruct((S, W), out_dtype), jax.ShapeDtypeStruct((1, W), F32)],
        compiler_params=_params(("arbitrary",)),
    )(*ins)


def _final_loss(h2, g, target, *, tr=256):
    S, D = h2.shape
    tr = _tile(S, tr)

    def body(h_ref, g_ref, t_ref, loss_ref, dh_ref, dg_ref):
        hv = h_ref[...]
        rstd = lax.rsqrt(jnp.mean(hv * hv, axis=-1, keepdims=True) + EPS)
        xhat = hv * rstd
        e = xhat * g_ref[...] - t_ref[...]
        lpart = (0.5 / D) * jnp.sum(jnp.sum(e * e, axis=-1, keepdims=True), axis=0, keepdims=True)
        dy = e * (1.0 / D)
        dyg = dy * g_ref[...]
        dh_ref[...] = rstd * (dyg - xhat * jnp.mean(dyg * xhat, axis=-1, keepdims=True))
        gpart = jnp.sum(dy * xhat, axis=0, keepdims=True)

        @pl.when(pl.program_id(0) == 0)
        def _():
            loss_ref[...] = lpart
            dg_ref[...] = gpart

        @pl.when(pl.program_id(0) > 0)
        def _():
            loss_ref[...] += lpart
            dg_ref[...] += gpart

    row = pl.BlockSpec((tr, D), lambda i: (i, 0))
    vec = pl.BlockSpec((1, D), lambda i: (0, 0))
    return _pallas(
        body, name="final_loss", grid=(S // tr,), in_specs=[row, vec, row],
        out_specs=[pl.BlockSpec((1, 1), lambda i: (0, 0)), row, vec],
        out_shape=[jax.ShapeDtypeStruct((1, 1), F32), jax.ShapeDtypeStruct((S, D), F32),
                   jax.ShapeDtypeStruct((1, D), F32)],
        compiler_params=_params(("arbitrary",)),
    )(h2, g, target)


def _sigmoid(v):
    return 1.0 / (1.0 + jnp.exp(-v))


def _merge_fwd(proj, y_ret, y_mla, D, off_gret, off_gmla, *, tr=256, tc=1024):
    S = y_ret.shape[0]
    tr, tc = _tile(S, tr), _tile(D, tc)
    b_ret, b_mla = off_gret // tc, off_gmla // tc

    def body(gr_ref, gm_ref, yr_ref, ym_ref, o_ref):
        o_ref[...] = (_sigmoid(gr_ref[...]) * yr_ref[...] + _sigmoid(gm_ref[...]) * ym_ref[...]).astype(BF16)

    blk = pl.BlockSpec((tr, tc), lambda i, j: (i, j))
    return _pallas(
        body, name="merge_fwd", grid=(S // tr, D // tc),
        in_specs=[pl.BlockSpec((tr, tc), lambda i, j: (i, b_ret + j)),
                  pl.BlockSpec((tr, tc), lambda i, j: (i, b_mla + j)), blk, blk],
        out_specs=blk, out_shape=jax.ShapeDtypeStruct((S, D), BF16),
        compiler_params=_params(("parallel", "parallel")),
    )(proj, proj, y_ret, y_mla)


def _merge_bwd(dmerged, proj, y_ret, y_mla, D, off_gret, off_gmla, *, tr=256, tc=1024):
    S = y_ret.shape[0]
    tr, tc = _tile(S, tr), _tile(D, tc)
    b_ret, b_mla = off_gret // tc, off_gmla // tc

    def body(dm_ref, gr_ref, gm_ref, yr_ref, ym_ref, dyr_ref, dym_ref, dgr_ref, dgm_ref):
        dm = dm_ref[...]
        sr, sm = _sigmoid(gr_ref[...]), _sigmoid(gm_ref[...])
        dyr_ref[...] = (dm * sr).astype(BF16)
        dym_ref[...] = (dm * sm).astype(BF16)
        dgr_ref[...] = (dm * yr_ref[...] * sr * (1.0 - sr)).astype(BF16)
        dgm_ref[...] = (dm * ym_ref[...] * sm * (1.0 - sm)).astype(BF16)

    blk = pl.BlockSpec((tr, tc), lambda i, j: (i, j))
    return _pallas(
        body, name="merge_bwd", grid=(S // tr, D // tc),
        in_specs=[blk, pl.BlockSpec((tr, tc), lambda i, j: (i, b_ret + j)),
                  pl.BlockSpec((tr, tc), lambda i, j: (i, b_mla + j)), blk, blk],
        out_specs=[blk] * 4, out_shape=[jax.ShapeDtypeStruct((S, D), BF16)] * 4,
        compiler_params=_params(("parallel", "parallel")),
    )(dmerged, proj, proj, y_ret, y_mla)


def _rope128(t, cos_full, sin_signed):
    return t * cos_full + pltpu.roll(t, RET_QK // 2, 1) * sin_signed


def _rope128_t(d, cos_full, sin_signed):
    return d * cos_full + pltpu.roll(d * sin_signed, RET_QK // 2, 1)


def _ret_consts(lg, T):
    pos = lax.broadcasted_iota(jnp.int32, (T, 1), 0).astype(F32)
    qd = jnp.exp(lg * (pos + 1.0))
    kd = jnp.exp(lg * (T - 1.0 - pos))
    n = lax.broadcasted_iota(jnp.int32, (T, T), 0)
    m = lax.broadcasted_iota(jnp.int32, (T, T), 1)
    vis = (m // CHUNK) <= (n // CHUNK)
    dist = jnp.abs(n - m).astype(F32)
    decay = jnp.where(vis, jnp.exp(lg * dist), 0.0)
    cdec = jnp.exp(lg * float(T))
    return qd, kd, decay, cdec


def _dot(a, b, dims):
    return lax.dot_general(a.astype(BF16), b.astype(BF16), (dims, ((), ())), preferred_element_type=F32)


NN = ((1,), (0,))
NT = ((1,), (1,))
TN = ((0,), (0,))


def _ret_fwd(proj, cosr, sinr, lgam, gain, RH, *, T):
    S = proj.shape[0]
    nb = S // T
    kq, kv = RH, RH
    scale = RET_QK ** -0.5

    def body(q_ref, k_ref, v_ref, g_ref, cos_ref, sin_ref, lg_ref, gain_ref, ry_ref, gated_ref, st_ref, state):
        b = pl.program_id(1)

        @pl.when(b == 0)
        def _():
            state[...] = jnp.zeros_like(state)

        lg = lg_ref[0:1, 0:1]
        qd, kd, decay, cdec = _ret_consts(lg, T)
        cosv, sinv = cos_ref[...], sin_ref[...]
        q = _rope128(q_ref[...], cosv, sinv)
        k = _rope128(k_ref[...], cosv, sinv) * scale
        v = v_ref[...]
        sprev = state[...]
        st_ref[...] = sprev
        a = _dot(q, k, NT) * decay
        o = _dot(a, v, NN) + _dot(q * qd, sprev, NN)
        state[...] = sprev * cdec + _dot(k * kd, v, TN)
        ry_ref[...] = o
        mu = jnp.mean(o, axis=-1, keepdims=True)
        oc = o - mu
        var = jnp.mean(oc * oc, axis=-1, keepdims=True)
        t = oc * lax.rsqrt(var + EPS) * gain_ref[...]
        gv = g_ref[...]
        gated_ref[...] = (t * (gv * _sigmoid(gv))).astype(BF16)

    return _pallas(
        body, name="ret_fwd", grid=(RH, nb),
        in_specs=[pl.BlockSpec((T, RET_QK), lambda h, b: (b, h)),
                  pl.BlockSpec((T, RET_QK), lambda h, b: (b, kq + h)),
                  pl.BlockSpec((T, RET_V), lambda h, b: (b, kv + h)),
                  pl.BlockSpec((T, RET_V), lambda h, b: (b, 2 * RH + h)),
                  pl.BlockSpec((T, RET_QK), lambda h, b: (b, 0)),
                  pl.BlockSpec((T, RET_QK), lambda h, b: (b, 0)),
                  pl.BlockSpec((None, 8, LANES), lambda h, b: (h, 0, 0)),
                  pl.BlockSpec((1, RET_V), lambda h, b: (0, h))],
        out_specs=[pl.BlockSpec((T, RET_V), lambda h, b: (b, h)),
                   pl.BlockSpec((T, RET_V), lambda h, b: (b, h)),
                   pl.BlockSpec((None, None, RET_QK, RET_V), lambda h, b: (h, b, 0, 0))],
        out_shape=[jax.ShapeDtypeStruct((S, RH * RET_V), F32), jax.ShapeDtypeStruct((S, RH * RET_V), BF16),
                   jax.ShapeDtypeStruct((RH, nb, RET_QK, RET_V), F32)],
        scratch_shapes=[pltpu.VMEM((RET_QK, RET_V), F32)],
        compiler_params=_params(("parallel", "arbitrary")),
    )(proj, proj, proj, proj, cosr, sinr, lgam, gain)


def _ret_bwd(proj, cosr, sinr, lgam, gain, ry, dgated, states, RH, *, T):
    S = proj.shape[0]
    nb = S // T
    scale = RET_QK ** -0.5

    def body(q_ref, k_ref, v_ref, g_ref, cos_ref, sin_ref, lg_ref, gain_ref, ry_ref, dg_ref, st_ref,
             dq_ref, dk_ref, dv_ref, drg_ref, dgain_ref, dstate):
        b = pl.program_id(1)

        @pl.when(b == 0)
        def _():
            dstate[...] = jnp.zeros_like(dstate)

        lg = lg_ref[0:1, 0:1]
        qd, kd, decay, cdec = _ret_consts(lg, T)
        cosv, sinv = cos_ref[...], sin_ref[...]
        q = _rope128(q_ref[...], cosv, sinv)
        k = _rope128(k_ref[...], cosv, sinv) * scale
        v = v_ref[...]
        sprev = st_ref[...]
        ds_new = dstate[...]
        o = ry_ref[...]
        mu = jnp.mean(o, axis=-1, keepdims=True)
        oc = o - mu
        rstd = lax.rsqrt(jnp.mean(oc * oc, axis=-1, keepdims=True) + EPS)
        ryn = oc * rstd
        gainv = gain_ref[...]
        gv = g_ref[...]
        sg = _sigmoid(gv)
        dgt = dg_ref[...]
        dt = dgt * (gv * sg)
        drg_ref[...] = (dgt * (ryn * gainv) * (sg * (1.0 + gv * (1.0 - sg)))).astype(BF16)
        gpart = jnp.sum(dt * ryn, axis=0, keepdims=True)

        @pl.when(b == 0)
        def _():
            dgain_ref[...] = gpart

        @pl.when(b > 0)
        def _():
            dgain_ref[...] += gpart

        dryn = dt * gainv
        do = rstd * (dryn - jnp.mean(dryn, axis=-1, keepdims=True)
                     - ryn * jnp.mean(dryn * ryn, axis=-1, keepdims=True))
        a = _dot(q, k, NT) * decay
        kdk = k * kd
        qdq = q * qd
        dv_ref[...] = (_dot(a, do, TN) + _dot(kdk, ds_new, NN)).astype(BF16)
        dp = _dot(do, v, NT) * decay
        dq = _dot(dp, k, NN) + _dot(do, sprev, NT) * qd
        dk = (_dot(dp, q, TN) + _dot(v, ds_new, NT) * kd) * scale
        dstate[...] = ds_new * cdec + _dot(qdq, do, TN)
        dq_ref[...] = _rope128_t(dq, cosv, sinv).astype(BF16)
        dk_ref[...] = _rope128_t(dk, cosv, sinv).astype(BF16)

    rb = lambda b: nb - 1 - b
    return _pallas(
        body, name="ret_bwd", grid=(RH, nb),
        in_specs=[pl.BlockSpec((T, RET_QK), lambda h, b: (rb(b), h)),
                  pl.BlockSpec((T, RET_QK), lambda h, b: (rb(b), RH + h)),
                  pl.BlockSpec((T, RET_V), lambda h, b: (rb(b), RH + h)),
                  pl.BlockSpec((T, RET_V), lambda h, b: (rb(b), 2 * RH + h)),
                  pl.BlockSpec((T, RET_QK), lambda h, b: (rb(b), 0)),
                  pl.BlockSpec((T, RET_QK), lambda h, b: (rb(b), 0)),
                  pl.BlockSpec((None, 8, LANES), lambda h, b: (h, 0, 0)),
                  pl.BlockSpec((1, RET_V), lambda h, b: (0, h)),
                  pl.BlockSpec((T, RET_V), lambda h, b: (rb(b), h)),
                  pl.BlockSpec((T, RET_V), lambda h, b: (rb(b), h)),
                  pl.BlockSpec((None, None, RET_QK, RET_V), lambda h, b: (h, rb(b), 0, 0))],
        out_specs=[pl.BlockSpec((T, RET_QK), lambda h, b: (rb(b), h)),
                   pl.BlockSpec((T, RET_QK), lambda h, b: (rb(b), h)),
                   pl.BlockSpec((T, RET_V), lambda h, b: (rb(b), h)),
                   pl.BlockSpec((T, RET_V), lambda h, b: (rb(b), h)),
                   pl.BlockSpec((1, RET_V), lambda h, b: (0, h))],
        out_shape=[jax.ShapeDtypeStruct((S, RH * RET_QK), BF16), jax.ShapeDtypeStruct((S, RH * RET_QK), BF16),
                   jax.ShapeDtypeStruct((S, RH * RET_V), BF16), jax.ShapeDtypeStruct((S, RH * RET_V), BF16),
                   jax.ShapeDtypeStruct((1, RH * RET_V), F32)],
        scratch_shapes=[pltpu.VMEM((RET_QK, RET_V), F32)],
        compiler_params=_params(("parallel", "arbitrary")),
    )(proj, proj, proj, proj, cosr, sinr, lgam, gain, ry, dgated, states)


def _rope_pe(t, c, s1, s2):
    return t * c + pltpu.roll(t, LANES - QK_ROPE // 2, 1) * s1 + pltpu.roll(t, QK_ROPE // 2, 1) * s2


def _rope_pe_t(d, c, s1, s2):
    return d * c + pltpu.roll(d * s1, QK_ROPE // 2, 1) + pltpu.roll(d * s2, LANES - QK_ROPE // 2, 1)


def _attn_prep(q_all, kv_all, kpe, tabs, MH, *, tr=512):
    S = q_all.shape[0]
    tr = _tile(S, tr)
    c_t, s1_t, s2_t = tabs

    def body(qn_ref, qp_ref, kn_ref, v_ref, kpe_ref, c_ref, s1_ref, s2_ref, qf_ref, kf_ref, vb_ref):
        c, s1, s2 = c_ref[...], s1_ref[...], s2_ref[...]
        qf_ref[:, :QK_NOPE] = qn_ref[...].astype(BF16)
        qf_ref[:, QK_NOPE:] = _rope_pe(qp_ref[...], c, s1, s2).astype(BF16)
        kf_ref[:, :QK_NOPE] = kn_ref[...].astype(BF16)
        kf_ref[:, QK_NOPE:] = _rope_pe(kpe_ref[...], c, s1, s2).astype(BF16)
        vb_ref[...] = v_ref[...].astype(BF16)

    tab = pl.BlockSpec((tr, LANES), lambda i, h: (i, 0))
    return _pallas(
        body, name="attn_prep", grid=(S // tr, MH),
        in_specs=[pl.BlockSpec((tr, LANES), lambda i, h: (i, h)),
                  pl.BlockSpec((tr, LANES), lambda i, h: (i, MH + h)),
                  pl.BlockSpec((tr, LANES), lambda i, h: (i, h)),
                  pl.BlockSpec((tr, LANES), lambda i, h: (i, MH + h)),
                  tab, tab, tab, tab],
        out_specs=[pl.BlockSpec((tr, 2 * LANES), lambda i, h: (i, h)),
                   pl.BlockSpec((tr, 2 * LANES), lambda i, h: (i, h)),
                   pl.BlockSpec((tr, LANES), lambda i, h: (i, h))],
        out_shape=[jax.ShapeDtypeStruct((S, MH * 2 * LANES), BF16), jax.ShapeDtypeStruct((S, MH * 2 * LANES), BF16),
                   jax.ShapeDtypeStruct((S, MH * LANES), BF16)],
        compiler_params=_params(("parallel", "parallel")),
    )(q_all, q_all, kv_all, kv_all, kpe, c_t, s1_t, s2_t)


def _chunk_mask(T):
    n = lax.broadcasted_iota(jnp.int32, (T, T), 0)
    m = lax.broadcasted_iota(jnp.int32, (T, T), 1)
    return (m // CHUNK) <= (n // CHUNK)


def _attn_fwd(qf, kf, vb, MH, *, T):
    S = qf.shape[0]
    nt = S // T
    scale = (QK_NOPE + QK_ROPE) ** -0.5

    def body(q_ref, k_ref, v_ref, o_ref, lse_ref, m_sc, l_sc, acc_sc):
        qi, ki = pl.program_id(1), pl.program_id(2)

        @pl.when(ki == 0)
        def _():
            m_sc[...] = jnp.full_like(m_sc, NEG)
            l_sc[...] = jnp.zeros_like(l_sc)
            acc_sc[...] = jnp.zeros_like(acc_sc)

        def step(masked):
            s = _dot(q_ref[...], k_ref[...], NT) * scale
            if masked:
                s = jnp.where(_chunk_mask(T), s, NEG)
            m_prev = m_sc[...]
            m_new = jnp.maximum(m_prev, jnp.max(s, axis=-1, keepdims=True))
            alpha = jnp.exp(m_prev - m_new)
            p = jnp.exp(s - m_new)
            l_sc[...] = alpha * l_sc[...] + jnp.sum(p, axis=-1, keepdims=True)
            acc_sc[...] = alpha * acc_sc[...] + _dot(p, v_ref[...], NN)
            m_sc[...] = m_new

        @pl.when(ki < qi)
        def _():
            step(False)

        @pl.when(ki == qi)
        def _():
            step(True)

        @pl.when(ki == nt - 1)
        def _():
            o_ref[...] = acc_sc[...] / l_sc[...]
            lse_ref[...] = m_sc[...] + jnp.log(l_sc[...])

    return _pallas(
        body, name="attn_fwd", grid=(MH, nt, nt),
        in_specs=[pl.BlockSpec((T, 2 * LANES), lambda h, i, j: (i, h)),
                  pl.BlockSpec((T, 2 * LANES), lambda h, i, j: (jnp.minimum(i, j), h)),
                  pl.BlockSpec((T, LANES), lambda h, i, j: (jnp.minimum(i, j), h))],
        out_specs=[pl.BlockSpec((T, LANES), lambda h, i, j: (i, h)),
                   pl.BlockSpec((None, T, 1), lambda h, i, j: (h, i, 0))],
        out_shape=[jax.ShapeDtypeStruct((S, MH * LANES), F32), jax.ShapeDtypeStruct((MH, S, 1), F32)],
        scratch_shapes=[pltpu.VMEM((T, 1), F32), pltpu.VMEM((T, 1), F32), pltpu.VMEM((T, LANES), F32)],
        compiler_params=_params(("parallel", "parallel", "arbitrary")),
    )(qf, kf, vb)


def _attn_bwd(qf, kf, vb, o, lse, do, MH, *, T):
    S = qf.shape[0]
    nt = S // T
    scale = (QK_NOPE + QK_ROPE) ** -0.5

    def body(q_ref, k_ref, v_ref, o_ref, lse_ref, do_ref, dq_ref, dk_ref, dv_ref, dk_sc, dv_sc):
        kj, qi = pl.program_id(1), pl.program_id(2)

        @pl.when(jnp.logical_and(kj == 0, qi == 0))
        def _():
            dq_ref[...] = jnp.zeros_like(dq_ref)

        @pl.when(qi == 0)
        def _():
            dk_sc[...] = jnp.zeros_like(dk_sc)
            dv_sc[...] = jnp.zeros_like(dv_sc)

        def step(masked):
            q, k, v = q_ref[...], k_ref[...], v_ref[...]
            dov = do_ref[...]
            delta = jnp.sum(dov * o_ref[...], axis=-1, keepdims=True)
            s = _dot(q, k, NT) * scale
            if masked:
                s = jnp.where(_chunk_mask(T), s, NEG)
            p = jnp.exp(s - lse_ref[...])
            dp = _dot(dov, v, NT)
            ds = p * (dp - delta) * scale
            dv_sc[...] += _dot(p, dov, TN)
            dk_sc[...] += _dot(ds, q, TN)
            rows = pl.ds(pl.multiple_of(qi * T, T), T)
            dq_ref[rows, :] += _dot(ds, k, NN)

        @pl.when(qi > kj)
        def _():
            step(False)

        @pl.when(qi == kj)
        def _():
            step(True)

        @pl.when(qi == nt - 1)
        def _():
            dk_ref[...] = dk_sc[...]
            dv_ref[...] = dv_sc[...].astype(BF16)

    qmap = lambda h, j, i: (jnp.maximum(i, j), h)
    return _pallas(
        body, name="attn_bwd", grid=(MH, nt, nt),
        in_specs=[pl.BlockSpec((T, 2 * LANES), qmap),
                  pl.BlockSpec((T, 2 * LANES), lambda h, j, i: (j, h)),
                  pl.BlockSpec((T, LANES), lambda h, j, i: (j, h)),
                  pl.BlockSpec((T, LANES), qmap),
                  pl.BlockSpec((None, T, 1), lambda h, j, i: (h, jnp.maximum(i, j), 0)),
                  pl.BlockSpec((T, LANES), qmap)],
        out_specs=[pl.BlockSpec((S, 2 * LANES), lambda h, j, i: (0, h)),
                   pl.BlockSpec((T, 2 * LANES), lambda h, j, i: (j, h)),
                   pl.BlockSpec((T, LANES), lambda h, j, i: (j, h))],
        out_shape=[jax.ShapeDtypeStruct((S, MH * 2 * LANES), F32), jax.ShapeDtypeStruct((S, MH * 2 * LANES), F32),
                   jax.ShapeDtypeStruct((S, MH * LANES), BF16)],
        scratch_shapes=[pltpu.VMEM((T, 2 * LANES), F32), pltpu.VMEM((T, LANES), F32)],
        compiler_params=_params(("parallel", "arbitrary", "arbitrary")),
    )(qf, kf, vb, o, lse, do)


def _attn_post(dqf, dkf, dvb, tabs, MH, *, tr=512):
    S = dqf.shape[0]
    tr = _tile(S, tr)
    c_t, s1_t, s2_t = tabs

    def body(dq_ref, dk_ref, dv_ref, c_ref, s1_ref, s2_ref, dqn_ref, dqp_ref, dkn_ref, dvo_ref, dkpe_ref, acc):
        h = pl.program_id(1)
        c, s1, s2 = c_ref[...], s1_ref[...], s2_ref[...]
        dqn_ref[...] = dq_ref[:, :QK_NOPE].astype(BF16)
        dqp_ref[...] = _rope_pe_t(dq_ref[:, QK_NOPE:], c, s1, s2).astype(BF16)
        dkn_ref[...] = dk_ref[:, :QK_NOPE].astype(BF16)
        dvo_ref[...] = dv_ref[...]

        @pl.when(h == 0)
        def _():
            acc[...] = dk_ref[:, QK_NOPE:]

        @pl.when(h > 0)
        def _():
            acc[...] += dk_ref[:, QK_NOPE:]

        @pl.when(h == MH - 1)
        def _():
            dkpe_ref[...] = _rope_pe_t(acc[...], c, s1, s2).astype(BF16)

    tab = pl.BlockSpec((tr, LANES), lambda i, h: (i, 0))
    return _pallas(
        body, name="attn_post", grid=(S // tr, MH),
        in_specs=[pl.BlockSpec((tr, 2 * LANES), lambda i, h: (i, h)),
                  pl.BlockSpec((tr, 2 * LANES), lambda i, h: (i, h)),
                  pl.BlockSpec((tr, LANES), lambda i, h: (i, h)), tab, tab, tab],
        out_specs=[pl.BlockSpec((tr, LANES), lambda i, h: (i, h)),
                   pl.BlockSpec((tr, LANES), lambda i, h: (i, h)),
                   pl.BlockSpec((tr, LANES), lambda i, h: (i, h)),
                   pl.BlockSpec((tr, LANES), lambda i, h: (i, h)), tab],
        out_shape=[jax.ShapeDtypeStruct((S, MH * LANES), BF16)] * 4 + [jax.ShapeDtypeStruct((S, LANES), BF16)],
        scratch_shapes=[pltpu.VMEM((tr, LANES), F32)],
        compiler_params=_params(("parallel", "arbitrary")),
    )(dqf, dkf, dvb, c_t, s1_t, s2_t)


def _rows_call(fn, ins, out_dtypes, *, name, block_bytes=2 << 20):
    R, C = ins[0].shape
    tr = 8
    while tr * 2 * C * 4 <= block_bytes:
        tr *= 2
    tr = _tile(R, tr)
    n_in = len(ins)

    def body(*refs):
        vals = fn(*[r[...] for r in refs[:n_in]])
        for r, v in zip(refs[n_in:], vals):
            r[...] = v.astype(r.dtype)

    blk = pl.BlockSpec((tr, C), lambda i: (i, 0))
    res = _pallas(
        body, name=name, grid=(R // tr,), in_specs=[blk] * n_in, out_specs=[blk] * len(out_dtypes),
        out_shape=[jax.ShapeDtypeStruct((R, C), d) for d in out_dtypes],
        compiler_params=_params(("parallel",)),
    )(*ins)
    return res


def _adamw_vals(w, g, m, v):
    m = ADAM_B1 * m + (1.0 - ADAM_B1) * g
    v = ADAM_B2 * v + (1.0 - ADAM_B2) * (g * g)
    m_hat = m / (1.0 - ADAM_B1 ** ADAM_STEP)
    v_hat = v / (1.0 - ADAM_B2 ** ADAM_STEP)
    delta = -ADAM_LR * (m_hat / (jnp.sqrt(v_hat) + ADAM_EPS) + ADAM_WD * w)
    return delta, m, v


def _me():
    return lax.axis_index("x"), lax.axis_index("y"), lax.axis_index("c")


def _other_chips(x, y):
    return [(1 - x, y), (x, 1 - y), (1 - x, 1 - y)]


def _rcopy(src, dst, ssem, rsem, dev):
    return pltpu.make_async_remote_copy(src_ref=src, dst_ref=dst, send_sem=ssem, recv_sem=rsem,
                                        device_id=dev, device_id_type=MESH)


_ANY = pl.BlockSpec(memory_space=pl.ANY)


def _gather_weights(shards):
    n = len(shards)

    def body(*refs):
        ins, outs = refs[:n], refs[n:2 * n]
        lsem, ssem, rsem, fssem, frsem = refs[2 * n:]
        x, y, c = _me()
        j = 2 * x + y
        sib = (x, y, 1 - c)
        chips = _other_chips(x, y)
        local, remote = [], []
        for i in range(n):
            half = ins[i].shape[0] // 2
            rows = pl.ds(c * half, half)
            cp = pltpu.make_async_copy(ins[i], outs[i].at[j], lsem.at[i])
            cp.start()
            local.append(cp)
            for k, (px, py) in enumerate(chips):
                r = _rcopy(ins[i].at[rows], outs[i].at[j, rows], ssem.at[i, k], rsem.at[i, k], (px, py, c))
                r.start()
                remote.append(r)
        for i in range(n):
            half = ins[i].shape[0] // 2
            rows = pl.ds(c * half, half)
            for k, (px, py) in enumerate(chips):
                jp = 2 * px + py
                _rcopy(ins[i].at[rows], outs[i].at[jp, rows], ssem.at[i, k], rsem.at[i, k], (px, py, c)).wait_recv()
                f = _rcopy(outs[i].at[jp, rows], outs[i].at[jp, rows], fssem.at[i, k], frsem.at[i, k], sib)
                f.start()
                remote.append(f)
        for i in range(n):
            half = ins[i].shape[0] // 2
            orow = pl.ds((1 - c) * half, half)
            for k, (px, py) in enumerate(chips):
                jp = 2 * px + py
                _rcopy(outs[i].at[jp, orow], outs[i].at[jp, orow], fssem.at[i, k], frsem.at[i, k], sib).wait_recv()
        for r in remote:
            r.wait_send()
        for cp in local:
            cp.wait()

    return _pallas(
        body, name="gather_weights", in_specs=[_ANY] * n, out_specs=[_ANY] * n,
        out_shape=[jax.ShapeDtypeStruct((N_CHIPS,) + s.shape, s.dtype) for s in shards],
        scratch_shapes=[pltpu.SemaphoreType.DMA((n,)), pltpu.SemaphoreType.DMA((n, 3)),
                        pltpu.SemaphoreType.DMA((n, 3)), pltpu.SemaphoreType.DMA((n, 3)),
                        pltpu.SemaphoreType.DMA((n, 3))],
        compiler_params=pltpu.CompilerParams(has_side_effects=True),
    )(*shards)


def _swap_halves(grads):
    n = len(grads)

    def body(*refs):
        ins, outs = refs[:n], refs[n:2 * n]
        ssem, rsem = refs[2 * n:]
        x, y, c = _me()
        sib = (x, y, 1 - c)
        cps = []
        for i in range(n):
            half = ins[i].shape[1] // 2
            r = _rcopy(ins[i].at[:, pl.ds((1 - c) * half, half)], outs[i], ssem.at[i], rsem.at[i], sib)
            r.start()
            cps.append(r)
        for r in cps:
            r.wait()

    return _pallas(
        body, name="swap_halves", in_specs=[_ANY] * n, out_specs=[_ANY] * n,
        out_shape=[jax.ShapeDtypeStruct((N_CHIPS, g.shape[1] // 2, g.shape[2]), g.dtype) for g in grads],
        scratch_shapes=[pltpu.SemaphoreType.DMA((n,)), pltpu.SemaphoreType.DMA((n,))],
        compiler_params=pltpu.CompilerParams(has_side_effects=True),
    )(*grads)


def _scatter_chips(sums):
    n = len(sums)

    def body(*refs):
        ins, outs = refs[:n], refs[n:2 * n]
        lsem, ssem, rsem = refs[2 * n:]
        x, y, c = _me()
        j = 2 * x + y
        chips = _other_chips(x, y)
        local, remote = [], []
        for i in range(n):
            cp = pltpu.make_async_copy(ins[i].at[j], outs[i].at[j], lsem.at[i])
            cp.start()
            local.append(cp)
            for k, (px, py) in enumerate(chips):
                jp = 2 * px + py
                r = _rcopy(ins[i].at[jp], outs[i].at[j], ssem.at[i, k], rsem.at[i, k], (px, py, c))
                r.start()
                remote.append(r)
        for i in range(n):
            for k, (px, py) in enumerate(chips):
                jp = 2 * px + py
                _rcopy(ins[i].at[jp], outs[i].at[jp], ssem.at[i, k], rsem.at[i, k], (px, py, c)).wait_recv()
        for r in remote:
            r.wait_send()
        for cp in local:
            cp.wait()

    return _pallas(
        body, name="scatter_chips", in_specs=[_ANY] * n, out_specs=[_ANY] * n,
        out_shape=[jax.ShapeDtypeStruct(s.shape, s.dtype) for s in sums],
        scratch_shapes=[pltpu.SemaphoreType.DMA((n,)), pltpu.SemaphoreType.DMA((n, 3)),
                        pltpu.SemaphoreType.DMA((n, 3))],
        compiler_params=pltpu.CompilerParams(has_side_effects=True),
    )(*sums)


def _join_halves(halves):
    n = len(halves)

    def body(*refs):
        ins, outs = refs[:n], refs[n:2 * n]
        lsem, ssem, rsem = refs[2 * n:]
        x, y, c = _me()
        sib = (x, y, 1 - c)
        cps = []
        for i in range(n):
            half = ins[i].shape[0]
            rows = pl.ds(c * half, half)
            cp = pltpu.make_async_copy(ins[i], outs[i].at[rows], lsem.at[i])
            cp.start()
            cps.append(cp)
            r = _rcopy(ins[i], outs[i].at[rows], ssem.at[i], rsem.at[i], sib)
            r.start()
            cps.append(r)
        for cp in cps:
            cp.wait()

    return _pallas(
        body, name="join_halves", in_specs=[_ANY] * n, out_specs=[_ANY] * n,
        out_shape=[jax.ShapeDtypeStruct((2 * h.shape[0], h.shape[1]), h.dtype) for h in halves],
        scratch_shapes=[pltpu.SemaphoreType.DMA((n,)), pltpu.SemaphoreType.DMA((n,)), pltpu.SemaphoreType.DMA((n,))],
        compiler_params=pltpu.CompilerParams(has_side_effects=True),
    )(*halves)


def _allreduce_small(vec):
    R = vec.shape[0]

    def body(v_ref, o_ref, buf, ssem, rsem):
        x, y, c = _me()
        me = 4 * x + 2 * y + c
        buf[me] = v_ref[...]
        cps = []
        for k in range(1, 8):
            peer = (x ^ (k >> 2), y ^ ((k >> 1) & 1), c ^ (k & 1))
            r = _rcopy(v_ref, buf.at[me], ssem.at[k - 1], rsem.at[k - 1], peer)
            r.start()
            cps.append(r)
        for k in range(1, 8):
            peer = (x ^ (k >> 2), y ^ ((k >> 1) & 1), c ^ (k & 1))
            pid = 4 * peer[0] + 2 * peer[1] + peer[2]
            _rcopy(v_ref, buf.at[pid], ssem.at[k - 1], rsem.at[k - 1], peer).wait_recv()
        for r in cps:
            r.wait_send()
        tot = buf[0]
        for d in range(1, 8):
            tot = tot + buf[d]
        o_ref[...] = tot

    vm = pl.BlockSpec(memory_space=pltpu.VMEM)
    return _pallas(
        body, name="allreduce_small", in_specs=[vm], out_specs=vm,
        out_shape=jax.ShapeDtypeStruct((R, LANES), F32),
        scratch_shapes=[pltpu.VMEM((8, R, LANES), F32), pltpu.SemaphoreType.DMA((7,)), pltpu.SemaphoreType.DMA((7,))],
        compiler_params=pltpu.CompilerParams(has_side_effects=True),
    )(vec)


def _rope_tables(positions, S):
    pos = positions.reshape(S, 1).astype(F32)
    half = RET_QK // 2
    inv = ROPE_THETA ** (-jnp.arange(half, dtype=F32) / half)
    ang = pos * inv
    cosr = jnp.concatenate([jnp.cos(ang), jnp.cos(ang)], axis=1)
    sinr = jnp.concatenate([-jnp.sin(ang), jnp.sin(ang)], axis=1)
    half = QK_ROPE // 2
    inv = ROPE_THETA ** (-jnp.arange(half, dtype=F32) / half)
    ang = pos * inv
    z = jnp.zeros((S, half), F32)
    c = jnp.concatenate([jnp.cos(ang), jnp.cos(ang), z, z], axis=1)
    s1 = jnp.concatenate([-jnp.sin(ang), z, z, z], axis=1)
    s2 = jnp.concatenate([z, jnp.sin(ang), z, z], axis=1)
    return cosr, sinr, (c, s1, s2)


def _cat_cols(g):
    return jnp.concatenate([g[j] for j in range(N_CHIPS)], axis=1)


def _split_cols(w):
    return jnp.stack(jnp.split(w, N_CHIPS, axis=1))


def _pack_small(vs, rows):
    flat = jnp.concatenate([v.reshape(-1) for v in vs])
    flat = jnp.pad(flat, (0, rows * LANES - flat.shape[0]))
    return flat.reshape(rows, LANES)


def kernel(x, positions, norm_mix_g, w_in, ret_norm_g, w_ret_o, q_a_norm_g, w_q_b, kv_a_norm_g, w_kv_b, w_mla_o, w_out, norm_mlp_g, w_up, w_down, norm_f_g, loss_target, m_norm_mix_g, m_w_in, m_ret_norm_g, m_w_ret_o, m_q_a_norm_g, m_w_q_b, m_kv_a_norm_g, m_w_kv_b, m_w_mla_o, m_w_out, m_norm_mlp_g, m_w_up, m_w_down, m_norm_f_g, v_norm_mix_g, v_w_in, v_ret_norm_g, v_w_ret_o, v_q_a_norm_g, v_w_q_b, v_kv_a_norm_g, v_w_kv_b, v_w_mla_o, v_w_out, v_norm_mlp_g, v_w_up, v_w_down, v_norm_f_g):
    S, D = x.shape[1], x.shape[2]
    RVW = w_ret_o.shape[1] * N_CHIPS
    RH = RVW // RET_V
    RQW = RH * RET_QK
    MVW = w_mla_o.shape[1] * N_CHIPS
    MH = MVW // V_HEAD
    QL, KVL = w_q_b.shape[1], w_kv_b.shape[1]
    DFF = w_up.shape[2] * N_CHIPS
    T_RET = _tile(S, 256)
    T_ATT = _tile(S, 512)

    xs = x.reshape(S, D)
    tgt = loss_target.reshape(S, D)
    cosr, sinr, pe_tabs = _rope_tables(positions, S)
    lgam = jnp.log(1.0 - 2.0 ** (-5.0 - jnp.arange(RH, dtype=F32)))
    lgam = jnp.broadcast_to(lgam[:, None, None], (RH, 8, LANES))

    big = ("w_in", "w_ret_o", "w_q_b", "w_kv_b", "w_mla_o", "w_out", "w_up", "w_down")
    w_sh = dict(w_in=w_in[0], w_ret_o=w_ret_o[0], w_q_b=w_q_b[0], w_kv_b=w_kv_b[0], w_mla_o=w_mla_o[0],
                w_out=w_out[0], w_up=w_up[0], w_down=w_down[0])
    m_sh = dict(w_in=m_w_in[0], w_ret_o=m_w_ret_o[0], w_q_b=m_w_q_b[0], w_kv_b=m_w_kv_b[0], w_mla_o=m_w_mla_o[0],
                w_out=m_w_out[0], w_up=m_w_up[0], w_down=m_w_down[0])
    v_sh = dict(w_in=v_w_in[0], w_ret_o=v_w_ret_o[0], w_q_b=v_w_q_b[0], w_kv_b=v_w_kv_b[0], w_mla_o=v_w_mla_o[0],
                w_out=v_w_out[0], w_up=v_w_up[0], w_down=v_w_down[0])
    col_sharded = ("w_in", "w_q_b", "w_kv_b", "w_up")
    gathered = _gather_weights([w_sh[k].astype(BF16) for k in big])
    full = {}
    for k, g in zip(big, gathered):
        full[k] = _cat_cols(g) if k in col_sharded else g.reshape(-1, g.shape[2])

    n_a = 2 * RQW + 2 * RVW + QL + KVL
    off_cq = 2 * RQW + 2 * RVW
    off_ckv = off_cq + QL
    off_gret, off_gmla = n_a, n_a + D
    wa = jnp.concatenate([full["w_in"][:, :n_a], full["w_in"][:, n_a + QK_ROPE:]], axis=1)
    wkpe = jnp.pad(full["w_in"][:, n_a:n_a + QK_ROPE], ((0, 0), (0, LANES - QK_ROPE)))
    wq = full["w_q_b"].reshape(QL, MH, QK_NOPE + QK_ROPE)
    wq = jnp.concatenate([wq[:, :, :QK_NOPE].reshape(QL, MH * QK_NOPE),
                          jnp.pad(wq[:, :, QK_NOPE:], ((0, 0), (0, 0), (0, LANES - QK_ROPE))).reshape(QL, MH * LANES)],
                         axis=1)
    wkv = full["w_kv_b"].reshape(KVL, MH, QK_NOPE + V_HEAD)
    wkv = jnp.concatenate([wkv[:, :, :QK_NOPE].reshape(KVL, MH * QK_NOPE),
                           wkv[:, :, QK_NOPE:].reshape(KVL, MH * V_HEAD)], axis=1)

    u, rstd0 = _rmsnorm_fwd(xs, norm_mix_g, name="norm_mix")
    proj = _mm(u, wa, mode="nn", outs=[F32], name="in_proj")
    kpe = _mm(u, wkpe, mode="nn", outs=[F32], name="kpe_proj")
    ry, gated, states = _ret_fwd(proj, cosr, sinr, lgam, ret_norm_g, RH, T=T_RET)
    y_ret = _mm(gated, full["w_ret_o"], mode="nn", outs=[F32], name="ret_o")
    cqn, rstd_q = _rmsnorm_fwd(proj, q_a_norm_g, name="norm_q", width=QL, col=off_cq // QL)
    ckvn, rstd_kv = _rmsnorm_fwd(proj, kv_a_norm_g, name="norm_kv", width=KVL, col=off_ckv // KVL)
    q_all = _mm(cqn, wq, mode="nn", outs=[F32], name="q_proj")
    kv_all = _mm(ckvn, wkv, mode="nn", outs=[F32], name="kv_proj")
    qf, kf, vb = _attn_prep(q_all, kv_all, kpe, pe_tabs, MH)
    my, lse = _attn_fwd(qf, kf, vb, MH, T=T_ATT)
    y_mla = _mm(my, full["w_mla_o"], mode="nn", outs=[F32], name="mla_o")
    merged = _merge_fwd(proj, y_ret, y_mla, D, off_gret, off_gmla)
    h1 = _mm(merged, full["w_out"], mode="nn", outs=[F32], name="out_proj",
             epi=lambda acc, r: (acc + r,), extras=(xs,))
    n1, rstd1 = _rmsnorm_fwd(h1, norm_mlp_g, name="norm_mlp")

    def up_epi(acc):
        r = jnp.maximum(acc, 0.0)
        return acc, r * r

    z, act = _mm(n1, full["w_up"], mode="nn", outs=[F32, BF16], name="up_proj", epi=up_epi)
    h2 = _mm(act, full["w_down"], mode="nn", outs=[F32], name="down_proj",
             epi=lambda acc, r: (acc + r,), extras=(h1,))
    loss11, dh2, g_norm_f = _final_loss(h2, norm_f_g.reshape(1, D), tgt)

    dz = _mm(dh2, full["w_down"], mode="nt", outs=[BF16], name="down_bwd_x",
             epi=lambda acc, zz: (acc * (2.0 * jnp.maximum(zz, 0.0)),), extras=(z,))
    g_w_down = _mm(act, dh2, mode="tn", outs=[F32], name="down_bwd_w")
    dn1 = _mm(dz, full["w_up"], mode="nt", outs=[F32], name="up_bwd_x")
    g_w_up = _mm(n1, dz, mode="tn", outs=[F32], name="up_bwd_w")
    dh1, g_norm_mlp = _rmsnorm_bwd(dn1, h1, rstd1, norm_mlp_g, name="norm_mlp_bwd", res=dh2)
    dmerged = _mm(dh1, full["w_out"], mode="nt", outs=[F32], name="out_bwd_x")
    g_w_out = _mm(merged, dh1, mode="tn", outs=[F32], name="out_bwd_w")
    dy_ret, dy_mla, dg_ret, dg_mla = _merge_bwd(dmerged, proj, y_ret, y_mla, D, off_gret, off_gmla)
    dgated = _mm(dy_ret, full["w_ret_o"], mode="nt", outs=[F32], name="ret_o_bwd_x")
    g_w_ret_o = _mm(gated, dy_ret, mode="tn", outs=[F32], name="ret_o_bwd_w")
    drq, drk, drv, drg, g_ret_norm = _ret_bwd(proj, cosr, sinr, lgam, ret_norm_g, ry, dgated, states, RH, T=T_RET)
    dmy = _mm(dy_mla, full["w_mla_o"], mode="nt", outs=[F32], name="mla_o_bwd_x")
    g_w_mla_o = _mm(my, dy_mla, mode="tn", outs=[F32], name="mla_o_bwd_w")
    dqf, dkf, dvb = _attn_bwd(qf, kf, vb, my, lse, dmy, MH, T=T_ATT)
    dqn, dqp, dkn, dvo, dkpe = _attn_post(dqf, dkf, dvb, pe_tabs, MH)
    dq_all = jnp.concatenate([dqn, dqp], axis=1)
    dkv_all = jnp.concatenate([dkn, dvo], axis=1)
    dcqn = _mm(dq_all, wq, mode="nt", outs=[F32], name="q_bwd_x")
    g_wq = _mm(cqn, dq_all, mode="tn", outs=[F32], name="q_bwd_w")
    dckvn = _mm(dkv_all, wkv, mode="nt", outs=[F32], name="kv_bwd_x")
    g_wkv = _mm(ckvn, dkv_all, mode="tn", outs=[F32], name="kv_bwd_w")
    dcq, g_q_a = _rmsnorm_bwd(dcqn, proj, rstd_q, q_a_norm_g, name="norm_q_bwd", out_dtype=BF16,
                              width=QL, col=off_cq // QL)
    dckv, g_kv_a = _rmsnorm_bwd(dckvn, proj, rstd_kv, kv_a_norm_g, name="norm_kv_bwd", out_dtype=BF16,
                                width=KVL, col=off_ckv // KVL)
    dproj = jnp.concatenate([drq, drk, drv, drg, dcq, dckv, dg_ret, dg_mla], axis=1)
    du_a = _mm(dproj, wa, mode="nt", outs=[F32], name="in_bwd_x", tk=1024)
    du = _mm(dkpe, wkpe, mode="nt", outs=[F32], name="kpe_bwd_x", epi=lambda acc, r: (acc + r,), extras=(du_a,))
    g_wa = _mm(u, dproj, mode="tn", outs=[F32], name="in_bwd_w")
    g_wkpe = _mm(u, dkpe, mode="tn", outs=[F32], name="kpe_bwd_w")
    dx, g_norm_mix = _rmsnorm_bwd(du, xs, rstd0, norm_mix_g, name="norm_mix_bwd", res=dh1)

    g_w_in = jnp.concatenate([g_wa[:, :n_a], g_wkpe[:, :QK_ROPE], g_wa[:, n_a:]], axis=1)
    gq = jnp.concatenate([g_wq[:, :MH * QK_NOPE].reshape(QL, MH, QK_NOPE),
                          g_wq[:, MH * QK_NOPE:].reshape(QL, MH, LANES)[:, :, :QK_ROPE]], axis=2)
    gq = gq.reshape(QL, MH * (QK_NOPE + QK_ROPE))
    gkv = jnp.concatenate([g_wkv[:, :MH * QK_NOPE].reshape(KVL, MH, QK_NOPE),
                           g_wkv[:, MH * QK_NOPE:].reshape(KVL, MH, V_HEAD)], axis=2)
    gkv = gkv.reshape(KVL, MH * (QK_NOPE + V_HEAD))
    g_full = dict(w_in=g_w_in, w_ret_o=g_w_ret_o, w_q_b=gq, w_kv_b=gkv, w_mla_o=g_w_mla_o, w_out=g_w_out,
                  w_up=g_w_up, w_down=g_w_down)
    pieces = []
    for k in big:
        g = g_full[k]
        pieces.append(_split_cols(g) if k in col_sharded else g.reshape(N_CHIPS, g.shape[0] // N_CHIPS, g.shape[1]))

    c = lax.axis_index("c")
    j = 2 * lax.axis_index("x") + lax.axis_index("y")
    theirs = _swap_halves(pieces)
    sums = []
    for k, p, t in zip(big, pieces, theirs):
        half = p.shape[1] // 2
        mine = lax.dynamic_slice_in_dim(p, c * half, half, axis=1)
        s = _rows_call(lambda a, b: (a + b,), [mine.reshape(-1, p.shape[2]), t.reshape(-1, p.shape[2])], [F32],
                       name="sum_pair_" + k)[0]
        sums.append(s.reshape(t.shape))
    recv = _scatter_chips(sums)
    halves = []
    for k, r in zip(big, recv):
        halves.append(_rows_call(lambda a, b, cc, d: (((a + b) + cc) + d,), [r[0], r[1], r[2], r[3]], [F32],
                                 name="sum_chips_" + k)[0])
    g_shard = dict(zip(big, _join_halves(halves)))

    small = ("norm_mix_g", "ret_norm_g", "q_a_norm_g", "kv_a_norm_g", "norm_mlp_g", "norm_f_g")
    g_small = [g_norm_mix, g_ret_norm, g_q_a, g_kv_a, g_norm_mlp, g_norm_f]
    sizes = [int(v.size) for v in g_small]
    n_small = sum(sizes) + LANES
    rows = -(-n_small // (8 * LANES)) * 8
    packed = _pack_small(g_small + [jnp.broadcast_to(loss11.reshape(1), (LANES,))], rows)
    red = _allreduce_small(packed).reshape(-1)
    loss = red[sum(sizes)]
    w_small = [norm_mix_g, ret_norm_g, q_a_norm_g, kv_a_norm_g, norm_mlp_g, norm_f_g]
    m_small = [m_norm_mix_g, m_ret_norm_g, m_q_a_norm_g, m_kv_a_norm_g, m_norm_mlp_g, m_norm_f_g]
    v_small = [v_norm_mix_g, v_ret_norm_g, v_q_a_norm_g, v_kv_a_norm_g, v_norm_mlp_g, v_norm_f_g]
    g_pk = red[:rows * LANES].reshape(rows, LANES)
    d_pk, m_pk, v_pk = _rows_call(_adamw_vals, [_pack_small(w_small, rows), g_pk, _pack_small(m_small, rows),
                                               _pack_small(v_small, rows)], [F32, F32, F32], name="adamw_small")
    out_g, out_d, out_m, out_v = {}, {}, {}, {}
    off = 0
    for k, wv, sz in zip(small, w_small, sizes):
        for dst, src in ((out_g, g_pk), (out_d, d_pk), (out_m, m_pk), (out_v, v_pk)):
            dst[k] = src.reshape(-1)[off:off + sz].reshape(wv.shape)
        off += sz

    for k in big:
        d_, m_, v_ = _rows_call(_adamw_vals, [w_sh[k], g_shard[k], m_sh[k], v_sh[k]], [F32, F32, F32],
                                name="adamw_" + k)
        out_g[k] = g_shard[k][None]
        out_d[k], out_m[k], out_v[k] = d_[None], m_[None], v_[None]

    order = ("norm_mix_g", "w_in", "ret_norm_g", "w_ret_o", "q_a_norm_g", "w_q_b", "kv_a_norm_g", "w_kv_b",
             "w_mla_o", "w_out", "norm_mlp_g", "w_up", "w_down", "norm_f_g")
    return (loss, dx.reshape(1, S, D), *[out_g[k] for k in order], *[out_d[k] for k in order],
            *[out_m[k] for k in order], *[out_v[k] for k in order])
```

```python
import math

import jax
import jax.numpy as jnp
from jax import lax
from jax.experimental import pallas as pl
from jax.experimental.pallas import tpu as pltpu

F32 = jnp.float32
BF16 = jnp.bfloat16

EPS = 1e-6
ROPE_THETA = 10000.0
CHUNK = 64
RET_QK = 128
RET_V = 256
RET_HEAD_COLS = 2 * RET_QK + 2 * RET_V
QK_NOPE = 128
QK_ROPE = 64
V_HEAD = 128
LANES = 128
LOG2E = math.log2(math.e)

ADAM_LR = 0.001
ADAM_B1 = 0.9
ADAM_B2 = 0.999
ADAM_EPS = 1e-08
ADAM_WD = 0.01
ADAM_STEP = 10

N_CHIPS = 4
VMEM_LIMIT = 56 * 1024 * 1024
MESH = pl.DeviceIdType.MESH
NEG = -1e30


def _pallas(body, **kw):
    return pl.pallas_call(body, **kw)


def _params(sem=None):
    return pltpu.CompilerParams(dimension_semantics=sem, vmem_limit_bytes=VMEM_LIMIT)


def _tile(n, want):
    t = min(n, want)
    while n % t:
        t //= 2
    return t


_ANY = pl.BlockSpec(memory_space=pl.ANY)


def _mm(a, b, *, mode, outs, name, epi=None, extras=(), tm=1024, tn=1024, tk=2048):
    if mode == "nn":
        (M, K), (_, N) = a.shape, b.shape
    elif mode == "nt":
        (M, K), (N, _) = a.shape, b.shape
    else:
        (K, M), (_, N) = a.shape, b.shape
    tm, tn, tk = _tile(M, tm), _tile(N, tn), _tile(K, tk)
    nk = K // tk
    if mode == "nn":
        a_spec = pl.BlockSpec((tm, tk), lambda i, j, k: (i, k))
        b_spec = pl.BlockSpec((tk, tn), lambda i, j, k: (k, j))
        dims = (((1,), (0,)), ((), ()))
    elif mode == "nt":
        a_spec = pl.BlockSpec((tm, tk), lambda i, j, k: (i, k))
        b_spec = pl.BlockSpec((tn, tk), lambda i, j, k: (j, k))
        dims = (((1,), (1,)), ((), ()))
    else:
        a_spec = pl.BlockSpec((tk, tm), lambda i, j, k: (k, i))
        b_spec = pl.BlockSpec((tk, tn), lambda i, j, k: (k, j))
        dims = (((0,), (0,)), ((), ()))
    o_spec = pl.BlockSpec((tm, tn), lambda i, j, k: (i, j))
    n_ex, n_out = len(extras), len(outs)
    if epi is None:
        epi = lambda acc: (acc,)

    def body(*refs):
        a_ref, b_ref = refs[0], refs[1]
        ex_refs = refs[2:2 + n_ex]
        o_refs = refs[2 + n_ex:2 + n_ex + n_out]
        part = lax.dot_general(a_ref[...].astype(BF16), b_ref[...].astype(BF16), dims,
                               preferred_element_type=F32)

        def finish(acc):
            vals = epi(acc, *[r[...] for r in ex_refs])
            for r, v in zip(o_refs, vals):
                r[...] = v.astype(r.dtype)

        if nk == 1:
            finish(part)
        else:
            acc_ref = refs[-1]
            k = pl.program_id(2)

            @pl.when(k == 0)
            def _():
                acc_ref[...] = part

            @pl.when(k > 0)
            def _():
                acc_ref[...] += part

            @pl.when(k == nk - 1)
            def _():
                finish(acc_ref[...])

    res = _pallas(
        body, name=name, grid=(M // tm, N // tn, nk),
        in_specs=[a_spec, b_spec] + [o_spec] * n_ex,
        out_specs=[o_spec] * n_out,
        out_shape=[jax.ShapeDtypeStruct((M, N), d) for d in outs],
        scratch_shapes=[pltpu.VMEM((tm, tn), F32)] if nk > 1 else [],
        compiler_params=_params(("parallel", "parallel", "arbitrary")),
    )(a, b, *extras)
    return res[0] if n_out == 1 else res


def _rmsnorm_fwd(x, g, *, name, width=None, col=0, tr=256):
    S = x.shape[0]
    W = x.shape[1] if width is None else width
    tr = _tile(S, tr)

    def body(x_ref, g_ref, y_ref, r_ref):
        xv = x_ref[...]
        rstd = lax.rsqrt(jnp.mean(xv * xv, axis=-1, keepdims=True) + EPS)
        y_ref[...] = (xv * rstd * g_ref[...]).astype(BF16)
        r_ref[...] = rstd

    return _pallas(
        body, name=name, grid=(S // tr,),
        in_specs=[pl.BlockSpec((tr, W), lambda i: (i, col)), pl.BlockSpec((1, W), lambda i: (0, 0))],
        out_specs=[pl.BlockSpec((tr, W), lambda i: (i, 0)), pl.BlockSpec((tr, 1), lambda i: (i, 0))],
        out_shape=[jax.ShapeDtypeStruct((S, W), BF16), jax.ShapeDtypeStruct((S, 1), F32)],
        compiler_params=_params(("parallel",)),
    )(x, g)


def _rmsnorm_bwd(dy, x, rstd, g, *, name, res=None, into=None, width=None, col=0, tr=256):
    S = x.shape[0]
    W = x.shape[1] if width is None else width
    tr = _tile(S, tr)
    has_res = res is not None

    def body(*refs):
        dy_ref, x_ref, r_ref, g_ref = refs[:4]
        dx_ref, dg_ref = refs[-2], refs[-1]
        rstd_v = r_ref[...]
        xhat = x_ref[...] * rstd_v
        dyv = dy_ref[...].astype(F32)
        dyg = dyv * g_ref[...]
        dx = rstd_v * (dyg - xhat * jnp.mean(dyg * xhat, axis=-1, keepdims=True))
        if has_res:
            dx = dx + refs[4][...]
        dx_ref[...] = dx.astype(dx_ref.dtype)
        part = jnp.sum(dyv * xhat, axis=0, keepdims=True)

        @pl.when(pl.program_id(0) == 0)
        def _():
            dg_ref[...] = part

        @pl.when(pl.program_id(0) > 0)
        def _():
            dg_ref[...] += part

    row = pl.BlockSpec((tr, W), lambda i: (i, 0))
    ins = [dy, x, rstd, g] + ([res] if has_res else [])
    in_specs = [row, pl.BlockSpec((tr, W), lambda i: (i, col)), pl.BlockSpec((tr, 1), lambda i: (i, 0)),
                pl.BlockSpec((1, W), lambda i: (0, 0))] + ([row] if has_res else [])
    if into is None:
        dx_spec, dx_shape, alias = row, jax.ShapeDtypeStruct((S, W), F32), {}
    else:
        buf, col_out = into
        ins.append(buf)
        in_specs.append(_ANY)
        dx_spec = pl.BlockSpec((tr, W), lambda i: (i, col_out))
        dx_shape = jax.ShapeDtypeStruct(buf.shape, buf.dtype)
        alias = {len(ins) - 1: 0}
    return _pallas(
        body, name=name, grid=(S // tr,), in_specs=in_specs,
        out_specs=[dx_spec, pl.BlockSpec((1, W), lambda i: (0, 0))],
        out_shape=[dx_shape, jax.ShapeDtypeStruct((1, W), F32)],
        input_output_aliases=alias,
        compiler_params=_params(("arbitrary",)),
    )(*ins)


def _final_loss(h2, g, target, *, tr=256):
    S, D = h2.shape
    tr = _tile(S, tr)

    def body(h_ref, g_ref, t_ref, loss_ref, dh_ref, dg_ref):
        hv = h_ref[...]
        rstd = lax.rsqrt(jnp.mean(hv * hv, axis=-1, keepdims=True) + EPS)
        xhat = hv * rstd
        e = xhat * g_ref[...] - t_ref[...]
        lpart = (0.5 / D) * jnp.sum(jnp.sum(e * e, axis=-1, keepdims=True), axis=0, keepdims=True)
        dy = e * (1.0 / D)
        dyg = dy * g_ref[...]
        dh_ref[...] = rstd * (dyg - xhat * jnp.mean(dyg * xhat, axis=-1, keepdims=True))
        gpart = jnp.sum(dy * xhat, axis=0, keepdims=True)

        @pl.when(pl.program_id(0) == 0)
        def _():
            loss_ref[...] = lpart
            dg_ref[...] = gpart

        @pl.when(pl.program_id(0) > 0)
        def _():
            loss_ref[...] += lpart
            dg_ref[...] += gpart

    row = pl.BlockSpec((tr, D), lambda i: (i, 0))
    vec = pl.BlockSpec((1, D), lambda i: (0, 0))
    return _pallas(
        body, name="final_loss", grid=(S // tr,), in_specs=[row, vec, row],
        out_specs=[pl.BlockSpec((1, 1), lambda i: (0, 0)), row, vec],
        out_shape=[jax.ShapeDtypeStruct((1, 1), F32), jax.ShapeDtypeStruct((S, D), F32),
                   jax.ShapeDtypeStruct((1, D), F32)],
        compiler_params=_params(("arbitrary",)),
    )(h2, g, target)


def _sigmoid(v):
    return 1.0 / (1.0 + jnp.exp(-v))


def _merge_fwd(proj, y_ret, y_mla, D, off_gret, off_gmla, *, tr=256, tc=1024):
    S = y_ret.shape[0]
    tr, tc = _tile(S, tr), _tile(D, tc)
    b_ret, b_mla = off_gret // tc, off_gmla // tc

    def body(gr_ref, gm_ref, yr_ref, ym_ref, o_ref):
        o_ref[...] = (_sigmoid(gr_ref[...]) * yr_ref[...] + _sigmoid(gm_ref[...]) * ym_ref[...]).astype(BF16)

    blk = pl.BlockSpec((tr, tc), lambda i, j: (i, j))
    return _pallas(
        body, name="merge_fwd", grid=(S // tr, D // tc),
        in_specs=[pl.BlockSpec((tr, tc), lambda i, j: (i, b_ret + j)),
                  pl.BlockSpec((tr, tc), lambda i, j: (i, b_mla + j)), blk, blk],
        out_specs=blk, out_shape=jax.ShapeDtypeStruct((S, D), BF16),
        compiler_params=_params(("parallel", "parallel")),
    )(proj, proj, y_ret, y_mla)


def _merge_bwd(dmerged, proj, y_ret, y_mla, D, off_gret, *, tr=256):
    S = y_ret.shape[0]
    tr = _tile(S, tr)
    b0 = off_gret // D

    def body(dm_ref, g_ref, yr_ref, ym_ref, dp_ref, dyr_ref, dym_ref):
        dm = dm_ref[...]
        sg = _sigmoid(g_ref[...])

        @pl.when(pl.program_id(1) == 0)
        def _():
            dyr_ref[...] = (dm * sg).astype(BF16)
            dp_ref[...] = (dm * yr_ref[...] * sg * (1.0 - sg)).astype(BF16)

        @pl.when(pl.program_id(1) == 1)
        def _():
            dym_ref[...] = (dm * sg).astype(BF16)
            dp_ref[...] = (dm * ym_ref[...] * sg * (1.0 - sg)).astype(BF16)

    blk = pl.BlockSpec((tr, D), lambda i, j: (i, 0))
    return _pallas(
        body, name="merge_bwd", grid=(S // tr, 2),
        in_specs=[blk, pl.BlockSpec((tr, D), lambda i, j: (i, b0 + j)), blk, blk],
        out_specs=[pl.BlockSpec((tr, D), lambda i, j: (i, b0 + j)), blk, blk],
        out_shape=[jax.ShapeDtypeStruct(proj.shape, BF16), jax.ShapeDtypeStruct((S, D), BF16),
                   jax.ShapeDtypeStruct((S, D), BF16)],
        compiler_params=_params(("parallel", "arbitrary")),
    )(dmerged, proj, y_ret, y_mla)


def _rope128(t, cos_full, sin_signed):
    return t * cos_full + pltpu.roll(t, RET_QK // 2, 1) * sin_signed


def _rope128_t(d, cos_full, sin_signed):
    return d * cos_full + pltpu.roll(d * sin_signed, RET_QK // 2, 1)


def _ret_consts(lg, T):
    pos = lax.broadcasted_iota(jnp.int32, (T, 1), 0).astype(F32)
    qd = jnp.exp(lg * (pos + 1.0))
    kd = jnp.exp(lg * (T - 1.0 - pos))
    n = lax.broadcasted_iota(jnp.int32, (T, T), 0)
    m = lax.broadcasted_iota(jnp.int32, (T, T), 1)
    vis = (m // CHUNK) <= (n // CHUNK)
    dist = jnp.abs(n - m).astype(F32)
    decay = jnp.where(vis, jnp.exp(lg * dist), 0.0)
    cdec = jnp.exp(lg * float(T))
    return qd, kd, decay, cdec


def _dot(a, b, dims):
    return lax.dot_general(a.astype(BF16), b.astype(BF16), (dims, ((), ())), preferred_element_type=F32)


NN = ((1,), (0,))
NT = ((1,), (1,))
TN = ((0,), (0,))
_RQ = slice(0, RET_QK)
_RK = slice(RET_QK, 2 * RET_QK)
_RV = slice(2 * RET_QK, 2 * RET_QK + RET_V)
_RG = slice(2 * RET_QK + RET_V, RET_HEAD_COLS)


def _ret_fwd(proj, cosr, sinr, lgam, gain, RH, *, T):
    S = proj.shape[0]
    nb = S // T
    scale = RET_QK ** -0.5

    def body(p_ref, cos_ref, sin_ref, lg_ref, gain_ref, ry_ref, gated_ref, st_ref, state):
        b = pl.program_id(1)

        @pl.when(b == 0)
        def _():
            state[...] = jnp.zeros_like(state)

        lg = lg_ref[0:1, 0:1]
        qd, kd, decay, cdec = _ret_consts(lg, T)
        cosv, sinv = cos_ref[...], sin_ref[...]
        q = _rope128(p_ref[:, _RQ], cosv, sinv)
        k = _rope128(p_ref[:, _RK], cosv, sinv) * scale
        v = p_ref[:, _RV]
        sprev = state[...]
        st_ref[...] = sprev
        a = _dot(q, k, NT) * decay
        o = _dot(a, v, NN) + _dot(q * qd, sprev, NN)
        state[...] = sprev * cdec + _dot(k * kd, v, TN)
        ry_ref[...] = o
        mu = jnp.mean(o, axis=-1, keepdims=True)
        oc = o - mu
        var = jnp.mean(oc * oc, axis=-1, keepdims=True)
        t = oc * lax.rsqrt(var + EPS) * gain_ref[...]
        gv = p_ref[:, _RG]
        gated_ref[...] = (t * (gv * _sigmoid(gv))).astype(BF16)

    return _pallas(
        body, name="ret_fwd", grid=(RH, nb),
        in_specs=[pl.BlockSpec((T, RET_HEAD_COLS), lambda h, b: (b, h)),
                  pl.BlockSpec((T, RET_QK), lambda h, b: (b, 0)),
                  pl.BlockSpec((T, RET_QK), lambda h, b: (b, 0)),
                  pl.BlockSpec((None, 8, LANES), lambda h, b: (h, 0, 0)),
                  pl.BlockSpec((1, RET_V), lambda h, b: (0, h))],
        out_specs=[pl.BlockSpec((T, RET_V), lambda h, b: (b, h)),
                   pl.BlockSpec((T, RET_V), lambda h, b: (b, h)),
                   pl.BlockSpec((None, None, RET_QK, RET_V), lambda h, b: (h, b, 0, 0))],
        out_shape=[jax.ShapeDtypeStruct((S, RH * RET_V), F32), jax.ShapeDtypeStruct((S, RH * RET_V), BF16),
                   jax.ShapeDtypeStruct((RH, nb, RET_QK, RET_V), F32)],
        scratch_shapes=[pltpu.VMEM((RET_QK, RET_V), F32)],
        compiler_params=_params(("parallel", "arbitrary")),
    )(proj, cosr, sinr, lgam, gain)


def _ret_bwd(proj, cosr, sinr, lgam, gain, ry, dgated, states, dproj, RH, *, T):
    S = proj.shape[0]
    nb = S // T
    scale = RET_QK ** -0.5

    def body(p_ref, cos_ref, sin_ref, lg_ref, gain_ref, ry_ref, dg_ref, st_ref, _, dp_ref, dgain_ref, dstate):
        b = pl.program_id(1)

        @pl.when(b == 0)
        def _():
            dstate[...] = jnp.zeros_like(dstate)

        lg = lg_ref[0:1, 0:1]
        qd, kd, decay, cdec = _ret_consts(lg, T)
        cosv, sinv = cos_ref[...], sin_ref[...]
        q = _rope128(p_ref[:, _RQ], cosv, sinv)
        k = _rope128(p_ref[:, _RK], cosv, sinv) * scale
        v = p_ref[:, _RV]
        sprev = st_ref[...]
        ds_new = dstate[...]
        o = ry_ref[...]
        mu = jnp.mean(o, axis=-1, keepdims=True)
        oc = o - mu
        rstd = lax.rsqrt(jnp.mean(oc * oc, axis=-1, keepdims=True) + EPS)
        ryn = oc * rstd
        gainv = gain_ref[...]
        gv = p_ref[:, _RG]
        sg = _sigmoid(gv)
        dgt = dg_ref[...]
        dt = dgt * (gv * sg)
        dp_ref[:, _RG] = (dgt * (ryn * gainv) * (sg * (1.0 + gv * (1.0 - sg)))).astype(BF16)
        gpart = jnp.sum(dt * ryn, axis=0, keepdims=True)

        @pl.when(b == 0)
        def _():
            dgain_ref[...] = gpart

        @pl.when(b > 0)
        def _():
            dgain_ref[...] += gpart

        dryn = dt * gainv
        do = rstd * (dryn - jnp.mean(dryn, axis=-1, keepdims=True)
                     - ryn * jnp.mean(dryn * ryn, axis=-1, keepdims=True))
        a = _dot(q, k, NT) * decay
        kdk = k * kd
        qdq = q * qd
        dp_ref[:, _RV] = (_dot(a, do, TN) + _dot(kdk, ds_new, NN)).astype(BF16)
        dp = _dot(do, v, NT) * decay
        dq = _dot(dp, k, NN) + _dot(do, sprev, NT) * qd
        dk = (_dot(dp, q, TN) + _dot(v, ds_new, NT) * kd) * scale
        dstate[...] = ds_new * cdec + _dot(qdq, do, TN)
        dp_ref[:, _RQ] = _rope128_t(dq, cosv, sinv).astype(BF16)
        dp_ref[:, _RK] = _rope128_t(dk, cosv, sinv).astype(BF16)

    rb = lambda b: nb - 1 - b
    return _pallas(
        body, name="ret_bwd", grid=(RH, nb),
        in_specs=[pl.BlockSpec((T, RET_HEAD_COLS), lambda h, b: (rb(b), h)),
                  pl.BlockSpec((T, RET_QK), lambda h, b: (rb(b), 0)),
                  pl.BlockSpec((T, RET_QK), lambda h, b: (rb(b), 0)),
                  pl.BlockSpec((None, 8, LANES), lambda h, b: (h, 0, 0)),
                  pl.BlockSpec((1, RET_V), lambda h, b: (0, h)),
                  pl.BlockSpec((T, RET_V), lambda h, b: (rb(b), h)),
                  pl.BlockSpec((T, RET_V), lambda h, b: (rb(b), h)),
                  pl.BlockSpec((None, None, RET_QK, RET_V), lambda h, b: (h, rb(b), 0, 0)),
                  _ANY],
        out_specs=[pl.BlockSpec((T, RET_HEAD_COLS), lambda h, b: (rb(b), h)),
                   pl.BlockSpec((1, RET_V), lambda h, b: (0, h))],
        out_shape=[jax.ShapeDtypeStruct(dproj.shape, dproj.dtype), jax.ShapeDtypeStruct((1, RH * RET_V), F32)],
        scratch_shapes=[pltpu.VMEM((RET_QK, RET_V), F32)],
        input_output_aliases={8: 0},
        compiler_params=_params(("parallel", "arbitrary")),
    )(proj, cosr, sinr, lgam, gain, ry, dgated, states, dproj)


def _rope_pe(t, c, s1, s2):
    return t * c + pltpu.roll(t, LANES - QK_ROPE // 2, 1) * s1 + pltpu.roll(t, QK_ROPE // 2, 1) * s2


def _rope_pe_t(d, c, s1, s2):
    return d * c + pltpu.roll(d * s1, QK_ROPE // 2, 1) + pltpu.roll(d * s2, LANES - QK_ROPE // 2, 1)


def _attn_prep(q_all, kv_all, kpe, tabs, MH, *, tr=512):
    S = q_all.shape[0]
    tr = _tile(S, tr)
    c_t, s1_t, s2_t = tabs

    def body(q_ref, kv_ref, kpe_ref, c_ref, s1_ref, s2_ref, qf_ref, kf_ref, vb_ref):
        c, s1, s2 = c_ref[...], s1_ref[...], s2_ref[...]
        qf_ref[:, :QK_NOPE] = q_ref[:, :QK_NOPE].astype(BF16)
        qf_ref[:, QK_NOPE:] = _rope_pe(q_ref[:, QK_NOPE:], c, s1, s2).astype(BF16)
        kf_ref[:, :QK_NOPE] = kv_ref[:, :QK_NOPE].astype(BF16)
        kf_ref[:, QK_NOPE:] = _rope_pe(kpe_ref[...], c, s1, s2).astype(BF16)
        vb_ref[...] = kv_ref[:, QK_NOPE:].astype(BF16)

    tab = pl.BlockSpec((tr, LANES), lambda i, h: (i, 0))
    head2 = pl.BlockSpec((tr, 2 * LANES), lambda i, h: (i, h))
    return _pallas(
        body, name="attn_prep", grid=(S // tr, MH),
        in_specs=[head2, head2, tab, tab, tab, tab],
        out_specs=[head2, head2, pl.BlockSpec((tr, LANES), lambda i, h: (i, h))],
        out_shape=[jax.ShapeDtypeStruct((S, MH * 2 * LANES), BF16), jax.ShapeDtypeStruct((S, MH * 2 * LANES), BF16),
                   jax.ShapeDtypeStruct((S, MH * LANES), BF16)],
        compiler_params=_params(("parallel", "parallel")),
    )(q_all, kv_all, kpe, c_t, s1_t, s2_t)


def _chunk_mask(T):
    n = lax.broadcasted_iota(jnp.int32, (T, T), 0)
    m = lax.broadcasted_iota(jnp.int32, (T, T), 1)
    return (m // CHUNK) <= (n // CHUNK)


def _lanes_to(v, width):
    return jnp.tile(v, (1, width // LANES))


def _attn_fwd(qf, kf, vb, MH, *, T):
    S = qf.shape[0]
    nt = S // T
    c2 = (QK_NOPE + QK_ROPE) ** -0.5 * LOG2E

    def body(q_ref, k_ref, v_ref, o_ref, lse_ref, m_sc, l_sc, acc_sc):
        qi = pl.program_id(1)
        m_sc[...] = jnp.full_like(m_sc, NEG)
        l_sc[...] = jnp.zeros_like(l_sc)
        acc_sc[...] = jnp.zeros_like(acc_sc)
        q = q_ref[...]

        def tile(kt, masked):
            rows = pl.ds(pl.multiple_of(kt * T, T), T)
            s = _dot(q, k_ref[rows, :], NT) * c2
            if masked:
                s = jnp.where(_chunk_mask(T), s, NEG)
            m_prev = m_sc[...]
            m_new = jnp.maximum(m_prev, jnp.max(s, axis=-1, keepdims=True))
            alpha = jnp.exp2(m_prev - m_new)
            p = jnp.exp2(s - _lanes_to(m_new, T))
            l_sc[...] = alpha * l_sc[...] + jnp.sum(p, axis=-1, keepdims=True)
            acc_sc[...] = alpha * acc_sc[...] + _dot(p, v_ref[rows, :], NN)
            m_sc[...] = m_new

        def unmasked(kt, carry):
            tile(kt, False)
            return carry

        lax.fori_loop(0, qi, unmasked, 0)
        tile(qi, True)
        l = l_sc[...]
        o_ref[...] = acc_sc[...] / l
        lse_ref[...] = m_sc[...] + jnp.log(l) * LOG2E

    return _pallas(
        body, name="attn_fwd", grid=(MH, nt),
        in_specs=[pl.BlockSpec((T, 2 * LANES), lambda h, i: (i, h)),
                  pl.BlockSpec((S, 2 * LANES), lambda h, i: (0, h)),
                  pl.BlockSpec((S, LANES), lambda h, i: (0, h))],
        out_specs=[pl.BlockSpec((T, LANES), lambda h, i: (i, h)),
                   pl.BlockSpec((None, T, LANES), lambda h, i: (h, i, 0))],
        out_shape=[jax.ShapeDtypeStruct((S, MH * LANES), F32), jax.ShapeDtypeStruct((MH, S, LANES), F32)],
        scratch_shapes=[pltpu.VMEM((T, LANES), F32), pltpu.VMEM((T, LANES), F32), pltpu.VMEM((T, LANES), F32)],
        compiler_params=_params(("parallel", "parallel")),
    )(qf, kf, vb)


def _attn_delta(do, o, MH, *, tr=512):
    S = do.shape[0]
    tr = _tile(S, tr)

    def body(do_ref, o_ref, d_ref, dob_ref):
        dov = do_ref[...]
        d_ref[...] = jnp.broadcast_to(jnp.sum(dov * o_ref[...], axis=-1, keepdims=True), (tr, LANES))
        dob_ref[...] = dov.astype(BF16)

    head = pl.BlockSpec((tr, LANES), lambda i, h: (i, h))
    return _pallas(
        body, name="attn_delta", grid=(S // tr, MH), in_specs=[head, head],
        out_specs=[pl.BlockSpec((None, tr, LANES), lambda i, h: (h, i, 0)), head],
        out_shape=[jax.ShapeDtypeStruct((MH, S, LANES), F32), jax.ShapeDtypeStruct((S, MH * LANES), BF16)],
        compiler_params=_params(("parallel", "parallel")),
    )(do, o)


def _attn_bwd(qf, kf, vb, dob, lse2, delta, MH, *, T):
    S = qf.shape[0]
    nt = S // T
    scale = (QK_NOPE + QK_ROPE) ** -0.5
    c2 = scale * LOG2E

    def body(q_ref, k_ref, v_ref, do_ref, lse_ref, dl_ref, dq_ref, dk_ref, dv_ref, dk_sc, dv_sc):
        kj = pl.program_id(1)

        @pl.when(kj == 0)
        def _():
            dq_ref[...] = jnp.zeros_like(dq_ref)

        dk_sc[...] = jnp.zeros_like(dk_sc)
        dv_sc[...] = jnp.zeros_like(dv_sc)
        k, v = k_ref[...], v_ref[...]

        def tile(qt, masked):
            rows = pl.ds(pl.multiple_of(qt * T, T), T)
            q, dov = q_ref[rows, :], do_ref[rows, :]
            s = _dot(q, k, NT) * c2
            if masked:
                s = jnp.where(_chunk_mask(T), s, NEG)
            p = jnp.exp2(s - _lanes_to(lse_ref[rows, :], T))
            dp = _dot(dov, v, NT)
            ds = p * (dp - _lanes_to(dl_ref[rows, :], T)) * scale
            dv_sc[...] += _dot(p, dov, TN)
            dk_sc[...] += _dot(ds, q, TN)
            dq_ref[rows, :] += _dot(ds, k, NN)

        def unmasked(qt, carry):
            tile(qt, False)
            return carry

        tile(kj, True)
        lax.fori_loop(kj + 1, nt, unmasked, 0)
        dk_ref[...] = dk_sc[...]
        dv_ref[...] = dv_sc[...].astype(BF16)

    stat = pl.BlockSpec((None, S, LANES), lambda h, j: (h, 0, 0))
    return _pallas(
        body, name="attn_bwd", grid=(MH, nt),
        in_specs=[pl.BlockSpec((S, 2 * LANES), lambda h, j: (0, h)),
                  pl.BlockSpec((T, 2 * LANES), lambda h, j: (j, h)),
                  pl.BlockSpec((T, LANES), lambda h, j: (j, h)),
                  pl.BlockSpec((S, LANES), lambda h, j: (0, h)), stat, stat],
        out_specs=[pl.BlockSpec((S, 2 * LANES), lambda h, j: (0, h)),
                   pl.BlockSpec((T, 2 * LANES), lambda h, j: (j, h)),
                   pl.BlockSpec((T, LANES), lambda h, j: (j, h))],
        out_shape=[jax.ShapeDtypeStruct((S, MH * 2 * LANES), F32), jax.ShapeDtypeStruct((S, MH * 2 * LANES), F32),
                   jax.ShapeDtypeStruct((S, MH * LANES), BF16)],
        scratch_shapes=[pltpu.VMEM((T, 2 * LANES), F32), pltpu.VMEM((T, LANES), F32)],
        compiler_params=_params(("parallel", "arbitrary")),
    )(qf, kf, vb, dob, lse2, delta)


def _attn_post(dqf, dkf, dvb, tabs, MH, *, tr=512):
    S = dqf.shape[0]
    tr = _tile(S, tr)
    c_t, s1_t, s2_t = tabs

    def body(dq_ref, dk_ref, dv_ref, c_ref, s1_ref, s2_ref, dqa_ref, dkv_ref, dkpe_ref, acc):
        h = pl.program_id(1)
        c, s1, s2 = c_ref[...], s1_ref[...], s2_ref[...]
        dqa_ref[:, :QK_NOPE] = dq_ref[:, :QK_NOPE].astype(BF16)
        dqa_ref[:, QK_NOPE:] = _rope_pe_t(dq_ref[:, QK_NOPE:], c, s1, s2).astype(BF16)
        dkv_ref[:, :QK_NOPE] = dk_ref[:, :QK_NOPE].astype(BF16)
        dkv_ref[:, QK_NOPE:] = dv_ref[...]

        @pl.when(h == 0)
        def _():
            acc[...] = dk_ref[:, QK_NOPE:]

        @pl.when(h > 0)
        def _():
            acc[...] += dk_ref[:, QK_NOPE:]

        @pl.when(h == MH - 1)
        def _():
            dkpe_ref[...] = _rope_pe_t(acc[...], c, s1, s2).astype(BF16)

    tab = pl.BlockSpec((tr, LANES), lambda i, h: (i, 0))
    head2 = pl.BlockSpec((tr, 2 * LANES), lambda i, h: (i, h))
    return _pallas(
        body, name="attn_post", grid=(S // tr, MH),
        in_specs=[head2, head2, pl.BlockSpec((tr, LANES), lambda i, h: (i, h)), tab, tab, tab],
        out_specs=[head2, head2, tab],
        out_shape=[jax.ShapeDtypeStruct((S, MH * 2 * LANES), BF16)] * 2 + [jax.ShapeDtypeStruct((S, LANES), BF16)],
        scratch_shapes=[pltpu.VMEM((tr, LANES), F32)],
        compiler_params=_params(("parallel", "arbitrary")),
    )(dqf, dkf, dvb, c_t, s1_t, s2_t)


def _block_rows(R, C, block_bytes=2 << 20):
    tr = 8
    while tr * 2 * C * 4 <= block_bytes:
        tr *= 2
    return _tile(R, tr)


def _rows_call(fn, ins, out_dtypes, *, name):
    R, C = ins[0].shape
    tr = _block_rows(R, C)
    n_in = len(ins)

    def body(*refs):
        vals = fn(*[r[...] for r in refs[:n_in]])
        for r, v in zip(refs[n_in:], vals):
            r[...] = v.astype(r.dtype)

    blk = pl.BlockSpec((tr, C), lambda i: (i, 0))
    res = _pallas(
        body, name=name, grid=(R // tr,), in_specs=[blk] * n_in, out_specs=[blk] * len(out_dtypes),
        out_shape=[jax.ShapeDtypeStruct((R, C), d) for d in out_dtypes],
        compiler_params=_params(("parallel",)),
    )(*ins)
    return res


def _adamw_vals(w, g, m, v):
    m = ADAM_B1 * m + (1.0 - ADAM_B1) * g
    v = ADAM_B2 * v + (1.0 - ADAM_B2) * (g * g)
    m_hat = m / (1.0 - ADAM_B1 ** ADAM_STEP)
    v_hat = v / (1.0 - ADAM_B2 ** ADAM_STEP)
    delta = -ADAM_LR * (m_hat / (jnp.sqrt(v_hat) + ADAM_EPS) + ADAM_WD * w)
    return delta, m, v


def _sum_pair(p, theirs, place, *, name):
    _, R, C = p.shape
    R2 = R // 2
    tr = _block_rows(R2, C)
    p4 = p.reshape(N_CHIPS, 2, R2, C)

    def body(place_ref, a_ref, b_ref, o_ref):
        o_ref[...] = (a_ref[...] + b_ref[...]).astype(BF16)

    spec = pltpu.PrefetchScalarGridSpec(
        num_scalar_prefetch=1, grid=(N_CHIPS, R2 // tr),
        in_specs=[pl.BlockSpec((None, None, tr, C), lambda q, i, pr: (q, pr[0], i, 0)),
                  pl.BlockSpec((None, tr, C), lambda q, i, pr: (q, i, 0))],
        out_specs=pl.BlockSpec((None, tr, C), lambda q, i, pr: (q, i, 0)))
    return _pallas(body, name=name, grid_spec=spec, out_shape=jax.ShapeDtypeStruct((N_CHIPS, R2, C), BF16),
                   compiler_params=_params(("parallel", "parallel")))(place, p4, theirs)


def _sum_chips(p, theirs, recv, place, *, name):
    _, R, C = p.shape
    R2 = R // 2
    tr = _block_rows(R2, C)
    p4 = p.reshape(N_CHIPS, 2, R2, C)

    def body(place_ref, a_ref, b_ref, r0_ref, r1_ref, r2_ref, o_ref):
        own = a_ref[...] + b_ref[...]
        o_ref[...] = ((own + r0_ref[...].astype(F32)) + r1_ref[...].astype(F32)) + r2_ref[...].astype(F32)

    def slot(k):
        return pl.BlockSpec((None, tr, C), lambda i, pr: (k, i, 0))

    spec = pltpu.PrefetchScalarGridSpec(
        num_scalar_prefetch=1, grid=(R2 // tr,),
        in_specs=[pl.BlockSpec((None, None, tr, C), lambda i, pr: (pr[1], pr[0], i, 0)),
                  pl.BlockSpec((None, tr, C), lambda i, pr: (pr[1], i, 0)), slot(0), slot(1), slot(2)],
        out_specs=pl.BlockSpec((tr, C), lambda i, pr: (i, 0)))
    return _pallas(body, name=name, grid_spec=spec, out_shape=jax.ShapeDtypeStruct((R2, C), F32),
                   compiler_params=_params(("parallel",)))(place, p4, theirs, recv, recv, recv)


def _me():
    return lax.axis_index("x"), lax.axis_index("y"), lax.axis_index("c")


def _other_chips(x, y):
    return [(1 - x, y), (x, 1 - y), (1 - x, 1 - y)]


def _rcopy(src, dst, ssem, rsem, dev):
    return pltpu.make_async_remote_copy(src_ref=src, dst_ref=dst, send_sem=ssem, recv_sem=rsem,
                                        device_id=dev, device_id_type=MESH)


def _gather_weights(shards):
    n = len(shards)

    def body(*refs):
        ins, outs = refs[:n], refs[n:2 * n]
        lsem, ssem, rsem, fssem, frsem = refs[2 * n:]
        x, y, c = _me()
        j = 2 * x + y
        sib = (x, y, 1 - c)
        chips = _other_chips(x, y)
        local, remote = [], []
        for i in range(n):
            half = ins[i].shape[0] // 2
            rows = pl.ds(c * half, half)
            cp = pltpu.make_async_copy(ins[i], outs[i].at[j], lsem.at[i])
            cp.start()
            local.append(cp)
            for k, (px, py) in enumerate(chips):
                r = _rcopy(ins[i].at[rows], outs[i].at[j, rows], ssem.at[i, k], rsem.at[i, k], (px, py, c))
                r.start()
                remote.append(r)
        for i in range(n):
            half = ins[i].shape[0] // 2
            rows = pl.ds(c * half, half)
            for k, (px, py) in enumerate(chips):
                jp = 2 * px + py
                _rcopy(ins[i].at[rows], outs[i].at[jp, rows], ssem.at[i, k], rsem.at[i, k], (px, py, c)).wait_recv()
                f = _rcopy(outs[i].at[jp, rows], outs[i].at[jp, rows], fssem.at[i, k], frsem.at[i, k], sib)
                f.start()
                remote.append(f)
        for i in range(n):
            half = ins[i].shape[0] // 2
            orow = pl.ds((1 - c) * half, half)
            for k, (px, py) in enumerate(chips):
                jp = 2 * px + py
                _rcopy(outs[i].at[jp, orow], outs[i].at[jp, orow], fssem.at[i, k], frsem.at[i, k], sib).wait_recv()
        for r in remote:
            r.wait_send()
        for cp in local:
            cp.wait()

    return _pallas(
        body, name="gather_weights", in_specs=[_ANY] * n, out_specs=[_ANY] * n,
        out_shape=[jax.ShapeDtypeStruct((N_CHIPS,) + s.shape, s.dtype) for s in shards],
        scratch_shapes=[pltpu.SemaphoreType.DMA((n,)), pltpu.SemaphoreType.DMA((n, 3)),
                        pltpu.SemaphoreType.DMA((n, 3)), pltpu.SemaphoreType.DMA((n, 3)),
                        pltpu.SemaphoreType.DMA((n, 3))],
        compiler_params=pltpu.CompilerParams(has_side_effects=True),
    )(*shards)


def _swap_halves(grads):
    n = len(grads)

    def body(*refs):
        ins, outs = refs[:n], refs[n:2 * n]
        ssem, rsem = refs[2 * n:]
        x, y, c = _me()
        sib = (x, y, 1 - c)
        cps = []
        for i in range(n):
            half = ins[i].shape[1] // 2
            r = _rcopy(ins[i].at[:, pl.ds((1 - c) * half, half)], outs[i], ssem.at[i], rsem.at[i], sib)
            r.start()
            cps.append(r)
        for r in cps:
            r.wait()

    return _pallas(
        body, name="swap_halves", in_specs=[_ANY] * n, out_specs=[_ANY] * n,
        out_shape=[jax.ShapeDtypeStruct((N_CHIPS, g.shape[1] // 2, g.shape[2]), g.dtype) for g in grads],
        scratch_shapes=[pltpu.SemaphoreType.DMA((n,)), pltpu.SemaphoreType.DMA((n,))],
        compiler_params=pltpu.CompilerParams(has_side_effects=True),
    )(*grads)


def _scatter_chips(sums):
    n = len(sums)

    def body(*refs):
        ins, outs = refs[:n], refs[n:2 * n]
        ssem, rsem = refs[2 * n:]
        x, y, c = _me()
        chips = _other_chips(x, y)
        remote = []
        for i in range(n):
            for k, (px, py) in enumerate(chips):
                r = _rcopy(ins[i].at[2 * px + py], outs[i].at[k], ssem.at[i, k], rsem.at[i, k], (px, py, c))
                r.start()
                remote.append(r)
        for r in remote:
            r.wait()

    return _pallas(
        body, name="scatter_chips", in_specs=[_ANY] * n, out_specs=[_ANY] * n,
        out_shape=[jax.ShapeDtypeStruct((3,) + s.shape[1:], s.dtype) for s in sums],
        scratch_shapes=[pltpu.SemaphoreType.DMA((n, 3)), pltpu.SemaphoreType.DMA((n, 3))],
        compiler_params=pltpu.CompilerParams(has_side_effects=True),
    )(*sums)


def _join_halves(halves):
    n = len(halves)

    def body(*refs):
        ins, outs = refs[:n], refs[n:2 * n]
        lsem, ssem, rsem = refs[2 * n:]
        x, y, c = _me()
        sib = (x, y, 1 - c)
        cps = []
        for i in range(n):
            half = ins[i].shape[0]
            rows = pl.ds(c * half, half)
            cp = pltpu.make_async_copy(ins[i], outs[i].at[rows], lsem.at[i])
            cp.start()
            cps.append(cp)
            r = _rcopy(ins[i], outs[i].at[rows], ssem.at[i], rsem.at[i], sib)
            r.start()
            cps.append(r)
        for cp in cps:
            cp.wait()

    return _pallas(
        body, name="join_halves", in_specs=[_ANY] * n, out_specs=[_ANY] * n,
        out_shape=[jax.ShapeDtypeStruct((2 * h.shape[0], h.shape[1]), h.dtype) for h in halves],
        scratch_shapes=[pltpu.SemaphoreType.DMA((n,)), pltpu.SemaphoreType.DMA((n,)), pltpu.SemaphoreType.DMA((n,))],
        compiler_params=pltpu.CompilerParams(has_side_effects=True),
    )(*halves)


def _allreduce_small(vec):
    R = vec.shape[0]

    def body(v_ref, o_ref, buf, ssem, rsem):
        x, y, c = _me()
        me = 4 * x + 2 * y + c
        buf[me] = v_ref[...]
        cps = []
        for k in range(1, 8):
            peer = (x ^ (k >> 2), y ^ ((k >> 1) & 1), c ^ (k & 1))
            r = _rcopy(v_ref, buf.at[me], ssem.at[k - 1], rsem.at[k - 1], peer)
            r.start()
            cps.append(r)
        for k in range(1, 8):
            peer = (x ^ (k >> 2), y ^ ((k >> 1) & 1), c ^ (k & 1))
            pid = 4 * peer[0] + 2 * peer[1] + peer[2]
            _rcopy(v_ref, buf.at[pid], ssem.at[k - 1], rsem.at[k - 1], peer).wait_recv()
        for r in cps:
            r.wait_send()
        tot = buf[0]
        for d in range(1, 8):
            tot = tot + buf[d]
        o_ref[...] = tot

    vm = pl.BlockSpec(memory_space=pltpu.VMEM)
    return _pallas(
        body, name="allreduce_small", in_specs=[vm], out_specs=vm,
        out_shape=jax.ShapeDtypeStruct((R, LANES), F32),
        scratch_shapes=[pltpu.VMEM((8, R, LANES), F32), pltpu.SemaphoreType.DMA((7,)), pltpu.SemaphoreType.DMA((7,))],
        compiler_params=pltpu.CompilerParams(has_side_effects=True),
    )(vec)


def _rope_tables(positions, S):
    pos = positions.reshape(S, 1).astype(F32)
    half = RET_QK // 2
    inv = ROPE_THETA ** (-jnp.arange(half, dtype=F32) / half)
    ang = pos * inv
    cosr = jnp.concatenate([jnp.cos(ang), jnp.cos(ang)], axis=1)
    sinr = jnp.concatenate([-jnp.sin(ang), jnp.sin(ang)], axis=1)
    half = QK_ROPE // 2
    inv = ROPE_THETA ** (-jnp.arange(half, dtype=F32) / half)
    ang = pos * inv
    z = jnp.zeros((S, half), F32)
    c = jnp.concatenate([jnp.cos(ang), jnp.cos(ang), z, z], axis=1)
    s1 = jnp.concatenate([-jnp.sin(ang), z, z, z], axis=1)
    s2 = jnp.concatenate([z, jnp.sin(ang), z, z], axis=1)
    return cosr, sinr, (c, s1, s2)


def _cat_cols(g):
    return jnp.concatenate([g[j] for j in range(N_CHIPS)], axis=1)


def _split_cols(w):
    return jnp.stack(jnp.split(w, N_CHIPS, axis=1))


def _pack_small(vs, rows):
    flat = jnp.concatenate([v.reshape(-1) for v in vs])
    flat = jnp.pad(flat, (0, rows * LANES - flat.shape[0]))
    return flat.reshape(rows, LANES)


def kernel(x, positions, norm_mix_g, w_in, ret_norm_g, w_ret_o, q_a_norm_g, w_q_b, kv_a_norm_g, w_kv_b, w_mla_o, w_out, norm_mlp_g, w_up, w_down, norm_f_g, loss_target, m_norm_mix_g, m_w_in, m_ret_norm_g, m_w_ret_o, m_q_a_norm_g, m_w_q_b, m_kv_a_norm_g, m_w_kv_b, m_w_mla_o, m_w_out, m_norm_mlp_g, m_w_up, m_w_down, m_norm_f_g, v_norm_mix_g, v_w_in, v_ret_norm_g, v_w_ret_o, v_q_a_norm_g, v_w_q_b, v_kv_a_norm_g, v_w_kv_b, v_w_mla_o, v_w_out, v_norm_mlp_g, v_w_up, v_w_down, v_norm_f_g):
    S, D = x.shape[1], x.shape[2]
    RVW = w_ret_o.shape[1] * N_CHIPS
    RH = RVW // RET_V
    RQW = RH * RET_QK
    MVW = w_mla_o.shape[1] * N_CHIPS
    MH = MVW // V_HEAD
    QL, KVL = w_q_b.shape[1], w_kv_b.shape[1]
    T_RET = _tile(S, 256)
    T_ATT = _tile(S, 512)

    xs = x.reshape(S, D)
    tgt = loss_target.reshape(S, D)
    cosr, sinr, pe_tabs = _rope_tables(positions, S)
    lgam = jnp.log(1.0 - 2.0 ** (-5.0 - jnp.arange(RH, dtype=F32)))
    lgam = jnp.broadcast_to(lgam[:, None, None], (RH, 8, LANES))

    big = ("w_in", "w_ret_o", "w_q_b", "w_kv_b", "w_mla_o", "w_out", "w_up", "w_down")
    w_sh = dict(w_in=w_in[0], w_ret_o=w_ret_o[0], w_q_b=w_q_b[0], w_kv_b=w_kv_b[0], w_mla_o=w_mla_o[0],
                w_out=w_out[0], w_up=w_up[0], w_down=w_down[0])
    m_sh = dict(w_in=m_w_in[0], w_ret_o=m_w_ret_o[0], w_q_b=m_w_q_b[0], w_kv_b=m_w_kv_b[0], w_mla_o=m_w_mla_o[0],
                w_out=m_w_out[0], w_up=m_w_up[0], w_down=m_w_down[0])
    v_sh = dict(w_in=v_w_in[0], w_ret_o=v_w_ret_o[0], w_q_b=v_w_q_b[0], w_kv_b=v_w_kv_b[0], w_mla_o=v_w_mla_o[0],
                w_out=v_w_out[0], w_up=v_w_up[0], w_down=v_w_down[0])
    col_sharded = ("w_in", "w_q_b", "w_kv_b", "w_up")
    gathered = _gather_weights([w_sh[k].astype(BF16) for k in big])
    full = {}
    for k, g in zip(big, gathered):
        full[k] = _cat_cols(g) if k in col_sharded else g.reshape(-1, g.shape[2])

    o_rq, o_rk, o_rv, o_rg = 0, RQW, 2 * RQW, 2 * RQW + RVW
    o_cq = 2 * RQW + 2 * RVW
    o_ckv, o_kpe = o_cq + QL, o_cq + QL + KVL
    o_gr = o_kpe + QK_ROPE
    o_gm = o_gr + D
    n_ret = RH * RET_HEAD_COLS
    off_gret, off_gmla, off_cq, off_ckv = n_ret, n_ret + D, n_ret + 2 * D, n_ret + 2 * D + QL
    wi = full["w_in"]
    ret_cols = jnp.concatenate([wi[:, o_rq:o_rk].reshape(D, RH, RET_QK), wi[:, o_rk:o_rv].reshape(D, RH, RET_QK),
                                wi[:, o_rv:o_rg].reshape(D, RH, RET_V), wi[:, o_rg:o_cq].reshape(D, RH, RET_V)],
                               axis=2).reshape(D, n_ret)
    wa = jnp.concatenate([ret_cols, wi[:, o_gr:o_gm], wi[:, o_gm:], wi[:, o_cq:o_ckv], wi[:, o_ckv:o_kpe]], axis=1)
    wkpe = jnp.pad(wi[:, o_kpe:o_gr], ((0, 0), (0, LANES - QK_ROPE)))
    wq = jnp.pad(full["w_q_b"].reshape(QL, MH, QK_NOPE + QK_ROPE),
                 ((0, 0), (0, 0), (0, LANES - QK_ROPE))).reshape(QL, MH * 2 * LANES)
    wkv = full["w_kv_b"]

    u, rstd0 = _rmsnorm_fwd(xs, norm_mix_g, name="norm_mix")
    proj = _mm(u, wa, mode="nn", outs=[F32], name="in_proj")
    kpe = _mm(u, wkpe, mode="nn", outs=[F32], name="kpe_proj")
    ry, gated, states = _ret_fwd(proj, cosr, sinr, lgam, ret_norm_g, RH, T=T_RET)
    y_ret = _mm(gated, full["w_ret_o"], mode="nn", outs=[F32], name="ret_o")
    cqn, rstd_q = _rmsnorm_fwd(proj, q_a_norm_g, name="norm_q", width=QL, col=off_cq // QL)
    ckvn, rstd_kv = _rmsnorm_fwd(proj, kv_a_norm_g, name="norm_kv", width=KVL, col=off_ckv // KVL)
    q_all = _mm(cqn, wq, mode="nn", outs=[F32], name="q_proj")
    kv_all = _mm(ckvn, wkv, mode="nn", outs=[F32], name="kv_proj")
    qf, kf, vb = _attn_prep(q_all, kv_all, kpe, pe_tabs, MH)
    my, lse2 = _attn_fwd(qf, kf, vb, MH, T=T_ATT)
    y_mla = _mm(my, full["w_mla_o"], mode="nn", outs=[F32], name="mla_o")
    merged = _merge_fwd(proj, y_ret, y_mla, D, off_gret, off_gmla)
    h1 = _mm(merged, full["w_out"], mode="nn", outs=[F32], name="out_proj",
             epi=lambda acc, r: (acc + r,), extras=(xs,))
    n1, rstd1 = _rmsnorm_fwd(h1, norm_mlp_g, name="norm_mlp")

    def up_epi(acc):
        r = jnp.maximum(acc, 0.0)
        return acc, r * r

    z, act = _mm(n1, full["w_up"], mode="nn", outs=[F32, BF16], name="up_proj", epi=up_epi)
    h2 = _mm(act, full["w_down"], mode="nn", outs=[F32], name="down_proj",
             epi=lambda acc, r: (acc + r,), extras=(h1,))
    loss11, dh2, g_norm_f = _final_loss(h2, norm_f_g.reshape(1, D), tgt)

    dz = _mm(dh2, full["w_down"], mode="nt", outs=[BF16], name="down_bwd_x",
             epi=lambda acc, zz: (acc * (2.0 * jnp.maximum(zz, 0.0)),), extras=(z,))
    g_w_down = _mm(act, dh2, mode="tn", outs=[F32], name="down_bwd_w")
    dn1 = _mm(dz, full["w_up"], mode="nt", outs=[F32], name="up_bwd_x")
    g_w_up = _mm(n1, dz, mode="tn", outs=[F32], name="up_bwd_w")
    dh1, g_norm_mlp = _rmsnorm_bwd(dn1, h1, rstd1, norm_mlp_g, name="norm_mlp_bwd", res=dh2)
    dmerged = _mm(dh1, full["w_out"], mode="nt", outs=[F32], name="out_bwd_x")
    g_w_out = _mm(merged, dh1, mode="tn", outs=[F32], name="out_bwd_w")
    dproj, dy_ret, dy_mla = _merge_bwd(dmerged, proj, y_ret, y_mla, D, off_gret)
    dgated = _mm(dy_ret, full["w_ret_o"], mode="nt", outs=[F32], name="ret_o_bwd_x")
    g_w_ret_o = _mm(gated, dy_ret, mode="tn", outs=[F32], name="ret_o_bwd_w")
    dproj, g_ret_norm = _ret_bwd(proj, cosr, sinr, lgam, ret_norm_g, ry, dgated, states, dproj, RH, T=T_RET)
    dmy = _mm(dy_mla, full["w_mla_o"], mode="nt", outs=[F32], name="mla_o_bwd_x")
    g_w_mla_o = _mm(my, dy_mla, mode="tn", outs=[F32], name="mla_o_bwd_w")
    delta, dob = _attn_delta(dmy, my, MH)
    dqf, dkf, dvb = _attn_bwd(qf, kf, vb, dob, lse2, delta, MH, T=T_ATT)
    dq_all, dkv_all, dkpe = _attn_post(dqf, dkf, dvb, pe_tabs, MH)
    dcqn = _mm(dq_all, wq, mode="nt", outs=[F32], name="q_bwd_x")
    g_wq = _mm(cqn, dq_all, mode="tn", outs=[F32], name="q_bwd_w")
    dckvn = _mm(dkv_all, wkv, mode="nt", outs=[F32], name="kv_bwd_x")
    g_wkv = _mm(ckvn, dkv_all, mode="tn", outs=[F32], name="kv_bwd_w")
    dproj, g_q_a = _rmsnorm_bwd(dcqn, proj, rstd_q, q_a_norm_g, name="norm_q_bwd", into=(dproj, off_cq // QL),
                                width=QL, col=off_cq // QL)
    dproj, g_kv_a = _rmsnorm_bwd(dckvn, proj, rstd_kv, kv_a_norm_g, name="norm_kv_bwd", into=(dproj, off_ckv // KVL),
                                 width=KVL, col=off_ckv // KVL)
    du_a = _mm(dproj, wa, mode="nt", outs=[F32], name="in_bwd_x", tk=1024)
    du = _mm(dkpe, wkpe, mode="nt", outs=[F32], name="kpe_bwd_x", epi=lambda acc, r: (acc + r,), extras=(du_a,))
    g_wa = _mm(u, dproj, mode="tn", outs=[F32], name="in_bwd_w")
    g_wkpe = _mm(u, dkpe, mode="tn", outs=[F32], name="kpe_bwd_w")
    dx, g_norm_mix = _rmsnorm_bwd(du, xs, rstd0, norm_mix_g, name="norm_mix_bwd", res=dh1)

    gr = g_wa[:, :n_ret].reshape(D, RH, RET_HEAD_COLS)
    g_w_in = jnp.concatenate([gr[:, :, _RQ].reshape(D, RQW), gr[:, :, _RK].reshape(D, RQW),
                              gr[:, :, _RV].reshape(D, RVW), gr[:, :, _RG].reshape(D, RVW),
                              g_wa[:, off_cq:], g_wkpe[:, :QK_ROPE], g_wa[:, off_gret:off_cq]], axis=1)
    gq = g_wq.reshape(QL, MH, 2 * LANES)[:, :, :QK_NOPE + QK_ROPE].reshape(QL, MH * (QK_NOPE + QK_ROPE))
    g_full = dict(w_in=g_w_in, w_ret_o=g_w_ret_o, w_q_b=gq, w_kv_b=g_wkv, w_mla_o=g_w_mla_o, w_out=g_w_out,
                  w_up=g_w_up, w_down=g_w_down)
    pieces = []
    for k in big:
        g = g_full[k]
        pieces.append(_split_cols(g) if k in col_sharded else g.reshape(N_CHIPS, g.shape[0] // N_CHIPS, g.shape[1]))

    place = jnp.stack([lax.axis_index("c"), 2 * lax.axis_index("x") + lax.axis_index("y")]).astype(jnp.int32)
    theirs = _swap_halves(pieces)
    sums = [_sum_pair(p, t, place, name="sum_pair_" + k) for k, p, t in zip(big, pieces, theirs)]
    recv = _scatter_chips(sums)
    halves = [_sum_chips(p, t, r, place, name="sum_chips_" + k) for k, p, t, r in zip(big, pieces, theirs, recv)]
    g_shard = dict(zip(big, _join_halves(halves)))

    small = ("norm_mix_g", "ret_norm_g", "q_a_norm_g", "kv_a_norm_g", "norm_mlp_g", "norm_f_g")
    g_small = [g_norm_mix, g_ret_norm, g_q_a, g_kv_a, g_norm_mlp, g_norm_f]
    sizes = [int(v.size) for v in g_small]
    n_small = sum(sizes) + LANES
    rows = -(-n_small // (8 * LANES)) * 8
    packed = _pack_small(g_small + [jnp.broadcast_to(loss11.reshape(1), (LANES,))], rows)
    red = _allreduce_small(packed).reshape(-1)
    loss = red[sum(sizes)]
    w_small = [norm_mix_g, ret_norm_g, q_a_norm_g, kv_a_norm_g, norm_mlp_g, norm_f_g]
    m_small = [m_norm_mix_g, m_ret_norm_g, m_q_a_norm_g, m_kv_a_norm_g, m_norm_mlp_g, m_norm_f_g]
    v_small = [v_norm_mix_g, v_ret_norm_g, v_q_a_norm_g, v_kv_a_norm_g, v_norm_mlp_g, v_norm_f_g]
    g_pk = red[:rows * LANES].reshape(rows, LANES)
    d_pk, m_pk, v_pk = _rows_call(_adamw_vals, [_pack_small(w_small, rows), g_pk, _pack_small(m_small, rows),
                                               _pack_small(v_small, rows)], [F32, F32, F32], name="adamw_small")
    out_g, out_d, out_m, out_v = {}, {}, {}, {}
    off = 0
    for k, wv, sz in zip(small, w_small, sizes):
        for dst, src in ((out_g, g_pk), (out_d, d_pk), (out_m, m_pk), (out_v, v_pk)):
            dst[k] = src.reshape(-1)[off:off + sz].reshape(wv.shape)
        off += sz

    for k in big:
        d_, m_, v_ = _rows_call(_adamw_vals, [w_sh[k], g_shard[k], m_sh[k], v_sh[k]], [F32, F32, F32],
                                name="adamw_" + k)
        out_g[k] = g_shard[k][None]
        out_d[k], out_m[k], out_v[k] = d_[None], m_[None], v_[None]

    order = ("norm_mix_g", "w_in", "ret_norm_g", "w_ret_o", "q_a_norm_g", "w_q_b", "kv_a_norm_g", "w_kv_b",
             "w_mla_o", "w_out", "norm_mlp_g", "w_up", "w_down", "norm_f_g")
    return (loss, dx.reshape(1, S, D), *[out_g[k] for k in order], *[out_d[k] for k in order],
            *[out_m[k] for k in order], *[out_v[k] for k in order])
```

```python
import math

import jax
import jax.numpy as jnp
from jax import lax
from jax.experimental import pallas as pl
from jax.experimental.pallas import tpu as pltpu

F32 = jnp.float32
BF16 = jnp.bfloat16

EPS = 1e-6
ROPE_THETA = 10000.0
CHUNK = 64
RET_QK = 128
RET_V = 256
RET_HEAD_COLS = 2 * RET_QK + 2 * RET_V
QK_NOPE = 128
QK_ROPE = 64
V_HEAD = 128
LANES = 128
LOG2E = math.log2(math.e)

ADAM_LR = 0.001
ADAM_B1 = 0.9
ADAM_B2 = 0.999
ADAM_EPS = 1e-08
ADAM_WD = 0.01
ADAM_STEP = 10

N_CHIPS = 4
VMEM_LIMIT = 56 * 1024 * 1024
MESH = pl.DeviceIdType.MESH
NEG = -1e30


def _pallas(body, **kw):
    return pl.pallas_call(body, **kw)


def _params(sem=None):
    return pltpu.CompilerParams(dimension_semantics=sem, vmem_limit_bytes=VMEM_LIMIT)


def _tile(n, want):
    t = min(n, want)
    while n % t:
        t //= 2
    return t


_ANY = pl.BlockSpec(memory_space=pl.ANY)


def _mm(a, b, *, mode, outs, name, epi=None, extras=(), tm=1024, tn=1024, tk=2048):
    if mode == "nn":
        (M, K), (_, N) = a.shape, b.shape
    elif mode == "nt":
        (M, K), (N, _) = a.shape, b.shape
    else:
        (K, M), (_, N) = a.shape, b.shape
    tm, tn, tk = _tile(M, tm), _tile(N, tn), _tile(K, tk)
    nk = K // tk
    if mode == "nn":
        a_spec = pl.BlockSpec((tm, tk), lambda i, j, k: (i, k))
        b_spec = pl.BlockSpec((tk, tn), lambda i, j, k: (k, j))
        dims = (((1,), (0,)), ((), ()))
    elif mode == "nt":
        a_spec = pl.BlockSpec((tm, tk), lambda i, j, k: (i, k))
        b_spec = pl.BlockSpec((tn, tk), lambda i, j, k: (j, k))
        dims = (((1,), (1,)), ((), ()))
    else:
        a_spec = pl.BlockSpec((tk, tm), lambda i, j, k: (k, i))
        b_spec = pl.BlockSpec((tk, tn), lambda i, j, k: (k, j))
        dims = (((0,), (0,)), ((), ()))
    o_spec = pl.BlockSpec((tm, tn), lambda i, j, k: (i, j))
    n_ex, n_out = len(extras), len(outs)
    if epi is None:
        epi = lambda acc: (acc,)

    def body(*refs):
        a_ref, b_ref = refs[0], refs[1]
        ex_refs = refs[2:2 + n_ex]
        o_refs = refs[2 + n_ex:2 + n_ex + n_out]
        part = lax.dot_general(a_ref[...].astype(BF16), b_ref[...].astype(BF16), dims,
                               preferred_element_type=F32)

        def finish(acc):
            vals = epi(acc, *[r[...] for r in ex_refs])
            for r, v in zip(o_refs, vals):
                r[...] = v.astype(r.dtype)

        if nk == 1:
            finish(part)
        else:
            acc_ref = refs[-1]
            k = pl.program_id(2)

            @pl.when(k == 0)
            def _():
                acc_ref[...] = part

            @pl.when(k > 0)
            def _():
                acc_ref[...] += part

            @pl.when(k == nk - 1)
            def _():
                finish(acc_ref[...])

    res = _pallas(
        body, name=name, grid=(M // tm, N // tn, nk),
        in_specs=[a_spec, b_spec] + [o_spec] * n_ex,
        out_specs=[o_spec] * n_out,
        out_shape=[jax.ShapeDtypeStruct((M, N), d) for d in outs],
        scratch_shapes=[pltpu.VMEM((tm, tn), F32)] if nk > 1 else [],
        compiler_params=_params(("parallel", "parallel", "arbitrary")),
    )(a, b, *extras)
    return res[0] if n_out == 1 else res


def _rmsnorm_fwd(x, g, *, name, width=None, col=0, tr=256):
    S = x.shape[0]
    W = x.shape[1] if width is None else width
    tr = _tile(S, tr)

    def body(x_ref, g_ref, y_ref, r_ref):
        xv = x_ref[...]
        rstd = lax.rsqrt(jnp.mean(xv * xv, axis=-1, keepdims=True) + EPS)
        y_ref[...] = (xv * rstd * g_ref[...]).astype(BF16)
        r_ref[...] = rstd

    return _pallas(
        body, name=name, grid=(S // tr,),
        in_specs=[pl.BlockSpec((tr, W), lambda i: (i, col)), pl.BlockSpec((1, W), lambda i: (0, 0))],
        out_specs=[pl.BlockSpec((tr, W), lambda i: (i, 0)), pl.BlockSpec((tr, 1), lambda i: (i, 0))],
        out_shape=[jax.ShapeDtypeStruct((S, W), BF16), jax.ShapeDtypeStruct((S, 1), F32)],
        compiler_params=_params(("parallel",)),
    )(x, g)


def _rmsnorm_bwd(dy, x, rstd, g, *, name, res=None, into=None, width=None, col=0, tr=256):
    S = x.shape[0]
    W = x.shape[1] if width is None else width
    tr = _tile(S, tr)
    has_res = res is not None

    def body(*refs):
        dy_ref, x_ref, r_ref, g_ref = refs[:4]
        dx_ref, dg_ref = refs[-2], refs[-1]
        rstd_v = r_ref[...]
        xhat = x_ref[...] * rstd_v
        dyv = dy_ref[...].astype(F32)
        dyg = dyv * g_ref[...]
        dx = rstd_v * (dyg - xhat * jnp.mean(dyg * xhat, axis=-1, keepdims=True))
        if has_res:
            dx = dx + refs[4][...]
        dx_ref[...] = dx.astype(dx_ref.dtype)
        part = jnp.sum(dyv * xhat, axis=0, keepdims=True)

        @pl.when(pl.program_id(0) == 0)
        def _():
            dg_ref[...] = part

        @pl.when(pl.program_id(0) > 0)
        def _():
            dg_ref[...] += part

    row = pl.BlockSpec((tr, W), lambda i: (i, 0))
    ins = [dy, x, rstd, g] + ([res] if has_res else [])
    in_specs = [row, pl.BlockSpec((tr, W), lambda i: (i, col)), pl.BlockSpec((tr, 1), lambda i: (i, 0)),
                pl.BlockSpec((1, W), lambda i: (0, 0))] + ([row] if has_res else [])
    if into is None:
        dx_spec, dx_shape, alias = row, jax.ShapeDtypeStruct((S, W), F32), {}
    else:
        buf, col_out = into
        ins.append(buf)
        in_specs.append(_ANY)
        dx_spec = pl.BlockSpec((tr, W), lambda i: (i, col_out))
        dx_shape = jax.ShapeDtypeStruct(buf.shape, buf.dtype)
        alias = {len(ins) - 1: 0}
    return _pallas(
        body, name=name, grid=(S // tr,), in_specs=in_specs,
        out_specs=[dx_spec, pl.BlockSpec((1, W), lambda i: (0, 0))],
        out_shape=[dx_shape, jax.ShapeDtypeStruct((1, W), F32)],
        input_output_aliases=alias,
        compiler_params=_params(("arbitrary",)),
    )(*ins)


def _final_loss(h2, g, target, *, tr=256):
    S, D = h2.shape
    tr = _tile(S, tr)

    def body(h_ref, g_ref, t_ref, loss_ref, dh_ref, dg_ref):
        hv = h_ref[...]
        rstd = lax.rsqrt(jnp.mean(hv * hv, axis=-1, keepdims=True) + EPS)
        xhat = hv * rstd
        e = xhat * g_ref[...] - t_ref[...]
        lpart = (0.5 / D) * jnp.sum(jnp.sum(e * e, axis=-1, keepdims=True), axis=0, keepdims=True)
        dy = e * (1.0 / D)
        dyg = dy * g_ref[...]
        dh_ref[...] = rstd * (dyg - xhat * jnp.mean(dyg * xhat, axis=-1, keepdims=True))
        gpart = jnp.sum(dy * xhat, axis=0, keepdims=True)

        @pl.when(pl.program_id(0) == 0)
        def _():
            loss_ref[...] = lpart
            dg_ref[...] = gpart

        @pl.when(pl.program_id(0) > 0)
        def _():
            loss_ref[...] += lpart
            dg_ref[...] += gpart

    row = pl.BlockSpec((tr, D), lambda i: (i, 0))
    vec = pl.BlockSpec((1, D), lambda i: (0, 0))
    return _pallas(
        body, name="final_loss", grid=(S // tr,), in_specs=[row, vec, row],
        out_specs=[pl.BlockSpec((1, 1), lambda i: (0, 0)), row, vec],
        out_shape=[jax.ShapeDtypeStruct((1, 1), F32), jax.ShapeDtypeStruct((S, D), F32),
                   jax.ShapeDtypeStruct((1, D), F32)],
        compiler_params=_params(("arbitrary",)),
    )(h2, g, target)


def _sigmoid(v):
    return 1.0 / (1.0 + jnp.exp(-v))


def _merge_fwd(proj, y_ret, y_mla, D, off_gret, off_gmla, *, tr=256, tc=1024):
    S = y_ret.shape[0]
    tr, tc = _tile(S, tr), _tile(D, tc)
    b_ret, b_mla = off_gret // tc, off_gmla // tc

    def body(gr_ref, gm_ref, yr_ref, ym_ref, o_ref):
        o_ref[...] = (_sigmoid(gr_ref[...]) * yr_ref[...] + _sigmoid(gm_ref[...]) * ym_ref[...]).astype(BF16)

    blk = pl.BlockSpec((tr, tc), lambda i, j: (i, j))
    return _pallas(
        body, name="merge_fwd", grid=(S // tr, D // tc),
        in_specs=[pl.BlockSpec((tr, tc), lambda i, j: (i, b_ret + j)),
                  pl.BlockSpec((tr, tc), lambda i, j: (i, b_mla + j)), blk, blk],
        out_specs=blk, out_shape=jax.ShapeDtypeStruct((S, D), BF16),
        compiler_params=_params(("parallel", "parallel")),
    )(proj, proj, y_ret, y_mla)


def _merge_bwd(dmerged, proj, y_ret, y_mla, D, off_gret, *, tr=256):
    S = y_ret.shape[0]
    tr = _tile(S, tr)
    b0 = off_gret // D

    def body(dm_ref, g_ref, yr_ref, ym_ref, dp_ref, dyr_ref, dym_ref):
        dm = dm_ref[...]
        sg = _sigmoid(g_ref[...])

        @pl.when(pl.program_id(1) == 0)
        def _():
            dyr_ref[...] = (dm * sg).astype(BF16)
            dp_ref[...] = (dm * yr_ref[...] * sg * (1.0 - sg)).astype(BF16)

        @pl.when(pl.program_id(1) == 1)
        def _():
            dym_ref[...] = (dm * sg).astype(BF16)
            dp_ref[...] = (dm * ym_ref[...] * sg * (1.0 - sg)).astype(BF16)

    blk = pl.BlockSpec((tr, D), lambda i, j: (i, 0))
    return _pallas(
        body, name="merge_bwd", grid=(S // tr, 2),
        in_specs=[blk, pl.BlockSpec((tr, D), lambda i, j: (i, b0 + j)), blk, blk],
        out_specs=[pl.BlockSpec((tr, D), lambda i, j: (i, b0 + j)), blk, blk],
        out_shape=[jax.ShapeDtypeStruct(proj.shape, BF16), jax.ShapeDtypeStruct((S, D), BF16),
                   jax.ShapeDtypeStruct((S, D), BF16)],
        compiler_params=_params(("parallel", "arbitrary")),
    )(dmerged, proj, y_ret, y_mla)


def _rope128(t, cos_full, sin_signed):
    return t * cos_full + pltpu.roll(t, RET_QK // 2, 1) * sin_signed


def _rope128_t(d, cos_full, sin_signed):
    return d * cos_full + pltpu.roll(d * sin_signed, RET_QK // 2, 1)


def _ret_consts(lg, T):
    pos = lax.broadcasted_iota(jnp.int32, (T, 1), 0).astype(F32)
    qd = jnp.exp(lg * (pos + 1.0))
    kd = jnp.exp(lg * (T - 1.0 - pos))
    n = lax.broadcasted_iota(jnp.int32, (T, T), 0)
    m = lax.broadcasted_iota(jnp.int32, (T, T), 1)
    vis = (m // CHUNK) <= (n // CHUNK)
    dist = jnp.abs(n - m).astype(F32)
    decay = jnp.where(vis, jnp.exp(lg * dist), 0.0)
    cdec = jnp.exp(lg * float(T))
    return qd, kd, decay, cdec


def _dot(a, b, dims):
    return lax.dot_general(a.astype(BF16), b.astype(BF16), (dims, ((), ())), preferred_element_type=F32)


NN = ((1,), (0,))
NT = ((1,), (1,))
TN = ((0,), (0,))
_RQ = slice(0, RET_QK)
_RK = slice(RET_QK, 2 * RET_QK)
_RV = slice(2 * RET_QK, 2 * RET_QK + RET_V)
_RG = slice(2 * RET_QK + RET_V, RET_HEAD_COLS)


def _ret_fwd(proj, cosr, sinr, lgam, gain, RH, *, T):
    S = proj.shape[0]
    nb = S // T
    scale = RET_QK ** -0.5

    def body(p_ref, cos_ref, sin_ref, lg_ref, gain_ref, ry_ref, gated_ref, st_ref, state):
        b = pl.program_id(1)

        @pl.when(b == 0)
        def _():
            state[...] = jnp.zeros_like(state)

        lg = lg_ref[0:1, 0:1]
        qd, kd, decay, cdec = _ret_consts(lg, T)
        cosv, sinv = cos_ref[...], sin_ref[...]
        q = _rope128(p_ref[:, _RQ], cosv, sinv)
        k = _rope128(p_ref[:, _RK], cosv, sinv) * scale
        v = p_ref[:, _RV]
        sprev = state[...]
        st_ref[...] = sprev
        a = _dot(q, k, NT) * decay
        o = _dot(a, v, NN) + _dot(q * qd, sprev, NN)
        state[...] = sprev * cdec + _dot(k * kd, v, TN)
        ry_ref[...] = o
        mu = jnp.mean(o, axis=-1, keepdims=True)
        oc = o - mu
        var = jnp.mean(oc * oc, axis=-1, keepdims=True)
        t = oc * lax.rsqrt(var + EPS) * gain_ref[...]
        gv = p_ref[:, _RG]
        gated_ref[...] = (t * (gv * _sigmoid(gv))).astype(BF16)

    return _pallas(
        body, name="ret_fwd", grid=(RH, nb),
        in_specs=[pl.BlockSpec((T, RET_HEAD_COLS), lambda h, b: (b, h)),
                  pl.BlockSpec((T, RET_QK), lambda h, b: (b, 0)),
                  pl.BlockSpec((T, RET_QK), lambda h, b: (b, 0)),
                  pl.BlockSpec((None, 8, LANES), lambda h, b: (h, 0, 0)),
                  pl.BlockSpec((1, RET_V), lambda h, b: (0, h))],
        out_specs=[pl.BlockSpec((T, RET_V), lambda h, b: (b, h)),
                   pl.BlockSpec((T, RET_V), lambda h, b: (b, h)),
                   pl.BlockSpec((None, None, RET_QK, RET_V), lambda h, b: (h, b, 0, 0))],
        out_shape=[jax.ShapeDtypeStruct((S, RH * RET_V), F32), jax.ShapeDtypeStruct((S, RH * RET_V), BF16),
                   jax.ShapeDtypeStruct((RH, nb, RET_QK, RET_V), F32)],
        scratch_shapes=[pltpu.VMEM((RET_QK, RET_V), F32)],
        compiler_params=_params(("parallel", "arbitrary")),
    )(proj, cosr, sinr, lgam, gain)


def _ret_bwd(proj, cosr, sinr, lgam, gain, ry, dgated, states, dproj, RH, *, T):
    S = proj.shape[0]
    nb = S // T
    scale = RET_QK ** -0.5

    def body(p_ref, cos_ref, sin_ref, lg_ref, gain_ref, ry_ref, dg_ref, st_ref, _, dp_ref, dgain_ref, dstate):
        b = pl.program_id(1)

        @pl.when(b == 0)
        def _():
            dstate[...] = jnp.zeros_like(dstate)

        lg = lg_ref[0:1, 0:1]
        qd, kd, decay, cdec = _ret_consts(lg, T)
        cosv, sinv = cos_ref[...], sin_ref[...]
        q = _rope128(p_ref[:, _RQ], cosv, sinv)
        k = _rope128(p_ref[:, _RK], cosv, sinv) * scale
        v = p_ref[:, _RV]
        sprev = st_ref[...]
        ds_new = dstate[...]
        o = ry_ref[...]
        mu = jnp.mean(o, axis=-1, keepdims=True)
        oc = o - mu
        rstd = lax.rsqrt(jnp.mean(oc * oc, axis=-1, keepdims=True) + EPS)
        ryn = oc * rstd
        gainv = gain_ref[...]
        gv = p_ref[:, _RG]
        sg = _sigmoid(gv)
        dgt = dg_ref[...]
        dt = dgt * (gv * sg)
        dp_ref[:, _RG] = (dgt * (ryn * gainv) * (sg * (1.0 + gv * (1.0 - sg)))).astype(BF16)
        gpart = jnp.sum(dt * ryn, axis=0, keepdims=True)

        @pl.when(b == 0)
        def _():
            dgain_ref[...] = gpart

        @pl.when(b > 0)
        def _():
            dgain_ref[...] += gpart

        dryn = dt * gainv
        do = rstd * (dryn - jnp.mean(dryn, axis=-1, keepdims=True)
                     - ryn * jnp.mean(dryn * ryn, axis=-1, keepdims=True))
        a = _dot(q, k, NT) * decay
        kdk = k * kd
        qdq = q * qd
        dp_ref[:, _RV] = (_dot(a, do, TN) + _dot(kdk, ds_new, NN)).astype(BF16)
        dp = _dot(do, v, NT) * decay
        dq = _dot(dp, k, NN) + _dot(do, sprev, NT) * qd
        dk = (_dot(dp, q, TN) + _dot(v, ds_new, NT) * kd) * scale
        dstate[...] = ds_new * cdec + _dot(qdq, do, TN)
        dp_ref[:, _RQ] = _rope128_t(dq, cosv, sinv).astype(BF16)
        dp_ref[:, _RK] = _rope128_t(dk, cosv, sinv).astype(BF16)

    rb = lambda b: nb - 1 - b
    return _pallas(
        body, name="ret_bwd", grid=(RH, nb),
        in_specs=[pl.BlockSpec((T, RET_HEAD_COLS), lambda h, b: (rb(b), h)),
                  pl.BlockSpec((T, RET_QK), lambda h, b: (rb(b), 0)),
                  pl.BlockSpec((T, RET_QK), lambda h, b: (rb(b), 0)),
                  pl.BlockSpec((None, 8, LANES), lambda h, b: (h, 0, 0)),
                  pl.BlockSpec((1, RET_V), lambda h, b: (0, h)),
                  pl.BlockSpec((T, RET_V), lambda h, b: (rb(b), h)),
                  pl.BlockSpec((T, RET_V), lambda h, b: (rb(b), h)),
                  pl.BlockSpec((None, None, RET_QK, RET_V), lambda h, b: (h, rb(b), 0, 0)),
                  _ANY],
        out_specs=[pl.BlockSpec((T, RET_HEAD_COLS), lambda h, b: (rb(b), h)),
                   pl.BlockSpec((1, RET_V), lambda h, b: (0, h))],
        out_shape=[jax.ShapeDtypeStruct(dproj.shape, dproj.dtype), jax.ShapeDtypeStruct((1, RH * RET_V), F32)],
        scratch_shapes=[pltpu.VMEM((RET_QK, RET_V), F32)],
        input_output_aliases={8: 0},
        compiler_params=_params(("parallel", "arbitrary")),
    )(proj, cosr, sinr, lgam, gain, ry, dgated, states, dproj)


def _rope_pe(t, c, s1, s2):
    return t * c + pltpu.roll(t, LANES - QK_ROPE // 2, 1) * s1 + pltpu.roll(t, QK_ROPE // 2, 1) * s2


def _rope_pe_t(d, c, s1, s2):
    return d * c + pltpu.roll(d * s1, QK_ROPE // 2, 1) + pltpu.roll(d * s2, LANES - QK_ROPE // 2, 1)


def _attn_prep(q_all, kv_all, kpe, tabs, MH, *, tr=512):
    S = q_all.shape[0]
    tr = _tile(S, tr)
    c_t, s1_t, s2_t = tabs

    def body(q_ref, kv_ref, kpe_ref, c_ref, s1_ref, s2_ref, qf_ref, kf_ref, vb_ref):
        c, s1, s2 = c_ref[...], s1_ref[...], s2_ref[...]
        qf_ref[:, :QK_NOPE] = q_ref[:, :QK_NOPE].astype(BF16)
        qf_ref[:, QK_NOPE:] = _rope_pe(q_ref[:, QK_NOPE:], c, s1, s2).astype(BF16)
        kf_ref[:, :QK_NOPE] = kv_ref[:, :QK_NOPE].astype(BF16)
        kf_ref[:, QK_NOPE:] = _rope_pe(kpe_ref[...], c, s1, s2).astype(BF16)
        vb_ref[...] = kv_ref[:, QK_NOPE:].astype(BF16)

    tab = pl.BlockSpec((tr, LANES), lambda i, h: (i, 0))
    head2 = pl.BlockSpec((tr, 2 * LANES), lambda i, h: (i, h))
    return _pallas(
        body, name="attn_prep", grid=(S // tr, MH),
        in_specs=[head2, head2, tab, tab, tab, tab],
        out_specs=[head2, head2, pl.BlockSpec((tr, LANES), lambda i, h: (i, h))],
        out_shape=[jax.ShapeDtypeStruct((S, MH * 2 * LANES), BF16), jax.ShapeDtypeStruct((S, MH * 2 * LANES), BF16),
                   jax.ShapeDtypeStruct((S, MH * LANES), BF16)],
        compiler_params=_params(("parallel", "parallel")),
    )(q_all, kv_all, kpe, c_t, s1_t, s2_t)


def _chunk_mask(T):
    n = lax.broadcasted_iota(jnp.int32, (T, T), 0)
    m = lax.broadcasted_iota(jnp.int32, (T, T), 1)
    return (m // CHUNK) <= (n // CHUNK)


def _lanes_to(v, width):
    return jnp.tile(v, (1, width // LANES))


def _attn_fwd(qf, kf, vb, MH, *, T):
    S = qf.shape[0]
    nt = S // T
    c2 = (QK_NOPE + QK_ROPE) ** -0.5 * LOG2E

    def body(q_ref, k_ref, v_ref, o_ref, lse_ref, m_sc, l_sc, acc_sc):
        qi = pl.program_id(1)
        m_sc[...] = jnp.full_like(m_sc, NEG)
        l_sc[...] = jnp.zeros_like(l_sc)
        acc_sc[...] = jnp.zeros_like(acc_sc)
        q = q_ref[...]

        def tile(kt, masked):
            rows = pl.ds(pl.multiple_of(kt * T, T), T)
            s = _dot(q, k_ref[rows, :], NT) * c2
            if masked:
                s = jnp.where(_chunk_mask(T), s, NEG)
            m_prev = m_sc[...]
            m_new = jnp.maximum(m_prev, jnp.max(s, axis=-1, keepdims=True))
            alpha = jnp.exp2(m_prev - m_new)
            p = jnp.exp2(s - _lanes_to(m_new, T))
            l_sc[...] = alpha * l_sc[...] + jnp.sum(p, axis=-1, keepdims=True)
            acc_sc[...] = alpha * acc_sc[...] + _dot(p, v_ref[rows, :], NN)
            m_sc[...] = m_new

        def unmasked(kt, carry):
            tile(kt, False)
            return carry

        lax.fori_loop(0, qi, unmasked, 0)
        tile(qi, True)
        l = l_sc[...]
        o_ref[...] = acc_sc[...] / l
        lse_ref[...] = m_sc[...] + jnp.log(l) * LOG2E

    return _pallas(
        body, name="attn_fwd", grid=(MH, nt),
        in_specs=[pl.BlockSpec((T, 2 * LANES), lambda h, i: (i, h)),
                  pl.BlockSpec((S, 2 * LANES), lambda h, i: (0, h)),
                  pl.BlockSpec((S, LANES), lambda h, i: (0, h))],
        out_specs=[pl.BlockSpec((T, LANES), lambda h, i: (i, h)),
                   pl.BlockSpec((None, T, LANES), lambda h, i: (h, i, 0))],
        out_shape=[jax.ShapeDtypeStruct((S, MH * LANES), F32), jax.ShapeDtypeStruct((MH, S, LANES), F32)],
        scratch_shapes=[pltpu.VMEM((T, LANES), F32), pltpu.VMEM((T, LANES), F32), pltpu.VMEM((T, LANES), F32)],
        compiler_params=_params(("parallel", "parallel")),
    )(qf, kf, vb)


def _attn_delta(do, o, MH, *, tr=512):
    S = do.shape[0]
    tr = _tile(S, tr)

    def body(do_ref, o_ref, d_ref, dob_ref):
        dov = do_ref[...]
        d_ref[...] = jnp.broadcast_to(jnp.sum(dov * o_ref[...], axis=-1, keepdims=True), (tr, LANES))
        dob_ref[...] = dov.astype(BF16)

    head = pl.BlockSpec((tr, LANES), lambda i, h: (i, h))
    return _pallas(
        body, name="attn_delta", grid=(S // tr, MH), in_specs=[head, head],
        out_specs=[pl.BlockSpec((None, tr, LANES), lambda i, h: (h, i, 0)), head],
        out_shape=[jax.ShapeDtypeStruct((MH, S, LANES), F32), jax.ShapeDtypeStruct((S, MH * LANES), BF16)],
        compiler_params=_params(("parallel", "parallel")),
    )(do, o)


def _attn_bwd(qf, kf, vb, dob, lse2, delta, MH, *, T):
    S = qf.shape[0]
    nt = S // T
    scale = (QK_NOPE + QK_ROPE) ** -0.5
    c2 = scale * LOG2E

    def body(q_ref, k_ref, v_ref, do_ref, lse_ref, dl_ref, dq_ref, dk_ref, dv_ref, dk_sc, dv_sc):
        kj = pl.program_id(1)

        @pl.when(kj == 0)
        def _():
            dq_ref[...] = jnp.zeros_like(dq_ref)

        dk_sc[...] = jnp.zeros_like(dk_sc)
        dv_sc[...] = jnp.zeros_like(dv_sc)
        k, v = k_ref[...], v_ref[...]

        def tile(qt, masked):
            rows = pl.ds(pl.multiple_of(qt * T, T), T)
            q, dov = q_ref[rows, :], do_ref[rows, :]
            s = _dot(q, k, NT) * c2
            if masked:
                s = jnp.where(_chunk_mask(T), s, NEG)
            p = jnp.exp2(s - _lanes_to(lse_ref[rows, :], T))
            dp = _dot(dov, v, NT)
            ds = p * (dp - _lanes_to(dl_ref[rows, :], T)) * scale
            dv_sc[...] += _dot(p, dov, TN)
            dk_sc[...] += _dot(ds, q, TN)
            dq_ref[rows, :] += _dot(ds, k, NN)

        def unmasked(qt, carry):
            tile(qt, False)
            return carry

        tile(kj, True)
        lax.fori_loop(kj + 1, nt, unmasked, 0)
        dk_ref[...] = dk_sc[...]
        dv_ref[...] = dv_sc[...].astype(BF16)

    stat = pl.BlockSpec((None, S, LANES), lambda h, j: (h, 0, 0))
    return _pallas(
        body, name="attn_bwd", grid=(MH, nt),
        in_specs=[pl.BlockSpec((S, 2 * LANES), lambda h, j: (0, h)),
                  pl.BlockSpec((T, 2 * LANES), lambda h, j: (j, h)),
                  pl.BlockSpec((T, LANES), lambda h, j: (j, h)),
                  pl.BlockSpec((S, LANES), lambda h, j: (0, h)), stat, stat],
        out_specs=[pl.BlockSpec((S, 2 * LANES), lambda h, j: (0, h)),
                   pl.BlockSpec((T, 2 * LANES), lambda h, j: (j, h)),
                   pl.BlockSpec((T, LANES), lambda h, j: (j, h))],
        out_shape=[jax.ShapeDtypeStruct((S, MH * 2 * LANES), F32), jax.ShapeDtypeStruct((S, MH * 2 * LANES), F32),
                   jax.ShapeDtypeStruct((S, MH * LANES), BF16)],
        scratch_shapes=[pltpu.VMEM((T, 2 * LANES), F32), pltpu.VMEM((T, LANES), F32)],
        compiler_params=_params(("parallel", "arbitrary")),
    )(qf, kf, vb, dob, lse2, delta)


def _attn_post(dqf, dkf, dvb, tabs, MH, *, tr=512):
    S = dqf.shape[0]
    tr = _tile(S, tr)
    c_t, s1_t, s2_t = tabs

    def body(dq_ref, dk_ref, dv_ref, c_ref, s1_ref, s2_ref, dqa_ref, dkv_ref, dkpe_ref, acc):
        h = pl.program_id(1)
        c, s1, s2 = c_ref[...], s1_ref[...], s2_ref[...]
        dqa_ref[:, :QK_NOPE] = dq_ref[:, :QK_NOPE].astype(BF16)
        dqa_ref[:, QK_NOPE:] = _rope_pe_t(dq_ref[:, QK_NOPE:], c, s1, s2).astype(BF16)
        dkv_ref[:, :QK_NOPE] = dk_ref[:, :QK_NOPE].astype(BF16)
        dkv_ref[:, QK_NOPE:] = dv_ref[...]

        @pl.when(h == 0)
        def _():
            acc[...] = dk_ref[:, QK_NOPE:]

        @pl.when(h > 0)
        def _():
            acc[...] += dk_ref[:, QK_NOPE:]

        @pl.when(h == MH - 1)
        def _():
            dkpe_ref[...] = _rope_pe_t(acc[...], c, s1, s2).astype(BF16)

    tab = pl.BlockSpec((tr, LANES), lambda i, h: (i, 0))
    head2 = pl.BlockSpec((tr, 2 * LANES), lambda i, h: (i, h))
    return _pallas(
        body, name="attn_post", grid=(S // tr, MH),
        in_specs=[head2, head2, pl.BlockSpec((tr, LANES), lambda i, h: (i, h)), tab, tab, tab],
        out_specs=[head2, head2, tab],
        out_shape=[jax.ShapeDtypeStruct((S, MH * 2 * LANES), BF16)] * 2 + [jax.ShapeDtypeStruct((S, LANES), BF16)],
        scratch_shapes=[pltpu.VMEM((tr, LANES), F32)],
        compiler_params=_params(("parallel", "arbitrary")),
    )(dqf, dkf, dvb, c_t, s1_t, s2_t)


def _block_rows(R, C, block_bytes=2 << 20):
    tr = 8
    while tr * 2 * C * 4 <= block_bytes:
        tr *= 2
    return _tile(R, tr)


def _rows_call(fn, ins, out_dtypes, *, name):
    R, C = ins[0].shape
    tr = _block_rows(R, C)
    n_in = len(ins)

    def body(*refs):
        vals = fn(*[r[...] for r in refs[:n_in]])
        for r, v in zip(refs[n_in:], vals):
            r[...] = v.astype(r.dtype)

    blk = pl.BlockSpec((tr, C), lambda i: (i, 0))
    res = _pallas(
        body, name=name, grid=(R // tr,), in_specs=[blk] * n_in, out_specs=[blk] * len(out_dtypes),
        out_shape=[jax.ShapeDtypeStruct((R, C), d) for d in out_dtypes],
        compiler_params=_params(("parallel",)),
    )(*ins)
    return res


def _adamw_vals(w, g, m, v):
    m = ADAM_B1 * m + (1.0 - ADAM_B1) * g
    v = ADAM_B2 * v + (1.0 - ADAM_B2) * (g * g)
    m_hat = m / (1.0 - ADAM_B1 ** ADAM_STEP)
    v_hat = v / (1.0 - ADAM_B2 ** ADAM_STEP)
    delta = -ADAM_LR * (m_hat / (jnp.sqrt(v_hat) + ADAM_EPS) + ADAM_WD * w)
    return delta, m, v


def _sum_pair(p, theirs, place, *, name):
    _, R, C = p.shape
    R2 = R // 2
    tr = _block_rows(R2, C)
    p4 = p.reshape(N_CHIPS, 2, R2, C)

    def body(place_ref, a_ref, b_ref, o_ref):
        o_ref[...] = (a_ref[...] + b_ref[...]).astype(BF16)

    spec = pltpu.PrefetchScalarGridSpec(
        num_scalar_prefetch=1, grid=(N_CHIPS, R2 // tr),
        in_specs=[pl.BlockSpec((None, None, tr, C), lambda q, i, pr: (q, pr[0], i, 0)),
                  pl.BlockSpec((None, tr, C), lambda q, i, pr: (q, i, 0))],
        out_specs=pl.BlockSpec((None, tr, C), lambda q, i, pr: (q, i, 0)))
    return _pallas(body, name=name, grid_spec=spec, out_shape=jax.ShapeDtypeStruct((N_CHIPS, R2, C), BF16),
                   compiler_params=_params(("parallel", "parallel")))(place, p4, theirs)


def _sum_chips(p, theirs, recv, place, *, name):
    _, R, C = p.shape
    R2 = R // 2
    tr = _block_rows(R2, C)
    p4 = p.reshape(N_CHIPS, 2, R2, C)

    def body(place_ref, a_ref, b_ref, r0_ref, r1_ref, r2_ref, o_ref):
        own = a_ref[...] + b_ref[...]
        o_ref[...] = ((own + r0_ref[...].astype(F32)) + r1_ref[...].astype(F32)) + r2_ref[...].astype(F32)

    def slot(k):
        return pl.BlockSpec((None, tr, C), lambda i, pr: (k, i, 0))

    spec = pltpu.PrefetchScalarGridSpec(
        num_scalar_prefetch=1, grid=(R2 // tr,),
        in_specs=[pl.BlockSpec((None, None, tr, C), lambda i, pr: (pr[1], pr[0], i, 0)),
                  pl.BlockSpec((None, tr, C), lambda i, pr: (pr[1], i, 0)), slot(0), slot(1), slot(2)],
        out_specs=pl.BlockSpec((None, tr, C), lambda i, pr: (pr[0], i, 0)))
    return _pallas(body, name=name, grid_spec=spec, out_shape=jax.ShapeDtypeStruct((2, R2, C), F32),
                   compiler_params=_params(("parallel",)))(place, p4, theirs, recv, recv, recv)


def _me():
    return lax.axis_index("x"), lax.axis_index("y"), lax.axis_index("c")


def _other_chips(x, y):
    return [(1 - x, y), (x, 1 - y), (1 - x, 1 - y)]


def _rcopy(src, dst, ssem, rsem, dev):
    return pltpu.make_async_remote_copy(src_ref=src, dst_ref=dst, send_sem=ssem, recv_sem=rsem,
                                        device_id=dev, device_id_type=MESH)


def _gather_weights(shards):
    n = len(shards)
    views = [s.reshape(2, s.shape[0] // 2, s.shape[1]) for s in shards]

    def body(*refs):
        ins, outs = refs[:n], refs[n:2 * n]
        lsem, ssem, rsem, fssem, frsem = refs[2 * n:]
        x, y, c = _me()
        j = 2 * x + y
        sib = (x, y, 1 - c)
        chips = _other_chips(x, y)
        local, remote = [], []
        for i in range(n):
            cp = pltpu.make_async_copy(ins[i], outs[i].at[j], lsem.at[i])
            cp.start()
            local.append(cp)
            for k, (px, py) in enumerate(chips):
                r = _rcopy(ins[i].at[c], outs[i].at[j, c], ssem.at[i, k], rsem.at[i, k], (px, py, c))
                r.start()
                remote.append(r)
        for i in range(n):
            for k, (px, py) in enumerate(chips):
                jp = 2 * px + py
                _rcopy(ins[i].at[c], outs[i].at[jp, c], ssem.at[i, k], rsem.at[i, k], (px, py, c)).wait_recv()
                f = _rcopy(outs[i].at[jp, c], outs[i].at[jp, c], fssem.at[i, k], frsem.at[i, k], sib)
                f.start()
                remote.append(f)
        for i in range(n):
            for k, (px, py) in enumerate(chips):
                jp = 2 * px + py
                _rcopy(outs[i].at[jp, 1 - c], outs[i].at[jp, 1 - c], fssem.at[i, k], frsem.at[i, k], sib).wait_recv()
        for r in remote:
            r.wait_send()
        for cp in local:
            cp.wait()

    res = _pallas(
        body, name="gather_weights", in_specs=[_ANY] * n, out_specs=[_ANY] * n,
        out_shape=[jax.ShapeDtypeStruct((N_CHIPS,) + v.shape, v.dtype) for v in views],
        scratch_shapes=[pltpu.SemaphoreType.DMA((n,)), pltpu.SemaphoreType.DMA((n, 3)),
                        pltpu.SemaphoreType.DMA((n, 3)), pltpu.SemaphoreType.DMA((n, 3)),
                        pltpu.SemaphoreType.DMA((n, 3))],
        compiler_params=pltpu.CompilerParams(has_side_effects=True),
    )(*views)
    return [r.reshape((N_CHIPS,) + s.shape) for r, s in zip(res, shards)]


def _swap_halves(grads):
    n = len(grads)
    views = [g.reshape(N_CHIPS, 2, g.shape[1] // 2, g.shape[2]) for g in grads]

    def body(*refs):
        ins, outs = refs[:n], refs[n:2 * n]
        ssem, rsem = refs[2 * n:]
        x, y, c = _me()
        sib = (x, y, 1 - c)
        cps = []
        for i in range(n):
            r = _rcopy(ins[i].at[:, 1 - c], outs[i], ssem.at[i], rsem.at[i], sib)
            r.start()
            cps.append(r)
        for r in cps:
            r.wait()

    return _pallas(
        body, name="swap_halves", in_specs=[_ANY] * n, out_specs=[_ANY] * n,
        out_shape=[jax.ShapeDtypeStruct((N_CHIPS,) + v.shape[2:], v.dtype) for v in views],
        scratch_shapes=[pltpu.SemaphoreType.DMA((n,)), pltpu.SemaphoreType.DMA((n,))],
        compiler_params=pltpu.CompilerParams(has_side_effects=True),
    )(*views)


def _scatter_chips(sums):
    n = len(sums)

    def body(*refs):
        ins, outs = refs[:n], refs[n:2 * n]
        ssem, rsem = refs[2 * n:]
        x, y, c = _me()
        chips = _other_chips(x, y)
        remote = []
        for i in range(n):
            for k, (px, py) in enumerate(chips):
                r = _rcopy(ins[i].at[2 * px + py], outs[i].at[k], ssem.at[i, k], rsem.at[i, k], (px, py, c))
                r.start()
                remote.append(r)
        for r in remote:
            r.wait()

    return _pallas(
        body, name="scatter_chips", in_specs=[_ANY] * n, out_specs=[_ANY] * n,
        out_shape=[jax.ShapeDtypeStruct((3,) + s.shape[1:], s.dtype) for s in sums],
        scratch_shapes=[pltpu.SemaphoreType.DMA((n, 3)), pltpu.SemaphoreType.DMA((n, 3))],
        compiler_params=pltpu.CompilerParams(has_side_effects=True),
    )(*sums)


def _join_halves(halves):
    n = len(halves)

    def body(*refs):
        outs = refs[n:2 * n]
        ssem, rsem = refs[2 * n:]
        x, y, c = _me()
        sib = (x, y, 1 - c)
        cps = []
        for i in range(n):
            r = _rcopy(outs[i].at[c], outs[i].at[c], ssem.at[i], rsem.at[i], sib)
            r.start()
            cps.append(r)
        for r in cps:
            r.wait()

    return _pallas(
        body, name="join_halves", in_specs=[_ANY] * n, out_specs=[_ANY] * n,
        out_shape=[jax.ShapeDtypeStruct(h.shape, h.dtype) for h in halves],
        scratch_shapes=[pltpu.SemaphoreType.DMA((n,)), pltpu.SemaphoreType.DMA((n,))],
        input_output_aliases={i: i for i in range(n)},
        compiler_params=pltpu.CompilerParams(has_side_effects=True),
    )(*halves)


def _allreduce_small(vec):
    R = vec.shape[0]

    def body(v_ref, o_ref, buf, ssem, rsem):
        x, y, c = _me()
        me = 4 * x + 2 * y + c
        buf[me] = v_ref[...]
        cps = []
        for k in range(1, 8):
            peer = (x ^ (k >> 2), y ^ ((k >> 1) & 1), c ^ (k & 1))
            r = _rcopy(v_ref, buf.at[me], ssem.at[k - 1], rsem.at[k - 1], peer)
            r.start()
            cps.append(r)
        for k in range(1, 8):
            peer = (x ^ (k >> 2), y ^ ((k >> 1) & 1), c ^ (k & 1))
            pid = 4 * peer[0] + 2 * peer[1] + peer[2]
            _rcopy(v_ref, buf.at[pid], ssem.at[k - 1], rsem.at[k - 1], peer).wait_recv()
        for r in cps:
            r.wait_send()
        tot = buf[0]
        for d in range(1, 8):
            tot = tot + buf[d]
        o_ref[...] = tot

    vm = pl.BlockSpec(memory_space=pltpu.VMEM)
    return _pallas(
        body, name="allreduce_small", in_specs=[vm], out_specs=vm,
        out_shape=jax.ShapeDtypeStruct((R, LANES), F32),
        scratch_shapes=[pltpu.VMEM((8, R, LANES), F32), pltpu.SemaphoreType.DMA((7,)), pltpu.SemaphoreType.DMA((7,))],
        compiler_params=pltpu.CompilerParams(has_side_effects=True),
    )(vec)


def _rope_tables(positions, S):
    pos = positions.reshape(S, 1).astype(F32)
    half = RET_QK // 2
    inv = ROPE_THETA ** (-jnp.arange(half, dtype=F32) / half)
    ang = pos * inv
    cosr = jnp.concatenate([jnp.cos(ang), jnp.cos(ang)], axis=1)
    sinr = jnp.concatenate([-jnp.sin(ang), jnp.sin(ang)], axis=1)
    half = QK_ROPE // 2
    inv = ROPE_THETA ** (-jnp.arange(half, dtype=F32) / half)
    ang = pos * inv
    z = jnp.zeros((S, half), F32)
    c = jnp.concatenate([jnp.cos(ang), jnp.cos(ang), z, z], axis=1)
    s1 = jnp.concatenate([-jnp.sin(ang), z, z, z], axis=1)
    s2 = jnp.concatenate([z, jnp.sin(ang), z, z], axis=1)
    return cosr, sinr, (c, s1, s2)


def _cat_cols(g):
    return jnp.concatenate([g[j] for j in range(N_CHIPS)], axis=1)


def _split_cols(w):
    return jnp.stack(jnp.split(w, N_CHIPS, axis=1))


def _pack_small(vs, rows):
    flat = jnp.concatenate([v.reshape(-1) for v in vs])
    flat = jnp.pad(flat, (0, rows * LANES - flat.shape[0]))
    return flat.reshape(rows, LANES)


def kernel(x, positions, norm_mix_g, w_in, ret_norm_g, w_ret_o, q_a_norm_g, w_q_b, kv_a_norm_g, w_kv_b, w_mla_o, w_out, norm_mlp_g, w_up, w_down, norm_f_g, loss_target, m_norm_mix_g, m_w_in, m_ret_norm_g, m_w_ret_o, m_q_a_norm_g, m_w_q_b, m_kv_a_norm_g, m_w_kv_b, m_w_mla_o, m_w_out, m_norm_mlp_g, m_w_up, m_w_down, m_norm_f_g, v_norm_mix_g, v_w_in, v_ret_norm_g, v_w_ret_o, v_q_a_norm_g, v_w_q_b, v_kv_a_norm_g, v_w_kv_b, v_w_mla_o, v_w_out, v_norm_mlp_g, v_w_up, v_w_down, v_norm_f_g):
    S, D = x.shape[1], x.shape[2]
    RVW = w_ret_o.shape[1] * N_CHIPS
    RH = RVW // RET_V
    RQW = RH * RET_QK
    MVW = w_mla_o.shape[1] * N_CHIPS
    MH = MVW // V_HEAD
    QL, KVL = w_q_b.shape[1], w_kv_b.shape[1]
    T_RET = _tile(S, 256)
    T_ATT = _tile(S, 512)

    xs = x.reshape(S, D)
    tgt = loss_target.reshape(S, D)
    cosr, sinr, pe_tabs = _rope_tables(positions, S)
    lgam = jnp.log(1.0 - 2.0 ** (-5.0 - jnp.arange(RH, dtype=F32)))
    lgam = jnp.broadcast_to(lgam[:, None, None], (RH, 8, LANES))

    big = ("w_in", "w_ret_o", "w_q_b", "w_kv_b", "w_mla_o", "w_out", "w_up", "w_down")
    w_sh = dict(w_in=w_in[0], w_ret_o=w_ret_o[0], w_q_b=w_q_b[0], w_kv_b=w_kv_b[0], w_mla_o=w_mla_o[0],
                w_out=w_out[0], w_up=w_up[0], w_down=w_down[0])
    m_sh = dict(w_in=m_w_in[0], w_ret_o=m_w_ret_o[0], w_q_b=m_w_q_b[0], w_kv_b=m_w_kv_b[0], w_mla_o=m_w_mla_o[0],
                w_out=m_w_out[0], w_up=m_w_up[0], w_down=m_w_down[0])
    v_sh = dict(w_in=v_w_in[0], w_ret_o=v_w_ret_o[0], w_q_b=v_w_q_b[0], w_kv_b=v_w_kv_b[0], w_mla_o=v_w_mla_o[0],
                w_out=v_w_out[0], w_up=v_w_up[0], w_down=v_w_down[0])
    col_sharded = ("w_in", "w_q_b", "w_kv_b", "w_up")
    gathered = _gather_weights([w_sh[k].astype(BF16) for k in big])
    full = {}
    for k, g in zip(big, gathered):
        full[k] = _cat_cols(g) if k in col_sharded else g.reshape(-1, g.shape[2])

    o_rq, o_rk, o_rv, o_rg = 0, RQW, 2 * RQW, 2 * RQW + RVW
    o_cq = 2 * RQW + 2 * RVW
    o_ckv, o_kpe = o_cq + QL, o_cq + QL + KVL
    o_gr = o_kpe + QK_ROPE
    o_gm = o_gr + D
    n_ret = RH * RET_HEAD_COLS
    off_gret, off_gmla, off_cq, off_ckv = n_ret, n_ret + D, n_ret + 2 * D, n_ret + 2 * D + QL
    wi = full["w_in"]
    ret_cols = jnp.concatenate([wi[:, o_rq:o_rk].reshape(D, RH, RET_QK), wi[:, o_rk:o_rv].reshape(D, RH, RET_QK),
                                wi[:, o_rv:o_rg].reshape(D, RH, RET_V), wi[:, o_rg:o_cq].reshape(D, RH, RET_V)],
                               axis=2).reshape(D, n_ret)
    wa = jnp.concatenate([ret_cols, wi[:, o_gr:o_gm], wi[:, o_gm:], wi[:, o_cq:o_ckv], wi[:, o_ckv:o_kpe]], axis=1)
    wkpe = jnp.pad(wi[:, o_kpe:o_gr], ((0, 0), (0, LANES - QK_ROPE)))
    wq = jnp.pad(full["w_q_b"].reshape(QL, MH, QK_NOPE + QK_ROPE),
                 ((0, 0), (0, 0), (0, LANES - QK_ROPE))).reshape(QL, MH * 2 * LANES)
    wkv = full["w_kv_b"]

    u, rstd0 = _rmsnorm_fwd(xs, norm_mix_g, name="norm_mix")
    proj = _mm(u, wa, mode="nn", outs=[F32], name="in_proj")
    kpe = _mm(u, wkpe, mode="nn", outs=[F32], name="kpe_proj")
    ry, gated, states = _ret_fwd(proj, cosr, sinr, lgam, ret_norm_g, RH, T=T_RET)
    y_ret = _mm(gated, full["w_ret_o"], mode="nn", outs=[F32], name="ret_o")
    cqn, rstd_q = _rmsnorm_fwd(proj, q_a_norm_g, name="norm_q", width=QL, col=off_cq // QL)
    ckvn, rstd_kv = _rmsnorm_fwd(proj, kv_a_norm_g, name="norm_kv", width=KVL, col=off_ckv // KVL)
    q_all = _mm(cqn, wq, mode="nn", outs=[F32], name="q_proj")
    kv_all = _mm(ckvn, wkv, mode="nn", outs=[F32], name="kv_proj")
    qf, kf, vb = _attn_prep(q_all, kv_all, kpe, pe_tabs, MH)
    my, lse2 = _attn_fwd(qf, kf, vb, MH, T=T_ATT)
    y_mla = _mm(my, full["w_mla_o"], mode="nn", outs=[F32], name="mla_o")
    merged = _merge_fwd(proj, y_ret, y_mla, D, off_gret, off_gmla)
    h1 = _mm(merged, full["w_out"], mode="nn", outs=[F32], name="out_proj",
             epi=lambda acc, r: (acc + r,), extras=(xs,))
    n1, rstd1 = _rmsnorm_fwd(h1, norm_mlp_g, name="norm_mlp")

    def up_epi(acc):
        r = jnp.maximum(acc, 0.0)
        return acc, r * r

    z, act = _mm(n1, full["w_up"], mode="nn", outs=[F32, BF16], name="up_proj", epi=up_epi)
    h2 = _mm(act, full["w_down"], mode="nn", outs=[F32], name="down_proj",
             epi=lambda acc, r: (acc + r,), extras=(h1,))
    loss11, dh2, g_norm_f = _final_loss(h2, norm_f_g.reshape(1, D), tgt)

    dz = _mm(dh2, full["w_down"], mode="nt", outs=[BF16], name="down_bwd_x",
             epi=lambda acc, zz: (acc * (2.0 * jnp.maximum(zz, 0.0)),), extras=(z,))
    g_w_down = _mm(act, dh2, mode="tn", outs=[F32], name="down_bwd_w")
    dn1 = _mm(dz, full["w_up"], mode="nt", outs=[F32], name="up_bwd_x")
    g_w_up = _mm(n1, dz, mode="tn", outs=[F32], name="up_bwd_w")
    dh1, g_norm_mlp = _rmsnorm_bwd(dn1, h1, rstd1, norm_mlp_g, name="norm_mlp_bwd", res=dh2)
    dmerged = _mm(dh1, full["w_out"], mode="nt", outs=[F32], name="out_bwd_x")
    g_w_out = _mm(merged, dh1, mode="tn", outs=[F32], name="out_bwd_w")
    dproj, dy_ret, dy_mla = _merge_bwd(dmerged, proj, y_ret, y_mla, D, off_gret)
    dgated = _mm(dy_ret, full["w_ret_o"], mode="nt", outs=[F32], name="ret_o_bwd_x")
    g_w_ret_o = _mm(gated, dy_ret, mode="tn", outs=[F32], name="ret_o_bwd_w")
    dproj, g_ret_norm = _ret_bwd(proj, cosr, sinr, lgam, ret_norm_g, ry, dgated, states, dproj, RH, T=T_RET)
    dmy = _mm(dy_mla, full["w_mla_o"], mode="nt", outs=[F32], name="mla_o_bwd_x")
    g_w_mla_o = _mm(my, dy_mla, mode="tn", outs=[F32], name="mla_o_bwd_w")
    delta, dob = _attn_delta(dmy, my, MH)
    dqf, dkf, dvb = _attn_bwd(qf, kf, vb, dob, lse2, delta, MH, T=T_ATT)
    dq_all, dkv_all, dkpe = _attn_post(dqf, dkf, dvb, pe_tabs, MH)
    dcqn = _mm(dq_all, wq, mode="nt", outs=[F32], name="q_bwd_x")
    g_wq = _mm(cqn, dq_all, mode="tn", outs=[F32], name="q_bwd_w")
    dckvn = _mm(dkv_all, wkv, mode="nt", outs=[F32], name="kv_bwd_x")
    g_wkv = _mm(ckvn, dkv_all, mode="tn", outs=[F32], name="kv_bwd_w")
    dproj, g_q_a = _rmsnorm_bwd(dcqn, proj, rstd_q, q_a_norm_g, name="norm_q_bwd", into=(dproj, off_cq // QL),
                                width=QL, col=off_cq // QL)
    dproj, g_kv_a = _rmsnorm_bwd(dckvn, proj, rstd_kv, kv_a_norm_g, name="norm_kv_bwd", into=(dproj, off_ckv // KVL),
                                 width=KVL, col=off_ckv // KVL)
    du_a = _mm(dproj, wa, mode="nt", outs=[F32], name="in_bwd_x", tk=1024)
    du = _mm(dkpe, wkpe, mode="nt", outs=[F32], name="kpe_bwd_x", epi=lambda acc, r: (acc + r,), extras=(du_a,))
    g_wa = _mm(u, dproj, mode="tn", outs=[F32], name="in_bwd_w")
    g_wkpe = _mm(u, dkpe, mode="tn", outs=[F32], name="kpe_bwd_w")
    dx, g_norm_mix = _rmsnorm_bwd(du, xs, rstd0, norm_mix_g, name="norm_mix_bwd", res=dh1)

    gr = g_wa[:, :n_ret].reshape(D, RH, RET_HEAD_COLS)
    g_w_in = jnp.concatenate([gr[:, :, _RQ].reshape(D, RQW), gr[:, :, _RK].reshape(D, RQW),
                              gr[:, :, _RV].reshape(D, RVW), gr[:, :, _RG].reshape(D, RVW),
                              g_wa[:, off_cq:], g_wkpe[:, :QK_ROPE], g_wa[:, off_gret:off_cq]], axis=1)
    gq = g_wq.reshape(QL, MH, 2 * LANES)[:, :, :QK_NOPE + QK_ROPE].reshape(QL, MH * (QK_NOPE + QK_ROPE))
    g_full = dict(w_in=g_w_in, w_ret_o=g_w_ret_o, w_q_b=gq, w_kv_b=g_wkv, w_mla_o=g_w_mla_o, w_out=g_w_out,
                  w_up=g_w_up, w_down=g_w_down)
    pieces = []
    for k in big:
        g = g_full[k]
        pieces.append(_split_cols(g) if k in col_sharded else g.reshape(N_CHIPS, g.shape[0] // N_CHIPS, g.shape[1]))

    place = jnp.stack([lax.axis_index("c"), 2 * lax.axis_index("x") + lax.axis_index("y")]).astype(jnp.int32)
    theirs = _swap_halves(pieces)
    sums = [_sum_pair(p, t, place, name="sum_pair_" + k) for k, p, t in zip(big, pieces, theirs)]
    recv = _scatter_chips(sums)
    halves = [_sum_chips(p, t, r, place, name="sum_chips_" + k) for k, p, t, r in zip(big, pieces, theirs, recv)]
    g_shard = {k: g.reshape(2 * g.shape[1], g.shape[2]) for k, g in zip(big, _join_halves(halves))}

    small = ("norm_mix_g", "ret_norm_g", "q_a_norm_g", "kv_a_norm_g", "norm_mlp_g", "norm_f_g")
    g_small = [g_norm_mix, g_ret_norm, g_q_a, g_kv_a, g_norm_mlp, g_norm_f]
    sizes = [int(v.size) for v in g_small]
    n_small = sum(sizes) + LANES
    rows = -(-n_small // (8 * LANES)) * 8
    packed = _pack_small(g_small + [jnp.broadcast_to(loss11.reshape(1), (LANES,))], rows)
    red = _allreduce_small(packed).reshape(-1)
    loss = red[sum(sizes)]
    w_small = [norm_mix_g, ret_norm_g, q_a_norm_g, kv_a_norm_g, norm_mlp_g, norm_f_g]
    m_small = [m_norm_mix_g, m_ret_norm_g, m_q_a_norm_g, m_kv_a_norm_g, m_norm_mlp_g, m_norm_f_g]
    v_small = [v_norm_mix_g, v_ret_norm_g, v_q_a_norm_g, v_kv_a_norm_g, v_norm_mlp_g, v_norm_f_g]
    g_pk = red[:rows * LANES].reshape(rows, LANES)
    d_pk, m_pk, v_pk = _rows_call(_adamw_vals, [_pack_small(w_small, rows), g_pk, _pack_small(m_small, rows),
                                               _pack_small(v_small, rows)], [F32, F32, F32], name="adamw_small")
    out_g, out_d, out_m, out_v = {}, {}, {}, {}
    off = 0
    for k, wv, sz in zip(small, w_small, sizes):
        for dst, src in ((out_g, g_pk), (out_d, d_pk), (out_m, m_pk), (out_v, v_pk)):
            dst[k] = src.reshape(-1)[off:off + sz].reshape(wv.shape)
        off += sz

    for k in big:
        d_, m_, v_ = _rows_call(_adamw_vals, [w_sh[k], g_shard[k], m_sh[k], v_sh[k]], [F32, F32, F32],
                                name="adamw_" + k)
        out_g[k] = g_shard[k][None]
        out_d[k], out_m[k], out_v[k] = d_[None], m_[None], v_[None]

    order = ("norm_mix_g", "w_in", "ret_norm_g", "w_ret_o", "q_a_norm_g", "w_q_b", "kv_a_norm_g", "w_kv_b",
             "w_mla_o", "w_out", "norm_mlp_g", "w_up", "w_down", "norm_f_g")
    return (loss, dx.reshape(1, S, D), *[out_g[k] for k in order], *[out_d[k] for k in order],
            *[out_m[k] for k in order], *[out_v[k] for k in order])
```

```python
import math

import jax
import jax.numpy as jnp
from jax import lax
from jax.experimental import pallas as pl
from jax.experimental.pallas import tpu as pltpu

F32 = jnp.float32
BF16 = jnp.bfloat16

EPS = 1e-6
ROPE_THETA = 10000.0
CHUNK = 64
RET_QK = 128
RET_V = 256
RET_HEAD_COLS = 2 * RET_QK + 2 * RET_V
QK_NOPE = 128
QK_ROPE = 64
V_HEAD = 128
LANES = 128
LOG2E = math.log2(math.e)

ADAM_LR = 0.001
ADAM_B1 = 0.9
ADAM_B2 = 0.999
ADAM_EPS = 1e-08
ADAM_WD = 0.01
ADAM_STEP = 10

N_CHIPS = 4
VMEM_LIMIT = 56 * 1024 * 1024
MESH = pl.DeviceIdType.MESH
NEG = -1e30


def _pallas(body, **kw):
    return pl.pallas_call(body, **kw)


def _params(sem=None):
    return pltpu.CompilerParams(dimension_semantics=sem, vmem_limit_bytes=VMEM_LIMIT)


def _tile(n, want):
    t = min(n, want)
    while n % t:
        t //= 2
    return t


_ANY = pl.BlockSpec(memory_space=pl.ANY)


def _mm(a, b, *, mode, outs, name, epi=None, extras=(), deps=(), tm=1024, tn=1024, tk=2048):
    if mode == "nn":
        (M, K), (_, N) = a.shape, b.shape
    elif mode == "nt":
        (M, K), (N, _) = a.shape, b.shape
    else:
        (K, M), (_, N) = a.shape, b.shape
    tm, tn, tk = _tile(M, tm), _tile(N, tn), _tile(K, tk)
    nk = K // tk
    if mode == "nn":
        a_spec = pl.BlockSpec((tm, tk), lambda i, j, k: (i, k))
        b_spec = pl.BlockSpec((tk, tn), lambda i, j, k: (k, j))
        dims = (((1,), (0,)), ((), ()))
    elif mode == "nt":
        a_spec = pl.BlockSpec((tm, tk), lambda i, j, k: (i, k))
        b_spec = pl.BlockSpec((tn, tk), lambda i, j, k: (j, k))
        dims = (((1,), (1,)), ((), ()))
    else:
        a_spec = pl.BlockSpec((tk, tm), lambda i, j, k: (k, i))
        b_spec = pl.BlockSpec((tk, tn), lambda i, j, k: (k, j))
        dims = (((0,), (0,)), ((), ()))
    o_spec = pl.BlockSpec((tm, tn), lambda i, j, k: (i, j))
    n_ex, n_out, n_dep = len(extras), len(outs), len(deps)
    if epi is None:
        epi = lambda acc: (acc,)

    def body(*refs):
        a_ref, b_ref = refs[0], refs[1]
        ex_refs = refs[2:2 + n_ex]
        o_refs = refs[2 + n_ex + n_dep:2 + n_ex + n_dep + n_out]
        part = lax.dot_general(a_ref[...].astype(BF16), b_ref[...].astype(BF16), dims,
                               preferred_element_type=F32)

        def finish(acc):
            vals = epi(acc, *[r[...] for r in ex_refs])
            for r, v in zip(o_refs, vals):
                r[...] = v.astype(r.dtype)

        if nk == 1:
            finish(part)
        else:
            acc_ref = refs[-1]
            k = pl.program_id(2)

            @pl.when(k == 0)
            def _():
                acc_ref[...] = part

            @pl.when(k > 0)
            def _():
                acc_ref[...] += part

            @pl.when(k == nk - 1)
            def _():
                finish(acc_ref[...])

    res = _pallas(
        body, name=name, grid=(M // tm, N // tn, nk),
        in_specs=[a_spec, b_spec] + [o_spec] * n_ex + [_ANY] * n_dep,
        out_specs=[o_spec] * n_out,
        out_shape=[jax.ShapeDtypeStruct((M, N), d) for d in outs],
        scratch_shapes=[pltpu.VMEM((tm, tn), F32)] if nk > 1 else [],
        compiler_params=_params(("parallel", "parallel", "arbitrary")),
    )(a, b, *extras, *deps)
    return res[0] if n_out == 1 else res


def _rmsnorm_fwd(x, g, *, name, width=None, col=0, tr=256):
    S = x.shape[0]
    W = x.shape[1] if width is None else width
    tr = _tile(S, tr)

    def body(x_ref, g_ref, y_ref, r_ref):
        xv = x_ref[...]
        rstd = lax.rsqrt(jnp.mean(xv * xv, axis=-1, keepdims=True) + EPS)
        y_ref[...] = (xv * rstd * g_ref[...]).astype(BF16)
        r_ref[...] = rstd

    return _pallas(
        body, name=name, grid=(S // tr,),
        in_specs=[pl.BlockSpec((tr, W), lambda i: (i, col)), pl.BlockSpec((1, W), lambda i: (0, 0))],
        out_specs=[pl.BlockSpec((tr, W), lambda i: (i, 0)), pl.BlockSpec((tr, 1), lambda i: (i, 0))],
        out_shape=[jax.ShapeDtypeStruct((S, W), BF16), jax.ShapeDtypeStruct((S, 1), F32)],
        compiler_params=_params(("parallel",)),
    )(x, g)


def _rmsnorm_bwd(dy, x, rstd, g, *, name, res=None, into=None, deps=(), width=None, col=0, tr=256):
    S = x.shape[0]
    W = x.shape[1] if width is None else width
    tr = _tile(S, tr)
    has_res = res is not None

    def body(*refs):
        dy_ref, x_ref, r_ref, g_ref = refs[:4]
        dx_ref, dg_ref = refs[-2], refs[-1]
        rstd_v = r_ref[...]
        xhat = x_ref[...] * rstd_v
        dyv = dy_ref[...].astype(F32)
        dyg = dyv * g_ref[...]
        dx = rstd_v * (dyg - xhat * jnp.mean(dyg * xhat, axis=-1, keepdims=True))
        if has_res:
            dx = dx + refs[4][...]
        dx_ref[...] = dx.astype(dx_ref.dtype)
        part = jnp.sum(dyv * xhat, axis=0, keepdims=True)

        @pl.when(pl.program_id(0) == 0)
        def _():
            dg_ref[...] = part

        @pl.when(pl.program_id(0) > 0)
        def _():
            dg_ref[...] += part

    row = pl.BlockSpec((tr, W), lambda i: (i, 0))
    ins = [dy, x, rstd, g] + ([res] if has_res else [])
    in_specs = [row, pl.BlockSpec((tr, W), lambda i: (i, col)), pl.BlockSpec((tr, 1), lambda i: (i, 0)),
                pl.BlockSpec((1, W), lambda i: (0, 0))] + ([row] if has_res else [])
    if into is None:
        dx_spec, dx_shape, alias = row, jax.ShapeDtypeStruct((S, W), F32), {}
    else:
        buf, col_out = into
        ins.append(buf)
        in_specs.append(_ANY)
        dx_spec = pl.BlockSpec((tr, W), lambda i: (i, col_out))
        dx_shape = jax.ShapeDtypeStruct(buf.shape, buf.dtype)
        alias = {len(ins) - 1: 0}
    ins += list(deps)
    in_specs += [_ANY] * len(deps)
    return _pallas(
        body, name=name, grid=(S // tr,), in_specs=in_specs,
        out_specs=[dx_spec, pl.BlockSpec((1, W), lambda i: (0, 0))],
        out_shape=[dx_shape, jax.ShapeDtypeStruct((1, W), F32)],
        input_output_aliases=alias,
        compiler_params=_params(("arbitrary",)),
    )(*ins)


def _final_loss(h2, g, target, *, tr=256):
    S, D = h2.shape
    tr = _tile(S, tr)

    def body(h_ref, g_ref, t_ref, loss_ref, dh_ref, dg_ref):
        hv = h_ref[...]
        rstd = lax.rsqrt(jnp.mean(hv * hv, axis=-1, keepdims=True) + EPS)
        xhat = hv * rstd
        e = xhat * g_ref[...] - t_ref[...]
        lpart = (0.5 / D) * jnp.sum(jnp.sum(e * e, axis=-1, keepdims=True), axis=0, keepdims=True)
        dy = e * (1.0 / D)
        dyg = dy * g_ref[...]
        dh_ref[...] = rstd * (dyg - xhat * jnp.mean(dyg * xhat, axis=-1, keepdims=True))
        gpart = jnp.sum(dy * xhat, axis=0, keepdims=True)

        @pl.when(pl.program_id(0) == 0)
        def _():
            loss_ref[...] = lpart
            dg_ref[...] = gpart

        @pl.when(pl.program_id(0) > 0)
        def _():
            loss_ref[...] += lpart
            dg_ref[...] += gpart

    row = pl.BlockSpec((tr, D), lambda i: (i, 0))
    vec = pl.BlockSpec((1, D), lambda i: (0, 0))
    return _pallas(
        body, name="final_loss", grid=(S // tr,), in_specs=[row, vec, row],
        out_specs=[pl.BlockSpec((1, 1), lambda i: (0, 0)), row, vec],
        out_shape=[jax.ShapeDtypeStruct((1, 1), F32), jax.ShapeDtypeStruct((S, D), F32),
                   jax.ShapeDtypeStruct((1, D), F32)],
        compiler_params=_params(("arbitrary",)),
    )(h2, g, target)


def _sigmoid(v):
    return 1.0 / (1.0 + jnp.exp(-v))


def _merge_fwd(proj, y_ret, y_mla, D, off_gret, off_gmla, *, tr=256, tc=1024):
    S = y_ret.shape[0]
    tr, tc = _tile(S, tr), _tile(D, tc)
    b_ret, b_mla = off_gret // tc, off_gmla // tc

    def body(gr_ref, gm_ref, yr_ref, ym_ref, o_ref):
        o_ref[...] = (_sigmoid(gr_ref[...]) * yr_ref[...] + _sigmoid(gm_ref[...]) * ym_ref[...]).astype(BF16)

    blk = pl.BlockSpec((tr, tc), lambda i, j: (i, j))
    return _pallas(
        body, name="merge_fwd", grid=(S // tr, D // tc),
        in_specs=[pl.BlockSpec((tr, tc), lambda i, j: (i, b_ret + j)),
                  pl.BlockSpec((tr, tc), lambda i, j: (i, b_mla + j)), blk, blk],
        out_specs=blk, out_shape=jax.ShapeDtypeStruct((S, D), BF16),
        compiler_params=_params(("parallel", "parallel")),
    )(proj, proj, y_ret, y_mla)


def _merge_bwd(dmerged, proj, y_ret, y_mla, D, off_gret, *, tr=256):
    S = y_ret.shape[0]
    tr = _tile(S, tr)
    b0 = off_gret // D

    def body(dm_ref, g_ref, yr_ref, ym_ref, dp_ref, dyr_ref, dym_ref):
        dm = dm_ref[...]
        sg = _sigmoid(g_ref[...])

        @pl.when(pl.program_id(1) == 0)
        def _():
            dyr_ref[...] = (dm * sg).astype(BF16)
            dp_ref[...] = (dm * yr_ref[...] * sg * (1.0 - sg)).astype(BF16)

        @pl.when(pl.program_id(1) == 1)
        def _():
            dym_ref[...] = (dm * sg).astype(BF16)
            dp_ref[...] = (dm * ym_ref[...] * sg * (1.0 - sg)).astype(BF16)

    blk = pl.BlockSpec((tr, D), lambda i, j: (i, 0))
    return _pallas(
        body, name="merge_bwd", grid=(S // tr, 2),
        in_specs=[blk, pl.BlockSpec((tr, D), lambda i, j: (i, b0 + j)), blk, blk],
        out_specs=[pl.BlockSpec((tr, D), lambda i, j: (i, b0 + j)), blk, blk],
        out_shape=[jax.ShapeDtypeStruct(proj.shape, BF16), jax.ShapeDtypeStruct((S, D), BF16),
                   jax.ShapeDtypeStruct((S, D), BF16)],
        compiler_params=_params(("parallel", "arbitrary")),
    )(dmerged, proj, y_ret, y_mla)


def _rope128(t, cos_full, sin_signed):
    return t * cos_full + pltpu.roll(t, RET_QK // 2, 1) * sin_signed


def _rope128_t(d, cos_full, sin_signed):
    return d * cos_full + pltpu.roll(d * sin_signed, RET_QK // 2, 1)


def _ret_consts(lg, T):
    pos = lax.broadcasted_iota(jnp.int32, (T, 1), 0).astype(F32)
    qd = jnp.exp(lg * (pos + 1.0))
    kd = jnp.exp(lg * (T - 1.0 - pos))
    n = lax.broadcasted_iota(jnp.int32, (T, T), 0)
    m = lax.broadcasted_iota(jnp.int32, (T, T), 1)
    vis = (m // CHUNK) <= (n // CHUNK)
    dist = jnp.abs(n - m).astype(F32)
    decay = jnp.where(vis, jnp.exp(lg * dist), 0.0)
    cdec = jnp.exp(lg * float(T))
    return qd, kd, decay, cdec


def _dot(a, b, dims):
    return lax.dot_general(a.astype(BF16), b.astype(BF16), (dims, ((), ())), preferred_element_type=F32)


NN = ((1,), (0,))
NT = ((1,), (1,))
TN = ((0,), (0,))
_RQ = slice(0, RET_QK)
_RK = slice(RET_QK, 2 * RET_QK)
_RV = slice(2 * RET_QK, 2 * RET_QK + RET_V)
_RG = slice(2 * RET_QK + RET_V, RET_HEAD_COLS)


def _ret_fwd(proj, cosr, sinr, lgam, gain, RH, *, T):
    S = proj.shape[0]
    nb = S // T
    scale = RET_QK ** -0.5

    def body(p_ref, cos_ref, sin_ref, lg_ref, gain_ref, ry_ref, gated_ref, st_ref, state):
        b = pl.program_id(1)

        @pl.when(b == 0)
        def _():
            state[...] = jnp.zeros_like(state)

        lg = lg_ref[0:1, 0:1]
        qd, kd, decay, cdec = _ret_consts(lg, T)
        cosv, sinv = cos_ref[...], sin_ref[...]
        q = _rope128(p_ref[:, _RQ], cosv, sinv)
        k = _rope128(p_ref[:, _RK], cosv, sinv) * scale
        v = p_ref[:, _RV]
        sprev = state[...]
        st_ref[...] = sprev
        a = _dot(q, k, NT) * decay
        o = _dot(a, v, NN) + _dot(q * qd, sprev, NN)
        state[...] = sprev * cdec + _dot(k * kd, v, TN)
        ry_ref[...] = o
        mu = jnp.mean(o, axis=-1, keepdims=True)
        oc = o - mu
        var = jnp.mean(oc * oc, axis=-1, keepdims=True)
        t = oc * lax.rsqrt(var + EPS) * gain_ref[...]
        gv = p_ref[:, _RG]
        gated_ref[...] = (t * (gv * _sigmoid(gv))).astype(BF16)

    return _pallas(
        body, name="ret_fwd", grid=(RH, nb),
        in_specs=[pl.BlockSpec((T, RET_HEAD_COLS), lambda h, b: (b, h)),
                  pl.BlockSpec((T, RET_QK), lambda h, b: (b, 0)),
                  pl.BlockSpec((T, RET_QK), lambda h, b: (b, 0)),
                  pl.BlockSpec((None, 8, LANES), lambda h, b: (h, 0, 0)),
                  pl.BlockSpec((1, RET_V), lambda h, b: (0, h))],
        out_specs=[pl.BlockSpec((T, RET_V), lambda h, b: (b, h)),
                   pl.BlockSpec((T, RET_V), lambda h, b: (b, h)),
                   pl.BlockSpec((None, None, RET_QK, RET_V), lambda h, b: (h, b, 0, 0))],
        out_shape=[jax.ShapeDtypeStruct((S, RH * RET_V), F32), jax.ShapeDtypeStruct((S, RH * RET_V), BF16),
                   jax.ShapeDtypeStruct((RH, nb, RET_QK, RET_V), F32)],
        scratch_shapes=[pltpu.VMEM((RET_QK, RET_V), F32)],
        compiler_params=_params(("parallel", "arbitrary")),
    )(proj, cosr, sinr, lgam, gain)


def _ret_bwd(proj, cosr, sinr, lgam, gain, ry, dgated, states, dproj, RH, *, T):
    S = proj.shape[0]
    nb = S // T
    scale = RET_QK ** -0.5

    def body(p_ref, cos_ref, sin_ref, lg_ref, gain_ref, ry_ref, dg_ref, st_ref, _, dp_ref, dgain_ref, dstate):
        b = pl.program_id(1)

        @pl.when(b == 0)
        def _():
            dstate[...] = jnp.zeros_like(dstate)

        lg = lg_ref[0:1, 0:1]
        qd, kd, decay, cdec = _ret_consts(lg, T)
        cosv, sinv = cos_ref[...], sin_ref[...]
        q = _rope128(p_ref[:, _RQ], cosv, sinv)
        k = _rope128(p_ref[:, _RK], cosv, sinv) * scale
        v = p_ref[:, _RV]
        sprev = st_ref[...]
        ds_new = dstate[...]
        o = ry_ref[...]
        mu = jnp.mean(o, axis=-1, keepdims=True)
        oc = o - mu
        rstd = lax.rsqrt(jnp.mean(oc * oc, axis=-1, keepdims=True) + EPS)
        ryn = oc * rstd
        gainv = gain_ref[...]
        gv = p_ref[:, _RG]
        sg = _sigmoid(gv)
        dgt = dg_ref[...]
        dt = dgt * (gv * sg)
        dp_ref[:, _RG] = (dgt * (ryn * gainv) * (sg * (1.0 + gv * (1.0 - sg)))).astype(BF16)
        gpart = jnp.sum(dt * ryn, axis=0, keepdims=True)

        @pl.when(b == 0)
        def _():
            dgain_ref[...] = gpart

        @pl.when(b > 0)
        def _():
            dgain_ref[...] += gpart

        dryn = dt * gainv
        do = rstd * (dryn - jnp.mean(dryn, axis=-1, keepdims=True)
                     - ryn * jnp.mean(dryn * ryn, axis=-1, keepdims=True))
        a = _dot(q, k, NT) * decay
        kdk = k * kd
        qdq = q * qd
        dp_ref[:, _RV] = (_dot(a, do, TN) + _dot(kdk, ds_new, NN)).astype(BF16)
        dp = _dot(do, v, NT) * decay
        dq = _dot(dp, k, NN) + _dot(do, sprev, NT) * qd
        dk = (_dot(dp, q, TN) + _dot(v, ds_new, NT) * kd) * scale
        dstate[...] = ds_new * cdec + _dot(qdq, do, TN)
        dp_ref[:, _RQ] = _rope128_t(dq, cosv, sinv).astype(BF16)
        dp_ref[:, _RK] = _rope128_t(dk, cosv, sinv).astype(BF16)

    rb = lambda b: nb - 1 - b
    return _pallas(
        body, name="ret_bwd", grid=(RH, nb),
        in_specs=[pl.BlockSpec((T, RET_HEAD_COLS), lambda h, b: (rb(b), h)),
                  pl.BlockSpec((T, RET_QK), lambda h, b: (rb(b), 0)),
                  pl.BlockSpec((T, RET_QK), lambda h, b: (rb(b), 0)),
                  pl.BlockSpec((None, 8, LANES), lambda h, b: (h, 0, 0)),
                  pl.BlockSpec((1, RET_V), lambda h, b: (0, h)),
                  pl.BlockSpec((T, RET_V), lambda h, b: (rb(b), h)),
                  pl.BlockSpec((T, RET_V), lambda h, b: (rb(b), h)),
                  pl.BlockSpec((None, None, RET_QK, RET_V), lambda h, b: (h, rb(b), 0, 0)),
                  _ANY],
        out_specs=[pl.BlockSpec((T, RET_HEAD_COLS), lambda h, b: (rb(b), h)),
                   pl.BlockSpec((1, RET_V), lambda h, b: (0, h))],
        out_shape=[jax.ShapeDtypeStruct(dproj.shape, dproj.dtype), jax.ShapeDtypeStruct((1, RH * RET_V), F32)],
        scratch_shapes=[pltpu.VMEM((RET_QK, RET_V), F32)],
        input_output_aliases={8: 0},
        compiler_params=_params(("parallel", "arbitrary")),
    )(proj, cosr, sinr, lgam, gain, ry, dgated, states, dproj)


def _rope_pe(t, c, s1, s2):
    return t * c + pltpu.roll(t, LANES - QK_ROPE // 2, 1) * s1 + pltpu.roll(t, QK_ROPE // 2, 1) * s2


def _rope_pe_t(d, c, s1, s2):
    return d * c + pltpu.roll(d * s1, QK_ROPE // 2, 1) + pltpu.roll(d * s2, LANES - QK_ROPE // 2, 1)


def _attn_prep(q_all, kv_all, kpe, tabs, MH, *, tr=512):
    S = q_all.shape[0]
    tr = _tile(S, tr)
    c_t, s1_t, s2_t = tabs

    def body(q_ref, kv_ref, kpe_ref, c_ref, s1_ref, s2_ref, qf_ref, kf_ref, vb_ref):
        c, s1, s2 = c_ref[...], s1_ref[...], s2_ref[...]
        qf_ref[:, :QK_NOPE] = q_ref[:, :QK_NOPE].astype(BF16)
        qf_ref[:, QK_NOPE:] = _rope_pe(q_ref[:, QK_NOPE:], c, s1, s2).astype(BF16)
        kf_ref[:, :QK_NOPE] = kv_ref[:, :QK_NOPE].astype(BF16)
        kf_ref[:, QK_NOPE:] = _rope_pe(kpe_ref[...], c, s1, s2).astype(BF16)
        vb_ref[...] = kv_ref[:, QK_NOPE:].astype(BF16)

    tab = pl.BlockSpec((tr, LANES), lambda i, h: (i, 0))
    head2 = pl.BlockSpec((tr, 2 * LANES), lambda i, h: (i, h))
    return _pallas(
        body, name="attn_prep", grid=(S // tr, MH),
        in_specs=[head2, head2, tab, tab, tab, tab],
        out_specs=[head2, head2, pl.BlockSpec((tr, LANES), lambda i, h: (i, h))],
        out_shape=[jax.ShapeDtypeStruct((S, MH * 2 * LANES), BF16), jax.ShapeDtypeStruct((S, MH * 2 * LANES), BF16),
                   jax.ShapeDtypeStruct((S, MH * LANES), BF16)],
        compiler_params=_params(("parallel", "parallel")),
    )(q_all, kv_all, kpe, c_t, s1_t, s2_t)


def _chunk_mask(T):
    n = lax.broadcasted_iota(jnp.int32, (T, T), 0)
    m = lax.broadcasted_iota(jnp.int32, (T, T), 1)
    return (m // CHUNK) <= (n // CHUNK)


def _lanes_to(v, width):
    return jnp.tile(v, (1, width // LANES))


def _attn_fwd(qf, kf, vb, MH, *, T):
    S = qf.shape[0]
    nt = S // T
    c2 = (QK_NOPE + QK_ROPE) ** -0.5 * LOG2E

    def body(q_ref, k_ref, v_ref, o_ref, lse_ref, m_sc, l_sc, acc_sc):
        qi = pl.program_id(1)
        m_sc[...] = jnp.full_like(m_sc, NEG)
        l_sc[...] = jnp.zeros_like(l_sc)
        acc_sc[...] = jnp.zeros_like(acc_sc)
        q = q_ref[...]

        def tile(kt, masked):
            rows = pl.ds(pl.multiple_of(kt * T, T), T)
            s = _dot(q, k_ref[rows, :], NT) * c2
            if masked:
                s = jnp.where(_chunk_mask(T), s, NEG)
            m_prev = m_sc[...]
            m_new = jnp.maximum(m_prev, jnp.max(s, axis=-1, keepdims=True))
            alpha = jnp.exp2(m_prev - m_new)
            p = jnp.exp2(s - _lanes_to(m_new, T))
            l_sc[...] = alpha * l_sc[...] + jnp.sum(p, axis=-1, keepdims=True)
            acc_sc[...] = alpha * acc_sc[...] + _dot(p, v_ref[rows, :], NN)
            m_sc[...] = m_new

        def unmasked(kt, carry):
            tile(kt, False)
            return carry

        lax.fori_loop(0, qi, unmasked, 0)
        tile(qi, True)
        l = l_sc[...]
        o_ref[...] = acc_sc[...] / l
        lse_ref[...] = m_sc[...] + jnp.log(l) * LOG2E

    return _pallas(
        body, name="attn_fwd", grid=(MH, nt),
        in_specs=[pl.BlockSpec((T, 2 * LANES), lambda h, i: (i, h)),
                  pl.BlockSpec((S, 2 * LANES), lambda h, i: (0, h)),
                  pl.BlockSpec((S, LANES), lambda h, i: (0, h))],
        out_specs=[pl.BlockSpec((T, LANES), lambda h, i: (i, h)),
                   pl.BlockSpec((None, T, LANES), lambda h, i: (h, i, 0))],
        out_shape=[jax.ShapeDtypeStruct((S, MH * LANES), F32), jax.ShapeDtypeStruct((MH, S, LANES), F32)],
        scratch_shapes=[pltpu.VMEM((T, LANES), F32), pltpu.VMEM((T, LANES), F32), pltpu.VMEM((T, LANES), F32)],
        compiler_params=_params(("parallel", "parallel")),
    )(qf, kf, vb)


def _attn_delta(do, o, MH, *, deps=(), tr=512):
    S = do.shape[0]
    tr = _tile(S, tr)

    def body(do_ref, o_ref, *rest):
        d_ref, dob_ref = rest[-2], rest[-1]
        dov = do_ref[...]
        d_ref[...] = jnp.broadcast_to(jnp.sum(dov * o_ref[...], axis=-1, keepdims=True), (tr, LANES))
        dob_ref[...] = dov.astype(BF16)

    head = pl.BlockSpec((tr, LANES), lambda i, h: (i, h))
    return _pallas(
        body, name="attn_delta", grid=(S // tr, MH), in_specs=[head, head] + [_ANY] * len(deps),
        out_specs=[pl.BlockSpec((None, tr, LANES), lambda i, h: (h, i, 0)), head],
        out_shape=[jax.ShapeDtypeStruct((MH, S, LANES), F32), jax.ShapeDtypeStruct((S, MH * LANES), BF16)],
        compiler_params=_params(("parallel", "parallel")),
    )(do, o, *deps)


def _attn_bwd(qf, kf, vb, dob, lse2, delta, MH, *, T):
    S = qf.shape[0]
    nt = S // T
    scale = (QK_NOPE + QK_ROPE) ** -0.5
    c2 = scale * LOG2E

    def body(q_ref, k_ref, v_ref, do_ref, lse_ref, dl_ref, dq_ref, dk_ref, dv_ref, dk_sc, dv_sc):
        kj = pl.program_id(1)

        @pl.when(kj == 0)
        def _():
            dq_ref[...] = jnp.zeros_like(dq_ref)

        dk_sc[...] = jnp.zeros_like(dk_sc)
        dv_sc[...] = jnp.zeros_like(dv_sc)
        k, v = k_ref[...], v_ref[...]

        def tile(qt, masked):
            rows = pl.ds(pl.multiple_of(qt * T, T), T)
            q, dov = q_ref[rows, :], do_ref[rows, :]
            s = _dot(q, k, NT) * c2
            if masked:
                s = jnp.where(_chunk_mask(T), s, NEG)
            p = jnp.exp2(s - _lanes_to(lse_ref[rows, :], T))
            dp = _dot(dov, v, NT)
            ds = p * (dp - _lanes_to(dl_ref[rows, :], T)) * scale
            dv_sc[...] += _dot(p, dov, TN)
            dk_sc[...] += _dot(ds, q, TN)
            dq_ref[rows, :] += _dot(ds, k, NN)

        def unmasked(qt, carry):
            tile(qt, False)
            return carry

        tile(kj, True)
        lax.fori_loop(kj + 1, nt, unmasked, 0)
        dk_ref[...] = dk_sc[...]
        dv_ref[...] = dv_sc[...].astype(BF16)

    stat = pl.BlockSpec((None, S, LANES), lambda h, j: (h, 0, 0))
    return _pallas(
        body, name="attn_bwd", grid=(MH, nt),
        in_specs=[pl.BlockSpec((S, 2 * LANES), lambda h, j: (0, h)),
                  pl.BlockSpec((T, 2 * LANES), lambda h, j: (j, h)),
                  pl.BlockSpec((T, LANES), lambda h, j: (j, h)),
                  pl.BlockSpec((S, LANES), lambda h, j: (0, h)), stat, stat],
        out_specs=[pl.BlockSpec((S, 2 * LANES), lambda h, j: (0, h)),
                   pl.BlockSpec((T, 2 * LANES), lambda h, j: (j, h)),
                   pl.BlockSpec((T, LANES), lambda h, j: (j, h))],
        out_shape=[jax.ShapeDtypeStruct((S, MH * 2 * LANES), F32), jax.ShapeDtypeStruct((S, MH * 2 * LANES), F32),
                   jax.ShapeDtypeStruct((S, MH * LANES), BF16)],
        scratch_shapes=[pltpu.VMEM((T, 2 * LANES), F32), pltpu.VMEM((T, LANES), F32)],
        compiler_params=_params(("parallel", "arbitrary")),
    )(qf, kf, vb, dob, lse2, delta)


def _attn_post(dqf, dkf, dvb, tabs, MH, *, tr=512):
    S = dqf.shape[0]
    tr = _tile(S, tr)
    c_t, s1_t, s2_t = tabs

    def body(dq_ref, dk_ref, dv_ref, c_ref, s1_ref, s2_ref, dqa_ref, dkv_ref, dkpe_ref, acc):
        h = pl.program_id(1)
        c, s1, s2 = c_ref[...], s1_ref[...], s2_ref[...]
        dqa_ref[:, :QK_NOPE] = dq_ref[:, :QK_NOPE].astype(BF16)
        dqa_ref[:, QK_NOPE:] = _rope_pe_t(dq_ref[:, QK_NOPE:], c, s1, s2).astype(BF16)
        dkv_ref[:, :QK_NOPE] = dk_ref[:, :QK_NOPE].astype(BF16)
        dkv_ref[:, QK_NOPE:] = dv_ref[...]

        @pl.when(h == 0)
        def _():
            acc[...] = dk_ref[:, QK_NOPE:]

        @pl.when(h > 0)
        def _():
            acc[...] += dk_ref[:, QK_NOPE:]

        @pl.when(h == MH - 1)
        def _():
            dkpe_ref[...] = _rope_pe_t(acc[...], c, s1, s2).astype(BF16)

    tab = pl.BlockSpec((tr, LANES), lambda i, h: (i, 0))
    head2 = pl.BlockSpec((tr, 2 * LANES), lambda i, h: (i, h))
    return _pallas(
        body, name="attn_post", grid=(S // tr, MH),
        in_specs=[head2, head2, pl.BlockSpec((tr, LANES), lambda i, h: (i, h)), tab, tab, tab],
        out_specs=[head2, head2, tab],
        out_shape=[jax.ShapeDtypeStruct((S, MH * 2 * LANES), BF16)] * 2 + [jax.ShapeDtypeStruct((S, LANES), BF16)],
        scratch_shapes=[pltpu.VMEM((tr, LANES), F32)],
        compiler_params=_params(("parallel", "arbitrary")),
    )(dqf, dkf, dvb, c_t, s1_t, s2_t)


def _block_rows(R, C, block_bytes=2 << 20):
    tr = 8
    while tr * 2 * C * 4 <= block_bytes:
        tr *= 2
    return _tile(R, tr)


def _rows_call(fn, ins, out_dtypes, *, name):
    R, C = ins[0].shape
    tr = _block_rows(R, C)
    n_in = len(ins)

    def body(*refs):
        vals = fn(*[r[...] for r in refs[:n_in]])
        for r, v in zip(refs[n_in:], vals):
            r[...] = v.astype(r.dtype)

    blk = pl.BlockSpec((tr, C), lambda i: (i, 0))
    res = _pallas(
        body, name=name, grid=(R // tr,), in_specs=[blk] * n_in, out_specs=[blk] * len(out_dtypes),
        out_shape=[jax.ShapeDtypeStruct((R, C), d) for d in out_dtypes],
        compiler_params=_params(("parallel",)),
    )(*ins)
    return res


def _adamw_vals(w, g, m, v):
    m = ADAM_B1 * m + (1.0 - ADAM_B1) * g
    v = ADAM_B2 * v + (1.0 - ADAM_B2) * (g * g)
    m_hat = m / (1.0 - ADAM_B1 ** ADAM_STEP)
    v_hat = v / (1.0 - ADAM_B2 ** ADAM_STEP)
    delta = -ADAM_LR * (m_hat / (jnp.sqrt(v_hat) + ADAM_EPS) + ADAM_WD * w)
    return delta, m, v


def _sum_pair(p, theirs, place, *, name):
    _, R, C = p.shape
    R2 = R // 2
    tr = _block_rows(R2, C)
    p4 = p.reshape(N_CHIPS, 2, R2, C)

    def body(place_ref, a_ref, b_ref, o_ref):
        o_ref[...] = (a_ref[...] + b_ref[...]).astype(BF16)

    spec = pltpu.PrefetchScalarGridSpec(
        num_scalar_prefetch=1, grid=(N_CHIPS, R2 // tr),
        in_specs=[pl.BlockSpec((None, None, tr, C), lambda q, i, pr: (q, pr[0], i, 0)),
                  pl.BlockSpec((None, tr, C), lambda q, i, pr: (q, i, 0))],
        out_specs=pl.BlockSpec((None, tr, C), lambda q, i, pr: (q, i, 0)))
    return _pallas(body, name=name, grid_spec=spec, out_shape=jax.ShapeDtypeStruct((N_CHIPS, R2, C), BF16),
                   compiler_params=_params(("parallel", "parallel")))(place, p4, theirs)


def _sum_chips(p, theirs, recv, place, *, name):
    _, R, C = p.shape
    R2 = R // 2
    tr = _block_rows(R2, C)
    p4 = p.reshape(N_CHIPS, 2, R2, C)

    def body(place_ref, a_ref, b_ref, r0_ref, r1_ref, r2_ref, o_ref):
        own = a_ref[...] + b_ref[...]
        o_ref[...] = ((own + r0_ref[...].astype(F32)) + r1_ref[...].astype(F32)) + r2_ref[...].astype(F32)

    def slot(k):
        return pl.BlockSpec((None, tr, C), lambda i, pr: (k, i, 0))

    spec = pltpu.PrefetchScalarGridSpec(
        num_scalar_prefetch=1, grid=(R2 // tr,),
        in_specs=[pl.BlockSpec((None, None, tr, C), lambda i, pr: (pr[1], pr[0], i, 0)),
                  pl.BlockSpec((None, tr, C), lambda i, pr: (pr[1], i, 0)), slot(0), slot(1), slot(2)],
        out_specs=pl.BlockSpec((None, tr, C), lambda i, pr: (pr[0], i, 0)))
    return _pallas(body, name=name, grid_spec=spec, out_shape=jax.ShapeDtypeStruct((2, R2, C), F32),
                   compiler_params=_params(("parallel",)))(place, p4, theirs, recv, recv, recv)


def _me():
    return lax.axis_index("x"), lax.axis_index("y"), lax.axis_index("c")


def _other_chips(x, y):
    return [(1 - x, y), (x, 1 - y), (1 - x, 1 - y)]


def _rcopy(src, dst, ssem, rsem, dev):
    return pltpu.make_async_remote_copy(src_ref=src, dst_ref=dst, send_sem=ssem, recv_sem=rsem,
                                        device_id=dev, device_id_type=MESH)


def _cast_into_slot(w, place, *, name, deps=()):
    R, C = w.shape
    tr = _block_rows(R, C)

    def body(place_ref, w_ref, *rest):
        rest[-1][...] = w_ref[...].astype(BF16)

    spec = pltpu.PrefetchScalarGridSpec(
        num_scalar_prefetch=1, grid=(R // tr,),
        in_specs=[pl.BlockSpec((tr, C), lambda i, pr: (i, 0))] + [_ANY] * len(deps),
        out_specs=pl.BlockSpec((None, tr, C), lambda i, pr: (pr[1], i, 0)))
    out = _pallas(body, name=name, grid_spec=spec, out_shape=jax.ShapeDtypeStruct((N_CHIPS, R, C), BF16),
                  compiler_params=_params(("parallel",)))(place, w, *deps)
    return out.reshape(N_CHIPS, 2, R // 2, C)


def _gather_ici_plan(bufs):
    x, y, c = _me()
    j = 2 * x + y
    plan = []
    for i, buf in enumerate(bufs):
        for k, (px, py) in enumerate(_other_chips(x, y)):
            plan.append((3 * i + k, buf.at[j, c], buf.at[j, c], (px, py, c)))
    return plan


def _gather_weights(bufs, *, name):
    n = len(bufs)

    def body(*refs):
        outs = refs[n:2 * n]
        ssem, rsem, fssem, frsem = refs[2 * n:]
        x, y, c = _me()
        sib = (x, y, 1 - c)
        chips = _other_chips(x, y)
        remote = []
        for s, src, dst, dev in _gather_ici_plan(outs):
            r = _rcopy(src, dst, ssem.at[s], rsem.at[s], dev)
            r.start()
            remote.append(r)
        for i in range(n):
            for k, (px, py) in enumerate(chips):
                slot = outs[i].at[2 * px + py, c]
                _rcopy(slot, slot, ssem.at[3 * i + k], rsem.at[3 * i + k], (px, py, c)).wait_recv()
                f = _rcopy(slot, slot, fssem.at[3 * i + k], frsem.at[3 * i + k], sib)
                f.start()
                remote.append(f)
        for i in range(n):
            for k, (px, py) in enumerate(chips):
                slot = outs[i].at[2 * px + py, 1 - c]
                _rcopy(slot, slot, fssem.at[3 * i + k], frsem.at[3 * i + k], sib).wait_recv()
        for r in remote:
            r.wait_send()

    return _pallas(
        body, name=name, in_specs=[_ANY] * n, out_specs=[_ANY] * n,
        out_shape=[jax.ShapeDtypeStruct(b.shape, b.dtype) for b in bufs],
        scratch_shapes=[pltpu.SemaphoreType.DMA((3 * n,))] * 4,
        input_output_aliases={i: i for i in range(n)},
        compiler_params=pltpu.CompilerParams(has_side_effects=True),
    )(*bufs)


def _forward_halves(bufs, *, name):
    n = len(bufs)

    def body(*refs):
        outs = refs[n:2 * n]
        ssem, rsem = refs[2 * n:]
        x, y, c = _me()
        sib = (x, y, 1 - c)
        cps = []
        for i in range(n):
            for k, (px, py) in enumerate(_other_chips(x, y)):
                slot = outs[i].at[2 * px + py, c]
                r = _rcopy(slot, slot, ssem.at[3 * i + k], rsem.at[3 * i + k], sib)
                r.start()
                cps.append(r)
        for r in cps:
            r.wait()

    return _pallas(
        body, name=name, in_specs=[_ANY] * n, out_specs=[_ANY] * n,
        out_shape=[jax.ShapeDtypeStruct(b.shape, b.dtype) for b in bufs],
        scratch_shapes=[pltpu.SemaphoreType.DMA((3 * n,))] * 2,
        input_output_aliases={i: i for i in range(n)},
        compiler_params=pltpu.CompilerParams(has_side_effects=True),
    )(*bufs)


_HBM = pl.BlockSpec(memory_space=pltpu.HBM)
_SEM = pl.BlockSpec(memory_space=pltpu.SEMAPHORE)
_EFFECT = pltpu.SideEffectType.DATAFLOW_SIDE_EFFECTING


def _split_start(bufs, plan, n_copies, *, name):
    n = len(bufs)

    def body(*refs):
        ssem, rsem = refs[n], refs[n + 1]
        for s, src, dst, dev in plan(refs[:n]):
            _rcopy(src, dst, ssem.at[s], rsem.at[s], dev).start()
        refs[-1][...] = jnp.zeros_like(refs[-1])

    res = _pallas(
        body, name=name, in_specs=[_HBM] * n,
        out_specs=(_SEM, _SEM, *[_HBM] * n, pl.BlockSpec(memory_space=pltpu.VMEM)),
        out_shape=(pltpu.SemaphoreType.DMA((n_copies,)), pltpu.SemaphoreType.DMA((n_copies,)),
                   *[pltpu.HBM(b.shape, b.dtype) for b in bufs], jax.ShapeDtypeStruct((8, LANES), F32)),
        input_output_aliases={i: 2 + i for i in range(n)},
        compiler_params=pltpu.CompilerParams(has_side_effects=_EFFECT),
    )(*[pltpu.with_memory_space_constraint(b, pltpu.HBM) for b in bufs])
    return res[0], res[1], list(res[2:2 + n]), res[-1]


def _split_wait(ssem, rsem, bufs, after, plan, *, name):
    n = len(bufs)

    def body(*refs):
        ssem_ref, rsem_ref = refs[n], refs[n + 1]
        for s, src, dst, dev in plan(refs[:n]):
            cp = _rcopy(src, dst, ssem_ref.at[s], rsem_ref.at[s], dev)
            cp.wait_send()
            cp.wait_recv()

    return list(_pallas(
        body, name=name, in_specs=[_HBM] * n + [_SEM, _SEM, _ANY], out_specs=[_HBM] * n,
        out_shape=[pltpu.HBM(b.shape, b.dtype) for b in bufs],
        input_output_aliases={i: i for i in range(n)},
        compiler_params=pltpu.CompilerParams(has_side_effects=_EFFECT),
    )(*bufs, ssem, rsem, after))


def _scatter_plan(n):
    def plan(bufs):
        x, y, c = _me()
        out = []
        for i in range(n):
            for k, (px, py) in enumerate(_other_chips(x, y)):
                out.append((3 * i + k, bufs[i].at[2 * px + py], bufs[n + i].at[k], (px, py, c)))
        return out
    return plan


def _swap_halves(grads, *, name):
    n = len(grads)
    views = [g.reshape(N_CHIPS, 2, g.shape[1] // 2, g.shape[2]) for g in grads]

    def body(*refs):
        ins, outs = refs[:n], refs[n:2 * n]
        ssem, rsem = refs[2 * n:]
        x, y, c = _me()
        sib = (x, y, 1 - c)
        cps = []
        for i in range(n):
            r = _rcopy(ins[i].at[:, 1 - c], outs[i], ssem.at[i], rsem.at[i], sib)
            r.start()
            cps.append(r)
        for r in cps:
            r.wait()

    return _pallas(
        body, name=name, in_specs=[_ANY] * n, out_specs=[_ANY] * n,
        out_shape=[jax.ShapeDtypeStruct((N_CHIPS,) + v.shape[2:], v.dtype) for v in views],
        scratch_shapes=[pltpu.SemaphoreType.DMA((n,)), pltpu.SemaphoreType.DMA((n,))],
        compiler_params=pltpu.CompilerParams(has_side_effects=True),
    )(*views)


def _scatter_chips(sums, *, name):
    n = len(sums)

    def body(*refs):
        ins, outs = refs[:n], refs[n:2 * n]
        ssem, rsem = refs[2 * n:]
        x, y, c = _me()
        chips = _other_chips(x, y)
        remote = []
        for i in range(n):
            for k, (px, py) in enumerate(chips):
                r = _rcopy(ins[i].at[2 * px + py], outs[i].at[k], ssem.at[i, k], rsem.at[i, k], (px, py, c))
                r.start()
                remote.append(r)
        for r in remote:
            r.wait()

    return _pallas(
        body, name=name, in_specs=[_ANY] * n, out_specs=[_ANY] * n,
        out_shape=[jax.ShapeDtypeStruct((3,) + s.shape[1:], s.dtype) for s in sums],
        scratch_shapes=[pltpu.SemaphoreType.DMA((n, 3)), pltpu.SemaphoreType.DMA((n, 3))],
        compiler_params=pltpu.CompilerParams(has_side_effects=True),
    )(*sums)


def _join_halves(halves, *, name):
    n = len(halves)

    def body(*refs):
        outs = refs[n:2 * n]
        ssem, rsem = refs[2 * n:]
        x, y, c = _me()
        sib = (x, y, 1 - c)
        cps = []
        for i in range(n):
            r = _rcopy(outs[i].at[c], outs[i].at[c], ssem.at[i], rsem.at[i], sib)
            r.start()
            cps.append(r)
        for r in cps:
            r.wait()

    return _pallas(
        body, name=name, in_specs=[_ANY] * n, out_specs=[_ANY] * n,
        out_shape=[jax.ShapeDtypeStruct(h.shape, h.dtype) for h in halves],
        scratch_shapes=[pltpu.SemaphoreType.DMA((n,)), pltpu.SemaphoreType.DMA((n,))],
        input_output_aliases={i: i for i in range(n)},
        compiler_params=pltpu.CompilerParams(has_side_effects=True),
    )(*halves)


def _allreduce_small(vec):
    R = vec.shape[0]

    def body(v_ref, o_ref, buf, ssem, rsem):
        x, y, c = _me()
        me = 4 * x + 2 * y + c
        buf[me] = v_ref[...]
        cps = []
        for k in range(1, 8):
            peer = (x ^ (k >> 2), y ^ ((k >> 1) & 1), c ^ (k & 1))
            r = _rcopy(v_ref, buf.at[me], ssem.at[k - 1], rsem.at[k - 1], peer)
            r.start()
            cps.append(r)
        for k in range(1, 8):
            peer = (x ^ (k >> 2), y ^ ((k >> 1) & 1), c ^ (k & 1))
            pid = 4 * peer[0] + 2 * peer[1] + peer[2]
            _rcopy(v_ref, buf.at[pid], ssem.at[k - 1], rsem.at[k - 1], peer).wait_recv()
        for r in cps:
            r.wait_send()
        tot = buf[0]
        for d in range(1, 8):
            tot = tot + buf[d]
        o_ref[...] = tot

    vm = pl.BlockSpec(memory_space=pltpu.VMEM)
    return _pallas(
        body, name="allreduce_small", in_specs=[vm], out_specs=vm,
        out_shape=jax.ShapeDtypeStruct((R, LANES), F32),
        scratch_shapes=[pltpu.VMEM((8, R, LANES), F32), pltpu.SemaphoreType.DMA((7,)), pltpu.SemaphoreType.DMA((7,))],
        compiler_params=pltpu.CompilerParams(has_side_effects=True),
    )(vec)


def _rope_tables(positions, S):
    pos = positions.reshape(S, 1).astype(F32)
    half = RET_QK // 2
    inv = ROPE_THETA ** (-jnp.arange(half, dtype=F32) / half)
    ang = pos * inv
    cosr = jnp.concatenate([jnp.cos(ang), jnp.cos(ang)], axis=1)
    sinr = jnp.concatenate([-jnp.sin(ang), jnp.sin(ang)], axis=1)
    half = QK_ROPE // 2
    inv = ROPE_THETA ** (-jnp.arange(half, dtype=F32) / half)
    ang = pos * inv
    z = jnp.zeros((S, half), F32)
    c = jnp.concatenate([jnp.cos(ang), jnp.cos(ang), z, z], axis=1)
    s1 = jnp.concatenate([-jnp.sin(ang), z, z, z], axis=1)
    s2 = jnp.concatenate([z, jnp.sin(ang), z, z], axis=1)
    return cosr, sinr, (c, s1, s2)


def _cat_cols(g):
    return jnp.concatenate([g[j] for j in range(N_CHIPS)], axis=1)


def _split_cols(w):
    return jnp.stack(jnp.split(w, N_CHIPS, axis=1))


def _pack_small(vs, rows):
    flat = jnp.concatenate([v.reshape(-1) for v in vs])
    flat = jnp.pad(flat, (0, rows * LANES - flat.shape[0]))
    return flat.reshape(rows, LANES)


def kernel(x, positions, norm_mix_g, w_in, ret_norm_g, w_ret_o, q_a_norm_g, w_q_b, kv_a_norm_g, w_kv_b, w_mla_o, w_out, norm_mlp_g, w_up, w_down, norm_f_g, loss_target, m_norm_mix_g, m_w_in, m_ret_norm_g, m_w_ret_o, m_q_a_norm_g, m_w_q_b, m_kv_a_norm_g, m_w_kv_b, m_w_mla_o, m_w_out, m_norm_mlp_g, m_w_up, m_w_down, m_norm_f_g, v_norm_mix_g, v_w_in, v_ret_norm_g, v_w_ret_o, v_q_a_norm_g, v_w_q_b, v_kv_a_norm_g, v_w_kv_b, v_w_mla_o, v_w_out, v_norm_mlp_g, v_w_up, v_w_down, v_norm_f_g):
    S, D = x.shape[1], x.shape[2]
    RVW = w_ret_o.shape[1] * N_CHIPS
    RH = RVW // RET_V
    RQW = RH * RET_QK
    MVW = w_mla_o.shape[1] * N_CHIPS
    MH = MVW // V_HEAD
    QL, KVL = w_q_b.shape[1], w_kv_b.shape[1]
    T_RET = _tile(S, 256)
    T_ATT = _tile(S, 512)

    xs = x.reshape(S, D)
    tgt = loss_target.reshape(S, D)
    cosr, sinr, pe_tabs = _rope_tables(positions, S)
    lgam = jnp.log(1.0 - 2.0 ** (-5.0 - jnp.arange(RH, dtype=F32)))
    lgam = jnp.broadcast_to(lgam[:, None, None], (RH, 8, LANES))

    big = ("w_in", "w_ret_o", "w_q_b", "w_kv_b", "w_mla_o", "w_out", "w_up", "w_down")
    w_sh = dict(w_in=w_in[0], w_ret_o=w_ret_o[0], w_q_b=w_q_b[0], w_kv_b=w_kv_b[0], w_mla_o=w_mla_o[0],
                w_out=w_out[0], w_up=w_up[0], w_down=w_down[0])
    m_sh = dict(w_in=m_w_in[0], w_ret_o=m_w_ret_o[0], w_q_b=m_w_q_b[0], w_kv_b=m_w_kv_b[0], w_mla_o=m_w_mla_o[0],
                w_out=m_w_out[0], w_up=m_w_up[0], w_down=m_w_down[0])
    v_sh = dict(w_in=v_w_in[0], w_ret_o=v_w_ret_o[0], w_q_b=v_w_q_b[0], w_kv_b=v_w_kv_b[0], w_mla_o=v_w_mla_o[0],
                w_out=v_w_out[0], w_up=v_w_up[0], w_down=v_w_down[0])
    col_sharded = ("w_in", "w_q_b", "w_kv_b", "w_up")
    place = jnp.stack([lax.axis_index("c"), 2 * lax.axis_index("x") + lax.axis_index("y")]).astype(jnp.int32)

    def whole(k, g):
        g = g.reshape(N_CHIPS, w_sh[k].shape[0], w_sh[k].shape[1])
        return _cat_cols(g) if k in col_sharded else g.reshape(-1, g.shape[2])

    first = ("w_in", "w_q_b", "w_kv_b")
    later = ("w_ret_o", "w_mla_o", "w_out", "w_up", "w_down")
    got = _gather_weights([_cast_into_slot(w_sh[k], place, name="cast_" + k) for k in first], name="gather_first")
    full = {k: whole(k, g) for k, g in zip(first, got)}
    later_bufs = [_cast_into_slot(w_sh[k], place, name="cast_" + k, deps=(got[0],)) for k in later]
    later_ssem, later_rsem, later_bufs, later_token = _split_start(
        later_bufs, _gather_ici_plan, 3 * len(later), name="gather_later_start")

    o_rq, o_rk, o_rv, o_rg = 0, RQW, 2 * RQW, 2 * RQW + RVW
    o_cq = 2 * RQW + 2 * RVW
    o_ckv, o_kpe = o_cq + QL, o_cq + QL + KVL
    o_gr = o_kpe + QK_ROPE
    o_gm = o_gr + D
    n_ret = RH * RET_HEAD_COLS
    off_gret, off_gmla, off_cq, off_ckv = n_ret, n_ret + D, n_ret + 2 * D, n_ret + 2 * D + QL
    wi = full["w_in"]
    ret_cols = jnp.concatenate([wi[:, o_rq:o_rk].reshape(D, RH, RET_QK), wi[:, o_rk:o_rv].reshape(D, RH, RET_QK),
                                wi[:, o_rv:o_rg].reshape(D, RH, RET_V), wi[:, o_rg:o_cq].reshape(D, RH, RET_V)],
                               axis=2).reshape(D, n_ret)
    wa = jnp.concatenate([ret_cols, wi[:, o_gr:o_gm], wi[:, o_gm:], wi[:, o_cq:o_ckv], wi[:, o_ckv:o_kpe]], axis=1)
    wkpe = jnp.pad(wi[:, o_kpe:o_gr], ((0, 0), (0, LANES - QK_ROPE)))
    wq = jnp.pad(full["w_q_b"].reshape(QL, MH, QK_NOPE + QK_ROPE),
                 ((0, 0), (0, 0), (0, LANES - QK_ROPE))).reshape(QL, MH * 2 * LANES)
    wkv = full["w_kv_b"]

    u, rstd0 = _rmsnorm_fwd(xs, norm_mix_g, name="norm_mix")
    proj = _mm(u, wa, mode="nn", outs=[F32], name="in_proj", deps=(later_token,))
    kpe = _mm(u, wkpe, mode="nn", outs=[F32], name="kpe_proj")
    ry, gated, states = _ret_fwd(proj, cosr, sinr, lgam, ret_norm_g, RH, T=T_RET)
    cqn, rstd_q = _rmsnorm_fwd(proj, q_a_norm_g, name="norm_q", width=QL, col=off_cq // QL)
    ckvn, rstd_kv = _rmsnorm_fwd(proj, kv_a_norm_g, name="norm_kv", width=KVL, col=off_ckv // KVL)
    q_all = _mm(cqn, wq, mode="nn", outs=[F32], name="q_proj")
    kv_all = _mm(ckvn, wkv, mode="nn", outs=[F32], name="kv_proj")
    qf, kf, vb = _attn_prep(q_all, kv_all, kpe, pe_tabs, MH)
    my, lse2 = _attn_fwd(qf, kf, vb, MH, T=T_ATT)
    later_bufs = _split_wait(later_ssem, later_rsem, later_bufs, my, _gather_ici_plan, name="gather_later_wait")
    later_bufs = _forward_halves(later_bufs, name="gather_later_forward")
    full.update({k: whole(k, g) for k, g in zip(later, later_bufs)})
    y_ret = _mm(gated, full["w_ret_o"], mode="nn", outs=[F32], name="ret_o")
    y_mla = _mm(my, full["w_mla_o"], mode="nn", outs=[F32], name="mla_o")
    merged = _merge_fwd(proj, y_ret, y_mla, D, off_gret, off_gmla)
    h1 = _mm(merged, full["w_out"], mode="nn", outs=[F32], name="out_proj",
             epi=lambda acc, r: (acc + r,), extras=(xs,))
    n1, rstd1 = _rmsnorm_fwd(h1, norm_mlp_g, name="norm_mlp")

    def up_epi(acc):
        r = jnp.maximum(acc, 0.0)
        return acc, r * r

    z, act = _mm(n1, full["w_up"], mode="nn", outs=[F32, BF16], name="up_proj", epi=up_epi)
    h2 = _mm(act, full["w_down"], mode="nn", outs=[F32], name="down_proj",
             epi=lambda acc, r: (acc + r,), extras=(h1,))
    loss11, dh2, g_norm_f = _final_loss(h2, norm_f_g.reshape(1, D), tgt)

    dz = _mm(dh2, full["w_down"], mode="nt", outs=[BF16], name="down_bwd_x",
             epi=lambda acc, zz: (acc * (2.0 * jnp.maximum(zz, 0.0)),), extras=(z,))
    g_w_down = _mm(act, dh2, mode="tn", outs=[F32], name="down_bwd_w")
    dn1 = _mm(dz, full["w_up"], mode="nt", outs=[F32], name="up_bwd_x")
    g_w_up = _mm(n1, dz, mode="tn", outs=[F32], name="up_bwd_w")

    def reduce_begin(tag, names, grads):
        pcs = [_split_cols(g) if k in col_sharded else g.reshape(N_CHIPS, g.shape[0] // N_CHIPS, g.shape[1])
               for k, g in zip(names, grads)]
        theirs = _swap_halves(pcs, name="swap_" + tag)
        sums = [_sum_pair(p, t, place, name="sum_pair_" + k) for k, p, t in zip(names, pcs, theirs)]
        return pcs, theirs, sums

    def scatter_begin(tag, sums):
        lands = [lax.empty((3,) + s.shape[1:], s.dtype) for s in sums]
        return _split_start(sums + lands, _scatter_plan(len(sums)), 3 * len(sums), name="scatter_" + tag + "_start")

    g1 = ("w_up", "w_down")
    pcs1, theirs1, sums1 = reduce_begin("g1", g1, (g_w_up, g_w_down))
    ssem1, rsem1, bufs1, token1 = scatter_begin("g1", sums1)
    dh1, g_norm_mlp = _rmsnorm_bwd(dn1, h1, rstd1, norm_mlp_g, name="norm_mlp_bwd", res=dh2, deps=(token1,))
    dmerged = _mm(dh1, full["w_out"], mode="nt", outs=[F32], name="out_bwd_x")
    g_w_out = _mm(merged, dh1, mode="tn", outs=[F32], name="out_bwd_w")
    dproj, dy_ret, dy_mla = _merge_bwd(dmerged, proj, y_ret, y_mla, D, off_gret)
    dgated = _mm(dy_ret, full["w_ret_o"], mode="nt", outs=[F32], name="ret_o_bwd_x")
    g_w_ret_o = _mm(gated, dy_ret, mode="tn", outs=[F32], name="ret_o_bwd_w")
    dproj, g_ret_norm = _ret_bwd(proj, cosr, sinr, lgam, ret_norm_g, ry, dgated, states, dproj, RH, T=T_RET)
    dmy = _mm(dy_mla, full["w_mla_o"], mode="nt", outs=[F32], name="mla_o_bwd_x")
    g_w_mla_o = _mm(my, dy_mla, mode="tn", outs=[F32], name="mla_o_bwd_w")
    g2 = ("w_out", "w_ret_o", "w_mla_o")
    pcs2, theirs2, sums2 = reduce_begin("g2", g2, (g_w_out, g_w_ret_o, g_w_mla_o))
    ssem2, rsem2, bufs2, token2 = scatter_begin("g2", sums2)
    delta, dob = _attn_delta(dmy, my, MH, deps=(token2,))
    dqf, dkf, dvb = _attn_bwd(qf, kf, vb, dob, lse2, delta, MH, T=T_ATT)
    dq_all, dkv_all, dkpe = _attn_post(dqf, dkf, dvb, pe_tabs, MH)
    dcqn = _mm(dq_all, wq, mode="nt", outs=[F32], name="q_bwd_x")
    g_wq = _mm(cqn, dq_all, mode="tn", outs=[F32], name="q_bwd_w")
    dckvn = _mm(dkv_all, wkv, mode="nt", outs=[F32], name="kv_bwd_x")
    g_wkv = _mm(ckvn, dkv_all, mode="tn", outs=[F32], name="kv_bwd_w")
    dproj, g_q_a = _rmsnorm_bwd(dcqn, proj, rstd_q, q_a_norm_g, name="norm_q_bwd", into=(dproj, off_cq // QL),
                                width=QL, col=off_cq // QL)
    dproj, g_kv_a = _rmsnorm_bwd(dckvn, proj, rstd_kv, kv_a_norm_g, name="norm_kv_bwd", into=(dproj, off_ckv // KVL),
                                 width=KVL, col=off_ckv // KVL)
    du_a = _mm(dproj, wa, mode="nt", outs=[F32], name="in_bwd_x", tk=1024)
    du = _mm(dkpe, wkpe, mode="nt", outs=[F32], name="kpe_bwd_x", epi=lambda acc, r: (acc + r,), extras=(du_a,))
    g_wa = _mm(u, dproj, mode="tn", outs=[F32], name="in_bwd_w")
    g_wkpe = _mm(u, dkpe, mode="tn", outs=[F32], name="kpe_bwd_w")
    dx, g_norm_mix = _rmsnorm_bwd(du, xs, rstd0, norm_mix_g, name="norm_mix_bwd", res=dh1)

    gr = g_wa[:, :n_ret].reshape(D, RH, RET_HEAD_COLS)
    g_w_in = jnp.concatenate([gr[:, :, _RQ].reshape(D, RQW), gr[:, :, _RK].reshape(D, RQW),
                              gr[:, :, _RV].reshape(D, RVW), gr[:, :, _RG].reshape(D, RVW),
                              g_wa[:, off_cq:], g_wkpe[:, :QK_ROPE], g_wa[:, off_gret:off_cq]], axis=1)
    gq = g_wq.reshape(QL, MH, 2 * LANES)[:, :, :QK_NOPE + QK_ROPE].reshape(QL, MH * (QK_NOPE + QK_ROPE))
    g3 = ("w_in", "w_q_b", "w_kv_b")
    pcs3, theirs3, sums3 = reduce_begin("g3", g3, (g_w_in, gq, g_wkv))
    recv3 = _scatter_chips(sums3, name="scatter_g3")
    bufs1 = _split_wait(ssem1, rsem1, bufs1, recv3[0], _scatter_plan(len(g1)), name="scatter_g1_wait")
    bufs2 = _split_wait(ssem2, rsem2, bufs2, recv3[0], _scatter_plan(len(g2)), name="scatter_g2_wait")
    recv1, recv2 = bufs1[len(g1):], bufs2[len(g2):]
    halves = {}
    for names, pcs, theirs, recv in ((g1, pcs1, theirs1, recv1), (g2, pcs2, theirs2, recv2), (g3, pcs3, theirs3, recv3)):
        for k, p, t, r in zip(names, pcs, theirs, recv):
            halves[k] = _sum_chips(p, t, r, place, name="sum_chips_" + k)
    joined = _join_halves([halves[k] for k in big], name="join_halves")
    g_shard = {k: g.reshape(2 * g.shape[1], g.shape[2]) for k, g in zip(big, joined)}

    small = ("norm_mix_g", "ret_norm_g", "q_a_norm_g", "kv_a_norm_g", "norm_mlp_g", "norm_f_g")
    g_small = [g_norm_mix, g_ret_norm, g_q_a, g_kv_a, g_norm_mlp, g_norm_f]
    sizes = [int(v.size) for v in g_small]
    n_small = sum(sizes) + LANES
    rows = -(-n_small // (8 * LANES)) * 8
    packed = _pack_small(g_small + [jnp.broadcast_to(loss11.reshape(1), (LANES,))], rows)
    red = _allreduce_small(packed).reshape(-1)
    loss = red[sum(sizes)]
    w_small = [norm_mix_g, ret_norm_g, q_a_norm_g, kv_a_norm_g, norm_mlp_g, norm_f_g]
    m_small = [m_norm_mix_g, m_ret_norm_g, m_q_a_norm_g, m_kv_a_norm_g, m_norm_mlp_g, m_norm_f_g]
    v_small = [v_norm_mix_g, v_ret_norm_g, v_q_a_norm_g, v_kv_a_norm_g, v_norm_mlp_g, v_norm_f_g]
    g_pk = red[:rows * LANES].reshape(rows, LANES)
    d_pk, m_pk, v_pk = _rows_call(_adamw_vals, [_pack_small(w_small, rows), g_pk, _pack_small(m_small, rows),
                                               _pack_small(v_small, rows)], [F32, F32, F32], name="adamw_small")
    out_g, out_d, out_m, out_v = {}, {}, {}, {}
    off = 0
    for k, wv, sz in zip(small, w_small, sizes):
        for dst, src in ((out_g, g_pk), (out_d, d_pk), (out_m, m_pk), (out_v, v_pk)):
            dst[k] = src.reshape(-1)[off:off + sz].reshape(wv.shape)
        off += sz

    for k in big:
        d_, m_, v_ = _rows_call(_adamw_vals, [w_sh[k], g_shard[k], m_sh[k], v_sh[k]], [F32, F32, F32],
                                name="adamw_" + k)
        out_g[k] = g_shard[k][None]
        out_d[k], out_m[k], out_v[k] = d_[None], m_[None], v_[None]

    order = ("norm_mix_g", "w_in", "ret_norm_g", "w_ret_o", "q_a_norm_g", "w_q_b", "kv_a_norm_g", "w_kv_b",
             "w_mla_o", "w_out", "norm_mlp_g", "w_up", "w_down", "norm_f_g")
    return (loss, dx.reshape(1, S, D), *[out_g[k] for k in order], *[out_d[k] for k in order],
            *[out_m[k] for k in order], *[out_v[k] for k in order])
```

```python
import math

import jax
import jax.numpy as jnp
from jax import lax
from jax.experimental import pallas as pl
from jax.experimental.pallas import tpu as pltpu

F32 = jnp.float32
BF16 = jnp.bfloat16

EPS = 1e-6
ROPE_THETA = 10000.0
CHUNK = 64
RET_QK = 128
RET_V = 256
RET_HEAD_COLS = 2 * RET_QK + 2 * RET_V
QK_NOPE = 128
QK_ROPE = 64
V_HEAD = 128
LANES = 128
LOG2E = math.log2(math.e)

ADAM_LR = 0.001
ADAM_B1 = 0.9
ADAM_B2 = 0.999
ADAM_EPS = 1e-08
ADAM_WD = 0.01
ADAM_STEP = 10

N_CHIPS = 4
VMEM_LIMIT = 56 * 1024 * 1024
MESH = pl.DeviceIdType.MESH
NEG = -1e30


def _pallas(body, **kw):
    return pl.pallas_call(body, **kw)


def _params(sem=None):
    return pltpu.CompilerParams(dimension_semantics=sem, vmem_limit_bytes=VMEM_LIMIT)


def _tile(n, want):
    t = min(n, want)
    while n % t:
        t //= 2
    return t


_ANY = pl.BlockSpec(memory_space=pl.ANY)


def _mm(a, b, *, mode, outs, name, epi=None, extras=(), deps=(), out_shards=False, tm=1024, tn=1024, tk=2048):
    shards = b.shape[0] if b.ndim == 3 else 1
    brows, bcols = b.shape[-2], b.shape[-1] * shards
    if mode == "nn":
        (M, K), N = a.shape, bcols
    elif mode == "nt":
        (M, K), N = a.shape, brows
    else:
        (K, M), N = a.shape, bcols
    tm = _tile(M, tm)
    tn = _tile(N // (shards if mode == "nn" else 1) // (N_CHIPS if out_shards else 1), tn)
    tk = _tile(K // (shards if mode == "nt" else 1), tk)
    nk = K // tk
    if mode == "nn":
        a_spec = pl.BlockSpec((tm, tk), lambda i, j, k: (i, k))
        dims = (((1,), (0,)), ((), ()))
        if shards > 1:
            per = N // shards // tn
            b_spec = pl.BlockSpec((None, tk, tn), lambda i, j, k: (j // per, k, j % per))
        else:
            b_spec = pl.BlockSpec((tk, tn), lambda i, j, k: (k, j))
    elif mode == "nt":
        a_spec = pl.BlockSpec((tm, tk), lambda i, j, k: (i, k))
        dims = (((1,), (1,)), ((), ()))
        if shards > 1:
            per = K // shards // tk
            b_spec = pl.BlockSpec((None, tn, tk), lambda i, j, k: (k // per, j, k % per))
        else:
            b_spec = pl.BlockSpec((tn, tk), lambda i, j, k: (j, k))
    else:
        assert shards == 1
        a_spec = pl.BlockSpec((tk, tm), lambda i, j, k: (k, i))
        b_spec = pl.BlockSpec((tk, tn), lambda i, j, k: (k, j))
        dims = (((0,), (0,)), ((), ()))
    if out_shards:
        assert not extras
        oper = N // N_CHIPS // tn
        o_spec = pl.BlockSpec((None, tm, tn), lambda i, j, k: (j // oper, i, j % oper))
        o_shape = (N_CHIPS, M, N // N_CHIPS)
    else:
        o_spec = pl.BlockSpec((tm, tn), lambda i, j, k: (i, j))
        o_shape = (M, N)
    n_ex, n_out, n_dep = len(extras), len(outs), len(deps)
    if epi is None:
        epi = lambda acc: (acc,)

    def body(*refs):
        a_ref, b_ref = refs[0], refs[1]
        ex_refs = refs[2:2 + n_ex]
        o_refs = refs[2 + n_ex + n_dep:2 + n_ex + n_dep + n_out]
        part = lax.dot_general(a_ref[...].astype(BF16), b_ref[...].astype(BF16), dims,
                               preferred_element_type=F32)

        def finish(acc):
            vals = epi(acc, *[r[...] for r in ex_refs])
            for r, v in zip(o_refs, vals):
                r[...] = v.astype(r.dtype)

        if nk == 1:
            finish(part)
        else:
            acc_ref = refs[-1]
            k = pl.program_id(2)

            @pl.when(k == 0)
            def _():
                acc_ref[...] = part

            @pl.when(k > 0)
            def _():
                acc_ref[...] += part

            @pl.when(k == nk - 1)
            def _():
                finish(acc_ref[...])

    res = _pallas(
        body, name=name, grid=(M // tm, N // tn, nk),
        in_specs=[a_spec, b_spec] + [o_spec] * n_ex + [_ANY] * n_dep,
        out_specs=[o_spec] * n_out,
        out_shape=[jax.ShapeDtypeStruct(o_shape, d) for d in outs],
        scratch_shapes=[pltpu.VMEM((tm, tn), F32)] if nk > 1 else [],
        compiler_params=_params(("parallel", "parallel", "arbitrary")),
    )(a, b, *extras, *deps)
    return res[0] if n_out == 1 else res


def _rmsnorm_fwd(x, g, *, name, width=None, col=0, tr=256):
    S = x.shape[0]
    W = x.shape[1] if width is None else width
    tr = _tile(S, tr)

    def body(x_ref, g_ref, y_ref, r_ref):
        xv = x_ref[...]
        rstd = lax.rsqrt(jnp.mean(xv * xv, axis=-1, keepdims=True) + EPS)
        y_ref[...] = (xv * rstd * g_ref[...]).astype(BF16)
        r_ref[...] = rstd

    return _pallas(
        body, name=name, grid=(S // tr,),
        in_specs=[pl.BlockSpec((tr, W), lambda i: (i, col)), pl.BlockSpec((1, W), lambda i: (0, 0))],
        out_specs=[pl.BlockSpec((tr, W), lambda i: (i, 0)), pl.BlockSpec((tr, 1), lambda i: (i, 0))],
        out_shape=[jax.ShapeDtypeStruct((S, W), BF16), jax.ShapeDtypeStruct((S, 1), F32)],
        compiler_params=_params(("parallel",)),
    )(x, g)


def _rmsnorm_bwd(dy, x, rstd, g, *, name, res=None, into=None, deps=(), width=None, col=0, tr=256):
    S = x.shape[0]
    W = x.shape[1] if width is None else width
    tr = _tile(S, tr)
    has_res = res is not None

    def body(*refs):
        dy_ref, x_ref, r_ref, g_ref = refs[:4]
        dx_ref, dg_ref = refs[-2], refs[-1]
        rstd_v = r_ref[...]
        xhat = x_ref[...] * rstd_v
        dyv = dy_ref[...].astype(F32)
        dyg = dyv * g_ref[...]
        dx = rstd_v * (dyg - xhat * jnp.mean(dyg * xhat, axis=-1, keepdims=True))
        if has_res:
            dx = dx + refs[4][...]
        dx_ref[...] = dx.astype(dx_ref.dtype)
        part = jnp.sum(dyv * xhat, axis=0, keepdims=True)

        @pl.when(pl.program_id(0) == 0)
        def _():
            dg_ref[...] = part

        @pl.when(pl.program_id(0) > 0)
        def _():
            dg_ref[...] += part

    row = pl.BlockSpec((tr, W), lambda i: (i, 0))
    ins = [dy, x, rstd, g] + ([res] if has_res else [])
    in_specs = [row, pl.BlockSpec((tr, W), lambda i: (i, col)), pl.BlockSpec((tr, 1), lambda i: (i, 0)),
                pl.BlockSpec((1, W), lambda i: (0, 0))] + ([row] if has_res else [])
    if into is None:
        dx_spec, dx_shape, alias = row, jax.ShapeDtypeStruct((S, W), F32), {}
    else:
        buf, col_out = into
        ins.append(buf)
        in_specs.append(_ANY)
        dx_spec = pl.BlockSpec((tr, W), lambda i: (i, col_out))
        dx_shape = jax.ShapeDtypeStruct(buf.shape, buf.dtype)
        alias = {len(ins) - 1: 0}
    ins += list(deps)
    in_specs += [_ANY] * len(deps)
    return _pallas(
        body, name=name, grid=(S // tr,), in_specs=in_specs,
        out_specs=[dx_spec, pl.BlockSpec((1, W), lambda i: (0, 0))],
        out_shape=[dx_shape, jax.ShapeDtypeStruct((1, W), F32)],
        input_output_aliases=alias,
        compiler_params=_params(("arbitrary",)),
    )(*ins)


def _final_loss(h2, g, target, *, tr=256):
    S, D = h2.shape
    tr = _tile(S, tr)

    def body(h_ref, g_ref, t_ref, loss_ref, dh_ref, dg_ref):
        hv = h_ref[...]
        rstd = lax.rsqrt(jnp.mean(hv * hv, axis=-1, keepdims=True) + EPS)
        xhat = hv * rstd
        e = xhat * g_ref[...] - t_ref[...]
        lpart = (0.5 / D) * jnp.sum(jnp.sum(e * e, axis=-1, keepdims=True), axis=0, keepdims=True)
        dy = e * (1.0 / D)
        dyg = dy * g_ref[...]
        dh_ref[...] = rstd * (dyg - xhat * jnp.mean(dyg * xhat, axis=-1, keepdims=True))
        gpart = jnp.sum(dy * xhat, axis=0, keepdims=True)

        @pl.when(pl.program_id(0) == 0)
        def _():
            loss_ref[...] = lpart
            dg_ref[...] = gpart

        @pl.when(pl.program_id(0) > 0)
        def _():
            loss_ref[...] += lpart
            dg_ref[...] += gpart

    row = pl.BlockSpec((tr, D), lambda i: (i, 0))
    vec = pl.BlockSpec((1, D), lambda i: (0, 0))
    return _pallas(
        body, name="final_loss", grid=(S // tr,), in_specs=[row, vec, row],
        out_specs=[pl.BlockSpec((1, 1), lambda i: (0, 0)), row, vec],
        out_shape=[jax.ShapeDtypeStruct((1, 1), F32), jax.ShapeDtypeStruct((S, D), F32),
                   jax.ShapeDtypeStruct((1, D), F32)],
        compiler_params=_params(("arbitrary",)),
    )(h2, g, target)


def _sigmoid(v):
    return 1.0 / (1.0 + jnp.exp(-v))


def _merge_fwd(proj, y_ret, y_mla, D, off_gret, off_gmla, *, tr=256, tc=1024):
    S = y_ret.shape[0]
    tr, tc = _tile(S, tr), _tile(D, tc)
    b_ret, b_mla = off_gret // tc, off_gmla // tc

    def body(gr_ref, gm_ref, yr_ref, ym_ref, o_ref):
        o_ref[...] = (_sigmoid(gr_ref[...]) * yr_ref[...] + _sigmoid(gm_ref[...]) * ym_ref[...]).astype(BF16)

    blk = pl.BlockSpec((tr, tc), lambda i, j: (i, j))
    return _pallas(
        body, name="merge_fwd", grid=(S // tr, D // tc),
        in_specs=[pl.BlockSpec((tr, tc), lambda i, j: (i, b_ret + j)),
                  pl.BlockSpec((tr, tc), lambda i, j: (i, b_mla + j)), blk, blk],
        out_specs=blk, out_shape=jax.ShapeDtypeStruct((S, D), BF16),
        compiler_params=_params(("parallel", "parallel")),
    )(proj, proj, y_ret, y_mla)


def _merge_bwd(dmerged, proj, y_ret, y_mla, D, off_gret, *, tr=256):
    S = y_ret.shape[0]
    tr = _tile(S, tr)
    b0 = off_gret // D

    def body(dm_ref, g_ref, yr_ref, ym_ref, dp_ref, dyr_ref, dym_ref):
        dm = dm_ref[...]
        sg = _sigmoid(g_ref[...])

        @pl.when(pl.program_id(1) == 0)
        def _():
            dyr_ref[...] = (dm * sg).astype(BF16)
            dp_ref[...] = (dm * yr_ref[...] * sg * (1.0 - sg)).astype(BF16)

        @pl.when(pl.program_id(1) == 1)
        def _():
            dym_ref[...] = (dm * sg).astype(BF16)
            dp_ref[...] = (dm * ym_ref[...] * sg * (1.0 - sg)).astype(BF16)

    blk = pl.BlockSpec((tr, D), lambda i, j: (i, 0))
    return _pallas(
        body, name="merge_bwd", grid=(S // tr, 2),
        in_specs=[blk, pl.BlockSpec((tr, D), lambda i, j: (i, b0 + j)), blk, blk],
        out_specs=[pl.BlockSpec((tr, D), lambda i, j: (i, b0 + j)), blk, blk],
        out_shape=[jax.ShapeDtypeStruct(proj.shape, BF16), jax.ShapeDtypeStruct((S, D), BF16),
                   jax.ShapeDtypeStruct((S, D), BF16)],
        compiler_params=_params(("parallel", "arbitrary")),
    )(dmerged, proj, y_ret, y_mla)


def _rope128(t, cos_full, sin_signed):
    return t * cos_full + pltpu.roll(t, RET_QK // 2, 1) * sin_signed


def _rope128_t(d, cos_full, sin_signed):
    return d * cos_full + pltpu.roll(d * sin_signed, RET_QK // 2, 1)


def _ret_consts(lg, T):
    pos = lax.broadcasted_iota(jnp.int32, (T, 1), 0).astype(F32)
    qd = jnp.exp(lg * (pos + 1.0))
    kd = jnp.exp(lg * (T - 1.0 - pos))
    n = lax.broadcasted_iota(jnp.int32, (T, T), 0)
    m = lax.broadcasted_iota(jnp.int32, (T, T), 1)
    vis = (m // CHUNK) <= (n // CHUNK)
    dist = jnp.abs(n - m).astype(F32)
    decay = jnp.where(vis, jnp.exp(lg * dist), 0.0)
    cdec = jnp.exp(lg * float(T))
    return qd, kd, decay, cdec


def _dot(a, b, dims):
    return lax.dot_general(a.astype(BF16), b.astype(BF16), (dims, ((), ())), preferred_element_type=F32)


NN = ((1,), (0,))
NT = ((1,), (1,))
TN = ((0,), (0,))
_RQ = slice(0, RET_QK)
_RK = slice(RET_QK, 2 * RET_QK)
_RV = slice(2 * RET_QK, 2 * RET_QK + RET_V)
_RG = slice(2 * RET_QK + RET_V, RET_HEAD_COLS)


def _ret_fwd(proj, cosr, sinr, lgam, gain, RH, *, T):
    S = proj.shape[0]
    nb = S // T
    scale = RET_QK ** -0.5

    def body(p_ref, cos_ref, sin_ref, lg_ref, gain_ref, ry_ref, gated_ref, st_ref, state):
        b = pl.program_id(1)

        @pl.when(b == 0)
        def _():
            state[...] = jnp.zeros_like(state)

        lg = lg_ref[0:1, 0:1]
        qd, kd, decay, cdec = _ret_consts(lg, T)
        cosv, sinv = cos_ref[...], sin_ref[...]
        q = _rope128(p_ref[:, _RQ], cosv, sinv)
        k = _rope128(p_ref[:, _RK], cosv, sinv) * scale
        v = p_ref[:, _RV]
        sprev = state[...]
        st_ref[...] = sprev
        a = _dot(q, k, NT) * decay
        o = _dot(a, v, NN) + _dot(q * qd, sprev, NN)
        state[...] = sprev * cdec + _dot(k * kd, v, TN)
        ry_ref[...] = o
        mu = jnp.mean(o, axis=-1, keepdims=True)
        oc = o - mu
        var = jnp.mean(oc * oc, axis=-1, keepdims=True)
        t = oc * lax.rsqrt(var + EPS) * gain_ref[...]
        gv = p_ref[:, _RG]
        gated_ref[...] = (t * (gv * _sigmoid(gv))).astype(BF16)

    return _pallas(
        body, name="ret_fwd", grid=(RH, nb),
        in_specs=[pl.BlockSpec((T, RET_HEAD_COLS), lambda h, b: (b, h)),
                  pl.BlockSpec((T, RET_QK), lambda h, b: (b, 0)),
                  pl.BlockSpec((T, RET_QK), lambda h, b: (b, 0)),
                  pl.BlockSpec((None, 8, LANES), lambda h, b: (h, 0, 0)),
                  pl.BlockSpec((1, RET_V), lambda h, b: (0, h))],
        out_specs=[pl.BlockSpec((T, RET_V), lambda h, b: (b, h)),
                   pl.BlockSpec((T, RET_V), lambda h, b: (b, h)),
                   pl.BlockSpec((None, None, RET_QK, RET_V), lambda h, b: (h, b, 0, 0))],
        out_shape=[jax.ShapeDtypeStruct((S, RH * RET_V), F32), jax.ShapeDtypeStruct((S, RH * RET_V), BF16),
                   jax.ShapeDtypeStruct((RH, nb, RET_QK, RET_V), F32)],
        scratch_shapes=[pltpu.VMEM((RET_QK, RET_V), F32)],
        compiler_params=_params(("parallel", "arbitrary")),
    )(proj, cosr, sinr, lgam, gain)


def _ret_bwd(proj, cosr, sinr, lgam, gain, ry, dgated, states, dproj, RH, *, T):
    S = proj.shape[0]
    nb = S // T
    scale = RET_QK ** -0.5

    def body(p_ref, cos_ref, sin_ref, lg_ref, gain_ref, ry_ref, dg_ref, st_ref, _, dp_ref, dgain_ref, dstate):
        b = pl.program_id(1)

        @pl.when(b == 0)
        def _():
            dstate[...] = jnp.zeros_like(dstate)

        lg = lg_ref[0:1, 0:1]
        qd, kd, decay, cdec = _ret_consts(lg, T)
        cosv, sinv = cos_ref[...], sin_ref[...]
        q = _rope128(p_ref[:, _RQ], cosv, sinv)
        k = _rope128(p_ref[:, _RK], cosv, sinv) * scale
        v = p_ref[:, _RV]
        sprev = st_ref[...]
        ds_new = dstate[...]
        o = ry_ref[...]
        mu = jnp.mean(o, axis=-1, keepdims=True)
        oc = o - mu
        rstd = lax.rsqrt(jnp.mean(oc * oc, axis=-1, keepdims=True) + EPS)
        ryn = oc * rstd
        gainv = gain_ref[...]
        gv = p_ref[:, _RG]
        sg = _sigmoid(gv)
        dgt = dg_ref[...]
        dt = dgt * (gv * sg)
        dp_ref[:, _RG] = (dgt * (ryn * gainv) * (sg * (1.0 + gv * (1.0 - sg)))).astype(BF16)
        gpart = jnp.sum(dt * ryn, axis=0, keepdims=True)

        @pl.when(b == 0)
        def _():
            dgain_ref[...] = gpart

        @pl.when(b > 0)
        def _():
            dgain_ref[...] += gpart

        dryn = dt * gainv
        do = rstd * (dryn - jnp.mean(dryn, axis=-1, keepdims=True)
                     - ryn * jnp.mean(dryn * ryn, axis=-1, keepdims=True))
        a = _dot(q, k, NT) * decay
        kdk = k * kd
        qdq = q * qd
        dp_ref[:, _RV] = (_dot(a, do, TN) + _dot(kdk, ds_new, NN)).astype(BF16)
        dp = _dot(do, v, NT) * decay
        dq = _dot(dp, k, NN) + _dot(do, sprev, NT) * qd
        dk = (_dot(dp, q, TN) + _dot(v, ds_new, NT) * kd) * scale
        dstate[...] = ds_new * cdec + _dot(qdq, do, TN)
        dp_ref[:, _RQ] = _rope128_t(dq, cosv, sinv).astype(BF16)
        dp_ref[:, _RK] = _rope128_t(dk, cosv, sinv).astype(BF16)

    rb = lambda b: nb - 1 - b
    return _pallas(
        body, name="ret_bwd", grid=(RH, nb),
        in_specs=[pl.BlockSpec((T, RET_HEAD_COLS), lambda h, b: (rb(b), h)),
                  pl.BlockSpec((T, RET_QK), lambda h, b: (rb(b), 0)),
                  pl.BlockSpec((T, RET_QK), lambda h, b: (rb(b), 0)),
                  pl.BlockSpec((None, 8, LANES), lambda h, b: (h, 0, 0)),
                  pl.BlockSpec((1, RET_V), lambda h, b: (0, h)),
                  pl.BlockSpec((T, RET_V), lambda h, b: (rb(b), h)),
                  pl.BlockSpec((T, RET_V), lambda h, b: (rb(b), h)),
                  pl.BlockSpec((None, None, RET_QK, RET_V), lambda h, b: (h, rb(b), 0, 0)),
                  _ANY],
        out_specs=[pl.BlockSpec((T, RET_HEAD_COLS), lambda h, b: (rb(b), h)),
                   pl.BlockSpec((1, RET_V), lambda h, b: (0, h))],
        out_shape=[jax.ShapeDtypeStruct(dproj.shape, dproj.dtype), jax.ShapeDtypeStruct((1, RH * RET_V), F32)],
        scratch_shapes=[pltpu.VMEM((RET_QK, RET_V), F32)],
        input_output_aliases={8: 0},
        compiler_params=_params(("parallel", "arbitrary")),
    )(proj, cosr, sinr, lgam, gain, ry, dgated, states, dproj)


def _rope_pe(t, c, s1, s2):
    return t * c + pltpu.roll(t, LANES - QK_ROPE // 2, 1) * s1 + pltpu.roll(t, QK_ROPE // 2, 1) * s2


def _rope_pe_t(d, c, s1, s2):
    return d * c + pltpu.roll(d * s1, QK_ROPE // 2, 1) + pltpu.roll(d * s2, LANES - QK_ROPE // 2, 1)


def _attn_prep(q_all, kv_all, kpe, tabs, MH, *, tr=512):
    S = q_all.shape[0]
    tr = _tile(S, tr)
    c_t, s1_t, s2_t = tabs

    def body(q_ref, kv_ref, kpe_ref, c_ref, s1_ref, s2_ref, qf_ref, kf_ref, vb_ref):
        c, s1, s2 = c_ref[...], s1_ref[...], s2_ref[...]
        qf_ref[:, :QK_NOPE] = q_ref[:, :QK_NOPE].astype(BF16)
        qf_ref[:, QK_NOPE:] = _rope_pe(q_ref[:, QK_NOPE:], c, s1, s2).astype(BF16)
        kf_ref[:, :QK_NOPE] = kv_ref[:, :QK_NOPE].astype(BF16)
        kf_ref[:, QK_NOPE:] = _rope_pe(kpe_ref[...], c, s1, s2).astype(BF16)
        vb_ref[...] = kv_ref[:, QK_NOPE:].astype(BF16)

    tab = pl.BlockSpec((tr, LANES), lambda i, h: (i, 0))
    head2 = pl.BlockSpec((tr, 2 * LANES), lambda i, h: (i, h))
    return _pallas(
        body, name="attn_prep", grid=(S // tr, MH),
        in_specs=[head2, head2, tab, tab, tab, tab],
        out_specs=[head2, head2, pl.BlockSpec((tr, LANES), lambda i, h: (i, h))],
        out_shape=[jax.ShapeDtypeStruct((S, MH * 2 * LANES), BF16), jax.ShapeDtypeStruct((S, MH * 2 * LANES), BF16),
                   jax.ShapeDtypeStruct((S, MH * LANES), BF16)],
        compiler_params=_params(("parallel", "parallel")),
    )(q_all, kv_all, kpe, c_t, s1_t, s2_t)


def _chunk_mask(T):
    n = lax.broadcasted_iota(jnp.int32, (T, T), 0)
    m = lax.broadcasted_iota(jnp.int32, (T, T), 1)
    return (m // CHUNK) <= (n // CHUNK)


def _lanes_to(v, width):
    return jnp.tile(v, (1, width // LANES))


def _attn_fwd(qf, kf, vb, MH, *, T):
    S = qf.shape[0]
    nt = S // T
    c2 = (QK_NOPE + QK_ROPE) ** -0.5 * LOG2E

    def body(q_ref, k_ref, v_ref, o_ref, lse_ref, m_sc, l_sc, acc_sc):
        qi = pl.program_id(1)
        m_sc[...] = jnp.full_like(m_sc, NEG)
        l_sc[...] = jnp.zeros_like(l_sc)
        acc_sc[...] = jnp.zeros_like(acc_sc)
        q = q_ref[...]

        def tile(kt, masked):
            rows = pl.ds(pl.multiple_of(kt * T, T), T)
            s = _dot(q, k_ref[rows, :], NT) * c2
            if masked:
                s = jnp.where(_chunk_mask(T), s, NEG)
            m_prev = m_sc[...]
            m_new = jnp.maximum(m_prev, jnp.max(s, axis=-1, keepdims=True))
            alpha = jnp.exp2(m_prev - m_new)
            p = jnp.exp2(s - _lanes_to(m_new, T))
            l_sc[...] = alpha * l_sc[...] + jnp.sum(p, axis=-1, keepdims=True)
            acc_sc[...] = alpha * acc_sc[...] + _dot(p, v_ref[rows, :], NN)
            m_sc[...] = m_new

        def unmasked(kt, carry):
            tile(kt, False)
            return carry

        lax.fori_loop(0, qi, unmasked, 0)
        tile(qi, True)
        l = l_sc[...]
        o_ref[...] = acc_sc[...] / l
        lse_ref[...] = m_sc[...] + jnp.log(l) * LOG2E

    return _pallas(
        body, name="attn_fwd", grid=(MH, nt),
        in_specs=[pl.BlockSpec((T, 2 * LANES), lambda h, i: (i, h)),
                  pl.BlockSpec((S, 2 * LANES), lambda h, i: (0, h)),
                  pl.BlockSpec((S, LANES), lambda h, i: (0, h))],
        out_specs=[pl.BlockSpec((T, LANES), lambda h, i: (i, h)),
                   pl.BlockSpec((None, T, LANES), lambda h, i: (h, i, 0))],
        out_shape=[jax.ShapeDtypeStruct((S, MH * LANES), F32), jax.ShapeDtypeStruct((MH, S, LANES), F32)],
        scratch_shapes=[pltpu.VMEM((T, LANES), F32), pltpu.VMEM((T, LANES), F32), pltpu.VMEM((T, LANES), F32)],
        compiler_params=_params(("parallel", "parallel")),
    )(qf, kf, vb)


def _attn_delta(do, o, MH, *, deps=(), tr=512):
    S = do.shape[0]
    tr = _tile(S, tr)

    def body(do_ref, o_ref, *rest):
        d_ref, dob_ref = rest[-2], rest[-1]
        dov = do_ref[...]
        d_ref[...] = jnp.broadcast_to(jnp.sum(dov * o_ref[...], axis=-1, keepdims=True), (tr, LANES))
        dob_ref[...] = dov.astype(BF16)

    head = pl.BlockSpec((tr, LANES), lambda i, h: (i, h))
    return _pallas(
        body, name="attn_delta", grid=(S // tr, MH), in_specs=[head, head] + [_ANY] * len(deps),
        out_specs=[pl.BlockSpec((None, tr, LANES), lambda i, h: (h, i, 0)), head],
        out_shape=[jax.ShapeDtypeStruct((MH, S, LANES), F32), jax.ShapeDtypeStruct((S, MH * LANES), BF16)],
        compiler_params=_params(("parallel", "parallel")),
    )(do, o, *deps)


def _attn_bwd(qf, kf, vb, dob, lse2, delta, MH, *, T):
    S = qf.shape[0]
    nt = S // T
    scale = (QK_NOPE + QK_ROPE) ** -0.5
    c2 = scale * LOG2E

    def body(q_ref, k_ref, v_ref, do_ref, lse_ref, dl_ref, dq_ref, dk_ref, dv_ref, dk_sc, dv_sc):
        kj = pl.program_id(1)

        @pl.when(kj == 0)
        def _():
            dq_ref[...] = jnp.zeros_like(dq_ref)

        dk_sc[...] = jnp.zeros_like(dk_sc)
        dv_sc[...] = jnp.zeros_like(dv_sc)
        k, v = k_ref[...], v_ref[...]

        def tile(qt, masked):
            rows = pl.ds(pl.multiple_of(qt * T, T), T)
            q, dov = q_ref[rows, :], do_ref[rows, :]
            s = _dot(q, k, NT) * c2
            if masked:
                s = jnp.where(_chunk_mask(T), s, NEG)
            p = jnp.exp2(s - _lanes_to(lse_ref[rows, :], T))
            dp = _dot(dov, v, NT)
            ds = p * (dp - _lanes_to(dl_ref[rows, :], T)) * scale
            dv_sc[...] += _dot(p, dov, TN)
            dk_sc[...] += _dot(ds, q, TN)
            dq_ref[rows, :] += _dot(ds, k, NN)

        def unmasked(qt, carry):
            tile(qt, False)
            return carry

        tile(kj, True)
        lax.fori_loop(kj + 1, nt, unmasked, 0)
        dk_ref[...] = dk_sc[...]
        dv_ref[...] = dv_sc[...].astype(BF16)

    stat = pl.BlockSpec((None, S, LANES), lambda h, j: (h, 0, 0))
    return _pallas(
        body, name="attn_bwd", grid=(MH, nt),
        in_specs=[pl.BlockSpec((S, 2 * LANES), lambda h, j: (0, h)),
                  pl.BlockSpec((T, 2 * LANES), lambda h, j: (j, h)),
                  pl.BlockSpec((T, LANES), lambda h, j: (j, h)),
                  pl.BlockSpec((S, LANES), lambda h, j: (0, h)), stat, stat],
        out_specs=[pl.BlockSpec((S, 2 * LANES), lambda h, j: (0, h)),
                   pl.BlockSpec((T, 2 * LANES), lambda h, j: (j, h)),
                   pl.BlockSpec((T, LANES), lambda h, j: (j, h))],
        out_shape=[jax.ShapeDtypeStruct((S, MH * 2 * LANES), F32), jax.ShapeDtypeStruct((S, MH * 2 * LANES), F32),
                   jax.ShapeDtypeStruct((S, MH * LANES), BF16)],
        scratch_shapes=[pltpu.VMEM((T, 2 * LANES), F32), pltpu.VMEM((T, LANES), F32)],
        compiler_params=_params(("parallel", "arbitrary")),
    )(qf, kf, vb, dob, lse2, delta)


def _attn_post(dqf, dkf, dvb, tabs, MH, *, tr=512):
    S = dqf.shape[0]
    tr = _tile(S, tr)
    c_t, s1_t, s2_t = tabs

    def body(dq_ref, dk_ref, dv_ref, c_ref, s1_ref, s2_ref, dqa_ref, dkv_ref, dkpe_ref, acc):
        h = pl.program_id(1)
        c, s1, s2 = c_ref[...], s1_ref[...], s2_ref[...]
        dqa_ref[:, :QK_NOPE] = dq_ref[:, :QK_NOPE].astype(BF16)
        dqa_ref[:, QK_NOPE:] = _rope_pe_t(dq_ref[:, QK_NOPE:], c, s1, s2).astype(BF16)
        dkv_ref[:, :QK_NOPE] = dk_ref[:, :QK_NOPE].astype(BF16)
        dkv_ref[:, QK_NOPE:] = dv_ref[...]

        @pl.when(h == 0)
        def _():
            acc[...] = dk_ref[:, QK_NOPE:]

        @pl.when(h > 0)
        def _():
            acc[...] += dk_ref[:, QK_NOPE:]

        @pl.when(h == MH - 1)
        def _():
            dkpe_ref[...] = _rope_pe_t(acc[...], c, s1, s2).astype(BF16)

    tab = pl.BlockSpec((tr, LANES), lambda i, h: (i, 0))
    head2 = pl.BlockSpec((tr, 2 * LANES), lambda i, h: (i, h))
    return _pallas(
        body, name="attn_post", grid=(S // tr, MH),
        in_specs=[head2, head2, pl.BlockSpec((tr, LANES), lambda i, h: (i, h)), tab, tab, tab],
        out_specs=[head2, head2, tab],
        out_shape=[jax.ShapeDtypeStruct((S, MH * 2 * LANES), BF16)] * 2 + [jax.ShapeDtypeStruct((S, LANES), BF16)],
        scratch_shapes=[pltpu.VMEM((tr, LANES), F32)],
        compiler_params=_params(("parallel", "arbitrary")),
    )(dqf, dkf, dvb, c_t, s1_t, s2_t)


def _block_rows(R, C, block_bytes=2 << 20):
    tr = 8
    while tr * 2 * C * 4 <= block_bytes:
        tr *= 2
    return _tile(R, tr)


def _rows_call(fn, ins, out_dtypes, *, name):
    R, C = ins[0].shape
    tr = _block_rows(R, C)
    n_in = len(ins)

    def body(*refs):
        vals = fn(*[r[...] for r in refs[:n_in]])
        for r, v in zip(refs[n_in:], vals):
            r[...] = v.astype(r.dtype)

    blk = pl.BlockSpec((tr, C), lambda i: (i, 0))
    res = _pallas(
        body, name=name, grid=(R // tr,), in_specs=[blk] * n_in, out_specs=[blk] * len(out_dtypes),
        out_shape=[jax.ShapeDtypeStruct((R, C), d) for d in out_dtypes],
        compiler_params=_params(("parallel",)),
    )(*ins)
    return res


def _adamw_vals(w, g, m, v):
    m = ADAM_B1 * m + (1.0 - ADAM_B1) * g
    v = ADAM_B2 * v + (1.0 - ADAM_B2) * (g * g)
    m_hat = m / (1.0 - ADAM_B1 ** ADAM_STEP)
    v_hat = v / (1.0 - ADAM_B2 ** ADAM_STEP)
    delta = -ADAM_LR * (m_hat / (jnp.sqrt(v_hat) + ADAM_EPS) + ADAM_WD * w)
    return delta, m, v


def _sum_pair(p, theirs, place, *, name):
    _, R, C = p.shape
    R2 = R // 2
    tr = _block_rows(R2, C)
    p4 = p.reshape(N_CHIPS, 2, R2, C)

    def body(place_ref, a_ref, b_ref, o_ref):
        o_ref[...] = (a_ref[...].astype(F32) + b_ref[...].astype(F32)).astype(BF16)

    spec = pltpu.PrefetchScalarGridSpec(
        num_scalar_prefetch=1, grid=(N_CHIPS, R2 // tr),
        in_specs=[pl.BlockSpec((None, None, tr, C), lambda q, i, pr: (q, pr[0], i, 0)),
                  pl.BlockSpec((None, tr, C), lambda q, i, pr: (q, i, 0))],
        out_specs=pl.BlockSpec((None, tr, C), lambda q, i, pr: (q, i, 0)))
    return _pallas(body, name=name, grid_spec=spec, out_shape=jax.ShapeDtypeStruct((N_CHIPS, R2, C), BF16),
                   compiler_params=_params(("parallel", "parallel")))(place, p4, theirs)


def _sum_chips(p, theirs, recv, place, *, name):
    _, R, C = p.shape
    R2 = R // 2
    tr = _block_rows(R2, C)
    p4 = p.reshape(N_CHIPS, 2, R2, C)

    def body(place_ref, a_ref, b_ref, r0_ref, r1_ref, r2_ref, o_ref):
        own = a_ref[...].astype(F32) + b_ref[...].astype(F32)
        o_ref[...] = ((own + r0_ref[...].astype(F32)) + r1_ref[...].astype(F32)) + r2_ref[...].astype(F32)

    def slot(k):
        return pl.BlockSpec((None, tr, C), lambda i, pr: (k, i, 0))

    spec = pltpu.PrefetchScalarGridSpec(
        num_scalar_prefetch=1, grid=(R2 // tr,),
        in_specs=[pl.BlockSpec((None, None, tr, C), lambda i, pr: (pr[1], pr[0], i, 0)),
                  pl.BlockSpec((None, tr, C), lambda i, pr: (pr[1], i, 0)), slot(0), slot(1), slot(2)],
        out_specs=pl.BlockSpec((None, tr, C), lambda i, pr: (pr[0], i, 0)))
    return _pallas(body, name=name, grid_spec=spec, out_shape=jax.ShapeDtypeStruct((2, R2, C), F32),
                   compiler_params=_params(("parallel",)))(place, p4, theirs, recv, recv, recv)


def _me():
    return lax.axis_index("x"), lax.axis_index("y"), lax.axis_index("c")


def _other_chips(x, y):
    return [(1 - x, y), (x, 1 - y), (1 - x, 1 - y)]


def _rcopy(src, dst, ssem, rsem, dev):
    return pltpu.make_async_remote_copy(src_ref=src, dst_ref=dst, send_sem=ssem, recv_sem=rsem,
                                        device_id=dev, device_id_type=MESH)


def _cast_into_slot(w, place, *, name, deps=()):
    R, C = w.shape
    tr = _block_rows(R, C)

    def body(place_ref, w_ref, *rest):
        rest[-1][...] = w_ref[...].astype(BF16)

    spec = pltpu.PrefetchScalarGridSpec(
        num_scalar_prefetch=1, grid=(R // tr,),
        in_specs=[pl.BlockSpec((tr, C), lambda i, pr: (i, 0))] + [_ANY] * len(deps),
        out_specs=pl.BlockSpec((None, tr, C), lambda i, pr: (pr[1], i, 0)))
    out = _pallas(body, name=name, grid_spec=spec, out_shape=jax.ShapeDtypeStruct((N_CHIPS, R, C), BF16),
                  compiler_params=_params(("parallel",)))(place, w, *deps)
    return out.reshape(N_CHIPS, 2, R // 2, C)


def _gather_ici_plan(bufs):
    x, y, c = _me()
    j = 2 * x + y
    plan = []
    for i, buf in enumerate(bufs):
        for k, (px, py) in enumerate(_other_chips(x, y)):
            plan.append((3 * i + k, buf.at[j, c], buf.at[j, c], (px, py, c)))
    return plan


def _gather_weights(bufs, *, name):
    n = len(bufs)

    def body(*refs):
        outs = refs[n:2 * n]
        ssem, rsem, fssem, frsem = refs[2 * n:]
        x, y, c = _me()
        sib = (x, y, 1 - c)
        chips = _other_chips(x, y)
        remote = []
        for s, src, dst, dev in _gather_ici_plan(outs):
            r = _rcopy(src, dst, ssem.at[s], rsem.at[s], dev)
            r.start()
            remote.append(r)
        for i in range(n):
            for k, (px, py) in enumerate(chips):
                slot = outs[i].at[2 * px + py, c]
                _rcopy(slot, slot, ssem.at[3 * i + k], rsem.at[3 * i + k], (px, py, c)).wait_recv()
                f = _rcopy(slot, slot, fssem.at[3 * i + k], frsem.at[3 * i + k], sib)
                f.start()
                remote.append(f)
        for i in range(n):
            for k, (px, py) in enumerate(chips):
                slot = outs[i].at[2 * px + py, 1 - c]
                _rcopy(slot, slot, fssem.at[3 * i + k], frsem.at[3 * i + k], sib).wait_recv()
        for r in remote:
            r.wait_send()

    return _pallas(
        body, name=name, in_specs=[_ANY] * n, out_specs=[_ANY] * n,
        out_shape=[jax.ShapeDtypeStruct(b.shape, b.dtype) for b in bufs],
        scratch_shapes=[pltpu.SemaphoreType.DMA((3 * n,))] * 4,
        input_output_aliases={i: i for i in range(n)},
        compiler_params=pltpu.CompilerParams(has_side_effects=True),
    )(*bufs)


def _forward_halves(bufs, *, name):
    n = len(bufs)

    def body(*refs):
        outs = refs[n:2 * n]
        ssem, rsem = refs[2 * n:]
        x, y, c = _me()
        sib = (x, y, 1 - c)
        cps = []
        for i in range(n):
            for k, (px, py) in enumerate(_other_chips(x, y)):
                slot = outs[i].at[2 * px + py, c]
                r = _rcopy(slot, slot, ssem.at[3 * i + k], rsem.at[3 * i + k], sib)
                r.start()
                cps.append(r)
        for r in cps:
            r.wait()

    return _pallas(
        body, name=name, in_specs=[_ANY] * n, out_specs=[_ANY] * n,
        out_shape=[jax.ShapeDtypeStruct(b.shape, b.dtype) for b in bufs],
        scratch_shapes=[pltpu.SemaphoreType.DMA((3 * n,))] * 2,
        input_output_aliases={i: i for i in range(n)},
        compiler_params=pltpu.CompilerParams(has_side_effects=True),
    )(*bufs)


_HBM = pl.BlockSpec(memory_space=pltpu.HBM)
_SEM = pl.BlockSpec(memory_space=pltpu.SEMAPHORE)
_EFFECT = pltpu.SideEffectType.DATAFLOW_SIDE_EFFECTING


def _split_start(bufs, plan, n_copies, *, name):
    n = len(bufs)

    def body(*refs):
        ssem, rsem = refs[n], refs[n + 1]
        for s, src, dst, dev in plan(refs[:n]):
            _rcopy(src, dst, ssem.at[s], rsem.at[s], dev).start()
        refs[-1][...] = jnp.zeros_like(refs[-1])

    res = _pallas(
        body, name=name, in_specs=[_HBM] * n,
        out_specs=(_SEM, _SEM, *[_HBM] * n, pl.BlockSpec(memory_space=pltpu.VMEM)),
        out_shape=(pltpu.SemaphoreType.DMA((n_copies,)), pltpu.SemaphoreType.DMA((n_copies,)),
                   *[pltpu.HBM(b.shape, b.dtype) for b in bufs], jax.ShapeDtypeStruct((8, LANES), F32)),
        input_output_aliases={i: 2 + i for i in range(n)},
        compiler_params=pltpu.CompilerParams(has_side_effects=_EFFECT),
    )(*[pltpu.with_memory_space_constraint(b, pltpu.HBM) for b in bufs])
    return res[0], res[1], list(res[2:2 + n]), res[-1]


def _split_wait(ssem, rsem, bufs, after, plan, *, name):
    n = len(bufs)

    def body(*refs):
        ssem_ref, rsem_ref = refs[n], refs[n + 1]
        for s, src, dst, dev in plan(refs[:n]):
            cp = _rcopy(src, dst, ssem_ref.at[s], rsem_ref.at[s], dev)
            cp.wait_send()
            cp.wait_recv()

    return list(_pallas(
        body, name=name, in_specs=[_HBM] * n + [_SEM, _SEM, _ANY], out_specs=[_HBM] * n,
        out_shape=[pltpu.HBM(b.shape, b.dtype) for b in bufs],
        input_output_aliases={i: i for i in range(n)},
        compiler_params=pltpu.CompilerParams(has_side_effects=_EFFECT),
    )(*bufs, ssem, rsem, after))


def _scatter_plan(n):
    def plan(bufs):
        x, y, c = _me()
        out = []
        for i in range(n):
            for k, (px, py) in enumerate(_other_chips(x, y)):
                out.append((3 * i + k, bufs[i].at[2 * px + py], bufs[n + i].at[k], (px, py, c)))
        return out
    return plan


def _swap_halves(grads, *, name):
    n = len(grads)
    views = [g.reshape(N_CHIPS, 2, g.shape[1] // 2, g.shape[2]) for g in grads]

    def body(*refs):
        ins, outs = refs[:n], refs[n:2 * n]
        ssem, rsem = refs[2 * n:]
        x, y, c = _me()
        sib = (x, y, 1 - c)
        cps = []
        for i in range(n):
            r = _rcopy(ins[i].at[:, 1 - c], outs[i], ssem.at[i], rsem.at[i], sib)
            r.start()
            cps.append(r)
        for r in cps:
            r.wait()

    return _pallas(
        body, name=name, in_specs=[_ANY] * n, out_specs=[_ANY] * n,
        out_shape=[jax.ShapeDtypeStruct((N_CHIPS,) + v.shape[2:], v.dtype) for v in views],
        scratch_shapes=[pltpu.SemaphoreType.DMA((n,)), pltpu.SemaphoreType.DMA((n,))],
        compiler_params=pltpu.CompilerParams(has_side_effects=True),
    )(*views)


def _scatter_chips(sums, *, name):
    n = len(sums)

    def body(*refs):
        ins, outs = refs[:n], refs[n:2 * n]
        ssem, rsem = refs[2 * n:]
        x, y, c = _me()
        chips = _other_chips(x, y)
        remote = []
        for i in range(n):
            for k, (px, py) in enumerate(chips):
                r = _rcopy(ins[i].at[2 * px + py], outs[i].at[k], ssem.at[i, k], rsem.at[i, k], (px, py, c))
                r.start()
                remote.append(r)
        for r in remote:
            r.wait()

    return _pallas(
        body, name=name, in_specs=[_ANY] * n, out_specs=[_ANY] * n,
        out_shape=[jax.ShapeDtypeStruct((3,) + s.shape[1:], s.dtype) for s in sums],
        scratch_shapes=[pltpu.SemaphoreType.DMA((n, 3)), pltpu.SemaphoreType.DMA((n, 3))],
        compiler_params=pltpu.CompilerParams(has_side_effects=True),
    )(*sums)


def _join_halves(halves, *, name):
    n = len(halves)

    def body(*refs):
        outs = refs[n:2 * n]
        ssem, rsem = refs[2 * n:]
        x, y, c = _me()
        sib = (x, y, 1 - c)
        cps = []
        for i in range(n):
            r = _rcopy(outs[i].at[c], outs[i].at[c], ssem.at[i], rsem.at[i], sib)
            r.start()
            cps.append(r)
        for r in cps:
            r.wait()

    return _pallas(
        body, name=name, in_specs=[_ANY] * n, out_specs=[_ANY] * n,
        out_shape=[jax.ShapeDtypeStruct(h.shape, h.dtype) for h in halves],
        scratch_shapes=[pltpu.SemaphoreType.DMA((n,)), pltpu.SemaphoreType.DMA((n,))],
        input_output_aliases={i: i for i in range(n)},
        compiler_params=pltpu.CompilerParams(has_side_effects=True),
    )(*halves)


def _allreduce_small(vec):
    R = vec.shape[0]

    def body(v_ref, o_ref, buf, ssem, rsem):
        x, y, c = _me()
        me = 4 * x + 2 * y + c
        buf[me] = v_ref[...]
        cps = []
        for k in range(1, 8):
            peer = (x ^ (k >> 2), y ^ ((k >> 1) & 1), c ^ (k & 1))
            r = _rcopy(v_ref, buf.at[me], ssem.at[k - 1], rsem.at[k - 1], peer)
            r.start()
            cps.append(r)
        for k in range(1, 8):
            peer = (x ^ (k >> 2), y ^ ((k >> 1) & 1), c ^ (k & 1))
            pid = 4 * peer[0] + 2 * peer[1] + peer[2]
            _rcopy(v_ref, buf.at[pid], ssem.at[k - 1], rsem.at[k - 1], peer).wait_recv()
        for r in cps:
            r.wait_send()
        tot = buf[0]
        for d in range(1, 8):
            tot = tot + buf[d]
        o_ref[...] = tot

    vm = pl.BlockSpec(memory_space=pltpu.VMEM)
    return _pallas(
        body, name="allreduce_small", in_specs=[vm], out_specs=vm,
        out_shape=jax.ShapeDtypeStruct((R, LANES), F32),
        scratch_shapes=[pltpu.VMEM((8, R, LANES), F32), pltpu.SemaphoreType.DMA((7,)), pltpu.SemaphoreType.DMA((7,))],
        compiler_params=pltpu.CompilerParams(has_side_effects=True),
    )(vec)


def _rope_tables(positions, S):
    pos = positions.reshape(S, 1).astype(F32)
    half = RET_QK // 2
    inv = ROPE_THETA ** (-jnp.arange(half, dtype=F32) / half)
    ang = pos * inv
    cosr = jnp.concatenate([jnp.cos(ang), jnp.cos(ang)], axis=1)
    sinr = jnp.concatenate([-jnp.sin(ang), jnp.sin(ang)], axis=1)
    half = QK_ROPE // 2
    inv = ROPE_THETA ** (-jnp.arange(half, dtype=F32) / half)
    ang = pos * inv
    z = jnp.zeros((S, half), F32)
    c = jnp.concatenate([jnp.cos(ang), jnp.cos(ang), z, z], axis=1)
    s1 = jnp.concatenate([-jnp.sin(ang), z, z, z], axis=1)
    s2 = jnp.concatenate([z, jnp.sin(ang), z, z], axis=1)
    return cosr, sinr, (c, s1, s2)


def _cat_cols(g):
    return jnp.concatenate([g[j] for j in range(N_CHIPS)], axis=1)


def _split_cols(w):
    return jnp.stack(jnp.split(w, N_CHIPS, axis=1))


def _pack_small(vs, rows):
    flat = jnp.concatenate([v.reshape(-1) for v in vs])
    flat = jnp.pad(flat, (0, rows * LANES - flat.shape[0]))
    return flat.reshape(rows, LANES)


def kernel(x, positions, norm_mix_g, w_in, ret_norm_g, w_ret_o, q_a_norm_g, w_q_b, kv_a_norm_g, w_kv_b, w_mla_o, w_out, norm_mlp_g, w_up, w_down, norm_f_g, loss_target, m_norm_mix_g, m_w_in, m_ret_norm_g, m_w_ret_o, m_q_a_norm_g, m_w_q_b, m_kv_a_norm_g, m_w_kv_b, m_w_mla_o, m_w_out, m_norm_mlp_g, m_w_up, m_w_down, m_norm_f_g, v_norm_mix_g, v_w_in, v_ret_norm_g, v_w_ret_o, v_q_a_norm_g, v_w_q_b, v_kv_a_norm_g, v_w_kv_b, v_w_mla_o, v_w_out, v_norm_mlp_g, v_w_up, v_w_down, v_norm_f_g):
    S, D = x.shape[1], x.shape[2]
    RVW = w_ret_o.shape[1] * N_CHIPS
    RH = RVW // RET_V
    RQW = RH * RET_QK
    MVW = w_mla_o.shape[1] * N_CHIPS
    MH = MVW // V_HEAD
    QL, KVL = w_q_b.shape[1], w_kv_b.shape[1]
    T_RET = _tile(S, 256)
    T_ATT = _tile(S, 512)

    xs = x.reshape(S, D)
    tgt = loss_target.reshape(S, D)
    cosr, sinr, pe_tabs = _rope_tables(positions, S)
    lgam = jnp.log(1.0 - 2.0 ** (-5.0 - jnp.arange(RH, dtype=F32)))
    lgam = jnp.broadcast_to(lgam[:, None, None], (RH, 8, LANES))

    big = ("w_in", "w_ret_o", "w_q_b", "w_kv_b", "w_mla_o", "w_out", "w_up", "w_down")
    w_sh = dict(w_in=w_in[0], w_ret_o=w_ret_o[0], w_q_b=w_q_b[0], w_kv_b=w_kv_b[0], w_mla_o=w_mla_o[0],
                w_out=w_out[0], w_up=w_up[0], w_down=w_down[0])
    m_sh = dict(w_in=m_w_in[0], w_ret_o=m_w_ret_o[0], w_q_b=m_w_q_b[0], w_kv_b=m_w_kv_b[0], w_mla_o=m_w_mla_o[0],
                w_out=m_w_out[0], w_up=m_w_up[0], w_down=m_w_down[0])
    v_sh = dict(w_in=v_w_in[0], w_ret_o=v_w_ret_o[0], w_q_b=v_w_q_b[0], w_kv_b=v_w_kv_b[0], w_mla_o=v_w_mla_o[0],
                w_out=v_w_out[0], w_up=v_w_up[0], w_down=v_w_down[0])
    col_sharded = ("w_in", "w_q_b", "w_kv_b", "w_up")
    place = jnp.stack([lax.axis_index("c"), 2 * lax.axis_index("x") + lax.axis_index("y")]).astype(jnp.int32)

    def whole(k, g):
        g = g.reshape(N_CHIPS, w_sh[k].shape[0], w_sh[k].shape[1])
        if k == "w_up":
            return g
        return _cat_cols(g) if k in col_sharded else g.reshape(-1, g.shape[2])

    first = ("w_in", "w_q_b", "w_kv_b")
    later = ("w_ret_o", "w_mla_o", "w_out", "w_up", "w_down")
    got = _gather_weights([_cast_into_slot(w_sh[k], place, name="cast_" + k) for k in first], name="gather_first")
    full = {k: whole(k, g) for k, g in zip(first[1:], got[1:])}
    later_bufs = [_cast_into_slot(w_sh[k], place, name="cast_" + k, deps=(got[0],)) for k in later]
    later_ssem, later_rsem, later_bufs, later_token = _split_start(
        later_bufs, _gather_ici_plan, 3 * len(later), name="gather_later_start")

    o_rq, o_rk, o_rv, o_rg = 0, RQW, 2 * RQW, 2 * RQW + RVW
    o_cq = 2 * RQW + 2 * RVW
    o_ckv, o_kpe = o_cq + QL, o_cq + QL + KVL
    o_gr = o_kpe + QK_ROPE
    o_gm = o_gr + D
    n_ret = RH * RET_HEAD_COLS
    off_gret, off_gmla, off_cq, off_ckv = n_ret, n_ret + D, n_ret + 2 * D, n_ret + 2 * D + QL
    n_a = off_ckv + KVL
    runs = []
    for h in range(RH):
        base = h * RET_HEAD_COLS
        runs += [(o_rq + h * RET_QK, RET_QK, base), (o_rk + h * RET_QK, RET_QK, base + RET_QK),
                 (o_rv + h * RET_V, RET_V, base + 2 * RET_QK), (o_rg + h * RET_V, RET_V, base + 2 * RET_QK + RET_V)]
    runs += [(o_gr, D, off_gret), (o_gm, D, off_gmla), (o_cq, QL, off_cq), (o_ckv, KVL, off_ckv),
             (o_kpe, QK_ROPE, n_a)]
    c_sh = w_in.shape[2]

    def take(parts, start, width):
        out, lo = [], 0
        for p in parts:
            hi = lo + p.shape[1]
            a, b = max(start, lo), min(start + width, hi)
            if a < b:
                out.append(p[:, a - lo:b - lo])
            lo = hi
        return out

    wi = [got[0].reshape(N_CHIPS, D, c_sh)[jj] for jj in range(N_CHIPS)]
    here = sorted(runs, key=lambda r: r[2])
    wa = jnp.concatenate([p for s0, w, _ in here[:-1] for p in take(wi, s0, w)], axis=1)
    wkpe = jnp.concatenate(take(wi, o_kpe, QK_ROPE) + [jnp.zeros((D, LANES - QK_ROPE), BF16)], axis=1)
    wq = jnp.pad(full["w_q_b"].reshape(QL, MH, QK_NOPE + QK_ROPE),
                 ((0, 0), (0, 0), (0, LANES - QK_ROPE))).reshape(QL, MH * 2 * LANES)
    wkv = full["w_kv_b"]

    u, rstd0 = _rmsnorm_fwd(xs, norm_mix_g, name="norm_mix")
    proj = _mm(u, wa, mode="nn", outs=[F32], name="in_proj", deps=(later_token,))
    kpe = _mm(u, wkpe, mode="nn", outs=[F32], name="kpe_proj")
    ry, gated, states = _ret_fwd(proj, cosr, sinr, lgam, ret_norm_g, RH, T=T_RET)
    cqn, rstd_q = _rmsnorm_fwd(proj, q_a_norm_g, name="norm_q", width=QL, col=off_cq // QL)
    ckvn, rstd_kv = _rmsnorm_fwd(proj, kv_a_norm_g, name="norm_kv", width=KVL, col=off_ckv // KVL)
    q_all = _mm(cqn, wq, mode="nn", outs=[F32], name="q_proj")
    kv_all = _mm(ckvn, wkv, mode="nn", outs=[F32], name="kv_proj")
    qf, kf, vb = _attn_prep(q_all, kv_all, kpe, pe_tabs, MH)
    my, lse2 = _attn_fwd(qf, kf, vb, MH, T=T_ATT)
    later_bufs = _split_wait(later_ssem, later_rsem, later_bufs, my, _gather_ici_plan, name="gather_later_wait")
    later_bufs = _forward_halves(later_bufs, name="gather_later_forward")
    full.update({k: whole(k, g) for k, g in zip(later, later_bufs)})
    y_ret = _mm(gated, full["w_ret_o"], mode="nn", outs=[F32], name="ret_o")
    y_mla = _mm(my, full["w_mla_o"], mode="nn", outs=[F32], name="mla_o")
    merged = _merge_fwd(proj, y_ret, y_mla, D, off_gret, off_gmla)
    h1 = _mm(merged, full["w_out"], mode="nn", outs=[F32], name="out_proj",
             epi=lambda acc, r: (acc + r,), extras=(xs,))
    n1, rstd1 = _rmsnorm_fwd(h1, norm_mlp_g, name="norm_mlp")

    def up_epi(acc):
        r = jnp.maximum(acc, 0.0)
        return acc, r * r

    z, act = _mm(n1, full["w_up"], mode="nn", outs=[F32, BF16], name="up_proj", epi=up_epi)
    h2 = _mm(act, full["w_down"], mode="nn", outs=[F32], name="down_proj",
             epi=lambda acc, r: (acc + r,), extras=(h1,))
    loss11, dh2, g_norm_f = _final_loss(h2, norm_f_g.reshape(1, D), tgt)

    dz = _mm(dh2, full["w_down"], mode="nt", outs=[BF16], name="down_bwd_x",
             epi=lambda acc, zz: (acc * (2.0 * jnp.maximum(zz, 0.0)),), extras=(z,))
    g_w_down = _mm(act, dh2, mode="tn", outs=[BF16], name="down_bwd_w")
    dn1 = _mm(dz, full["w_up"], mode="nt", outs=[F32], name="up_bwd_x")
    g_w_up = _mm(n1, dz, mode="tn", outs=[BF16], name="up_bwd_w", out_shards=True)

    def reduce_begin(tag, names, grads):
        pcs = [g if g.ndim == 3 else g.reshape(N_CHIPS, g.shape[0] // N_CHIPS, g.shape[1]) for g in grads]
        theirs = _swap_halves(pcs, name="swap_" + tag)
        sums = [_sum_pair(p, t, place, name="sum_pair_" + k) for k, p, t in zip(names, pcs, theirs)]
        return pcs, theirs, sums

    def scatter_begin(tag, sums):
        lands = [lax.empty((3,) + s.shape[1:], s.dtype) for s in sums]
        return _split_start(sums + lands, _scatter_plan(len(sums)), 3 * len(sums), name="scatter_" + tag + "_start")

    g1 = ("w_up", "w_down")
    pcs1, theirs1, sums1 = reduce_begin("g1", g1, (g_w_up, g_w_down))
    ssem1, rsem1, bufs1, token1 = scatter_begin("g1", sums1)
    dh1, g_norm_mlp = _rmsnorm_bwd(dn1, h1, rstd1, norm_mlp_g, name="norm_mlp_bwd", res=dh2, deps=(token1,))
    dmerged = _mm(dh1, full["w_out"], mode="nt", outs=[F32], name="out_bwd_x")
    g_w_out = _mm(merged, dh1, mode="tn", outs=[BF16], name="out_bwd_w")
    dproj, dy_ret, dy_mla = _merge_bwd(dmerged, proj, y_ret, y_mla, D, off_gret)
    dgated = _mm(dy_ret, full["w_ret_o"], mode="nt", outs=[F32], name="ret_o_bwd_x")
    g_w_ret_o = _mm(gated, dy_ret, mode="tn", outs=[BF16], name="ret_o_bwd_w")
    dproj, g_ret_norm = _ret_bwd(proj, cosr, sinr, lgam, ret_norm_g, ry, dgated, states, dproj, RH, T=T_RET)
    dmy = _mm(dy_mla, full["w_mla_o"], mode="nt", outs=[F32], name="mla_o_bwd_x")
    g_w_mla_o = _mm(my, dy_mla, mode="tn", outs=[BF16], name="mla_o_bwd_w")
    g2 = ("w_out", "w_ret_o", "w_mla_o")
    pcs2, theirs2, sums2 = reduce_begin("g2", g2, (g_w_out, g_w_ret_o, g_w_mla_o))
    ssem2, rsem2, bufs2, token2 = scatter_begin("g2", sums2)
    delta, dob = _attn_delta(dmy, my, MH, deps=(token2,))
    dqf, dkf, dvb = _attn_bwd(qf, kf, vb, dob, lse2, delta, MH, T=T_ATT)
    dq_all, dkv_all, dkpe = _attn_post(dqf, dkf, dvb, pe_tabs, MH)
    dcqn = _mm(dq_all, wq, mode="nt", outs=[F32], name="q_bwd_x")
    g_wq = _mm(cqn, dq_all, mode="tn", outs=[BF16], name="q_bwd_w")
    dckvn = _mm(dkv_all, wkv, mode="nt", outs=[F32], name="kv_bwd_x")
    g_wkv = _mm(ckvn, dkv_all, mode="tn", outs=[BF16], name="kv_bwd_w")
    dproj, g_q_a = _rmsnorm_bwd(dcqn, proj, rstd_q, q_a_norm_g, name="norm_q_bwd", into=(dproj, off_cq // QL),
                                width=QL, col=off_cq // QL)
    dproj, g_kv_a = _rmsnorm_bwd(dckvn, proj, rstd_kv, kv_a_norm_g, name="norm_kv_bwd", into=(dproj, off_ckv // KVL),
                                 width=KVL, col=off_ckv // KVL)
    g_wa = _mm(u, dproj, mode="tn", outs=[BF16], name="in_bwd_w")
    g_wkpe = _mm(u, dkpe, mode="tn", outs=[BF16], name="kpe_bwd_w")

    there = sorted(runs)
    g_parts = [g_wa, g_wkpe]
    g_w_in = jnp.stack([jnp.concatenate(
        [p for s0, w, d0 in there for a, b in [(max(s0, jj * c_sh), min(s0 + w, (jj + 1) * c_sh))] if a < b
         for p in take(g_parts, d0 + a - s0, b - a)], axis=1) for jj in range(N_CHIPS)])
    gq = g_wq.reshape(QL, MH, 2 * LANES)[:, :, :QK_NOPE + QK_ROPE].reshape(QL, MH * (QK_NOPE + QK_ROPE))
    g3 = ("w_in", "w_q_b", "w_kv_b")
    pcs3, theirs3, sums3 = reduce_begin("g3", g3, (g_w_in, _split_cols(gq), _split_cols(g_wkv)))
    ssem3, rsem3, bufs3, token3 = scatter_begin("g3", sums3)
    du_a = _mm(dproj, wa, mode="nt", outs=[F32], name="in_bwd_x", tk=1024, deps=(token3,))
    du = _mm(dkpe, wkpe, mode="nt", outs=[F32], name="kpe_bwd_x", epi=lambda acc, r: (acc + r,), extras=(du_a,))
    dx, g_norm_mix = _rmsnorm_bwd(du, xs, rstd0, norm_mix_g, name="norm_mix_bwd", res=dh1)

    bufs1 = _split_wait(ssem1, rsem1, bufs1, dx, _scatter_plan(len(g1)), name="scatter_g1_wait")
    bufs2 = _split_wait(ssem2, rsem2, bufs2, dx, _scatter_plan(len(g2)), name="scatter_g2_wait")
    bufs3 = _split_wait(ssem3, rsem3, bufs3, dx, _scatter_plan(len(g3)), name="scatter_g3_wait")
    recv1, recv2, recv3 = bufs1[len(g1):], bufs2[len(g2):], bufs3[len(g3):]
    halves = {}
    for names, pcs, theirs, recv in ((g1, pcs1, theirs1, recv1), (g2, pcs2, theirs2, recv2), (g3, pcs3, theirs3, recv3)):
        for k, p, t, r in zip(names, pcs, theirs, recv):
            halves[k] = _sum_chips(p, t, r, place, name="sum_chips_" + k)
    joined = _join_halves([halves[k] for k in big], name="join_halves")
    g_shard = {k: g.reshape(2 * g.shape[1], g.shape[2]) for k, g in zip(big, joined)}

    small = ("norm_mix_g", "ret_norm_g", "q_a_norm_g", "kv_a_norm_g", "norm_mlp_g", "norm_f_g")
    g_small = [g_norm_mix, g_ret_norm, g_q_a, g_kv_a, g_norm_mlp, g_norm_f]
    sizes = [int(v.size) for v in g_small]
    n_small = sum(sizes) + LANES
    rows = -(-n_small // (8 * LANES)) * 8
    packed = _pack_small(g_small + [jnp.broadcast_to(loss11.reshape(1), (LANES,))], rows)
    red = _allreduce_small(packed).reshape(-1)
    loss = red[sum(sizes)]
    w_small = [norm_mix_g, ret_norm_g, q_a_norm_g, kv_a_norm_g, norm_mlp_g, norm_f_g]
    m_small = [m_norm_mix_g, m_ret_norm_g, m_q_a_norm_g, m_kv_a_norm_g, m_norm_mlp_g, m_norm_f_g]
    v_small = [v_norm_mix_g, v_ret_norm_g, v_q_a_norm_g, v_kv_a_norm_g, v_norm_mlp_g, v_norm_f_g]
    g_pk = red[:rows * LANES].reshape(rows, LANES)
    d_pk, m_pk, v_pk = _rows_call(_adamw_vals, [_pack_small(w_small, rows), g_pk, _pack_small(m_small, rows),
                                               _pack_small(v_small, rows)], [F32, F32, F32], name="adamw_small")
    out_g, out_d, out_m, out_v = {}, {}, {}, {}
    off = 0
    for k, wv, sz in zip(small, w_small, sizes):
        for dst, src in ((out_g, g_pk), (out_d, d_pk), (out_m, m_pk), (out_v, v_pk)):
            dst[k] = src.reshape(-1)[off:off + sz].reshape(wv.shape)
        off += sz

    for k in big:
        g_, d_, m_, v_ = _rows_call(lambda w, g, m, v: (g,) + _adamw_vals(w, g, m, v),
                                    [w_sh[k], g_shard[k], m_sh[k], v_sh[k]], [F32] * 4, name="adamw_" + k)
        out_g[k] = g_[None]
        out_d[k], out_m[k], out_v[k] = d_[None], m_[None], v_[None]

    order = ("norm_mix_g", "w_in", "ret_norm_g", "w_ret_o", "q_a_norm_g", "w_q_b", "kv_a_norm_g", "w_kv_b",
             "w_mla_o", "w_out", "norm_mlp_g", "w_up", "w_down", "norm_f_g")
    return (loss, dx.reshape(1, S, D), *[out_g[k] for k in order], *[out_d[k] for k in order],
            *[out_m[k] for k in order], *[out_v[k] for k in order])
```

```python
import math

import jax
import jax.numpy as jnp
from jax import lax
from jax.experimental import pallas as pl
from jax.experimental.pallas import tpu as pltpu

F32 = jnp.float32
BF16 = jnp.bfloat16

EPS = 1e-6
ROPE_THETA = 10000.0
CHUNK = 64
RET_QK = 128
RET_V = 256
RET_HEAD_COLS = 2 * RET_QK + 2 * RET_V
QK_NOPE = 128
QK_ROPE = 64
V_HEAD = 128
LANES = 128
LOG2E = math.log2(math.e)

ADAM_LR = 0.001
ADAM_B1 = 0.9
ADAM_B2 = 0.999
ADAM_EPS = 1e-08
ADAM_WD = 0.01
ADAM_STEP = 10

N_CHIPS = 4
VMEM_LIMIT = 56 * 1024 * 1024
MESH = pl.DeviceIdType.MESH
NEG = -1e30


def _pallas(body, **kw):
    return pl.pallas_call(body, **kw)


def _params(sem=None):
    return pltpu.CompilerParams(dimension_semantics=sem, vmem_limit_bytes=VMEM_LIMIT)


def _tile(n, want):
    t = min(n, want)
    while n % t:
        t //= 2
    return t


_ANY = pl.BlockSpec(memory_space=pl.ANY)


def _mm(a, b, *, mode, outs, name, epi=None, extras=(), deps=(), out_shards=False, tm=1024, tn=1024, tk=2048):
    shards = b.shape[0] if b.ndim == 3 else 1
    brows, bcols = b.shape[-2], b.shape[-1] * shards
    if mode == "nn":
        (M, K), N = a.shape, bcols
    elif mode == "nt":
        (M, K), N = a.shape, brows
    else:
        (K, M), N = a.shape, bcols
    tm = _tile(M, tm)
    tn = _tile(N // (shards if mode == "nn" else 1) // (N_CHIPS if out_shards else 1), tn)
    tk = _tile(K // (shards if mode == "nt" else 1), tk)
    nk = K // tk
    if mode == "nn":
        a_spec = pl.BlockSpec((tm, tk), lambda i, j, k: (i, k))
        dims = (((1,), (0,)), ((), ()))
        if shards > 1:
            per = N // shards // tn
            b_spec = pl.BlockSpec((None, tk, tn), lambda i, j, k: (j // per, k, j % per))
        else:
            b_spec = pl.BlockSpec((tk, tn), lambda i, j, k: (k, j))
    elif mode == "nt":
        a_spec = pl.BlockSpec((tm, tk), lambda i, j, k: (i, k))
        dims = (((1,), (1,)), ((), ()))
        if shards > 1:
            per = K // shards // tk
            b_spec = pl.BlockSpec((None, tn, tk), lambda i, j, k: (k // per, j, k % per))
        else:
            b_spec = pl.BlockSpec((tn, tk), lambda i, j, k: (j, k))
    else:
        assert shards == 1
        a_spec = pl.BlockSpec((tk, tm), lambda i, j, k: (k, i))
        b_spec = pl.BlockSpec((tk, tn), lambda i, j, k: (k, j))
        dims = (((0,), (0,)), ((), ()))
    if out_shards:
        assert not extras
        oper = N // N_CHIPS // tn
        o_spec = pl.BlockSpec((None, tm, tn), lambda i, j, k: (j // oper, i, j % oper))
        o_shape = (N_CHIPS, M, N // N_CHIPS)
    else:
        o_spec = pl.BlockSpec((tm, tn), lambda i, j, k: (i, j))
        o_shape = (M, N)
    n_ex, n_out, n_dep = len(extras), len(outs), len(deps)
    if epi is None:
        epi = lambda acc: (acc,)

    def body(*refs):
        a_ref, b_ref = refs[0], refs[1]
        ex_refs = refs[2:2 + n_ex]
        o_refs = refs[2 + n_ex + n_dep:2 + n_ex + n_dep + n_out]
        part = lax.dot_general(a_ref[...].astype(BF16), b_ref[...].astype(BF16), dims,
                               preferred_element_type=F32)

        def finish(acc):
            vals = epi(acc, *[r[...] for r in ex_refs])
            for r, v in zip(o_refs, vals):
                r[...] = v.astype(r.dtype)

        if nk == 1:
            finish(part)
        else:
            acc_ref = refs[-1]
            k = pl.program_id(2)

            @pl.when(k == 0)
            def _():
                acc_ref[...] = part

            @pl.when(k > 0)
            def _():
                acc_ref[...] += part

            @pl.when(k == nk - 1)
            def _():
                finish(acc_ref[...])

    res = _pallas(
        body, name=name, grid=(M // tm, N // tn, nk),
        in_specs=[a_spec, b_spec] + [o_spec] * n_ex + [_ANY] * n_dep,
        out_specs=[o_spec] * n_out,
        out_shape=[jax.ShapeDtypeStruct(o_shape, d) for d in outs],
        scratch_shapes=[pltpu.VMEM((tm, tn), F32)] if nk > 1 else [],
        compiler_params=_params(("parallel", "parallel", "arbitrary")),
    )(a, b, *extras, *deps)
    return res[0] if n_out == 1 else res


def _rmsnorm_fwd(x, g, *, name, width=None, col=0, tr=256):
    S = x.shape[0]
    W = x.shape[1] if width is None else width
    tr = _tile(S, tr)

    def body(x_ref, g_ref, y_ref, r_ref):
        xv = x_ref[...]
        rstd = lax.rsqrt(jnp.mean(xv * xv, axis=-1, keepdims=True) + EPS)
        y_ref[...] = (xv * rstd * g_ref[...]).astype(BF16)
        r_ref[...] = rstd

    return _pallas(
        body, name=name, grid=(S // tr,),
        in_specs=[pl.BlockSpec((tr, W), lambda i: (i, col)), pl.BlockSpec((1, W), lambda i: (0, 0))],
        out_specs=[pl.BlockSpec((tr, W), lambda i: (i, 0)), pl.BlockSpec((tr, 1), lambda i: (i, 0))],
        out_shape=[jax.ShapeDtypeStruct((S, W), BF16), jax.ShapeDtypeStruct((S, 1), F32)],
        compiler_params=_params(("parallel",)),
    )(x, g)


def _rmsnorm_bwd(dy, x, rstd, g, *, name, res=None, into=None, deps=(), width=None, col=0, tr=256):
    S = x.shape[0]
    W = x.shape[1] if width is None else width
    tr = _tile(S, tr)
    has_res = res is not None

    def body(*refs):
        dy_ref, x_ref, r_ref, g_ref = refs[:4]
        dx_ref, dg_ref = refs[-2], refs[-1]
        rstd_v = r_ref[...]
        xhat = x_ref[...] * rstd_v
        dyv = dy_ref[...].astype(F32)
        dyg = dyv * g_ref[...]
        dx = rstd_v * (dyg - xhat * jnp.mean(dyg * xhat, axis=-1, keepdims=True))
        if has_res:
            dx = dx + refs[4][...]
        dx_ref[...] = dx.astype(dx_ref.dtype)
        part = jnp.sum(dyv * xhat, axis=0, keepdims=True)

        @pl.when(pl.program_id(0) == 0)
        def _():
            dg_ref[...] = part

        @pl.when(pl.program_id(0) > 0)
        def _():
            dg_ref[...] += part

    row = pl.BlockSpec((tr, W), lambda i: (i, 0))
    ins = [dy, x, rstd, g] + ([res] if has_res else [])
    in_specs = [row, pl.BlockSpec((tr, W), lambda i: (i, col)), pl.BlockSpec((tr, 1), lambda i: (i, 0)),
                pl.BlockSpec((1, W), lambda i: (0, 0))] + ([row] if has_res else [])
    if into is None:
        dx_spec, dx_shape, alias = row, jax.ShapeDtypeStruct((S, W), F32), {}
    else:
        buf, col_out = into
        ins.append(buf)
        in_specs.append(_ANY)
        dx_spec = pl.BlockSpec((tr, W), lambda i: (i, col_out))
        dx_shape = jax.ShapeDtypeStruct(buf.shape, buf.dtype)
        alias = {len(ins) - 1: 0}
    ins += list(deps)
    in_specs += [_ANY] * len(deps)
    return _pallas(
        body, name=name, grid=(S // tr,), in_specs=in_specs,
        out_specs=[dx_spec, pl.BlockSpec((1, W), lambda i: (0, 0))],
        out_shape=[dx_shape, jax.ShapeDtypeStruct((1, W), F32)],
        input_output_aliases=alias,
        compiler_params=_params(("arbitrary",)),
    )(*ins)


def _final_loss(h2, g, target, *, tr=256):
    S, D = h2.shape
    tr = _tile(S, tr)

    def body(h_ref, g_ref, t_ref, loss_ref, dh_ref, dg_ref):
        hv = h_ref[...]
        rstd = lax.rsqrt(jnp.mean(hv * hv, axis=-1, keepdims=True) + EPS)
        xhat = hv * rstd
        e = xhat * g_ref[...] - t_ref[...]
        lpart = (0.5 / D) * jnp.sum(jnp.sum(e * e, axis=-1, keepdims=True), axis=0, keepdims=True)
        dy = e * (1.0 / D)
        dyg = dy * g_ref[...]
        dh_ref[...] = rstd * (dyg - xhat * jnp.mean(dyg * xhat, axis=-1, keepdims=True))
        gpart = jnp.sum(dy * xhat, axis=0, keepdims=True)

        @pl.when(pl.program_id(0) == 0)
        def _():
            loss_ref[...] = lpart
            dg_ref[...] = gpart

        @pl.when(pl.program_id(0) > 0)
        def _():
            loss_ref[...] += lpart
            dg_ref[...] += gpart

    row = pl.BlockSpec((tr, D), lambda i: (i, 0))
    vec = pl.BlockSpec((1, D), lambda i: (0, 0))
    return _pallas(
        body, name="final_loss", grid=(S // tr,), in_specs=[row, vec, row],
        out_specs=[pl.BlockSpec((1, 1), lambda i: (0, 0)), row, vec],
        out_shape=[jax.ShapeDtypeStruct((1, 1), F32), jax.ShapeDtypeStruct((S, D), F32),
                   jax.ShapeDtypeStruct((1, D), F32)],
        compiler_params=_params(("arbitrary",)),
    )(h2, g, target)


def _sigmoid(v):
    return 1.0 / (1.0 + jnp.exp(-v))


def _merge_fwd(proj, y_ret, y_mla, D, off_gret, off_gmla, *, tr=256, tc=1024):
    S = y_ret.shape[0]
    tr, tc = _tile(S, tr), _tile(D, tc)
    b_ret, b_mla = off_gret // tc, off_gmla // tc

    def body(gr_ref, gm_ref, yr_ref, ym_ref, o_ref):
        o_ref[...] = (_sigmoid(gr_ref[...]) * yr_ref[...] + _sigmoid(gm_ref[...]) * ym_ref[...]).astype(BF16)

    blk = pl.BlockSpec((tr, tc), lambda i, j: (i, j))
    return _pallas(
        body, name="merge_fwd", grid=(S // tr, D // tc),
        in_specs=[pl.BlockSpec((tr, tc), lambda i, j: (i, b_ret + j)),
                  pl.BlockSpec((tr, tc), lambda i, j: (i, b_mla + j)), blk, blk],
        out_specs=blk, out_shape=jax.ShapeDtypeStruct((S, D), BF16),
        compiler_params=_params(("parallel", "parallel")),
    )(proj, proj, y_ret, y_mla)


def _merge_bwd(dmerged, proj, y_ret, y_mla, D, off_gret, *, tr=256):
    S = y_ret.shape[0]
    tr = _tile(S, tr)
    b0 = off_gret // D

    def body(dm_ref, g_ref, yr_ref, ym_ref, dp_ref, dyr_ref, dym_ref):
        dm = dm_ref[...]
        sg = _sigmoid(g_ref[...])

        @pl.when(pl.program_id(1) == 0)
        def _():
            dyr_ref[...] = (dm * sg).astype(BF16)
            dp_ref[...] = (dm * yr_ref[...] * sg * (1.0 - sg)).astype(BF16)

        @pl.when(pl.program_id(1) == 1)
        def _():
            dym_ref[...] = (dm * sg).astype(BF16)
            dp_ref[...] = (dm * ym_ref[...] * sg * (1.0 - sg)).astype(BF16)

    blk = pl.BlockSpec((tr, D), lambda i, j: (i, 0))
    return _pallas(
        body, name="merge_bwd", grid=(S // tr, 2),
        in_specs=[blk, pl.BlockSpec((tr, D), lambda i, j: (i, b0 + j)), blk, blk],
        out_specs=[pl.BlockSpec((tr, D), lambda i, j: (i, b0 + j)), blk, blk],
        out_shape=[jax.ShapeDtypeStruct(proj.shape, BF16), jax.ShapeDtypeStruct((S, D), BF16),
                   jax.ShapeDtypeStruct((S, D), BF16)],
        compiler_params=_params(("parallel", "arbitrary")),
    )(dmerged, proj, y_ret, y_mla)


def _rope128(t, cos_full, sin_signed):
    return t * cos_full + pltpu.roll(t, RET_QK // 2, 1) * sin_signed


def _rope128_t(d, cos_full, sin_signed):
    return d * cos_full + pltpu.roll(d * sin_signed, RET_QK // 2, 1)


def _ret_consts(lg, T):
    pos = lax.broadcasted_iota(jnp.int32, (T, 1), 0).astype(F32)
    qd = jnp.exp(lg * (pos + 1.0))
    kd = jnp.exp(lg * (T - 1.0 - pos))
    n = lax.broadcasted_iota(jnp.int32, (T, T), 0)
    m = lax.broadcasted_iota(jnp.int32, (T, T), 1)
    vis = (m // CHUNK) <= (n // CHUNK)
    dist = jnp.abs(n - m).astype(F32)
    decay = jnp.where(vis, jnp.exp(lg * dist), 0.0)
    cdec = jnp.exp(lg * float(T))
    return qd, kd, decay, cdec


def _dot(a, b, dims):
    return lax.dot_general(a.astype(BF16), b.astype(BF16), (dims, ((), ())), preferred_element_type=F32)


NN = ((1,), (0,))
NT = ((1,), (1,))
TN = ((0,), (0,))
_RQ = slice(0, RET_QK)
_RK = slice(RET_QK, 2 * RET_QK)
_RV = slice(2 * RET_QK, 2 * RET_QK + RET_V)
_RG = slice(2 * RET_QK + RET_V, RET_HEAD_COLS)


def _ret_fwd(proj, cosr, sinr, lgam, gain, RH, *, T):
    S = proj.shape[0]
    nb = S // T
    scale = RET_QK ** -0.5

    def body(p_ref, cos_ref, sin_ref, lg_ref, gain_ref, ry_ref, gated_ref, st_ref, state):
        b = pl.program_id(1)

        @pl.when(b == 0)
        def _():
            state[...] = jnp.zeros_like(state)

        lg = lg_ref[0:1, 0:1]
        qd, kd, decay, cdec = _ret_consts(lg, T)
        cosv, sinv = cos_ref[...], sin_ref[...]
        q = _rope128(p_ref[:, _RQ], cosv, sinv)
        k = _rope128(p_ref[:, _RK], cosv, sinv) * scale
        v = p_ref[:, _RV]
        sprev = state[...]
        st_ref[...] = sprev
        a = _dot(q, k, NT) * decay
        o = _dot(a, v, NN) + _dot(q * qd, sprev, NN)
        state[...] = sprev * cdec + _dot(k * kd, v, TN)
        ry_ref[...] = o
        mu = jnp.mean(o, axis=-1, keepdims=True)
        oc = o - mu
        var = jnp.mean(oc * oc, axis=-1, keepdims=True)
        t = oc * lax.rsqrt(var + EPS) * gain_ref[...]
        gv = p_ref[:, _RG]
        gated_ref[...] = (t * (gv * _sigmoid(gv))).astype(BF16)

    return _pallas(
        body, name="ret_fwd", grid=(RH, nb),
        in_specs=[pl.BlockSpec((T, RET_HEAD_COLS), lambda h, b: (b, h)),
                  pl.BlockSpec((T, RET_QK), lambda h, b: (b, 0)),
                  pl.BlockSpec((T, RET_QK), lambda h, b: (b, 0)),
                  pl.BlockSpec((None, 8, LANES), lambda h, b: (h, 0, 0)),
                  pl.BlockSpec((1, RET_V), lambda h, b: (0, h))],
        out_specs=[pl.BlockSpec((T, RET_V), lambda h, b: (b, h)),
                   pl.BlockSpec((T, RET_V), lambda h, b: (b, h)),
                   pl.BlockSpec((None, None, RET_QK, RET_V), lambda h, b: (h, b, 0, 0))],
        out_shape=[jax.ShapeDtypeStruct((S, RH * RET_V), F32), jax.ShapeDtypeStruct((S, RH * RET_V), BF16),
                   jax.ShapeDtypeStruct((RH, nb, RET_QK, RET_V), F32)],
        scratch_shapes=[pltpu.VMEM((RET_QK, RET_V), F32)],
        compiler_params=_params(("parallel", "arbitrary")),
    )(proj, cosr, sinr, lgam, gain)


def _ret_bwd(proj, cosr, sinr, lgam, gain, ry, dgated, states, dproj, RH, *, T):
    S = proj.shape[0]
    nb = S // T
    scale = RET_QK ** -0.5

    def body(p_ref, cos_ref, sin_ref, lg_ref, gain_ref, ry_ref, dg_ref, st_ref, _, dp_ref, dgain_ref, dstate):
        b = pl.program_id(1)

        @pl.when(b == 0)
        def _():
            dstate[...] = jnp.zeros_like(dstate)

        lg = lg_ref[0:1, 0:1]
        qd, kd, decay, cdec = _ret_consts(lg, T)
        cosv, sinv = cos_ref[...], sin_ref[...]
        q = _rope128(p_ref[:, _RQ], cosv, sinv)
        k = _rope128(p_ref[:, _RK], cosv, sinv) * scale
        v = p_ref[:, _RV]
        sprev = st_ref[...]
        ds_new = dstate[...]
        o = ry_ref[...]
        mu = jnp.mean(o, axis=-1, keepdims=True)
        oc = o - mu
        rstd = lax.rsqrt(jnp.mean(oc * oc, axis=-1, keepdims=True) + EPS)
        ryn = oc * rstd
        gainv = gain_ref[...]
        gv = p_ref[:, _RG]
        sg = _sigmoid(gv)
        dgt = dg_ref[...]
        dt = dgt * (gv * sg)
        dp_ref[:, _RG] = (dgt * (ryn * gainv) * (sg * (1.0 + gv * (1.0 - sg)))).astype(BF16)
        gpart = jnp.sum(dt * ryn, axis=0, keepdims=True)

        @pl.when(b == 0)
        def _():
            dgain_ref[...] = gpart

        @pl.when(b > 0)
        def _():
            dgain_ref[...] += gpart

        dryn = dt * gainv
        do = rstd * (dryn - jnp.mean(dryn, axis=-1, keepdims=True)
                     - ryn * jnp.mean(dryn * ryn, axis=-1, keepdims=True))
        a = _dot(q, k, NT) * decay
        kdk = k * kd
        qdq = q * qd
        dp_ref[:, _RV] = (_dot(a, do, TN) + _dot(kdk, ds_new, NN)).astype(BF16)
        dp = _dot(do, v, NT) * decay
        dq = _dot(dp, k, NN) + _dot(do, sprev, NT) * qd
        dk = (_dot(dp, q, TN) + _dot(v, ds_new, NT) * kd) * scale
        dstate[...] = ds_new * cdec + _dot(qdq, do, TN)
        dp_ref[:, _RQ] = _rope128_t(dq, cosv, sinv).astype(BF16)
        dp_ref[:, _RK] = _rope128_t(dk, cosv, sinv).astype(BF16)

    rb = lambda b: nb - 1 - b
    return _pallas(
        body, name="ret_bwd", grid=(RH, nb),
        in_specs=[pl.BlockSpec((T, RET_HEAD_COLS), lambda h, b: (rb(b), h)),
                  pl.BlockSpec((T, RET_QK), lambda h, b: (rb(b), 0)),
                  pl.BlockSpec((T, RET_QK), lambda h, b: (rb(b), 0)),
                  pl.BlockSpec((None, 8, LANES), lambda h, b: (h, 0, 0)),
                  pl.BlockSpec((1, RET_V), lambda h, b: (0, h)),
                  pl.BlockSpec((T, RET_V), lambda h, b: (rb(b), h)),
                  pl.BlockSpec((T, RET_V), lambda h, b: (rb(b), h)),
                  pl.BlockSpec((None, None, RET_QK, RET_V), lambda h, b: (h, rb(b), 0, 0)),
                  _ANY],
        out_specs=[pl.BlockSpec((T, RET_HEAD_COLS), lambda h, b: (rb(b), h)),
                   pl.BlockSpec((1, RET_V), lambda h, b: (0, h))],
        out_shape=[jax.ShapeDtypeStruct(dproj.shape, dproj.dtype), jax.ShapeDtypeStruct((1, RH * RET_V), F32)],
        scratch_shapes=[pltpu.VMEM((RET_QK, RET_V), F32)],
        input_output_aliases={8: 0},
        compiler_params=_params(("parallel", "arbitrary")),
    )(proj, cosr, sinr, lgam, gain, ry, dgated, states, dproj)


def _rope_pe(t, c, s1, s2):
    return t * c + pltpu.roll(t, LANES - QK_ROPE // 2, 1) * s1 + pltpu.roll(t, QK_ROPE // 2, 1) * s2


def _rope_pe_t(d, c, s1, s2):
    return d * c + pltpu.roll(d * s1, QK_ROPE // 2, 1) + pltpu.roll(d * s2, LANES - QK_ROPE // 2, 1)


ATTN_C2 = (QK_NOPE + QK_ROPE) ** -0.5 * LOG2E


def _qkv_proj(cqn, ckvn, wq, wkv, kpe, tabs, MH, *, tm=512, heads=4):
    S = cqn.shape[0]
    tm = _tile(S, tm)
    hb = _tile(MH, heads)
    W = 2 * LANES
    c_t, s1_t, s2_t = tabs

    def body(cq_ref, ckv_ref, wq_ref, wkv_ref, kpe_ref, c_ref, s1_ref, s2_ref, qf_ref, kf_ref, v1_ref):
        c, s1, s2 = c_ref[...], s1_ref[...], s2_ref[...]
        q = _dot(cq_ref[...], wq_ref[...], NN)
        kv = _dot(ckv_ref[...], wkv_ref[...], NN)
        kper = _rope_pe(kpe_ref[...], c, s1, s2).astype(BF16)
        ones = jnp.ones((tm, LANES), BF16)
        for h in range(hb):
            lo, mid, hi = h * W, h * W + QK_NOPE, (h + 1) * W
            qf_ref[:, lo:mid] = (q[:, lo:mid] * ATTN_C2).astype(BF16)
            qf_ref[:, mid:hi] = (_rope_pe(q[:, mid:hi], c, s1, s2) * ATTN_C2).astype(BF16)
            kf_ref[:, lo:mid] = kv[:, lo:mid].astype(BF16)
            kf_ref[:, mid:hi] = kper
            v1_ref[:, lo:mid] = kv[:, mid:hi].astype(BF16)
            v1_ref[:, mid:hi] = ones

    tab = pl.BlockSpec((tm, LANES), lambda i, j: (i, 0))
    grp = pl.BlockSpec((tm, hb * W), lambda i, j: (i, j))
    return _pallas(
        body, name="qkv_proj", grid=(S // tm, MH // hb),
        in_specs=[pl.BlockSpec((tm, cqn.shape[1]), lambda i, j: (i, 0)),
                  pl.BlockSpec((tm, ckvn.shape[1]), lambda i, j: (i, 0)),
                  pl.BlockSpec((wq.shape[0], hb * W), lambda i, j: (0, j)),
                  pl.BlockSpec((wkv.shape[0], hb * W), lambda i, j: (0, j)), tab, tab, tab, tab],
        out_specs=[grp, grp, grp],
        out_shape=[jax.ShapeDtypeStruct((S, MH * W), BF16)] * 3,
        compiler_params=_params(("parallel", "parallel")),
    )(cqn, ckvn, wq, wkv, kpe, c_t, s1_t, s2_t)


def _chunk_mask(T):
    n = lax.broadcasted_iota(jnp.int32, (T, T), 0)
    m = lax.broadcasted_iota(jnp.int32, (T, T), 1)
    return (m // CHUNK) <= (n // CHUNK)


def _lanes_to(v, width):
    return jnp.tile(v, (1, width // LANES))


def _attn_fwd(qf, kf, v1, MH, *, T):
    S = qf.shape[0]
    nt = S // T

    def body(q_ref, k_ref, v_ref, o_ref, lse_ref, m_sc, acc_sc):
        qi = pl.program_id(1)
        m_sc[...] = jnp.full_like(m_sc, NEG)
        acc_sc[...] = jnp.zeros_like(acc_sc)
        q = q_ref[...]

        def tile(kt, masked):
            rows = pl.ds(pl.multiple_of(kt * T, T), T)
            s = _dot(q, k_ref[rows, :], NT)
            if masked:
                s = jnp.where(_chunk_mask(T), s, NEG)
            m_prev = m_sc[...]
            m_new = jnp.maximum(m_prev, jnp.max(s, axis=-1, keepdims=True))
            alpha = jnp.exp2(m_prev - m_new)
            p = jnp.exp2(s - _lanes_to(m_new, T))
            acc_sc[...] = _lanes_to(alpha, 2 * LANES) * acc_sc[...] + _dot(p, v_ref[rows, :], NN)
            m_sc[...] = m_new

        def unmasked(kt, carry):
            tile(kt, False)
            return carry

        lax.fori_loop(0, qi, unmasked, 0)
        tile(qi, True)
        l = acc_sc[:, LANES:]
        o_ref[...] = acc_sc[:, :LANES] / l
        lse_ref[...] = m_sc[...] + jnp.log(l) * LOG2E

    return _pallas(
        body, name="attn_fwd", grid=(MH, nt),
        in_specs=[pl.BlockSpec((T, 2 * LANES), lambda h, i: (i, h)),
                  pl.BlockSpec((S, 2 * LANES), lambda h, i: (0, h)),
                  pl.BlockSpec((S, 2 * LANES), lambda h, i: (0, h))],
        out_specs=[pl.BlockSpec((T, LANES), lambda h, i: (i, h)),
                   pl.BlockSpec((None, T, LANES), lambda h, i: (h, i, 0))],
        out_shape=[jax.ShapeDtypeStruct((S, MH * LANES), F32), jax.ShapeDtypeStruct((MH, S, LANES), F32)],
        scratch_shapes=[pltpu.VMEM((T, LANES), F32), pltpu.VMEM((T, 2 * LANES), F32)],
        compiler_params=_params(("parallel", "parallel")),
    )(qf, kf, v1)


def _attn_delta(do, o, MH, *, deps=(), tr=512):
    S = do.shape[0]
    tr = _tile(S, tr)

    def body(do_ref, o_ref, *rest):
        d_ref, dob_ref = rest[-2], rest[-1]
        dov = do_ref[...]
        d_ref[...] = jnp.broadcast_to(jnp.sum(dov * o_ref[...], axis=-1, keepdims=True), (tr, LANES))
        dob_ref[...] = dov.astype(BF16)

    head = pl.BlockSpec((tr, LANES), lambda i, h: (i, h))
    return _pallas(
        body, name="attn_delta", grid=(S // tr, MH), in_specs=[head, head] + [_ANY] * len(deps),
        out_specs=[pl.BlockSpec((None, tr, LANES), lambda i, h: (h, i, 0)), head],
        out_shape=[jax.ShapeDtypeStruct((MH, S, LANES), F32), jax.ShapeDtypeStruct((S, MH * LANES), BF16)],
        compiler_params=_params(("parallel", "parallel")),
    )(do, o, *deps)


def _attn_bwd(qf, kf, v1, dob, lse2, delta, MH, *, T):
    S = qf.shape[0]
    nt = S // T

    def body(q_ref, k_ref, v_ref, do_ref, lse_ref, dl_ref, dq_ref, dkv_ref, dkpe_ref, dk_sc, dv_sc):
        kj = pl.program_id(1)

        @pl.when(kj == 0)
        def _():
            dq_ref[...] = jnp.zeros_like(dq_ref)

        dk_sc[...] = jnp.zeros_like(dk_sc)
        dv_sc[...] = jnp.zeros_like(dv_sc)
        k, v = k_ref[...], v_ref[...]

        def tile(qt, masked):
            rows = pl.ds(pl.multiple_of(qt * T, T), T)
            q, dov = q_ref[rows, :], do_ref[rows, :]
            s = _dot(q, k, NT)
            if masked:
                s = jnp.where(_chunk_mask(T), s, NEG)
            p = jnp.exp2(s - _lanes_to(lse_ref[rows, :], T))
            dp = _dot(dov, v, NT)
            ds = p * (dp - _lanes_to(dl_ref[rows, :], T))
            dv_sc[...] += _dot(p, dov, TN)
            dk_sc[...] += _dot(ds, q, TN)
            dq_ref[rows, :] += _dot(ds, k, NN)

        def unmasked(qt, carry):
            tile(qt, False)
            return carry

        tile(kj, True)
        lax.fori_loop(kj + 1, nt, unmasked, 0)
        dkv_ref[:, :QK_NOPE] = (dk_sc[:, :QK_NOPE] * (1.0 / LOG2E)).astype(BF16)
        dkv_ref[:, QK_NOPE:] = dv_sc[...].astype(BF16)
        dkpe_ref[...] = dk_sc[:, QK_NOPE:] * (1.0 / LOG2E)

    stat = pl.BlockSpec((None, S, LANES), lambda h, j: (h, 0, 0))
    return _pallas(
        body, name="attn_bwd", grid=(MH, nt),
        in_specs=[pl.BlockSpec((S, 2 * LANES), lambda h, j: (0, h)),
                  pl.BlockSpec((T, 2 * LANES), lambda h, j: (j, h)),
                  pl.BlockSpec((T, LANES), lambda h, j: (j, 2 * h)),
                  pl.BlockSpec((S, LANES), lambda h, j: (0, h)), stat, stat],
        out_specs=[pl.BlockSpec((S, 2 * LANES), lambda h, j: (0, h)),
                   pl.BlockSpec((T, 2 * LANES), lambda h, j: (j, h)),
                   pl.BlockSpec((T, LANES), lambda h, j: (j, h))],
        out_shape=[jax.ShapeDtypeStruct((S, MH * 2 * LANES), F32), jax.ShapeDtypeStruct((S, MH * 2 * LANES), BF16),
                   jax.ShapeDtypeStruct((S, MH * LANES), F32)],
        scratch_shapes=[pltpu.VMEM((T, 2 * LANES), F32), pltpu.VMEM((T, LANES), F32)],
        compiler_params=_params(("parallel", "arbitrary")),
    )(qf, kf, v1, dob, lse2, delta)


def _attn_post(dqf, dkpe_h, tabs, MH, *, tr=512):
    S = dqf.shape[0]
    tr = _tile(S, tr)
    c_t, s1_t, s2_t = tabs
    scale = (QK_NOPE + QK_ROPE) ** -0.5

    def body(dq_ref, dk_ref, c_ref, s1_ref, s2_ref, dqa_ref, dkpe_ref, acc):
        h = pl.program_id(1)
        c, s1, s2 = c_ref[...], s1_ref[...], s2_ref[...]
        dqa_ref[:, :QK_NOPE] = (dq_ref[:, :QK_NOPE] * scale).astype(BF16)
        dqa_ref[:, QK_NOPE:] = (_rope_pe_t(dq_ref[:, QK_NOPE:], c, s1, s2) * scale).astype(BF16)

        @pl.when(h == 0)
        def _():
            acc[...] = dk_ref[...]

        @pl.when(h > 0)
        def _():
            acc[...] += dk_ref[...]

        @pl.when(h == MH - 1)
        def _():
            dkpe_ref[...] = _rope_pe_t(acc[...], c, s1, s2).astype(BF16)

    tab = pl.BlockSpec((tr, LANES), lambda i, h: (i, 0))
    head2 = pl.BlockSpec((tr, 2 * LANES), lambda i, h: (i, h))
    return _pallas(
        body, name="attn_post", grid=(S // tr, MH),
        in_specs=[head2, pl.BlockSpec((tr, LANES), lambda i, h: (i, h)), tab, tab, tab],
        out_specs=[head2, tab],
        out_shape=[jax.ShapeDtypeStruct((S, MH * 2 * LANES), BF16), jax.ShapeDtypeStruct((S, LANES), BF16)],
        scratch_shapes=[pltpu.VMEM((tr, LANES), F32)],
        compiler_params=_params(("parallel", "arbitrary")),
    )(dqf, dkpe_h, c_t, s1_t, s2_t)


def _block_rows(R, C, block_bytes=2 << 20):
    tr = 8
    while tr * 2 * C * 4 <= block_bytes:
        tr *= 2
    return _tile(R, tr)


def _rows_call(fn, ins, out_dtypes, *, name):
    R, C = ins[0].shape
    tr = _block_rows(R, C)
    n_in = len(ins)

    def body(*refs):
        vals = fn(*[r[...] for r in refs[:n_in]])
        for r, v in zip(refs[n_in:], vals):
            r[...] = v.astype(r.dtype)

    blk = pl.BlockSpec((tr, C), lambda i: (i, 0))
    res = _pallas(
        body, name=name, grid=(R // tr,), in_specs=[blk] * n_in, out_specs=[blk] * len(out_dtypes),
        out_shape=[jax.ShapeDtypeStruct((R, C), d) for d in out_dtypes],
        compiler_params=_params(("parallel",)),
    )(*ins)
    return res


def _adamw_vals(w, g, m, v):
    m = ADAM_B1 * m + (1.0 - ADAM_B1) * g
    v = ADAM_B2 * v + (1.0 - ADAM_B2) * (g * g)
    m_hat = m / (1.0 - ADAM_B1 ** ADAM_STEP)
    v_hat = v / (1.0 - ADAM_B2 ** ADAM_STEP)
    delta = -ADAM_LR * (m_hat / (jnp.sqrt(v_hat) + ADAM_EPS) + ADAM_WD * w)
    return delta, m, v


def _sum_pair(p, theirs, place, *, name):
    _, R, C = p.shape
    R2 = R // 2
    tr = _block_rows(R2, C)
    p4 = p.reshape(N_CHIPS, 2, R2, C)

    def body(place_ref, a_ref, b_ref, o_ref):
        o_ref[...] = (a_ref[...].astype(F32) + b_ref[...].astype(F32)).astype(BF16)

    spec = pltpu.PrefetchScalarGridSpec(
        num_scalar_prefetch=1, grid=(N_CHIPS, R2 // tr),
        in_specs=[pl.BlockSpec((None, None, tr, C), lambda q, i, pr: (q, pr[0], i, 0)),
                  pl.BlockSpec((None, tr, C), lambda q, i, pr: (q, i, 0))],
        out_specs=pl.BlockSpec((None, tr, C), lambda q, i, pr: (q, i, 0)))
    return _pallas(body, name=name, grid_spec=spec, out_shape=jax.ShapeDtypeStruct((N_CHIPS, R2, C), BF16),
                   compiler_params=_params(("parallel", "parallel")))(place, p4, theirs)


def _sum_chips(p, theirs, recv, place, *, name):
    _, R, C = p.shape
    R2 = R // 2
    tr = _block_rows(R2, C)
    p4 = p.reshape(N_CHIPS, 2, R2, C)

    def body(place_ref, a_ref, b_ref, r0_ref, r1_ref, r2_ref, o_ref):
        own = a_ref[...].astype(F32) + b_ref[...].astype(F32)
        o_ref[...] = ((own + r0_ref[...].astype(F32)) + r1_ref[...].astype(F32)) + r2_ref[...].astype(F32)

    def slot(k):
        return pl.BlockSpec((None, tr, C), lambda i, pr: (k, i, 0))

    spec = pltpu.PrefetchScalarGridSpec(
        num_scalar_prefetch=1, grid=(R2 // tr,),
        in_specs=[pl.BlockSpec((None, None, tr, C), lambda i, pr: (pr[1], pr[0], i, 0)),
                  pl.BlockSpec((None, tr, C), lambda i, pr: (pr[1], i, 0)), slot(0), slot(1), slot(2)],
        out_specs=pl.BlockSpec((None, tr, C), lambda i, pr: (pr[0], i, 0)))
    return _pallas(body, name=name, grid_spec=spec, out_shape=jax.ShapeDtypeStruct((2, R2, C), F32),
                   compiler_params=_params(("parallel",)))(place, p4, theirs, recv, recv, recv)


def _me():
    return lax.axis_index("x"), lax.axis_index("y"), lax.axis_index("c")


def _other_chips(x, y):
    return [(1 - x, y), (x, 1 - y), (1 - x, 1 - y)]


def _rcopy(src, dst, ssem, rsem, dev):
    return pltpu.make_async_remote_copy(src_ref=src, dst_ref=dst, send_sem=ssem, recv_sem=rsem,
                                        device_id=dev, device_id_type=MESH)


def _cast_into_slot(w, place, *, name, deps=()):
    R, C = w.shape
    tr = _block_rows(R, C)

    def body(place_ref, w_ref, *rest):
        rest[-1][...] = w_ref[...].astype(BF16)

    spec = pltpu.PrefetchScalarGridSpec(
        num_scalar_prefetch=1, grid=(R // tr,),
        in_specs=[pl.BlockSpec((tr, C), lambda i, pr: (i, 0))] + [_ANY] * len(deps),
        out_specs=pl.BlockSpec((None, tr, C), lambda i, pr: (pr[1], i, 0)))
    out = _pallas(body, name=name, grid_spec=spec, out_shape=jax.ShapeDtypeStruct((N_CHIPS, R, C), BF16),
                  compiler_params=_params(("parallel",)))(place, w, *deps)
    return out.reshape(N_CHIPS, 2, R // 2, C)


def _gather_ici_plan(bufs):
    x, y, c = _me()
    j = 2 * x + y
    plan = []
    for i, buf in enumerate(bufs):
        for k, (px, py) in enumerate(_other_chips(x, y)):
            plan.append((3 * i + k, buf.at[j, c], buf.at[j, c], (px, py, c)))
    return plan


def _gather_weights(bufs, *, name):
    n = len(bufs)

    def body(*refs):
        outs = refs[n:2 * n]
        ssem, rsem, fssem, frsem = refs[2 * n:]
        x, y, c = _me()
        sib = (x, y, 1 - c)
        chips = _other_chips(x, y)
        remote = []
        for s, src, dst, dev in _gather_ici_plan(outs):
            r = _rcopy(src, dst, ssem.at[s], rsem.at[s], dev)
            r.start()
            remote.append(r)
        for i in range(n):
            for k, (px, py) in enumerate(chips):
                slot = outs[i].at[2 * px + py, c]
                _rcopy(slot, slot, ssem.at[3 * i + k], rsem.at[3 * i + k], (px, py, c)).wait_recv()
                f = _rcopy(slot, slot, fssem.at[3 * i + k], frsem.at[3 * i + k], sib)
                f.start()
                remote.append(f)
        for i in range(n):
            for k, (px, py) in enumerate(chips):
                slot = outs[i].at[2 * px + py, 1 - c]
                _rcopy(slot, slot, fssem.at[3 * i + k], frsem.at[3 * i + k], sib).wait_recv()
        for r in remote:
            r.wait_send()

    return _pallas(
        body, name=name, in_specs=[_ANY] * n, out_specs=[_ANY] * n,
        out_shape=[jax.ShapeDtypeStruct(b.shape, b.dtype) for b in bufs],
        scratch_shapes=[pltpu.SemaphoreType.DMA((3 * n,))] * 4,
        input_output_aliases={i: i for i in range(n)},
        compiler_params=pltpu.CompilerParams(has_side_effects=True),
    )(*bufs)


def _forward_halves(bufs, *, name):
    n = len(bufs)

    def body(*refs):
        outs = refs[n:2 * n]
        ssem, rsem = refs[2 * n:]
        x, y, c = _me()
        sib = (x, y, 1 - c)
        cps = []
        for i in range(n):
            for k, (px, py) in enumerate(_other_chips(x, y)):
                slot = outs[i].at[2 * px + py, c]
                r = _rcopy(slot, slot, ssem.at[3 * i + k], rsem.at[3 * i + k], sib)
                r.start()
                cps.append(r)
        for r in cps:
            r.wait()

    return _pallas(
        body, name=name, in_specs=[_ANY] * n, out_specs=[_ANY] * n,
        out_shape=[jax.ShapeDtypeStruct(b.shape, b.dtype) for b in bufs],
        scratch_shapes=[pltpu.SemaphoreType.DMA((3 * n,))] * 2,
        input_output_aliases={i: i for i in range(n)},
        compiler_params=pltpu.CompilerParams(has_side_effects=True),
    )(*bufs)


_HBM = pl.BlockSpec(memory_space=pltpu.HBM)
_SEM = pl.BlockSpec(memory_space=pltpu.SEMAPHORE)
_EFFECT = pltpu.SideEffectType.DATAFLOW_SIDE_EFFECTING


def _split_start(bufs, plan, n_copies, *, name):
    n = len(bufs)

    def body(*refs):
        ssem, rsem = refs[n], refs[n + 1]
        for s, src, dst, dev in plan(refs[:n]):
            _rcopy(src, dst, ssem.at[s], rsem.at[s], dev).start()
        refs[-1][...] = jnp.zeros_like(refs[-1])

    res = _pallas(
        body, name=name, in_specs=[_HBM] * n,
        out_specs=(_SEM, _SEM, *[_HBM] * n, pl.BlockSpec(memory_space=pltpu.VMEM)),
        out_shape=(pltpu.SemaphoreType.DMA((n_copies,)), pltpu.SemaphoreType.DMA((n_copies,)),
                   *[pltpu.HBM(b.shape, b.dtype) for b in bufs], jax.ShapeDtypeStruct((8, LANES), F32)),
        input_output_aliases={i: 2 + i for i in range(n)},
        compiler_params=pltpu.CompilerParams(has_side_effects=_EFFECT),
    )(*[pltpu.with_memory_space_constraint(b, pltpu.HBM) for b in bufs])
    return res[0], res[1], list(res[2:2 + n]), res[-1]


def _split_wait(ssem, rsem, bufs, after, plan, *, name):
    n = len(bufs)

    def body(*refs):
        ssem_ref, rsem_ref = refs[n], refs[n + 1]
        for s, src, dst, dev in plan(refs[:n]):
            cp = _rcopy(src, dst, ssem_ref.at[s], rsem_ref.at[s], dev)
            cp.wait_send()
            cp.wait_recv()

    return list(_pallas(
        body, name=name, in_specs=[_HBM] * n + [_SEM, _SEM, _ANY], out_specs=[_HBM] * n,
        out_shape=[pltpu.HBM(b.shape, b.dtype) for b in bufs],
        input_output_aliases={i: i for i in range(n)},
        compiler_params=pltpu.CompilerParams(has_side_effects=_EFFECT),
    )(*bufs, ssem, rsem, after))


def _scatter_plan(n):
    def plan(bufs):
        x, y, c = _me()
        out = []
        for i in range(n):
            for k, (px, py) in enumerate(_other_chips(x, y)):
                out.append((3 * i + k, bufs[i].at[2 * px + py], bufs[n + i].at[k], (px, py, c)))
        return out
    return plan


def _swap_halves(grads, *, name):
    n = len(grads)
    views = [g.reshape(N_CHIPS, 2, g.shape[1] // 2, g.shape[2]) for g in grads]

    def body(*refs):
        ins, outs = refs[:n], refs[n:2 * n]
        ssem, rsem = refs[2 * n:]
        x, y, c = _me()
        sib = (x, y, 1 - c)
        cps = []
        for i in range(n):
            r = _rcopy(ins[i].at[:, 1 - c], outs[i], ssem.at[i], rsem.at[i], sib)
            r.start()
            cps.append(r)
        for r in cps:
            r.wait()

    return _pallas(
        body, name=name, in_specs=[_ANY] * n, out_specs=[_ANY] * n,
        out_shape=[jax.ShapeDtypeStruct((N_CHIPS,) + v.shape[2:], v.dtype) for v in views],
        scratch_shapes=[pltpu.SemaphoreType.DMA((n,)), pltpu.SemaphoreType.DMA((n,))],
        compiler_params=pltpu.CompilerParams(has_side_effects=True),
    )(*views)


def _join_halves(halves, *, name):
    n = len(halves)

    def body(*refs):
        outs = refs[n:2 * n]
        ssem, rsem = refs[2 * n:]
        x, y, c = _me()
        sib = (x, y, 1 - c)
        cps = []
        for i in range(n):
            r = _rcopy(outs[i].at[c], outs[i].at[c], ssem.at[i], rsem.at[i], sib)
            r.start()
            cps.append(r)
        for r in cps:
            r.wait()

    return _pallas(
        body, name=name, in_specs=[_ANY] * n, out_specs=[_ANY] * n,
        out_shape=[jax.ShapeDtypeStruct(h.shape, h.dtype) for h in halves],
        scratch_shapes=[pltpu.SemaphoreType.DMA((n,)), pltpu.SemaphoreType.DMA((n,))],
        input_output_aliases={i: i for i in range(n)},
        compiler_params=pltpu.CompilerParams(has_side_effects=True),
    )(*halves)


def _allreduce_small(vec):
    R = vec.shape[0]

    def body(v_ref, o_ref, buf, ssem, rsem):
        x, y, c = _me()
        me = 4 * x + 2 * y + c
        buf[me] = v_ref[...]
        cps = []
        for k in range(1, 8):
            peer = (x ^ (k >> 2), y ^ ((k >> 1) & 1), c ^ (k & 1))
            r = _rcopy(v_ref, buf.at[me], ssem.at[k - 1], rsem.at[k - 1], peer)
            r.start()
            cps.append(r)
        for k in range(1, 8):
            peer = (x ^ (k >> 2), y ^ ((k >> 1) & 1), c ^ (k & 1))
            pid = 4 * peer[0] + 2 * peer[1] + peer[2]
            _rcopy(v_ref, buf.at[pid], ssem.at[k - 1], rsem.at[k - 1], peer).wait_recv()
        for r in cps:
            r.wait_send()
        tot = buf[0]
        for d in range(1, 8):
            tot = tot + buf[d]
        o_ref[...] = tot

    vm = pl.BlockSpec(memory_space=pltpu.VMEM)
    return _pallas(
        body, name="allreduce_small", in_specs=[vm], out_specs=vm,
        out_shape=jax.ShapeDtypeStruct((R, LANES), F32),
        scratch_shapes=[pltpu.VMEM((8, R, LANES), F32), pltpu.SemaphoreType.DMA((7,)), pltpu.SemaphoreType.DMA((7,))],
        compiler_params=pltpu.CompilerParams(has_side_effects=True),
    )(vec)


def _rope_tables(positions, S):
    pos = positions.reshape(S, 1).astype(F32)
    half = RET_QK // 2
    inv = ROPE_THETA ** (-jnp.arange(half, dtype=F32) / half)
    ang = pos * inv
    cosr = jnp.concatenate([jnp.cos(ang), jnp.cos(ang)], axis=1)
    sinr = jnp.concatenate([-jnp.sin(ang), jnp.sin(ang)], axis=1)
    half = QK_ROPE // 2
    inv = ROPE_THETA ** (-jnp.arange(half, dtype=F32) / half)
    ang = pos * inv
    z = jnp.zeros((S, half), F32)
    c = jnp.concatenate([jnp.cos(ang), jnp.cos(ang), z, z], axis=1)
    s1 = jnp.concatenate([-jnp.sin(ang), z, z, z], axis=1)
    s2 = jnp.concatenate([z, jnp.sin(ang), z, z], axis=1)
    return cosr, sinr, (c, s1, s2)


def _cat_cols(g):
    return jnp.concatenate([g[j] for j in range(N_CHIPS)], axis=1)


def _split_cols(w):
    return jnp.stack(jnp.split(w, N_CHIPS, axis=1))


def _pack_small(vs, rows):
    flat = jnp.concatenate([v.reshape(-1) for v in vs])
    flat = jnp.pad(flat, (0, rows * LANES - flat.shape[0]))
    return flat.reshape(rows, LANES)


def kernel(x, positions, norm_mix_g, w_in, ret_norm_g, w_ret_o, q_a_norm_g, w_q_b, kv_a_norm_g, w_kv_b, w_mla_o, w_out, norm_mlp_g, w_up, w_down, norm_f_g, loss_target, m_norm_mix_g, m_w_in, m_ret_norm_g, m_w_ret_o, m_q_a_norm_g, m_w_q_b, m_kv_a_norm_g, m_w_kv_b, m_w_mla_o, m_w_out, m_norm_mlp_g, m_w_up, m_w_down, m_norm_f_g, v_norm_mix_g, v_w_in, v_ret_norm_g, v_w_ret_o, v_q_a_norm_g, v_w_q_b, v_kv_a_norm_g, v_w_kv_b, v_w_mla_o, v_w_out, v_norm_mlp_g, v_w_up, v_w_down, v_norm_f_g):
    S, D = x.shape[1], x.shape[2]
    RVW = w_ret_o.shape[1] * N_CHIPS
    RH = RVW // RET_V
    RQW = RH * RET_QK
    MVW = w_mla_o.shape[1] * N_CHIPS
    MH = MVW // V_HEAD
    QL, KVL = w_q_b.shape[1], w_kv_b.shape[1]
    T_RET = _tile(S, 256)
    T_ATT = _tile(S, 512)

    xs = x.reshape(S, D)
    tgt = loss_target.reshape(S, D)
    cosr, sinr, pe_tabs = _rope_tables(positions, S)
    lgam = jnp.log(1.0 - 2.0 ** (-5.0 - jnp.arange(RH, dtype=F32)))
    lgam = jnp.broadcast_to(lgam[:, None, None], (RH, 8, LANES))

    big = ("w_in", "w_ret_o", "w_q_b", "w_kv_b", "w_mla_o", "w_out", "w_up", "w_down")
    w_sh = dict(w_in=w_in[0], w_ret_o=w_ret_o[0], w_q_b=w_q_b[0], w_kv_b=w_kv_b[0], w_mla_o=w_mla_o[0],
                w_out=w_out[0], w_up=w_up[0], w_down=w_down[0])
    m_sh = dict(w_in=m_w_in[0], w_ret_o=m_w_ret_o[0], w_q_b=m_w_q_b[0], w_kv_b=m_w_kv_b[0], w_mla_o=m_w_mla_o[0],
                w_out=m_w_out[0], w_up=m_w_up[0], w_down=m_w_down[0])
    v_sh = dict(w_in=v_w_in[0], w_ret_o=v_w_ret_o[0], w_q_b=v_w_q_b[0], w_kv_b=v_w_kv_b[0], w_mla_o=v_w_mla_o[0],
                w_out=v_w_out[0], w_up=v_w_up[0], w_down=v_w_down[0])
    col_sharded = ("w_in", "w_q_b", "w_kv_b", "w_up")
    place = jnp.stack([lax.axis_index("c"), 2 * lax.axis_index("x") + lax.axis_index("y")]).astype(jnp.int32)

    def whole(k, g):
        g = g.reshape(N_CHIPS, w_sh[k].shape[0], w_sh[k].shape[1])
        if k == "w_up":
            return g
        return _cat_cols(g) if k in col_sharded else g.reshape(-1, g.shape[2])

    first = ("w_in", "w_q_b", "w_kv_b")
    later = ("w_ret_o", "w_mla_o", "w_out", "w_up", "w_down")
    got = _gather_weights([_cast_into_slot(w_sh[k], place, name="cast_" + k) for k in first], name="gather_first")
    full = {k: whole(k, g) for k, g in zip(first[1:], got[1:])}
    later_bufs = [_cast_into_slot(w_sh[k], place, name="cast_" + k, deps=(got[0],)) for k in later]
    later_ssem, later_rsem, later_bufs, later_token = _split_start(
        later_bufs, _gather_ici_plan, 3 * len(later), name="gather_later_start")

    o_rq, o_rk, o_rv, o_rg = 0, RQW, 2 * RQW, 2 * RQW + RVW
    o_cq = 2 * RQW + 2 * RVW
    o_ckv, o_kpe = o_cq + QL, o_cq + QL + KVL
    o_gr = o_kpe + QK_ROPE
    o_gm = o_gr + D
    n_ret = RH * RET_HEAD_COLS
    off_gret, off_gmla, off_cq, off_ckv = n_ret, n_ret + D, n_ret + 2 * D, n_ret + 2 * D + QL
    n_a = off_ckv + KVL
    runs = []
    for h in range(RH):
        base = h * RET_HEAD_COLS
        runs += [(o_rq + h * RET_QK, RET_QK, base), (o_rk + h * RET_QK, RET_QK, base + RET_QK),
                 (o_rv + h * RET_V, RET_V, base + 2 * RET_QK), (o_rg + h * RET_V, RET_V, base + 2 * RET_QK + RET_V)]
    runs += [(o_gr, D, off_gret), (o_gm, D, off_gmla), (o_cq, QL, off_cq), (o_ckv, KVL, off_ckv),
             (o_kpe, QK_ROPE, n_a)]
    c_sh = w_in.shape[2]

    def take(parts, start, width):
        out, lo = [], 0
        for p in parts:
            hi = lo + p.shape[1]
            a, b = max(start, lo), min(start + width, hi)
            if a < b:
                out.append(p[:, a - lo:b - lo])
            lo = hi
        return out

    wi = [got[0].reshape(N_CHIPS, D, c_sh)[jj] for jj in range(N_CHIPS)]
    here = sorted(runs, key=lambda r: r[2])
    wa = jnp.concatenate([p for s0, w, _ in here[:-1] for p in take(wi, s0, w)], axis=1)
    wkpe = jnp.concatenate(take(wi, o_kpe, QK_ROPE) + [jnp.zeros((D, LANES - QK_ROPE), BF16)], axis=1)
    wq = jnp.pad(full["w_q_b"].reshape(QL, MH, QK_NOPE + QK_ROPE),
                 ((0, 0), (0, 0), (0, LANES - QK_ROPE))).reshape(QL, MH * 2 * LANES)
    wkv = full["w_kv_b"]

    u, rstd0 = _rmsnorm_fwd(xs, norm_mix_g, name="norm_mix")
    proj = _mm(u, wa, mode="nn", outs=[F32], name="in_proj", deps=(later_token,))
    kpe = _mm(u, wkpe, mode="nn", outs=[F32], name="kpe_proj")
    ry, gated, states = _ret_fwd(proj, cosr, sinr, lgam, ret_norm_g, RH, T=T_RET)
    cqn, rstd_q = _rmsnorm_fwd(proj, q_a_norm_g, name="norm_q", width=QL, col=off_cq // QL)
    ckvn, rstd_kv = _rmsnorm_fwd(proj, kv_a_norm_g, name="norm_kv", width=KVL, col=off_ckv // KVL)
    qf, kf, v1 = _qkv_proj(cqn, ckvn, wq, wkv, kpe, pe_tabs, MH)
    my, lse2 = _attn_fwd(qf, kf, v1, MH, T=T_ATT)
    later_bufs = _split_wait(later_ssem, later_rsem, later_bufs, my, _gather_ici_plan, name="gather_later_wait")
    later_bufs = _forward_halves(later_bufs, name="gather_later_forward")
    full.update({k: whole(k, g) for k, g in zip(later, later_bufs)})
    y_ret = _mm(gated, full["w_ret_o"], mode="nn", outs=[F32], name="ret_o")
    y_mla = _mm(my, full["w_mla_o"], mode="nn", outs=[F32], name="mla_o")
    merged = _merge_fwd(proj, y_ret, y_mla, D, off_gret, off_gmla)
    h1 = _mm(merged, full["w_out"], mode="nn", outs=[F32], name="out_proj",
             epi=lambda acc, r: (acc + r,), extras=(xs,))
    n1, rstd1 = _rmsnorm_fwd(h1, norm_mlp_g, name="norm_mlp")

    def up_epi(acc):
        r = jnp.maximum(acc, 0.0)
        return acc, r * r

    z, act = _mm(n1, full["w_up"], mode="nn", outs=[F32, BF16], name="up_proj", epi=up_epi)
    h2 = _mm(act, full["w_down"], mode="nn", outs=[F32], name="down_proj",
             epi=lambda acc, r: (acc + r,), extras=(h1,))
    loss11, dh2, g_norm_f = _final_loss(h2, norm_f_g.reshape(1, D), tgt)

    dz = _mm(dh2, full["w_down"], mode="nt", outs=[BF16], name="down_bwd_x",
             epi=lambda acc, zz: (acc * (2.0 * jnp.maximum(zz, 0.0)),), extras=(z,))
    g_w_down = _mm(act, dh2, mode="tn", outs=[BF16], name="down_bwd_w")
    dn1 = _mm(dz, full["w_up"], mode="nt", outs=[F32], name="up_bwd_x")
    g_w_up = _mm(n1, dz, mode="tn", outs=[BF16], name="up_bwd_w", out_shards=True)

    def reduce_begin(tag, names, grads):
        pcs = [g if g.ndim == 3 else g.reshape(N_CHIPS, g.shape[0] // N_CHIPS, g.shape[1]) for g in grads]
        theirs = _swap_halves(pcs, name="swap_" + tag)
        sums = [_sum_pair(p, t, place, name="sum_pair_" + k) for k, p, t in zip(names, pcs, theirs)]
        return pcs, theirs, sums

    def scatter_begin(tag, sums):
        lands = [lax.empty((3,) + s.shape[1:], s.dtype) for s in sums]
        return _split_start(sums + lands, _scatter_plan(len(sums)), 3 * len(sums), name="scatter_" + tag + "_start")

    g1 = ("w_up", "w_down")
    pcs1, theirs1, sums1 = reduce_begin("g1", g1, (g_w_up, g_w_down))
    ssem1, rsem1, bufs1, token1 = scatter_begin("g1", sums1)
    dh1, g_norm_mlp = _rmsnorm_bwd(dn1, h1, rstd1, norm_mlp_g, name="norm_mlp_bwd", res=dh2, deps=(token1,))
    dmerged = _mm(dh1, full["w_out"], mode="nt", outs=[F32], name="out_bwd_x")
    g_w_out = _mm(merged, dh1, mode="tn", outs=[BF16], name="out_bwd_w")
    dproj, dy_ret, dy_mla = _merge_bwd(dmerged, proj, y_ret, y_mla, D, off_gret)
    dgated = _mm(dy_ret, full["w_ret_o"], mode="nt", outs=[F32], name="ret_o_bwd_x")
    g_w_ret_o = _mm(gated, dy_ret, mode="tn", outs=[BF16], name="ret_o_bwd_w")
    dproj, g_ret_norm = _ret_bwd(proj, cosr, sinr, lgam, ret_norm_g, ry, dgated, states, dproj, RH, T=T_RET)
    dmy = _mm(dy_mla, full["w_mla_o"], mode="nt", outs=[F32], name="mla_o_bwd_x")
    g_w_mla_o = _mm(my, dy_mla, mode="tn", outs=[BF16], name="mla_o_bwd_w")
    g2 = ("w_out", "w_ret_o", "w_mla_o")
    pcs2, theirs2, sums2 = reduce_begin("g2", g2, (g_w_out, g_w_ret_o, g_w_mla_o))
    ssem2, rsem2, bufs2, token2 = scatter_begin("g2", sums2)
    delta, dob = _attn_delta(dmy, my, MH, deps=(token2,))
    dqf, dkv_all, dkpe_h = _attn_bwd(qf, kf, v1, dob, lse2, delta, MH, T=T_ATT)
    dq_all, dkpe = _attn_post(dqf, dkpe_h, pe_tabs, MH)
    dcqn = _mm(dq_all, wq, mode="nt", outs=[F32], name="q_bwd_x")
    g_wq = _mm(cqn, dq_all, mode="tn", outs=[BF16], name="q_bwd_w")
    dckvn = _mm(dkv_all, wkv, mode="nt", outs=[F32], name="kv_bwd_x")
    g_wkv = _mm(ckvn, dkv_all, mode="tn", outs=[BF16], name="kv_bwd_w")
    dproj, g_q_a = _rmsnorm_bwd(dcqn, proj, rstd_q, q_a_norm_g, name="norm_q_bwd", into=(dproj, off_cq // QL),
                                width=QL, col=off_cq // QL)
    dproj, g_kv_a = _rmsnorm_bwd(dckvn, proj, rstd_kv, kv_a_norm_g, name="norm_kv_bwd", into=(dproj, off_ckv // KVL),
                                 width=KVL, col=off_ckv // KVL)
    g_wa = _mm(u, dproj, mode="tn", outs=[BF16], name="in_bwd_w")
    g_wkpe = _mm(u, dkpe, mode="tn", outs=[BF16], name="kpe_bwd_w")

    there = sorted(runs)
    g_parts = [g_wa, g_wkpe]
    g_w_in = jnp.stack([jnp.concatenate(
        [p for s0, w, d0 in there for a, b in [(max(s0, jj * c_sh), min(s0 + w, (jj + 1) * c_sh))] if a < b
         for p in take(g_parts, d0 + a - s0, b - a)], axis=1) for jj in range(N_CHIPS)])
    gq = g_wq.reshape(QL, MH, 2 * LANES)[:, :, :QK_NOPE + QK_ROPE].reshape(QL, MH * (QK_NOPE + QK_ROPE))
    g3 = ("w_in", "w_q_b", "w_kv_b")
    pcs3, theirs3, sums3 = reduce_begin("g3", g3, (g_w_in, _split_cols(gq), _split_cols(g_wkv)))
    ssem3, rsem3, bufs3, token3 = scatter_begin("g3", sums3)
    du_a = _mm(dproj, wa, mode="nt", outs=[F32], name="in_bwd_x", tk=1024, deps=(token3,))
    du = _mm(dkpe, wkpe, mode="nt", outs=[F32], name="kpe_bwd_x", epi=lambda acc, r: (acc + r,), extras=(du_a,))
    dx, g_norm_mix = _rmsnorm_bwd(du, xs, rstd0, norm_mix_g, name="norm_mix_bwd", res=dh1)

    bufs1 = _split_wait(ssem1, rsem1, bufs1, dx, _scatter_plan(len(g1)), name="scatter_g1_wait")
    bufs2 = _split_wait(ssem2, rsem2, bufs2, dx, _scatter_plan(len(g2)), name="scatter_g2_wait")
    bufs3 = _split_wait(ssem3, rsem3, bufs3, dx, _scatter_plan(len(g3)), name="scatter_g3_wait")
    recv1, recv2, recv3 = bufs1[len(g1):], bufs2[len(g2):], bufs3[len(g3):]
    halves = {}
    for names, pcs, theirs, recv in ((g1, pcs1, theirs1, recv1), (g2, pcs2, theirs2, recv2), (g3, pcs3, theirs3, recv3)):
        for k, p, t, r in zip(names, pcs, theirs, recv):
            halves[k] = _sum_chips(p, t, r, place, name="sum_chips_" + k)
    joined = _join_halves([halves[k] for k in big], name="join_halves")
    g_shard = {k: g.reshape(2 * g.shape[1], g.shape[2]) for k, g in zip(big, joined)}

    small = ("norm_mix_g", "ret_norm_g", "q_a_norm_g", "kv_a_norm_g", "norm_mlp_g", "norm_f_g")
    g_small = [g_norm_mix, g_ret_norm, g_q_a, g_kv_a, g_norm_mlp, g_norm_f]
    sizes = [int(v.size) for v in g_small]
    n_small = sum(sizes) + LANES
    rows = -(-n_small // (8 * LANES)) * 8
    packed = _pack_small(g_small + [jnp.broadcast_to(loss11.reshape(1), (LANES,))], rows)
    red = _allreduce_small(packed).reshape(-1)
    loss = red[sum(sizes)]
    w_small = [norm_mix_g, ret_norm_g, q_a_norm_g, kv_a_norm_g, norm_mlp_g, norm_f_g]
    m_small = [m_norm_mix_g, m_ret_norm_g, m_q_a_norm_g, m_kv_a_norm_g, m_norm_mlp_g, m_norm_f_g]
    v_small = [v_norm_mix_g, v_ret_norm_g, v_q_a_norm_g, v_kv_a_norm_g, v_norm_mlp_g, v_norm_f_g]
    g_pk = red[:rows * LANES].reshape(rows, LANES)
    d_pk, m_pk, v_pk = _rows_call(_adamw_vals, [_pack_small(w_small, rows), g_pk, _pack_small(m_small, rows),
                                               _pack_small(v_small, rows)], [F32, F32, F32], name="adamw_small")
    out_g, out_d, out_m, out_v = {}, {}, {}, {}
    off = 0
    for k, wv, sz in zip(small, w_small, sizes):
        for dst, src in ((out_g, g_pk), (out_d, d_pk), (out_m, m_pk), (out_v, v_pk)):
            dst[k] = src.reshape(-1)[off:off + sz].reshape(wv.shape)
        off += sz

    for k in big:
        g_, d_, m_, v_ = _rows_call(lambda w, g, m, v: (g,) + _adamw_vals(w, g, m, v),
                                    [w_sh[k], g_shard[k], m_sh[k], v_sh[k]], [F32] * 4, name="adamw_" + k)
        out_g[k] = g_[None]
        out_d[k], out_m[k], out_v[k] = d_[None], m_[None], v_[None]

    order = ("norm_mix_g", "w_in", "ret_norm_g", "w_ret_o", "q_a_norm_g", "w_q_b", "kv_a_norm_g", "w_kv_b",
             "w_mla_o", "w_out", "norm_mlp_g", "w_up", "w_down", "norm_f_g")
    return (loss, dx.reshape(1, S, D), *[out_g[k] for k in order], *[out_d[k] for k in order],
            *[out_m[k] for k in order], *[out_v[k] for k in order])
```

```python
import math

import jax
import jax.numpy as jnp
from jax import lax
from jax.experimental import pallas as pl
from jax.experimental.pallas import tpu as pltpu

F32 = jnp.float32
BF16 = jnp.bfloat16

EPS = 1e-6
ROPE_THETA = 10000.0
CHUNK = 64
RET_QK = 128
RET_V = 256
RET_HEAD_COLS = 2 * RET_QK + 2 * RET_V
QK_NOPE = 128
QK_ROPE = 64
V_HEAD = 128
LANES = 128
LOG2E = math.log2(math.e)

ADAM_LR = 0.001
ADAM_B1 = 0.9
ADAM_B2 = 0.999
ADAM_EPS = 1e-08
ADAM_WD = 0.01
ADAM_STEP = 10

N_CHIPS = 4
VMEM_LIMIT = 56 * 1024 * 1024
MESH = pl.DeviceIdType.MESH
NEG = -1e30


def _pallas(body, **kw):
    return pl.pallas_call(body, **kw)


def _params(sem=None):
    return pltpu.CompilerParams(dimension_semantics=sem, vmem_limit_bytes=VMEM_LIMIT)


def _tile(n, want):
    t = min(n, want)
    while n % t:
        t //= 2
    return t


_ANY = pl.BlockSpec(memory_space=pl.ANY)


def _mm(a, b, *, mode, outs, name, epi=None, extras=(), deps=(), out_shards=False, tm=1024, tn=1024, tk=2048):
    shards = b.shape[0] if b.ndim == 3 else 1
    brows, bcols = b.shape[-2], b.shape[-1] * shards
    if mode == "nn":
        (M, K), N = a.shape, bcols
    elif mode == "nt":
        (M, K), N = a.shape, brows
    else:
        (K, M), N = a.shape, bcols
    tm = _tile(M, tm)
    tn = _tile(N // (shards if mode == "nn" else 1) // (N_CHIPS if out_shards else 1), tn)
    tk = _tile(K // (shards if mode == "nt" else 1), tk)
    nk = K // tk
    if mode == "nn":
        a_spec = pl.BlockSpec((tm, tk), lambda i, j, k: (i, k))
        dims = (((1,), (0,)), ((), ()))
        if shards > 1:
            per = N // shards // tn
            b_spec = pl.BlockSpec((None, tk, tn), lambda i, j, k: (j // per, k, j % per))
        else:
            b_spec = pl.BlockSpec((tk, tn), lambda i, j, k: (k, j))
    elif mode == "nt":
        a_spec = pl.BlockSpec((tm, tk), lambda i, j, k: (i, k))
        dims = (((1,), (1,)), ((), ()))
        if shards > 1:
            per = K // shards // tk
            b_spec = pl.BlockSpec((None, tn, tk), lambda i, j, k: (k // per, j, k % per))
        else:
            b_spec = pl.BlockSpec((tn, tk), lambda i, j, k: (j, k))
    else:
        assert shards == 1
        a_spec = pl.BlockSpec((tk, tm), lambda i, j, k: (k, i))
        b_spec = pl.BlockSpec((tk, tn), lambda i, j, k: (k, j))
        dims = (((0,), (0,)), ((), ()))
    if out_shards:
        assert not extras
        oper = N // N_CHIPS // tn
        o_spec = pl.BlockSpec((None, tm, tn), lambda i, j, k: (j // oper, i, j % oper))
        o_shape = (N_CHIPS, M, N // N_CHIPS)
    else:
        o_spec = pl.BlockSpec((tm, tn), lambda i, j, k: (i, j))
        o_shape = (M, N)
    n_ex, n_out, n_dep = len(extras), len(outs), len(deps)
    if epi is None:
        epi = lambda acc: (acc,)

    def body(*refs):
        a_ref, b_ref = refs[0], refs[1]
        ex_refs = refs[2:2 + n_ex]
        o_refs = refs[2 + n_ex + n_dep:2 + n_ex + n_dep + n_out]
        part = lax.dot_general(a_ref[...].astype(BF16), b_ref[...].astype(BF16), dims,
                               preferred_element_type=F32)

        def finish(acc):
            vals = epi(acc, *[r[...] for r in ex_refs])
            for r, v in zip(o_refs, vals):
                r[...] = v.astype(r.dtype)

        if nk == 1:
            finish(part)
        else:
            acc_ref = refs[-1]
            k = pl.program_id(2)

            @pl.when(k == 0)
            def _():
                acc_ref[...] = part

            @pl.when(k > 0)
            def _():
                acc_ref[...] += part

            @pl.when(k == nk - 1)
            def _():
                finish(acc_ref[...])

    res = _pallas(
        body, name=name, grid=(M // tm, N // tn, nk),
        in_specs=[a_spec, b_spec] + [o_spec] * n_ex + [_ANY] * n_dep,
        out_specs=[o_spec] * n_out,
        out_shape=[jax.ShapeDtypeStruct(o_shape, d) for d in outs],
        scratch_shapes=[pltpu.VMEM((tm, tn), F32)] if nk > 1 else [],
        compiler_params=_params(("parallel", "parallel", "arbitrary")),
    )(a, b, *extras, *deps)
    return res[0] if n_out == 1 else res


def _rmsnorm_fwd(x, g, *, name, width=None, col=0, tr=256):
    S = x.shape[0]
    W = x.shape[1] if width is None else width
    tr = _tile(S, tr)

    def body(x_ref, g_ref, y_ref, r_ref):
        xv = x_ref[...]
        rstd = lax.rsqrt(jnp.mean(xv * xv, axis=-1, keepdims=True) + EPS)
        y_ref[...] = (xv * rstd * g_ref[...]).astype(BF16)
        r_ref[...] = rstd

    return _pallas(
        body, name=name, grid=(S // tr,),
        in_specs=[pl.BlockSpec((tr, W), lambda i: (i, col)), pl.BlockSpec((1, W), lambda i: (0, 0))],
        out_specs=[pl.BlockSpec((tr, W), lambda i: (i, 0)), pl.BlockSpec((tr, 1), lambda i: (i, 0))],
        out_shape=[jax.ShapeDtypeStruct((S, W), BF16), jax.ShapeDtypeStruct((S, 1), F32)],
        compiler_params=_params(("parallel",)),
    )(x, g)


def _rmsnorm_bwd(dy, x, rstd, g, *, name, res=None, into=None, deps=(), width=None, col=0, tr=256):
    S = x.shape[0]
    W = x.shape[1] if width is None else width
    tr = _tile(S, tr)
    has_res = res is not None

    def body(*refs):
        dy_ref, x_ref, r_ref, g_ref = refs[:4]
        dx_ref, dg_ref = refs[-2], refs[-1]
        rstd_v = r_ref[...]
        xhat = x_ref[...] * rstd_v
        dyv = dy_ref[...].astype(F32)
        dyg = dyv * g_ref[...]
        dx = rstd_v * (dyg - xhat * jnp.mean(dyg * xhat, axis=-1, keepdims=True))
        if has_res:
            dx = dx + refs[4][...]
        dx_ref[...] = dx.astype(dx_ref.dtype)
        part = jnp.sum(dyv * xhat, axis=0, keepdims=True)

        @pl.when(pl.program_id(0) == 0)
        def _():
            dg_ref[...] = part

        @pl.when(pl.program_id(0) > 0)
        def _():
            dg_ref[...] += part

    row = pl.BlockSpec((tr, W), lambda i: (i, 0))
    ins = [dy, x, rstd, g] + ([res] if has_res else [])
    in_specs = [row, pl.BlockSpec((tr, W), lambda i: (i, col)), pl.BlockSpec((tr, 1), lambda i: (i, 0)),
                pl.BlockSpec((1, W), lambda i: (0, 0))] + ([row] if has_res else [])
    if into is None:
        dx_spec, dx_shape, alias = row, jax.ShapeDtypeStruct((S, W), F32), {}
    else:
        buf, col_out = into
        ins.append(buf)
        in_specs.append(_ANY)
        dx_spec = pl.BlockSpec((tr, W), lambda i: (i, col_out))
        dx_shape = jax.ShapeDtypeStruct(buf.shape, buf.dtype)
        alias = {len(ins) - 1: 0}
    ins += list(deps)
    in_specs += [_ANY] * len(deps)
    return _pallas(
        body, name=name, grid=(S // tr,), in_specs=in_specs,
        out_specs=[dx_spec, pl.BlockSpec((1, W), lambda i: (0, 0))],
        out_shape=[dx_shape, jax.ShapeDtypeStruct((1, W), F32)],
        input_output_aliases=alias,
        compiler_params=_params(("arbitrary",)),
    )(*ins)


def _final_loss(h2, g, target, *, tr=256):
    S, D = h2.shape
    tr = _tile(S, tr)

    def body(h_ref, g_ref, t_ref, loss_ref, dh_ref, dg_ref):
        hv = h_ref[...]
        rstd = lax.rsqrt(jnp.mean(hv * hv, axis=-1, keepdims=True) + EPS)
        xhat = hv * rstd
        e = xhat * g_ref[...] - t_ref[...]
        lpart = (0.5 / D) * jnp.sum(jnp.sum(e * e, axis=-1, keepdims=True), axis=0, keepdims=True)
        dy = e * (1.0 / D)
        dyg = dy * g_ref[...]
        dh_ref[...] = rstd * (dyg - xhat * jnp.mean(dyg * xhat, axis=-1, keepdims=True))
        gpart = jnp.sum(dy * xhat, axis=0, keepdims=True)

        @pl.when(pl.program_id(0) == 0)
        def _():
            loss_ref[...] = lpart
            dg_ref[...] = gpart

        @pl.when(pl.program_id(0) > 0)
        def _():
            loss_ref[...] += lpart
            dg_ref[...] += gpart

    row = pl.BlockSpec((tr, D), lambda i: (i, 0))
    vec = pl.BlockSpec((1, D), lambda i: (0, 0))
    return _pallas(
        body, name="final_loss", grid=(S // tr,), in_specs=[row, vec, row],
        out_specs=[pl.BlockSpec((1, 1), lambda i: (0, 0)), row, vec],
        out_shape=[jax.ShapeDtypeStruct((1, 1), F32), jax.ShapeDtypeStruct((S, D), F32),
                   jax.ShapeDtypeStruct((1, D), F32)],
        compiler_params=_params(("arbitrary",)),
    )(h2, g, target)


def _sigmoid(v):
    return 1.0 / (1.0 + jnp.exp(-v))


def _merge_fwd(proj, y_ret, y_mla, D, off_gret, off_gmla, *, tr=256, tc=1024):
    S = y_ret.shape[0]
    tr, tc = _tile(S, tr), _tile(D, tc)
    b_ret, b_mla = off_gret // tc, off_gmla // tc

    def body(gr_ref, gm_ref, yr_ref, ym_ref, o_ref):
        o_ref[...] = (_sigmoid(gr_ref[...]) * yr_ref[...] + _sigmoid(gm_ref[...]) * ym_ref[...]).astype(BF16)

    blk = pl.BlockSpec((tr, tc), lambda i, j: (i, j))
    return _pallas(
        body, name="merge_fwd", grid=(S // tr, D // tc),
        in_specs=[pl.BlockSpec((tr, tc), lambda i, j: (i, b_ret + j)),
                  pl.BlockSpec((tr, tc), lambda i, j: (i, b_mla + j)), blk, blk],
        out_specs=blk, out_shape=jax.ShapeDtypeStruct((S, D), BF16),
        compiler_params=_params(("parallel", "parallel")),
    )(proj, proj, y_ret, y_mla)


def _merge_bwd(dmerged, proj, y_ret, y_mla, D, off_gret, *, tr=256):
    S = y_ret.shape[0]
    tr = _tile(S, tr)
    b0 = off_gret // D

    def body(dm_ref, g_ref, yr_ref, ym_ref, dp_ref, dyr_ref, dym_ref):
        dm = dm_ref[...]
        sg = _sigmoid(g_ref[...])

        @pl.when(pl.program_id(1) == 0)
        def _():
            dyr_ref[...] = (dm * sg).astype(BF16)
            dp_ref[...] = (dm * yr_ref[...] * sg * (1.0 - sg)).astype(BF16)

        @pl.when(pl.program_id(1) == 1)
        def _():
            dym_ref[...] = (dm * sg).astype(BF16)
            dp_ref[...] = (dm * ym_ref[...] * sg * (1.0 - sg)).astype(BF16)

    blk = pl.BlockSpec((tr, D), lambda i, j: (i, 0))
    return _pallas(
        body, name="merge_bwd", grid=(S // tr, 2),
        in_specs=[blk, pl.BlockSpec((tr, D), lambda i, j: (i, b0 + j)), blk, blk],
        out_specs=[pl.BlockSpec((tr, D), lambda i, j: (i, b0 + j)), blk, blk],
        out_shape=[jax.ShapeDtypeStruct(proj.shape, BF16), jax.ShapeDtypeStruct((S, D), BF16),
                   jax.ShapeDtypeStruct((S, D), BF16)],
        compiler_params=_params(("parallel", "arbitrary")),
    )(dmerged, proj, y_ret, y_mla)


def _rope128(t, cos_full, sin_signed):
    return t * cos_full + pltpu.roll(t, RET_QK // 2, 1) * sin_signed


def _rope128_t(d, cos_full, sin_signed):
    return d * cos_full + pltpu.roll(d * sin_signed, RET_QK // 2, 1)


def _ret_consts(lg, T):
    pos = lax.broadcasted_iota(jnp.int32, (T, 1), 0).astype(F32)
    qd = jnp.exp(lg * (pos + 1.0))
    kd = jnp.exp(lg * (T - 1.0 - pos))
    n = lax.broadcasted_iota(jnp.int32, (T, T), 0)
    m = lax.broadcasted_iota(jnp.int32, (T, T), 1)
    vis = (m // CHUNK) <= (n // CHUNK)
    dist = jnp.abs(n - m).astype(F32)
    decay = jnp.where(vis, jnp.exp(lg * dist), 0.0)
    cdec = jnp.exp(lg * float(T))
    return qd, kd, decay, cdec


def _dot(a, b, dims):
    return lax.dot_general(a.astype(BF16), b.astype(BF16), (dims, ((), ())), preferred_element_type=F32)


NN = ((1,), (0,))
NT = ((1,), (1,))
TN = ((0,), (0,))
_RQ = slice(0, RET_QK)
_RK = slice(RET_QK, 2 * RET_QK)
_RV = slice(2 * RET_QK, 2 * RET_QK + RET_V)
_RG = slice(2 * RET_QK + RET_V, RET_HEAD_COLS)


def _ret_fwd(proj, cosr, sinr, lgam, gain, RH, *, T):
    S = proj.shape[0]
    nb = S // T
    scale = RET_QK ** -0.5

    def body(p_ref, cos_ref, sin_ref, lg_ref, gain_ref, ry_ref, gated_ref, st_ref, state):
        b = pl.program_id(1)

        @pl.when(b == 0)
        def _():
            state[...] = jnp.zeros_like(state)

        lg = lg_ref[0:1, 0:1]
        qd, kd, decay, cdec = _ret_consts(lg, T)
        cosv, sinv = cos_ref[...], sin_ref[...]
        q = _rope128(p_ref[:, _RQ], cosv, sinv)
        k = _rope128(p_ref[:, _RK], cosv, sinv) * scale
        v = p_ref[:, _RV]
        sprev = state[...]
        st_ref[...] = sprev
        a = _dot(q, k, NT) * decay
        o = _dot(a, v, NN) + _dot(q * qd, sprev, NN)
        state[...] = sprev * cdec + _dot(k * kd, v, TN)
        ry_ref[...] = o
        mu = jnp.mean(o, axis=-1, keepdims=True)
        oc = o - mu
        var = jnp.mean(oc * oc, axis=-1, keepdims=True)
        t = oc * lax.rsqrt(var + EPS) * gain_ref[...]
        gv = p_ref[:, _RG]
        gated_ref[...] = (t * (gv * _sigmoid(gv))).astype(BF16)

    return _pallas(
        body, name="ret_fwd", grid=(RH, nb),
        in_specs=[pl.BlockSpec((T, RET_HEAD_COLS), lambda h, b: (b, h)),
                  pl.BlockSpec((T, RET_QK), lambda h, b: (b, 0)),
                  pl.BlockSpec((T, RET_QK), lambda h, b: (b, 0)),
                  pl.BlockSpec((None, 8, LANES), lambda h, b: (h, 0, 0)),
                  pl.BlockSpec((1, RET_V), lambda h, b: (0, h))],
        out_specs=[pl.BlockSpec((T, RET_V), lambda h, b: (b, h)),
                   pl.BlockSpec((T, RET_V), lambda h, b: (b, h)),
                   pl.BlockSpec((None, None, RET_QK, RET_V), lambda h, b: (h, b, 0, 0))],
        out_shape=[jax.ShapeDtypeStruct((S, RH * RET_V), F32), jax.ShapeDtypeStruct((S, RH * RET_V), BF16),
                   jax.ShapeDtypeStruct((RH, nb, RET_QK, RET_V), F32)],
        scratch_shapes=[pltpu.VMEM((RET_QK, RET_V), F32)],
        compiler_params=_params(("parallel", "arbitrary")),
    )(proj, cosr, sinr, lgam, gain)


def _ret_bwd(proj, cosr, sinr, lgam, gain, ry, dgated, states, dproj, RH, *, T):
    S = proj.shape[0]
    nb = S // T
    scale = RET_QK ** -0.5

    def body(p_ref, cos_ref, sin_ref, lg_ref, gain_ref, ry_ref, dg_ref, st_ref, _, dp_ref, dgain_ref, dstate):
        b = pl.program_id(1)

        @pl.when(b == 0)
        def _():
            dstate[...] = jnp.zeros_like(dstate)

        lg = lg_ref[0:1, 0:1]
        qd, kd, decay, cdec = _ret_consts(lg, T)
        cosv, sinv = cos_ref[...], sin_ref[...]
        q = _rope128(p_ref[:, _RQ], cosv, sinv)
        k = _rope128(p_ref[:, _RK], cosv, sinv) * scale
        v = p_ref[:, _RV]
        sprev = st_ref[...]
        ds_new = dstate[...]
        o = ry_ref[...]
        mu = jnp.mean(o, axis=-1, keepdims=True)
        oc = o - mu
        rstd = lax.rsqrt(jnp.mean(oc * oc, axis=-1, keepdims=True) + EPS)
        ryn = oc * rstd
        gainv = gain_ref[...]
        gv = p_ref[:, _RG]
        sg = _sigmoid(gv)
        dgt = dg_ref[...]
        dt = dgt * (gv * sg)
        dp_ref[:, _RG] = (dgt * (ryn * gainv) * (sg * (1.0 + gv * (1.0 - sg)))).astype(BF16)
        gpart = jnp.sum(dt * ryn, axis=0, keepdims=True)

        @pl.when(b == 0)
        def _():
            dgain_ref[...] = gpart

        @pl.when(b > 0)
        def _():
            dgain_ref[...] += gpart

        dryn = dt * gainv
        do = rstd * (dryn - jnp.mean(dryn, axis=-1, keepdims=True)
                     - ryn * jnp.mean(dryn * ryn, axis=-1, keepdims=True))
        a = _dot(q, k, NT) * decay
        kdk = k * kd
        qdq = q * qd
        dp_ref[:, _RV] = (_dot(a, do, TN) + _dot(kdk, ds_new, NN)).astype(BF16)
        dp = _dot(do, v, NT) * decay
        dq = _dot(dp, k, NN) + _dot(do, sprev, NT) * qd
        dk = (_dot(dp, q, TN) + _dot(v, ds_new, NT) * kd) * scale
        dstate[...] = ds_new * cdec + _dot(qdq, do, TN)
        dp_ref[:, _RQ] = _rope128_t(dq, cosv, sinv).astype(BF16)
        dp_ref[:, _RK] = _rope128_t(dk, cosv, sinv).astype(BF16)

    rb = lambda b: nb - 1 - b
    return _pallas(
        body, name="ret_bwd", grid=(RH, nb),
        in_specs=[pl.BlockSpec((T, RET_HEAD_COLS), lambda h, b: (rb(b), h)),
                  pl.BlockSpec((T, RET_QK), lambda h, b: (rb(b), 0)),
                  pl.BlockSpec((T, RET_QK), lambda h, b: (rb(b), 0)),
                  pl.BlockSpec((None, 8, LANES), lambda h, b: (h, 0, 0)),
                  pl.BlockSpec((1, RET_V), lambda h, b: (0, h)),
                  pl.BlockSpec((T, RET_V), lambda h, b: (rb(b), h)),
                  pl.BlockSpec((T, RET_V), lambda h, b: (rb(b), h)),
                  pl.BlockSpec((None, None, RET_QK, RET_V), lambda h, b: (h, rb(b), 0, 0)),
                  _ANY],
        out_specs=[pl.BlockSpec((T, RET_HEAD_COLS), lambda h, b: (rb(b), h)),
                   pl.BlockSpec((1, RET_V), lambda h, b: (0, h))],
        out_shape=[jax.ShapeDtypeStruct(dproj.shape, dproj.dtype), jax.ShapeDtypeStruct((1, RH * RET_V), F32)],
        scratch_shapes=[pltpu.VMEM((RET_QK, RET_V), F32)],
        input_output_aliases={8: 0},
        compiler_params=_params(("parallel", "arbitrary")),
    )(proj, cosr, sinr, lgam, gain, ry, dgated, states, dproj)


def _rope_pe(t, c, s1, s2):
    return t * c + pltpu.roll(t, LANES - QK_ROPE // 2, 1) * s1 + pltpu.roll(t, QK_ROPE // 2, 1) * s2


def _rope_pe_t(d, c, s1, s2):
    return d * c + pltpu.roll(d * s1, QK_ROPE // 2, 1) + pltpu.roll(d * s2, LANES - QK_ROPE // 2, 1)


ATTN_C2 = (QK_NOPE + QK_ROPE) ** -0.5 * LOG2E


def _qkv_proj(cqn, ckvn, wq, wkv, kpe, tabs, MH, *, tm=512, heads=4):
    S = cqn.shape[0]
    tm = _tile(S, tm)
    hb = _tile(MH, heads)
    W = 2 * LANES
    c_t, s1_t, s2_t = tabs

    def body(cq_ref, ckv_ref, wq_ref, wkv_ref, kpe_ref, c_ref, s1_ref, s2_ref, qf_ref, kf_ref, v_ref):
        c, s1, s2 = c_ref[...], s1_ref[...], s2_ref[...]
        q = _dot(cq_ref[...], wq_ref[...], NN)
        kv = _dot(ckv_ref[...], wkv_ref[...], NN)
        kper = _rope_pe(kpe_ref[...], c, s1, s2).astype(BF16)
        for h in range(hb):
            lo, mid, hi = h * W, h * W + QK_NOPE, (h + 1) * W
            qf_ref[:, lo:mid] = (q[:, lo:mid] * ATTN_C2).astype(BF16)
            qf_ref[:, mid:hi] = (_rope_pe(q[:, mid:hi], c, s1, s2) * ATTN_C2).astype(BF16)
            kf_ref[:, lo:mid] = kv[:, lo:mid].astype(BF16)
            kf_ref[:, mid:hi] = kper
            v_ref[:, h * V_HEAD:(h + 1) * V_HEAD] = kv[:, mid:hi].astype(BF16)

    tab = pl.BlockSpec((tm, LANES), lambda i, j: (i, 0))
    grp = pl.BlockSpec((tm, hb * W), lambda i, j: (i, j))
    return _pallas(
        body, name="qkv_proj", grid=(S // tm, MH // hb),
        in_specs=[pl.BlockSpec((tm, cqn.shape[1]), lambda i, j: (i, 0)),
                  pl.BlockSpec((tm, ckvn.shape[1]), lambda i, j: (i, 0)),
                  pl.BlockSpec((wq.shape[0], hb * W), lambda i, j: (0, j)),
                  pl.BlockSpec((wkv.shape[0], hb * W), lambda i, j: (0, j)), tab, tab, tab, tab],
        out_specs=[grp, grp, pl.BlockSpec((tm, hb * V_HEAD), lambda i, j: (i, j))],
        out_shape=[jax.ShapeDtypeStruct((S, MH * W), BF16)] * 2 + [jax.ShapeDtypeStruct((S, MH * V_HEAD), BF16)],
        compiler_params=_params(("parallel", "parallel")),
    )(cqn, ckvn, wq, wkv, kpe, c_t, s1_t, s2_t)


def _chunk_mask(T):
    n = lax.broadcasted_iota(jnp.int32, (T, T), 0)
    m = lax.broadcasted_iota(jnp.int32, (T, T), 1)
    return (m // CHUNK) <= (n // CHUNK)


def _lanes_to(v, width):
    return jnp.tile(v, (1, width // LANES))


def _attn_fwd(qf, kf, vb, MH, *, T):
    S = qf.shape[0]
    nt = S // T

    def body(q_ref, k_ref, v_ref, o_ref, lse_ref, m_sc, l_sc, acc_sc, s_a, s_b):
        qi = pl.program_id(1)
        m_sc[...] = jnp.full_like(m_sc, NEG)
        l_sc[...] = jnp.zeros_like(l_sc)
        acc_sc[...] = jnp.zeros_like(acc_sc)

        def rows_of(kt):
            return pl.ds(pl.multiple_of(kt * T, T), T)

        def scores(kt):
            return _dot(q_ref[...], k_ref[rows_of(kt), :], NT)

        def update(s, kt):
            m_prev = m_sc[...]
            m_new = jnp.maximum(m_prev, jnp.max(s, axis=-1, keepdims=True))
            alpha = jnp.exp2(m_prev - m_new)
            p = jnp.exp2(s - _lanes_to(m_new, T))
            l_sc[...] = alpha * l_sc[...] + jnp.sum(p, axis=-1, keepdims=True)
            acc_sc[...] = alpha * acc_sc[...] + _dot(p, v_ref[rows_of(kt), :], NN)
            m_sc[...] = m_new

        update(jnp.where(_chunk_mask(T), scores(qi), NEG), qi)

        @pl.when(qi % 2 == 1)
        def _():
            update(scores(qi - 1), qi - 1)

        pairs = qi // 2

        @pl.when(pairs > 0)
        def _():
            s_a[...] = scores(0)

        def pair(j, carry):
            t0 = 2 * j
            s_b[...] = scores(t0 + 1)
            update(s_a[...], t0)
            s_a[...] = scores(jnp.minimum(t0 + 2, 2 * pairs - 1))
            update(s_b[...], t0 + 1)
            return carry

        lax.fori_loop(0, pairs, pair, 0)
        l = l_sc[...]
        o_ref[...] = acc_sc[...] / l
        lse_ref[...] = m_sc[...] + jnp.log(l) * LOG2E

    return _pallas(
        body, name="attn_fwd", grid=(MH, nt),
        in_specs=[pl.BlockSpec((T, 2 * LANES), lambda h, i: (i, h)),
                  pl.BlockSpec((S, 2 * LANES), lambda h, i: (0, h)),
                  pl.BlockSpec((S, LANES), lambda h, i: (0, h))],
        out_specs=[pl.BlockSpec((T, LANES), lambda h, i: (i, h)),
                   pl.BlockSpec((None, T, LANES), lambda h, i: (h, i, 0))],
        out_shape=[jax.ShapeDtypeStruct((S, MH * LANES), F32), jax.ShapeDtypeStruct((MH, S, LANES), F32)],
        scratch_shapes=[pltpu.VMEM((T, LANES), F32), pltpu.VMEM((T, LANES), F32), pltpu.VMEM((T, LANES), F32),
                        pltpu.VMEM((T, T), F32), pltpu.VMEM((T, T), F32)],
        compiler_params=_params(("parallel", "parallel")),
    )(qf, kf, vb)


def _attn_delta(do, o, MH, *, deps=(), tr=512):
    S = do.shape[0]
    tr = _tile(S, tr)

    def body(do_ref, o_ref, *rest):
        d_ref, dob_ref = rest[-2], rest[-1]
        dov = do_ref[...]
        d_ref[...] = jnp.broadcast_to(jnp.sum(dov * o_ref[...], axis=-1, keepdims=True), (tr, LANES))
        dob_ref[...] = dov.astype(BF16)

    head = pl.BlockSpec((tr, LANES), lambda i, h: (i, h))
    return _pallas(
        body, name="attn_delta", grid=(S // tr, MH), in_specs=[head, head] + [_ANY] * len(deps),
        out_specs=[pl.BlockSpec((None, tr, LANES), lambda i, h: (h, i, 0)), head],
        out_shape=[jax.ShapeDtypeStruct((MH, S, LANES), F32), jax.ShapeDtypeStruct((S, MH * LANES), BF16)],
        compiler_params=_params(("parallel", "parallel")),
    )(do, o, *deps)


def _attn_bwd(qf, kf, vb, dob, lse2, delta, MH, *, T):
    S = qf.shape[0]
    nt = S // T

    def body(q_ref, k_ref, v_ref, do_ref, lse_ref, dl_ref, dq_ref, dkv_ref, dkpe_ref, dk_sc, dv_sc,
             s_a, dp_a, s_b, dp_b):
        kj = pl.program_id(1)

        @pl.when(kj == 0)
        def _():
            dq_ref[...] = jnp.zeros_like(dq_ref)

        dk_sc[...] = jnp.zeros_like(dk_sc)
        dv_sc[...] = jnp.zeros_like(dv_sc)

        def rows_of(qt):
            return pl.ds(pl.multiple_of(qt * T, T), T)

        def products(qt):
            rows = rows_of(qt)
            return _dot(q_ref[rows, :], k_ref[...], NT), _dot(do_ref[rows, :], v_ref[...], NT)

        def update(s, dp, qt):
            rows = rows_of(qt)
            q, dov = q_ref[rows, :], do_ref[rows, :]
            p = jnp.exp2(s - _lanes_to(lse_ref[rows, :], T))
            ds = p * (dp - _lanes_to(dl_ref[rows, :], T))
            dv_sc[...] += _dot(p, dov, TN)
            dk_sc[...] += _dot(ds, q, TN)
            dq_ref[rows, :] += _dot(ds, k_ref[...], NN)

        s, dp = products(kj)
        update(jnp.where(_chunk_mask(T), s, NEG), dp, kj)
        rest = nt - 1 - kj

        @pl.when(rest % 2 == 1)
        def _():
            s1, dp1 = products(nt - 1)
            update(s1, dp1, nt - 1)

        pairs = rest // 2

        @pl.when(pairs > 0)
        def _():
            s_a[...], dp_a[...] = products(kj + 1)

        def pair(j, carry):
            t0 = kj + 1 + 2 * j
            s_b[...], dp_b[...] = products(t0 + 1)
            update(s_a[...], dp_a[...], t0)
            s_a[...], dp_a[...] = products(jnp.minimum(t0 + 2, kj + 2 * pairs))
            update(s_b[...], dp_b[...], t0 + 1)
            return carry

        lax.fori_loop(0, pairs, pair, 0)
        dkv_ref[:, :QK_NOPE] = (dk_sc[:, :QK_NOPE] * (1.0 / LOG2E)).astype(BF16)
        dkv_ref[:, QK_NOPE:] = dv_sc[...].astype(BF16)
        dkpe_ref[...] = dk_sc[:, QK_NOPE:] * (1.0 / LOG2E)

    stat = pl.BlockSpec((None, S, LANES), lambda h, j: (h, 0, 0))
    return _pallas(
        body, name="attn_bwd", grid=(MH, nt),
        in_specs=[pl.BlockSpec((S, 2 * LANES), lambda h, j: (0, h)),
                  pl.BlockSpec((T, 2 * LANES), lambda h, j: (j, h)),
                  pl.BlockSpec((T, LANES), lambda h, j: (j, h)),
                  pl.BlockSpec((S, LANES), lambda h, j: (0, h)), stat, stat],
        out_specs=[pl.BlockSpec((S, 2 * LANES), lambda h, j: (0, h)),
                   pl.BlockSpec((T, 2 * LANES), lambda h, j: (j, h)),
                   pl.BlockSpec((T, LANES), lambda h, j: (j, h))],
        out_shape=[jax.ShapeDtypeStruct((S, MH * 2 * LANES), F32), jax.ShapeDtypeStruct((S, MH * 2 * LANES), BF16),
                   jax.ShapeDtypeStruct((S, MH * LANES), F32)],
        scratch_shapes=[pltpu.VMEM((T, 2 * LANES), F32), pltpu.VMEM((T, LANES), F32)] + [pltpu.VMEM((T, T), F32)] * 4,
        compiler_params=_params(("parallel", "arbitrary")),
    )(qf, kf, vb, dob, lse2, delta)


def _attn_post(dqf, dkpe_h, tabs, MH, *, tr=512):
    S = dqf.shape[0]
    tr = _tile(S, tr)
    c_t, s1_t, s2_t = tabs
    scale = (QK_NOPE + QK_ROPE) ** -0.5

    def body(dq_ref, dk_ref, c_ref, s1_ref, s2_ref, dqa_ref, dkpe_ref, acc):
        h = pl.program_id(1)
        c, s1, s2 = c_ref[...], s1_ref[...], s2_ref[...]
        dqa_ref[:, :QK_NOPE] = (dq_ref[:, :QK_NOPE] * scale).astype(BF16)
        dqa_ref[:, QK_NOPE:] = (_rope_pe_t(dq_ref[:, QK_NOPE:], c, s1, s2) * scale).astype(BF16)

        @pl.when(h == 0)
        def _():
            acc[...] = dk_ref[...]

        @pl.when(h > 0)
        def _():
            acc[...] += dk_ref[...]

        @pl.when(h == MH - 1)
        def _():
            dkpe_ref[...] = _rope_pe_t(acc[...], c, s1, s2).astype(BF16)

    tab = pl.BlockSpec((tr, LANES), lambda i, h: (i, 0))
    head2 = pl.BlockSpec((tr, 2 * LANES), lambda i, h: (i, h))
    return _pallas(
        body, name="attn_post", grid=(S // tr, MH),
        in_specs=[head2, pl.BlockSpec((tr, LANES), lambda i, h: (i, h)), tab, tab, tab],
        out_specs=[head2, tab],
        out_shape=[jax.ShapeDtypeStruct((S, MH * 2 * LANES), BF16), jax.ShapeDtypeStruct((S, LANES), BF16)],
        scratch_shapes=[pltpu.VMEM((tr, LANES), F32)],
        compiler_params=_params(("parallel", "arbitrary")),
    )(dqf, dkpe_h, c_t, s1_t, s2_t)


def _block_rows(R, C, block_bytes=2 << 20):
    tr = 8
    while tr * 2 * C * 4 <= block_bytes:
        tr *= 2
    return _tile(R, tr)


def _rows_call(fn, ins, out_dtypes, *, name):
    R, C = ins[0].shape
    tr = _block_rows(R, C)
    n_in = len(ins)

    def body(*refs):
        vals = fn(*[r[...] for r in refs[:n_in]])
        for r, v in zip(refs[n_in:], vals):
            r[...] = v.astype(r.dtype)

    blk = pl.BlockSpec((tr, C), lambda i: (i, 0))
    res = _pallas(
        body, name=name, grid=(R // tr,), in_specs=[blk] * n_in, out_specs=[blk] * len(out_dtypes),
        out_shape=[jax.ShapeDtypeStruct((R, C), d) for d in out_dtypes],
        compiler_params=_params(("parallel",)),
    )(*ins)
    return res


def _adamw_vals(w, g, m, v):
    m = ADAM_B1 * m + (1.0 - ADAM_B1) * g
    v = ADAM_B2 * v + (1.0 - ADAM_B2) * (g * g)
    m_hat = m / (1.0 - ADAM_B1 ** ADAM_STEP)
    v_hat = v / (1.0 - ADAM_B2 ** ADAM_STEP)
    delta = -ADAM_LR * (m_hat / (jnp.sqrt(v_hat) + ADAM_EPS) + ADAM_WD * w)
    return delta, m, v


def _sum_pair(p, theirs, place, *, name):
    _, R, C = p.shape
    R2 = R // 2
    tr = _block_rows(R2, C)
    p4 = p.reshape(N_CHIPS, 2, R2, C)

    def body(place_ref, a_ref, b_ref, o_ref):
        o_ref[...] = (a_ref[...].astype(F32) + b_ref[...].astype(F32)).astype(BF16)

    spec = pltpu.PrefetchScalarGridSpec(
        num_scalar_prefetch=1, grid=(N_CHIPS, R2 // tr),
        in_specs=[pl.BlockSpec((None, None, tr, C), lambda q, i, pr: (q, pr[0], i, 0)),
                  pl.BlockSpec((None, tr, C), lambda q, i, pr: (q, i, 0))],
        out_specs=pl.BlockSpec((None, tr, C), lambda q, i, pr: (q, i, 0)))
    return _pallas(body, name=name, grid_spec=spec, out_shape=jax.ShapeDtypeStruct((N_CHIPS, R2, C), BF16),
                   compiler_params=_params(("parallel", "parallel")))(place, p4, theirs)


def _sum_chips(p, theirs, recv, place, *, name):
    _, R, C = p.shape
    R2 = R // 2
    tr = _block_rows(R2, C)
    p4 = p.reshape(N_CHIPS, 2, R2, C)

    def body(place_ref, a_ref, b_ref, r0_ref, r1_ref, r2_ref, o_ref):
        own = a_ref[...].astype(F32) + b_ref[...].astype(F32)
        o_ref[...] = ((own + r0_ref[...].astype(F32)) + r1_ref[...].astype(F32)) + r2_ref[...].astype(F32)

    def slot(k):
        return pl.BlockSpec((None, tr, C), lambda i, pr: (k, i, 0))

    spec = pltpu.PrefetchScalarGridSpec(
        num_scalar_prefetch=1, grid=(R2 // tr,),
        in_specs=[pl.BlockSpec((None, None, tr, C), lambda i, pr: (pr[1], pr[0], i, 0)),
                  pl.BlockSpec((None, tr, C), lambda i, pr: (pr[1], i, 0)), slot(0), slot(1), slot(2)],
        out_specs=pl.BlockSpec((None, tr, C), lambda i, pr: (pr[0], i, 0)))
    return _pallas(body, name=name, grid_spec=spec, out_shape=jax.ShapeDtypeStruct((2, R2, C), F32),
                   compiler_params=_params(("parallel",)))(place, p4, theirs, recv, recv, recv)


def _me():
    return lax.axis_index("x"), lax.axis_index("y"), lax.axis_index("c")


def _other_chips(x, y):
    return [(1 - x, y), (x, 1 - y), (1 - x, 1 - y)]


def _rcopy(src, dst, ssem, rsem, dev):
    return pltpu.make_async_remote_copy(src_ref=src, dst_ref=dst, send_sem=ssem, recv_sem=rsem,
                                        device_id=dev, device_id_type=MESH)


def _cast_into_slot(w, place, *, name, deps=()):
    R, C = w.shape
    tr = _block_rows(R, C)

    def body(place_ref, w_ref, *rest):
        rest[-1][...] = w_ref[...].astype(BF16)

    spec = pltpu.PrefetchScalarGridSpec(
        num_scalar_prefetch=1, grid=(R // tr,),
        in_specs=[pl.BlockSpec((tr, C), lambda i, pr: (i, 0))] + [_ANY] * len(deps),
        out_specs=pl.BlockSpec((None, tr, C), lambda i, pr: (pr[1], i, 0)))
    out = _pallas(body, name=name, grid_spec=spec, out_shape=jax.ShapeDtypeStruct((N_CHIPS, R, C), BF16),
                  compiler_params=_params(("parallel",)))(place, w, *deps)
    return out.reshape(N_CHIPS, 2, R // 2, C)


def _gather_ici_plan(bufs):
    x, y, c = _me()
    j = 2 * x + y
    plan = []
    for i, buf in enumerate(bufs):
        for k, (px, py) in enumerate(_other_chips(x, y)):
            plan.append((3 * i + k, buf.at[j, c], buf.at[j, c], (px, py, c)))
    return plan


def _gather_weights(bufs, *, name):
    n = len(bufs)

    def body(*refs):
        outs = refs[n:2 * n]
        ssem, rsem, fssem, frsem = refs[2 * n:]
        x, y, c = _me()
        sib = (x, y, 1 - c)
        chips = _other_chips(x, y)
        remote = []
        for s, src, dst, dev in _gather_ici_plan(outs):
            r = _rcopy(src, dst, ssem.at[s], rsem.at[s], dev)
            r.start()
            remote.append(r)
        for i in range(n):
            for k, (px, py) in enumerate(chips):
                slot = outs[i].at[2 * px + py, c]
                _rcopy(slot, slot, ssem.at[3 * i + k], rsem.at[3 * i + k], (px, py, c)).wait_recv()
                f = _rcopy(slot, slot, fssem.at[3 * i + k], frsem.at[3 * i + k], sib)
                f.start()
                remote.append(f)
        for i in range(n):
            for k, (px, py) in enumerate(chips):
                slot = outs[i].at[2 * px + py, 1 - c]
                _rcopy(slot, slot, fssem.at[3 * i + k], frsem.at[3 * i + k], sib).wait_recv()
        for r in remote:
            r.wait_send()

    return _pallas(
        body, name=name, in_specs=[_ANY] * n, out_specs=[_ANY] * n,
        out_shape=[jax.ShapeDtypeStruct(b.shape, b.dtype) for b in bufs],
        scratch_shapes=[pltpu.SemaphoreType.DMA((3 * n,))] * 4,
        input_output_aliases={i: i for i in range(n)},
        compiler_params=pltpu.CompilerParams(has_side_effects=True),
    )(*bufs)


def _forward_halves(bufs, *, name):
    n = len(bufs)

    def body(*refs):
        outs = refs[n:2 * n]
        ssem, rsem = refs[2 * n:]
        x, y, c = _me()
        sib = (x, y, 1 - c)
        cps = []
        for i in range(n):
            for k, (px, py) in enumerate(_other_chips(x, y)):
                slot = outs[i].at[2 * px + py, c]
                r = _rcopy(slot, slot, ssem.at[3 * i + k], rsem.at[3 * i + k], sib)
                r.start()
                cps.append(r)
        for r in cps:
            r.wait()

    return _pallas(
        body, name=name, in_specs=[_ANY] * n, out_specs=[_ANY] * n,
        out_shape=[jax.ShapeDtypeStruct(b.shape, b.dtype) for b in bufs],
        scratch_shapes=[pltpu.SemaphoreType.DMA((3 * n,))] * 2,
        input_output_aliases={i: i for i in range(n)},
        compiler_params=pltpu.CompilerParams(has_side_effects=True),
    )(*bufs)


_HBM = pl.BlockSpec(memory_space=pltpu.HBM)
_SEM = pl.BlockSpec(memory_space=pltpu.SEMAPHORE)
_EFFECT = pltpu.SideEffectType.DATAFLOW_SIDE_EFFECTING


def _split_start(bufs, plan, n_copies, *, name):
    n = len(bufs)

    def body(*refs):
        ssem, rsem = refs[n], refs[n + 1]
        for s, src, dst, dev in plan(refs[:n]):
            _rcopy(src, dst, ssem.at[s], rsem.at[s], dev).start()
        refs[-1][...] = jnp.zeros_like(refs[-1])

    res = _pallas(
        body, name=name, in_specs=[_HBM] * n,
        out_specs=(_SEM, _SEM, *[_HBM] * n, pl.BlockSpec(memory_space=pltpu.VMEM)),
        out_shape=(pltpu.SemaphoreType.DMA((n_copies,)), pltpu.SemaphoreType.DMA((n_copies,)),
                   *[pltpu.HBM(b.shape, b.dtype) for b in bufs], jax.ShapeDtypeStruct((8, LANES), F32)),
        input_output_aliases={i: 2 + i for i in range(n)},
        compiler_params=pltpu.CompilerParams(has_side_effects=_EFFECT),
    )(*[pltpu.with_memory_space_constraint(b, pltpu.HBM) for b in bufs])
    return res[0], res[1], list(res[2:2 + n]), res[-1]


def _split_wait(ssem, rsem, bufs, after, plan, *, name):
    n = len(bufs)

    def body(*refs):
        ssem_ref, rsem_ref = refs[n], refs[n + 1]
        for s, src, dst, dev in plan(refs[:n]):
            cp = _rcopy(src, dst, ssem_ref.at[s], rsem_ref.at[s], dev)
            cp.wait_send()
            cp.wait_recv()

    return list(_pallas(
        body, name=name, in_specs=[_HBM] * n + [_SEM, _SEM, _ANY], out_specs=[_HBM] * n,
        out_shape=[pltpu.HBM(b.shape, b.dtype) for b in bufs],
        input_output_aliases={i: i for i in range(n)},
        compiler_params=pltpu.CompilerParams(has_side_effects=_EFFECT),
    )(*bufs, ssem, rsem, after))


def _scatter_plan(n):
    def plan(bufs):
        x, y, c = _me()
        out = []
        for i in range(n):
            for k, (px, py) in enumerate(_other_chips(x, y)):
                out.append((3 * i + k, bufs[i].at[2 * px + py], bufs[n + i].at[k], (px, py, c)))
        return out
    return plan


def _swap_halves(grads, *, name):
    n = len(grads)
    views = [g.reshape(N_CHIPS, 2, g.shape[1] // 2, g.shape[2]) for g in grads]

    def body(*refs):
        ins, outs = refs[:n], refs[n:2 * n]
        ssem, rsem = refs[2 * n:]
        x, y, c = _me()
        sib = (x, y, 1 - c)
        cps = []
        for i in range(n):
            r = _rcopy(ins[i].at[:, 1 - c], outs[i], ssem.at[i], rsem.at[i], sib)
            r.start()
            cps.append(r)
        for r in cps:
            r.wait()

    return _pallas(
        body, name=name, in_specs=[_ANY] * n, out_specs=[_ANY] * n,
        out_shape=[jax.ShapeDtypeStruct((N_CHIPS,) + v.shape[2:], v.dtype) for v in views],
        scratch_shapes=[pltpu.SemaphoreType.DMA((n,)), pltpu.SemaphoreType.DMA((n,))],
        compiler_params=pltpu.CompilerParams(has_side_effects=True),
    )(*views)


def _join_halves(halves, *, name):
    n = len(halves)

    def body(*refs):
        outs = refs[n:2 * n]
        ssem, rsem = refs[2 * n:]
        x, y, c = _me()
        sib = (x, y, 1 - c)
        cps = []
        for i in range(n):
            r = _rcopy(outs[i].at[c], outs[i].at[c], ssem.at[i], rsem.at[i], sib)
            r.start()
            cps.append(r)
        for r in cps:
            r.wait()

    return _pallas(
        body, name=name, in_specs=[_ANY] * n, out_specs=[_ANY] * n,
        out_shape=[jax.ShapeDtypeStruct(h.shape, h.dtype) for h in halves],
        scratch_shapes=[pltpu.SemaphoreType.DMA((n,)), pltpu.SemaphoreType.DMA((n,))],
        input_output_aliases={i: i for i in range(n)},
        compiler_params=pltpu.CompilerParams(has_side_effects=True),
    )(*halves)


def _allreduce_small(vec):
    R = vec.shape[0]

    def body(v_ref, o_ref, buf, ssem, rsem):
        x, y, c = _me()
        me = 4 * x + 2 * y + c
        buf[me] = v_ref[...]
        cps = []
        for k in range(1, 8):
            peer = (x ^ (k >> 2), y ^ ((k >> 1) & 1), c ^ (k & 1))
            r = _rcopy(v_ref, buf.at[me], ssem.at[k - 1], rsem.at[k - 1], peer)
            r.start()
            cps.append(r)
        for k in range(1, 8):
            peer = (x ^ (k >> 2), y ^ ((k >> 1) & 1), c ^ (k & 1))
            pid = 4 * peer[0] + 2 * peer[1] + peer[2]
            _rcopy(v_ref, buf.at[pid], ssem.at[k - 1], rsem.at[k - 1], peer).wait_recv()
        for r in cps:
            r.wait_send()
        tot = buf[0]
        for d in range(1, 8):
            tot = tot + buf[d]
        o_ref[...] = tot

    vm = pl.BlockSpec(memory_space=pltpu.VMEM)
    return _pallas(
        body, name="allreduce_small", in_specs=[vm], out_specs=vm,
        out_shape=jax.ShapeDtypeStruct((R, LANES), F32),
        scratch_shapes=[pltpu.VMEM((8, R, LANES), F32), pltpu.SemaphoreType.DMA((7,)), pltpu.SemaphoreType.DMA((7,))],
        compiler_params=pltpu.CompilerParams(has_side_effects=True),
    )(vec)


def _rope_tables(positions, S):
    pos = positions.reshape(S, 1).astype(F32)
    half = RET_QK // 2
    inv = ROPE_THETA ** (-jnp.arange(half, dtype=F32) / half)
    ang = pos * inv
    cosr = jnp.concatenate([jnp.cos(ang), jnp.cos(ang)], axis=1)
    sinr = jnp.concatenate([-jnp.sin(ang), jnp.sin(ang)], axis=1)
    half = QK_ROPE // 2
    inv = ROPE_THETA ** (-jnp.arange(half, dtype=F32) / half)
    ang = pos * inv
    z = jnp.zeros((S, half), F32)
    c = jnp.concatenate([jnp.cos(ang), jnp.cos(ang), z, z], axis=1)
    s1 = jnp.concatenate([-jnp.sin(ang), z, z, z], axis=1)
    s2 = jnp.concatenate([z, jnp.sin(ang), z, z], axis=1)
    return cosr, sinr, (c, s1, s2)


def _cat_cols(g):
    return jnp.concatenate([g[j] for j in range(N_CHIPS)], axis=1)


def _split_cols(w):
    return jnp.stack(jnp.split(w, N_CHIPS, axis=1))


def _pack_small(vs, rows):
    flat = jnp.concatenate([v.reshape(-1) for v in vs])
    flat = jnp.pad(flat, (0, rows * LANES - flat.shape[0]))
    return flat.reshape(rows, LANES)


def kernel(x, positions, norm_mix_g, w_in, ret_norm_g, w_ret_o, q_a_norm_g, w_q_b, kv_a_norm_g, w_kv_b, w_mla_o, w_out, norm_mlp_g, w_up, w_down, norm_f_g, loss_target, m_norm_mix_g, m_w_in, m_ret_norm_g, m_w_ret_o, m_q_a_norm_g, m_w_q_b, m_kv_a_norm_g, m_w_kv_b, m_w_mla_o, m_w_out, m_norm_mlp_g, m_w_up, m_w_down, m_norm_f_g, v_norm_mix_g, v_w_in, v_ret_norm_g, v_w_ret_o, v_q_a_norm_g, v_w_q_b, v_kv_a_norm_g, v_w_kv_b, v_w_mla_o, v_w_out, v_norm_mlp_g, v_w_up, v_w_down, v_norm_f_g):
    S, D = x.shape[1], x.shape[2]
    RVW = w_ret_o.shape[1] * N_CHIPS
    RH = RVW // RET_V
    RQW = RH * RET_QK
    MVW = w_mla_o.shape[1] * N_CHIPS
    MH = MVW // V_HEAD
    QL, KVL = w_q_b.shape[1], w_kv_b.shape[1]
    T_RET = _tile(S, 256)
    T_ATT = _tile(S, 512)

    xs = x.reshape(S, D)
    tgt = loss_target.reshape(S, D)
    cosr, sinr, pe_tabs = _rope_tables(positions, S)
    lgam = jnp.log(1.0 - 2.0 ** (-5.0 - jnp.arange(RH, dtype=F32)))
    lgam = jnp.broadcast_to(lgam[:, None, None], (RH, 8, LANES))

    big = ("w_in", "w_ret_o", "w_q_b", "w_kv_b", "w_mla_o", "w_out", "w_up", "w_down")
    w_sh = dict(w_in=w_in[0], w_ret_o=w_ret_o[0], w_q_b=w_q_b[0], w_kv_b=w_kv_b[0], w_mla_o=w_mla_o[0],
                w_out=w_out[0], w_up=w_up[0], w_down=w_down[0])
    m_sh = dict(w_in=m_w_in[0], w_ret_o=m_w_ret_o[0], w_q_b=m_w_q_b[0], w_kv_b=m_w_kv_b[0], w_mla_o=m_w_mla_o[0],
                w_out=m_w_out[0], w_up=m_w_up[0], w_down=m_w_down[0])
    v_sh = dict(w_in=v_w_in[0], w_ret_o=v_w_ret_o[0], w_q_b=v_w_q_b[0], w_kv_b=v_w_kv_b[0], w_mla_o=v_w_mla_o[0],
                w_out=v_w_out[0], w_up=v_w_up[0], w_down=v_w_down[0])
    col_sharded = ("w_in", "w_q_b", "w_kv_b", "w_up")
    place = jnp.stack([lax.axis_index("c"), 2 * lax.axis_index("x") + lax.axis_index("y")]).astype(jnp.int32)

    def whole(k, g):
        g = g.reshape(N_CHIPS, w_sh[k].shape[0], w_sh[k].shape[1])
        if k == "w_up":
            return g
        return _cat_cols(g) if k in col_sharded else g.reshape(-1, g.shape[2])

    first = ("w_in", "w_q_b", "w_kv_b")
    later = ("w_ret_o", "w_mla_o", "w_out", "w_up", "w_down")
    got = _gather_weights([_cast_into_slot(w_sh[k], place, name="cast_" + k) for k in first], name="gather_first")
    full = {k: whole(k, g) for k, g in zip(first[1:], got[1:])}
    later_bufs = [_cast_into_slot(w_sh[k], place, name="cast_" + k, deps=(got[0],)) for k in later]
    later_ssem, later_rsem, later_bufs, later_token = _split_start(
        later_bufs, _gather_ici_plan, 3 * len(later), name="gather_later_start")

    o_rq, o_rk, o_rv, o_rg = 0, RQW, 2 * RQW, 2 * RQW + RVW
    o_cq = 2 * RQW + 2 * RVW
    o_ckv, o_kpe = o_cq + QL, o_cq + QL + KVL
    o_gr = o_kpe + QK_ROPE
    o_gm = o_gr + D
    n_ret = RH * RET_HEAD_COLS
    off_gret, off_gmla, off_cq, off_ckv = n_ret, n_ret + D, n_ret + 2 * D, n_ret + 2 * D + QL
    n_a = off_ckv + KVL
    runs = []
    for h in range(RH):
        base = h * RET_HEAD_COLS
        runs += [(o_rq + h * RET_QK, RET_QK, base), (o_rk + h * RET_QK, RET_QK, base + RET_QK),
                 (o_rv + h * RET_V, RET_V, base + 2 * RET_QK), (o_rg + h * RET_V, RET_V, base + 2 * RET_QK + RET_V)]
    runs += [(o_gr, D, off_gret), (o_gm, D, off_gmla), (o_cq, QL, off_cq), (o_ckv, KVL, off_ckv),
             (o_kpe, QK_ROPE, n_a)]
    c_sh = w_in.shape[2]

    def take(parts, start, width):
        out, lo = [], 0
        for p in parts:
            hi = lo + p.shape[1]
            a, b = max(start, lo), min(start + width, hi)
            if a < b:
                out.append(p[:, a - lo:b - lo])
            lo = hi
        return out

    wi = [got[0].reshape(N_CHIPS, D, c_sh)[jj] for jj in range(N_CHIPS)]
    here = sorted(runs, key=lambda r: r[2])
    wa = jnp.concatenate([p for s0, w, _ in here[:-1] for p in take(wi, s0, w)], axis=1)
    wkpe = jnp.concatenate(take(wi, o_kpe, QK_ROPE) + [jnp.zeros((D, LANES - QK_ROPE), BF16)], axis=1)
    wq = jnp.pad(full["w_q_b"].reshape(QL, MH, QK_NOPE + QK_ROPE),
                 ((0, 0), (0, 0), (0, LANES - QK_ROPE))).reshape(QL, MH * 2 * LANES)
    wkv = full["w_kv_b"]

    u, rstd0 = _rmsnorm_fwd(xs, norm_mix_g, name="norm_mix")
    proj = _mm(u, wa, mode="nn", outs=[F32], name="in_proj", deps=(later_token,))
    kpe = _mm(u, wkpe, mode="nn", outs=[F32], name="kpe_proj")
    ry, gated, states = _ret_fwd(proj, cosr, sinr, lgam, ret_norm_g, RH, T=T_RET)
    cqn, rstd_q = _rmsnorm_fwd(proj, q_a_norm_g, name="norm_q", width=QL, col=off_cq // QL)
    ckvn, rstd_kv = _rmsnorm_fwd(proj, kv_a_norm_g, name="norm_kv", width=KVL, col=off_ckv // KVL)
    qf, kf, vb = _qkv_proj(cqn, ckvn, wq, wkv, kpe, pe_tabs, MH)
    my, lse2 = _attn_fwd(qf, kf, vb, MH, T=T_ATT)
    later_bufs = _split_wait(later_ssem, later_rsem, later_bufs, my, _gather_ici_plan, name="gather_later_wait")
    later_bufs = _forward_halves(later_bufs, name="gather_later_forward")
    full.update({k: whole(k, g) for k, g in zip(later, later_bufs)})
    y_ret = _mm(gated, full["w_ret_o"], mode="nn", outs=[F32], name="ret_o")
    y_mla = _mm(my, full["w_mla_o"], mode="nn", outs=[F32], name="mla_o")
    merged = _merge_fwd(proj, y_ret, y_mla, D, off_gret, off_gmla)
    h1 = _mm(merged, full["w_out"], mode="nn", outs=[F32], name="out_proj",
             epi=lambda acc, r: (acc + r,), extras=(xs,))
    n1, rstd1 = _rmsnorm_fwd(h1, norm_mlp_g, name="norm_mlp")

    def up_epi(acc):
        r = jnp.maximum(acc, 0.0)
        return acc, r * r

    z, act = _mm(n1, full["w_up"], mode="nn", outs=[F32, BF16], name="up_proj", epi=up_epi)
    h2 = _mm(act, full["w_down"], mode="nn", outs=[F32], name="down_proj",
             epi=lambda acc, r: (acc + r,), extras=(h1,))
    loss11, dh2, g_norm_f = _final_loss(h2, norm_f_g.reshape(1, D), tgt)

    dz = _mm(dh2, full["w_down"], mode="nt", outs=[BF16], name="down_bwd_x",
             epi=lambda acc, zz: (acc * (2.0 * jnp.maximum(zz, 0.0)),), extras=(z,))
    g_w_down = _mm(act, dh2, mode="tn", outs=[BF16], name="down_bwd_w")
    dn1 = _mm(dz, full["w_up"], mode="nt", outs=[F32], name="up_bwd_x")
    g_w_up = _mm(n1, dz, mode="tn", outs=[BF16], name="up_bwd_w", out_shards=True)

    def reduce_begin(tag, names, grads):
        pcs = [g if g.ndim == 3 else g.reshape(N_CHIPS, g.shape[0] // N_CHIPS, g.shape[1]) for g in grads]
        theirs = _swap_halves(pcs, name="swap_" + tag)
        sums = [_sum_pair(p, t, place, name="sum_pair_" + k) for k, p, t in zip(names, pcs, theirs)]
        return pcs, theirs, sums

    def scatter_begin(tag, sums):
        lands = [lax.empty((3,) + s.shape[1:], s.dtype) for s in sums]
        return _split_start(sums + lands, _scatter_plan(len(sums)), 3 * len(sums), name="scatter_" + tag + "_start")

    g1 = ("w_up", "w_down")
    pcs1, theirs1, sums1 = reduce_begin("g1", g1, (g_w_up, g_w_down))
    ssem1, rsem1, bufs1, token1 = scatter_begin("g1", sums1)
    dh1, g_norm_mlp = _rmsnorm_bwd(dn1, h1, rstd1, norm_mlp_g, name="norm_mlp_bwd", res=dh2, deps=(token1,))
    dmerged = _mm(dh1, full["w_out"], mode="nt", outs=[F32], name="out_bwd_x")
    g_w_out = _mm(merged, dh1, mode="tn", outs=[BF16], name="out_bwd_w")
    dproj, dy_ret, dy_mla = _merge_bwd(dmerged, proj, y_ret, y_mla, D, off_gret)
    dgated = _mm(dy_ret, full["w_ret_o"], mode="nt", outs=[F32], name="ret_o_bwd_x")
    g_w_ret_o = _mm(gated, dy_ret, mode="tn", outs=[BF16], name="ret_o_bwd_w")
    dproj, g_ret_norm = _ret_bwd(proj, cosr, sinr, lgam, ret_norm_g, ry, dgated, states, dproj, RH, T=T_RET)
    dmy = _mm(dy_mla, full["w_mla_o"], mode="nt", outs=[F32], name="mla_o_bwd_x")
    g_w_mla_o = _mm(my, dy_mla, mode="tn", outs=[BF16], name="mla_o_bwd_w")
    g2 = ("w_out", "w_ret_o", "w_mla_o")
    pcs2, theirs2, sums2 = reduce_begin("g2", g2, (g_w_out, g_w_ret_o, g_w_mla_o))
    ssem2, rsem2, bufs2, token2 = scatter_begin("g2", sums2)
    delta, dob = _attn_delta(dmy, my, MH, deps=(token2,))
    dqf, dkv_all, dkpe_h = _attn_bwd(qf, kf, vb, dob, lse2, delta, MH, T=T_ATT)
    dq_all, dkpe = _attn_post(dqf, dkpe_h, pe_tabs, MH)
    dcqn = _mm(dq_all, wq, mode="nt", outs=[F32], name="q_bwd_x")
    g_wq = _mm(cqn, dq_all, mode="tn", outs=[BF16], name="q_bwd_w")
    dckvn = _mm(dkv_all, wkv, mode="nt", outs=[F32], name="kv_bwd_x")
    g_wkv = _mm(ckvn, dkv_all, mode="tn", outs=[BF16], name="kv_bwd_w")
    dproj, g_q_a = _rmsnorm_bwd(dcqn, proj, rstd_q, q_a_norm_g, name="norm_q_bwd", into=(dproj, off_cq // QL),
                                width=QL, col=off_cq // QL)
    dproj, g_kv_a = _rmsnorm_bwd(dckvn, proj, rstd_kv, kv_a_norm_g, name="norm_kv_bwd", into=(dproj, off_ckv // KVL),
                                 width=KVL, col=off_ckv // KVL)
    g_wa = _mm(u, dproj, mode="tn", outs=[BF16], name="in_bwd_w")
    g_wkpe = _mm(u, dkpe, mode="tn", outs=[BF16], name="kpe_bwd_w")

    there = sorted(runs)
    g_parts = [g_wa, g_wkpe]
    g_w_in = jnp.stack([jnp.concatenate(
        [p for s0, w, d0 in there for a, b in [(max(s0, jj * c_sh), min(s0 + w, (jj + 1) * c_sh))] if a < b
         for p in take(g_parts, d0 + a - s0, b - a)], axis=1) for jj in range(N_CHIPS)])
    gq = g_wq.reshape(QL, MH, 2 * LANES)[:, :, :QK_NOPE + QK_ROPE].reshape(QL, MH * (QK_NOPE + QK_ROPE))
    g3 = ("w_in", "w_q_b", "w_kv_b")
    pcs3, theirs3, sums3 = reduce_begin("g3", g3, (g_w_in, _split_cols(gq), _split_cols(g_wkv)))
    ssem3, rsem3, bufs3, token3 = scatter_begin("g3", sums3)
    du_a = _mm(dproj, wa, mode="nt", outs=[F32], name="in_bwd_x", tk=1024, deps=(token3,))
    du = _mm(dkpe, wkpe, mode="nt", outs=[F32], name="kpe_bwd_x", epi=lambda acc, r: (acc + r,), extras=(du_a,))
    dx, g_norm_mix = _rmsnorm_bwd(du, xs, rstd0, norm_mix_g, name="norm_mix_bwd", res=dh1)

    bufs1 = _split_wait(ssem1, rsem1, bufs1, dx, _scatter_plan(len(g1)), name="scatter_g1_wait")
    bufs2 = _split_wait(ssem2, rsem2, bufs2, dx, _scatter_plan(len(g2)), name="scatter_g2_wait")
    bufs3 = _split_wait(ssem3, rsem3, bufs3, dx, _scatter_plan(len(g3)), name="scatter_g3_wait")
    recv1, recv2, recv3 = bufs1[len(g1):], bufs2[len(g2):], bufs3[len(g3):]
    halves = {}
    for names, pcs, theirs, recv in ((g1, pcs1, theirs1, recv1), (g2, pcs2, theirs2, recv2), (g3, pcs3, theirs3, recv3)):
        for k, p, t, r in zip(names, pcs, theirs, recv):
            halves[k] = _sum_chips(p, t, r, place, name="sum_chips_" + k)
    joined = _join_halves([halves[k] for k in big], name="join_halves")
    g_shard = {k: g.reshape(2 * g.shape[1], g.shape[2]) for k, g in zip(big, joined)}

    small = ("norm_mix_g", "ret_norm_g", "q_a_norm_g", "kv_a_norm_g", "norm_mlp_g", "norm_f_g")
    g_small = [g_norm_mix, g_ret_norm, g_q_a, g_kv_a, g_norm_mlp, g_norm_f]
    sizes = [int(v.size) for v in g_small]
    n_small = sum(sizes) + LANES
    rows = -(-n_small // (8 * LANES)) * 8
    packed = _pack_small(g_small + [jnp.broadcast_to(loss11.reshape(1), (LANES,))], rows)
    red = _allreduce_small(packed).reshape(-1)
    loss = red[sum(sizes)]
    w_small = [norm_mix_g, ret_norm_g, q_a_norm_g, kv_a_norm_g, norm_mlp_g, norm_f_g]
    m_small = [m_norm_mix_g, m_ret_norm_g, m_q_a_norm_g, m_kv_a_norm_g, m_norm_mlp_g, m_norm_f_g]
    v_small = [v_norm_mix_g, v_ret_norm_g, v_q_a_norm_g, v_kv_a_norm_g, v_norm_mlp_g, v_norm_f_g]
    g_pk = red[:rows * LANES].reshape(rows, LANES)
    d_pk, m_pk, v_pk = _rows_call(_adamw_vals, [_pack_small(w_small, rows), g_pk, _pack_small(m_small, rows),
                                               _pack_small(v_small, rows)], [F32, F32, F32], name="adamw_small")
    out_g, out_d, out_m, out_v = {}, {}, {}, {}
    off = 0
    for k, wv, sz in zip(small, w_small, sizes):
        for dst, src in ((out_g, g_pk), (out_d, d_pk), (out_m, m_pk), (out_v, v_pk)):
            dst[k] = src.reshape(-1)[off:off + sz].reshape(wv.shape)
        off += sz

    for k in big:
        g_, d_, m_, v_ = _rows_call(lambda w, g, m, v: (g,) + _adamw_vals(w, g, m, v),
                                    [w_sh[k], g_shard[k], m_sh[k], v_sh[k]], [F32] * 4, name="adamw_" + k)
        out_g[k] = g_[None]
        out_d[k], out_m[k], out_v[k] = d_[None], m_[None], v_[None]

    order = ("norm_mix_g", "w_in", "ret_norm_g", "w_ret_o", "q_a_norm_g", "w_q_b", "kv_a_norm_g", "w_kv_b",
             "w_mla_o", "w_out", "norm_mlp_g", "w_up", "w_down", "norm_f_g")
    return (loss, dx.reshape(1, S, D), *[out_g[k] for k in order], *[out_d[k] for k in order],
            *[out_m[k] for k in order], *[out_v[k] for k in order])
```

```python
import math

import jax
import jax.numpy as jnp
from jax import lax
from jax.experimental import pallas as pl
from jax.experimental.pallas import tpu as pltpu

F32 = jnp.float32
BF16 = jnp.bfloat16

EPS = 1e-6
ROPE_THETA = 10000.0
CHUNK = 64
RET_QK = 128
RET_V = 256
RET_HEAD_COLS = 2 * RET_QK + 2 * RET_V
QK_NOPE = 128
QK_ROPE = 64
V_HEAD = 128
LANES = 128
LOG2E = math.log2(math.e)

ADAM_LR = 0.001
ADAM_B1 = 0.9
ADAM_B2 = 0.999
ADAM_EPS = 1e-08
ADAM_WD = 0.01
ADAM_STEP = 10

N_CHIPS = 4
VMEM_LIMIT = 56 * 1024 * 1024
MESH = pl.DeviceIdType.MESH
NEG = -1e30


def _pallas(body, **kw):
    return pl.pallas_call(body, **kw)


def _params(sem=None):
    return pltpu.CompilerParams(dimension_semantics=sem, vmem_limit_bytes=VMEM_LIMIT)


def _tile(n, want):
    t = min(n, want)
    while n % t:
        t //= 2
    return t


_ANY = pl.BlockSpec(memory_space=pl.ANY)


def _mm(a, b, *, mode, outs, name, epi=None, extras=(), deps=(), out_shards=False, tm=1024, tn=1024, tk=2048):
    shards = b.shape[0] if b.ndim == 3 else 1
    brows, bcols = b.shape[-2], b.shape[-1] * shards
    if mode == "nn":
        (M, K), N = a.shape, bcols
    elif mode == "nt":
        (M, K), N = a.shape, brows
    else:
        (K, M), N = a.shape, bcols
    tm = _tile(M, tm)
    tn = _tile(N // (shards if mode == "nn" else 1) // (N_CHIPS if out_shards else 1), tn)
    tk = _tile(K // (shards if mode == "nt" else 1), tk)
    nk = K // tk
    if mode == "nn":
        a_spec = pl.BlockSpec((tm, tk), lambda i, j, k: (i, k))
        dims = (((1,), (0,)), ((), ()))
        if shards > 1:
            per = N // shards // tn
            b_spec = pl.BlockSpec((None, tk, tn), lambda i, j, k: (j // per, k, j % per))
        else:
            b_spec = pl.BlockSpec((tk, tn), lambda i, j, k: (k, j))
    elif mode == "nt":
        a_spec = pl.BlockSpec((tm, tk), lambda i, j, k: (i, k))
        dims = (((1,), (1,)), ((), ()))
        if shards > 1:
            per = K // shards // tk
            b_spec = pl.BlockSpec((None, tn, tk), lambda i, j, k: (k // per, j, k % per))
        else:
            b_spec = pl.BlockSpec((tn, tk), lambda i, j, k: (j, k))
    else:
        assert shards == 1
        a_spec = pl.BlockSpec((tk, tm), lambda i, j, k: (k, i))
        b_spec = pl.BlockSpec((tk, tn), lambda i, j, k: (k, j))
        dims = (((0,), (0,)), ((), ()))
    if out_shards:
        assert not extras
        oper = N // N_CHIPS // tn
        o_spec = pl.BlockSpec((None, tm, tn), lambda i, j, k: (j // oper, i, j % oper))
        o_shape = (N_CHIPS, M, N // N_CHIPS)
    else:
        o_spec = pl.BlockSpec((tm, tn), lambda i, j, k: (i, j))
        o_shape = (M, N)
    n_ex, n_out, n_dep = len(extras), len(outs), len(deps)
    if epi is None:
        epi = lambda acc: (acc,)

    def body(*refs):
        a_ref, b_ref = refs[0], refs[1]
        ex_refs = refs[2:2 + n_ex]
        o_refs = refs[2 + n_ex + n_dep:2 + n_ex + n_dep + n_out]
        part = lax.dot_general(a_ref[...].astype(BF16), b_ref[...].astype(BF16), dims,
                               preferred_element_type=F32)

        def finish(acc):
            vals = epi(acc, *[r[...] for r in ex_refs])
            for r, v in zip(o_refs, vals):
                r[...] = v.astype(r.dtype)

        if nk == 1:
            finish(part)
        else:
            acc_ref = refs[-1]
            k = pl.program_id(2)

            @pl.when(k == 0)
            def _():
                acc_ref[...] = part

            @pl.when(k > 0)
            def _():
                acc_ref[...] += part

            @pl.when(k == nk - 1)
            def _():
                finish(acc_ref[...])

    res = _pallas(
        body, name=name, grid=(M // tm, N // tn, nk),
        in_specs=[a_spec, b_spec] + [o_spec] * n_ex + [_ANY] * n_dep,
        out_specs=[o_spec] * n_out,
        out_shape=[jax.ShapeDtypeStruct(o_shape, d) for d in outs],
        scratch_shapes=[pltpu.VMEM((tm, tn), F32)] if nk > 1 else [],
        compiler_params=_params(("parallel", "parallel", "arbitrary")),
    )(a, b, *extras, *deps)
    return res[0] if n_out == 1 else res


def _rmsnorm_fwd(x, g, *, name, width=None, col=0, tr=256):
    S = x.shape[0]
    W = x.shape[1] if width is None else width
    tr = _tile(S, tr)

    def body(x_ref, g_ref, y_ref, r_ref):
        xv = x_ref[...]
        rstd = lax.rsqrt(jnp.mean(xv * xv, axis=-1, keepdims=True) + EPS)
        y_ref[...] = (xv * rstd * g_ref[...]).astype(BF16)
        r_ref[...] = rstd

    return _pallas(
        body, name=name, grid=(S // tr,),
        in_specs=[pl.BlockSpec((tr, W), lambda i: (i, col)), pl.BlockSpec((1, W), lambda i: (0, 0))],
        out_specs=[pl.BlockSpec((tr, W), lambda i: (i, 0)), pl.BlockSpec((tr, 1), lambda i: (i, 0))],
        out_shape=[jax.ShapeDtypeStruct((S, W), BF16), jax.ShapeDtypeStruct((S, 1), F32)],
        compiler_params=_params(("parallel",)),
    )(x, g)


def _rmsnorm_bwd(dy, x, rstd, g, *, name, res=None, into=None, deps=(), width=None, col=0, tr=256):
    S = x.shape[0]
    W = x.shape[1] if width is None else width
    tr = _tile(S, tr)
    has_res = res is not None

    def body(*refs):
        dy_ref, x_ref, r_ref, g_ref = refs[:4]
        dx_ref, dg_ref = refs[-2], refs[-1]
        rstd_v = r_ref[...]
        xhat = x_ref[...] * rstd_v
        dyv = dy_ref[...].astype(F32)
        dyg = dyv * g_ref[...]
        dx = rstd_v * (dyg - xhat * jnp.mean(dyg * xhat, axis=-1, keepdims=True))
        if has_res:
            dx = dx + refs[4][...]
        dx_ref[...] = dx.astype(dx_ref.dtype)
        part = jnp.sum(dyv * xhat, axis=0, keepdims=True)

        @pl.when(pl.program_id(0) == 0)
        def _():
            dg_ref[...] = part

        @pl.when(pl.program_id(0) > 0)
        def _():
            dg_ref[...] += part

    row = pl.BlockSpec((tr, W), lambda i: (i, 0))
    ins = [dy, x, rstd, g] + ([res] if has_res else [])
    in_specs = [row, pl.BlockSpec((tr, W), lambda i: (i, col)), pl.BlockSpec((tr, 1), lambda i: (i, 0)),
                pl.BlockSpec((1, W), lambda i: (0, 0))] + ([row] if has_res else [])
    if into is None:
        dx_spec, dx_shape, alias = row, jax.ShapeDtypeStruct((S, W), F32), {}
    else:
        buf, col_out = into
        ins.append(buf)
        in_specs.append(_ANY)
        dx_spec = pl.BlockSpec((tr, W), lambda i: (i, col_out))
        dx_shape = jax.ShapeDtypeStruct(buf.shape, buf.dtype)
        alias = {len(ins) - 1: 0}
    ins += list(deps)
    in_specs += [_ANY] * len(deps)
    return _pallas(
        body, name=name, grid=(S // tr,), in_specs=in_specs,
        out_specs=[dx_spec, pl.BlockSpec((1, W), lambda i: (0, 0))],
        out_shape=[dx_shape, jax.ShapeDtypeStruct((1, W), F32)],
        input_output_aliases=alias,
        compiler_params=_params(("arbitrary",)),
    )(*ins)


def _final_loss(h2, g, target, *, tr=256):
    S, D = h2.shape
    tr = _tile(S, tr)

    def body(h_ref, g_ref, t_ref, loss_ref, dh_ref, dg_ref):
        hv = h_ref[...]
        rstd = lax.rsqrt(jnp.mean(hv * hv, axis=-1, keepdims=True) + EPS)
        xhat = hv * rstd
        e = xhat * g_ref[...] - t_ref[...]
        lpart = (0.5 / D) * jnp.sum(jnp.sum(e * e, axis=-1, keepdims=True), axis=0, keepdims=True)
        dy = e * (1.0 / D)
        dyg = dy * g_ref[...]
        dh_ref[...] = rstd * (dyg - xhat * jnp.mean(dyg * xhat, axis=-1, keepdims=True))
        gpart = jnp.sum(dy * xhat, axis=0, keepdims=True)

        @pl.when(pl.program_id(0) == 0)
        def _():
            loss_ref[...] = lpart
            dg_ref[...] = gpart

        @pl.when(pl.program_id(0) > 0)
        def _():
            loss_ref[...] += lpart
            dg_ref[...] += gpart

    row = pl.BlockSpec((tr, D), lambda i: (i, 0))
    vec = pl.BlockSpec((1, D), lambda i: (0, 0))
    return _pallas(
        body, name="final_loss", grid=(S // tr,), in_specs=[row, vec, row],
        out_specs=[pl.BlockSpec((1, 1), lambda i: (0, 0)), row, vec],
        out_shape=[jax.ShapeDtypeStruct((1, 1), F32), jax.ShapeDtypeStruct((S, D), F32),
                   jax.ShapeDtypeStruct((1, D), F32)],
        compiler_params=_params(("arbitrary",)),
    )(h2, g, target)


def _sigmoid(v):
    return 1.0 / (1.0 + jnp.exp(-v))


def _merge_fwd(proj, y_ret, y_mla, D, off_gret, off_gmla, *, tr=256, tc=1024):
    S = y_ret.shape[0]
    tr, tc = _tile(S, tr), _tile(D, tc)
    b_ret, b_mla = off_gret // tc, off_gmla // tc

    def body(gr_ref, gm_ref, yr_ref, ym_ref, o_ref):
        o_ref[...] = (_sigmoid(gr_ref[...]) * yr_ref[...] + _sigmoid(gm_ref[...]) * ym_ref[...]).astype(BF16)

    blk = pl.BlockSpec((tr, tc), lambda i, j: (i, j))
    return _pallas(
        body, name="merge_fwd", grid=(S // tr, D // tc),
        in_specs=[pl.BlockSpec((tr, tc), lambda i, j: (i, b_ret + j)),
                  pl.BlockSpec((tr, tc), lambda i, j: (i, b_mla + j)), blk, blk],
        out_specs=blk, out_shape=jax.ShapeDtypeStruct((S, D), BF16),
        compiler_params=_params(("parallel", "parallel")),
    )(proj, proj, y_ret, y_mla)


def _merge_bwd(dmerged, proj, y_ret, y_mla, D, off_gret, *, tr=256):
    S = y_ret.shape[0]
    tr = _tile(S, tr)
    b0 = off_gret // D

    def body(dm_ref, g_ref, yr_ref, ym_ref, dp_ref, dyr_ref, dym_ref):
        dm = dm_ref[...]
        sg = _sigmoid(g_ref[...])

        @pl.when(pl.program_id(1) == 0)
        def _():
            dyr_ref[...] = (dm * sg).astype(BF16)
            dp_ref[...] = (dm * yr_ref[...] * sg * (1.0 - sg)).astype(BF16)

        @pl.when(pl.program_id(1) == 1)
        def _():
            dym_ref[...] = (dm * sg).astype(BF16)
            dp_ref[...] = (dm * ym_ref[...] * sg * (1.0 - sg)).astype(BF16)

    blk = pl.BlockSpec((tr, D), lambda i, j: (i, 0))
    return _pallas(
        body, name="merge_bwd", grid=(S // tr, 2),
        in_specs=[blk, pl.BlockSpec((tr, D), lambda i, j: (i, b0 + j)), blk, blk],
        out_specs=[pl.BlockSpec((tr, D), lambda i, j: (i, b0 + j)), blk, blk],
        out_shape=[jax.ShapeDtypeStruct(proj.shape, BF16), jax.ShapeDtypeStruct((S, D), BF16),
                   jax.ShapeDtypeStruct((S, D), BF16)],
        compiler_params=_params(("parallel", "arbitrary")),
    )(dmerged, proj, y_ret, y_mla)


def _rope128(t, cos_full, sin_signed):
    return t * cos_full + pltpu.roll(t, RET_QK // 2, 1) * sin_signed


def _rope128_t(d, cos_full, sin_signed):
    return d * cos_full + pltpu.roll(d * sin_signed, RET_QK // 2, 1)


def _ret_consts(lg, T):
    pos = lax.broadcasted_iota(jnp.int32, (T, 1), 0).astype(F32)
    qd = jnp.exp(lg * (pos + 1.0))
    kd = jnp.exp(lg * (T - 1.0 - pos))
    n = lax.broadcasted_iota(jnp.int32, (T, T), 0)
    m = lax.broadcasted_iota(jnp.int32, (T, T), 1)
    vis = (m // CHUNK) <= (n // CHUNK)
    dist = jnp.abs(n - m).astype(F32)
    decay = jnp.where(vis, jnp.exp(lg * dist), 0.0)
    cdec = jnp.exp(lg * float(T))
    return qd, kd, decay, cdec


def _dot(a, b, dims):
    return lax.dot_general(a.astype(BF16), b.astype(BF16), (dims, ((), ())), preferred_element_type=F32)


NN = ((1,), (0,))
NT = ((1,), (1,))
TN = ((0,), (0,))
_RQ = slice(0, RET_QK)
_RK = slice(RET_QK, 2 * RET_QK)
_RV = slice(2 * RET_QK, 2 * RET_QK + RET_V)
_RG = slice(2 * RET_QK + RET_V, RET_HEAD_COLS)


def _ret_fwd(proj, cosr, sinr, lgam, gain, RH, *, T):
    S = proj.shape[0]
    nb = S // T
    scale = RET_QK ** -0.5

    def body(p_ref, cos_ref, sin_ref, lg_ref, gain_ref, ry_ref, gated_ref, st_ref, state):
        b = pl.program_id(1)

        @pl.when(b == 0)
        def _():
            state[...] = jnp.zeros_like(state)

        lg = lg_ref[0:1, 0:1]
        qd, kd, decay, cdec = _ret_consts(lg, T)
        cosv, sinv = cos_ref[...], sin_ref[...]
        q = _rope128(p_ref[:, _RQ], cosv, sinv)
        k = _rope128(p_ref[:, _RK], cosv, sinv) * scale
        v = p_ref[:, _RV]
        sprev = state[...]
        st_ref[...] = sprev
        a = _dot(q, k, NT) * decay
        o = _dot(a, v, NN) + _dot(q * qd, sprev, NN)
        state[...] = sprev * cdec + _dot(k * kd, v, TN)
        ry_ref[...] = o
        mu = jnp.mean(o, axis=-1, keepdims=True)
        oc = o - mu
        var = jnp.mean(oc * oc, axis=-1, keepdims=True)
        t = oc * lax.rsqrt(var + EPS) * gain_ref[...]
        gv = p_ref[:, _RG]
        gated_ref[...] = (t * (gv * _sigmoid(gv))).astype(BF16)

    return _pallas(
        body, name="ret_fwd", grid=(RH, nb),
        in_specs=[pl.BlockSpec((T, RET_HEAD_COLS), lambda h, b: (b, h)),
                  pl.BlockSpec((T, RET_QK), lambda h, b: (b, 0)),
                  pl.BlockSpec((T, RET_QK), lambda h, b: (b, 0)),
                  pl.BlockSpec((None, 8, LANES), lambda h, b: (h, 0, 0)),
                  pl.BlockSpec((1, RET_V), lambda h, b: (0, h))],
        out_specs=[pl.BlockSpec((T, RET_V), lambda h, b: (b, h)),
                   pl.BlockSpec((T, RET_V), lambda h, b: (b, h)),
                   pl.BlockSpec((None, None, RET_QK, RET_V), lambda h, b: (h, b, 0, 0))],
        out_shape=[jax.ShapeDtypeStruct((S, RH * RET_V), F32), jax.ShapeDtypeStruct((S, RH * RET_V), BF16),
                   jax.ShapeDtypeStruct((RH, nb, RET_QK, RET_V), F32)],
        scratch_shapes=[pltpu.VMEM((RET_QK, RET_V), F32)],
        compiler_params=_params(("parallel", "arbitrary")),
    )(proj, cosr, sinr, lgam, gain)


def _ret_bwd(proj, cosr, sinr, lgam, gain, ry, dgated, states, dproj, RH, *, T):
    S = proj.shape[0]
    nb = S // T
    scale = RET_QK ** -0.5

    def body(p_ref, cos_ref, sin_ref, lg_ref, gain_ref, ry_ref, dg_ref, st_ref, _, dp_ref, dgain_ref, dstate):
        b = pl.program_id(1)

        @pl.when(b == 0)
        def _():
            dstate[...] = jnp.zeros_like(dstate)

        lg = lg_ref[0:1, 0:1]
        qd, kd, decay, cdec = _ret_consts(lg, T)
        cosv, sinv = cos_ref[...], sin_ref[...]
        q = _rope128(p_ref[:, _RQ], cosv, sinv)
        k = _rope128(p_ref[:, _RK], cosv, sinv) * scale
        v = p_ref[:, _RV]
        sprev = st_ref[...]
        ds_new = dstate[...]
        o = ry_ref[...]
        mu = jnp.mean(o, axis=-1, keepdims=True)
        oc = o - mu
        rstd = lax.rsqrt(jnp.mean(oc * oc, axis=-1, keepdims=True) + EPS)
        ryn = oc * rstd
        gainv = gain_ref[...]
        gv = p_ref[:, _RG]
        sg = _sigmoid(gv)
        dgt = dg_ref[...]
        dt = dgt * (gv * sg)
        dp_ref[:, _RG] = (dgt * (ryn * gainv) * (sg * (1.0 + gv * (1.0 - sg)))).astype(BF16)
        gpart = jnp.sum(dt * ryn, axis=0, keepdims=True)

        @pl.when(b == 0)
        def _():
            dgain_ref[...] = gpart

        @pl.when(b > 0)
        def _():
            dgain_ref[...] += gpart

        dryn = dt * gainv
        do = rstd * (dryn - jnp.mean(dryn, axis=-1, keepdims=True)
                     - ryn * jnp.mean(dryn * ryn, axis=-1, keepdims=True))
        a = _dot(q, k, NT) * decay
        kdk = k * kd
        qdq = q * qd
        dp_ref[:, _RV] = (_dot(a, do, TN) + _dot(kdk, ds_new, NN)).astype(BF16)
        dp = _dot(do, v, NT) * decay
        dq = _dot(dp, k, NN) + _dot(do, sprev, NT) * qd
        dk = (_dot(dp, q, TN) + _dot(v, ds_new, NT) * kd) * scale
        dstate[...] = ds_new * cdec + _dot(qdq, do, TN)
        dp_ref[:, _RQ] = _rope128_t(dq, cosv, sinv).astype(BF16)
        dp_ref[:, _RK] = _rope128_t(dk, cosv, sinv).astype(BF16)

    rb = lambda b: nb - 1 - b
    return _pallas(
        body, name="ret_bwd", grid=(RH, nb),
        in_specs=[pl.BlockSpec((T, RET_HEAD_COLS), lambda h, b: (rb(b), h)),
                  pl.BlockSpec((T, RET_QK), lambda h, b: (rb(b), 0)),
                  pl.BlockSpec((T, RET_QK), lambda h, b: (rb(b), 0)),
                  pl.BlockSpec((None, 8, LANES), lambda h, b: (h, 0, 0)),
                  pl.BlockSpec((1, RET_V), lambda h, b: (0, h)),
                  pl.BlockSpec((T, RET_V), lambda h, b: (rb(b), h)),
                  pl.BlockSpec((T, RET_V), lambda h, b: (rb(b), h)),
                  pl.BlockSpec((None, None, RET_QK, RET_V), lambda h, b: (h, rb(b), 0, 0)),
                  _ANY],
        out_specs=[pl.BlockSpec((T, RET_HEAD_COLS), lambda h, b: (rb(b), h)),
                   pl.BlockSpec((1, RET_V), lambda h, b: (0, h))],
        out_shape=[jax.ShapeDtypeStruct(dproj.shape, dproj.dtype), jax.ShapeDtypeStruct((1, RH * RET_V), F32)],
        scratch_shapes=[pltpu.VMEM((RET_QK, RET_V), F32)],
        input_output_aliases={8: 0},
        compiler_params=_params(("parallel", "arbitrary")),
    )(proj, cosr, sinr, lgam, gain, ry, dgated, states, dproj)


def _rope_pe(t, c, s1, s2):
    return t * c + pltpu.roll(t, LANES - QK_ROPE // 2, 1) * s1 + pltpu.roll(t, QK_ROPE // 2, 1) * s2


def _rope_pe_t(d, c, s1, s2):
    return d * c + pltpu.roll(d * s1, QK_ROPE // 2, 1) + pltpu.roll(d * s2, LANES - QK_ROPE // 2, 1)


ATTN_C2 = (QK_NOPE + QK_ROPE) ** -0.5 * LOG2E


def _qkv_proj(cqn, ckvn, wq, wkv, kpe, tabs, MH, *, tm=512, heads=4):
    S = cqn.shape[0]
    tm = _tile(S, tm)
    hb = _tile(MH, heads)
    W = 2 * LANES
    c_t, s1_t, s2_t = tabs

    def body(cq_ref, ckv_ref, wq_ref, wkv_ref, kpe_ref, c_ref, s1_ref, s2_ref, qf_ref, kf_ref, v_ref):
        c, s1, s2 = c_ref[...], s1_ref[...], s2_ref[...]
        q = _dot(cq_ref[...], wq_ref[...], NN)
        kv = _dot(ckv_ref[...], wkv_ref[...], NN)
        kper = _rope_pe(kpe_ref[...], c, s1, s2).astype(BF16)
        for h in range(hb):
            lo, mid, hi = h * W, h * W + QK_NOPE, (h + 1) * W
            qf_ref[:, lo:mid] = (q[:, lo:mid] * ATTN_C2).astype(BF16)
            qf_ref[:, mid:hi] = (_rope_pe(q[:, mid:hi], c, s1, s2) * ATTN_C2).astype(BF16)
            kf_ref[:, lo:mid] = kv[:, lo:mid].astype(BF16)
            kf_ref[:, mid:hi] = kper
            v_ref[:, h * V_HEAD:(h + 1) * V_HEAD] = kv[:, mid:hi].astype(BF16)

    tab = pl.BlockSpec((tm, LANES), lambda i, j: (i, 0))
    grp = pl.BlockSpec((tm, hb * W), lambda i, j: (i, j))
    return _pallas(
        body, name="qkv_proj", grid=(S // tm, MH // hb),
        in_specs=[pl.BlockSpec((tm, cqn.shape[1]), lambda i, j: (i, 0)),
                  pl.BlockSpec((tm, ckvn.shape[1]), lambda i, j: (i, 0)),
                  pl.BlockSpec((wq.shape[0], hb * W), lambda i, j: (0, j)),
                  pl.BlockSpec((wkv.shape[0], hb * W), lambda i, j: (0, j)), tab, tab, tab, tab],
        out_specs=[grp, grp, pl.BlockSpec((tm, hb * V_HEAD), lambda i, j: (i, j))],
        out_shape=[jax.ShapeDtypeStruct((S, MH * W), BF16)] * 2 + [jax.ShapeDtypeStruct((S, MH * V_HEAD), BF16)],
        compiler_params=_params(("parallel", "parallel")),
    )(cqn, ckvn, wq, wkv, kpe, c_t, s1_t, s2_t)


def _chunk_mask(T):
    n = lax.broadcasted_iota(jnp.int32, (T, T), 0)
    m = lax.broadcasted_iota(jnp.int32, (T, T), 1)
    return (m // CHUNK) <= (n // CHUNK)


def _lanes_to(v, width):
    return jnp.tile(v, (1, width // LANES))


def _attn_fwd(qf, kf, vb, MH, *, T):
    S = qf.shape[0]
    nt = S // T

    def body(q_ref, k_ref, v_ref, o_ref, lse_ref, m_sc, l_sc, acc_sc, s_a, s_b):
        qi = pl.program_id(1)
        m_sc[...] = jnp.full_like(m_sc, NEG)
        l_sc[...] = jnp.zeros_like(l_sc)
        acc_sc[...] = jnp.zeros_like(acc_sc)

        def rows_of(kt):
            return pl.ds(pl.multiple_of(kt * T, T), T)

        def scores(kt):
            return _dot(q_ref[...], k_ref[rows_of(kt), :], NT)

        def update(s, kt):
            m_prev = m_sc[...]
            m_new = jnp.maximum(m_prev, jnp.max(s, axis=-1, keepdims=True))
            alpha = jnp.exp2(m_prev - m_new)
            p = jnp.exp2(s - _lanes_to(m_new, T))
            l_sc[...] = alpha * l_sc[...] + jnp.sum(p, axis=-1, keepdims=True)
            acc_sc[...] = alpha * acc_sc[...] + _dot(p, v_ref[rows_of(kt), :], NN)
            m_sc[...] = m_new

        update(jnp.where(_chunk_mask(T), scores(qi), NEG), qi)

        @pl.when(qi % 2 == 1)
        def _():
            update(scores(qi - 1), qi - 1)

        pairs = qi // 2

        @pl.when(pairs > 0)
        def _():
            s_a[...] = scores(0)

        def pair(j, carry):
            t0 = 2 * j
            s_b[...] = scores(t0 + 1)
            update(s_a[...], t0)
            s_a[...] = scores(jnp.minimum(t0 + 2, 2 * pairs - 1))
            update(s_b[...], t0 + 1)
            return carry

        lax.fori_loop(0, pairs, pair, 0)
        l = l_sc[...]
        o_ref[...] = acc_sc[...] / l
        lse_ref[...] = m_sc[...] + jnp.log(l) * LOG2E

    return _pallas(
        body, name="attn_fwd", grid=(MH, nt),
        in_specs=[pl.BlockSpec((T, 2 * LANES), lambda h, i: (i, h)),
                  pl.BlockSpec((S, 2 * LANES), lambda h, i: (0, h)),
                  pl.BlockSpec((S, LANES), lambda h, i: (0, h))],
        out_specs=[pl.BlockSpec((T, LANES), lambda h, i: (i, h)),
                   pl.BlockSpec((None, T, LANES), lambda h, i: (h, i, 0))],
        out_shape=[jax.ShapeDtypeStruct((S, MH * LANES), F32), jax.ShapeDtypeStruct((MH, S, LANES), F32)],
        scratch_shapes=[pltpu.VMEM((T, LANES), F32), pltpu.VMEM((T, LANES), F32), pltpu.VMEM((T, LANES), F32),
                        pltpu.VMEM((T, T), F32), pltpu.VMEM((T, T), F32)],
        compiler_params=_params(("parallel", "parallel")),
    )(qf, kf, vb)


def _attn_delta(do, o, MH, *, deps=(), tr=512):
    S = do.shape[0]
    tr = _tile(S, tr)

    def body(do_ref, o_ref, *rest):
        d_ref, dob_ref = rest[-2], rest[-1]
        dov = do_ref[...]
        d_ref[...] = jnp.broadcast_to(jnp.sum(dov * o_ref[...], axis=-1, keepdims=True), (tr, LANES))
        dob_ref[...] = dov.astype(BF16)

    head = pl.BlockSpec((tr, LANES), lambda i, h: (i, h))
    return _pallas(
        body, name="attn_delta", grid=(S // tr, MH), in_specs=[head, head] + [_ANY] * len(deps),
        out_specs=[pl.BlockSpec((None, tr, LANES), lambda i, h: (h, i, 0)), head],
        out_shape=[jax.ShapeDtypeStruct((MH, S, LANES), F32), jax.ShapeDtypeStruct((S, MH * LANES), BF16)],
        compiler_params=_params(("parallel", "parallel")),
    )(do, o, *deps)


def _attn_bwd(qf, kf, vb, dob, lse2, delta, MH, *, T):
    S = qf.shape[0]
    nt = S // T

    def body(q_ref, k_ref, v_ref, do_ref, lse_ref, dl_ref, dq_ref, dkv_ref, dkpe_ref, dk_sc, dv_sc,
             s_a, dp_a, s_b, dp_b):
        kj = pl.program_id(1)

        @pl.when(kj == 0)
        def _():
            dq_ref[...] = jnp.zeros_like(dq_ref)

        dk_sc[...] = jnp.zeros_like(dk_sc)
        dv_sc[...] = jnp.zeros_like(dv_sc)

        def rows_of(qt):
            return pl.ds(pl.multiple_of(qt * T, T), T)

        def products(qt):
            rows = rows_of(qt)
            return _dot(q_ref[rows, :], k_ref[...], NT), _dot(do_ref[rows, :], v_ref[...], NT)

        def update(s, dp, qt):
            rows = rows_of(qt)
            q, dov = q_ref[rows, :], do_ref[rows, :]
            p = jnp.exp2(s - _lanes_to(lse_ref[rows, :], T))
            ds = p * (dp - _lanes_to(dl_ref[rows, :], T))
            dv_sc[...] += _dot(p, dov, TN)
            dk_sc[...] += _dot(ds, q, TN)
            dq_ref[rows, :] += _dot(ds, k_ref[...], NN)

        s, dp = products(kj)
        update(jnp.where(_chunk_mask(T), s, NEG), dp, kj)
        rest = nt - 1 - kj

        @pl.when(rest % 2 == 1)
        def _():
            s1, dp1 = products(nt - 1)
            update(s1, dp1, nt - 1)

        pairs = rest // 2

        @pl.when(pairs > 0)
        def _():
            s_a[...], dp_a[...] = products(kj + 1)

        def pair(j, carry):
            t0 = kj + 1 + 2 * j
            s_b[...], dp_b[...] = products(t0 + 1)
            update(s_a[...], dp_a[...], t0)
            s_a[...], dp_a[...] = products(jnp.minimum(t0 + 2, kj + 2 * pairs))
            update(s_b[...], dp_b[...], t0 + 1)
            return carry

        lax.fori_loop(0, pairs, pair, 0)
        dkv_ref[:, :QK_NOPE] = (dk_sc[:, :QK_NOPE] * (1.0 / LOG2E)).astype(BF16)
        dkv_ref[:, QK_NOPE:] = dv_sc[...].astype(BF16)
        dkpe_ref[...] = dk_sc[:, QK_NOPE:] * (1.0 / LOG2E)

    stat = pl.BlockSpec((None, S, LANES), lambda h, j: (h, 0, 0))
    return _pallas(
        body, name="attn_bwd", grid=(MH, nt),
        in_specs=[pl.BlockSpec((S, 2 * LANES), lambda h, j: (0, h)),
                  pl.BlockSpec((T, 2 * LANES), lambda h, j: (j, h)),
                  pl.BlockSpec((T, LANES), lambda h, j: (j, h)),
                  pl.BlockSpec((S, LANES), lambda h, j: (0, h)), stat, stat],
        out_specs=[pl.BlockSpec((S, 2 * LANES), lambda h, j: (0, h)),
                   pl.BlockSpec((T, 2 * LANES), lambda h, j: (j, h)),
                   pl.BlockSpec((T, LANES), lambda h, j: (j, h))],
        out_shape=[jax.ShapeDtypeStruct((S, MH * 2 * LANES), F32), jax.ShapeDtypeStruct((S, MH * 2 * LANES), BF16),
                   jax.ShapeDtypeStruct((S, MH * LANES), F32)],
        scratch_shapes=[pltpu.VMEM((T, 2 * LANES), F32), pltpu.VMEM((T, LANES), F32)] + [pltpu.VMEM((T, T), F32)] * 4,
        compiler_params=_params(("parallel", "arbitrary")),
    )(qf, kf, vb, dob, lse2, delta)


def _attn_post(dqf, dkpe_h, tabs, MH, *, tr=512):
    S = dqf.shape[0]
    tr = _tile(S, tr)
    c_t, s1_t, s2_t = tabs
    scale = (QK_NOPE + QK_ROPE) ** -0.5

    def body(dq_ref, dk_ref, c_ref, s1_ref, s2_ref, dqa_ref, dkpe_ref, acc):
        h = pl.program_id(1)
        c, s1, s2 = c_ref[...], s1_ref[...], s2_ref[...]
        dqa_ref[:, :QK_NOPE] = (dq_ref[:, :QK_NOPE] * scale).astype(BF16)
        dqa_ref[:, QK_NOPE:] = (_rope_pe_t(dq_ref[:, QK_NOPE:], c, s1, s2) * scale).astype(BF16)

        @pl.when(h == 0)
        def _():
            acc[...] = dk_ref[...]

        @pl.when(h > 0)
        def _():
            acc[...] += dk_ref[...]

        @pl.when(h == MH - 1)
        def _():
            dkpe_ref[...] = _rope_pe_t(acc[...], c, s1, s2).astype(BF16)

    tab = pl.BlockSpec((tr, LANES), lambda i, h: (i, 0))
    head2 = pl.BlockSpec((tr, 2 * LANES), lambda i, h: (i, h))
    return _pallas(
        body, name="attn_post", grid=(S // tr, MH),
        in_specs=[head2, pl.BlockSpec((tr, LANES), lambda i, h: (i, h)), tab, tab, tab],
        out_specs=[head2, tab],
        out_shape=[jax.ShapeDtypeStruct((S, MH * 2 * LANES), BF16), jax.ShapeDtypeStruct((S, LANES), BF16)],
        scratch_shapes=[pltpu.VMEM((tr, LANES), F32)],
        compiler_params=_params(("parallel", "arbitrary")),
    )(dqf, dkpe_h, c_t, s1_t, s2_t)


ROW_ALIGN = 16


def _blk(R, C, block_bytes=2 << 20):
    cap = max(ROW_ALIGN, block_bytes // (C * 4))
    for t in range(min(R, cap) // ROW_ALIGN * ROW_ALIGN, LANES - 1, -ROW_ALIGN):
        if R % t == 0:
            return t, C
    if R <= cap:
        return R, C
    tc = C
    while R * tc * 4 > block_bytes and tc % (2 * LANES) == 0:
        tc //= 2
    return R, tc


def _rows_call(fn, ins, out_dtypes, *, name):
    R, C = ins[0].shape
    tr, tc = _blk(R, C)
    n_in = len(ins)

    def body(*refs):
        vals = fn(*[r[...] for r in refs[:n_in]])
        for r, v in zip(refs[n_in:], vals):
            r[...] = v.astype(r.dtype)

    blk = pl.BlockSpec((tr, tc), lambda i, j: (i, j))
    res = _pallas(
        body, name=name, grid=(R // tr, C // tc), in_specs=[blk] * n_in, out_specs=[blk] * len(out_dtypes),
        out_shape=[jax.ShapeDtypeStruct((R, C), d) for d in out_dtypes],
        compiler_params=_params(("parallel", "parallel")),
    )(*ins)
    return res


def _adamw_vals(w, g, m, v):
    m = ADAM_B1 * m + (1.0 - ADAM_B1) * g
    v = ADAM_B2 * v + (1.0 - ADAM_B2) * (g * g)
    m_hat = m / (1.0 - ADAM_B1 ** ADAM_STEP)
    v_hat = v / (1.0 - ADAM_B2 ** ADAM_STEP)
    delta = -ADAM_LR * (m_hat / (jnp.sqrt(v_hat) + ADAM_EPS) + ADAM_WD * w)
    return delta, m, v


def _sum_pair(p, theirs, place, *, name):
    _, R, C = p.shape
    R2 = R // 2
    tr, tc = _blk(R2, C)
    p4 = p.reshape(N_CHIPS, 2, R2, C)

    def body(place_ref, a_ref, b_ref, o_ref):
        o_ref[...] = (a_ref[...].astype(F32) + b_ref[...].astype(F32)).astype(BF16)

    spec = pltpu.PrefetchScalarGridSpec(
        num_scalar_prefetch=1, grid=(N_CHIPS, R2 // tr, C // tc),
        in_specs=[pl.BlockSpec((None, None, tr, tc), lambda q, i, j, pr: (q, pr[0], i, j)),
                  pl.BlockSpec((None, tr, tc), lambda q, i, j, pr: (q, i, j))],
        out_specs=pl.BlockSpec((None, tr, tc), lambda q, i, j, pr: (q, i, j)))
    return _pallas(body, name=name, grid_spec=spec, out_shape=jax.ShapeDtypeStruct((N_CHIPS, R2, C), BF16),
                   compiler_params=_params(("parallel", "parallel", "parallel")))(place, p4, theirs)


def _sum_chips(p, theirs, recv, place, *, name):
    _, R, C = p.shape
    R2 = R // 2
    tr, tc = _blk(R2, C)
    p4 = p.reshape(N_CHIPS, 2, R2, C)

    def body(place_ref, a_ref, b_ref, r0_ref, r1_ref, r2_ref, o_ref):
        own = a_ref[...].astype(F32) + b_ref[...].astype(F32)
        o_ref[...] = ((own + r0_ref[...].astype(F32)) + r1_ref[...].astype(F32)) + r2_ref[...].astype(F32)

    def slot(k):
        return pl.BlockSpec((None, tr, tc), lambda i, j, pr: (k, i, j))

    spec = pltpu.PrefetchScalarGridSpec(
        num_scalar_prefetch=1, grid=(R2 // tr, C // tc),
        in_specs=[pl.BlockSpec((None, None, tr, tc), lambda i, j, pr: (pr[1], pr[0], i, j)),
                  pl.BlockSpec((None, tr, tc), lambda i, j, pr: (pr[1], i, j)), slot(0), slot(1), slot(2)],
        out_specs=pl.BlockSpec((None, tr, tc), lambda i, j, pr: (pr[0], i, j)))
    return _pallas(body, name=name, grid_spec=spec, out_shape=jax.ShapeDtypeStruct((2, R2, C), F32),
                   compiler_params=_params(("parallel", "parallel")))(place, p4, theirs, recv, recv, recv)


def _me():
    return lax.axis_index("x"), lax.axis_index("y"), lax.axis_index("c")


def _other_chips(x, y):
    return [(1 - x, y), (x, 1 - y), (1 - x, 1 - y)]


def _rcopy(src, dst, ssem, rsem, dev):
    return pltpu.make_async_remote_copy(src_ref=src, dst_ref=dst, send_sem=ssem, recv_sem=rsem,
                                        device_id=dev, device_id_type=MESH)


def _cast_into_slot(w, place, *, name, rows=None, deps=()):
    R, C = w.shape
    rows = R if rows is None else rows
    tr, tc = _blk(R, C)

    def body(place_ref, w_ref, *rest):
        rest[-1][...] = w_ref[...].astype(BF16)

    spec = pltpu.PrefetchScalarGridSpec(
        num_scalar_prefetch=1, grid=(R // tr, C // tc),
        in_specs=[pl.BlockSpec((tr, tc), lambda i, j, pr: (i, j))] + [_ANY] * len(deps),
        out_specs=pl.BlockSpec((None, tr, tc), lambda i, j, pr: (pr[1], i, j)))
    out = _pallas(body, name=name, grid_spec=spec, out_shape=jax.ShapeDtypeStruct((N_CHIPS, rows, C), BF16),
                  compiler_params=_params(("parallel", "parallel")))(place, w, *deps)
    return out.reshape(N_CHIPS, 2, rows // 2, C)


def _gather_ici_plan(bufs):
    x, y, c = _me()
    j = 2 * x + y
    plan = []
    for i, buf in enumerate(bufs):
        for k, (px, py) in enumerate(_other_chips(x, y)):
            plan.append((3 * i + k, buf.at[j, c], buf.at[j, c], (px, py, c)))
    return plan


def _gather_weights(bufs, *, name):
    n = len(bufs)

    def body(*refs):
        outs = refs[n:2 * n]
        ssem, rsem, fssem, frsem = refs[2 * n:]
        x, y, c = _me()
        sib = (x, y, 1 - c)
        chips = _other_chips(x, y)
        remote = []
        for s, src, dst, dev in _gather_ici_plan(outs):
            r = _rcopy(src, dst, ssem.at[s], rsem.at[s], dev)
            r.start()
            remote.append(r)
        for i in range(n):
            for k, (px, py) in enumerate(chips):
                slot = outs[i].at[2 * px + py, c]
                _rcopy(slot, slot, ssem.at[3 * i + k], rsem.at[3 * i + k], (px, py, c)).wait_recv()
                f = _rcopy(slot, slot, fssem.at[3 * i + k], frsem.at[3 * i + k], sib)
                f.start()
                remote.append(f)
        for i in range(n):
            for k, (px, py) in enumerate(chips):
                slot = outs[i].at[2 * px + py, 1 - c]
                _rcopy(slot, slot, fssem.at[3 * i + k], frsem.at[3 * i + k], sib).wait_recv()
        for r in remote:
            r.wait_send()

    return _pallas(
        body, name=name, in_specs=[_ANY] * n, out_specs=[_ANY] * n,
        out_shape=[jax.ShapeDtypeStruct(b.shape, b.dtype) for b in bufs],
        scratch_shapes=[pltpu.SemaphoreType.DMA((3 * n,))] * 4,
        input_output_aliases={i: i for i in range(n)},
        compiler_params=pltpu.CompilerParams(has_side_effects=True),
    )(*bufs)


def _forward_halves(bufs, *, name):
    n = len(bufs)

    def body(*refs):
        outs = refs[n:2 * n]
        ssem, rsem = refs[2 * n:]
        x, y, c = _me()
        sib = (x, y, 1 - c)
        cps = []
        for i in range(n):
            for k, (px, py) in enumerate(_other_chips(x, y)):
                slot = outs[i].at[2 * px + py, c]
                r = _rcopy(slot, slot, ssem.at[3 * i + k], rsem.at[3 * i + k], sib)
                r.start()
                cps.append(r)
        for r in cps:
            r.wait()

    return _pallas(
        body, name=name, in_specs=[_ANY] * n, out_specs=[_ANY] * n,
        out_shape=[jax.ShapeDtypeStruct(b.shape, b.dtype) for b in bufs],
        scratch_shapes=[pltpu.SemaphoreType.DMA((3 * n,))] * 2,
        input_output_aliases={i: i for i in range(n)},
        compiler_params=pltpu.CompilerParams(has_side_effects=True),
    )(*bufs)


_HBM = pl.BlockSpec(memory_space=pltpu.HBM)
_SEM = pl.BlockSpec(memory_space=pltpu.SEMAPHORE)
_EFFECT = pltpu.SideEffectType.DATAFLOW_SIDE_EFFECTING


def _split_start(bufs, plan, n_copies, *, name):
    n = len(bufs)

    def body(*refs):
        ssem, rsem = refs[n], refs[n + 1]
        for s, src, dst, dev in plan(refs[:n]):
            _rcopy(src, dst, ssem.at[s], rsem.at[s], dev).start()
        refs[-1][...] = jnp.zeros_like(refs[-1])

    res = _pallas(
        body, name=name, in_specs=[_HBM] * n,
        out_specs=(_SEM, _SEM, *[_HBM] * n, pl.BlockSpec(memory_space=pltpu.VMEM)),
        out_shape=(pltpu.SemaphoreType.DMA((n_copies,)), pltpu.SemaphoreType.DMA((n_copies,)),
                   *[pltpu.HBM(b.shape, b.dtype) for b in bufs], jax.ShapeDtypeStruct((8, LANES), F32)),
        input_output_aliases={i: 2 + i for i in range(n)},
        compiler_params=pltpu.CompilerParams(has_side_effects=_EFFECT),
    )(*[pltpu.with_memory_space_constraint(b, pltpu.HBM) for b in bufs])
    return res[0], res[1], list(res[2:2 + n]), res[-1]


def _split_wait(ssem, rsem, bufs, after, plan, *, name):
    n = len(bufs)

    def body(*refs):
        ssem_ref, rsem_ref = refs[n], refs[n + 1]
        for s, src, dst, dev in plan(refs[:n]):
            cp = _rcopy(src, dst, ssem_ref.at[s], rsem_ref.at[s], dev)
            cp.wait_send()
            cp.wait_recv()

    return list(_pallas(
        body, name=name, in_specs=[_HBM] * n + [_SEM, _SEM, _ANY], out_specs=[_HBM] * n,
        out_shape=[pltpu.HBM(b.shape, b.dtype) for b in bufs],
        input_output_aliases={i: i for i in range(n)},
        compiler_params=pltpu.CompilerParams(has_side_effects=_EFFECT),
    )(*bufs, ssem, rsem, after))


def _scatter_plan(n):
    def plan(bufs):
        x, y, c = _me()
        out = []
        for i in range(n):
            for k, (px, py) in enumerate(_other_chips(x, y)):
                out.append((3 * i + k, bufs[i].at[2 * px + py], bufs[n + i].at[k], (px, py, c)))
        return out
    return plan


def _swap_halves(grads, *, name):
    n = len(grads)
    views = [g.reshape(N_CHIPS, 2, g.shape[1] // 2, g.shape[2]) for g in grads]

    def body(*refs):
        ins, outs = refs[:n], refs[n:2 * n]
        ssem, rsem = refs[2 * n:]
        x, y, c = _me()
        sib = (x, y, 1 - c)
        cps = []
        for i in range(n):
            r = _rcopy(ins[i].at[:, 1 - c], outs[i], ssem.at[i], rsem.at[i], sib)
            r.start()
            cps.append(r)
        for r in cps:
            r.wait()

    return _pallas(
        body, name=name, in_specs=[_ANY] * n, out_specs=[_ANY] * n,
        out_shape=[jax.ShapeDtypeStruct((N_CHIPS,) + v.shape[2:], v.dtype) for v in views],
        scratch_shapes=[pltpu.SemaphoreType.DMA((n,)), pltpu.SemaphoreType.DMA((n,))],
        compiler_params=pltpu.CompilerParams(has_side_effects=True),
    )(*views)


def _join_halves(halves, *, name):
    n = len(halves)

    def body(*refs):
        outs = refs[n:2 * n]
        ssem, rsem = refs[2 * n:]
        x, y, c = _me()
        sib = (x, y, 1 - c)
        cps = []
        for i in range(n):
            r = _rcopy(outs[i].at[c], outs[i].at[c], ssem.at[i], rsem.at[i], sib)
            r.start()
            cps.append(r)
        for r in cps:
            r.wait()

    return _pallas(
        body, name=name, in_specs=[_ANY] * n, out_specs=[_ANY] * n,
        out_shape=[jax.ShapeDtypeStruct(h.shape, h.dtype) for h in halves],
        scratch_shapes=[pltpu.SemaphoreType.DMA((n,)), pltpu.SemaphoreType.DMA((n,))],
        input_output_aliases={i: i for i in range(n)},
        compiler_params=pltpu.CompilerParams(has_side_effects=True),
    )(*halves)


def _allreduce_small(vec):
    R = vec.shape[0]

    def body(v_ref, o_ref, buf, ssem, rsem):
        x, y, c = _me()
        me = 4 * x + 2 * y + c
        buf[me] = v_ref[...]
        cps = []
        for k in range(1, 8):
            peer = (x ^ (k >> 2), y ^ ((k >> 1) & 1), c ^ (k & 1))
            r = _rcopy(v_ref, buf.at[me], ssem.at[k - 1], rsem.at[k - 1], peer)
            r.start()
            cps.append(r)
        for k in range(1, 8):
            peer = (x ^ (k >> 2), y ^ ((k >> 1) & 1), c ^ (k & 1))
            pid = 4 * peer[0] + 2 * peer[1] + peer[2]
            _rcopy(v_ref, buf.at[pid], ssem.at[k - 1], rsem.at[k - 1], peer).wait_recv()
        for r in cps:
            r.wait_send()
        tot = buf[0]
        for d in range(1, 8):
            tot = tot + buf[d]
        o_ref[...] = tot

    vm = pl.BlockSpec(memory_space=pltpu.VMEM)
    return _pallas(
        body, name="allreduce_small", in_specs=[vm], out_specs=vm,
        out_shape=jax.ShapeDtypeStruct((R, LANES), F32),
        scratch_shapes=[pltpu.VMEM((8, R, LANES), F32), pltpu.SemaphoreType.DMA((7,)), pltpu.SemaphoreType.DMA((7,))],
        compiler_params=pltpu.CompilerParams(has_side_effects=True),
    )(vec)


def _rope_tables(positions, S):
    pos = positions.reshape(S, 1).astype(F32)
    half = RET_QK // 2
    inv = ROPE_THETA ** (-jnp.arange(half, dtype=F32) / half)
    ang = pos * inv
    cosr = jnp.concatenate([jnp.cos(ang), jnp.cos(ang)], axis=1)
    sinr = jnp.concatenate([-jnp.sin(ang), jnp.sin(ang)], axis=1)
    half = QK_ROPE // 2
    inv = ROPE_THETA ** (-jnp.arange(half, dtype=F32) / half)
    ang = pos * inv
    z = jnp.zeros((S, half), F32)
    c = jnp.concatenate([jnp.cos(ang), jnp.cos(ang), z, z], axis=1)
    s1 = jnp.concatenate([-jnp.sin(ang), z, z, z], axis=1)
    s2 = jnp.concatenate([z, jnp.sin(ang), z, z], axis=1)
    return cosr, sinr, (c, s1, s2)


def _cat_cols(g):
    return jnp.concatenate([g[j] for j in range(N_CHIPS)], axis=1)


def _split_cols(w):
    return jnp.stack(jnp.split(w, N_CHIPS, axis=1))


def _pack_small(vs, rows):
    flat = jnp.concatenate([v.reshape(-1) for v in vs])
    flat = jnp.pad(flat, (0, rows * LANES - flat.shape[0]))
    return flat.reshape(rows, LANES)


def kernel(x, positions, norm_mix_g, w_in, ret_norm_g, w_ret_o, q_a_norm_g, w_q_b, kv_a_norm_g, w_kv_b, w_mla_o, w_out, norm_mlp_g, w_up, w_down, norm_f_g, loss_target, m_norm_mix_g, m_w_in, m_ret_norm_g, m_w_ret_o, m_q_a_norm_g, m_w_q_b, m_kv_a_norm_g, m_w_kv_b, m_w_mla_o, m_w_out, m_norm_mlp_g, m_w_up, m_w_down, m_norm_f_g, v_norm_mix_g, v_w_in, v_ret_norm_g, v_w_ret_o, v_q_a_norm_g, v_w_q_b, v_kv_a_norm_g, v_w_kv_b, v_w_mla_o, v_w_out, v_norm_mlp_g, v_w_up, v_w_down, v_norm_f_g):
    S, D = x.shape[1], x.shape[2]
    RVW = w_ret_o.shape[1] * N_CHIPS
    RH = RVW // RET_V
    RQW = RH * RET_QK
    MVW = w_mla_o.shape[1] * N_CHIPS
    MH = MVW // V_HEAD
    QL, KVL = w_q_b.shape[1], w_kv_b.shape[1]
    T_RET = _tile(S, 256)
    T_ATT = _tile(S, 512)

    xs = x.reshape(S, D)
    tgt = loss_target.reshape(S, D)
    cosr, sinr, pe_tabs = _rope_tables(positions, S)
    lgam = jnp.log(1.0 - 2.0 ** (-5.0 - jnp.arange(RH, dtype=F32)))
    lgam = jnp.broadcast_to(lgam[:, None, None], (RH, 8, LANES))

    big = ("w_in", "w_ret_o", "w_q_b", "w_kv_b", "w_mla_o", "w_out", "w_up", "w_down")
    w_sh = dict(w_in=w_in[0].T, w_ret_o=w_ret_o[0], w_q_b=w_q_b[0], w_kv_b=w_kv_b[0], w_mla_o=w_mla_o[0],
                w_out=w_out[0], w_up=w_up[0], w_down=w_down[0])
    m_sh = dict(w_in=m_w_in[0].T, w_ret_o=m_w_ret_o[0], w_q_b=m_w_q_b[0], w_kv_b=m_w_kv_b[0],
                w_mla_o=m_w_mla_o[0], w_out=m_w_out[0], w_up=m_w_up[0], w_down=m_w_down[0])
    v_sh = dict(w_in=v_w_in[0].T, w_ret_o=v_w_ret_o[0], w_q_b=v_w_q_b[0], w_kv_b=v_w_kv_b[0],
                w_mla_o=v_w_mla_o[0], w_out=v_w_out[0], w_up=v_w_up[0], w_down=v_w_down[0])
    col_sharded = ("w_q_b", "w_kv_b", "w_up")
    c_sh = w_in.shape[2]
    c_pad = -(-c_sh // 64) * 64
    place = jnp.stack([lax.axis_index("c"), 2 * lax.axis_index("x") + lax.axis_index("y")]).astype(jnp.int32)

    def whole(k, g):
        g = g.reshape(N_CHIPS, w_sh[k].shape[0], w_sh[k].shape[1])
        if k == "w_up":
            return g
        return _cat_cols(g) if k in col_sharded else g.reshape(-1, g.shape[2])

    first = ("w_in", "w_q_b", "w_kv_b")
    later = ("w_ret_o", "w_mla_o", "w_out", "w_up", "w_down")
    got = _gather_weights([_cast_into_slot(w_sh[k], place, name="cast_" + k, rows=c_pad if k == "w_in" else None)
                           for k in first], name="gather_first")
    full = {k: whole(k, g) for k, g in zip(first[1:], got[1:])}
    later_bufs = [_cast_into_slot(w_sh[k], place, name="cast_" + k, deps=(got[0],)) for k in later]
    later_ssem, later_rsem, later_bufs, later_token = _split_start(
        later_bufs, _gather_ici_plan, 3 * len(later), name="gather_later_start")

    o_rq, o_rk, o_rv, o_rg = 0, RQW, 2 * RQW, 2 * RQW + RVW
    o_cq = 2 * RQW + 2 * RVW
    o_ckv, o_kpe = o_cq + QL, o_cq + QL + KVL
    o_gr = o_kpe + QK_ROPE
    o_gm = o_gr + D
    n_ret = RH * RET_HEAD_COLS
    off_gret, off_gmla, off_cq, off_ckv = n_ret, n_ret + D, n_ret + 2 * D, n_ret + 2 * D + QL
    n_a = off_ckv + KVL
    runs = []
    for h in range(RH):
        base = h * RET_HEAD_COLS
        runs += [(o_rq + h * RET_QK, RET_QK, base), (o_rk + h * RET_QK, RET_QK, base + RET_QK),
                 (o_rv + h * RET_V, RET_V, base + 2 * RET_QK), (o_rg + h * RET_V, RET_V, base + 2 * RET_QK + RET_V)]
    runs += [(o_gr, D, off_gret), (o_gm, D, off_gmla), (o_cq, QL, off_cq), (o_ckv, KVL, off_ckv),
             (o_kpe, QK_ROPE, n_a)]

    def take(parts, start, width):
        out, lo = [], 0
        for p in parts:
            hi = lo + p.shape[0]
            a, b = max(start, lo), min(start + width, hi)
            if a < b:
                out.append(p[a - lo:b - lo])
            lo = hi
        return out

    wi = [got[0].reshape(N_CHIPS, c_pad, D)[jj, :c_sh] for jj in range(N_CHIPS)]
    here = sorted(runs, key=lambda r: r[2])
    wa = jnp.concatenate([p for s0, w, _ in here[:-1] for p in take(wi, s0, w)], axis=0)
    wkpe = jnp.concatenate(take(wi, o_kpe, QK_ROPE) + [jnp.zeros((LANES - QK_ROPE, D), BF16)], axis=0)
    wq = jnp.pad(full["w_q_b"].reshape(QL, MH, QK_NOPE + QK_ROPE),
                 ((0, 0), (0, 0), (0, LANES - QK_ROPE))).reshape(QL, MH * 2 * LANES)
    wkv = full["w_kv_b"]

    u, rstd0 = _rmsnorm_fwd(xs, norm_mix_g, name="norm_mix")
    proj = _mm(u, wa, mode="nt", outs=[F32], name="in_proj", deps=(later_token,))
    kpe = _mm(u, wkpe, mode="nt", outs=[F32], name="kpe_proj")
    ry, gated, states = _ret_fwd(proj, cosr, sinr, lgam, ret_norm_g, RH, T=T_RET)
    cqn, rstd_q = _rmsnorm_fwd(proj, q_a_norm_g, name="norm_q", width=QL, col=off_cq // QL)
    ckvn, rstd_kv = _rmsnorm_fwd(proj, kv_a_norm_g, name="norm_kv", width=KVL, col=off_ckv // KVL)
    qf, kf, vb = _qkv_proj(cqn, ckvn, wq, wkv, kpe, pe_tabs, MH)
    my, lse2 = _attn_fwd(qf, kf, vb, MH, T=T_ATT)
    later_bufs = _split_wait(later_ssem, later_rsem, later_bufs, my, _gather_ici_plan, name="gather_later_wait")
    later_bufs = _forward_halves(later_bufs, name="gather_later_forward")
    full.update({k: whole(k, g) for k, g in zip(later, later_bufs)})
    y_ret = _mm(gated, full["w_ret_o"], mode="nn", outs=[F32], name="ret_o")
    y_mla = _mm(my, full["w_mla_o"], mode="nn", outs=[F32], name="mla_o")
    merged = _merge_fwd(proj, y_ret, y_mla, D, off_gret, off_gmla)
    h1 = _mm(merged, full["w_out"], mode="nn", outs=[F32], name="out_proj",
             epi=lambda acc, r: (acc + r,), extras=(xs,))
    n1, rstd1 = _rmsnorm_fwd(h1, norm_mlp_g, name="norm_mlp")

    def up_epi(acc):
        r = jnp.maximum(acc, 0.0)
        return acc, r * r

    z, act = _mm(n1, full["w_up"], mode="nn", outs=[F32, BF16], name="up_proj", epi=up_epi)
    h2 = _mm(act, full["w_down"], mode="nn", outs=[F32], name="down_proj",
             epi=lambda acc, r: (acc + r,), extras=(h1,))
    loss11, dh2, g_norm_f = _final_loss(h2, norm_f_g.reshape(1, D), tgt)

    dz = _mm(dh2, full["w_down"], mode="nt", outs=[BF16], name="down_bwd_x",
             epi=lambda acc, zz: (acc * (2.0 * jnp.maximum(zz, 0.0)),), extras=(z,))
    g_w_down = _mm(act, dh2, mode="tn", outs=[BF16], name="down_bwd_w")
    dn1 = _mm(dz, full["w_up"], mode="nt", outs=[F32], name="up_bwd_x")
    g_w_up = _mm(n1, dz, mode="tn", outs=[BF16], name="up_bwd_w", out_shards=True)

    def reduce_begin(tag, names, grads):
        pcs = [g if g.ndim == 3 else g.reshape(N_CHIPS, g.shape[0] // N_CHIPS, g.shape[1]) for g in grads]
        theirs = _swap_halves(pcs, name="swap_" + tag)
        sums = [_sum_pair(p, t, place, name="sum_pair_" + k) for k, p, t in zip(names, pcs, theirs)]
        return pcs, theirs, sums

    def scatter_begin(tag, sums):
        lands = [lax.empty((3,) + s.shape[1:], s.dtype) for s in sums]
        return _split_start(sums + lands, _scatter_plan(len(sums)), 3 * len(sums), name="scatter_" + tag + "_start")

    g1 = ("w_up", "w_down")
    pcs1, theirs1, sums1 = reduce_begin("g1", g1, (g_w_up, g_w_down))
    ssem1, rsem1, bufs1, token1 = scatter_begin("g1", sums1)
    dh1, g_norm_mlp = _rmsnorm_bwd(dn1, h1, rstd1, norm_mlp_g, name="norm_mlp_bwd", res=dh2, deps=(token1,))
    dmerged = _mm(dh1, full["w_out"], mode="nt", outs=[F32], name="out_bwd_x")
    g_w_out = _mm(merged, dh1, mode="tn", outs=[BF16], name="out_bwd_w")
    dproj, dy_ret, dy_mla = _merge_bwd(dmerged, proj, y_ret, y_mla, D, off_gret)
    dgated = _mm(dy_ret, full["w_ret_o"], mode="nt", outs=[F32], name="ret_o_bwd_x")
    g_w_ret_o = _mm(gated, dy_ret, mode="tn", outs=[BF16], name="ret_o_bwd_w")
    dproj, g_ret_norm = _ret_bwd(proj, cosr, sinr, lgam, ret_norm_g, ry, dgated, states, dproj, RH, T=T_RET)
    dmy = _mm(dy_mla, full["w_mla_o"], mode="nt", outs=[F32], name="mla_o_bwd_x")
    g_w_mla_o = _mm(my, dy_mla, mode="tn", outs=[BF16], name="mla_o_bwd_w")
    g2 = ("w_out", "w_ret_o", "w_mla_o")
    pcs2, theirs2, sums2 = reduce_begin("g2", g2, (g_w_out, g_w_ret_o, g_w_mla_o))
    ssem2, rsem2, bufs2, token2 = scatter_begin("g2", sums2)
    delta, dob = _attn_delta(dmy, my, MH, deps=(token2,))
    dqf, dkv_all, dkpe_h = _attn_bwd(qf, kf, vb, dob, lse2, delta, MH, T=T_ATT)
    dq_all, dkpe = _attn_post(dqf, dkpe_h, pe_tabs, MH)
    dcqn = _mm(dq_all, wq, mode="nt", outs=[F32], name="q_bwd_x")
    g_wq = _mm(cqn, dq_all, mode="tn", outs=[BF16], name="q_bwd_w")
    dckvn = _mm(dkv_all, wkv, mode="nt", outs=[F32], name="kv_bwd_x")
    g_wkv = _mm(ckvn, dkv_all, mode="tn", outs=[BF16], name="kv_bwd_w")
    dproj, g_q_a = _rmsnorm_bwd(dcqn, proj, rstd_q, q_a_norm_g, name="norm_q_bwd", into=(dproj, off_cq // QL),
                                width=QL, col=off_cq // QL)
    dproj, g_kv_a = _rmsnorm_bwd(dckvn, proj, rstd_kv, kv_a_norm_g, name="norm_kv_bwd", into=(dproj, off_ckv // KVL),
                                 width=KVL, col=off_ckv // KVL)
    g_wa = _mm(dproj, u, mode="tn", outs=[BF16], name="in_bwd_w")
    g_wkpe = _mm(dkpe, u, mode="tn", outs=[BF16], name="kpe_bwd_w")

    there = sorted(runs)
    g_parts = [g_wa, g_wkpe]
    g_w_in = jnp.stack([jnp.concatenate(
        [p for s0, w, d0 in there for a, b in [(max(s0, jj * c_sh), min(s0 + w, (jj + 1) * c_sh))] if a < b
         for p in take(g_parts, d0 + a - s0, b - a)] + [jnp.zeros((c_pad - c_sh, D), BF16)], axis=0)
        for jj in range(N_CHIPS)])
    gq = g_wq.reshape(QL, MH, 2 * LANES)[:, :, :QK_NOPE + QK_ROPE].reshape(QL, MH * (QK_NOPE + QK_ROPE))
    g3 = ("w_in", "w_q_b", "w_kv_b")
    pcs3, theirs3, sums3 = reduce_begin("g3", g3, (g_w_in, _split_cols(gq), _split_cols(g_wkv)))
    ssem3, rsem3, bufs3, token3 = scatter_begin("g3", sums3)
    du_a = _mm(dproj, wa, mode="nn", outs=[F32], name="in_bwd_x", tk=1024, deps=(token3,))
    du = _mm(dkpe, wkpe, mode="nn", outs=[F32], name="kpe_bwd_x", epi=lambda acc, r: (acc + r,), extras=(du_a,))
    dx, g_norm_mix = _rmsnorm_bwd(du, xs, rstd0, norm_mix_g, name="norm_mix_bwd", res=dh1)

    bufs1 = _split_wait(ssem1, rsem1, bufs1, dx, _scatter_plan(len(g1)), name="scatter_g1_wait")
    bufs2 = _split_wait(ssem2, rsem2, bufs2, dx, _scatter_plan(len(g2)), name="scatter_g2_wait")
    bufs3 = _split_wait(ssem3, rsem3, bufs3, dx, _scatter_plan(len(g3)), name="scatter_g3_wait")
    recv1, recv2, recv3 = bufs1[len(g1):], bufs2[len(g2):], bufs3[len(g3):]
    halves = {}
    for names, pcs, theirs, recv in ((g1, pcs1, theirs1, recv1), (g2, pcs2, theirs2, recv2), (g3, pcs3, theirs3, recv3)):
        for k, p, t, r in zip(names, pcs, theirs, recv):
            halves[k] = _sum_chips(p, t, r, place, name="sum_chips_" + k)
    joined = _join_halves([halves[k] for k in big], name="join_halves")
    g_shard = {k: g.reshape(2 * g.shape[1], g.shape[2]) for k, g in zip(big, joined)}

    small = ("norm_mix_g", "ret_norm_g", "q_a_norm_g", "kv_a_norm_g", "norm_mlp_g", "norm_f_g")
    g_small = [g_norm_mix, g_ret_norm, g_q_a, g_kv_a, g_norm_mlp, g_norm_f]
    sizes = [int(v.size) for v in g_small]
    n_small = sum(sizes) + LANES
    rows = -(-n_small // (8 * LANES)) * 8
    packed = _pack_small(g_small + [jnp.broadcast_to(loss11.reshape(1), (LANES,))], rows)
    red = _allreduce_small(packed).reshape(-1)
    loss = red[sum(sizes)]
    w_small = [norm_mix_g, ret_norm_g, q_a_norm_g, kv_a_norm_g, norm_mlp_g, norm_f_g]
    m_small = [m_norm_mix_g, m_ret_norm_g, m_q_a_norm_g, m_kv_a_norm_g, m_norm_mlp_g, m_norm_f_g]
    v_small = [v_norm_mix_g, v_ret_norm_g, v_q_a_norm_g, v_kv_a_norm_g, v_norm_mlp_g, v_norm_f_g]
    g_pk = red[:rows * LANES].reshape(rows, LANES)
    d_pk, m_pk, v_pk = _rows_call(_adamw_vals, [_pack_small(w_small, rows), g_pk, _pack_small(m_small, rows),
                                               _pack_small(v_small, rows)], [F32, F32, F32], name="adamw_small")
    out_g, out_d, out_m, out_v = {}, {}, {}, {}
    off = 0
    for k, wv, sz in zip(small, w_small, sizes):
        for dst, src in ((out_g, g_pk), (out_d, d_pk), (out_m, m_pk), (out_v, v_pk)):
            dst[k] = src.reshape(-1)[off:off + sz].reshape(wv.shape)
        off += sz

    for k in big:
        res = _rows_call(lambda w, g, m, v: (g,) + _adamw_vals(w, g, m, v),
                         [w_sh[k], g_shard[k], m_sh[k], v_sh[k]], [F32] * 4, name="adamw_" + k)
        if k == "w_in":
            res = [r.T for r in res]
        out_g[k], out_d[k], out_m[k], out_v[k] = [r[None] for r in res]

    order = ("norm_mix_g", "w_in", "ret_norm_g", "w_ret_o", "q_a_norm_g", "w_q_b", "kv_a_norm_g", "w_kv_b",
             "w_mla_o", "w_out", "norm_mlp_g", "w_up", "w_down", "norm_f_g")
    return (loss, dx.reshape(1, S, D), *[out_g[k] for k in order], *[out_d[k] for k in order],
            *[out_m[k] for k in order], *[out_v[k] for k in order])
```

```python
import math

import jax
import jax.numpy as jnp
from jax import lax
from jax.experimental import pallas as pl
from jax.experimental.pallas import tpu as pltpu

F32 = jnp.float32
BF16 = jnp.bfloat16

EPS = 1e-6
ROPE_THETA = 10000.0
CHUNK = 64
RET_QK = 128
RET_V = 256
RET_HEAD_COLS = 2 * RET_QK + 2 * RET_V
QK_NOPE = 128
QK_ROPE = 64
V_HEAD = 128
LANES = 128
LOG2E = math.log2(math.e)

ADAM_LR = 0.001
ADAM_B1 = 0.9
ADAM_B2 = 0.999
ADAM_EPS = 1e-08
ADAM_WD = 0.01
ADAM_STEP = 10

N_CHIPS = 4
VMEM_LIMIT = 56 * 1024 * 1024
MESH = pl.DeviceIdType.MESH
NEG = -1e30


def _pallas(body, **kw):
    return pl.pallas_call(body, **kw)


def _params(sem=None):
    return pltpu.CompilerParams(dimension_semantics=sem, vmem_limit_bytes=VMEM_LIMIT)


def _tile(n, want):
    t = min(n, want)
    while n % t:
        t //= 2
    return t


_ANY = pl.BlockSpec(memory_space=pl.ANY)


def _mm(a, b, *, mode, outs, name, epi=None, extras=(), deps=(), out_shards=False, more_outs=None,
        tm=1024, tn=1024, tk=2048):
    shards = b.shape[0] if b.ndim == 3 else 1
    brows, bcols = b.shape[-2], b.shape[-1] * shards
    if mode == "nn":
        (M, K), N = a.shape, bcols
    elif mode == "nt":
        (M, K), N = a.shape, brows
    else:
        (K, M), N = a.shape, bcols
    tm = _tile(M, tm)
    tn = _tile(N // (shards if mode == "nn" else 1) // (N_CHIPS if out_shards else 1), tn)
    tk = _tile(K // (shards if mode == "nt" else 1), tk)
    nk = K // tk
    if mode == "nn":
        a_spec = pl.BlockSpec((tm, tk), lambda i, j, k: (i, k))
        dims = (((1,), (0,)), ((), ()))
        if shards > 1:
            per = N // shards // tn
            b_spec = pl.BlockSpec((None, tk, tn), lambda i, j, k: (j // per, k, j % per))
        else:
            b_spec = pl.BlockSpec((tk, tn), lambda i, j, k: (k, j))
    elif mode == "nt":
        a_spec = pl.BlockSpec((tm, tk), lambda i, j, k: (i, k))
        dims = (((1,), (1,)), ((), ()))
        if shards > 1:
            per = K // shards // tk
            b_spec = pl.BlockSpec((None, tn, tk), lambda i, j, k: (k // per, j, k % per))
        else:
            b_spec = pl.BlockSpec((tn, tk), lambda i, j, k: (j, k))
    else:
        assert shards == 1
        a_spec = pl.BlockSpec((tk, tm), lambda i, j, k: (k, i))
        b_spec = pl.BlockSpec((tk, tn), lambda i, j, k: (k, j))
        dims = (((0,), (0,)), ((), ()))
    if out_shards:
        assert not extras
        oper = N // N_CHIPS // tn
        o_spec = pl.BlockSpec((None, tm, tn), lambda i, j, k: (j // oper, i, j % oper))
        o_shape = (N_CHIPS, M, N // N_CHIPS)
    else:
        o_spec = pl.BlockSpec((tm, tn), lambda i, j, k: (i, j))
        o_shape = (M, N)
    more = [] if more_outs is None else more_outs(tm, tn)
    n_ex, n_out, n_dep = len(extras), len(outs) + len(more), len(deps)
    if epi is None:
        epi = lambda acc: (acc,)

    def body(*refs):
        a_ref, b_ref = refs[0], refs[1]
        ex_refs = refs[2:2 + n_ex]
        o_refs = refs[2 + n_ex + n_dep:2 + n_ex + n_dep + n_out]
        part = lax.dot_general(a_ref[...].astype(BF16), b_ref[...].astype(BF16), dims,
                               preferred_element_type=F32)

        def finish(acc):
            vals = epi(acc, *[r[...] for r in ex_refs])
            for r, v in zip(o_refs, vals):
                if isinstance(v, (list, tuple)):
                    for lead, piece in enumerate(v):
                        r[lead] = piece.astype(r.dtype)
                else:
                    r[...] = v.astype(r.dtype)

        if nk == 1:
            finish(part)
        else:
            acc_ref = refs[-1]
            k = pl.program_id(2)

            @pl.when(k == 0)
            def _():
                acc_ref[...] = part

            @pl.when(k > 0)
            def _():
                acc_ref[...] += part

            @pl.when(k == nk - 1)
            def _():
                finish(acc_ref[...])

    res = _pallas(
        body, name=name, grid=(M // tm, N // tn, nk),
        in_specs=[a_spec, b_spec] + [o_spec] * n_ex + [_ANY] * n_dep,
        out_specs=[o_spec] * len(outs) + [spec for _, spec in more],
        out_shape=[jax.ShapeDtypeStruct(o_shape, d) for d in outs] + [shape for shape, _ in more],
        scratch_shapes=[pltpu.VMEM((tm, tn), F32)] if nk > 1 else [],
        compiler_params=_params(("parallel", "parallel", "arbitrary")),
    )(a, b, *extras, *deps)
    return res[0] if n_out == 1 else res


def _rmsnorm_fwd(x, g, *, name, width=None, col=0, tr=256):
    S = x.shape[0]
    W = x.shape[1] if width is None else width
    tr = _tile(S, tr)

    def body(x_ref, g_ref, y_ref, r_ref):
        xv = x_ref[...]
        rstd = lax.rsqrt(jnp.mean(xv * xv, axis=-1, keepdims=True) + EPS)
        y_ref[...] = (xv * rstd * g_ref[...]).astype(BF16)
        r_ref[...] = rstd

    return _pallas(
        body, name=name, grid=(S // tr,),
        in_specs=[pl.BlockSpec((tr, W), lambda i: (i, col)), pl.BlockSpec((1, W), lambda i: (0, 0))],
        out_specs=[pl.BlockSpec((tr, W), lambda i: (i, 0)), pl.BlockSpec((tr, 1), lambda i: (i, 0))],
        out_shape=[jax.ShapeDtypeStruct((S, W), BF16), jax.ShapeDtypeStruct((S, 1), F32)],
        compiler_params=_params(("parallel",)),
    )(x, g)


def _rmsnorm_bwd(dy, x, rstd, g, *, name, res=None, into=None, deps=(), width=None, col=0, tr=256):
    S = x.shape[0]
    W = x.shape[1] if width is None else width
    tr = _tile(S, tr)
    has_res = res is not None

    def body(*refs):
        dy_ref, x_ref, r_ref, g_ref = refs[:4]
        dx_ref, dg_ref = refs[-2], refs[-1]
        rstd_v = r_ref[...]
        xhat = x_ref[...] * rstd_v
        dyv = dy_ref[...].astype(F32)
        dyg = dyv * g_ref[...]
        dx = rstd_v * (dyg - xhat * jnp.mean(dyg * xhat, axis=-1, keepdims=True))
        if has_res:
            dx = dx + refs[4][...]
        dx_ref[...] = dx.astype(dx_ref.dtype)
        part = jnp.sum(dyv * xhat, axis=0, keepdims=True)

        @pl.when(pl.program_id(0) == 0)
        def _():
            dg_ref[...] = part

        @pl.when(pl.program_id(0) > 0)
        def _():
            dg_ref[...] += part

    row = pl.BlockSpec((tr, W), lambda i: (i, 0))
    ins = [dy, x, rstd, g] + ([res] if has_res else [])
    in_specs = [row, pl.BlockSpec((tr, W), lambda i: (i, col)), pl.BlockSpec((tr, 1), lambda i: (i, 0)),
                pl.BlockSpec((1, W), lambda i: (0, 0))] + ([row] if has_res else [])
    if into is None:
        dx_spec, dx_shape, alias = row, jax.ShapeDtypeStruct((S, W), F32), {}
    else:
        buf, col_out = into
        ins.append(buf)
        in_specs.append(_ANY)
        dx_spec = pl.BlockSpec((tr, W), lambda i: (i, col_out))
        dx_shape = jax.ShapeDtypeStruct(buf.shape, buf.dtype)
        alias = {len(ins) - 1: 0}
    ins += list(deps)
    in_specs += [_ANY] * len(deps)
    return _pallas(
        body, name=name, grid=(S // tr,), in_specs=in_specs,
        out_specs=[dx_spec, pl.BlockSpec((1, W), lambda i: (0, 0))],
        out_shape=[dx_shape, jax.ShapeDtypeStruct((1, W), F32)],
        input_output_aliases=alias,
        compiler_params=_params(("arbitrary",)),
    )(*ins)


def _final_loss(h2, g, target, *, tr=256):
    S, D = h2.shape
    tr = _tile(S, tr)

    def body(h_ref, g_ref, t_ref, loss_ref, dh_ref, dg_ref):
        hv = h_ref[...]
        rstd = lax.rsqrt(jnp.mean(hv * hv, axis=-1, keepdims=True) + EPS)
        xhat = hv * rstd
        e = xhat * g_ref[...] - t_ref[...]
        lpart = (0.5 / D) * jnp.sum(jnp.sum(e * e, axis=-1, keepdims=True), axis=0, keepdims=True)
        dy = e * (1.0 / D)
        dyg = dy * g_ref[...]
        dh_ref[...] = rstd * (dyg - xhat * jnp.mean(dyg * xhat, axis=-1, keepdims=True))
        gpart = jnp.sum(dy * xhat, axis=0, keepdims=True)

        @pl.when(pl.program_id(0) == 0)
        def _():
            loss_ref[...] = lpart
            dg_ref[...] = gpart

        @pl.when(pl.program_id(0) > 0)
        def _():
            loss_ref[...] += lpart
            dg_ref[...] += gpart

    row = pl.BlockSpec((tr, D), lambda i: (i, 0))
    vec = pl.BlockSpec((1, D), lambda i: (0, 0))
    return _pallas(
        body, name="final_loss", grid=(S // tr,), in_specs=[row, vec, row],
        out_specs=[pl.BlockSpec((1, 1), lambda i: (0, 0)), row, vec],
        out_shape=[jax.ShapeDtypeStruct((1, 1), F32), jax.ShapeDtypeStruct((S, D), F32),
                   jax.ShapeDtypeStruct((1, D), F32)],
        compiler_params=_params(("arbitrary",)),
    )(h2, g, target)


def _sigmoid(v):
    return 1.0 / (1.0 + jnp.exp(-v))


def _merge_fwd(proj, y_ret, y_mla, D, off_gret, off_gmla, *, tr=256, tc=1024):
    S = y_ret.shape[0]
    tr, tc = _tile(S, tr), _tile(D, tc)
    b_ret, b_mla = off_gret // tc, off_gmla // tc

    def body(gr_ref, gm_ref, yr_ref, ym_ref, o_ref):
        o_ref[...] = (_sigmoid(gr_ref[...]) * yr_ref[...] + _sigmoid(gm_ref[...]) * ym_ref[...]).astype(BF16)

    blk = pl.BlockSpec((tr, tc), lambda i, j: (i, j))
    return _pallas(
        body, name="merge_fwd", grid=(S // tr, D // tc),
        in_specs=[pl.BlockSpec((tr, tc), lambda i, j: (i, b_ret + j)),
                  pl.BlockSpec((tr, tc), lambda i, j: (i, b_mla + j)), blk, blk],
        out_specs=blk, out_shape=jax.ShapeDtypeStruct((S, D), BF16),
        compiler_params=_params(("parallel", "parallel")),
    )(proj, proj, y_ret, y_mla)


def _merge_bwd(dmerged, proj, y_ret, y_mla, D, off_gret, *, tr=256):
    S = y_ret.shape[0]
    tr = _tile(S, tr)
    b0 = off_gret // D

    def body(dm_ref, g_ref, yr_ref, ym_ref, dp_ref, dyr_ref, dym_ref):
        dm = dm_ref[...]
        sg = _sigmoid(g_ref[...])

        @pl.when(pl.program_id(1) == 0)
        def _():
            dyr_ref[...] = (dm * sg).astype(BF16)
            dp_ref[...] = (dm * yr_ref[...] * sg * (1.0 - sg)).astype(BF16)

        @pl.when(pl.program_id(1) == 1)
        def _():
            dym_ref[...] = (dm * sg).astype(BF16)
            dp_ref[...] = (dm * ym_ref[...] * sg * (1.0 - sg)).astype(BF16)

    blk = pl.BlockSpec((tr, D), lambda i, j: (i, 0))
    return _pallas(
        body, name="merge_bwd", grid=(S // tr, 2),
        in_specs=[blk, pl.BlockSpec((tr, D), lambda i, j: (i, b0 + j)), blk, blk],
        out_specs=[pl.BlockSpec((tr, D), lambda i, j: (i, b0 + j)), blk, blk],
        out_shape=[jax.ShapeDtypeStruct(proj.shape, BF16), jax.ShapeDtypeStruct((S, D), BF16),
                   jax.ShapeDtypeStruct((S, D), BF16)],
        compiler_params=_params(("parallel", "arbitrary")),
    )(dmerged, proj, y_ret, y_mla)


def _rope128(t, cos_full, sin_signed):
    return t * cos_full + pltpu.roll(t, RET_QK // 2, 1) * sin_signed


def _rope128_t(d, cos_full, sin_signed):
    return d * cos_full + pltpu.roll(d * sin_signed, RET_QK // 2, 1)


def _ret_consts(lg, T):
    pos = lax.broadcasted_iota(jnp.int32, (T, 1), 0).astype(F32)
    qd = jnp.exp(lg * (pos + 1.0))
    kd = jnp.exp(lg * (T - 1.0 - pos))
    n = lax.broadcasted_iota(jnp.int32, (T, T), 0)
    m = lax.broadcasted_iota(jnp.int32, (T, T), 1)
    vis = (m // CHUNK) <= (n // CHUNK)
    dist = jnp.abs(n - m).astype(F32)
    decay = jnp.where(vis, jnp.exp(lg * dist), 0.0)
    cdec = jnp.exp(lg * float(T))
    return qd, kd, decay, cdec


def _dot(a, b, dims):
    return lax.dot_general(a.astype(BF16), b.astype(BF16), (dims, ((), ())), preferred_element_type=F32)


NN = ((1,), (0,))
NT = ((1,), (1,))
TN = ((0,), (0,))
_RQ = slice(0, RET_QK)
_RK = slice(RET_QK, 2 * RET_QK)
_RV = slice(2 * RET_QK, 2 * RET_QK + RET_V)
_RG = slice(2 * RET_QK + RET_V, RET_HEAD_COLS)


RET_GROUP = 4


def _head_cols(h, part):
    return slice(h * RET_HEAD_COLS + part.start, h * RET_HEAD_COLS + part.stop)


def _ret_fwd(proj, cosr, sinr, lgam, gain, RH, *, T):
    S = proj.shape[0]
    nb = S // T
    G = _tile(RH, RET_GROUP)
    heads = range(G)
    scale = RET_QK ** -0.5

    def body(p_ref, cos_ref, sin_ref, lg_ref, gain_ref, ry_ref, gated_ref, st_ref, state):
        b = pl.program_id(1)

        @pl.when(b == 0)
        def _():
            state[...] = jnp.zeros_like(state)

        consts = [_ret_consts(lg_ref[h, 0:1, 0:1], T) for h in heads]
        cosv, sinv = cos_ref[...], sin_ref[...]
        q = [_rope128(p_ref[:, _head_cols(h, _RQ)], cosv, sinv) for h in heads]
        k = [_rope128(p_ref[:, _head_cols(h, _RK)], cosv, sinv) * scale for h in heads]
        v = [p_ref[:, _head_cols(h, _RV)] for h in heads]
        sprev = [state[h] for h in heads]
        for h in heads:
            st_ref[h] = sprev[h]
        a = [_dot(q[h], k[h], NT) for h in heads]
        qs = [_dot(q[h] * consts[h][0], sprev[h], NN) for h in heads]
        kv = [_dot(k[h] * consts[h][1], v[h], TN) for h in heads]
        o = [_dot(a[h] * consts[h][2], v[h], NN) + qs[h] for h in heads]
        for h in heads:
            state[h] = sprev[h] * consts[h][3] + kv[h]
            vals = slice(h * RET_V, (h + 1) * RET_V)
            ry_ref[:, vals] = o[h]
            mu = jnp.mean(o[h], axis=-1, keepdims=True)
            oc = o[h] - mu
            var = jnp.mean(oc * oc, axis=-1, keepdims=True)
            t = oc * lax.rsqrt(var + EPS) * gain_ref[:, vals]
            gv = p_ref[:, _head_cols(h, _RG)]
            gated_ref[:, vals] = (t * (gv * _sigmoid(gv))).astype(BF16)

    return _pallas(
        body, name="ret_fwd", grid=(RH // G, nb),
        in_specs=[pl.BlockSpec((T, G * RET_HEAD_COLS), lambda h, b: (b, h)),
                  pl.BlockSpec((T, RET_QK), lambda h, b: (b, 0)),
                  pl.BlockSpec((T, RET_QK), lambda h, b: (b, 0)),
                  pl.BlockSpec((G, 8, LANES), lambda h, b: (h, 0, 0)),
                  pl.BlockSpec((1, G * RET_V), lambda h, b: (0, h))],
        out_specs=[pl.BlockSpec((T, G * RET_V), lambda h, b: (b, h)),
                   pl.BlockSpec((T, G * RET_V), lambda h, b: (b, h)),
                   pl.BlockSpec((G, None, RET_QK, RET_V), lambda h, b: (h, b, 0, 0))],
        out_shape=[jax.ShapeDtypeStruct((S, RH * RET_V), F32), jax.ShapeDtypeStruct((S, RH * RET_V), BF16),
                   jax.ShapeDtypeStruct((RH, nb, RET_QK, RET_V), F32)],
        scratch_shapes=[pltpu.VMEM((G, RET_QK, RET_V), F32)],
        compiler_params=_params(("parallel", "arbitrary")),
    )(proj, cosr, sinr, lgam, gain)


def _ret_bwd(proj, cosr, sinr, lgam, gain, ry, dgated, states, dproj, RH, *, T):
    S = proj.shape[0]
    nb = S // T
    G = _tile(RH, RET_GROUP)
    heads = range(G)
    scale = RET_QK ** -0.5

    def body(p_ref, cos_ref, sin_ref, lg_ref, gain_ref, ry_ref, dg_ref, st_ref, _, dp_ref, dgain_ref, dstate):
        b = pl.program_id(1)

        @pl.when(b == 0)
        def _():
            dstate[...] = jnp.zeros_like(dstate)

        consts = [_ret_consts(lg_ref[h, 0:1, 0:1], T) for h in heads]
        qd, kd, decay, cdec = [[c[i] for c in consts] for i in range(4)]
        cosv, sinv = cos_ref[...], sin_ref[...]
        q = [_rope128(p_ref[:, _head_cols(h, _RQ)], cosv, sinv) for h in heads]
        k = [_rope128(p_ref[:, _head_cols(h, _RK)], cosv, sinv) * scale for h in heads]
        v = [p_ref[:, _head_cols(h, _RV)] for h in heads]
        sprev = [st_ref[h] for h in heads]
        ds_new = [dstate[h] for h in heads]
        a = [_dot(q[h], k[h], NT) for h in heads]
        do, gparts = [], []
        for h in heads:
            vals = slice(h * RET_V, (h + 1) * RET_V)
            o = ry_ref[:, vals]
            mu = jnp.mean(o, axis=-1, keepdims=True)
            oc = o - mu
            rstd = lax.rsqrt(jnp.mean(oc * oc, axis=-1, keepdims=True) + EPS)
            ryn = oc * rstd
            gainv = gain_ref[:, vals]
            gv = p_ref[:, _head_cols(h, _RG)]
            sg = _sigmoid(gv)
            dgt = dg_ref[:, vals]
            dt = dgt * (gv * sg)
            dp_ref[:, _head_cols(h, _RG)] = (dgt * (ryn * gainv) * (sg * (1.0 + gv * (1.0 - sg)))).astype(BF16)
            gparts.append(jnp.sum(dt * ryn, axis=0, keepdims=True))
            dryn = dt * gainv
            do.append(rstd * (dryn - jnp.mean(dryn, axis=-1, keepdims=True)
                              - ryn * jnp.mean(dryn * ryn, axis=-1, keepdims=True)))
        gpart = jnp.concatenate(gparts, axis=1)

        @pl.when(b == 0)
        def _():
            dgain_ref[...] = gpart

        @pl.when(b > 0)
        def _():
            dgain_ref[...] += gpart

        dpm = [_dot(do[h], v[h], NT) for h in heads]
        dq_s = [_dot(do[h], sprev[h], NT) for h in heads]
        dk_s = [_dot(v[h], ds_new[h], NT) for h in heads]
        dv_s = [_dot(k[h] * kd[h], ds_new[h], NN) for h in heads]
        dst = [_dot(q[h] * qd[h], do[h], TN) for h in heads]
        a = [a[h] * decay[h] for h in heads]
        dpm = [dpm[h] * decay[h] for h in heads]
        dv = [_dot(a[h], do[h], TN) + dv_s[h] for h in heads]
        dq = [_dot(dpm[h], k[h], NN) + dq_s[h] * qd[h] for h in heads]
        dk = [(_dot(dpm[h], q[h], TN) + dk_s[h] * kd[h]) * scale for h in heads]
        for h in heads:
            dstate[h] = ds_new[h] * cdec[h] + dst[h]
            dp_ref[:, _head_cols(h, _RV)] = dv[h].astype(BF16)
            dp_ref[:, _head_cols(h, _RQ)] = _rope128_t(dq[h], cosv, sinv).astype(BF16)
            dp_ref[:, _head_cols(h, _RK)] = _rope128_t(dk[h], cosv, sinv).astype(BF16)

    rb = lambda b: nb - 1 - b
    return _pallas(
        body, name="ret_bwd", grid=(RH // G, nb),
        in_specs=[pl.BlockSpec((T, G * RET_HEAD_COLS), lambda h, b: (rb(b), h)),
                  pl.BlockSpec((T, RET_QK), lambda h, b: (rb(b), 0)),
                  pl.BlockSpec((T, RET_QK), lambda h, b: (rb(b), 0)),
                  pl.BlockSpec((G, 8, LANES), lambda h, b: (h, 0, 0)),
                  pl.BlockSpec((1, G * RET_V), lambda h, b: (0, h)),
                  pl.BlockSpec((T, G * RET_V), lambda h, b: (rb(b), h)),
                  pl.BlockSpec((T, G * RET_V), lambda h, b: (rb(b), h)),
                  pl.BlockSpec((G, None, RET_QK, RET_V), lambda h, b: (h, rb(b), 0, 0)),
                  _ANY],
        out_specs=[pl.BlockSpec((T, G * RET_HEAD_COLS), lambda h, b: (rb(b), h)),
                   pl.BlockSpec((1, G * RET_V), lambda h, b: (0, h))],
        out_shape=[jax.ShapeDtypeStruct(dproj.shape, dproj.dtype), jax.ShapeDtypeStruct((1, RH * RET_V), F32)],
        scratch_shapes=[pltpu.VMEM((G, RET_QK, RET_V), F32)],
        input_output_aliases={8: 0},
        compiler_params=_params(("parallel", "arbitrary")),
    )(proj, cosr, sinr, lgam, gain, ry, dgated, states, dproj)


def _rope_pe(t, c, s1, s2):
    return t * c + pltpu.roll(t, LANES - QK_ROPE // 2, 1) * s1 + pltpu.roll(t, QK_ROPE // 2, 1) * s2


def _rope_pe_t(d, c, s1, s2):
    return d * c + pltpu.roll(d * s1, QK_ROPE // 2, 1) + pltpu.roll(d * s2, LANES - QK_ROPE // 2, 1)


ATTN_C2 = (QK_NOPE + QK_ROPE) ** -0.5 * LOG2E


def _qkv_proj(cqn, ckvn, wq, wkv, kpe, tabs, MH, *, tm=512, heads=4):
    S = cqn.shape[0]
    tm = _tile(S, tm)
    hb = _tile(MH, heads)
    W = 2 * LANES
    c_t, s1_t, s2_t = tabs

    def body(cq_ref, ckv_ref, wq_ref, wkv_ref, kpe_ref, c_ref, s1_ref, s2_ref, qf_ref, kf_ref, v_ref):
        c, s1, s2 = c_ref[...], s1_ref[...], s2_ref[...]
        q = _dot(cq_ref[...], wq_ref[...], NN)
        kv = _dot(ckv_ref[...], wkv_ref[...], NN)
        kper = _rope_pe(kpe_ref[...], c, s1, s2).astype(BF16)
        for h in range(hb):
            lo, mid, hi = h * W, h * W + QK_NOPE, (h + 1) * W
            qf_ref[:, lo:mid] = (q[:, lo:mid] * ATTN_C2).astype(BF16)
            qf_ref[:, mid:hi] = (_rope_pe(q[:, mid:hi], c, s1, s2) * ATTN_C2).astype(BF16)
            kf_ref[:, lo:mid] = kv[:, lo:mid].astype(BF16)
            kf_ref[:, mid:hi] = kper
            v_ref[:, h * V_HEAD:(h + 1) * V_HEAD] = kv[:, mid:hi].astype(BF16)

    tab = pl.BlockSpec((tm, LANES), lambda i, j: (i, 0))
    grp = pl.BlockSpec((tm, hb * W), lambda i, j: (i, j))
    return _pallas(
        body, name="qkv_proj", grid=(S // tm, MH // hb),
        in_specs=[pl.BlockSpec((tm, cqn.shape[1]), lambda i, j: (i, 0)),
                  pl.BlockSpec((tm, ckvn.shape[1]), lambda i, j: (i, 0)),
                  pl.BlockSpec((wq.shape[0], hb * W), lambda i, j: (0, j)),
                  pl.BlockSpec((wkv.shape[0], hb * W), lambda i, j: (0, j)), tab, tab, tab, tab],
        out_specs=[grp, grp, pl.BlockSpec((tm, hb * V_HEAD), lambda i, j: (i, j))],
        out_shape=[jax.ShapeDtypeStruct((S, MH * W), BF16)] * 2 + [jax.ShapeDtypeStruct((S, MH * V_HEAD), BF16)],
        compiler_params=_params(("parallel", "parallel")),
    )(cqn, ckvn, wq, wkv, kpe, c_t, s1_t, s2_t)


def _chunk_mask(T):
    n = lax.broadcasted_iota(jnp.int32, (T, T), 0)
    m = lax.broadcasted_iota(jnp.int32, (T, T), 1)
    return (m // CHUNK) <= (n // CHUNK)


def _lanes_to(v, width):
    return jnp.tile(v, (1, width // LANES))


def _attn_fwd(qf, kf, vb, MH, *, T):
    S = qf.shape[0]
    nt = S // T

    def body(q_ref, k_ref, v_ref, o_ref, lse_ref, m_sc, l_sc, acc_sc, s_a, s_b):
        qi = pl.program_id(1)
        m_sc[...] = jnp.full_like(m_sc, NEG)
        l_sc[...] = jnp.zeros_like(l_sc)
        acc_sc[...] = jnp.zeros_like(acc_sc)

        def rows_of(kt):
            return pl.ds(pl.multiple_of(kt * T, T), T)

        def scores(kt):
            return _dot(q_ref[...], k_ref[rows_of(kt), :], NT)

        def update(s, kt):
            m_prev = m_sc[...]
            m_new = jnp.maximum(m_prev, jnp.max(s, axis=-1, keepdims=True))
            alpha = jnp.exp2(m_prev - m_new)
            p = jnp.exp2(s - _lanes_to(m_new, T))
            l_sc[...] = alpha * l_sc[...] + jnp.sum(p, axis=-1, keepdims=True)
            acc_sc[...] = alpha * acc_sc[...] + _dot(p, v_ref[rows_of(kt), :], NN)
            m_sc[...] = m_new

        update(jnp.where(_chunk_mask(T), scores(qi), NEG), qi)

        @pl.when(qi % 2 == 1)
        def _():
            update(scores(qi - 1), qi - 1)

        pairs = qi // 2

        @pl.when(pairs > 0)
        def _():
            s_a[...] = scores(0)

        def pair(j, carry):
            t0 = 2 * j
            s_b[...] = scores(t0 + 1)
            update(s_a[...], t0)
            s_a[...] = scores(jnp.minimum(t0 + 2, 2 * pairs - 1))
            update(s_b[...], t0 + 1)
            return carry

        lax.fori_loop(0, pairs, pair, 0)
        l = l_sc[...]
        o_ref[...] = acc_sc[...] / l
        lse_ref[...] = m_sc[...] + jnp.log(l) * LOG2E

    return _pallas(
        body, name="attn_fwd", grid=(MH, nt),
        in_specs=[pl.BlockSpec((T, 2 * LANES), lambda h, i: (i, h)),
                  pl.BlockSpec((S, 2 * LANES), lambda h, i: (0, h)),
                  pl.BlockSpec((S, LANES), lambda h, i: (0, h))],
        out_specs=[pl.BlockSpec((T, LANES), lambda h, i: (i, h)),
                   pl.BlockSpec((None, T, LANES), lambda h, i: (h, i, 0))],
        out_shape=[jax.ShapeDtypeStruct((S, MH * LANES), F32), jax.ShapeDtypeStruct((MH, S, LANES), F32)],
        scratch_shapes=[pltpu.VMEM((T, LANES), F32), pltpu.VMEM((T, LANES), F32), pltpu.VMEM((T, LANES), F32),
                        pltpu.VMEM((T, T), F32), pltpu.VMEM((T, T), F32)],
        compiler_params=_params(("parallel", "parallel")),
    )(qf, kf, vb)


def _attn_bwd(qf, kf, vb, dob, lse2, delta, tabs, MH, *, T, deps=()):
    S = qf.shape[0]
    nt = S // T
    scale = (QK_NOPE + QK_ROPE) ** -0.5
    n_dep = len(deps)

    def body(q_ref, k_ref, v_ref, do_ref, lse_ref, dl_ref, c_ref, s1_ref, s2_ref, *rest):
        dqa_ref, dkv_ref, dkpe_ref, dq_ref, dk_sc, dv_sc, s_a, dp_a, s_b, dp_b = rest[n_dep:]
        kj = pl.program_id(1)

        @pl.when(kj == 0)
        def _():
            dq_ref[...] = jnp.zeros_like(dq_ref)

        dk_sc[...] = jnp.zeros_like(dk_sc)
        dv_sc[...] = jnp.zeros_like(dv_sc)

        def rows_of(qt):
            return pl.ds(pl.multiple_of(qt * T, T), T)

        def products(qt):
            rows = rows_of(qt)
            return _dot(q_ref[rows, :], k_ref[...], NT), _dot(do_ref[rows, :], v_ref[...], NT)

        def update(s, dp, qt):
            rows = rows_of(qt)
            q, dov = q_ref[rows, :], do_ref[rows, :]
            p = jnp.exp2(s - _lanes_to(lse_ref[rows, :], T))
            ds = p * (dp - _lanes_to(dl_ref[rows, :], T))
            dv_sc[...] += _dot(p, dov, TN)
            dk_sc[...] += _dot(ds, q, TN)
            dq_ref[rows, :] += _dot(ds, k_ref[...], NN)

        s, dp = products(kj)
        update(jnp.where(_chunk_mask(T), s, NEG), dp, kj)
        rest = nt - 1 - kj

        @pl.when(rest % 2 == 1)
        def _():
            s1, dp1 = products(nt - 1)
            update(s1, dp1, nt - 1)

        pairs = rest // 2

        @pl.when(pairs > 0)
        def _():
            s_a[...], dp_a[...] = products(kj + 1)

        def pair(j, carry):
            t0 = kj + 1 + 2 * j
            s_b[...], dp_b[...] = products(t0 + 1)
            update(s_a[...], dp_a[...], t0)
            s_a[...], dp_a[...] = products(jnp.minimum(t0 + 2, kj + 2 * pairs))
            update(s_b[...], dp_b[...], t0 + 1)
            return carry

        lax.fori_loop(0, pairs, pair, 0)
        dkv_ref[:, :QK_NOPE] = (dk_sc[:, :QK_NOPE] * (1.0 / LOG2E)).astype(BF16)
        dkv_ref[:, QK_NOPE:] = dv_sc[...].astype(BF16)
        dkpe_ref[...] = dk_sc[:, QK_NOPE:] * (1.0 / LOG2E)

        @pl.when(kj == nt - 1)
        def _():
            dqa_ref[:, :QK_NOPE] = (dq_ref[:, :QK_NOPE] * scale).astype(BF16)
            dqa_ref[:, QK_NOPE:] = (_rope_pe_t(dq_ref[:, QK_NOPE:], c_ref[...], s1_ref[...], s2_ref[...])
                                    * scale).astype(BF16)

    stat = pl.BlockSpec((None, S, LANES), lambda h, j: (h, 0, 0))
    tab = pl.BlockSpec((S, LANES), lambda h, j: (0, 0))
    return _pallas(
        body, name="attn_bwd", grid=(MH, nt),
        in_specs=[pl.BlockSpec((S, 2 * LANES), lambda h, j: (0, h)),
                  pl.BlockSpec((T, 2 * LANES), lambda h, j: (j, h)),
                  pl.BlockSpec((T, LANES), lambda h, j: (j, h)),
                  pl.BlockSpec((S, LANES), lambda h, j: (0, h)), stat, stat, tab, tab, tab] + [_ANY] * n_dep,
        out_specs=[pl.BlockSpec((S, 2 * LANES), lambda h, j: (0, h)),
                   pl.BlockSpec((T, 2 * LANES), lambda h, j: (j, h)),
                   pl.BlockSpec((T, LANES), lambda h, j: (j, h))],
        out_shape=[jax.ShapeDtypeStruct((S, MH * 2 * LANES), BF16), jax.ShapeDtypeStruct((S, MH * 2 * LANES), BF16),
                   jax.ShapeDtypeStruct((S, MH * LANES), F32)],
        scratch_shapes=[pltpu.VMEM((S, 2 * LANES), F32), pltpu.VMEM((T, 2 * LANES), F32), pltpu.VMEM((T, LANES), F32)]
        + [pltpu.VMEM((T, T), F32)] * 4,
        compiler_params=_params(("parallel", "arbitrary")),
    )(qf, kf, vb, dob, lse2, delta, *tabs, *deps)


def _kpe_sum(dkpe_h, tabs, MH, *, tr=256):
    S = dkpe_h.shape[0]
    tr = _tile(S, tr)

    def body(dk_ref, c_ref, s1_ref, s2_ref, dkpe_ref):
        tot = dk_ref[:, :LANES]
        for h in range(1, MH):
            tot = tot + dk_ref[:, h * LANES:(h + 1) * LANES]
        dkpe_ref[...] = _rope_pe_t(tot, c_ref[...], s1_ref[...], s2_ref[...]).astype(BF16)

    tab = pl.BlockSpec((tr, LANES), lambda i: (i, 0))
    return _pallas(
        body, name="kpe_sum", grid=(S // tr,),
        in_specs=[pl.BlockSpec((tr, MH * LANES), lambda i: (i, 0)), tab, tab, tab],
        out_specs=tab, out_shape=jax.ShapeDtypeStruct((S, LANES), BF16),
        compiler_params=_params(("parallel",)),
    )(dkpe_h, *tabs)


ROW_ALIGN = 16


def _blk(R, C, block_bytes=2 << 20):
    cap = max(ROW_ALIGN, block_bytes // (C * 4))
    for t in range(min(R, cap) // ROW_ALIGN * ROW_ALIGN, LANES - 1, -ROW_ALIGN):
        if R % t == 0:
            return t, C
    if R <= cap:
        return R, C
    tc = C
    while R * tc * 4 > block_bytes and tc % (2 * LANES) == 0:
        tc //= 2
    return R, tc


def _rows_call(fn, ins, out_dtypes, *, name):
    R, C = ins[0].shape
    tr, tc = _blk(R, C)
    n_in = len(ins)

    def body(*refs):
        vals = fn(*[r[...] for r in refs[:n_in]])
        for r, v in zip(refs[n_in:], vals):
            r[...] = v.astype(r.dtype)

    blk = pl.BlockSpec((tr, tc), lambda i, j: (i, j))
    res = _pallas(
        body, name=name, grid=(R // tr, C // tc), in_specs=[blk] * n_in, out_specs=[blk] * len(out_dtypes),
        out_shape=[jax.ShapeDtypeStruct((R, C), d) for d in out_dtypes],
        compiler_params=_params(("parallel", "parallel")),
    )(*ins)
    return res


def _adamw_vals(w, g, m, v):
    m = ADAM_B1 * m + (1.0 - ADAM_B1) * g
    v = ADAM_B2 * v + (1.0 - ADAM_B2) * (g * g)
    m_hat = m / (1.0 - ADAM_B1 ** ADAM_STEP)
    v_hat = v / (1.0 - ADAM_B2 ** ADAM_STEP)
    delta = -ADAM_LR * (m_hat / (jnp.sqrt(v_hat) + ADAM_EPS) + ADAM_WD * w)
    return delta, m, v


def _sum_pair(p, theirs, place, *, name):
    _, R, C = p.shape
    R2 = R // 2
    tr, tc = _blk(R2, C)
    p4 = p.reshape(N_CHIPS, 2, R2, C)

    def body(place_ref, a_ref, b_ref, o_ref):
        o_ref[...] = (a_ref[...].astype(F32) + b_ref[...].astype(F32)).astype(BF16)

    spec = pltpu.PrefetchScalarGridSpec(
        num_scalar_prefetch=1, grid=(N_CHIPS, R2 // tr, C // tc),
        in_specs=[pl.BlockSpec((None, None, tr, tc), lambda q, i, j, pr: (q, pr[0], i, j)),
                  pl.BlockSpec((None, tr, tc), lambda q, i, j, pr: (q, i, j))],
        out_specs=pl.BlockSpec((None, tr, tc), lambda q, i, j, pr: (q, i, j)))
    return _pallas(body, name=name, grid_spec=spec, out_shape=jax.ShapeDtypeStruct((N_CHIPS, R2, C), BF16),
                   compiler_params=_params(("parallel", "parallel", "parallel")))(place, p4, theirs)


def _sum_chips(p, theirs, recv, place, *, name):
    _, R, C = p.shape
    R2 = R // 2
    tr, tc = _blk(R2, C)
    p4 = p.reshape(N_CHIPS, 2, R2, C)

    def body(place_ref, a_ref, b_ref, r0_ref, r1_ref, r2_ref, o_ref):
        own = a_ref[...].astype(F32) + b_ref[...].astype(F32)
        o_ref[...] = ((own + r0_ref[...].astype(F32)) + r1_ref[...].astype(F32)) + r2_ref[...].astype(F32)

    def slot(k):
        return pl.BlockSpec((None, tr, tc), lambda i, j, pr: (k, i, j))

    spec = pltpu.PrefetchScalarGridSpec(
        num_scalar_prefetch=1, grid=(R2 // tr, C // tc),
        in_specs=[pl.BlockSpec((None, None, tr, tc), lambda i, j, pr: (pr[1], pr[0], i, j)),
                  pl.BlockSpec((None, tr, tc), lambda i, j, pr: (pr[1], i, j)), slot(0), slot(1), slot(2)],
        out_specs=pl.BlockSpec((None, tr, tc), lambda i, j, pr: (pr[0], i, j)))
    return _pallas(body, name=name, grid_spec=spec, out_shape=jax.ShapeDtypeStruct((2, R2, C), F32),
                   compiler_params=_params(("parallel", "parallel")))(place, p4, theirs, recv, recv, recv)


def _me():
    return lax.axis_index("x"), lax.axis_index("y"), lax.axis_index("c")


def _other_chips(x, y):
    return [(1 - x, y), (x, 1 - y), (1 - x, 1 - y)]


def _rcopy(src, dst, ssem, rsem, dev):
    return pltpu.make_async_remote_copy(src_ref=src, dst_ref=dst, send_sem=ssem, recv_sem=rsem,
                                        device_id=dev, device_id_type=MESH)


def _cast_into_slot(w, place, *, name, rows=None, deps=()):
    R, C = w.shape
    rows = R if rows is None else rows
    tr, tc = _blk(R, C)

    def body(place_ref, w_ref, *rest):
        rest[-1][...] = w_ref[...].astype(BF16)

    spec = pltpu.PrefetchScalarGridSpec(
        num_scalar_prefetch=1, grid=(R // tr, C // tc),
        in_specs=[pl.BlockSpec((tr, tc), lambda i, j, pr: (i, j))] + [_ANY] * len(deps),
        out_specs=pl.BlockSpec((None, tr, tc), lambda i, j, pr: (pr[1], i, j)))
    out = _pallas(body, name=name, grid_spec=spec, out_shape=jax.ShapeDtypeStruct((N_CHIPS, rows, C), BF16),
                  compiler_params=_params(("parallel", "parallel")))(place, w, *deps)
    return out.reshape(N_CHIPS, 2, rows // 2, C)


def _gather_ici_plan(bufs):
    x, y, c = _me()
    j = 2 * x + y
    plan = []
    for i, buf in enumerate(bufs):
        for k, (px, py) in enumerate(_other_chips(x, y)):
            plan.append((3 * i + k, buf.at[j, c], buf.at[j, c], (px, py, c)))
    return plan


def _gather_weights(bufs, *, name):
    n = len(bufs)

    def body(*refs):
        outs = refs[n:2 * n]
        ssem, rsem, fssem, frsem = refs[2 * n:]
        x, y, c = _me()
        sib = (x, y, 1 - c)
        chips = _other_chips(x, y)
        remote = []
        for s, src, dst, dev in _gather_ici_plan(outs):
            r = _rcopy(src, dst, ssem.at[s], rsem.at[s], dev)
            r.start()
            remote.append(r)
        for i in range(n):
            for k, (px, py) in enumerate(chips):
                slot = outs[i].at[2 * px + py, c]
                _rcopy(slot, slot, ssem.at[3 * i + k], rsem.at[3 * i + k], (px, py, c)).wait_recv()
                f = _rcopy(slot, slot, fssem.at[3 * i + k], frsem.at[3 * i + k], sib)
                f.start()
                remote.append(f)
        for i in range(n):
            for k, (px, py) in enumerate(chips):
                slot = outs[i].at[2 * px + py, 1 - c]
                _rcopy(slot, slot, fssem.at[3 * i + k], frsem.at[3 * i + k], sib).wait_recv()
        for r in remote:
            r.wait_send()

    return _pallas(
        body, name=name, in_specs=[_ANY] * n, out_specs=[_ANY] * n,
        out_shape=[jax.ShapeDtypeStruct(b.shape, b.dtype) for b in bufs],
        scratch_shapes=[pltpu.SemaphoreType.DMA((3 * n,))] * 4,
        input_output_aliases={i: i for i in range(n)},
        compiler_params=pltpu.CompilerParams(has_side_effects=True),
    )(*bufs)


def _forward_halves(bufs, *, name):
    n = len(bufs)

    def body(*refs):
        outs = refs[n:2 * n]
        ssem, rsem = refs[2 * n:]
        x, y, c = _me()
        sib = (x, y, 1 - c)
        cps = []
        for i in range(n):
            for k, (px, py) in enumerate(_other_chips(x, y)):
                slot = outs[i].at[2 * px + py, c]
                r = _rcopy(slot, slot, ssem.at[3 * i + k], rsem.at[3 * i + k], sib)
                r.start()
                cps.append(r)
        for r in cps:
            r.wait()

    return _pallas(
        body, name=name, in_specs=[_ANY] * n, out_specs=[_ANY] * n,
        out_shape=[jax.ShapeDtypeStruct(b.shape, b.dtype) for b in bufs],
        scratch_shapes=[pltpu.SemaphoreType.DMA((3 * n,))] * 2,
        input_output_aliases={i: i for i in range(n)},
        compiler_params=pltpu.CompilerParams(has_side_effects=True),
    )(*bufs)


_HBM = pl.BlockSpec(memory_space=pltpu.HBM)
_SEM = pl.BlockSpec(memory_space=pltpu.SEMAPHORE)
_EFFECT = pltpu.SideEffectType.DATAFLOW_SIDE_EFFECTING


def _split_start(bufs, plan, n_copies, *, name):
    n = len(bufs)

    def body(*refs):
        ssem, rsem = refs[n], refs[n + 1]
        for s, src, dst, dev in plan(refs[:n]):
            _rcopy(src, dst, ssem.at[s], rsem.at[s], dev).start()
        refs[-1][...] = jnp.zeros_like(refs[-1])

    res = _pallas(
        body, name=name, in_specs=[_HBM] * n,
        out_specs=(_SEM, _SEM, *[_HBM] * n, pl.BlockSpec(memory_space=pltpu.VMEM)),
        out_shape=(pltpu.SemaphoreType.DMA((n_copies,)), pltpu.SemaphoreType.DMA((n_copies,)),
                   *[pltpu.HBM(b.shape, b.dtype) for b in bufs], jax.ShapeDtypeStruct((8, LANES), F32)),
        input_output_aliases={i: 2 + i for i in range(n)},
        compiler_params=pltpu.CompilerParams(has_side_effects=_EFFECT),
    )(*[pltpu.with_memory_space_constraint(b, pltpu.HBM) for b in bufs])
    return res[0], res[1], list(res[2:2 + n]), res[-1]


def _split_wait(ssem, rsem, bufs, after, plan, *, name):
    n = len(bufs)

    def body(*refs):
        ssem_ref, rsem_ref = refs[n], refs[n + 1]
        for s, src, dst, dev in plan(refs[:n]):
            cp = _rcopy(src, dst, ssem_ref.at[s], rsem_ref.at[s], dev)
            cp.wait_send()
            cp.wait_recv()

    return list(_pallas(
        body, name=name, in_specs=[_HBM] * n + [_SEM, _SEM, _ANY], out_specs=[_HBM] * n,
        out_shape=[pltpu.HBM(b.shape, b.dtype) for b in bufs],
        input_output_aliases={i: i for i in range(n)},
        compiler_params=pltpu.CompilerParams(has_side_effects=_EFFECT),
    )(*bufs, ssem, rsem, after))


def _scatter_plan(n):
    def plan(bufs):
        x, y, c = _me()
        out = []
        for i in range(n):
            for k, (px, py) in enumerate(_other_chips(x, y)):
                out.append((3 * i + k, bufs[i].at[2 * px + py], bufs[n + i].at[k], (px, py, c)))
        return out
    return plan


def _swap_halves(grads, *, name):
    n = len(grads)
    views = [g.reshape(N_CHIPS, 2, g.shape[1] // 2, g.shape[2]) for g in grads]

    def body(*refs):
        ins, outs = refs[:n], refs[n:2 * n]
        ssem, rsem = refs[2 * n:]
        x, y, c = _me()
        sib = (x, y, 1 - c)
        cps = []
        for i in range(n):
            r = _rcopy(ins[i].at[:, 1 - c], outs[i], ssem.at[i], rsem.at[i], sib)
            r.start()
            cps.append(r)
        for r in cps:
            r.wait()

    return _pallas(
        body, name=name, in_specs=[_ANY] * n, out_specs=[_ANY] * n,
        out_shape=[jax.ShapeDtypeStruct((N_CHIPS,) + v.shape[2:], v.dtype) for v in views],
        scratch_shapes=[pltpu.SemaphoreType.DMA((n,)), pltpu.SemaphoreType.DMA((n,))],
        compiler_params=pltpu.CompilerParams(has_side_effects=True),
    )(*views)


def _join_halves(halves, *, name):
    n = len(halves)

    def body(*refs):
        outs = refs[n:2 * n]
        ssem, rsem = refs[2 * n:]
        x, y, c = _me()
        sib = (x, y, 1 - c)
        cps = []
        for i in range(n):
            r = _rcopy(outs[i].at[c], outs[i].at[c], ssem.at[i], rsem.at[i], sib)
            r.start()
            cps.append(r)
        for r in cps:
            r.wait()

    return _pallas(
        body, name=name, in_specs=[_ANY] * n, out_specs=[_ANY] * n,
        out_shape=[jax.ShapeDtypeStruct(h.shape, h.dtype) for h in halves],
        scratch_shapes=[pltpu.SemaphoreType.DMA((n,)), pltpu.SemaphoreType.DMA((n,))],
        input_output_aliases={i: i for i in range(n)},
        compiler_params=pltpu.CompilerParams(has_side_effects=True),
    )(*halves)


def _allreduce_small(vec):
    R = vec.shape[0]

    def body(v_ref, o_ref, buf, ssem, rsem):
        x, y, c = _me()
        me = 4 * x + 2 * y + c
        buf[me] = v_ref[...]
        cps = []
        for k in range(1, 8):
            peer = (x ^ (k >> 2), y ^ ((k >> 1) & 1), c ^ (k & 1))
            r = _rcopy(v_ref, buf.at[me], ssem.at[k - 1], rsem.at[k - 1], peer)
            r.start()
            cps.append(r)
        for k in range(1, 8):
            peer = (x ^ (k >> 2), y ^ ((k >> 1) & 1), c ^ (k & 1))
            pid = 4 * peer[0] + 2 * peer[1] + peer[2]
            _rcopy(v_ref, buf.at[pid], ssem.at[k - 1], rsem.at[k - 1], peer).wait_recv()
        for r in cps:
            r.wait_send()
        tot = buf[0]
        for d in range(1, 8):
            tot = tot + buf[d]
        o_ref[...] = tot

    vm = pl.BlockSpec(memory_space=pltpu.VMEM)
    return _pallas(
        body, name="allreduce_small", in_specs=[vm], out_specs=vm,
        out_shape=jax.ShapeDtypeStruct((R, LANES), F32),
        scratch_shapes=[pltpu.VMEM((8, R, LANES), F32), pltpu.SemaphoreType.DMA((7,)), pltpu.SemaphoreType.DMA((7,))],
        compiler_params=pltpu.CompilerParams(has_side_effects=True),
    )(vec)


def _rope_tables(positions, S):
    pos = positions.reshape(S, 1).astype(F32)
    half = RET_QK // 2
    inv = ROPE_THETA ** (-jnp.arange(half, dtype=F32) / half)
    ang = pos * inv
    cosr = jnp.concatenate([jnp.cos(ang), jnp.cos(ang)], axis=1)
    sinr = jnp.concatenate([-jnp.sin(ang), jnp.sin(ang)], axis=1)
    half = QK_ROPE // 2
    inv = ROPE_THETA ** (-jnp.arange(half, dtype=F32) / half)
    ang = pos * inv
    z = jnp.zeros((S, half), F32)
    c = jnp.concatenate([jnp.cos(ang), jnp.cos(ang), z, z], axis=1)
    s1 = jnp.concatenate([-jnp.sin(ang), z, z, z], axis=1)
    s2 = jnp.concatenate([z, jnp.sin(ang), z, z], axis=1)
    return cosr, sinr, (c, s1, s2)


def _cat_cols(g):
    return jnp.concatenate([g[j] for j in range(N_CHIPS)], axis=1)


def _split_cols(w):
    return jnp.stack(jnp.split(w, N_CHIPS, axis=1))


def _pack_small(vs, rows):
    flat = jnp.concatenate([v.reshape(-1) for v in vs])
    flat = jnp.pad(flat, (0, rows * LANES - flat.shape[0]))
    return flat.reshape(rows, LANES)


def kernel(x, positions, norm_mix_g, w_in, ret_norm_g, w_ret_o, q_a_norm_g, w_q_b, kv_a_norm_g, w_kv_b, w_mla_o, w_out, norm_mlp_g, w_up, w_down, norm_f_g, loss_target, m_norm_mix_g, m_w_in, m_ret_norm_g, m_w_ret_o, m_q_a_norm_g, m_w_q_b, m_kv_a_norm_g, m_w_kv_b, m_w_mla_o, m_w_out, m_norm_mlp_g, m_w_up, m_w_down, m_norm_f_g, v_norm_mix_g, v_w_in, v_ret_norm_g, v_w_ret_o, v_q_a_norm_g, v_w_q_b, v_kv_a_norm_g, v_w_kv_b, v_w_mla_o, v_w_out, v_norm_mlp_g, v_w_up, v_w_down, v_norm_f_g):
    S, D = x.shape[1], x.shape[2]
    RVW = w_ret_o.shape[1] * N_CHIPS
    RH = RVW // RET_V
    RQW = RH * RET_QK
    MVW = w_mla_o.shape[1] * N_CHIPS
    MH = MVW // V_HEAD
    QL, KVL = w_q_b.shape[1], w_kv_b.shape[1]
    T_RET = _tile(S, 256)
    T_ATT = _tile(S, 512)

    xs = x.reshape(S, D)
    tgt = loss_target.reshape(S, D)
    cosr, sinr, pe_tabs = _rope_tables(positions, S)
    lgam = jnp.log(1.0 - 2.0 ** (-5.0 - jnp.arange(RH, dtype=F32)))
    lgam = jnp.broadcast_to(lgam[:, None, None], (RH, 8, LANES))

    big = ("w_in", "w_ret_o", "w_q_b", "w_kv_b", "w_mla_o", "w_out", "w_up", "w_down")
    w_sh = dict(w_in=w_in[0].T, w_ret_o=w_ret_o[0], w_q_b=w_q_b[0], w_kv_b=w_kv_b[0], w_mla_o=w_mla_o[0],
                w_out=w_out[0], w_up=w_up[0], w_down=w_down[0])
    m_sh = dict(w_in=m_w_in[0].T, w_ret_o=m_w_ret_o[0], w_q_b=m_w_q_b[0], w_kv_b=m_w_kv_b[0],
                w_mla_o=m_w_mla_o[0], w_out=m_w_out[0], w_up=m_w_up[0], w_down=m_w_down[0])
    v_sh = dict(w_in=v_w_in[0].T, w_ret_o=v_w_ret_o[0], w_q_b=v_w_q_b[0], w_kv_b=v_w_kv_b[0],
                w_mla_o=v_w_mla_o[0], w_out=v_w_out[0], w_up=v_w_up[0], w_down=v_w_down[0])
    col_sharded = ("w_q_b", "w_kv_b", "w_up")
    c_sh = w_in.shape[2]
    c_pad = -(-c_sh // 64) * 64
    place = jnp.stack([lax.axis_index("c"), 2 * lax.axis_index("x") + lax.axis_index("y")]).astype(jnp.int32)

    def whole(k, g):
        g = g.reshape(N_CHIPS, w_sh[k].shape[0], w_sh[k].shape[1])
        if k == "w_up":
            return g
        return _cat_cols(g) if k in col_sharded else g.reshape(-1, g.shape[2])

    first = ("w_in", "w_q_b", "w_kv_b")
    later = ("w_ret_o", "w_mla_o", "w_out", "w_up", "w_down")
    first_bufs = [_cast_into_slot(w_sh[k], place, name="cast_" + k, rows=c_pad if k == "w_in" else None)
                  for k in first]
    first_ssem, first_rsem, first_bufs, first_token = _split_start(
        first_bufs, _gather_ici_plan, 3 * len(first), name="gather_first_start")
    later_bufs = [_cast_into_slot(w_sh[k], place, name="cast_" + k, deps=(first_token,)) for k in later[:-1]]
    first_bufs = _split_wait(first_ssem, first_rsem, first_bufs, later_bufs[-1], _gather_ici_plan,
                             name="gather_first_wait")
    got = _forward_halves(first_bufs, name="gather_first_forward")
    full = {k: whole(k, g) for k, g in zip(first[1:], got[1:])}
    later_bufs.append(_cast_into_slot(w_sh[later[-1]], place, name="cast_" + later[-1], deps=(got[0],)))
    later_ssem, later_rsem, later_bufs, later_token = _split_start(
        later_bufs, _gather_ici_plan, 3 * len(later), name="gather_later_start")

    o_rq, o_rk, o_rv, o_rg = 0, RQW, 2 * RQW, 2 * RQW + RVW
    o_cq = 2 * RQW + 2 * RVW
    o_ckv, o_kpe = o_cq + QL, o_cq + QL + KVL
    o_gr = o_kpe + QK_ROPE
    o_gm = o_gr + D
    n_ret = RH * RET_HEAD_COLS
    off_gret, off_gmla, off_cq, off_ckv = n_ret, n_ret + D, n_ret + 2 * D, n_ret + 2 * D + QL
    n_a = off_ckv + KVL
    runs = []
    for h in range(RH):
        base = h * RET_HEAD_COLS
        runs += [(o_rq + h * RET_QK, RET_QK, base), (o_rk + h * RET_QK, RET_QK, base + RET_QK),
                 (o_rv + h * RET_V, RET_V, base + 2 * RET_QK), (o_rg + h * RET_V, RET_V, base + 2 * RET_QK + RET_V)]
    runs += [(o_gr, D, off_gret), (o_gm, D, off_gmla), (o_cq, QL, off_cq), (o_ckv, KVL, off_ckv),
             (o_kpe, QK_ROPE, n_a)]

    def take(parts, start, width):
        out, lo = [], 0
        for p in parts:
            hi = lo + p.shape[0]
            a, b = max(start, lo), min(start + width, hi)
            if a < b:
                out.append(p[a - lo:b - lo])
            lo = hi
        return out

    wi = [got[0].reshape(N_CHIPS, c_pad, D)[jj, :c_sh] for jj in range(N_CHIPS)]
    here = sorted(runs, key=lambda r: r[2])
    wa = jnp.concatenate([p for s0, w, _ in here[:-1] for p in take(wi, s0, w)], axis=0)
    wkpe = jnp.concatenate(take(wi, o_kpe, QK_ROPE) + [jnp.zeros((LANES - QK_ROPE, D), BF16)], axis=0)
    wq = jnp.pad(full["w_q_b"].reshape(QL, MH, QK_NOPE + QK_ROPE),
                 ((0, 0), (0, 0), (0, LANES - QK_ROPE))).reshape(QL, MH * 2 * LANES)
    wkv = full["w_kv_b"]

    u, rstd0 = _rmsnorm_fwd(xs, norm_mix_g, name="norm_mix")
    proj = _mm(u, wa, mode="nt", outs=[F32], name="in_proj", deps=(later_token,))
    kpe = _mm(u, wkpe, mode="nt", outs=[F32], name="kpe_proj")
    ry, gated, states = _ret_fwd(proj, cosr, sinr, lgam, ret_norm_g, RH, T=T_RET)
    cqn, rstd_q = _rmsnorm_fwd(proj, q_a_norm_g, name="norm_q", width=QL, col=off_cq // QL)
    ckvn, rstd_kv = _rmsnorm_fwd(proj, kv_a_norm_g, name="norm_kv", width=KVL, col=off_ckv // KVL)
    qf, kf, vb = _qkv_proj(cqn, ckvn, wq, wkv, kpe, pe_tabs, MH)
    my, lse2 = _attn_fwd(qf, kf, vb, MH, T=T_ATT)
    later_bufs = _split_wait(later_ssem, later_rsem, later_bufs, my, _gather_ici_plan, name="gather_later_wait")
    later_bufs = _forward_halves(later_bufs, name="gather_later_forward")
    full.update({k: whole(k, g) for k, g in zip(later, later_bufs)})
    y_ret = _mm(gated, full["w_ret_o"], mode="nn", outs=[F32], name="ret_o")
    y_mla = _mm(my, full["w_mla_o"], mode="nn", outs=[F32], name="mla_o")
    merged = _merge_fwd(proj, y_ret, y_mla, D, off_gret, off_gmla)
    h1 = _mm(merged, full["w_out"], mode="nn", outs=[F32], name="out_proj",
             epi=lambda acc, r: (acc + r,), extras=(xs,))
    n1, rstd1 = _rmsnorm_fwd(h1, norm_mlp_g, name="norm_mlp")

    def up_epi(acc):
        r = jnp.maximum(acc, 0.0)
        return acc, r * r

    z, act = _mm(n1, full["w_up"], mode="nn", outs=[F32, BF16], name="up_proj", epi=up_epi)
    h2 = _mm(act, full["w_down"], mode="nn", outs=[F32], name="down_proj",
             epi=lambda acc, r: (acc + r,), extras=(h1,))
    loss11, dh2, g_norm_f = _final_loss(h2, norm_f_g.reshape(1, D), tgt)

    dz = _mm(dh2, full["w_down"], mode="nt", outs=[BF16], name="down_bwd_x",
             epi=lambda acc, zz: (acc * (2.0 * jnp.maximum(zz, 0.0)),), extras=(z,))
    g_w_down = _mm(act, dh2, mode="tn", outs=[BF16], name="down_bwd_w")
    dn1 = _mm(dz, full["w_up"], mode="nt", outs=[F32], name="up_bwd_x")
    g_w_up = _mm(n1, dz, mode="tn", outs=[BF16], name="up_bwd_w", out_shards=True)

    def reduce_begin(tag, names, grads):
        pcs = [g if g.ndim == 3 else g.reshape(N_CHIPS, g.shape[0] // N_CHIPS, g.shape[1]) for g in grads]
        theirs = _swap_halves(pcs, name="swap_" + tag)
        sums = [_sum_pair(p, t, place, name="sum_pair_" + k) for k, p, t in zip(names, pcs, theirs)]
        return pcs, theirs, sums

    def scatter_begin(tag, sums):
        lands = [lax.empty((3,) + s.shape[1:], s.dtype) for s in sums]
        return _split_start(sums + lands, _scatter_plan(len(sums)), 3 * len(sums), name="scatter_" + tag + "_start")

    g1 = ("w_up", "w_down")
    pcs1, theirs1, sums1 = reduce_begin("g1", g1, (g_w_up, g_w_down))
    ssem1, rsem1, bufs1, token1 = scatter_begin("g1", sums1)
    dh1, g_norm_mlp = _rmsnorm_bwd(dn1, h1, rstd1, norm_mlp_g, name="norm_mlp_bwd", res=dh2, deps=(token1,))
    dmerged = _mm(dh1, full["w_out"], mode="nt", outs=[F32], name="out_bwd_x")
    g_w_out = _mm(merged, dh1, mode="tn", outs=[BF16], name="out_bwd_w")
    dproj, dy_ret, dy_mla = _merge_bwd(dmerged, proj, y_ret, y_mla, D, off_gret)
    dgated = _mm(dy_ret, full["w_ret_o"], mode="nt", outs=[F32], name="ret_o_bwd_x")
    g_w_ret_o = _mm(gated, dy_ret, mode="tn", outs=[BF16], name="ret_o_bwd_w")
    dproj, g_ret_norm = _ret_bwd(proj, cosr, sinr, lgam, ret_norm_g, ry, dgated, states, dproj, RH, T=T_RET)
    def delta_epi(acc, o):
        rows = acc.shape[0]
        return acc, [jnp.broadcast_to(jnp.sum(acc[:, lo:lo + V_HEAD] * o[:, lo:lo + V_HEAD], axis=-1, keepdims=True),
                                      (rows, LANES)) for lo in range(0, acc.shape[1], V_HEAD)]

    dob, delta = _mm(dy_mla, full["w_mla_o"], mode="nt", outs=[BF16], name="mla_o_bwd_x", epi=delta_epi,
                     extras=(my,), more_outs=lambda tm, tn: [
                         (jax.ShapeDtypeStruct((MH, S, LANES), F32),
                          pl.BlockSpec((tn // V_HEAD, tm, LANES), lambda i, j, k: (j, i, 0)))])
    g_w_mla_o = _mm(my, dy_mla, mode="tn", outs=[BF16], name="mla_o_bwd_w")
    g2 = ("w_out", "w_ret_o", "w_mla_o")
    pcs2, theirs2, sums2 = reduce_begin("g2", g2, (g_w_out, g_w_ret_o, g_w_mla_o))
    ssem2, rsem2, bufs2, token2 = scatter_begin("g2", sums2)
    dq_all, dkv_all, dkpe_h = _attn_bwd(qf, kf, vb, dob, lse2, delta, pe_tabs, MH, T=T_ATT, deps=(token2,))
    dkpe = _kpe_sum(dkpe_h, pe_tabs, MH)
    dcqn = _mm(dq_all, wq, mode="nt", outs=[F32], name="q_bwd_x")
    g_wq = _mm(cqn, dq_all, mode="tn", outs=[BF16], name="q_bwd_w")
    dckvn = _mm(dkv_all, wkv, mode="nt", outs=[F32], name="kv_bwd_x")
    g_wkv = _mm(ckvn, dkv_all, mode="tn", outs=[BF16], name="kv_bwd_w")
    dproj, g_q_a = _rmsnorm_bwd(dcqn, proj, rstd_q, q_a_norm_g, name="norm_q_bwd", into=(dproj, off_cq // QL),
                                width=QL, col=off_cq // QL)
    dproj, g_kv_a = _rmsnorm_bwd(dckvn, proj, rstd_kv, kv_a_norm_g, name="norm_kv_bwd", into=(dproj, off_ckv // KVL),
                                 width=KVL, col=off_ckv // KVL)
    g_wa = _mm(dproj, u, mode="tn", outs=[BF16], name="in_bwd_w")
    g_wkpe = _mm(dkpe, u, mode="tn", outs=[BF16], name="kpe_bwd_w")

    there = sorted(runs)
    g_parts = [g_wa, g_wkpe]
    g_w_in = jnp.stack([jnp.concatenate(
        [p for s0, w, d0 in there for a, b in [(max(s0, jj * c_sh), min(s0 + w, (jj + 1) * c_sh))] if a < b
         for p in take(g_parts, d0 + a - s0, b - a)] + [jnp.zeros((c_pad - c_sh, D), BF16)], axis=0)
        for jj in range(N_CHIPS)])
    gq = g_wq.reshape(QL, MH, 2 * LANES)[:, :, :QK_NOPE + QK_ROPE].reshape(QL, MH * (QK_NOPE + QK_ROPE))
    g3 = ("w_in", "w_q_b", "w_kv_b")
    pcs3, theirs3, sums3 = reduce_begin("g3", g3, (g_w_in, _split_cols(gq), _split_cols(g_wkv)))
    ssem3, rsem3, bufs3, token3 = scatter_begin("g3", sums3)
    du_a = _mm(dproj, wa, mode="nn", outs=[F32], name="in_bwd_x", tk=1024, deps=(token3,))
    du = _mm(dkpe, wkpe, mode="nn", outs=[F32], name="kpe_bwd_x", epi=lambda acc, r: (acc + r,), extras=(du_a,))
    dx, g_norm_mix = _rmsnorm_bwd(du, xs, rstd0, norm_mix_g, name="norm_mix_bwd", res=dh1)

    bufs1 = _split_wait(ssem1, rsem1, bufs1, dx, _scatter_plan(len(g1)), name="scatter_g1_wait")
    bufs2 = _split_wait(ssem2, rsem2, bufs2, dx, _scatter_plan(len(g2)), name="scatter_g2_wait")
    bufs3 = _split_wait(ssem3, rsem3, bufs3, dx, _scatter_plan(len(g3)), name="scatter_g3_wait")
    recv1, recv2, recv3 = bufs1[len(g1):], bufs2[len(g2):], bufs3[len(g3):]
    halves = {}
    for names, pcs, theirs, recv in ((g1, pcs1, theirs1, recv1), (g2, pcs2, theirs2, recv2), (g3, pcs3, theirs3, recv3)):
        for k, p, t, r in zip(names, pcs, theirs, recv):
            halves[k] = _sum_chips(p, t, r, place, name="sum_chips_" + k)
    joined = _join_halves([halves[k] for k in big], name="join_halves")
    g_shard = {k: g.reshape(2 * g.shape[1], g.shape[2]) for k, g in zip(big, joined)}

    small = ("norm_mix_g", "ret_norm_g", "q_a_norm_g", "kv_a_norm_g", "norm_mlp_g", "norm_f_g")
    g_small = [g_norm_mix, g_ret_norm, g_q_a, g_kv_a, g_norm_mlp, g_norm_f]
    sizes = [int(v.size) for v in g_small]
    n_small = sum(sizes) + LANES
    rows = -(-n_small // (8 * LANES)) * 8
    packed = _pack_small(g_small + [jnp.broadcast_to(loss11.reshape(1), (LANES,))], rows)
    red = _allreduce_small(packed).reshape(-1)
    loss = red[sum(sizes)]
    w_small = [norm_mix_g, ret_norm_g, q_a_norm_g, kv_a_norm_g, norm_mlp_g, norm_f_g]
    m_small = [m_norm_mix_g, m_ret_norm_g, m_q_a_norm_g, m_kv_a_norm_g, m_norm_mlp_g, m_norm_f_g]
    v_small = [v_norm_mix_g, v_ret_norm_g, v_q_a_norm_g, v_kv_a_norm_g, v_norm_mlp_g, v_norm_f_g]
    g_pk = red[:rows * LANES].reshape(rows, LANES)
    d_pk, m_pk, v_pk = _rows_call(_adamw_vals, [_pack_small(w_small, rows), g_pk, _pack_small(m_small, rows),
                                               _pack_small(v_small, rows)], [F32, F32, F32], name="adamw_small")
    out_g, out_d, out_m, out_v = {}, {}, {}, {}
    off = 0
    for k, wv, sz in zip(small, w_small, sizes):
        for dst, src in ((out_g, g_pk), (out_d, d_pk), (out_m, m_pk), (out_v, v_pk)):
            dst[k] = src.reshape(-1)[off:off + sz].reshape(wv.shape)
        off += sz

    for k in big:
        res = _rows_call(lambda w, g, m, v: (g,) + _adamw_vals(w, g, m, v),
                         [w_sh[k], g_shard[k], m_sh[k], v_sh[k]], [F32] * 4, name="adamw_" + k)
        if k == "w_in":
            res = [r.T for r in res]
        out_g[k], out_d[k], out_m[k], out_v[k] = [r[None] for r in res]

    order = ("norm_mix_g", "w_in", "ret_norm_g", "w_ret_o", "q_a_norm_g", "w_q_b", "kv_a_norm_g", "w_kv_b",
             "w_mla_o", "w_out", "norm_mlp_g", "w_up", "w_down", "norm_f_g")
    return (loss, dx.reshape(1, S, D), *[out_g[k] for k in order], *[out_d[k] for k in order],
            *[out_m[k] for k in order], *[out_v[k] for k in order])
```

```python
import math

import jax
import jax.numpy as jnp
from jax import lax
from jax.experimental import pallas as pl
from jax.experimental.pallas import tpu as pltpu

F32 = jnp.float32
BF16 = jnp.bfloat16

EPS = 1e-6
ROPE_THETA = 10000.0
CHUNK = 64
RET_QK = 128
RET_V = 256
RET_HEAD_COLS = 2 * RET_QK + 2 * RET_V
QK_NOPE = 128
QK_ROPE = 64
V_HEAD = 128
LANES = 128
LOG2E = math.log2(math.e)

ADAM_LR = 0.001
ADAM_B1 = 0.9
ADAM_B2 = 0.999
ADAM_EPS = 1e-08
ADAM_WD = 0.01
ADAM_STEP = 10

N_CHIPS = 4
VMEM_LIMIT = 56 * 1024 * 1024
MESH = pl.DeviceIdType.MESH
NEG = -1e30


def _pallas(body, **kw):
    return pl.pallas_call(body, **kw)


def _params(sem=None):
    return pltpu.CompilerParams(dimension_semantics=sem, vmem_limit_bytes=VMEM_LIMIT)


def _tile(n, want):
    t = min(n, want)
    while n % t:
        t //= 2
    return t


_ANY = pl.BlockSpec(memory_space=pl.ANY)


def _mm(a, b, *, mode, outs, name, epi=None, extras=(), deps=(), out_shards=False, more_outs=None,
        tm=1024, tn=1024, tk=2048):
    shards = b.shape[0] if b.ndim == 3 else 1
    brows, bcols = b.shape[-2], b.shape[-1] * shards
    if mode == "nn":
        (M, K), N = a.shape, bcols
    elif mode == "nt":
        (M, K), N = a.shape, brows
    else:
        (K, M), N = a.shape, bcols
    tm = _tile(M, tm)
    tn = _tile(N // (shards if mode == "nn" else 1) // (N_CHIPS if out_shards else 1), tn)
    tk = _tile(K // (shards if mode == "nt" else 1), tk)
    nk = K // tk
    if mode == "nn":
        a_spec = pl.BlockSpec((tm, tk), lambda i, j, k: (i, k))
        dims = (((1,), (0,)), ((), ()))
        if shards > 1:
            per = N // shards // tn
            b_spec = pl.BlockSpec((None, tk, tn), lambda i, j, k: (j // per, k, j % per))
        else:
            b_spec = pl.BlockSpec((tk, tn), lambda i, j, k: (k, j))
    elif mode == "nt":
        a_spec = pl.BlockSpec((tm, tk), lambda i, j, k: (i, k))
        dims = (((1,), (1,)), ((), ()))
        if shards > 1:
            per = K // shards // tk
            b_spec = pl.BlockSpec((None, tn, tk), lambda i, j, k: (k // per, j, k % per))
        else:
            b_spec = pl.BlockSpec((tn, tk), lambda i, j, k: (j, k))
    else:
        assert shards == 1
        a_spec = pl.BlockSpec((tk, tm), lambda i, j, k: (k, i))
        b_spec = pl.BlockSpec((tk, tn), lambda i, j, k: (k, j))
        dims = (((0,), (0,)), ((), ()))
    if out_shards:
        assert not extras
        oper = N // N_CHIPS // tn
        o_spec = pl.BlockSpec((None, tm, tn), lambda i, j, k: (j // oper, i, j % oper))
        o_shape = (N_CHIPS, M, N // N_CHIPS)
    else:
        o_spec = pl.BlockSpec((tm, tn), lambda i, j, k: (i, j))
        o_shape = (M, N)
    more = [] if more_outs is None else more_outs(tm, tn)
    n_ex, n_out, n_dep = len(extras), len(outs) + len(more), len(deps)
    if epi is None:
        epi = lambda acc: (acc,)

    def body(*refs):
        a_ref, b_ref = refs[0], refs[1]
        ex_refs = refs[2:2 + n_ex]
        o_refs = refs[2 + n_ex + n_dep:2 + n_ex + n_dep + n_out]
        part = lax.dot_general(a_ref[...].astype(BF16), b_ref[...].astype(BF16), dims,
                               preferred_element_type=F32)

        def finish(acc):
            vals = epi(acc, *[r[...] for r in ex_refs])
            for r, v in zip(o_refs, vals):
                if isinstance(v, (list, tuple)):
                    for lead, piece in enumerate(v):
                        r[lead] = piece.astype(r.dtype)
                else:
                    r[...] = v.astype(r.dtype)

        if nk == 1:
            finish(part)
        else:
            acc_ref = refs[-1]
            k = pl.program_id(2)

            @pl.when(k == 0)
            def _():
                acc_ref[...] = part

            @pl.when(k > 0)
            def _():
                acc_ref[...] += part

            @pl.when(k == nk - 1)
            def _():
                finish(acc_ref[...])

    res = _pallas(
        body, name=name, grid=(M // tm, N // tn, nk),
        in_specs=[a_spec, b_spec] + [o_spec] * n_ex + [_ANY] * n_dep,
        out_specs=[o_spec] * len(outs) + [spec for _, spec in more],
        out_shape=[jax.ShapeDtypeStruct(o_shape, d) for d in outs] + [shape for shape, _ in more],
        scratch_shapes=[pltpu.VMEM((tm, tn), F32)] if nk > 1 else [],
        compiler_params=_params(("parallel", "parallel", "arbitrary")),
    )(a, b, *extras, *deps)
    return res[0] if n_out == 1 else res


def _rmsnorm_fwd(x, g, *, name, width=None, col=0, tr=256):
    S = x.shape[0]
    W = x.shape[1] if width is None else width
    tr = _tile(S, tr)

    def body(x_ref, g_ref, y_ref, r_ref):
        xv = x_ref[...]
        rstd = lax.rsqrt(jnp.mean(xv * xv, axis=-1, keepdims=True) + EPS)
        y_ref[...] = (xv * rstd * g_ref[...]).astype(BF16)
        r_ref[...] = rstd

    return _pallas(
        body, name=name, grid=(S // tr,),
        in_specs=[pl.BlockSpec((tr, W), lambda i: (i, col)), pl.BlockSpec((1, W), lambda i: (0, 0))],
        out_specs=[pl.BlockSpec((tr, W), lambda i: (i, 0)), pl.BlockSpec((tr, 1), lambda i: (i, 0))],
        out_shape=[jax.ShapeDtypeStruct((S, W), BF16), jax.ShapeDtypeStruct((S, 1), F32)],
        compiler_params=_params(("parallel",)),
    )(x, g)


def _rmsnorm_bwd(dy, x, rstd, g, *, name, res=None, into=None, deps=(), width=None, col=0, tr=256):
    S = x.shape[0]
    W = x.shape[1] if width is None else width
    tr = _tile(S, tr)
    has_res = res is not None

    def body(*refs):
        dy_ref, x_ref, r_ref, g_ref = refs[:4]
        dx_ref, dg_ref = refs[-2], refs[-1]
        rstd_v = r_ref[...]
        xhat = x_ref[...] * rstd_v
        dyv = dy_ref[...].astype(F32)
        dyg = dyv * g_ref[...]
        dx = rstd_v * (dyg - xhat * jnp.mean(dyg * xhat, axis=-1, keepdims=True))
        if has_res:
            dx = dx + refs[4][...]
        dx_ref[...] = dx.astype(dx_ref.dtype)
        part = jnp.sum(dyv * xhat, axis=0, keepdims=True)

        @pl.when(pl.program_id(0) == 0)
        def _():
            dg_ref[...] = part

        @pl.when(pl.program_id(0) > 0)
        def _():
            dg_ref[...] += part

    row = pl.BlockSpec((tr, W), lambda i: (i, 0))
    ins = [dy, x, rstd, g] + ([res] if has_res else [])
    in_specs = [row, pl.BlockSpec((tr, W), lambda i: (i, col)), pl.BlockSpec((tr, 1), lambda i: (i, 0)),
                pl.BlockSpec((1, W), lambda i: (0, 0))] + ([row] if has_res else [])
    if into is None:
        dx_spec, dx_shape, alias = row, jax.ShapeDtypeStruct((S, W), F32), {}
    else:
        buf, col_out = into
        ins.append(buf)
        in_specs.append(_ANY)
        dx_spec = pl.BlockSpec((tr, W), lambda i: (i, col_out))
        dx_shape = jax.ShapeDtypeStruct(buf.shape, buf.dtype)
        alias = {len(ins) - 1: 0}
    ins += list(deps)
    in_specs += [_ANY] * len(deps)
    return _pallas(
        body, name=name, grid=(S // tr,), in_specs=in_specs,
        out_specs=[dx_spec, pl.BlockSpec((1, W), lambda i: (0, 0))],
        out_shape=[dx_shape, jax.ShapeDtypeStruct((1, W), F32)],
        input_output_aliases=alias,
        compiler_params=_params(("arbitrary",)),
    )(*ins)


def _final_loss(h2, g, target, *, tr=256):
    S, D = h2.shape
    tr = _tile(S, tr)

    def body(h_ref, g_ref, t_ref, loss_ref, dh_ref, dg_ref):
        hv = h_ref[...]
        rstd = lax.rsqrt(jnp.mean(hv * hv, axis=-1, keepdims=True) + EPS)
        xhat = hv * rstd
        e = xhat * g_ref[...] - t_ref[...]
        lpart = (0.5 / D) * jnp.sum(jnp.sum(e * e, axis=-1, keepdims=True), axis=0, keepdims=True)
        dy = e * (1.0 / D)
        dyg = dy * g_ref[...]
        dh_ref[...] = rstd * (dyg - xhat * jnp.mean(dyg * xhat, axis=-1, keepdims=True))
        gpart = jnp.sum(dy * xhat, axis=0, keepdims=True)

        @pl.when(pl.program_id(0) == 0)
        def _():
            loss_ref[...] = lpart
            dg_ref[...] = gpart

        @pl.when(pl.program_id(0) > 0)
        def _():
            loss_ref[...] += lpart
            dg_ref[...] += gpart

    row = pl.BlockSpec((tr, D), lambda i: (i, 0))
    vec = pl.BlockSpec((1, D), lambda i: (0, 0))
    return _pallas(
        body, name="final_loss", grid=(S // tr,), in_specs=[row, vec, row],
        out_specs=[pl.BlockSpec((1, 1), lambda i: (0, 0)), row, vec],
        out_shape=[jax.ShapeDtypeStruct((1, 1), F32), jax.ShapeDtypeStruct((S, D), F32),
                   jax.ShapeDtypeStruct((1, D), F32)],
        compiler_params=_params(("arbitrary",)),
    )(h2, g, target)


def _sigmoid(v):
    return 1.0 / (1.0 + jnp.exp(-v))


def _merge_fwd(proj, y_ret, y_mla, D, off_gret, off_gmla, *, tr=256, tc=1024):
    S = y_ret.shape[0]
    tr, tc = _tile(S, tr), _tile(D, tc)
    b_ret, b_mla = off_gret // tc, off_gmla // tc

    def body(gr_ref, gm_ref, yr_ref, ym_ref, o_ref):
        o_ref[...] = (_sigmoid(gr_ref[...]) * yr_ref[...] + _sigmoid(gm_ref[...]) * ym_ref[...]).astype(BF16)

    blk = pl.BlockSpec((tr, tc), lambda i, j: (i, j))
    return _pallas(
        body, name="merge_fwd", grid=(S // tr, D // tc),
        in_specs=[pl.BlockSpec((tr, tc), lambda i, j: (i, b_ret + j)),
                  pl.BlockSpec((tr, tc), lambda i, j: (i, b_mla + j)), blk, blk],
        out_specs=blk, out_shape=jax.ShapeDtypeStruct((S, D), BF16),
        compiler_params=_params(("parallel", "parallel")),
    )(proj, proj, y_ret, y_mla)


def _merge_bwd(dmerged, proj, y_ret, y_mla, D, off_gret, *, tr=256):
    S = y_ret.shape[0]
    tr = _tile(S, tr)
    b0 = off_gret // D

    def body(dm_ref, g_ref, yr_ref, ym_ref, dp_ref, dyr_ref, dym_ref):
        dm = dm_ref[...]
        sg = _sigmoid(g_ref[...])

        @pl.when(pl.program_id(1) == 0)
        def _():
            dyr_ref[...] = (dm * sg).astype(BF16)
            dp_ref[...] = (dm * yr_ref[...] * sg * (1.0 - sg)).astype(BF16)

        @pl.when(pl.program_id(1) == 1)
        def _():
            dym_ref[...] = (dm * sg).astype(BF16)
            dp_ref[...] = (dm * ym_ref[...] * sg * (1.0 - sg)).astype(BF16)

    blk = pl.BlockSpec((tr, D), lambda i, j: (i, 0))
    return _pallas(
        body, name="merge_bwd", grid=(S // tr, 2),
        in_specs=[blk, pl.BlockSpec((tr, D), lambda i, j: (i, b0 + j)), blk, blk],
        out_specs=[pl.BlockSpec((tr, D), lambda i, j: (i, b0 + j)), blk, blk],
        out_shape=[jax.ShapeDtypeStruct(proj.shape, BF16), jax.ShapeDtypeStruct((S, D), BF16),
                   jax.ShapeDtypeStruct((S, D), BF16)],
        compiler_params=_params(("parallel", "arbitrary")),
    )(dmerged, proj, y_ret, y_mla)


def _rope128(t, cos_full, sin_signed):
    return t * cos_full + pltpu.roll(t, RET_QK // 2, 1) * sin_signed


def _rope128_t(d, cos_full, sin_signed):
    return d * cos_full + pltpu.roll(d * sin_signed, RET_QK // 2, 1)


def _ret_consts(lg, T):
    pos = lax.broadcasted_iota(jnp.int32, (T, 1), 0).astype(F32)
    qd = jnp.exp(lg * (pos + 1.0))
    kd = jnp.exp(lg * (T - 1.0 - pos))
    n = lax.broadcasted_iota(jnp.int32, (T, T), 0)
    m = lax.broadcasted_iota(jnp.int32, (T, T), 1)
    vis = (m // CHUNK) <= (n // CHUNK)
    dist = jnp.abs(n - m).astype(F32)
    decay = jnp.where(vis, jnp.exp(lg * dist), 0.0)
    cdec = jnp.exp(lg * float(T))
    return qd, kd, decay, cdec


def _dot(a, b, dims):
    return lax.dot_general(a.astype(BF16), b.astype(BF16), (dims, ((), ())), preferred_element_type=F32)


NN = ((1,), (0,))
NT = ((1,), (1,))
TN = ((0,), (0,))
_RQ = slice(0, RET_QK)
_RK = slice(RET_QK, 2 * RET_QK)
_RV = slice(2 * RET_QK, 2 * RET_QK + RET_V)
_RG = slice(2 * RET_QK + RET_V, RET_HEAD_COLS)


RET_GROUP = 4


def _head_cols(h, part):
    return slice(h * RET_HEAD_COLS + part.start, h * RET_HEAD_COLS + part.stop)


def _ret_fwd(proj, cosr, sinr, lgam, gain, RH, *, T):
    S = proj.shape[0]
    nb = S // T
    G = _tile(RH, RET_GROUP)
    heads = range(G)
    scale = RET_QK ** -0.5

    def body(p_ref, cos_ref, sin_ref, lg_ref, gain_ref, ry_ref, gated_ref, st_ref, state):
        b = pl.program_id(1)

        @pl.when(b == 0)
        def _():
            state[...] = jnp.zeros_like(state)

        consts = [_ret_consts(lg_ref[h, 0:1, 0:1], T) for h in heads]
        cosv, sinv = cos_ref[...], sin_ref[...]
        q = [_rope128(p_ref[:, _head_cols(h, _RQ)], cosv, sinv) for h in heads]
        k = [_rope128(p_ref[:, _head_cols(h, _RK)], cosv, sinv) * scale for h in heads]
        v = [p_ref[:, _head_cols(h, _RV)] for h in heads]
        sprev = [state[h] for h in heads]
        for h in heads:
            st_ref[h] = sprev[h]
        a = [_dot(q[h], k[h], NT) for h in heads]
        qs = [_dot(q[h] * consts[h][0], sprev[h], NN) for h in heads]
        kv = [_dot(k[h] * consts[h][1], v[h], TN) for h in heads]
        o = [_dot(a[h] * consts[h][2], v[h], NN) + qs[h] for h in heads]
        for h in heads:
            state[h] = sprev[h] * consts[h][3] + kv[h]
            vals = slice(h * RET_V, (h + 1) * RET_V)
            ry_ref[:, vals] = o[h]
            mu = jnp.mean(o[h], axis=-1, keepdims=True)
            oc = o[h] - mu
            var = jnp.mean(oc * oc, axis=-1, keepdims=True)
            t = oc * lax.rsqrt(var + EPS) * gain_ref[:, vals]
            gv = p_ref[:, _head_cols(h, _RG)]
            gated_ref[:, vals] = (t * (gv * _sigmoid(gv))).astype(BF16)

    return _pallas(
        body, name="ret_fwd", grid=(RH // G, nb),
        in_specs=[pl.BlockSpec((T, G * RET_HEAD_COLS), lambda h, b: (b, h)),
                  pl.BlockSpec((T, RET_QK), lambda h, b: (b, 0)),
                  pl.BlockSpec((T, RET_QK), lambda h, b: (b, 0)),
                  pl.BlockSpec((G, 8, LANES), lambda h, b: (h, 0, 0)),
                  pl.BlockSpec((1, G * RET_V), lambda h, b: (0, h))],
        out_specs=[pl.BlockSpec((T, G * RET_V), lambda h, b: (b, h)),
                   pl.BlockSpec((T, G * RET_V), lambda h, b: (b, h)),
                   pl.BlockSpec((G, None, RET_QK, RET_V), lambda h, b: (h, b, 0, 0))],
        out_shape=[jax.ShapeDtypeStruct((S, RH * RET_V), F32), jax.ShapeDtypeStruct((S, RH * RET_V), BF16),
                   jax.ShapeDtypeStruct((RH, nb, RET_QK, RET_V), F32)],
        scratch_shapes=[pltpu.VMEM((G, RET_QK, RET_V), F32)],
        compiler_params=_params(("parallel", "arbitrary")),
    )(proj, cosr, sinr, lgam, gain)


def _ret_bwd(proj, cosr, sinr, lgam, gain, ry, dgated, states, dproj, RH, *, T):
    S = proj.shape[0]
    nb = S // T
    G = _tile(RH, RET_GROUP)
    heads = range(G)
    scale = RET_QK ** -0.5

    def body(p_ref, cos_ref, sin_ref, lg_ref, gain_ref, ry_ref, dg_ref, st_ref, _, dp_ref, dgain_ref, dstate):
        b = pl.program_id(1)

        @pl.when(b == 0)
        def _():
            dstate[...] = jnp.zeros_like(dstate)

        consts = [_ret_consts(lg_ref[h, 0:1, 0:1], T) for h in heads]
        qd, kd, decay, cdec = [[c[i] for c in consts] for i in range(4)]
        cosv, sinv = cos_ref[...], sin_ref[...]
        q = [_rope128(p_ref[:, _head_cols(h, _RQ)], cosv, sinv) for h in heads]
        k = [_rope128(p_ref[:, _head_cols(h, _RK)], cosv, sinv) * scale for h in heads]
        v = [p_ref[:, _head_cols(h, _RV)] for h in heads]
        sprev = [st_ref[h] for h in heads]
        ds_new = [dstate[h] for h in heads]
        a = [_dot(q[h], k[h], NT) for h in heads]
        do, gparts = [], []
        for h in heads:
            vals = slice(h * RET_V, (h + 1) * RET_V)
            o = ry_ref[:, vals]
            mu = jnp.mean(o, axis=-1, keepdims=True)
            oc = o - mu
            rstd = lax.rsqrt(jnp.mean(oc * oc, axis=-1, keepdims=True) + EPS)
            ryn = oc * rstd
            gainv = gain_ref[:, vals]
            gv = p_ref[:, _head_cols(h, _RG)]
            sg = _sigmoid(gv)
            dgt = dg_ref[:, vals]
            dt = dgt * (gv * sg)
            dp_ref[:, _head_cols(h, _RG)] = (dgt * (ryn * gainv) * (sg * (1.0 + gv * (1.0 - sg)))).astype(BF16)
            gparts.append(jnp.sum(dt * ryn, axis=0, keepdims=True))
            dryn = dt * gainv
            do.append(rstd * (dryn - jnp.mean(dryn, axis=-1, keepdims=True)
                              - ryn * jnp.mean(dryn * ryn, axis=-1, keepdims=True)))
        gpart = jnp.concatenate(gparts, axis=1)

        @pl.when(b == 0)
        def _():
            dgain_ref[...] = gpart

        @pl.when(b > 0)
        def _():
            dgain_ref[...] += gpart

        dpm = [_dot(do[h], v[h], NT) for h in heads]
        dq_s = [_dot(do[h], sprev[h], NT) for h in heads]
        dk_s = [_dot(v[h], ds_new[h], NT) for h in heads]
        dv_s = [_dot(k[h] * kd[h], ds_new[h], NN) for h in heads]
        dst = [_dot(q[h] * qd[h], do[h], TN) for h in heads]
        a = [a[h] * decay[h] for h in heads]
        dpm = [dpm[h] * decay[h] for h in heads]
        dv = [_dot(a[h], do[h], TN) + dv_s[h] for h in heads]
        dq = [_dot(dpm[h], k[h], NN) + dq_s[h] * qd[h] for h in heads]
        dk = [(_dot(dpm[h], q[h], TN) + dk_s[h] * kd[h]) * scale for h in heads]
        for h in heads:
            dstate[h] = ds_new[h] * cdec[h] + dst[h]
            dp_ref[:, _head_cols(h, _RV)] = dv[h].astype(BF16)
            dp_ref[:, _head_cols(h, _RQ)] = _rope128_t(dq[h], cosv, sinv).astype(BF16)
            dp_ref[:, _head_cols(h, _RK)] = _rope128_t(dk[h], cosv, sinv).astype(BF16)

    rb = lambda b: nb - 1 - b
    return _pallas(
        body, name="ret_bwd", grid=(RH // G, nb),
        in_specs=[pl.BlockSpec((T, G * RET_HEAD_COLS), lambda h, b: (rb(b), h)),
                  pl.BlockSpec((T, RET_QK), lambda h, b: (rb(b), 0)),
                  pl.BlockSpec((T, RET_QK), lambda h, b: (rb(b), 0)),
                  pl.BlockSpec((G, 8, LANES), lambda h, b: (h, 0, 0)),
                  pl.BlockSpec((1, G * RET_V), lambda h, b: (0, h)),
                  pl.BlockSpec((T, G * RET_V), lambda h, b: (rb(b), h)),
                  pl.BlockSpec((T, G * RET_V), lambda h, b: (rb(b), h)),
                  pl.BlockSpec((G, None, RET_QK, RET_V), lambda h, b: (h, rb(b), 0, 0)),
                  _ANY],
        out_specs=[pl.BlockSpec((T, G * RET_HEAD_COLS), lambda h, b: (rb(b), h)),
                   pl.BlockSpec((1, G * RET_V), lambda h, b: (0, h))],
        out_shape=[jax.ShapeDtypeStruct(dproj.shape, dproj.dtype), jax.ShapeDtypeStruct((1, RH * RET_V), F32)],
        scratch_shapes=[pltpu.VMEM((G, RET_QK, RET_V), F32)],
        input_output_aliases={8: 0},
        compiler_params=_params(("parallel", "arbitrary")),
    )(proj, cosr, sinr, lgam, gain, ry, dgated, states, dproj)


def _rope_pe(t, c, s1, s2):
    return t * c + pltpu.roll(t, LANES - QK_ROPE // 2, 1) * s1 + pltpu.roll(t, QK_ROPE // 2, 1) * s2


def _rope_pe_t(d, c, s1, s2):
    return d * c + pltpu.roll(d * s1, QK_ROPE // 2, 1) + pltpu.roll(d * s2, LANES - QK_ROPE // 2, 1)


ATTN_C2 = (QK_NOPE + QK_ROPE) ** -0.5 * LOG2E


def _qkv_proj(cqn, ckvn, wq, wkv, kpe, tabs, MH, *, tm=512, heads=4):
    S = cqn.shape[0]
    tm = _tile(S, tm)
    hb = _tile(MH, heads)
    W = 2 * LANES
    c_t, s1_t, s2_t = tabs

    def body(cq_ref, ckv_ref, wq_ref, wkv_ref, kpe_ref, c_ref, s1_ref, s2_ref, qf_ref, kf_ref, v_ref):
        c, s1, s2 = c_ref[...], s1_ref[...], s2_ref[...]
        q = _dot(cq_ref[...], wq_ref[...], NN)
        kv = _dot(ckv_ref[...], wkv_ref[...], NN)
        kper = _rope_pe(kpe_ref[...], c, s1, s2).astype(BF16)
        for h in range(hb):
            lo, mid, hi = h * W, h * W + QK_NOPE, (h + 1) * W
            qf_ref[:, lo:mid] = (q[:, lo:mid] * ATTN_C2).astype(BF16)
            qf_ref[:, mid:hi] = (_rope_pe(q[:, mid:hi], c, s1, s2) * ATTN_C2).astype(BF16)
            kf_ref[:, lo:mid] = kv[:, lo:mid].astype(BF16)
            kf_ref[:, mid:hi] = kper
            v_ref[:, h * V_HEAD:(h + 1) * V_HEAD] = kv[:, mid:hi].astype(BF16)

    tab = pl.BlockSpec((tm, LANES), lambda i, j: (i, 0))
    grp = pl.BlockSpec((tm, hb * W), lambda i, j: (i, j))
    return _pallas(
        body, name="qkv_proj", grid=(S // tm, MH // hb),
        in_specs=[pl.BlockSpec((tm, cqn.shape[1]), lambda i, j: (i, 0)),
                  pl.BlockSpec((tm, ckvn.shape[1]), lambda i, j: (i, 0)),
                  pl.BlockSpec((wq.shape[0], hb * W), lambda i, j: (0, j)),
                  pl.BlockSpec((wkv.shape[0], hb * W), lambda i, j: (0, j)), tab, tab, tab, tab],
        out_specs=[grp, grp, pl.BlockSpec((tm, hb * V_HEAD), lambda i, j: (i, j))],
        out_shape=[jax.ShapeDtypeStruct((S, MH * W), BF16)] * 2 + [jax.ShapeDtypeStruct((S, MH * V_HEAD), BF16)],
        compiler_params=_params(("parallel", "parallel")),
    )(cqn, ckvn, wq, wkv, kpe, c_t, s1_t, s2_t)


def _chunk_mask(T):
    n = lax.broadcasted_iota(jnp.int32, (T, T), 0)
    m = lax.broadcasted_iota(jnp.int32, (T, T), 1)
    return (m // CHUNK) <= (n // CHUNK)


def _lanes_to(v, width):
    return jnp.tile(v, (1, width // LANES))


def _attn_fwd(qf, kf, vb, MH, *, T):
    S = qf.shape[0]
    nt = S // T

    def body(q_ref, k_ref, v_ref, o_ref, lse_ref, m_sc, l_sc, acc_sc, s_a, s_b):
        qi = pl.program_id(1)
        m_sc[...] = jnp.full_like(m_sc, NEG)
        l_sc[...] = jnp.zeros_like(l_sc)
        acc_sc[...] = jnp.zeros_like(acc_sc)

        def rows_of(kt):
            return pl.ds(pl.multiple_of(kt * T, T), T)

        def scores(kt):
            return _dot(q_ref[...], k_ref[rows_of(kt), :], NT)

        def update(s, kt):
            m_prev = m_sc[...]
            m_new = jnp.maximum(m_prev, jnp.max(s, axis=-1, keepdims=True))
            alpha = jnp.exp2(m_prev - m_new)
            p = jnp.exp2(s - _lanes_to(m_new, T))
            l_sc[...] = alpha * l_sc[...] + jnp.sum(p, axis=-1, keepdims=True)
            acc_sc[...] = alpha * acc_sc[...] + _dot(p, v_ref[rows_of(kt), :], NN)
            m_sc[...] = m_new

        update(jnp.where(_chunk_mask(T), scores(qi), NEG), qi)

        @pl.when(qi % 2 == 1)
        def _():
            update(scores(qi - 1), qi - 1)

        pairs = qi // 2

        @pl.when(pairs > 0)
        def _():
            s_a[...] = scores(0)

        def pair(j, carry):
            t0 = 2 * j
            s_b[...] = scores(t0 + 1)
            update(s_a[...], t0)
            s_a[...] = scores(jnp.minimum(t0 + 2, 2 * pairs - 1))
            update(s_b[...], t0 + 1)
            return carry

        lax.fori_loop(0, pairs, pair, 0)
        l = l_sc[...]
        o_ref[...] = acc_sc[...] / l
        lse_ref[...] = m_sc[...] + jnp.log(l) * LOG2E

    return _pallas(
        body, name="attn_fwd", grid=(MH, nt),
        in_specs=[pl.BlockSpec((T, 2 * LANES), lambda h, i: (i, h)),
                  pl.BlockSpec((S, 2 * LANES), lambda h, i: (0, h)),
                  pl.BlockSpec((S, LANES), lambda h, i: (0, h))],
        out_specs=[pl.BlockSpec((T, LANES), lambda h, i: (i, h)),
                   pl.BlockSpec((None, T, LANES), lambda h, i: (h, i, 0))],
        out_shape=[jax.ShapeDtypeStruct((S, MH * LANES), F32), jax.ShapeDtypeStruct((MH, S, LANES), F32)],
        scratch_shapes=[pltpu.VMEM((T, LANES), F32), pltpu.VMEM((T, LANES), F32), pltpu.VMEM((T, LANES), F32),
                        pltpu.VMEM((T, T), F32), pltpu.VMEM((T, T), F32)],
        compiler_params=_params(("parallel", "parallel")),
    )(qf, kf, vb)


def _attn_bwd(qf, kf, vb, dob, lse2, delta, tabs, MH, *, T, deps=()):
    S = qf.shape[0]
    nt = S // T
    scale = (QK_NOPE + QK_ROPE) ** -0.5
    n_dep = len(deps)

    def body(q_ref, k_ref, v_ref, do_ref, lse_ref, dl_ref, c_ref, s1_ref, s2_ref, *rest):
        dqa_ref, dkv_ref, dkpe_ref, dq_ref, dk_sc, dv_sc, s_a, dp_a, s_b, dp_b = rest[n_dep:]
        kj = pl.program_id(1)

        @pl.when(kj == 0)
        def _():
            dq_ref[...] = jnp.zeros_like(dq_ref)

        dk_sc[...] = jnp.zeros_like(dk_sc)
        dv_sc[...] = jnp.zeros_like(dv_sc)

        def rows_of(qt):
            return pl.ds(pl.multiple_of(qt * T, T), T)

        def products(qt):
            rows = rows_of(qt)
            return _dot(q_ref[rows, :], k_ref[...], NT), _dot(do_ref[rows, :], v_ref[...], NT)

        def update(s, dp, qt):
            rows = rows_of(qt)
            q, dov = q_ref[rows, :], do_ref[rows, :]
            p = jnp.exp2(s - _lanes_to(lse_ref[rows, :], T))
            ds = p * (dp - _lanes_to(dl_ref[rows, :], T))
            dv_sc[...] += _dot(p, dov, TN)
            dk_sc[...] += _dot(ds, q, TN)
            dq_ref[rows, :] += _dot(ds, k_ref[...], NN)

        s, dp = products(kj)
        update(jnp.where(_chunk_mask(T), s, NEG), dp, kj)
        rest = nt - 1 - kj

        @pl.when(rest % 2 == 1)
        def _():
            s1, dp1 = products(nt - 1)
            update(s1, dp1, nt - 1)

        pairs = rest // 2

        @pl.when(pairs > 0)
        def _():
            s_a[...], dp_a[...] = products(kj + 1)

        def pair(j, carry):
            t0 = kj + 1 + 2 * j
            s_b[...], dp_b[...] = products(t0 + 1)
            update(s_a[...], dp_a[...], t0)
            s_a[...], dp_a[...] = products(jnp.minimum(t0 + 2, kj + 2 * pairs))
            update(s_b[...], dp_b[...], t0 + 1)
            return carry

        lax.fori_loop(0, pairs, pair, 0)
        dkv_ref[:, :QK_NOPE] = (dk_sc[:, :QK_NOPE] * (1.0 / LOG2E)).astype(BF16)
        dkv_ref[:, QK_NOPE:] = dv_sc[...].astype(BF16)
        dkpe_ref[...] = dk_sc[:, QK_NOPE:] * (1.0 / LOG2E)

        @pl.when(kj == nt - 1)
        def _():
            dqa_ref[:, :QK_NOPE] = (dq_ref[:, :QK_NOPE] * scale).astype(BF16)
            dqa_ref[:, QK_NOPE:] = (_rope_pe_t(dq_ref[:, QK_NOPE:], c_ref[...], s1_ref[...], s2_ref[...])
                                    * scale).astype(BF16)

    stat = pl.BlockSpec((None, S, LANES), lambda h, j: (h, 0, 0))
    tab = pl.BlockSpec((S, LANES), lambda h, j: (0, 0))
    return _pallas(
        body, name="attn_bwd", grid=(MH, nt),
        in_specs=[pl.BlockSpec((S, 2 * LANES), lambda h, j: (0, h)),
                  pl.BlockSpec((T, 2 * LANES), lambda h, j: (j, h)),
                  pl.BlockSpec((T, LANES), lambda h, j: (j, h)),
                  pl.BlockSpec((S, LANES), lambda h, j: (0, h)), stat, stat, tab, tab, tab] + [_ANY] * n_dep,
        out_specs=[pl.BlockSpec((S, 2 * LANES), lambda h, j: (0, h)),
                   pl.BlockSpec((T, 2 * LANES), lambda h, j: (j, h)),
                   pl.BlockSpec((T, LANES), lambda h, j: (j, h))],
        out_shape=[jax.ShapeDtypeStruct((S, MH * 2 * LANES), BF16), jax.ShapeDtypeStruct((S, MH * 2 * LANES), BF16),
                   jax.ShapeDtypeStruct((S, MH * LANES), F32)],
        scratch_shapes=[pltpu.VMEM((S, 2 * LANES), F32), pltpu.VMEM((T, 2 * LANES), F32), pltpu.VMEM((T, LANES), F32)]
        + [pltpu.VMEM((T, T), F32)] * 4,
        compiler_params=_params(("parallel", "arbitrary")),
    )(qf, kf, vb, dob, lse2, delta, *tabs, *deps)


def _kpe_sum(dkpe_h, tabs, MH, *, tr=256):
    S = dkpe_h.shape[0]
    tr = _tile(S, tr)

    def body(dk_ref, c_ref, s1_ref, s2_ref, dkpe_ref):
        tot = dk_ref[:, :LANES]
        for h in range(1, MH):
            tot = tot + dk_ref[:, h * LANES:(h + 1) * LANES]
        dkpe_ref[...] = _rope_pe_t(tot, c_ref[...], s1_ref[...], s2_ref[...]).astype(BF16)

    tab = pl.BlockSpec((tr, LANES), lambda i: (i, 0))
    return _pallas(
        body, name="kpe_sum", grid=(S // tr,),
        in_specs=[pl.BlockSpec((tr, MH * LANES), lambda i: (i, 0)), tab, tab, tab],
        out_specs=tab, out_shape=jax.ShapeDtypeStruct((S, LANES), BF16),
        compiler_params=_params(("parallel",)),
    )(dkpe_h, *tabs)


ROW_ALIGN = 16


def _blk(R, C, block_bytes=2 << 20):
    cap = max(ROW_ALIGN, block_bytes // (C * 4))
    for t in range(min(R, cap) // ROW_ALIGN * ROW_ALIGN, LANES - 1, -ROW_ALIGN):
        if R % t == 0:
            return t, C
    if R <= cap:
        return R, C
    tc = C
    while R * tc * 4 > block_bytes and tc % (2 * LANES) == 0:
        tc //= 2
    return R, tc


def _rows_call(fn, ins, out_dtypes, *, name):
    R, C = ins[0].shape
    tr, tc = _blk(R, C)
    n_in = len(ins)

    def body(*refs):
        vals = fn(*[r[...] for r in refs[:n_in]])
        for r, v in zip(refs[n_in:], vals):
            r[...] = v.astype(r.dtype)

    blk = pl.BlockSpec((tr, tc), lambda i, j: (i, j))
    res = _pallas(
        body, name=name, grid=(R // tr, C // tc), in_specs=[blk] * n_in, out_specs=[blk] * len(out_dtypes),
        out_shape=[jax.ShapeDtypeStruct((R, C), d) for d in out_dtypes],
        compiler_params=_params(("parallel", "parallel")),
    )(*ins)
    return res


def _adamw_vals(w, g, m, v):
    m = ADAM_B1 * m + (1.0 - ADAM_B1) * g
    v = ADAM_B2 * v + (1.0 - ADAM_B2) * (g * g)
    m_hat = m / (1.0 - ADAM_B1 ** ADAM_STEP)
    v_hat = v / (1.0 - ADAM_B2 ** ADAM_STEP)
    delta = -ADAM_LR * (m_hat / (jnp.sqrt(v_hat) + ADAM_EPS) + ADAM_WD * w)
    return delta, m, v


def _sum_pair(p, theirs, place, *, name):
    _, R, C = p.shape
    R2 = R // 2
    tr, tc = _blk(R2, C)
    p4 = p.reshape(N_CHIPS, 2, R2, C)

    def body(place_ref, a_ref, b_ref, o_ref):
        o_ref[...] = (a_ref[...].astype(F32) + b_ref[...].astype(F32)).astype(BF16)

    spec = pltpu.PrefetchScalarGridSpec(
        num_scalar_prefetch=1, grid=(N_CHIPS, R2 // tr, C // tc),
        in_specs=[pl.BlockSpec((None, None, tr, tc), lambda q, i, j, pr: (q, pr[0], i, j)),
                  pl.BlockSpec((None, tr, tc), lambda q, i, j, pr: (q, i, j))],
        out_specs=pl.BlockSpec((None, tr, tc), lambda q, i, j, pr: (q, i, j)))
    return _pallas(body, name=name, grid_spec=spec, out_shape=jax.ShapeDtypeStruct((N_CHIPS, R2, C), BF16),
                   compiler_params=_params(("parallel", "parallel", "parallel")))(place, p4, theirs)


def _sum_chips(p, theirs, recv, place, *, name):
    _, R, C = p.shape
    R2 = R // 2
    tr, tc = _blk(R2, C)
    p4 = p.reshape(N_CHIPS, 2, R2, C)

    def body(place_ref, a_ref, b_ref, r0_ref, r1_ref, r2_ref, o_ref):
        own = a_ref[...].astype(F32) + b_ref[...].astype(F32)
        o_ref[...] = ((own + r0_ref[...].astype(F32)) + r1_ref[...].astype(F32)) + r2_ref[...].astype(F32)

    def slot(k):
        return pl.BlockSpec((None, tr, tc), lambda i, j, pr: (k, i, j))

    spec = pltpu.PrefetchScalarGridSpec(
        num_scalar_prefetch=1, grid=(R2 // tr, C // tc),
        in_specs=[pl.BlockSpec((None, None, tr, tc), lambda i, j, pr: (pr[1], pr[0], i, j)),
                  pl.BlockSpec((None, tr, tc), lambda i, j, pr: (pr[1], i, j)), slot(0), slot(1), slot(2)],
        out_specs=pl.BlockSpec((None, tr, tc), lambda i, j, pr: (pr[0], i, j)))
    return _pallas(body, name=name, grid_spec=spec, out_shape=jax.ShapeDtypeStruct((2, R2, C), F32),
                   compiler_params=_params(("parallel", "parallel")))(place, p4, theirs, recv, recv, recv)


def _me():
    return lax.axis_index("x"), lax.axis_index("y"), lax.axis_index("c")


def _other_chips(x, y):
    return [(1 - x, y), (x, 1 - y), (1 - x, 1 - y)]


def _rcopy(src, dst, ssem, rsem, dev):
    return pltpu.make_async_remote_copy(src_ref=src, dst_ref=dst, send_sem=ssem, recv_sem=rsem,
                                        device_id=dev, device_id_type=MESH)


def _cast_into_slot(w, place, *, name, rows=None, deps=()):
    R, C = w.shape
    rows = R if rows is None else rows
    tr, tc = _blk(R, C)

    def body(place_ref, w_ref, *rest):
        rest[-1][...] = w_ref[...].astype(BF16)

    spec = pltpu.PrefetchScalarGridSpec(
        num_scalar_prefetch=1, grid=(R // tr, C // tc),
        in_specs=[pl.BlockSpec((tr, tc), lambda i, j, pr: (i, j))] + [_ANY] * len(deps),
        out_specs=pl.BlockSpec((None, tr, tc), lambda i, j, pr: (pr[1], i, j)))
    out = _pallas(body, name=name, grid_spec=spec, out_shape=jax.ShapeDtypeStruct((N_CHIPS, rows, C), BF16),
                  compiler_params=_params(("parallel", "parallel")))(place, w, *deps)
    return out.reshape(N_CHIPS, 2, rows // 2, C)


def _gather_ici_plan(bufs):
    x, y, c = _me()
    j = 2 * x + y
    plan = []
    for i, buf in enumerate(bufs):
        for k, (px, py) in enumerate(_other_chips(x, y)):
            plan.append((3 * i + k, buf.at[j, c], buf.at[j, c], (px, py, c)))
    return plan


def _forward_halves(bufs, *, name):
    n = len(bufs)

    def body(*refs):
        outs = refs[n:2 * n]
        ssem, rsem = refs[2 * n:]
        x, y, c = _me()
        sib = (x, y, 1 - c)
        cps = []
        for i in range(n):
            for k, (px, py) in enumerate(_other_chips(x, y)):
                slot = outs[i].at[2 * px + py, c]
                r = _rcopy(slot, slot, ssem.at[3 * i + k], rsem.at[3 * i + k], sib)
                r.start()
                cps.append(r)
        for r in cps:
            r.wait()

    return _pallas(
        body, name=name, in_specs=[_ANY] * n, out_specs=[_ANY] * n,
        out_shape=[jax.ShapeDtypeStruct(b.shape, b.dtype) for b in bufs],
        scratch_shapes=[pltpu.SemaphoreType.DMA((3 * n,))] * 2,
        input_output_aliases={i: i for i in range(n)},
        compiler_params=pltpu.CompilerParams(has_side_effects=True),
    )(*bufs)


_HBM = pl.BlockSpec(memory_space=pltpu.HBM)
_SEM = pl.BlockSpec(memory_space=pltpu.SEMAPHORE)
_EFFECT = pltpu.SideEffectType.DATAFLOW_SIDE_EFFECTING


def _split_start(bufs, plan, n_copies, *, name):
    n = len(bufs)

    def body(*refs):
        ssem, rsem = refs[n], refs[n + 1]
        for s, src, dst, dev in plan(refs[:n]):
            _rcopy(src, dst, ssem.at[s], rsem.at[s], dev).start()
        refs[-1][...] = jnp.zeros_like(refs[-1])

    res = _pallas(
        body, name=name, in_specs=[_HBM] * n,
        out_specs=(_SEM, _SEM, *[_HBM] * n, pl.BlockSpec(memory_space=pltpu.VMEM)),
        out_shape=(pltpu.SemaphoreType.DMA((n_copies,)), pltpu.SemaphoreType.DMA((n_copies,)),
                   *[pltpu.HBM(b.shape, b.dtype) for b in bufs], jax.ShapeDtypeStruct((8, LANES), F32)),
        input_output_aliases={i: 2 + i for i in range(n)},
        compiler_params=pltpu.CompilerParams(has_side_effects=_EFFECT),
    )(*[pltpu.with_memory_space_constraint(b, pltpu.HBM) for b in bufs])
    return res[0], res[1], list(res[2:2 + n]), res[-1]


def _split_wait(ssem, rsem, bufs, after, plan, *, name):
    n = len(bufs)

    def body(*refs):
        ssem_ref, rsem_ref = refs[n], refs[n + 1]
        for s, src, dst, dev in plan(refs[:n]):
            cp = _rcopy(src, dst, ssem_ref.at[s], rsem_ref.at[s], dev)
            cp.wait_send()
            cp.wait_recv()

    return list(_pallas(
        body, name=name, in_specs=[_HBM] * n + [_SEM, _SEM, _ANY], out_specs=[_HBM] * n,
        out_shape=[pltpu.HBM(b.shape, b.dtype) for b in bufs],
        input_output_aliases={i: i for i in range(n)},
        compiler_params=pltpu.CompilerParams(has_side_effects=_EFFECT),
    )(*bufs, ssem, rsem, after))


def _swap_plan(n):
    def plan(bufs):
        x, y, c = _me()
        return [(i, bufs[i].at[:, 1 - c], bufs[n + i], (x, y, 1 - c)) for i in range(n)]
    return plan


def _scatter_plan(n):
    def plan(bufs):
        x, y, c = _me()
        out = []
        for i in range(n):
            for k, (px, py) in enumerate(_other_chips(x, y)):
                out.append((3 * i + k, bufs[i].at[2 * px + py], bufs[n + i].at[k], (px, py, c)))
        return out
    return plan


def _swap_halves(grads, *, name):
    n = len(grads)
    views = [g.reshape(N_CHIPS, 2, g.shape[1] // 2, g.shape[2]) for g in grads]

    def body(*refs):
        ins, outs = refs[:n], refs[n:2 * n]
        ssem, rsem = refs[2 * n:]
        x, y, c = _me()
        sib = (x, y, 1 - c)
        cps = []
        for i in range(n):
            r = _rcopy(ins[i].at[:, 1 - c], outs[i], ssem.at[i], rsem.at[i], sib)
            r.start()
            cps.append(r)
        for r in cps:
            r.wait()

    return _pallas(
        body, name=name, in_specs=[_ANY] * n, out_specs=[_ANY] * n,
        out_shape=[jax.ShapeDtypeStruct((N_CHIPS,) + v.shape[2:], v.dtype) for v in views],
        scratch_shapes=[pltpu.SemaphoreType.DMA((n,)), pltpu.SemaphoreType.DMA((n,))],
        compiler_params=pltpu.CompilerParams(has_side_effects=True),
    )(*views)


def _join_halves(halves, *, name):
    n = len(halves)

    def body(*refs):
        outs = refs[n:2 * n]
        ssem, rsem = refs[2 * n:]
        x, y, c = _me()
        sib = (x, y, 1 - c)
        cps = []
        for i in range(n):
            r = _rcopy(outs[i].at[c], outs[i].at[c], ssem.at[i], rsem.at[i], sib)
            r.start()
            cps.append(r)
        for r in cps:
            r.wait()

    return _pallas(
        body, name=name, in_specs=[_ANY] * n, out_specs=[_ANY] * n,
        out_shape=[jax.ShapeDtypeStruct(h.shape, h.dtype) for h in halves],
        scratch_shapes=[pltpu.SemaphoreType.DMA((n,)), pltpu.SemaphoreType.DMA((n,))],
        input_output_aliases={i: i for i in range(n)},
        compiler_params=pltpu.CompilerParams(has_side_effects=True),
    )(*halves)


def _allreduce_small(parts, loss11):
    n = len(parts)
    widths = [p.shape[1] for p in parts]
    total = sum(widths) + LANES

    def body(*refs):
        o_ref, mine, buf, ssem, rsem = refs[n + 1:]
        x, y, c = _me()
        me = 4 * x + 2 * y + c
        off = 0
        for r, w in zip(refs[:n], widths):
            mine[:, off:off + w] = r[...]
            off += w
        mine[:, off:] = jnp.broadcast_to(refs[n][...], (1, LANES))
        buf[me] = mine[...]
        cps = []
        for k in range(1, 8):
            peer = (x ^ (k >> 2), y ^ ((k >> 1) & 1), c ^ (k & 1))
            r = _rcopy(mine, buf.at[me], ssem.at[k - 1], rsem.at[k - 1], peer)
            r.start()
            cps.append(r)
        for k in range(1, 8):
            peer = (x ^ (k >> 2), y ^ ((k >> 1) & 1), c ^ (k & 1))
            pid = 4 * peer[0] + 2 * peer[1] + peer[2]
            _rcopy(mine, buf.at[pid], ssem.at[k - 1], rsem.at[k - 1], peer).wait_recv()
        for r in cps:
            r.wait_send()
        tot = buf[0]
        for d in range(1, 8):
            tot = tot + buf[d]
        o_ref[...] = tot

    vm = pl.BlockSpec(memory_space=pltpu.VMEM)
    return _pallas(
        body, name="allreduce_small", in_specs=[vm] * (n + 1), out_specs=vm,
        out_shape=jax.ShapeDtypeStruct((1, total), F32),
        scratch_shapes=[pltpu.VMEM((1, total), F32), pltpu.VMEM((8, 1, total), F32),
                        pltpu.SemaphoreType.DMA((7,)), pltpu.SemaphoreType.DMA((7,))],
        compiler_params=pltpu.CompilerParams(has_side_effects=True),
    )(*parts, loss11)


def _adamw_small(red, ws, ms, vs):
    n = len(ws)

    def body(*refs):
        red_ref = refs[0]
        outs = refs[1 + 3 * n:]
        off = 0
        for i in range(n):
            w = refs[1 + i].shape[1]
            g = red_ref[:, off:off + w]
            d, m, v = _adamw_vals(refs[1 + i][...], g, refs[1 + n + i][...], refs[1 + 2 * n + i][...])
            for o, val in zip(outs[4 * i:4 * i + 4], (g, d, m, v)):
                o[...] = val
            off += w

    vm = pl.BlockSpec(memory_space=pltpu.VMEM)
    res = _pallas(
        body, name="adamw_small", in_specs=[vm] * (1 + 3 * n), out_specs=[vm] * (4 * n),
        out_shape=[jax.ShapeDtypeStruct(w.shape, F32) for w in ws for _ in range(4)],
    )(red, *ws, *ms, *vs)
    return [res[4 * i:4 * i + 4] for i in range(n)]


def _rope_tables(positions, S):
    pos = positions.reshape(S, 1).astype(F32)
    half = RET_QK // 2
    inv = ROPE_THETA ** (-jnp.arange(half, dtype=F32) / half)
    ang = pos * inv
    cosr = jnp.concatenate([jnp.cos(ang), jnp.cos(ang)], axis=1)
    sinr = jnp.concatenate([-jnp.sin(ang), jnp.sin(ang)], axis=1)
    half = QK_ROPE // 2
    inv = ROPE_THETA ** (-jnp.arange(half, dtype=F32) / half)
    ang = pos * inv
    z = jnp.zeros((S, half), F32)
    c = jnp.concatenate([jnp.cos(ang), jnp.cos(ang), z, z], axis=1)
    s1 = jnp.concatenate([-jnp.sin(ang), z, z, z], axis=1)
    s2 = jnp.concatenate([z, jnp.sin(ang), z, z], axis=1)
    return cosr, sinr, (c, s1, s2)


def _cat_cols(g):
    return jnp.concatenate([g[j] for j in range(N_CHIPS)], axis=1)


def _split_cols(w):
    return jnp.stack(jnp.split(w, N_CHIPS, axis=1))


def kernel(x, positions, norm_mix_g, w_in, ret_norm_g, w_ret_o, q_a_norm_g, w_q_b, kv_a_norm_g, w_kv_b, w_mla_o, w_out, norm_mlp_g, w_up, w_down, norm_f_g, loss_target, m_norm_mix_g, m_w_in, m_ret_norm_g, m_w_ret_o, m_q_a_norm_g, m_w_q_b, m_kv_a_norm_g, m_w_kv_b, m_w_mla_o, m_w_out, m_norm_mlp_g, m_w_up, m_w_down, m_norm_f_g, v_norm_mix_g, v_w_in, v_ret_norm_g, v_w_ret_o, v_q_a_norm_g, v_w_q_b, v_kv_a_norm_g, v_w_kv_b, v_w_mla_o, v_w_out, v_norm_mlp_g, v_w_up, v_w_down, v_norm_f_g):
    S, D = x.shape[1], x.shape[2]
    RVW = w_ret_o.shape[1] * N_CHIPS
    RH = RVW // RET_V
    RQW = RH * RET_QK
    MVW = w_mla_o.shape[1] * N_CHIPS
    MH = MVW // V_HEAD
    QL, KVL = w_q_b.shape[1], w_kv_b.shape[1]
    T_RET = _tile(S, 256)
    T_ATT = _tile(S, 512)

    xs = x.reshape(S, D)
    tgt = loss_target.reshape(S, D)
    cosr, sinr, pe_tabs = _rope_tables(positions, S)
    lgam = jnp.log(1.0 - 2.0 ** (-5.0 - jnp.arange(RH, dtype=F32)))
    lgam = jnp.broadcast_to(lgam[:, None, None], (RH, 8, LANES))

    big = ("w_in", "w_ret_o", "w_q_b", "w_kv_b", "w_mla_o", "w_out", "w_up", "w_down")
    w_sh = dict(w_in=w_in[0].T, w_ret_o=w_ret_o[0], w_q_b=w_q_b[0], w_kv_b=w_kv_b[0], w_mla_o=w_mla_o[0],
                w_out=w_out[0], w_up=w_up[0], w_down=w_down[0])
    m_sh = dict(w_in=m_w_in[0].T, w_ret_o=m_w_ret_o[0], w_q_b=m_w_q_b[0], w_kv_b=m_w_kv_b[0],
                w_mla_o=m_w_mla_o[0], w_out=m_w_out[0], w_up=m_w_up[0], w_down=m_w_down[0])
    v_sh = dict(w_in=v_w_in[0].T, w_ret_o=v_w_ret_o[0], w_q_b=v_w_q_b[0], w_kv_b=v_w_kv_b[0],
                w_mla_o=v_w_mla_o[0], w_out=v_w_out[0], w_up=v_w_up[0], w_down=v_w_down[0])
    col_sharded = ("w_q_b", "w_kv_b", "w_up")
    c_sh = w_in.shape[2]
    c_pad = -(-c_sh // 64) * 64
    place = jnp.stack([lax.axis_index("c"), 2 * lax.axis_index("x") + lax.axis_index("y")]).astype(jnp.int32)

    def whole(k, g):
        g = g.reshape(N_CHIPS, w_sh[k].shape[0], w_sh[k].shape[1])
        if k == "w_up":
            return g
        return _cat_cols(g) if k in col_sharded else g.reshape(-1, g.shape[2])

    first = ("w_in", "w_q_b", "w_kv_b")
    later = ("w_ret_o", "w_mla_o", "w_out", "w_up", "w_down")
    first_bufs = [_cast_into_slot(w_sh[k], place, name="cast_" + k, rows=c_pad if k == "w_in" else None)
                  for k in first]
    first_ssem, first_rsem, first_bufs, first_token = _split_start(
        first_bufs, _gather_ici_plan, 3 * len(first), name="gather_first_start")
    later_bufs = [_cast_into_slot(w_sh[k], place, name="cast_" + k, deps=(first_token,)) for k in later[:-1]]
    first_bufs = _split_wait(first_ssem, first_rsem, first_bufs, later_bufs[-1], _gather_ici_plan,
                             name="gather_first_wait")
    got = _forward_halves(first_bufs, name="gather_first_forward")
    full = {k: whole(k, g) for k, g in zip(first[1:], got[1:])}
    later_bufs.append(_cast_into_slot(w_sh[later[-1]], place, name="cast_" + later[-1], deps=(got[0],)))
    later_ssem, later_rsem, later_bufs, later_token = _split_start(
        later_bufs, _gather_ici_plan, 3 * len(later), name="gather_later_start")

    o_rq, o_rk, o_rv, o_rg = 0, RQW, 2 * RQW, 2 * RQW + RVW
    o_cq = 2 * RQW + 2 * RVW
    o_ckv, o_kpe = o_cq + QL, o_cq + QL + KVL
    o_gr = o_kpe + QK_ROPE
    o_gm = o_gr + D
    n_ret = RH * RET_HEAD_COLS
    off_gret, off_gmla, off_cq, off_ckv = n_ret, n_ret + D, n_ret + 2 * D, n_ret + 2 * D + QL
    n_a = off_ckv + KVL
    runs = []
    for h in range(RH):
        base = h * RET_HEAD_COLS
        runs += [(o_rq + h * RET_QK, RET_QK, base), (o_rk + h * RET_QK, RET_QK, base + RET_QK),
                 (o_rv + h * RET_V, RET_V, base + 2 * RET_QK), (o_rg + h * RET_V, RET_V, base + 2 * RET_QK + RET_V)]
    runs += [(o_gr, D, off_gret), (o_gm, D, off_gmla), (o_cq, QL, off_cq), (o_ckv, KVL, off_ckv),
             (o_kpe, QK_ROPE, n_a)]

    def take(parts, start, width):
        out, lo = [], 0
        for p in parts:
            hi = lo + p.shape[0]
            a, b = max(start, lo), min(start + width, hi)
            if a < b:
                out.append(p[a - lo:b - lo])
            lo = hi
        return out

    wi = [got[0].reshape(N_CHIPS, c_pad, D)[jj, :c_sh] for jj in range(N_CHIPS)]
    here = sorted(runs, key=lambda r: r[2])
    wa = jnp.concatenate([p for s0, w, _ in here[:-1] for p in take(wi, s0, w)], axis=0)
    wkpe = jnp.concatenate(take(wi, o_kpe, QK_ROPE) + [jnp.zeros((LANES - QK_ROPE, D), BF16)], axis=0)
    wq = jnp.pad(full["w_q_b"].reshape(QL, MH, QK_NOPE + QK_ROPE),
                 ((0, 0), (0, 0), (0, LANES - QK_ROPE))).reshape(QL, MH * 2 * LANES)
    wkv = full["w_kv_b"]

    u, rstd0 = _rmsnorm_fwd(xs, norm_mix_g, name="norm_mix")
    proj = _mm(u, wa, mode="nt", outs=[F32], name="in_proj", deps=(later_token,))
    kpe = _mm(u, wkpe, mode="nt", outs=[F32], name="kpe_proj")
    ry, gated, states = _ret_fwd(proj, cosr, sinr, lgam, ret_norm_g, RH, T=T_RET)
    cqn, rstd_q = _rmsnorm_fwd(proj, q_a_norm_g, name="norm_q", width=QL, col=off_cq // QL)
    ckvn, rstd_kv = _rmsnorm_fwd(proj, kv_a_norm_g, name="norm_kv", width=KVL, col=off_ckv // KVL)
    qf, kf, vb = _qkv_proj(cqn, ckvn, wq, wkv, kpe, pe_tabs, MH)
    my, lse2 = _attn_fwd(qf, kf, vb, MH, T=T_ATT)
    later_bufs = _split_wait(later_ssem, later_rsem, later_bufs, my, _gather_ici_plan, name="gather_later_wait")
    later_bufs = _forward_halves(later_bufs, name="gather_later_forward")
    full.update({k: whole(k, g) for k, g in zip(later, later_bufs)})
    y_ret = _mm(gated, full["w_ret_o"], mode="nn", outs=[F32], name="ret_o")
    y_mla = _mm(my, full["w_mla_o"], mode="nn", outs=[F32], name="mla_o")
    merged = _merge_fwd(proj, y_ret, y_mla, D, off_gret, off_gmla)
    h1 = _mm(merged, full["w_out"], mode="nn", outs=[F32], name="out_proj",
             epi=lambda acc, r: (acc + r,), extras=(xs,))
    n1, rstd1 = _rmsnorm_fwd(h1, norm_mlp_g, name="norm_mlp")

    def up_epi(acc):
        r = jnp.maximum(acc, 0.0)
        return acc, r * r

    z, act = _mm(n1, full["w_up"], mode="nn", outs=[F32, BF16], name="up_proj", epi=up_epi)
    h2 = _mm(act, full["w_down"], mode="nn", outs=[F32], name="down_proj",
             epi=lambda acc, r: (acc + r,), extras=(h1,))
    loss11, dh2, g_norm_f = _final_loss(h2, norm_f_g.reshape(1, D), tgt)

    dz = _mm(dh2, full["w_down"], mode="nt", outs=[BF16], name="down_bwd_x",
             epi=lambda acc, zz: (acc * (2.0 * jnp.maximum(zz, 0.0)),), extras=(z,))
    g_w_down = _mm(act, dh2, mode="tn", outs=[BF16], name="down_bwd_w")
    dn1 = _mm(dz, full["w_up"], mode="nt", outs=[F32], name="up_bwd_x")
    g_w_up = _mm(n1, dz, mode="tn", outs=[BF16], name="up_bwd_w", out_shards=True)

    def reduce_begin(tag, names, grads):
        pcs = [g if g.ndim == 3 else g.reshape(N_CHIPS, g.shape[0] // N_CHIPS, g.shape[1]) for g in grads]
        theirs = _swap_halves(pcs, name="swap_" + tag)
        sums = [_sum_pair(p, t, place, name="sum_pair_" + k) for k, p, t in zip(names, pcs, theirs)]
        return pcs, theirs, sums

    def scatter_begin(tag, sums):
        lands = [lax.empty((3,) + s.shape[1:], s.dtype) for s in sums]
        return _split_start(sums + lands, _scatter_plan(len(sums)), 3 * len(sums), name="scatter_" + tag + "_start")

    def swap_begin(tag, grads):
        views = [g if g.ndim == 3 else g.reshape(N_CHIPS, g.shape[0] // N_CHIPS, g.shape[1]) for g in grads]
        views = [v.reshape(N_CHIPS, 2, v.shape[1] // 2, v.shape[2]) for v in views]
        lands = [lax.empty((N_CHIPS,) + v.shape[2:], v.dtype) for v in views]
        return _split_start(views + lands, _swap_plan(len(views)), len(views), name="swap_" + tag + "_start")

    def swap_end(tag, names, handle, after):
        n = len(names)
        bufs = _split_wait(handle[0], handle[1], handle[2], after, _swap_plan(n), name="swap_" + tag + "_wait")
        pcs = [b.reshape(N_CHIPS, 2 * b.shape[2], b.shape[3]) for b in bufs[:n]]
        sums = [_sum_pair(p, t, place, name="sum_pair_" + k) for k, p, t in zip(names, pcs, bufs[n:])]
        return pcs, bufs[n:], sums

    g1 = ("w_up", "w_down")
    swap1 = swap_begin("g1", (g_w_up, g_w_down))
    dh1, g_norm_mlp = _rmsnorm_bwd(dn1, h1, rstd1, norm_mlp_g, name="norm_mlp_bwd", res=dh2, deps=(swap1[3],))
    dmerged = _mm(dh1, full["w_out"], mode="nt", outs=[F32], name="out_bwd_x")
    pcs1, theirs1, sums1 = swap_end("g1", g1, swap1, dmerged)
    ssem1, rsem1, bufs1, token1 = scatter_begin("g1", sums1)
    g_w_out = _mm(merged, dh1, mode="tn", outs=[BF16], name="out_bwd_w", deps=(token1,))
    dproj, dy_ret, dy_mla = _merge_bwd(dmerged, proj, y_ret, y_mla, D, off_gret)
    dgated = _mm(dy_ret, full["w_ret_o"], mode="nt", outs=[F32], name="ret_o_bwd_x")
    g_w_ret_o = _mm(gated, dy_ret, mode="tn", outs=[BF16], name="ret_o_bwd_w")
    dproj, g_ret_norm = _ret_bwd(proj, cosr, sinr, lgam, ret_norm_g, ry, dgated, states, dproj, RH, T=T_RET)
    def delta_epi(acc, o):
        rows = acc.shape[0]
        return acc, [jnp.broadcast_to(jnp.sum(acc[:, lo:lo + V_HEAD] * o[:, lo:lo + V_HEAD], axis=-1, keepdims=True),
                                      (rows, LANES)) for lo in range(0, acc.shape[1], V_HEAD)]

    dob, delta = _mm(dy_mla, full["w_mla_o"], mode="nt", outs=[BF16], name="mla_o_bwd_x", epi=delta_epi,
                     extras=(my,), more_outs=lambda tm, tn: [
                         (jax.ShapeDtypeStruct((MH, S, LANES), F32),
                          pl.BlockSpec((tn // V_HEAD, tm, LANES), lambda i, j, k: (j, i, 0)))])
    g_w_mla_o = _mm(my, dy_mla, mode="tn", outs=[BF16], name="mla_o_bwd_w")
    g2 = ("w_out", "w_ret_o", "w_mla_o")
    swap2 = swap_begin("g2", (g_w_out, g_w_ret_o, g_w_mla_o))
    dq_all, dkv_all, dkpe_h = _attn_bwd(qf, kf, vb, dob, lse2, delta, pe_tabs, MH, T=T_ATT, deps=(swap2[3],))
    pcs2, theirs2, sums2 = swap_end("g2", g2, swap2, dkv_all)
    ssem2, rsem2, bufs2, token2 = scatter_begin("g2", sums2)
    dkpe = _kpe_sum(dkpe_h, pe_tabs, MH)
    dcqn = _mm(dq_all, wq, mode="nt", outs=[F32], name="q_bwd_x", deps=(token2,))
    g_wq = _mm(cqn, dq_all, mode="tn", outs=[BF16], name="q_bwd_w")
    dckvn = _mm(dkv_all, wkv, mode="nt", outs=[F32], name="kv_bwd_x")
    g_wkv = _mm(ckvn, dkv_all, mode="tn", outs=[BF16], name="kv_bwd_w")
    dproj, g_q_a = _rmsnorm_bwd(dcqn, proj, rstd_q, q_a_norm_g, name="norm_q_bwd", into=(dproj, off_cq // QL),
                                width=QL, col=off_cq // QL)
    dproj, g_kv_a = _rmsnorm_bwd(dckvn, proj, rstd_kv, kv_a_norm_g, name="norm_kv_bwd", into=(dproj, off_ckv // KVL),
                                 width=KVL, col=off_ckv // KVL)
    g_wa = _mm(dproj, u, mode="tn", outs=[BF16], name="in_bwd_w")
    g_wkpe = _mm(dkpe, u, mode="tn", outs=[BF16], name="kpe_bwd_w")

    there = sorted(runs)
    g_parts = [g_wa, g_wkpe]
    g_w_in = jnp.stack([jnp.concatenate(
        [p for s0, w, d0 in there for a, b in [(max(s0, jj * c_sh), min(s0 + w, (jj + 1) * c_sh))] if a < b
         for p in take(g_parts, d0 + a - s0, b - a)] + [jnp.zeros((c_pad - c_sh, D), BF16)], axis=0)
        for jj in range(N_CHIPS)])
    gq = g_wq.reshape(QL, MH, 2 * LANES)[:, :, :QK_NOPE + QK_ROPE].reshape(QL, MH * (QK_NOPE + QK_ROPE))
    g3 = ("w_in", "w_q_b", "w_kv_b")
    pcs3, theirs3, sums3 = reduce_begin("g3", g3, (g_w_in, _split_cols(gq), _split_cols(g_wkv)))
    ssem3, rsem3, bufs3, token3 = scatter_begin("g3", sums3)
    du_a = _mm(dproj, wa, mode="nn", outs=[F32], name="in_bwd_x", tk=2816, deps=(token3,))
    du = _mm(dkpe, wkpe, mode="nn", outs=[F32], name="kpe_bwd_x", epi=lambda acc, r: (acc + r,), extras=(du_a,))
    dx, g_norm_mix = _rmsnorm_bwd(du, xs, rstd0, norm_mix_g, name="norm_mix_bwd", res=dh1)

    bufs1 = _split_wait(ssem1, rsem1, bufs1, dx, _scatter_plan(len(g1)), name="scatter_g1_wait")
    bufs2 = _split_wait(ssem2, rsem2, bufs2, dx, _scatter_plan(len(g2)), name="scatter_g2_wait")
    bufs3 = _split_wait(ssem3, rsem3, bufs3, dx, _scatter_plan(len(g3)), name="scatter_g3_wait")
    recv1, recv2, recv3 = bufs1[len(g1):], bufs2[len(g2):], bufs3[len(g3):]
    halves = {}
    for names, pcs, theirs, recv in ((g1, pcs1, theirs1, recv1), (g2, pcs2, theirs2, recv2), (g3, pcs3, theirs3, recv3)):
        for k, p, t, r in zip(names, pcs, theirs, recv):
            halves[k] = _sum_chips(p, t, r, place, name="sum_chips_" + k)
    joined = _join_halves([halves[k] for k in big], name="join_halves")
    g_shard = {k: g.reshape(2 * g.shape[1], g.shape[2]) for k, g in zip(big, joined)}

    small = ("norm_mix_g", "ret_norm_g", "q_a_norm_g", "kv_a_norm_g", "norm_mlp_g", "norm_f_g")
    g_small = [g_norm_mix, g_ret_norm, g_q_a, g_kv_a, g_norm_mlp, g_norm_f]
    red = _allreduce_small(g_small, loss11)
    loss = red[0, red.shape[1] - 1]
    w_small = [norm_mix_g, ret_norm_g, q_a_norm_g, kv_a_norm_g, norm_mlp_g, norm_f_g]
    m_small = [m_norm_mix_g, m_ret_norm_g, m_q_a_norm_g, m_kv_a_norm_g, m_norm_mlp_g, m_norm_f_g]
    v_small = [v_norm_mix_g, v_ret_norm_g, v_q_a_norm_g, v_kv_a_norm_g, v_norm_mlp_g, v_norm_f_g]
    row = lambda a: a.reshape(1, -1)
    upd = _adamw_small(red, [row(a) for a in w_small], [row(a) for a in m_small], [row(a) for a in v_small])
    out_g, out_d, out_m, out_v = {}, {}, {}, {}
    for k, wv, (g_, d_, m_, v_) in zip(small, w_small, upd):
        out_g[k], out_d[k], out_m[k], out_v[k] = [a.reshape(wv.shape) for a in (g_, d_, m_, v_)]

    for k in big:
        res = _rows_call(lambda w, g, m, v: (g,) + _adamw_vals(w, g, m, v),
                         [w_sh[k], g_shard[k], m_sh[k], v_sh[k]], [F32] * 4, name="adamw_" + k)
        if k == "w_in":
            res = [r.T for r in res]
        out_g[k], out_d[k], out_m[k], out_v[k] = [r[None] for r in res]

    order = ("norm_mix_g", "w_in", "ret_norm_g", "w_ret_o", "q_a_norm_g", "w_q_b", "kv_a_norm_g", "w_kv_b",
             "w_mla_o", "w_out", "norm_mlp_g", "w_up", "w_down", "norm_f_g")
    return (loss, dx.reshape(1, S, D), *[out_g[k] for k in order], *[out_d[k] for k in order],
            *[out_m[k] for k in order], *[out_v[k] for k in order])
```

```python
import math

import jax
import jax.numpy as jnp
from jax import lax
from jax.experimental import pallas as pl
from jax.experimental.pallas import tpu as pltpu

F32 = jnp.float32
BF16 = jnp.bfloat16

EPS = 1e-6
ROPE_THETA = 10000.0
CHUNK = 64
RET_QK = 128
RET_V = 256
RET_HEAD_COLS = 2 * RET_QK + 2 * RET_V
QK_NOPE = 128
QK_ROPE = 64
V_HEAD = 128
LANES = 128
LOG2E = math.log2(math.e)

ADAM_LR = 0.001
ADAM_B1 = 0.9
ADAM_B2 = 0.999
ADAM_EPS = 1e-08
ADAM_WD = 0.01
ADAM_STEP = 10

N_CHIPS = 4
VMEM_LIMIT = 56 * 1024 * 1024
MESH = pl.DeviceIdType.MESH
NEG = -1e30


def _pallas(body, **kw):
    return pl.pallas_call(body, **kw)


def _params(sem=None):
    return pltpu.CompilerParams(dimension_semantics=sem, vmem_limit_bytes=VMEM_LIMIT)


def _tile(n, want):
    t = min(n, want)
    while n % t:
        t //= 2
    return t


_ANY = pl.BlockSpec(memory_space=pl.ANY)
TN_BF16_TK = 4096


def _mm(a, b, *, mode, outs, name, epi=None, extras=(), deps=(), out_shards=False, more_outs=None,
        tm=1024, tn=1024, tk=2048):
    shards = b.shape[0] if b.ndim == 3 else 1
    brows, bcols = b.shape[-2], b.shape[-1] * shards
    if mode == "nn":
        (M, K), N = a.shape, bcols
    elif mode == "nt":
        (M, K), N = a.shape, brows
    else:
        (K, M), N = a.shape, bcols
    if mode == "tn" and a.dtype == BF16 and b.dtype == BF16:
        tk = max(tk, TN_BF16_TK)
    tm = _tile(M, tm)
    tn = _tile(N // (shards if mode == "nn" else 1) // (N_CHIPS if out_shards else 1), tn)
    tk = _tile(K // (shards if mode == "nt" else 1), tk)
    nk = K // tk
    if mode == "nn":
        a_spec = pl.BlockSpec((tm, tk), lambda i, j, k: (i, k))
        dims = (((1,), (0,)), ((), ()))
        if shards > 1:
            per = N // shards // tn
            b_spec = pl.BlockSpec((None, tk, tn), lambda i, j, k: (j // per, k, j % per))
        else:
            b_spec = pl.BlockSpec((tk, tn), lambda i, j, k: (k, j))
    elif mode == "nt":
        a_spec = pl.BlockSpec((tm, tk), lambda i, j, k: (i, k))
        dims = (((1,), (1,)), ((), ()))
        if shards > 1:
            per = K // shards // tk
            b_spec = pl.BlockSpec((None, tn, tk), lambda i, j, k: (k // per, j, k % per))
        else:
            b_spec = pl.BlockSpec((tn, tk), lambda i, j, k: (j, k))
    else:
        assert shards == 1
        a_spec = pl.BlockSpec((tk, tm), lambda i, j, k: (k, i))
        b_spec = pl.BlockSpec((tk, tn), lambda i, j, k: (k, j))
        dims = (((0,), (0,)), ((), ()))
    if out_shards:
        assert not extras
        oper = N // N_CHIPS // tn
        o_spec = pl.BlockSpec((None, tm, tn), lambda i, j, k: (j // oper, i, j % oper))
        o_shape = (N_CHIPS, M, N // N_CHIPS)
    else:
        o_spec = pl.BlockSpec((tm, tn), lambda i, j, k: (i, j))
        o_shape = (M, N)
    more = [] if more_outs is None else more_outs(tm, tn)
    n_ex, n_out, n_dep = len(extras), len(outs) + len(more), len(deps)
    if epi is None:
        epi = lambda acc: (acc,)

    def body(*refs):
        a_ref, b_ref = refs[0], refs[1]
        ex_refs = refs[2:2 + n_ex]
        o_refs = refs[2 + n_ex + n_dep:2 + n_ex + n_dep + n_out]
        part = lax.dot_general(a_ref[...].astype(BF16), b_ref[...].astype(BF16), dims,
                               preferred_element_type=F32)

        def finish(acc):
            vals = epi(acc, *[r[...] for r in ex_refs])
            for r, v in zip(o_refs, vals):
                if isinstance(v, (list, tuple)):
                    for lead, piece in enumerate(v):
                        r[lead] = piece.astype(r.dtype)
                else:
                    r[...] = v.astype(r.dtype)

        if nk == 1:
            finish(part)
        else:
            acc_ref = refs[-1]
            k = pl.program_id(2)

            @pl.when(k == 0)
            def _():
                acc_ref[...] = part

            @pl.when(k > 0)
            def _():
                acc_ref[...] += part

            @pl.when(k == nk - 1)
            def _():
                finish(acc_ref[...])

    res = _pallas(
        body, name=name, grid=(M // tm, N // tn, nk),
        in_specs=[a_spec, b_spec] + [o_spec] * n_ex + [_ANY] * n_dep,
        out_specs=[o_spec] * len(outs) + [spec for _, spec in more],
        out_shape=[jax.ShapeDtypeStruct(o_shape, d) for d in outs] + [shape for shape, _ in more],
        scratch_shapes=[pltpu.VMEM((tm, tn), F32)] if nk > 1 else [],
        compiler_params=_params(("parallel", "parallel", "arbitrary")),
    )(a, b, *extras, *deps)
    return res[0] if n_out == 1 else res


def _rmsnorm_fwd(x, g, *, name, width=None, col=0, tr=256):
    S = x.shape[0]
    W = x.shape[1] if width is None else width
    tr = _tile(S, tr)

    def body(x_ref, g_ref, y_ref, r_ref):
        xv = x_ref[...]
        rstd = lax.rsqrt(jnp.mean(xv * xv, axis=-1, keepdims=True) + EPS)
        y_ref[...] = (xv * rstd * g_ref[...]).astype(BF16)
        r_ref[...] = rstd

    return _pallas(
        body, name=name, grid=(S // tr,),
        in_specs=[pl.BlockSpec((tr, W), lambda i: (i, col)), pl.BlockSpec((1, W), lambda i: (0, 0))],
        out_specs=[pl.BlockSpec((tr, W), lambda i: (i, 0)), pl.BlockSpec((tr, 1), lambda i: (i, 0))],
        out_shape=[jax.ShapeDtypeStruct((S, W), BF16), jax.ShapeDtypeStruct((S, 1), F32)],
        compiler_params=_params(("parallel",)),
    )(x, g)


def _rmsnorm_bwd(dy, x, rstd, g, *, name, res=None, into=None, deps=(), bf16_copy=0, width=None, col=0, tr=256):
    S = x.shape[0]
    W = x.shape[1] if width is None else width
    tr = _tile(S, tr)
    has_res = res is not None

    def body(*refs):
        dy_ref, x_ref, r_ref, g_ref = refs[:4]
        dx_ref, dg_ref = refs[-2 - bf16_copy], refs[-1 - bf16_copy]
        rstd_v = r_ref[...]
        xhat = x_ref[...] * rstd_v
        dyv = dy_ref[...].astype(F32)
        dyg = dyv * g_ref[...]
        dx = rstd_v * (dyg - xhat * jnp.mean(dyg * xhat, axis=-1, keepdims=True))
        if has_res:
            dx = dx + refs[4][...]
        dx_ref[...] = dx.astype(dx_ref.dtype)
        if bf16_copy:
            refs[-1][...] = dx.astype(BF16)
        part = jnp.sum(dyv * xhat, axis=0, keepdims=True)

        @pl.when(pl.program_id(0) == 0)
        def _():
            dg_ref[...] = part

        @pl.when(pl.program_id(0) > 0)
        def _():
            dg_ref[...] += part

    row = pl.BlockSpec((tr, W), lambda i: (i, 0))
    ins = [dy, x, rstd, g] + ([res] if has_res else [])
    in_specs = [row, pl.BlockSpec((tr, W), lambda i: (i, col)), pl.BlockSpec((tr, 1), lambda i: (i, 0)),
                pl.BlockSpec((1, W), lambda i: (0, 0))] + ([row] if has_res else [])
    if into is None:
        dx_spec, dx_shape, alias = row, jax.ShapeDtypeStruct((S, W), F32), {}
    else:
        buf, col_out = into
        ins.append(buf)
        in_specs.append(_ANY)
        dx_spec = pl.BlockSpec((tr, W), lambda i: (i, col_out))
        dx_shape = jax.ShapeDtypeStruct(buf.shape, buf.dtype)
        alias = {len(ins) - 1: 0}
    ins += list(deps)
    in_specs += [_ANY] * len(deps)
    return _pallas(
        body, name=name, grid=(S // tr,), in_specs=in_specs,
        out_specs=[dx_spec, pl.BlockSpec((1, W), lambda i: (0, 0))] + [row] * bf16_copy,
        out_shape=[dx_shape, jax.ShapeDtypeStruct((1, W), F32)] + [jax.ShapeDtypeStruct((S, W), BF16)] * bf16_copy,
        input_output_aliases=alias,
        compiler_params=_params(("arbitrary",)),
    )(*ins)


def _final_loss(h2, g, target, *, tr=256):
    S, D = h2.shape
    tr = _tile(S, tr)

    def body(h_ref, g_ref, t_ref, loss_ref, dh_ref, dhb_ref, dg_ref):
        hv = h_ref[...]
        rstd = lax.rsqrt(jnp.mean(hv * hv, axis=-1, keepdims=True) + EPS)
        xhat = hv * rstd
        e = xhat * g_ref[...] - t_ref[...]
        lpart = (0.5 / D) * jnp.sum(jnp.sum(e * e, axis=-1, keepdims=True), axis=0, keepdims=True)
        dy = e * (1.0 / D)
        dyg = dy * g_ref[...]
        dh = rstd * (dyg - xhat * jnp.mean(dyg * xhat, axis=-1, keepdims=True))
        dh_ref[...] = dh
        dhb_ref[...] = dh.astype(BF16)
        gpart = jnp.sum(dy * xhat, axis=0, keepdims=True)

        @pl.when(pl.program_id(0) == 0)
        def _():
            loss_ref[...] = lpart
            dg_ref[...] = gpart

        @pl.when(pl.program_id(0) > 0)
        def _():
            loss_ref[...] += lpart
            dg_ref[...] += gpart

    row = pl.BlockSpec((tr, D), lambda i: (i, 0))
    vec = pl.BlockSpec((1, D), lambda i: (0, 0))
    return _pallas(
        body, name="final_loss", grid=(S // tr,), in_specs=[row, vec, row],
        out_specs=[pl.BlockSpec((1, 1), lambda i: (0, 0)), row, row, vec],
        out_shape=[jax.ShapeDtypeStruct((1, 1), F32), jax.ShapeDtypeStruct((S, D), F32),
                   jax.ShapeDtypeStruct((S, D), BF16), jax.ShapeDtypeStruct((1, D), F32)],
        compiler_params=_params(("arbitrary",)),
    )(h2, g, target)


def _sigmoid(v):
    return 1.0 / (1.0 + jnp.exp(-v))


def _merge_fwd(proj, y_ret, y_mla, D, off_gret, off_gmla, *, tr=256, tc=1024):
    S = y_ret.shape[0]
    tr, tc = _tile(S, tr), _tile(D, tc)
    b_ret, b_mla = off_gret // tc, off_gmla // tc

    def body(gr_ref, gm_ref, yr_ref, ym_ref, o_ref):
        o_ref[...] = (_sigmoid(gr_ref[...]) * yr_ref[...] + _sigmoid(gm_ref[...]) * ym_ref[...]).astype(BF16)

    blk = pl.BlockSpec((tr, tc), lambda i, j: (i, j))
    return _pallas(
        body, name="merge_fwd", grid=(S // tr, D // tc),
        in_specs=[pl.BlockSpec((tr, tc), lambda i, j: (i, b_ret + j)),
                  pl.BlockSpec((tr, tc), lambda i, j: (i, b_mla + j)), blk, blk],
        out_specs=blk, out_shape=jax.ShapeDtypeStruct((S, D), BF16),
        compiler_params=_params(("parallel", "parallel")),
    )(proj, proj, y_ret, y_mla)


def _merge_bwd(dmerged, proj, y_ret, y_mla, D, off_gret, *, tr=256):
    S = y_ret.shape[0]
    tr = _tile(S, tr)
    b0 = off_gret // D

    def body(dm_ref, g_ref, yr_ref, ym_ref, dp_ref, dyr_ref, dym_ref):
        dm = dm_ref[...]
        sg = _sigmoid(g_ref[...])

        @pl.when(pl.program_id(1) == 0)
        def _():
            dyr_ref[...] = (dm * sg).astype(BF16)
            dp_ref[...] = (dm * yr_ref[...] * sg * (1.0 - sg)).astype(BF16)

        @pl.when(pl.program_id(1) == 1)
        def _():
            dym_ref[...] = (dm * sg).astype(BF16)
            dp_ref[...] = (dm * ym_ref[...] * sg * (1.0 - sg)).astype(BF16)

    blk = pl.BlockSpec((tr, D), lambda i, j: (i, 0))
    return _pallas(
        body, name="merge_bwd", grid=(S // tr, 2),
        in_specs=[blk, pl.BlockSpec((tr, D), lambda i, j: (i, b0 + j)), blk, blk],
        out_specs=[pl.BlockSpec((tr, D), lambda i, j: (i, b0 + j)), blk, blk],
        out_shape=[jax.ShapeDtypeStruct(proj.shape, BF16), jax.ShapeDtypeStruct((S, D), BF16),
                   jax.ShapeDtypeStruct((S, D), BF16)],
        compiler_params=_params(("parallel", "arbitrary")),
    )(dmerged, proj, y_ret, y_mla)


def _rope128(t, cos_full, sin_signed):
    return t * cos_full + pltpu.roll(t, RET_QK // 2, 1) * sin_signed


def _rope128_t(d, cos_full, sin_signed):
    return d * cos_full + pltpu.roll(d * sin_signed, RET_QK // 2, 1)


def _ret_consts(lg, T):
    pos = lax.broadcasted_iota(jnp.int32, (T, 1), 0).astype(F32)
    qd = jnp.exp(lg * (pos + 1.0))
    kd = jnp.exp(lg * (T - 1.0 - pos))
    n = lax.broadcasted_iota(jnp.int32, (T, T), 0)
    m = lax.broadcasted_iota(jnp.int32, (T, T), 1)
    vis = (m // CHUNK) <= (n // CHUNK)
    dist = jnp.abs(n - m).astype(F32)
    decay = jnp.where(vis, jnp.exp(lg * dist), 0.0)
    cdec = jnp.exp(lg * float(T))
    return qd, kd, decay, cdec


def _dot(a, b, dims):
    return lax.dot_general(a.astype(BF16), b.astype(BF16), (dims, ((), ())), preferred_element_type=F32)


NN = ((1,), (0,))
NT = ((1,), (1,))
TN = ((0,), (0,))
_RQ = slice(0, RET_QK)
_RK = slice(RET_QK, 2 * RET_QK)
_RV = slice(2 * RET_QK, 2 * RET_QK + RET_V)
_RG = slice(2 * RET_QK + RET_V, RET_HEAD_COLS)


RET_GROUP = 4


def _head_cols(h, part):
    return slice(h * RET_HEAD_COLS + part.start, h * RET_HEAD_COLS + part.stop)


def _ret_fwd(proj, cosr, sinr, lgam, gain, RH, *, T):
    S = proj.shape[0]
    nb = S // T
    G = _tile(RH, RET_GROUP)
    heads = range(G)
    scale = RET_QK ** -0.5

    def body(p_ref, cos_ref, sin_ref, lg_ref, gain_ref, ry_ref, gated_ref, st_ref, state):
        b = pl.program_id(1)

        @pl.when(b == 0)
        def _():
            state[...] = jnp.zeros_like(state)

        consts = [_ret_consts(lg_ref[h, 0:1, 0:1], T) for h in heads]
        cosv, sinv = cos_ref[...], sin_ref[...]
        q = [_rope128(p_ref[:, _head_cols(h, _RQ)], cosv, sinv) for h in heads]
        k = [_rope128(p_ref[:, _head_cols(h, _RK)], cosv, sinv) * scale for h in heads]
        v = [p_ref[:, _head_cols(h, _RV)] for h in heads]
        sprev = [state[h] for h in heads]
        for h in heads:
            st_ref[h] = sprev[h]
        a = [_dot(q[h], k[h], NT) for h in heads]
        qs = [_dot(q[h] * consts[h][0], sprev[h], NN) for h in heads]
        kv = [_dot(k[h] * consts[h][1], v[h], TN) for h in heads]
        o = [_dot(a[h] * consts[h][2], v[h], NN) + qs[h] for h in heads]
        for h in heads:
            state[h] = sprev[h] * consts[h][3] + kv[h]
            vals = slice(h * RET_V, (h + 1) * RET_V)
            ry_ref[:, vals] = o[h]
            mu = jnp.mean(o[h], axis=-1, keepdims=True)
            oc = o[h] - mu
            var = jnp.mean(oc * oc, axis=-1, keepdims=True)
            t = oc * lax.rsqrt(var + EPS) * gain_ref[:, vals]
            gv = p_ref[:, _head_cols(h, _RG)]
            gated_ref[:, vals] = (t * (gv * _sigmoid(gv))).astype(BF16)

    return _pallas(
        body, name="ret_fwd", grid=(RH // G, nb),
        in_specs=[pl.BlockSpec((T, G * RET_HEAD_COLS), lambda h, b: (b, h)),
                  pl.BlockSpec((T, RET_QK), lambda h, b: (b, 0)),
                  pl.BlockSpec((T, RET_QK), lambda h, b: (b, 0)),
                  pl.BlockSpec((G, 8, LANES), lambda h, b: (h, 0, 0)),
                  pl.BlockSpec((1, G * RET_V), lambda h, b: (0, h))],
        out_specs=[pl.BlockSpec((T, G * RET_V), lambda h, b: (b, h)),
                   pl.BlockSpec((T, G * RET_V), lambda h, b: (b, h)),
                   pl.BlockSpec((G, None, RET_QK, RET_V), lambda h, b: (h, b, 0, 0))],
        out_shape=[jax.ShapeDtypeStruct((S, RH * RET_V), F32), jax.ShapeDtypeStruct((S, RH * RET_V), BF16),
                   jax.ShapeDtypeStruct((RH, nb, RET_QK, RET_V), F32)],
        scratch_shapes=[pltpu.VMEM((G, RET_QK, RET_V), F32)],
        compiler_params=_params(("parallel", "arbitrary")),
    )(proj, cosr, sinr, lgam, gain)


def _ret_bwd(proj, cosr, sinr, lgam, gain, ry, dgated, states, dproj, RH, *, T):
    S = proj.shape[0]
    nb = S // T
    G = _tile(RH, RET_GROUP)
    heads = range(G)
    scale = RET_QK ** -0.5

    def body(p_ref, cos_ref, sin_ref, lg_ref, gain_ref, ry_ref, dg_ref, st_ref, _, dp_ref, dgain_ref, dstate):
        b = pl.program_id(1)

        @pl.when(b == 0)
        def _():
            dstate[...] = jnp.zeros_like(dstate)

        consts = [_ret_consts(lg_ref[h, 0:1, 0:1], T) for h in heads]
        qd, kd, decay, cdec = [[c[i] for c in consts] for i in range(4)]
        cosv, sinv = cos_ref[...], sin_ref[...]
        q = [_rope128(p_ref[:, _head_cols(h, _RQ)], cosv, sinv) for h in heads]
        k = [_rope128(p_ref[:, _head_cols(h, _RK)], cosv, sinv) * scale for h in heads]
        v = [p_ref[:, _head_cols(h, _RV)] for h in heads]
        sprev = [st_ref[h] for h in heads]
        ds_new = [dstate[h] for h in heads]
        a = [_dot(q[h], k[h], NT) for h in heads]
        do, gparts = [], []
        for h in heads:
            vals = slice(h * RET_V, (h + 1) * RET_V)
            o = ry_ref[:, vals]
            mu = jnp.mean(o, axis=-1, keepdims=True)
            oc = o - mu
            rstd = lax.rsqrt(jnp.mean(oc * oc, axis=-1, keepdims=True) + EPS)
            ryn = oc * rstd
            gainv = gain_ref[:, vals]
            gv = p_ref[:, _head_cols(h, _RG)]
            sg = _sigmoid(gv)
            dgt = dg_ref[:, vals]
            dt = dgt * (gv * sg)
            dp_ref[:, _head_cols(h, _RG)] = (dgt * (ryn * gainv) * (sg * (1.0 + gv * (1.0 - sg)))).astype(BF16)
            gparts.append(jnp.sum(dt * ryn, axis=0, keepdims=True))
            dryn = dt * gainv
            do.append(rstd * (dryn - jnp.mean(dryn, axis=-1, keepdims=True)
                              - ryn * jnp.mean(dryn * ryn, axis=-1, keepdims=True)))
        gpart = jnp.concatenate(gparts, axis=1)

        @pl.when(b == 0)
        def _():
            dgain_ref[...] = gpart

        @pl.when(b > 0)
        def _():
            dgain_ref[...] += gpart

        dpm = [_dot(do[h], v[h], NT) for h in heads]
        dq_s = [_dot(do[h], sprev[h], NT) for h in heads]
        dk_s = [_dot(v[h], ds_new[h], NT) for h in heads]
        dv_s = [_dot(k[h] * kd[h], ds_new[h], NN) for h in heads]
        dst = [_dot(q[h] * qd[h], do[h], TN) for h in heads]
        a = [a[h] * decay[h] for h in heads]
        dpm = [dpm[h] * decay[h] for h in heads]
        dv = [_dot(a[h], do[h], TN) + dv_s[h] for h in heads]
        dq = [_dot(dpm[h], k[h], NN) + dq_s[h] * qd[h] for h in heads]
        dk = [(_dot(dpm[h], q[h], TN) + dk_s[h] * kd[h]) * scale for h in heads]
        for h in heads:
            dstate[h] = ds_new[h] * cdec[h] + dst[h]
            dp_ref[:, _head_cols(h, _RV)] = dv[h].astype(BF16)
            dp_ref[:, _head_cols(h, _RQ)] = _rope128_t(dq[h], cosv, sinv).astype(BF16)
            dp_ref[:, _head_cols(h, _RK)] = _rope128_t(dk[h], cosv, sinv).astype(BF16)

    rb = lambda b: nb - 1 - b
    return _pallas(
        body, name="ret_bwd", grid=(RH // G, nb),
        in_specs=[pl.BlockSpec((T, G * RET_HEAD_COLS), lambda h, b: (rb(b), h)),
                  pl.BlockSpec((T, RET_QK), lambda h, b: (rb(b), 0)),
                  pl.BlockSpec((T, RET_QK), lambda h, b: (rb(b), 0)),
                  pl.BlockSpec((G, 8, LANES), lambda h, b: (h, 0, 0)),
                  pl.BlockSpec((1, G * RET_V), lambda h, b: (0, h)),
                  pl.BlockSpec((T, G * RET_V), lambda h, b: (rb(b), h)),
                  pl.BlockSpec((T, G * RET_V), lambda h, b: (rb(b), h)),
                  pl.BlockSpec((G, None, RET_QK, RET_V), lambda h, b: (h, rb(b), 0, 0)),
                  _ANY],
        out_specs=[pl.BlockSpec((T, G * RET_HEAD_COLS), lambda h, b: (rb(b), h)),
                   pl.BlockSpec((1, G * RET_V), lambda h, b: (0, h))],
        out_shape=[jax.ShapeDtypeStruct(dproj.shape, dproj.dtype), jax.ShapeDtypeStruct((1, RH * RET_V), F32)],
        scratch_shapes=[pltpu.VMEM((G, RET_QK, RET_V), F32)],
        input_output_aliases={8: 0},
        compiler_params=_params(("parallel", "arbitrary")),
    )(proj, cosr, sinr, lgam, gain, ry, dgated, states, dproj)


def _rope_pe(t, c, s1, s2):
    return t * c + pltpu.roll(t, LANES - QK_ROPE // 2, 1) * s1 + pltpu.roll(t, QK_ROPE // 2, 1) * s2


def _rope_pe_t(d, c, s1, s2):
    return d * c + pltpu.roll(d * s1, QK_ROPE // 2, 1) + pltpu.roll(d * s2, LANES - QK_ROPE // 2, 1)


ATTN_C2 = (QK_NOPE + QK_ROPE) ** -0.5 * LOG2E


def _qkv_proj(cqn, ckvn, wq, wkv, kpe, tabs, MH, *, tm=512, heads=4):
    S = cqn.shape[0]
    tm = _tile(S, tm)
    hb = _tile(MH, heads)
    W = 2 * LANES
    c_t, s1_t, s2_t = tabs

    def body(cq_ref, ckv_ref, wq_ref, wkv_ref, kpe_ref, c_ref, s1_ref, s2_ref, qf_ref, kf_ref, v_ref):
        c, s1, s2 = c_ref[...], s1_ref[...], s2_ref[...]
        q = _dot(cq_ref[...], wq_ref[...], NN)
        kv = _dot(ckv_ref[...], wkv_ref[...], NN)
        kper = _rope_pe(kpe_ref[...], c, s1, s2).astype(BF16)
        for h in range(hb):
            lo, mid, hi = h * W, h * W + QK_NOPE, (h + 1) * W
            qf_ref[:, lo:mid] = (q[:, lo:mid] * ATTN_C2).astype(BF16)
            qf_ref[:, mid:hi] = (_rope_pe(q[:, mid:hi], c, s1, s2) * ATTN_C2).astype(BF16)
            kf_ref[:, lo:mid] = kv[:, lo:mid].astype(BF16)
            kf_ref[:, mid:hi] = kper
            v_ref[:, h * V_HEAD:(h + 1) * V_HEAD] = kv[:, mid:hi].astype(BF16)

    tab = pl.BlockSpec((tm, LANES), lambda i, j: (i, 0))
    grp = pl.BlockSpec((tm, hb * W), lambda i, j: (i, j))
    return _pallas(
        body, name="qkv_proj", grid=(S // tm, MH // hb),
        in_specs=[pl.BlockSpec((tm, cqn.shape[1]), lambda i, j: (i, 0)),
                  pl.BlockSpec((tm, ckvn.shape[1]), lambda i, j: (i, 0)),
                  pl.BlockSpec((wq.shape[0], hb * W), lambda i, j: (0, j)),
                  pl.BlockSpec((wkv.shape[0], hb * W), lambda i, j: (0, j)), tab, tab, tab, tab],
        out_specs=[grp, grp, pl.BlockSpec((tm, hb * V_HEAD), lambda i, j: (i, j))],
        out_shape=[jax.ShapeDtypeStruct((S, MH * W), BF16)] * 2 + [jax.ShapeDtypeStruct((S, MH * V_HEAD), BF16)],
        compiler_params=_params(("parallel", "parallel")),
    )(cqn, ckvn, wq, wkv, kpe, c_t, s1_t, s2_t)


def _chunk_mask(T):
    n = lax.broadcasted_iota(jnp.int32, (T, T), 0)
    m = lax.broadcasted_iota(jnp.int32, (T, T), 1)
    return (m // CHUNK) <= (n // CHUNK)


def _lanes_to(v, width):
    return jnp.tile(v, (1, width // LANES))


def _attn_fwd(qf, kf, vb, MH, *, T):
    S = qf.shape[0]
    nt = S // T

    def body(q_ref, k_ref, v_ref, o_ref, ob_ref, lse_ref, m_sc, l_sc, acc_sc, s_a, s_b):
        qi = pl.program_id(1)
        m_sc[...] = jnp.full_like(m_sc, NEG)
        l_sc[...] = jnp.zeros_like(l_sc)
        acc_sc[...] = jnp.zeros_like(acc_sc)

        def rows_of(kt):
            return pl.ds(pl.multiple_of(kt * T, T), T)

        def scores(kt):
            return _dot(q_ref[...], k_ref[rows_of(kt), :], NT)

        def update(s, kt):
            m_prev = m_sc[...]
            m_new = jnp.maximum(m_prev, jnp.max(s, axis=-1, keepdims=True))
            alpha = jnp.exp2(m_prev - m_new)
            p = jnp.exp2(s - _lanes_to(m_new, T))
            l_sc[...] = alpha * l_sc[...] + jnp.sum(p, axis=-1, keepdims=True)
            acc_sc[...] = alpha * acc_sc[...] + _dot(p, v_ref[rows_of(kt), :], NN)
            m_sc[...] = m_new

        update(jnp.where(_chunk_mask(T), scores(qi), NEG), qi)

        @pl.when(qi % 2 == 1)
        def _():
            update(scores(qi - 1), qi - 1)

        pairs = qi // 2

        @pl.when(pairs > 0)
        def _():
            s_a[...] = scores(0)

        def pair(j, carry):
            t0 = 2 * j
            s_b[...] = scores(t0 + 1)
            update(s_a[...], t0)
            s_a[...] = scores(jnp.minimum(t0 + 2, 2 * pairs - 1))
            update(s_b[...], t0 + 1)
            return carry

        lax.fori_loop(0, pairs, pair, 0)
        l = l_sc[...]
        o = acc_sc[...] / l
        o_ref[...] = o
        ob_ref[...] = o.astype(BF16)
        lse_ref[...] = m_sc[...] + jnp.log(l) * LOG2E

    return _pallas(
        body, name="attn_fwd", grid=(MH, nt),
        in_specs=[pl.BlockSpec((T, 2 * LANES), lambda h, i: (i, h)),
                  pl.BlockSpec((S, 2 * LANES), lambda h, i: (0, h)),
                  pl.BlockSpec((S, LANES), lambda h, i: (0, h))],
        out_specs=[pl.BlockSpec((T, LANES), lambda h, i: (i, h)), pl.BlockSpec((T, LANES), lambda h, i: (i, h)),
                   pl.BlockSpec((None, T, LANES), lambda h, i: (h, i, 0))],
        out_shape=[jax.ShapeDtypeStruct((S, MH * LANES), F32), jax.ShapeDtypeStruct((S, MH * LANES), BF16),
                   jax.ShapeDtypeStruct((MH, S, LANES), F32)],
        scratch_shapes=[pltpu.VMEM((T, LANES), F32), pltpu.VMEM((T, LANES), F32), pltpu.VMEM((T, LANES), F32),
                        pltpu.VMEM((T, T), F32), pltpu.VMEM((T, T), F32)],
        compiler_params=_params(("parallel", "parallel")),
    )(qf, kf, vb)


def _attn_bwd(qf, kf, vb, dob, lse2, delta, tabs, MH, *, T, deps=()):
    S = qf.shape[0]
    nt = S // T
    scale = (QK_NOPE + QK_ROPE) ** -0.5
    n_dep = len(deps)

    def body(q_ref, k_ref, v_ref, do_ref, lse_ref, dl_ref, c_ref, s1_ref, s2_ref, *rest):
        dqa_ref, dkv_ref, dkpe_ref, dq_ref, dk_sc, dv_sc, s_a, dp_a, s_b, dp_b = rest[n_dep:]
        kj = pl.program_id(1)

        @pl.when(kj == 0)
        def _():
            dq_ref[...] = jnp.zeros_like(dq_ref)

        dk_sc[...] = jnp.zeros_like(dk_sc)
        dv_sc[...] = jnp.zeros_like(dv_sc)

        def rows_of(qt):
            return pl.ds(pl.multiple_of(qt * T, T), T)

        def products(qt):
            rows = rows_of(qt)
            return _dot(q_ref[rows, :], k_ref[...], NT), _dot(do_ref[rows, :], v_ref[...], NT)

        def update(s, dp, qt):
            rows = rows_of(qt)
            q, dov = q_ref[rows, :], do_ref[rows, :]
            p = jnp.exp2(s - _lanes_to(lse_ref[rows, :], T))
            ds = p * (dp - _lanes_to(dl_ref[rows, :], T))
            dv_sc[...] += _dot(p, dov, TN)
            dk_sc[...] += _dot(ds, q, TN)
            dq_ref[rows, :] += _dot(ds, k_ref[...], NN)

        s, dp = products(kj)
        update(jnp.where(_chunk_mask(T), s, NEG), dp, kj)
        rest = nt - 1 - kj

        @pl.when(rest % 2 == 1)
        def _():
            s1, dp1 = products(nt - 1)
            update(s1, dp1, nt - 1)

        pairs = rest // 2

        @pl.when(pairs > 0)
        def _():
            s_a[...], dp_a[...] = products(kj + 1)

        def pair(j, carry):
            t0 = kj + 1 + 2 * j
            s_b[...], dp_b[...] = products(t0 + 1)
            update(s_a[...], dp_a[...], t0)
            s_a[...], dp_a[...] = products(jnp.minimum(t0 + 2, kj + 2 * pairs))
            update(s_b[...], dp_b[...], t0 + 1)
            return carry

        lax.fori_loop(0, pairs, pair, 0)
        dkv_ref[:, :QK_NOPE] = (dk_sc[:, :QK_NOPE] * (1.0 / LOG2E)).astype(BF16)
        dkv_ref[:, QK_NOPE:] = dv_sc[...].astype(BF16)
        dkpe_ref[...] = dk_sc[:, QK_NOPE:] * (1.0 / LOG2E)

        @pl.when(kj == nt - 1)
        def _():
            dqa_ref[:, :QK_NOPE] = (dq_ref[:, :QK_NOPE] * scale).astype(BF16)
            dqa_ref[:, QK_NOPE:] = (_rope_pe_t(dq_ref[:, QK_NOPE:], c_ref[...], s1_ref[...], s2_ref[...])
                                    * scale).astype(BF16)

    stat = pl.BlockSpec((None, S, LANES), lambda h, j: (h, 0, 0))
    tab = pl.BlockSpec((S, LANES), lambda h, j: (0, 0))
    return _pallas(
        body, name="attn_bwd", grid=(MH, nt),
        in_specs=[pl.BlockSpec((S, 2 * LANES), lambda h, j: (0, h)),
                  pl.BlockSpec((T, 2 * LANES), lambda h, j: (j, h)),
                  pl.BlockSpec((T, LANES), lambda h, j: (j, h)),
                  pl.BlockSpec((S, LANES), lambda h, j: (0, h)), stat, stat, tab, tab, tab] + [_ANY] * n_dep,
        out_specs=[pl.BlockSpec((S, 2 * LANES), lambda h, j: (0, h)),
                   pl.BlockSpec((T, 2 * LANES), lambda h, j: (j, h)),
                   pl.BlockSpec((T, LANES), lambda h, j: (j, h))],
        out_shape=[jax.ShapeDtypeStruct((S, MH * 2 * LANES), BF16), jax.ShapeDtypeStruct((S, MH * 2 * LANES), BF16),
                   jax.ShapeDtypeStruct((S, MH * LANES), F32)],
        scratch_shapes=[pltpu.VMEM((S, 2 * LANES), F32), pltpu.VMEM((T, 2 * LANES), F32), pltpu.VMEM((T, LANES), F32)]
        + [pltpu.VMEM((T, T), F32)] * 4,
        compiler_params=_params(("parallel", "arbitrary")),
    )(qf, kf, vb, dob, lse2, delta, *tabs, *deps)


def _kpe_sum(dkpe_h, tabs, MH, *, tr=256):
    S = dkpe_h.shape[0]
    tr = _tile(S, tr)

    def body(dk_ref, c_ref, s1_ref, s2_ref, dkpe_ref):
        tot = dk_ref[:, :LANES]
        for h in range(1, MH):
            tot = tot + dk_ref[:, h * LANES:(h + 1) * LANES]
        dkpe_ref[...] = _rope_pe_t(tot, c_ref[...], s1_ref[...], s2_ref[...]).astype(BF16)

    tab = pl.BlockSpec((tr, LANES), lambda i: (i, 0))
    return _pallas(
        body, name="kpe_sum", grid=(S // tr,),
        in_specs=[pl.BlockSpec((tr, MH * LANES), lambda i: (i, 0)), tab, tab, tab],
        out_specs=tab, out_shape=jax.ShapeDtypeStruct((S, LANES), BF16),
        compiler_params=_params(("parallel",)),
    )(dkpe_h, *tabs)


ROW_ALIGN = 16


def _blk(R, C, block_bytes=2 << 20):
    cap = max(ROW_ALIGN, block_bytes // (C * 4))
    for t in range(min(R, cap) // ROW_ALIGN * ROW_ALIGN, LANES - 1, -ROW_ALIGN):
        if R % t == 0:
            return t, C
    if R <= cap:
        return R, C
    tc = C
    while R * tc * 4 > block_bytes and tc % (2 * LANES) == 0:
        tc //= 2
    return R, tc


def _rows_call(fn, ins, out_dtypes, *, name):
    R, C = ins[0].shape
    tr, tc = _blk(R, C)
    n_in = len(ins)

    def body(*refs):
        vals = fn(*[r[...] for r in refs[:n_in]])
        for r, v in zip(refs[n_in:], vals):
            r[...] = v.astype(r.dtype)

    blk = pl.BlockSpec((tr, tc), lambda i, j: (i, j))
    res = _pallas(
        body, name=name, grid=(R // tr, C // tc), in_specs=[blk] * n_in, out_specs=[blk] * len(out_dtypes),
        out_shape=[jax.ShapeDtypeStruct((R, C), d) for d in out_dtypes],
        compiler_params=_params(("parallel", "parallel")),
    )(*ins)
    return res


def _adamw_vals(w, g, m, v):
    m = ADAM_B1 * m + (1.0 - ADAM_B1) * g
    v = ADAM_B2 * v + (1.0 - ADAM_B2) * (g * g)
    m_hat = m / (1.0 - ADAM_B1 ** ADAM_STEP)
    v_hat = v / (1.0 - ADAM_B2 ** ADAM_STEP)
    delta = -ADAM_LR * (m_hat / (jnp.sqrt(v_hat) + ADAM_EPS) + ADAM_WD * w)
    return delta, m, v


def _sum_pair(p, theirs, place, *, name):
    _, R, C = p.shape
    R2 = R // 2
    tr, tc = _blk(R2, C)
    p4 = p.reshape(N_CHIPS, 2, R2, C)

    def body(place_ref, a_ref, b_ref, o_ref):
        o_ref[...] = (a_ref[...].astype(F32) + b_ref[...].astype(F32)).astype(BF16)

    spec = pltpu.PrefetchScalarGridSpec(
        num_scalar_prefetch=1, grid=(N_CHIPS, R2 // tr, C // tc),
        in_specs=[pl.BlockSpec((None, None, tr, tc), lambda q, i, j, pr: (q, pr[0], i, j)),
                  pl.BlockSpec((None, tr, tc), lambda q, i, j, pr: (q, i, j))],
        out_specs=pl.BlockSpec((None, tr, tc), lambda q, i, j, pr: (q, i, j)))
    return _pallas(body, name=name, grid_spec=spec, out_shape=jax.ShapeDtypeStruct((N_CHIPS, R2, C), BF16),
                   compiler_params=_params(("parallel", "parallel", "parallel")))(place, p4, theirs)


def _sum_chips(p, theirs, recv, place, *, name):
    _, R, C = p.shape
    R2 = R // 2
    tr, tc = _blk(R2, C)
    p4 = p.reshape(N_CHIPS, 2, R2, C)

    def body(place_ref, a_ref, b_ref, r0_ref, r1_ref, r2_ref, o_ref):
        own = a_ref[...].astype(F32) + b_ref[...].astype(F32)
        o_ref[...] = ((own + r0_ref[...].astype(F32)) + r1_ref[...].astype(F32)) + r2_ref[...].astype(F32)

    def slot(k):
        return pl.BlockSpec((None, tr, tc), lambda i, j, pr: (k, i, j))

    spec = pltpu.PrefetchScalarGridSpec(
        num_scalar_prefetch=1, grid=(R2 // tr, C // tc),
        in_specs=[pl.BlockSpec((None, None, tr, tc), lambda i, j, pr: (pr[1], pr[0], i, j)),
                  pl.BlockSpec((None, tr, tc), lambda i, j, pr: (pr[1], i, j)), slot(0), slot(1), slot(2)],
        out_specs=pl.BlockSpec((None, tr, tc), lambda i, j, pr: (pr[0], i, j)))
    return _pallas(body, name=name, grid_spec=spec, out_shape=jax.ShapeDtypeStruct((2, R2, C), F32),
                   compiler_params=_params(("parallel", "parallel")))(place, p4, theirs, recv, recv, recv)


def _me():
    return lax.axis_index("x"), lax.axis_index("y"), lax.axis_index("c")


def _other_chips(x, y):
    return [(1 - x, y), (x, 1 - y), (1 - x, 1 - y)]


def _rcopy(src, dst, ssem, rsem, dev):
    return pltpu.make_async_remote_copy(src_ref=src, dst_ref=dst, send_sem=ssem, recv_sem=rsem,
                                        device_id=dev, device_id_type=MESH)


def _cast_into_slot(w, place, *, name, rows=None, deps=()):
    R, C = w.shape
    rows = R if rows is None else rows
    tr, tc = _blk(R, C)

    def body(place_ref, w_ref, *rest):
        rest[-1][...] = w_ref[...].astype(BF16)

    spec = pltpu.PrefetchScalarGridSpec(
        num_scalar_prefetch=1, grid=(R // tr, C // tc),
        in_specs=[pl.BlockSpec((tr, tc), lambda i, j, pr: (i, j))] + [_ANY] * len(deps),
        out_specs=pl.BlockSpec((None, tr, tc), lambda i, j, pr: (pr[1], i, j)))
    out = _pallas(body, name=name, grid_spec=spec, out_shape=jax.ShapeDtypeStruct((N_CHIPS, rows, C), BF16),
                  compiler_params=_params(("parallel", "parallel")))(place, w, *deps)
    return out.reshape(N_CHIPS, 2, rows // 2, C)


def _gather_ici_plan(bufs):
    x, y, c = _me()
    j = 2 * x + y
    plan = []
    for i, buf in enumerate(bufs):
        for k, (px, py) in enumerate(_other_chips(x, y)):
            plan.append((3 * i + k, buf.at[j, c], buf.at[j, c], (px, py, c)))
    return plan


def _forward_halves(bufs, *, name):
    n = len(bufs)

    def body(*refs):
        outs = refs[n:2 * n]
        ssem, rsem = refs[2 * n:]
        x, y, c = _me()
        sib = (x, y, 1 - c)
        cps = []
        for i in range(n):
            for k, (px, py) in enumerate(_other_chips(x, y)):
                slot = outs[i].at[2 * px + py, c]
                r = _rcopy(slot, slot, ssem.at[3 * i + k], rsem.at[3 * i + k], sib)
                r.start()
                cps.append(r)
        for r in cps:
            r.wait()

    return _pallas(
        body, name=name, in_specs=[_ANY] * n, out_specs=[_ANY] * n,
        out_shape=[jax.ShapeDtypeStruct(b.shape, b.dtype) for b in bufs],
        scratch_shapes=[pltpu.SemaphoreType.DMA((3 * n,))] * 2,
        input_output_aliases={i: i for i in range(n)},
        compiler_params=pltpu.CompilerParams(has_side_effects=True),
    )(*bufs)


_HBM = pl.BlockSpec(memory_space=pltpu.HBM)
_SEM = pl.BlockSpec(memory_space=pltpu.SEMAPHORE)
_EFFECT = pltpu.SideEffectType.DATAFLOW_SIDE_EFFECTING


def _split_start(bufs, plan, n_copies, *, name):
    n = len(bufs)

    def body(*refs):
        ssem, rsem = refs[n], refs[n + 1]
        for s, src, dst, dev in plan(refs[:n]):
            _rcopy(src, dst, ssem.at[s], rsem.at[s], dev).start()
        refs[-1][...] = jnp.zeros_like(refs[-1])

    res = _pallas(
        body, name=name, in_specs=[_HBM] * n,
        out_specs=(_SEM, _SEM, *[_HBM] * n, pl.BlockSpec(memory_space=pltpu.VMEM)),
        out_shape=(pltpu.SemaphoreType.DMA((n_copies,)), pltpu.SemaphoreType.DMA((n_copies,)),
                   *[pltpu.HBM(b.shape, b.dtype) for b in bufs], jax.ShapeDtypeStruct((8, LANES), F32)),
        input_output_aliases={i: 2 + i for i in range(n)},
        compiler_params=pltpu.CompilerParams(has_side_effects=_EFFECT),
    )(*[pltpu.with_memory_space_constraint(b, pltpu.HBM) for b in bufs])
    return res[0], res[1], list(res[2:2 + n]), res[-1]


def _split_wait(ssem, rsem, bufs, after, plan, *, name):
    n = len(bufs)

    def body(*refs):
        ssem_ref, rsem_ref = refs[n], refs[n + 1]
        for s, src, dst, dev in plan(refs[:n]):
            cp = _rcopy(src, dst, ssem_ref.at[s], rsem_ref.at[s], dev)
            cp.wait_send()
            cp.wait_recv()

    return list(_pallas(
        body, name=name, in_specs=[_HBM] * n + [_SEM, _SEM, _ANY], out_specs=[_HBM] * n,
        out_shape=[pltpu.HBM(b.shape, b.dtype) for b in bufs],
        input_output_aliases={i: i for i in range(n)},
        compiler_params=pltpu.CompilerParams(has_side_effects=_EFFECT),
    )(*bufs, ssem, rsem, after))


def _swap_plan(n):
    def plan(bufs):
        x, y, c = _me()
        return [(i, bufs[i].at[:, 1 - c], bufs[n + i], (x, y, 1 - c)) for i in range(n)]
    return plan


def _scatter_plan(n):
    def plan(bufs):
        x, y, c = _me()
        out = []
        for i in range(n):
            for k, (px, py) in enumerate(_other_chips(x, y)):
                out.append((3 * i + k, bufs[i].at[2 * px + py], bufs[n + i].at[k], (px, py, c)))
        return out
    return plan


def _swap_halves(grads, *, name):
    n = len(grads)
    views = [g.reshape(N_CHIPS, 2, g.shape[1] // 2, g.shape[2]) for g in grads]

    def body(*refs):
        ins, outs = refs[:n], refs[n:2 * n]
        ssem, rsem = refs[2 * n:]
        x, y, c = _me()
        sib = (x, y, 1 - c)
        cps = []
        for i in range(n):
            r = _rcopy(ins[i].at[:, 1 - c], outs[i], ssem.at[i], rsem.at[i], sib)
            r.start()
            cps.append(r)
        for r in cps:
            r.wait()

    return _pallas(
        body, name=name, in_specs=[_ANY] * n, out_specs=[_ANY] * n,
        out_shape=[jax.ShapeDtypeStruct((N_CHIPS,) + v.shape[2:], v.dtype) for v in views],
        scratch_shapes=[pltpu.SemaphoreType.DMA((n,)), pltpu.SemaphoreType.DMA((n,))],
        compiler_params=pltpu.CompilerParams(has_side_effects=True),
    )(*views)


def _join_halves(halves, *, name):
    n = len(halves)

    def body(*refs):
        outs = refs[n:2 * n]
        ssem, rsem = refs[2 * n:]
        x, y, c = _me()
        sib = (x, y, 1 - c)
        cps = []
        for i in range(n):
            r = _rcopy(outs[i].at[c], outs[i].at[c], ssem.at[i], rsem.at[i], sib)
            r.start()
            cps.append(r)
        for r in cps:
            r.wait()

    return _pallas(
        body, name=name, in_specs=[_ANY] * n, out_specs=[_ANY] * n,
        out_shape=[jax.ShapeDtypeStruct(h.shape, h.dtype) for h in halves],
        scratch_shapes=[pltpu.SemaphoreType.DMA((n,)), pltpu.SemaphoreType.DMA((n,))],
        input_output_aliases={i: i for i in range(n)},
        compiler_params=pltpu.CompilerParams(has_side_effects=True),
    )(*halves)


def _allreduce_small(parts, loss11):
    n = len(parts)
    widths = [p.shape[1] for p in parts]
    total = sum(widths) + LANES

    def body(*refs):
        o_ref, mine, buf, ssem, rsem = refs[n + 1:]
        x, y, c = _me()
        me = 4 * x + 2 * y + c
        off = 0
        for r, w in zip(refs[:n], widths):
            mine[:, off:off + w] = r[...]
            off += w
        mine[:, off:] = jnp.broadcast_to(refs[n][...], (1, LANES))
        buf[me] = mine[...]
        cps = []
        for k in range(1, 8):
            peer = (x ^ (k >> 2), y ^ ((k >> 1) & 1), c ^ (k & 1))
            r = _rcopy(mine, buf.at[me], ssem.at[k - 1], rsem.at[k - 1], peer)
            r.start()
            cps.append(r)
        for k in range(1, 8):
            peer = (x ^ (k >> 2), y ^ ((k >> 1) & 1), c ^ (k & 1))
            pid = 4 * peer[0] + 2 * peer[1] + peer[2]
            _rcopy(mine, buf.at[pid], ssem.at[k - 1], rsem.at[k - 1], peer).wait_recv()
        for r in cps:
            r.wait_send()
        tot = buf[0]
        for d in range(1, 8):
            tot = tot + buf[d]
        o_ref[...] = tot

    vm = pl.BlockSpec(memory_space=pltpu.VMEM)
    return _pallas(
        body, name="allreduce_small", in_specs=[vm] * (n + 1), out_specs=vm,
        out_shape=jax.ShapeDtypeStruct((1, total), F32),
        scratch_shapes=[pltpu.VMEM((1, total), F32), pltpu.VMEM((8, 1, total), F32),
                        pltpu.SemaphoreType.DMA((7,)), pltpu.SemaphoreType.DMA((7,))],
        compiler_params=pltpu.CompilerParams(has_side_effects=True),
    )(*parts, loss11)


def _adamw_small(red, ws, ms, vs):
    n = len(ws)

    def body(*refs):
        red_ref = refs[0]
        outs = refs[1 + 3 * n:]
        off = 0
        for i in range(n):
            w = refs[1 + i].shape[1]
            g = red_ref[:, off:off + w]
            d, m, v = _adamw_vals(refs[1 + i][...], g, refs[1 + n + i][...], refs[1 + 2 * n + i][...])
            for o, val in zip(outs[4 * i:4 * i + 4], (g, d, m, v)):
                o[...] = val
            off += w

    vm = pl.BlockSpec(memory_space=pltpu.VMEM)
    res = _pallas(
        body, name="adamw_small", in_specs=[vm] * (1 + 3 * n), out_specs=[vm] * (4 * n),
        out_shape=[jax.ShapeDtypeStruct(w.shape, F32) for w in ws for _ in range(4)],
    )(red, *ws, *ms, *vs)
    return [res[4 * i:4 * i + 4] for i in range(n)]


def _rope_tables(positions, S):
    pos = positions.reshape(S, 1).astype(F32)
    half = RET_QK // 2
    inv = ROPE_THETA ** (-jnp.arange(half, dtype=F32) / half)
    ang = pos * inv
    cosr = jnp.concatenate([jnp.cos(ang), jnp.cos(ang)], axis=1)
    sinr = jnp.concatenate([-jnp.sin(ang), jnp.sin(ang)], axis=1)
    half = QK_ROPE // 2
    inv = ROPE_THETA ** (-jnp.arange(half, dtype=F32) / half)
    ang = pos * inv
    z = jnp.zeros((S, half), F32)
    c = jnp.concatenate([jnp.cos(ang), jnp.cos(ang), z, z], axis=1)
    s1 = jnp.concatenate([-jnp.sin(ang), z, z, z], axis=1)
    s2 = jnp.concatenate([z, jnp.sin(ang), z, z], axis=1)
    return cosr, sinr, (c, s1, s2)


def _cat_cols(g):
    return jnp.concatenate([g[j] for j in range(N_CHIPS)], axis=1)


def _split_cols(w):
    return jnp.stack(jnp.split(w, N_CHIPS, axis=1))


def kernel(x, positions, norm_mix_g, w_in, ret_norm_g, w_ret_o, q_a_norm_g, w_q_b, kv_a_norm_g, w_kv_b, w_mla_o, w_out, norm_mlp_g, w_up, w_down, norm_f_g, loss_target, m_norm_mix_g, m_w_in, m_ret_norm_g, m_w_ret_o, m_q_a_norm_g, m_w_q_b, m_kv_a_norm_g, m_w_kv_b, m_w_mla_o, m_w_out, m_norm_mlp_g, m_w_up, m_w_down, m_norm_f_g, v_norm_mix_g, v_w_in, v_ret_norm_g, v_w_ret_o, v_q_a_norm_g, v_w_q_b, v_kv_a_norm_g, v_w_kv_b, v_w_mla_o, v_w_out, v_norm_mlp_g, v_w_up, v_w_down, v_norm_f_g):
    S, D = x.shape[1], x.shape[2]
    RVW = w_ret_o.shape[1] * N_CHIPS
    RH = RVW // RET_V
    RQW = RH * RET_QK
    MVW = w_mla_o.shape[1] * N_CHIPS
    MH = MVW // V_HEAD
    QL, KVL = w_q_b.shape[1], w_kv_b.shape[1]
    T_RET = _tile(S, 256)
    T_ATT = _tile(S, 512)

    xs = x.reshape(S, D)
    tgt = loss_target.reshape(S, D)
    cosr, sinr, pe_tabs = _rope_tables(positions, S)
    lgam = jnp.log(1.0 - 2.0 ** (-5.0 - jnp.arange(RH, dtype=F32)))
    lgam = jnp.broadcast_to(lgam[:, None, None], (RH, 8, LANES))

    big = ("w_in", "w_ret_o", "w_q_b", "w_kv_b", "w_mla_o", "w_out", "w_up", "w_down")
    w_sh = dict(w_in=w_in[0].T, w_ret_o=w_ret_o[0], w_q_b=w_q_b[0], w_kv_b=w_kv_b[0], w_mla_o=w_mla_o[0],
                w_out=w_out[0], w_up=w_up[0], w_down=w_down[0])
    m_sh = dict(w_in=m_w_in[0].T, w_ret_o=m_w_ret_o[0], w_q_b=m_w_q_b[0], w_kv_b=m_w_kv_b[0],
                w_mla_o=m_w_mla_o[0], w_out=m_w_out[0], w_up=m_w_up[0], w_down=m_w_down[0])
    v_sh = dict(w_in=v_w_in[0].T, w_ret_o=v_w_ret_o[0], w_q_b=v_w_q_b[0], w_kv_b=v_w_kv_b[0],
                w_mla_o=v_w_mla_o[0], w_out=v_w_out[0], w_up=v_w_up[0], w_down=v_w_down[0])
    col_sharded = ("w_q_b", "w_kv_b", "w_up")
    c_sh = w_in.shape[2]
    c_pad = -(-c_sh // 64) * 64
    place = jnp.stack([lax.axis_index("c"), 2 * lax.axis_index("x") + lax.axis_index("y")]).astype(jnp.int32)

    def whole(k, g):
        g = g.reshape(N_CHIPS, w_sh[k].shape[0], w_sh[k].shape[1])
        if k == "w_up":
            return g
        return _cat_cols(g) if k in col_sharded else g.reshape(-1, g.shape[2])

    first = ("w_in", "w_q_b", "w_kv_b")
    later = ("w_ret_o", "w_mla_o", "w_out", "w_up", "w_down")
    first_bufs = [_cast_into_slot(w_sh[k], place, name="cast_" + k, rows=c_pad if k == "w_in" else None)
                  for k in first]
    first_ssem, first_rsem, first_bufs, first_token = _split_start(
        first_bufs, _gather_ici_plan, 3 * len(first), name="gather_first_start")
    later_bufs = [_cast_into_slot(w_sh[k], place, name="cast_" + k, deps=(first_token,)) for k in later[:-1]]
    first_bufs = _split_wait(first_ssem, first_rsem, first_bufs, later_bufs[-1], _gather_ici_plan,
                             name="gather_first_wait")
    got = _forward_halves(first_bufs, name="gather_first_forward")
    full = {k: whole(k, g) for k, g in zip(first[1:], got[1:])}
    later_bufs.append(_cast_into_slot(w_sh[later[-1]], place, name="cast_" + later[-1], deps=(got[0],)))
    later_ssem, later_rsem, later_bufs, later_token = _split_start(
        later_bufs, _gather_ici_plan, 3 * len(later), name="gather_later_start")

    o_rq, o_rk, o_rv, o_rg = 0, RQW, 2 * RQW, 2 * RQW + RVW
    o_cq = 2 * RQW + 2 * RVW
    o_ckv, o_kpe = o_cq + QL, o_cq + QL + KVL
    o_gr = o_kpe + QK_ROPE
    o_gm = o_gr + D
    n_ret = RH * RET_HEAD_COLS
    off_gret, off_gmla, off_cq, off_ckv = n_ret, n_ret + D, n_ret + 2 * D, n_ret + 2 * D + QL
    n_a = off_ckv + KVL
    runs = []
    for h in range(RH):
        base = h * RET_HEAD_COLS
        runs += [(o_rq + h * RET_QK, RET_QK, base), (o_rk + h * RET_QK, RET_QK, base + RET_QK),
                 (o_rv + h * RET_V, RET_V, base + 2 * RET_QK), (o_rg + h * RET_V, RET_V, base + 2 * RET_QK + RET_V)]
    runs += [(o_gr, D, off_gret), (o_gm, D, off_gmla), (o_cq, QL, off_cq), (o_ckv, KVL, off_ckv),
             (o_kpe, QK_ROPE, n_a)]

    def take(parts, start, width):
        out, lo = [], 0
        for p in parts:
            hi = lo + p.shape[0]
            a, b = max(start, lo), min(start + width, hi)
            if a < b:
                out.append(p[a - lo:b - lo])
            lo = hi
        return out

    wi = [got[0].reshape(N_CHIPS, c_pad, D)[jj, :c_sh] for jj in range(N_CHIPS)]
    here = sorted(runs, key=lambda r: r[2])
    wa = jnp.concatenate([p for s0, w, _ in here[:-1] for p in take(wi, s0, w)], axis=0)
    wkpe = jnp.concatenate(take(wi, o_kpe, QK_ROPE) + [jnp.zeros((LANES - QK_ROPE, D), BF16)], axis=0)
    wq = jnp.pad(full["w_q_b"].reshape(QL, MH, QK_NOPE + QK_ROPE),
                 ((0, 0), (0, 0), (0, LANES - QK_ROPE))).reshape(QL, MH * 2 * LANES)
    wkv = full["w_kv_b"]

    u, rstd0 = _rmsnorm_fwd(xs, norm_mix_g, name="norm_mix")
    proj = _mm(u, wa, mode="nt", outs=[F32], name="in_proj", deps=(later_token,))
    kpe = _mm(u, wkpe, mode="nt", outs=[F32], name="kpe_proj")
    ry, gated, states = _ret_fwd(proj, cosr, sinr, lgam, ret_norm_g, RH, T=T_RET)
    cqn, rstd_q = _rmsnorm_fwd(proj, q_a_norm_g, name="norm_q", width=QL, col=off_cq // QL)
    ckvn, rstd_kv = _rmsnorm_fwd(proj, kv_a_norm_g, name="norm_kv", width=KVL, col=off_ckv // KVL)
    qf, kf, vb = _qkv_proj(cqn, ckvn, wq, wkv, kpe, pe_tabs, MH)
    my, my_b, lse2 = _attn_fwd(qf, kf, vb, MH, T=T_ATT)
    later_bufs = _split_wait(later_ssem, later_rsem, later_bufs, my, _gather_ici_plan, name="gather_later_wait")
    later_bufs = _forward_halves(later_bufs, name="gather_later_forward")
    full.update({k: whole(k, g) for k, g in zip(later, later_bufs)})
    y_ret = _mm(gated, full["w_ret_o"], mode="nn", outs=[F32], name="ret_o")
    y_mla = _mm(my_b, full["w_mla_o"], mode="nn", outs=[F32], name="mla_o")
    merged = _merge_fwd(proj, y_ret, y_mla, D, off_gret, off_gmla)
    h1 = _mm(merged, full["w_out"], mode="nn", outs=[F32], name="out_proj",
             epi=lambda acc, r: (acc + r,), extras=(xs,))
    n1, rstd1 = _rmsnorm_fwd(h1, norm_mlp_g, name="norm_mlp")

    def up_epi(acc):
        r = jnp.maximum(acc, 0.0)
        return acc, r * r

    z, act = _mm(n1, full["w_up"], mode="nn", outs=[F32, BF16], name="up_proj", epi=up_epi)
    h2 = _mm(act, full["w_down"], mode="nn", outs=[F32], name="down_proj",
             epi=lambda acc, r: (acc + r,), extras=(h1,))
    loss11, dh2, dh2_b, g_norm_f = _final_loss(h2, norm_f_g.reshape(1, D), tgt)

    dz = _mm(dh2_b, full["w_down"], mode="nt", outs=[BF16], name="down_bwd_x",
             epi=lambda acc, zz: (acc * (2.0 * jnp.maximum(zz, 0.0)),), extras=(z,))
    g_w_down = _mm(act, dh2_b, mode="tn", outs=[BF16], name="down_bwd_w")
    dn1 = _mm(dz, full["w_up"], mode="nt", outs=[F32], name="up_bwd_x")
    g_w_up = _mm(n1, dz, mode="tn", outs=[BF16], name="up_bwd_w", out_shards=True)

    def reduce_begin(tag, names, grads):
        pcs = [g if g.ndim == 3 else g.reshape(N_CHIPS, g.shape[0] // N_CHIPS, g.shape[1]) for g in grads]
        theirs = _swap_halves(pcs, name="swap_" + tag)
        sums = [_sum_pair(p, t, place, name="sum_pair_" + k) for k, p, t in zip(names, pcs, theirs)]
        return pcs, theirs, sums

    def scatter_begin(tag, sums):
        lands = [lax.empty((3,) + s.shape[1:], s.dtype) for s in sums]
        return _split_start(sums + lands, _scatter_plan(len(sums)), 3 * len(sums), name="scatter_" + tag + "_start")

    def swap_begin(tag, grads):
        views = [g if g.ndim == 3 else g.reshape(N_CHIPS, g.shape[0] // N_CHIPS, g.shape[1]) for g in grads]
        views = [v.reshape(N_CHIPS, 2, v.shape[1] // 2, v.shape[2]) for v in views]
        lands = [lax.empty((N_CHIPS,) + v.shape[2:], v.dtype) for v in views]
        return _split_start(views + lands, _swap_plan(len(views)), len(views), name="swap_" + tag + "_start")

    def swap_end(tag, names, handle, after):
        n = len(names)
        bufs = _split_wait(handle[0], handle[1], handle[2], after, _swap_plan(n), name="swap_" + tag + "_wait")
        pcs = [b.reshape(N_CHIPS, 2 * b.shape[2], b.shape[3]) for b in bufs[:n]]
        sums = [_sum_pair(p, t, place, name="sum_pair_" + k) for k, p, t in zip(names, pcs, bufs[n:])]
        return pcs, bufs[n:], sums

    g1 = ("w_up", "w_down")
    swap1 = swap_begin("g1", (g_w_up, g_w_down))
    dh1, g_norm_mlp, dh1_b = _rmsnorm_bwd(dn1, h1, rstd1, norm_mlp_g, name="norm_mlp_bwd", res=dh2,
                                          deps=(swap1[3],), bf16_copy=1)
    dmerged = _mm(dh1_b, full["w_out"], mode="nt", outs=[F32], name="out_bwd_x")
    pcs1, theirs1, sums1 = swap_end("g1", g1, swap1, dmerged)
    ssem1, rsem1, bufs1, token1 = scatter_begin("g1", sums1)
    g_w_out = _mm(merged, dh1_b, mode="tn", outs=[BF16], name="out_bwd_w", deps=(token1,))
    dproj, dy_ret, dy_mla = _merge_bwd(dmerged, proj, y_ret, y_mla, D, off_gret)
    dgated = _mm(dy_ret, full["w_ret_o"], mode="nt", outs=[F32], name="ret_o_bwd_x")
    g_w_ret_o = _mm(gated, dy_ret, mode="tn", outs=[BF16], name="ret_o_bwd_w")
    dproj, g_ret_norm = _ret_bwd(proj, cosr, sinr, lgam, ret_norm_g, ry, dgated, states, dproj, RH, T=T_RET)
    def delta_epi(acc, o):
        rows = acc.shape[0]
        return acc, [jnp.broadcast_to(jnp.sum(acc[:, lo:lo + V_HEAD] * o[:, lo:lo + V_HEAD], axis=-1, keepdims=True),
                                      (rows, LANES)) for lo in range(0, acc.shape[1], V_HEAD)]

    dob, delta = _mm(dy_mla, full["w_mla_o"], mode="nt", outs=[BF16], name="mla_o_bwd_x", epi=delta_epi,
                     extras=(my,), more_outs=lambda tm, tn: [
                         (jax.ShapeDtypeStruct((MH, S, LANES), F32),
                          pl.BlockSpec((tn // V_HEAD, tm, LANES), lambda i, j, k: (j, i, 0)))])
    g_w_mla_o = _mm(my_b, dy_mla, mode="tn", outs=[BF16], name="mla_o_bwd_w")
    g2 = ("w_out", "w_ret_o", "w_mla_o")
    swap2 = swap_begin("g2", (g_w_out, g_w_ret_o, g_w_mla_o))
    dq_all, dkv_all, dkpe_h = _attn_bwd(qf, kf, vb, dob, lse2, delta, pe_tabs, MH, T=T_ATT, deps=(swap2[3],))
    pcs2, theirs2, sums2 = swap_end("g2", g2, swap2, dkv_all)
    ssem2, rsem2, bufs2, token2 = scatter_begin("g2", sums2)
    dkpe = _kpe_sum(dkpe_h, pe_tabs, MH)
    dcqn = _mm(dq_all, wq, mode="nt", outs=[F32], name="q_bwd_x", deps=(token2,))
    g_wq = _mm(cqn, dq_all, mode="tn", outs=[BF16], name="q_bwd_w")
    dckvn = _mm(dkv_all, wkv, mode="nt", outs=[F32], name="kv_bwd_x")
    g_wkv = _mm(ckvn, dkv_all, mode="tn", outs=[BF16], name="kv_bwd_w")
    dproj, g_q_a = _rmsnorm_bwd(dcqn, proj, rstd_q, q_a_norm_g, name="norm_q_bwd", into=(dproj, off_cq // QL),
                                width=QL, col=off_cq // QL)
    dproj, g_kv_a = _rmsnorm_bwd(dckvn, proj, rstd_kv, kv_a_norm_g, name="norm_kv_bwd", into=(dproj, off_ckv // KVL),
                                 width=KVL, col=off_ckv // KVL)
    g_wa = _mm(dproj, u, mode="tn", outs=[BF16], name="in_bwd_w")
    g_wkpe = _mm(dkpe, u, mode="tn", outs=[BF16], name="kpe_bwd_w")

    there = sorted(runs)
    g_parts = [g_wa, g_wkpe]
    g_w_in = jnp.stack([jnp.concatenate(
        [p for s0, w, d0 in there for a, b in [(max(s0, jj * c_sh), min(s0 + w, (jj + 1) * c_sh))] if a < b
         for p in take(g_parts, d0 + a - s0, b - a)] + [jnp.zeros((c_pad - c_sh, D), BF16)], axis=0)
        for jj in range(N_CHIPS)])
    gq = g_wq.reshape(QL, MH, 2 * LANES)[:, :, :QK_NOPE + QK_ROPE].reshape(QL, MH * (QK_NOPE + QK_ROPE))
    g3 = ("w_in", "w_q_b", "w_kv_b")
    pcs3, theirs3, sums3 = reduce_begin("g3", g3, (g_w_in, _split_cols(gq), _split_cols(g_wkv)))
    ssem3, rsem3, bufs3, token3 = scatter_begin("g3", sums3)
    du_a = _mm(dproj, wa, mode="nn", outs=[F32], name="in_bwd_x", tk=2816, deps=(token3,))
    du = _mm(dkpe, wkpe, mode="nn", outs=[F32], name="kpe_bwd_x", epi=lambda acc, r: (acc + r,), extras=(du_a,))
    dx, g_norm_mix = _rmsnorm_bwd(du, xs, rstd0, norm_mix_g, name="norm_mix_bwd", res=dh1)

    bufs1 = _split_wait(ssem1, rsem1, bufs1, dx, _scatter_plan(len(g1)), name="scatter_g1_wait")
    bufs2 = _split_wait(ssem2, rsem2, bufs2, dx, _scatter_plan(len(g2)), name="scatter_g2_wait")
    bufs3 = _split_wait(ssem3, rsem3, bufs3, dx, _scatter_plan(len(g3)), name="scatter_g3_wait")
    recv1, recv2, recv3 = bufs1[len(g1):], bufs2[len(g2):], bufs3[len(g3):]
    halves = {}
    for names, pcs, theirs, recv in ((g1, pcs1, theirs1, recv1), (g2, pcs2, theirs2, recv2), (g3, pcs3, theirs3, recv3)):
        for k, p, t, r in zip(names, pcs, theirs, recv):
            halves[k] = _sum_chips(p, t, r, place, name="sum_chips_" + k)
    joined = _join_halves([halves[k] for k in big], name="join_halves")
    g_shard = {k: g.reshape(2 * g.shape[1], g.shape[2]) for k, g in zip(big, joined)}

    small = ("norm_mix_g", "ret_norm_g", "q_a_norm_g", "kv_a_norm_g", "norm_mlp_g", "norm_f_g")
    g_small = [g_norm_mix, g_ret_norm, g_q_a, g_kv_a, g_norm_mlp, g_norm_f]
    red = _allreduce_small(g_small, loss11)
    loss = red[0, red.shape[1] - 1]
    w_small = [norm_mix_g, ret_norm_g, q_a_norm_g, kv_a_norm_g, norm_mlp_g, norm_f_g]
    m_small = [m_norm_mix_g, m_ret_norm_g, m_q_a_norm_g, m_kv_a_norm_g, m_norm_mlp_g, m_norm_f_g]
    v_small = [v_norm_mix_g, v_ret_norm_g, v_q_a_norm_g, v_kv_a_norm_g, v_norm_mlp_g, v_norm_f_g]
    row = lambda a: a.reshape(1, -1)
    upd = _adamw_small(red, [row(a) for a in w_small], [row(a) for a in m_small], [row(a) for a in v_small])
    out_g, out_d, out_m, out_v = {}, {}, {}, {}
    for k, wv, (g_, d_, m_, v_) in zip(small, w_small, upd):
        out_g[k], out_d[k], out_m[k], out_v[k] = [a.reshape(wv.shape) for a in (g_, d_, m_, v_)]

    for k in big:
        res = _rows_call(lambda w, g, m, v: (g,) + _adamw_vals(w, g, m, v),
                         [w_sh[k], g_shard[k], m_sh[k], v_sh[k]], [F32] * 4, name="adamw_" + k)
        if k == "w_in":
            res = [r.T for r in res]
        out_g[k], out_d[k], out_m[k], out_v[k] = [r[None] for r in res]

    order = ("norm_mix_g", "w_in", "ret_norm_g", "w_ret_o", "q_a_norm_g", "w_q_b", "kv_a_norm_g", "w_kv_b",
             "w_mla_o", "w_out", "norm_mlp_g", "w_up", "w_down", "norm_f_g")
    return (loss, dx.reshape(1, S, D), *[out_g[k] for k in order], *[out_d[k] for k in order],
            *[out_m[k] for k in order], *[out_v[k] for k in order])
```

```python
import math

import jax
import jax.numpy as jnp
from jax import lax
from jax.experimental import pallas as pl
from jax.experimental.pallas import tpu as pltpu

F32 = jnp.float32
BF16 = jnp.bfloat16

EPS = 1e-6
ROPE_THETA = 10000.0
CHUNK = 64
RET_QK = 128
RET_V = 256
RET_HEAD_COLS = 2 * RET_QK + 2 * RET_V
QK_NOPE = 128
QK_ROPE = 64
V_HEAD = 128
LANES = 128
LOG2E = math.log2(math.e)

ADAM_LR = 0.001
ADAM_B1 = 0.9
ADAM_B2 = 0.999
ADAM_EPS = 1e-08
ADAM_WD = 0.01
ADAM_STEP = 10

N_CHIPS = 4
VMEM_LIMIT = 56 * 1024 * 1024
MESH = pl.DeviceIdType.MESH
NEG = -1e30


def _pallas(body, **kw):
    return pl.pallas_call(body, **kw)


def _params(sem=None):
    return pltpu.CompilerParams(dimension_semantics=sem, vmem_limit_bytes=VMEM_LIMIT)


def _tile(n, want):
    t = min(n, want)
    while n % t:
        t //= 2
    return t


_ANY = pl.BlockSpec(memory_space=pl.ANY)
TN_BF16_TK = 4096


def _mm(a, b, *, mode, outs, name, epi=None, extras=(), deps=(), out_shards=False, more_outs=None,
        tm=1024, tn=1024, tk=2048):
    shards = b.shape[0] if b.ndim == 3 else 1
    brows, bcols = b.shape[-2], b.shape[-1] * shards
    if mode == "nn":
        (M, K), N = a.shape, bcols
    elif mode == "nt":
        (M, K), N = a.shape, brows
    else:
        (K, M), N = a.shape, bcols
    if mode == "tn" and a.dtype == BF16 and b.dtype == BF16:
        tk = max(tk, TN_BF16_TK)
    tm = _tile(M, tm)
    tn = _tile(N // (shards if mode == "nn" else 1) // (N_CHIPS if out_shards else 1), tn)
    tk = _tile(K // (shards if mode == "nt" else 1), tk)
    nk = K // tk
    if mode == "nn":
        a_spec = pl.BlockSpec((tm, tk), lambda i, j, k: (i, k))
        dims = (((1,), (0,)), ((), ()))
        if shards > 1:
            per = N // shards // tn
            b_spec = pl.BlockSpec((None, tk, tn), lambda i, j, k: (j // per, k, j % per))
        else:
            b_spec = pl.BlockSpec((tk, tn), lambda i, j, k: (k, j))
    elif mode == "nt":
        a_spec = pl.BlockSpec((tm, tk), lambda i, j, k: (i, k))
        dims = (((1,), (1,)), ((), ()))
        if shards > 1:
            per = K // shards // tk
            b_spec = pl.BlockSpec((None, tn, tk), lambda i, j, k: (k // per, j, k % per))
        else:
            b_spec = pl.BlockSpec((tn, tk), lambda i, j, k: (j, k))
    else:
        assert shards == 1
        a_spec = pl.BlockSpec((tk, tm), lambda i, j, k: (k, i))
        b_spec = pl.BlockSpec((tk, tn), lambda i, j, k: (k, j))
        dims = (((0,), (0,)), ((), ()))
    if out_shards:
        assert not extras
        oper = N // N_CHIPS // tn
        o_spec = pl.BlockSpec((None, tm, tn), lambda i, j, k: (j // oper, i, j % oper))
        o_shape = (N_CHIPS, M, N // N_CHIPS)
    else:
        o_spec = pl.BlockSpec((tm, tn), lambda i, j, k: (i, j))
        o_shape = (M, N)
    more = [] if more_outs is None else more_outs(tm, tn)
    n_ex, n_out, n_dep = len(extras), len(outs) + len(more), len(deps)
    if epi is None:
        epi = lambda acc: (acc,)

    def body(*refs):
        a_ref, b_ref = refs[0], refs[1]
        ex_refs = refs[2:2 + n_ex]
        o_refs = refs[2 + n_ex + n_dep:2 + n_ex + n_dep + n_out]
        part = lax.dot_general(a_ref[...].astype(BF16), b_ref[...].astype(BF16), dims,
                               preferred_element_type=F32)

        def finish(acc):
            vals = epi(acc, *[r[...] for r in ex_refs])
            for r, v in zip(o_refs, vals):
                if isinstance(v, (list, tuple)):
                    for lead, piece in enumerate(v):
                        r[lead] = piece.astype(r.dtype)
                else:
                    r[...] = v.astype(r.dtype)

        if nk == 1:
            finish(part)
        else:
            acc_ref = refs[-1]
            k = pl.program_id(2)

            @pl.when(k == 0)
            def _():
                acc_ref[...] = part

            @pl.when(k > 0)
            def _():
                acc_ref[...] += part

            @pl.when(k == nk - 1)
            def _():
                finish(acc_ref[...])

    res = _pallas(
        body, name=name, grid=(M // tm, N // tn, nk),
        in_specs=[a_spec, b_spec] + [o_spec] * n_ex + [_ANY] * n_dep,
        out_specs=[o_spec] * len(outs) + [spec for _, spec in more],
        out_shape=[jax.ShapeDtypeStruct(o_shape, d) for d in outs] + [shape for shape, _ in more],
        scratch_shapes=[pltpu.VMEM((tm, tn), F32)] if nk > 1 else [],
        compiler_params=_params(("parallel", "parallel", "arbitrary")),
    )(a, b, *extras, *deps)
    return res[0] if n_out == 1 else res


def _rmsnorm_fwd(x, g, *, name, width=None, col=0, tr=256):
    S = x.shape[0]
    W = x.shape[1] if width is None else width
    tr = _tile(S, tr)

    def body(x_ref, g_ref, y_ref, r_ref):
        xv = x_ref[...]
        rstd = lax.rsqrt(jnp.mean(xv * xv, axis=-1, keepdims=True) + EPS)
        y_ref[...] = (xv * rstd * g_ref[...]).astype(BF16)
        r_ref[...] = rstd

    return _pallas(
        body, name=name, grid=(S // tr,),
        in_specs=[pl.BlockSpec((tr, W), lambda i: (i, col)), pl.BlockSpec((1, W), lambda i: (0, 0))],
        out_specs=[pl.BlockSpec((tr, W), lambda i: (i, 0)), pl.BlockSpec((tr, 1), lambda i: (i, 0))],
        out_shape=[jax.ShapeDtypeStruct((S, W), BF16), jax.ShapeDtypeStruct((S, 1), F32)],
        compiler_params=_params(("parallel",)),
    )(x, g)


def _rmsnorm_bwd(dy, x, rstd, g, *, name, res=None, into=None, deps=(), bf16_copy=0, width=None, col=0, tr=256):
    S = x.shape[0]
    W = x.shape[1] if width is None else width
    tr = _tile(S, tr)
    has_res = res is not None

    def body(*refs):
        dy_ref, x_ref, r_ref, g_ref = refs[:4]
        dx_ref, dg_ref = refs[-2 - bf16_copy], refs[-1 - bf16_copy]
        rstd_v = r_ref[...]
        xhat = x_ref[...] * rstd_v
        dyv = dy_ref[...].astype(F32)
        dyg = dyv * g_ref[...]
        dx = rstd_v * (dyg - xhat * jnp.mean(dyg * xhat, axis=-1, keepdims=True))
        if has_res:
            dx = dx + refs[4][...]
        dx_ref[...] = dx.astype(dx_ref.dtype)
        if bf16_copy:
            refs[-1][...] = dx.astype(BF16)
        part = jnp.sum(dyv * xhat, axis=0, keepdims=True)

        @pl.when(pl.program_id(0) == 0)
        def _():
            dg_ref[...] = part

        @pl.when(pl.program_id(0) > 0)
        def _():
            dg_ref[...] += part

    row = pl.BlockSpec((tr, W), lambda i: (i, 0))
    ins = [dy, x, rstd, g] + ([res] if has_res else [])
    in_specs = [row, pl.BlockSpec((tr, W), lambda i: (i, col)), pl.BlockSpec((tr, 1), lambda i: (i, 0)),
                pl.BlockSpec((1, W), lambda i: (0, 0))] + ([row] if has_res else [])
    if into is None:
        dx_spec, dx_shape, alias = row, jax.ShapeDtypeStruct((S, W), F32), {}
    else:
        buf, col_out = into
        ins.append(buf)
        in_specs.append(_ANY)
        dx_spec = pl.BlockSpec((tr, W), lambda i: (i, col_out))
        dx_shape = jax.ShapeDtypeStruct(buf.shape, buf.dtype)
        alias = {len(ins) - 1: 0}
    ins += list(deps)
    in_specs += [_ANY] * len(deps)
    return _pallas(
        body, name=name, grid=(S // tr,), in_specs=in_specs,
        out_specs=[dx_spec, pl.BlockSpec((1, W), lambda i: (0, 0))] + [row] * bf16_copy,
        out_shape=[dx_shape, jax.ShapeDtypeStruct((1, W), F32)] + [jax.ShapeDtypeStruct((S, W), BF16)] * bf16_copy,
        input_output_aliases=alias,
        compiler_params=_params(("arbitrary",)),
    )(*ins)


def _final_loss(h2, g, target, *, tr=256):
    S, D = h2.shape
    tr = _tile(S, tr)

    def body(h_ref, g_ref, t_ref, loss_ref, dh_ref, dhb_ref, dg_ref):
        hv = h_ref[...]
        rstd = lax.rsqrt(jnp.mean(hv * hv, axis=-1, keepdims=True) + EPS)
        xhat = hv * rstd
        e = xhat * g_ref[...] - t_ref[...]
        lpart = (0.5 / D) * jnp.sum(jnp.sum(e * e, axis=-1, keepdims=True), axis=0, keepdims=True)
        dy = e * (1.0 / D)
        dyg = dy * g_ref[...]
        dh = rstd * (dyg - xhat * jnp.mean(dyg * xhat, axis=-1, keepdims=True))
        dh_ref[...] = dh
        dhb_ref[...] = dh.astype(BF16)
        gpart = jnp.sum(dy * xhat, axis=0, keepdims=True)

        @pl.when(pl.program_id(0) == 0)
        def _():
            loss_ref[...] = lpart
            dg_ref[...] = gpart

        @pl.when(pl.program_id(0) > 0)
        def _():
            loss_ref[...] += lpart
            dg_ref[...] += gpart

    row = pl.BlockSpec((tr, D), lambda i: (i, 0))
    vec = pl.BlockSpec((1, D), lambda i: (0, 0))
    return _pallas(
        body, name="final_loss", grid=(S // tr,), in_specs=[row, vec, row],
        out_specs=[pl.BlockSpec((1, 1), lambda i: (0, 0)), row, row, vec],
        out_shape=[jax.ShapeDtypeStruct((1, 1), F32), jax.ShapeDtypeStruct((S, D), F32),
                   jax.ShapeDtypeStruct((S, D), BF16), jax.ShapeDtypeStruct((1, D), F32)],
        compiler_params=_params(("arbitrary",)),
    )(h2, g, target)


def _sigmoid(v):
    return 1.0 / (1.0 + jnp.exp(-v))


def _merge_fwd(proj, y_ret, y_mla, D, off_gret, off_gmla, *, tr=256, tc=1024):
    S = y_ret.shape[0]
    tr, tc = _tile(S, tr), _tile(D, tc)
    b_ret, b_mla = off_gret // tc, off_gmla // tc

    def body(gr_ref, gm_ref, yr_ref, ym_ref, o_ref):
        o_ref[...] = (_sigmoid(gr_ref[...]) * yr_ref[...] + _sigmoid(gm_ref[...]) * ym_ref[...]).astype(BF16)

    blk = pl.BlockSpec((tr, tc), lambda i, j: (i, j))
    return _pallas(
        body, name="merge_fwd", grid=(S // tr, D // tc),
        in_specs=[pl.BlockSpec((tr, tc), lambda i, j: (i, b_ret + j)),
                  pl.BlockSpec((tr, tc), lambda i, j: (i, b_mla + j)), blk, blk],
        out_specs=blk, out_shape=jax.ShapeDtypeStruct((S, D), BF16),
        compiler_params=_params(("parallel", "parallel")),
    )(proj, proj, y_ret, y_mla)


def _merge_bwd(dmerged, proj, y_ret, y_mla, D, off_gret, *, tr=256):
    S = y_ret.shape[0]
    tr = _tile(S, tr)
    b0 = off_gret // D

    def body(dm_ref, g_ref, yr_ref, ym_ref, dp_ref, dyr_ref, dym_ref):
        dm = dm_ref[...]
        sg = _sigmoid(g_ref[...])

        @pl.when(pl.program_id(1) == 0)
        def _():
            dyr_ref[...] = (dm * sg).astype(BF16)
            dp_ref[...] = (dm * yr_ref[...] * sg * (1.0 - sg)).astype(BF16)

        @pl.when(pl.program_id(1) == 1)
        def _():
            dym_ref[...] = (dm * sg).astype(BF16)
            dp_ref[...] = (dm * ym_ref[...] * sg * (1.0 - sg)).astype(BF16)

    blk = pl.BlockSpec((tr, D), lambda i, j: (i, 0))
    return _pallas(
        body, name="merge_bwd", grid=(S // tr, 2),
        in_specs=[blk, pl.BlockSpec((tr, D), lambda i, j: (i, b0 + j)), blk, blk],
        out_specs=[pl.BlockSpec((tr, D), lambda i, j: (i, b0 + j)), blk, blk],
        out_shape=[jax.ShapeDtypeStruct(proj.shape, BF16), jax.ShapeDtypeStruct((S, D), BF16),
                   jax.ShapeDtypeStruct((S, D), BF16)],
        compiler_params=_params(("parallel", "arbitrary")),
    )(dmerged, proj, y_ret, y_mla)


def _rope128(t, cos_full, sin_signed):
    return t * cos_full + pltpu.roll(t, RET_QK // 2, 1) * sin_signed


def _rope128_t(d, cos_full, sin_signed):
    return d * cos_full + pltpu.roll(d * sin_signed, RET_QK // 2, 1)


def _ret_consts(lg, T):
    pos = lax.broadcasted_iota(jnp.int32, (T, 1), 0).astype(F32)
    qd = jnp.exp(lg * (pos + 1.0))
    kd = jnp.exp(lg * (T - 1.0 - pos))
    n = lax.broadcasted_iota(jnp.int32, (T, T), 0)
    m = lax.broadcasted_iota(jnp.int32, (T, T), 1)
    vis = (m // CHUNK) <= (n // CHUNK)
    dist = jnp.abs(n - m).astype(F32)
    decay = jnp.where(vis, jnp.exp(lg * dist), 0.0)
    cdec = jnp.exp(lg * float(T))
    return qd, kd, decay, cdec


def _dot(a, b, dims):
    return lax.dot_general(a.astype(BF16), b.astype(BF16), (dims, ((), ())), preferred_element_type=F32)


NN = ((1,), (0,))
NT = ((1,), (1,))
TN = ((0,), (0,))
_RQ = slice(0, RET_QK)
_RK = slice(RET_QK, 2 * RET_QK)
_RV = slice(2 * RET_QK, 2 * RET_QK + RET_V)
_RG = slice(2 * RET_QK + RET_V, RET_HEAD_COLS)


RET_GROUP = 4


def _head_cols(h, part):
    return slice(h * RET_HEAD_COLS + part.start, h * RET_HEAD_COLS + part.stop)


def _ret_fwd(proj, cosr, sinr, lgam, gain, RH, *, T):
    S = proj.shape[0]
    nb = S // T
    G = _tile(RH, RET_GROUP)
    heads = range(G)
    scale = RET_QK ** -0.5

    def body(p_ref, cos_ref, sin_ref, lg_ref, gain_ref, ry_ref, gated_ref, st_ref, state):
        b = pl.program_id(1)

        @pl.when(b == 0)
        def _():
            state[...] = jnp.zeros_like(state)

        consts = [_ret_consts(lg_ref[h, 0:1, 0:1], T) for h in heads]
        cosv, sinv = cos_ref[...], sin_ref[...]
        q = [_rope128(p_ref[:, _head_cols(h, _RQ)], cosv, sinv) for h in heads]
        k = [_rope128(p_ref[:, _head_cols(h, _RK)], cosv, sinv) * scale for h in heads]
        v = [p_ref[:, _head_cols(h, _RV)] for h in heads]
        sprev = [state[h] for h in heads]
        for h in heads:
            st_ref[h] = sprev[h]
        a = [_dot(q[h], k[h], NT) for h in heads]
        qs = [_dot(q[h] * consts[h][0], sprev[h], NN) for h in heads]
        kv = [_dot(k[h] * consts[h][1], v[h], TN) for h in heads]
        o = [_dot(a[h] * consts[h][2], v[h], NN) + qs[h] for h in heads]
        for h in heads:
            state[h] = sprev[h] * consts[h][3] + kv[h]
            vals = slice(h * RET_V, (h + 1) * RET_V)
            ry_ref[:, vals] = o[h]
            mu = jnp.mean(o[h], axis=-1, keepdims=True)
            oc = o[h] - mu
            var = jnp.mean(oc * oc, axis=-1, keepdims=True)
            t = oc * lax.rsqrt(var + EPS) * gain_ref[:, vals]
            gv = p_ref[:, _head_cols(h, _RG)]
            gated_ref[:, vals] = (t * (gv * _sigmoid(gv))).astype(BF16)

    return _pallas(
        body, name="ret_fwd", grid=(RH // G, nb),
        in_specs=[pl.BlockSpec((T, G * RET_HEAD_COLS), lambda h, b: (b, h)),
                  pl.BlockSpec((T, RET_QK), lambda h, b: (b, 0)),
                  pl.BlockSpec((T, RET_QK), lambda h, b: (b, 0)),
                  pl.BlockSpec((G, 8, LANES), lambda h, b: (h, 0, 0)),
                  pl.BlockSpec((1, G * RET_V), lambda h, b: (0, h))],
        out_specs=[pl.BlockSpec((T, G * RET_V), lambda h, b: (b, h)),
                   pl.BlockSpec((T, G * RET_V), lambda h, b: (b, h)),
                   pl.BlockSpec((G, None, RET_QK, RET_V), lambda h, b: (h, b, 0, 0))],
        out_shape=[jax.ShapeDtypeStruct((S, RH * RET_V), F32), jax.ShapeDtypeStruct((S, RH * RET_V), BF16),
                   jax.ShapeDtypeStruct((RH, nb, RET_QK, RET_V), F32)],
        scratch_shapes=[pltpu.VMEM((G, RET_QK, RET_V), F32)],
        compiler_params=_params(("parallel", "arbitrary")),
    )(proj, cosr, sinr, lgam, gain)


def _ret_bwd(proj, cosr, sinr, lgam, gain, ry, dgated, states, dproj, RH, *, T):
    S = proj.shape[0]
    nb = S // T
    G = _tile(RH, RET_GROUP)
    heads = range(G)
    scale = RET_QK ** -0.5

    def body(p_ref, cos_ref, sin_ref, lg_ref, gain_ref, ry_ref, dg_ref, st_ref, _, dp_ref, dgain_ref, dstate):
        b = pl.program_id(1)

        @pl.when(b == 0)
        def _():
            dstate[...] = jnp.zeros_like(dstate)

        consts = [_ret_consts(lg_ref[h, 0:1, 0:1], T) for h in heads]
        qd, kd, decay, cdec = [[c[i] for c in consts] for i in range(4)]
        cosv, sinv = cos_ref[...], sin_ref[...]
        q = [_rope128(p_ref[:, _head_cols(h, _RQ)], cosv, sinv) for h in heads]
        k = [_rope128(p_ref[:, _head_cols(h, _RK)], cosv, sinv) * scale for h in heads]
        v = [p_ref[:, _head_cols(h, _RV)] for h in heads]
        sprev = [st_ref[h] for h in heads]
        ds_new = [dstate[h] for h in heads]
        a = [_dot(q[h], k[h], NT) for h in heads]
        do, gparts = [], []
        for h in heads:
            vals = slice(h * RET_V, (h + 1) * RET_V)
            o = ry_ref[:, vals]
            mu = jnp.mean(o, axis=-1, keepdims=True)
            oc = o - mu
            rstd = lax.rsqrt(jnp.mean(oc * oc, axis=-1, keepdims=True) + EPS)
            ryn = oc * rstd
            gainv = gain_ref[:, vals]
            gv = p_ref[:, _head_cols(h, _RG)]
            sg = _sigmoid(gv)
            dgt = dg_ref[:, vals]
            dt = dgt * (gv * sg)
            dp_ref[:, _head_cols(h, _RG)] = (dgt * (ryn * gainv) * (sg * (1.0 + gv * (1.0 - sg)))).astype(BF16)
            gparts.append(jnp.sum(dt * ryn, axis=0, keepdims=True))
            dryn = dt * gainv
            do.append(rstd * (dryn - jnp.mean(dryn, axis=-1, keepdims=True)
                              - ryn * jnp.mean(dryn * ryn, axis=-1, keepdims=True)))
        gpart = jnp.concatenate(gparts, axis=1)

        @pl.when(b == 0)
        def _():
            dgain_ref[...] = gpart

        @pl.when(b > 0)
        def _():
            dgain_ref[...] += gpart

        dpm = [_dot(do[h], v[h], NT) for h in heads]
        dq_s = [_dot(do[h], sprev[h], NT) for h in heads]
        dk_s = [_dot(v[h], ds_new[h], NT) for h in heads]
        dv_s = [_dot(k[h] * kd[h], ds_new[h], NN) for h in heads]
        dst = [_dot(q[h] * qd[h], do[h], TN) for h in heads]
        a = [a[h] * decay[h] for h in heads]
        dpm = [dpm[h] * decay[h] for h in heads]
        dv = [_dot(a[h], do[h], TN) + dv_s[h] for h in heads]
        dq = [_dot(dpm[h], k[h], NN) + dq_s[h] * qd[h] for h in heads]
        dk = [(_dot(dpm[h], q[h], TN) + dk_s[h] * kd[h]) * scale for h in heads]
        for h in heads:
            dstate[h] = ds_new[h] * cdec[h] + dst[h]
            dp_ref[:, _head_cols(h, _RV)] = dv[h].astype(BF16)
            dp_ref[:, _head_cols(h, _RQ)] = _rope128_t(dq[h], cosv, sinv).astype(BF16)
            dp_ref[:, _head_cols(h, _RK)] = _rope128_t(dk[h], cosv, sinv).astype(BF16)

    rb = lambda b: nb - 1 - b
    return _pallas(
        body, name="ret_bwd", grid=(RH // G, nb),
        in_specs=[pl.BlockSpec((T, G * RET_HEAD_COLS), lambda h, b: (rb(b), h)),
                  pl.BlockSpec((T, RET_QK), lambda h, b: (rb(b), 0)),
                  pl.BlockSpec((T, RET_QK), lambda h, b: (rb(b), 0)),
                  pl.BlockSpec((G, 8, LANES), lambda h, b: (h, 0, 0)),
                  pl.BlockSpec((1, G * RET_V), lambda h, b: (0, h)),
                  pl.BlockSpec((T, G * RET_V), lambda h, b: (rb(b), h)),
                  pl.BlockSpec((T, G * RET_V), lambda h, b: (rb(b), h)),
                  pl.BlockSpec((G, None, RET_QK, RET_V), lambda h, b: (h, rb(b), 0, 0)),
                  _ANY],
        out_specs=[pl.BlockSpec((T, G * RET_HEAD_COLS), lambda h, b: (rb(b), h)),
                   pl.BlockSpec((1, G * RET_V), lambda h, b: (0, h))],
        out_shape=[jax.ShapeDtypeStruct(dproj.shape, dproj.dtype), jax.ShapeDtypeStruct((1, RH * RET_V), F32)],
        scratch_shapes=[pltpu.VMEM((G, RET_QK, RET_V), F32)],
        input_output_aliases={8: 0},
        compiler_params=_params(("parallel", "arbitrary")),
    )(proj, cosr, sinr, lgam, gain, ry, dgated, states, dproj)


def _rope_pe(t, c, s1, s2):
    return t * c + pltpu.roll(t, LANES - QK_ROPE // 2, 1) * s1 + pltpu.roll(t, QK_ROPE // 2, 1) * s2


def _rope_pe_t(d, c, s1, s2):
    return d * c + pltpu.roll(d * s1, QK_ROPE // 2, 1) + pltpu.roll(d * s2, LANES - QK_ROPE // 2, 1)


ATTN_C2 = (QK_NOPE + QK_ROPE) ** -0.5 * LOG2E


def _qkv_proj(cqn, ckvn, wq, wkv, kpe, tabs, MH, *, tm=512, heads=4):
    S = cqn.shape[0]
    tm = _tile(S, tm)
    hb = _tile(MH, heads)
    W = 2 * LANES
    c_t, s1_t, s2_t = tabs

    def body(cq_ref, ckv_ref, wq_ref, wkv_ref, kpe_ref, c_ref, s1_ref, s2_ref, qf_ref, kf_ref, v_ref):
        c, s1, s2 = c_ref[...], s1_ref[...], s2_ref[...]
        q = _dot(cq_ref[...], wq_ref[...], NN)
        kv = _dot(ckv_ref[...], wkv_ref[...], NN)
        kper = _rope_pe(kpe_ref[...], c, s1, s2).astype(BF16)
        for h in range(hb):
            lo, mid, hi = h * W, h * W + QK_NOPE, (h + 1) * W
            qf_ref[:, lo:mid] = (q[:, lo:mid] * ATTN_C2).astype(BF16)
            qf_ref[:, mid:hi] = (_rope_pe(q[:, mid:hi], c, s1, s2) * ATTN_C2).astype(BF16)
            kf_ref[:, lo:mid] = kv[:, lo:mid].astype(BF16)
            kf_ref[:, mid:hi] = kper
            v_ref[:, h * V_HEAD:(h + 1) * V_HEAD] = kv[:, mid:hi].astype(BF16)

    tab = pl.BlockSpec((tm, LANES), lambda i, j: (i, 0))
    grp = pl.BlockSpec((tm, hb * W), lambda i, j: (i, j))
    return _pallas(
        body, name="qkv_proj", grid=(S // tm, MH // hb),
        in_specs=[pl.BlockSpec((tm, cqn.shape[1]), lambda i, j: (i, 0)),
                  pl.BlockSpec((tm, ckvn.shape[1]), lambda i, j: (i, 0)),
                  pl.BlockSpec((wq.shape[0], hb * W), lambda i, j: (0, j)),
                  pl.BlockSpec((wkv.shape[0], hb * W), lambda i, j: (0, j)), tab, tab, tab, tab],
        out_specs=[grp, grp, pl.BlockSpec((tm, hb * V_HEAD), lambda i, j: (i, j))],
        out_shape=[jax.ShapeDtypeStruct((S, MH * W), BF16)] * 2 + [jax.ShapeDtypeStruct((S, MH * V_HEAD), BF16)],
        compiler_params=_params(("parallel", "parallel")),
    )(cqn, ckvn, wq, wkv, kpe, c_t, s1_t, s2_t)


def _chunk_mask(T):
    n = lax.broadcasted_iota(jnp.int32, (T, T), 0)
    m = lax.broadcasted_iota(jnp.int32, (T, T), 1)
    return (m // CHUNK) <= (n // CHUNK)


def _lanes_to(v, width):
    return jnp.tile(v, (1, width // LANES))


def _attn_fwd(qf, kf, vb, MH, *, T):
    S = qf.shape[0]
    nt = S // T

    def body(q_ref, k_ref, v_ref, o_ref, ob_ref, lse_ref, m_sc, l_sc, acc_sc, s_a, s_b):
        qi = pl.program_id(1)
        m_sc[...] = jnp.full_like(m_sc, NEG)
        l_sc[...] = jnp.zeros_like(l_sc)
        acc_sc[...] = jnp.zeros_like(acc_sc)

        def rows_of(kt):
            return pl.ds(pl.multiple_of(kt * T, T), T)

        def scores(kt):
            return _dot(q_ref[...], k_ref[rows_of(kt), :], NT)

        def update(s, kt):
            m_prev = m_sc[...]
            m_new = jnp.maximum(m_prev, jnp.max(s, axis=-1, keepdims=True))
            alpha = jnp.exp2(m_prev - m_new)
            p = jnp.exp2(s - _lanes_to(m_new, T))
            l_sc[...] = alpha * l_sc[...] + jnp.sum(p, axis=-1, keepdims=True)
            acc_sc[...] = alpha * acc_sc[...] + _dot(p, v_ref[rows_of(kt), :], NN)
            m_sc[...] = m_new

        def masked(s):
            return jnp.where(_chunk_mask(T), s, NEG)

        @pl.when(qi == 0)
        def _():
            update(masked(scores(0)), 0)

        @pl.when(qi > 0)
        def _():
            s_a[...] = masked(scores(qi))
            s_b[...] = scores(0)
            update(s_a[...], qi)
            s_a[...] = scores(jnp.minimum(1, qi - 1))
            update(s_b[...], 0)

            def pair(j, carry):
                s_b[...] = scores(2 * j)
                update(s_a[...], 2 * j - 1)
                s_a[...] = scores(jnp.minimum(2 * j + 1, qi - 1))
                update(s_b[...], 2 * j)
                return carry

            lax.fori_loop(1, (qi + 1) // 2, pair, 0)

            @pl.when(qi % 2 == 0)
            def _():
                update(s_a[...], qi - 1)
        l = l_sc[...]
        o = acc_sc[...] / l
        o_ref[...] = o
        ob_ref[...] = o.astype(BF16)
        lse_ref[...] = m_sc[...] + jnp.log(l) * LOG2E

    return _pallas(
        body, name="attn_fwd", grid=(MH, nt),
        in_specs=[pl.BlockSpec((T, 2 * LANES), lambda h, i: (i, h)),
                  pl.BlockSpec((S, 2 * LANES), lambda h, i: (0, h)),
                  pl.BlockSpec((S, LANES), lambda h, i: (0, h))],
        out_specs=[pl.BlockSpec((T, LANES), lambda h, i: (i, h)), pl.BlockSpec((T, LANES), lambda h, i: (i, h)),
                   pl.BlockSpec((None, T, LANES), lambda h, i: (h, i, 0))],
        out_shape=[jax.ShapeDtypeStruct((S, MH * LANES), F32), jax.ShapeDtypeStruct((S, MH * LANES), BF16),
                   jax.ShapeDtypeStruct((MH, S, LANES), F32)],
        scratch_shapes=[pltpu.VMEM((T, LANES), F32), pltpu.VMEM((T, LANES), F32), pltpu.VMEM((T, LANES), F32),
                        pltpu.VMEM((T, T), F32), pltpu.VMEM((T, T), F32)],
        compiler_params=_params(("parallel", "parallel")),
    )(qf, kf, vb)


def _attn_bwd(qf, kf, vb, dob, lse2, delta, tabs, MH, *, T, deps=()):
    S = qf.shape[0]
    nt = S // T
    scale = (QK_NOPE + QK_ROPE) ** -0.5
    n_dep = len(deps)

    def body(q_ref, k_ref, v_ref, do_ref, lse_ref, dl_ref, c_ref, s1_ref, s2_ref, *rest):
        dqa_ref, dkv_ref, dkpe_ref, dq_ref, dk_sc, dv_sc, s_a, dp_a, s_b, dp_b = rest[n_dep:]
        kj = pl.program_id(1)

        @pl.when(kj == 0)
        def _():
            dq_ref[...] = jnp.zeros_like(dq_ref)

        dk_sc[...] = jnp.zeros_like(dk_sc)
        dv_sc[...] = jnp.zeros_like(dv_sc)

        def rows_of(qt):
            return pl.ds(pl.multiple_of(qt * T, T), T)

        def products(qt):
            rows = rows_of(qt)
            return _dot(q_ref[rows, :], k_ref[...], NT), _dot(do_ref[rows, :], v_ref[...], NT)

        def update(s, dp, qt):
            rows = rows_of(qt)
            q, dov = q_ref[rows, :], do_ref[rows, :]
            p = jnp.exp2(s - _lanes_to(lse_ref[rows, :], T))
            ds = p * (dp - _lanes_to(dl_ref[rows, :], T))
            dv_sc[...] += _dot(p, dov, TN)
            dk_sc[...] += _dot(ds, q, TN)
            dq_ref[rows, :] += _dot(ds, k_ref[...], NN)

        def masked(s):
            return jnp.where(_chunk_mask(T), s, NEG)

        @pl.when(kj == nt - 1)
        def _():
            s, dp = products(kj)
            update(masked(s), dp, kj)

        @pl.when(kj < nt - 1)
        def _():
            s, dp = products(kj)
            s_a[...], dp_a[...] = masked(s), dp
            s_b[...], dp_b[...] = products(kj + 1)
            update(s_a[...], dp_a[...], kj)
            s_a[...], dp_a[...] = products(jnp.minimum(kj + 2, nt - 1))
            update(s_b[...], dp_b[...], kj + 1)

            def pair(j, carry):
                t0 = kj + 2 * j
                s_b[...], dp_b[...] = products(t0 + 1)
                update(s_a[...], dp_a[...], t0)
                s_a[...], dp_a[...] = products(jnp.minimum(t0 + 2, nt - 1))
                update(s_b[...], dp_b[...], t0 + 1)
                return carry

            lax.fori_loop(1, (nt - kj) // 2, pair, 0)

            @pl.when((nt - kj) % 2 == 1)
            def _():
                update(s_a[...], dp_a[...], nt - 1)
        dkv_ref[:, :QK_NOPE] = (dk_sc[:, :QK_NOPE] * (1.0 / LOG2E)).astype(BF16)
        dkv_ref[:, QK_NOPE:] = dv_sc[...].astype(BF16)
        dkpe_ref[...] = dk_sc[:, QK_NOPE:] * (1.0 / LOG2E)

        @pl.when(kj == nt - 1)
        def _():
            dqa_ref[:, :QK_NOPE] = (dq_ref[:, :QK_NOPE] * scale).astype(BF16)
            dqa_ref[:, QK_NOPE:] = (_rope_pe_t(dq_ref[:, QK_NOPE:], c_ref[...], s1_ref[...], s2_ref[...])
                                    * scale).astype(BF16)

    stat = pl.BlockSpec((None, S, LANES), lambda h, j: (h, 0, 0))
    tab = pl.BlockSpec((S, LANES), lambda h, j: (0, 0))
    return _pallas(
        body, name="attn_bwd", grid=(MH, nt),
        in_specs=[pl.BlockSpec((S, 2 * LANES), lambda h, j: (0, h)),
                  pl.BlockSpec((T, 2 * LANES), lambda h, j: (j, h)),
                  pl.BlockSpec((T, LANES), lambda h, j: (j, h)),
                  pl.BlockSpec((S, LANES), lambda h, j: (0, h)), stat, stat, tab, tab, tab] + [_ANY] * n_dep,
        out_specs=[pl.BlockSpec((S, 2 * LANES), lambda h, j: (0, h)),
                   pl.BlockSpec((T, 2 * LANES), lambda h, j: (j, h)),
                   pl.BlockSpec((T, LANES), lambda h, j: (j, h))],
        out_shape=[jax.ShapeDtypeStruct((S, MH * 2 * LANES), BF16), jax.ShapeDtypeStruct((S, MH * 2 * LANES), BF16),
                   jax.ShapeDtypeStruct((S, MH * LANES), F32)],
        scratch_shapes=[pltpu.VMEM((S, 2 * LANES), F32), pltpu.VMEM((T, 2 * LANES), F32), pltpu.VMEM((T, LANES), F32)]
        + [pltpu.VMEM((T, T), F32)] * 4,
        compiler_params=_params(("parallel", "arbitrary")),
    )(qf, kf, vb, dob, lse2, delta, *tabs, *deps)


def _kpe_sum(dkpe_h, tabs, MH, *, tr=256):
    S = dkpe_h.shape[0]
    tr = _tile(S, tr)

    def body(dk_ref, c_ref, s1_ref, s2_ref, dkpe_ref):
        tot = dk_ref[:, :LANES]
        for h in range(1, MH):
            tot = tot + dk_ref[:, h * LANES:(h + 1) * LANES]
        dkpe_ref[...] = _rope_pe_t(tot, c_ref[...], s1_ref[...], s2_ref[...]).astype(BF16)

    tab = pl.BlockSpec((tr, LANES), lambda i: (i, 0))
    return _pallas(
        body, name="kpe_sum", grid=(S // tr,),
        in_specs=[pl.BlockSpec((tr, MH * LANES), lambda i: (i, 0)), tab, tab, tab],
        out_specs=tab, out_shape=jax.ShapeDtypeStruct((S, LANES), BF16),
        compiler_params=_params(("parallel",)),
    )(dkpe_h, *tabs)


ROW_ALIGN = 16


def _blk(R, C, block_bytes=2 << 20):
    cap = max(ROW_ALIGN, block_bytes // (C * 4))
    for t in range(min(R, cap) // ROW_ALIGN * ROW_ALIGN, LANES - 1, -ROW_ALIGN):
        if R % t == 0:
            return t, C
    if R <= cap:
        return R, C
    tc = C
    while R * tc * 4 > block_bytes and tc % (2 * LANES) == 0:
        tc //= 2
    return R, tc


def _rows_call(fn, ins, out_dtypes, *, name):
    R, C = ins[0].shape
    tr, tc = _blk(R, C)
    n_in = len(ins)

    def body(*refs):
        vals = fn(*[r[...] for r in refs[:n_in]])
        for r, v in zip(refs[n_in:], vals):
            r[...] = v.astype(r.dtype)

    blk = pl.BlockSpec((tr, tc), lambda i, j: (i, j))
    res = _pallas(
        body, name=name, grid=(R // tr, C // tc), in_specs=[blk] * n_in, out_specs=[blk] * len(out_dtypes),
        out_shape=[jax.ShapeDtypeStruct((R, C), d) for d in out_dtypes],
        compiler_params=_params(("parallel", "parallel")),
    )(*ins)
    return res


def _adamw_vals(w, g, m, v):
    m = ADAM_B1 * m + (1.0 - ADAM_B1) * g
    v = ADAM_B2 * v + (1.0 - ADAM_B2) * (g * g)
    m_hat = m / (1.0 - ADAM_B1 ** ADAM_STEP)
    v_hat = v / (1.0 - ADAM_B2 ** ADAM_STEP)
    delta = -ADAM_LR * (m_hat / (jnp.sqrt(v_hat) + ADAM_EPS) + ADAM_WD * w)
    return delta, m, v


def _sum_pair(p, theirs, place, *, name):
    _, R, C = p.shape
    R2 = R // 2
    tr, tc = _blk(R2, C)
    p4 = p.reshape(N_CHIPS, 2, R2, C)

    def body(place_ref, a_ref, b_ref, o_ref):
        o_ref[...] = (a_ref[...].astype(F32) + b_ref[...].astype(F32)).astype(BF16)

    spec = pltpu.PrefetchScalarGridSpec(
        num_scalar_prefetch=1, grid=(N_CHIPS, R2 // tr, C // tc),
        in_specs=[pl.BlockSpec((None, None, tr, tc), lambda q, i, j, pr: (q, pr[0], i, j)),
                  pl.BlockSpec((None, tr, tc), lambda q, i, j, pr: (q, i, j))],
        out_specs=pl.BlockSpec((None, tr, tc), lambda q, i, j, pr: (q, i, j)))
    return _pallas(body, name=name, grid_spec=spec, out_shape=jax.ShapeDtypeStruct((N_CHIPS, R2, C), BF16),
                   compiler_params=_params(("parallel", "parallel", "parallel")))(place, p4, theirs)


def _sum_chips(p, theirs, recv, place, *, name):
    _, R, C = p.shape
    R2 = R // 2
    tr, tc = _blk(R2, C)
    p4 = p.reshape(N_CHIPS, 2, R2, C)

    def body(place_ref, a_ref, b_ref, r0_ref, r1_ref, r2_ref, o_ref):
        own = a_ref[...].astype(F32) + b_ref[...].astype(F32)
        o_ref[...] = ((own + r0_ref[...].astype(F32)) + r1_ref[...].astype(F32)) + r2_ref[...].astype(F32)

    def slot(k):
        return pl.BlockSpec((None, tr, tc), lambda i, j, pr: (k, i, j))

    spec = pltpu.PrefetchScalarGridSpec(
        num_scalar_prefetch=1, grid=(R2 // tr, C // tc),
        in_specs=[pl.BlockSpec((None, None, tr, tc), lambda i, j, pr: (pr[1], pr[0], i, j)),
                  pl.BlockSpec((None, tr, tc), lambda i, j, pr: (pr[1], i, j)), slot(0), slot(1), slot(2)],
        out_specs=pl.BlockSpec((None, tr, tc), lambda i, j, pr: (pr[0], i, j)))
    return _pallas(body, name=name, grid_spec=spec, out_shape=jax.ShapeDtypeStruct((2, R2, C), F32),
                   compiler_params=_params(("parallel", "parallel")))(place, p4, theirs, recv, recv, recv)


def _me():
    return lax.axis_index("x"), lax.axis_index("y"), lax.axis_index("c")


def _other_chips(x, y):
    return [(1 - x, y), (x, 1 - y), (1 - x, 1 - y)]


def _rcopy(src, dst, ssem, rsem, dev):
    return pltpu.make_async_remote_copy(src_ref=src, dst_ref=dst, send_sem=ssem, recv_sem=rsem,
                                        device_id=dev, device_id_type=MESH)


def _cast_into_slot(w, place, *, name, rows=None, deps=()):
    R, C = w.shape
    rows = R if rows is None else rows
    tr, tc = _blk(R, C)

    def body(place_ref, w_ref, *rest):
        rest[-1][...] = w_ref[...].astype(BF16)

    spec = pltpu.PrefetchScalarGridSpec(
        num_scalar_prefetch=1, grid=(R // tr, C // tc),
        in_specs=[pl.BlockSpec((tr, tc), lambda i, j, pr: (i, j))] + [_ANY] * len(deps),
        out_specs=pl.BlockSpec((None, tr, tc), lambda i, j, pr: (pr[1], i, j)))
    out = _pallas(body, name=name, grid_spec=spec, out_shape=jax.ShapeDtypeStruct((N_CHIPS, rows, C), BF16),
                  compiler_params=_params(("parallel", "parallel")))(place, w, *deps)
    return out.reshape(N_CHIPS, 2, rows // 2, C)


def _gather_ici_plan(bufs):
    x, y, c = _me()
    j = 2 * x + y
    plan = []
    for i, buf in enumerate(bufs):
        for k, (px, py) in enumerate(_other_chips(x, y)):
            plan.append((3 * i + k, buf.at[j, c], buf.at[j, c], (px, py, c)))
    return plan


def _forward_halves(bufs, *, name):
    n = len(bufs)

    def body(*refs):
        outs = refs[n:2 * n]
        ssem, rsem = refs[2 * n:]
        x, y, c = _me()
        sib = (x, y, 1 - c)
        cps = []
        for i in range(n):
            for k, (px, py) in enumerate(_other_chips(x, y)):
                slot = outs[i].at[2 * px + py, c]
                r = _rcopy(slot, slot, ssem.at[3 * i + k], rsem.at[3 * i + k], sib)
                r.start()
                cps.append(r)
        for r in cps:
            r.wait()

    return _pallas(
        body, name=name, in_specs=[_ANY] * n, out_specs=[_ANY] * n,
        out_shape=[jax.ShapeDtypeStruct(b.shape, b.dtype) for b in bufs],
        scratch_shapes=[pltpu.SemaphoreType.DMA((3 * n,))] * 2,
        input_output_aliases={i: i for i in range(n)},
        compiler_params=pltpu.CompilerParams(has_side_effects=True),
    )(*bufs)


_HBM = pl.BlockSpec(memory_space=pltpu.HBM)
_SEM = pl.BlockSpec(memory_space=pltpu.SEMAPHORE)
_EFFECT = pltpu.SideEffectType.DATAFLOW_SIDE_EFFECTING


def _split_start(bufs, plan, n_copies, *, name):
    n = len(bufs)

    def body(*refs):
        ssem, rsem = refs[n], refs[n + 1]
        for s, src, dst, dev in plan(refs[:n]):
            _rcopy(src, dst, ssem.at[s], rsem.at[s], dev).start()
        refs[-1][...] = jnp.zeros_like(refs[-1])

    res = _pallas(
        body, name=name, in_specs=[_HBM] * n,
        out_specs=(_SEM, _SEM, *[_HBM] * n, pl.BlockSpec(memory_space=pltpu.VMEM)),
        out_shape=(pltpu.SemaphoreType.DMA((n_copies,)), pltpu.SemaphoreType.DMA((n_copies,)),
                   *[pltpu.HBM(b.shape, b.dtype) for b in bufs], jax.ShapeDtypeStruct((8, LANES), F32)),
        input_output_aliases={i: 2 + i for i in range(n)},
        compiler_params=pltpu.CompilerParams(has_side_effects=_EFFECT),
    )(*[pltpu.with_memory_space_constraint(b, pltpu.HBM) for b in bufs])
    return res[0], res[1], list(res[2:2 + n]), res[-1]


def _split_wait(ssem, rsem, bufs, after, plan, *, name):
    n = len(bufs)

    def body(*refs):
        ssem_ref, rsem_ref = refs[n], refs[n + 1]
        for s, src, dst, dev in plan(refs[:n]):
            cp = _rcopy(src, dst, ssem_ref.at[s], rsem_ref.at[s], dev)
            cp.wait_send()
            cp.wait_recv()

    return list(_pallas(
        body, name=name, in_specs=[_HBM] * n + [_SEM, _SEM, _ANY], out_specs=[_HBM] * n,
        out_shape=[pltpu.HBM(b.shape, b.dtype) for b in bufs],
        input_output_aliases={i: i for i in range(n)},
        compiler_params=pltpu.CompilerParams(has_side_effects=_EFFECT),
    )(*bufs, ssem, rsem, after))


def _swap_plan(n):
    def plan(bufs):
        x, y, c = _me()
        return [(i, bufs[i].at[:, 1 - c], bufs[n + i], (x, y, 1 - c)) for i in range(n)]
    return plan


def _scatter_plan(n):
    def plan(bufs):
        x, y, c = _me()
        out = []
        for i in range(n):
            for k, (px, py) in enumerate(_other_chips(x, y)):
                out.append((3 * i + k, bufs[i].at[2 * px + py], bufs[n + i].at[k], (px, py, c)))
        return out
    return plan


def _swap_halves(grads, *, name):
    n = len(grads)
    views = [g.reshape(N_CHIPS, 2, g.shape[1] // 2, g.shape[2]) for g in grads]

    def body(*refs):
        ins, outs = refs[:n], refs[n:2 * n]
        ssem, rsem = refs[2 * n:]
        x, y, c = _me()
        sib = (x, y, 1 - c)
        cps = []
        for i in range(n):
            r = _rcopy(ins[i].at[:, 1 - c], outs[i], ssem.at[i], rsem.at[i], sib)
            r.start()
            cps.append(r)
        for r in cps:
            r.wait()

    return _pallas(
        body, name=name, in_specs=[_ANY] * n, out_specs=[_ANY] * n,
        out_shape=[jax.ShapeDtypeStruct((N_CHIPS,) + v.shape[2:], v.dtype) for v in views],
        scratch_shapes=[pltpu.SemaphoreType.DMA((n,)), pltpu.SemaphoreType.DMA((n,))],
        compiler_params=pltpu.CompilerParams(has_side_effects=True),
    )(*views)


def _join_halves(halves, *, name):
    n = len(halves)

    def body(*refs):
        outs = refs[n:2 * n]
        ssem, rsem = refs[2 * n:]
        x, y, c = _me()
        sib = (x, y, 1 - c)
        cps = []
        for i in range(n):
            r = _rcopy(outs[i].at[c], outs[i].at[c], ssem.at[i], rsem.at[i], sib)
            r.start()
            cps.append(r)
        for r in cps:
            r.wait()

    return _pallas(
        body, name=name, in_specs=[_ANY] * n, out_specs=[_ANY] * n,
        out_shape=[jax.ShapeDtypeStruct(h.shape, h.dtype) for h in halves],
        scratch_shapes=[pltpu.SemaphoreType.DMA((n,)), pltpu.SemaphoreType.DMA((n,))],
        input_output_aliases={i: i for i in range(n)},
        compiler_params=pltpu.CompilerParams(has_side_effects=True),
    )(*halves)


def _allreduce_small(parts, loss11):
    n = len(parts)
    widths = [p.shape[1] for p in parts]
    total = sum(widths) + LANES

    def body(*refs):
        o_ref, mine, buf, ssem, rsem = refs[n + 1:]
        x, y, c = _me()
        me = 4 * x + 2 * y + c
        off = 0
        for r, w in zip(refs[:n], widths):
            mine[:, off:off + w] = r[...]
            off += w
        mine[:, off:] = jnp.broadcast_to(refs[n][...], (1, LANES))
        buf[me] = mine[...]
        cps = []
        for k in range(1, 8):
            peer = (x ^ (k >> 2), y ^ ((k >> 1) & 1), c ^ (k & 1))
            r = _rcopy(mine, buf.at[me], ssem.at[k - 1], rsem.at[k - 1], peer)
            r.start()
            cps.append(r)
        for k in range(1, 8):
            peer = (x ^ (k >> 2), y ^ ((k >> 1) & 1), c ^ (k & 1))
            pid = 4 * peer[0] + 2 * peer[1] + peer[2]
            _rcopy(mine, buf.at[pid], ssem.at[k - 1], rsem.at[k - 1], peer).wait_recv()
        for r in cps:
            r.wait_send()
        tot = buf[0]
        for d in range(1, 8):
            tot = tot + buf[d]
        o_ref[...] = tot

    vm = pl.BlockSpec(memory_space=pltpu.VMEM)
    return _pallas(
        body, name="allreduce_small", in_specs=[vm] * (n + 1), out_specs=vm,
        out_shape=jax.ShapeDtypeStruct((1, total), F32),
        scratch_shapes=[pltpu.VMEM((1, total), F32), pltpu.VMEM((8, 1, total), F32),
                        pltpu.SemaphoreType.DMA((7,)), pltpu.SemaphoreType.DMA((7,))],
        compiler_params=pltpu.CompilerParams(has_side_effects=True),
    )(*parts, loss11)


def _adamw_small(red, ws, ms, vs):
    n = len(ws)

    def body(*refs):
        red_ref = refs[0]
        outs = refs[1 + 3 * n:]
        off = 0
        for i in range(n):
            w = refs[1 + i].shape[1]
            g = red_ref[:, off:off + w]
            d, m, v = _adamw_vals(refs[1 + i][...], g, refs[1 + n + i][...], refs[1 + 2 * n + i][...])
            for o, val in zip(outs[4 * i:4 * i + 4], (g, d, m, v)):
                o[...] = val
            off += w

    vm = pl.BlockSpec(memory_space=pltpu.VMEM)
    res = _pallas(
        body, name="adamw_small", in_specs=[vm] * (1 + 3 * n), out_specs=[vm] * (4 * n),
        out_shape=[jax.ShapeDtypeStruct(w.shape, F32) for w in ws for _ in range(4)],
    )(red, *ws, *ms, *vs)
    return [res[4 * i:4 * i + 4] for i in range(n)]


def _rope_tables(positions, S):
    pos = positions.reshape(S, 1).astype(F32)
    half = RET_QK // 2
    inv = ROPE_THETA ** (-jnp.arange(half, dtype=F32) / half)
    ang = pos * inv
    cosr = jnp.concatenate([jnp.cos(ang), jnp.cos(ang)], axis=1)
    sinr = jnp.concatenate([-jnp.sin(ang), jnp.sin(ang)], axis=1)
    half = QK_ROPE // 2
    inv = ROPE_THETA ** (-jnp.arange(half, dtype=F32) / half)
    ang = pos * inv
    z = jnp.zeros((S, half), F32)
    c = jnp.concatenate([jnp.cos(ang), jnp.cos(ang), z, z], axis=1)
    s1 = jnp.concatenate([-jnp.sin(ang), z, z, z], axis=1)
    s2 = jnp.concatenate([z, jnp.sin(ang), z, z], axis=1)
    return cosr, sinr, (c, s1, s2)


def _cat_cols(g):
    return jnp.concatenate([g[j] for j in range(N_CHIPS)], axis=1)


def _split_cols(w):
    return jnp.stack(jnp.split(w, N_CHIPS, axis=1))


def kernel(x, positions, norm_mix_g, w_in, ret_norm_g, w_ret_o, q_a_norm_g, w_q_b, kv_a_norm_g, w_kv_b, w_mla_o, w_out, norm_mlp_g, w_up, w_down, norm_f_g, loss_target, m_norm_mix_g, m_w_in, m_ret_norm_g, m_w_ret_o, m_q_a_norm_g, m_w_q_b, m_kv_a_norm_g, m_w_kv_b, m_w_mla_o, m_w_out, m_norm_mlp_g, m_w_up, m_w_down, m_norm_f_g, v_norm_mix_g, v_w_in, v_ret_norm_g, v_w_ret_o, v_q_a_norm_g, v_w_q_b, v_kv_a_norm_g, v_w_kv_b, v_w_mla_o, v_w_out, v_norm_mlp_g, v_w_up, v_w_down, v_norm_f_g):
    S, D = x.shape[1], x.shape[2]
    RVW = w_ret_o.shape[1] * N_CHIPS
    RH = RVW // RET_V
    RQW = RH * RET_QK
    MVW = w_mla_o.shape[1] * N_CHIPS
    MH = MVW // V_HEAD
    QL, KVL = w_q_b.shape[1], w_kv_b.shape[1]
    T_RET = _tile(S, 256)
    T_ATT = _tile(S, 512)

    xs = x.reshape(S, D)
    tgt = loss_target.reshape(S, D)
    cosr, sinr, pe_tabs = _rope_tables(positions, S)
    lgam = jnp.log(1.0 - 2.0 ** (-5.0 - jnp.arange(RH, dtype=F32)))
    lgam = jnp.broadcast_to(lgam[:, None, None], (RH, 8, LANES))

    big = ("w_in", "w_ret_o", "w_q_b", "w_kv_b", "w_mla_o", "w_out", "w_up", "w_down")
    w_sh = dict(w_in=w_in[0].T, w_ret_o=w_ret_o[0], w_q_b=w_q_b[0], w_kv_b=w_kv_b[0], w_mla_o=w_mla_o[0],
                w_out=w_out[0], w_up=w_up[0], w_down=w_down[0])
    m_sh = dict(w_in=m_w_in[0].T, w_ret_o=m_w_ret_o[0], w_q_b=m_w_q_b[0], w_kv_b=m_w_kv_b[0],
                w_mla_o=m_w_mla_o[0], w_out=m_w_out[0], w_up=m_w_up[0], w_down=m_w_down[0])
    v_sh = dict(w_in=v_w_in[0].T, w_ret_o=v_w_ret_o[0], w_q_b=v_w_q_b[0], w_kv_b=v_w_kv_b[0],
                w_mla_o=v_w_mla_o[0], w_out=v_w_out[0], w_up=v_w_up[0], w_down=v_w_down[0])
    col_sharded = ("w_q_b", "w_kv_b", "w_up")
    c_sh = w_in.shape[2]
    c_pad = -(-c_sh // 64) * 64
    place = jnp.stack([lax.axis_index("c"), 2 * lax.axis_index("x") + lax.axis_index("y")]).astype(jnp.int32)

    def whole(k, g):
        g = g.reshape(N_CHIPS, w_sh[k].shape[0], w_sh[k].shape[1])
        if k == "w_up":
            return g
        return _cat_cols(g) if k in col_sharded else g.reshape(-1, g.shape[2])

    first = ("w_in", "w_q_b", "w_kv_b")
    later = ("w_ret_o", "w_mla_o", "w_out", "w_up", "w_down")
    first_bufs = [_cast_into_slot(w_sh[k], place, name="cast_" + k, rows=c_pad if k == "w_in" else None)
                  for k in first]
    first_ssem, first_rsem, first_bufs, first_token = _split_start(
        first_bufs, _gather_ici_plan, 3 * len(first), name="gather_first_start")
    later_bufs = [_cast_into_slot(w_sh[k], place, name="cast_" + k, deps=(first_token,)) for k in later[:-1]]
    first_bufs = _split_wait(first_ssem, first_rsem, first_bufs, later_bufs[-1], _gather_ici_plan,
                             name="gather_first_wait")
    got = _forward_halves(first_bufs, name="gather_first_forward")
    full = {k: whole(k, g) for k, g in zip(first[1:], got[1:])}
    later_bufs.append(_cast_into_slot(w_sh[later[-1]], place, name="cast_" + later[-1], deps=(got[0],)))
    later_ssem, later_rsem, later_bufs, later_token = _split_start(
        later_bufs, _gather_ici_plan, 3 * len(later), name="gather_later_start")

    o_rq, o_rk, o_rv, o_rg = 0, RQW, 2 * RQW, 2 * RQW + RVW
    o_cq = 2 * RQW + 2 * RVW
    o_ckv, o_kpe = o_cq + QL, o_cq + QL + KVL
    o_gr = o_kpe + QK_ROPE
    o_gm = o_gr + D
    n_ret = RH * RET_HEAD_COLS
    off_gret, off_gmla, off_cq, off_ckv = n_ret, n_ret + D, n_ret + 2 * D, n_ret + 2 * D + QL
    n_a = off_ckv + KVL
    runs = []
    for h in range(RH):
        base = h * RET_HEAD_COLS
        runs += [(o_rq + h * RET_QK, RET_QK, base), (o_rk + h * RET_QK, RET_QK, base + RET_QK),
                 (o_rv + h * RET_V, RET_V, base + 2 * RET_QK), (o_rg + h * RET_V, RET_V, base + 2 * RET_QK + RET_V)]
    runs += [(o_gr, D, off_gret), (o_gm, D, off_gmla), (o_cq, QL, off_cq), (o_ckv, KVL, off_ckv),
             (o_kpe, QK_ROPE, n_a)]

    def take(parts, start, width):
        out, lo = [], 0
        for p in parts:
            hi = lo + p.shape[0]
            a, b = max(start, lo), min(start + width, hi)
            if a < b:
                out.append(p[a - lo:b - lo])
            lo = hi
        return out

    wi = [got[0].reshape(N_CHIPS, c_pad, D)[jj, :c_sh] for jj in range(N_CHIPS)]
    here = sorted(runs, key=lambda r: r[2])
    wa = jnp.concatenate([p for s0, w, _ in here[:-1] for p in take(wi, s0, w)], axis=0)
    wkpe = jnp.concatenate(take(wi, o_kpe, QK_ROPE) + [jnp.zeros((LANES - QK_ROPE, D), BF16)], axis=0)
    wq = jnp.pad(full["w_q_b"].reshape(QL, MH, QK_NOPE + QK_ROPE),
                 ((0, 0), (0, 0), (0, LANES - QK_ROPE))).reshape(QL, MH * 2 * LANES)
    wkv = full["w_kv_b"]

    u, rstd0 = _rmsnorm_fwd(xs, norm_mix_g, name="norm_mix")
    proj = _mm(u, wa, mode="nt", outs=[F32], name="in_proj", deps=(later_token,))
    kpe = _mm(u, wkpe, mode="nt", outs=[F32], name="kpe_proj")
    ry, gated, states = _ret_fwd(proj, cosr, sinr, lgam, ret_norm_g, RH, T=T_RET)
    cqn, rstd_q = _rmsnorm_fwd(proj, q_a_norm_g, name="norm_q", width=QL, col=off_cq // QL)
    ckvn, rstd_kv = _rmsnorm_fwd(proj, kv_a_norm_g, name="norm_kv", width=KVL, col=off_ckv // KVL)
    qf, kf, vb = _qkv_proj(cqn, ckvn, wq, wkv, kpe, pe_tabs, MH)
    my, my_b, lse2 = _attn_fwd(qf, kf, vb, MH, T=T_ATT)
    later_bufs = _split_wait(later_ssem, later_rsem, later_bufs, my, _gather_ici_plan, name="gather_later_wait")
    later_bufs = _forward_halves(later_bufs, name="gather_later_forward")
    full.update({k: whole(k, g) for k, g in zip(later, later_bufs)})
    y_ret = _mm(gated, full["w_ret_o"], mode="nn", outs=[F32], name="ret_o")
    y_mla = _mm(my_b, full["w_mla_o"], mode="nn", outs=[F32], name="mla_o")
    merged = _merge_fwd(proj, y_ret, y_mla, D, off_gret, off_gmla)
    h1 = _mm(merged, full["w_out"], mode="nn", outs=[F32], name="out_proj",
             epi=lambda acc, r: (acc + r,), extras=(xs,))
    n1, rstd1 = _rmsnorm_fwd(h1, norm_mlp_g, name="norm_mlp")

    def up_epi(acc):
        r = jnp.maximum(acc, 0.0)
        return acc, r * r

    z, act = _mm(n1, full["w_up"], mode="nn", outs=[F32, BF16], name="up_proj", epi=up_epi)
    h2 = _mm(act, full["w_down"], mode="nn", outs=[F32], name="down_proj",
             epi=lambda acc, r: (acc + r,), extras=(h1,))
    loss11, dh2, dh2_b, g_norm_f = _final_loss(h2, norm_f_g.reshape(1, D), tgt)

    dz = _mm(dh2_b, full["w_down"], mode="nt", outs=[BF16], name="down_bwd_x",
             epi=lambda acc, zz: (acc * (2.0 * jnp.maximum(zz, 0.0)),), extras=(z,))
    g_w_down = _mm(act, dh2_b, mode="tn", outs=[BF16], name="down_bwd_w")
    dn1 = _mm(dz, full["w_up"], mode="nt", outs=[F32], name="up_bwd_x")
    g_w_up = _mm(n1, dz, mode="tn", outs=[BF16], name="up_bwd_w", out_shards=True)

    def reduce_begin(tag, names, grads):
        pcs = [g if g.ndim == 3 else g.reshape(N_CHIPS, g.shape[0] // N_CHIPS, g.shape[1]) for g in grads]
        theirs = _swap_halves(pcs, name="swap_" + tag)
        sums = [_sum_pair(p, t, place, name="sum_pair_" + k) for k, p, t in zip(names, pcs, theirs)]
        return pcs, theirs, sums

    def scatter_begin(tag, sums):
        lands = [lax.empty((3,) + s.shape[1:], s.dtype) for s in sums]
        return _split_start(sums + lands, _scatter_plan(len(sums)), 3 * len(sums), name="scatter_" + tag + "_start")

    def swap_begin(tag, grads):
        views = [g if g.ndim == 3 else g.reshape(N_CHIPS, g.shape[0] // N_CHIPS, g.shape[1]) for g in grads]
        views = [v.reshape(N_CHIPS, 2, v.shape[1] // 2, v.shape[2]) for v in views]
        lands = [lax.empty((N_CHIPS,) + v.shape[2:], v.dtype) for v in views]
        return _split_start(views + lands, _swap_plan(len(views)), len(views), name="swap_" + tag + "_start")

    def swap_end(tag, names, handle, after):
        n = len(names)
        bufs = _split_wait(handle[0], handle[1], handle[2], after, _swap_plan(n), name="swap_" + tag + "_wait")
        pcs = [b.reshape(N_CHIPS, 2 * b.shape[2], b.shape[3]) for b in bufs[:n]]
        sums = [_sum_pair(p, t, place, name="sum_pair_" + k) for k, p, t in zip(names, pcs, bufs[n:])]
        return pcs, bufs[n:], sums

    g1 = ("w_up", "w_down")
    swap1 = swap_begin("g1", (g_w_up, g_w_down))
    dh1, g_norm_mlp, dh1_b = _rmsnorm_bwd(dn1, h1, rstd1, norm_mlp_g, name="norm_mlp_bwd", res=dh2,
                                          deps=(swap1[3],), bf16_copy=1)
    dmerged = _mm(dh1_b, full["w_out"], mode="nt", outs=[F32], name="out_bwd_x")
    pcs1, theirs1, sums1 = swap_end("g1", g1, swap1, dmerged)
    ssem1, rsem1, bufs1, token1 = scatter_begin("g1", sums1)
    g_w_out = _mm(merged, dh1_b, mode="tn", outs=[BF16], name="out_bwd_w", deps=(token1,))
    dproj, dy_ret, dy_mla = _merge_bwd(dmerged, proj, y_ret, y_mla, D, off_gret)
    dgated = _mm(dy_ret, full["w_ret_o"], mode="nt", outs=[F32], name="ret_o_bwd_x")
    g_w_ret_o = _mm(gated, dy_ret, mode="tn", outs=[BF16], name="ret_o_bwd_w")
    dproj, g_ret_norm = _ret_bwd(proj, cosr, sinr, lgam, ret_norm_g, ry, dgated, states, dproj, RH, T=T_RET)
    def delta_epi(acc, o):
        rows = acc.shape[0]
        return acc, [jnp.broadcast_to(jnp.sum(acc[:, lo:lo + V_HEAD] * o[:, lo:lo + V_HEAD], axis=-1, keepdims=True),
                                      (rows, LANES)) for lo in range(0, acc.shape[1], V_HEAD)]

    dob, delta = _mm(dy_mla, full["w_mla_o"], mode="nt", outs=[BF16], name="mla_o_bwd_x", epi=delta_epi,
                     extras=(my,), more_outs=lambda tm, tn: [
                         (jax.ShapeDtypeStruct((MH, S, LANES), F32),
                          pl.BlockSpec((tn // V_HEAD, tm, LANES), lambda i, j, k: (j, i, 0)))])
    g_w_mla_o = _mm(my_b, dy_mla, mode="tn", outs=[BF16], name="mla_o_bwd_w")
    g2 = ("w_out", "w_ret_o", "w_mla_o")
    swap2 = swap_begin("g2", (g_w_out, g_w_ret_o, g_w_mla_o))
    dq_all, dkv_all, dkpe_h = _attn_bwd(qf, kf, vb, dob, lse2, delta, pe_tabs, MH, T=T_ATT, deps=(swap2[3],))
    pcs2, theirs2, sums2 = swap_end("g2", g2, swap2, dkv_all)
    ssem2, rsem2, bufs2, token2 = scatter_begin("g2", sums2)
    dkpe = _kpe_sum(dkpe_h, pe_tabs, MH)
    dcqn = _mm(dq_all, wq, mode="nt", outs=[F32], name="q_bwd_x", deps=(token2,))
    g_wq = _mm(cqn, dq_all, mode="tn", outs=[BF16], name="q_bwd_w")
    dckvn = _mm(dkv_all, wkv, mode="nt", outs=[F32], name="kv_bwd_x")
    g_wkv = _mm(ckvn, dkv_all, mode="tn", outs=[BF16], name="kv_bwd_w")
    dproj, g_q_a = _rmsnorm_bwd(dcqn, proj, rstd_q, q_a_norm_g, name="norm_q_bwd", into=(dproj, off_cq // QL),
                                width=QL, col=off_cq // QL)
    dproj, g_kv_a = _rmsnorm_bwd(dckvn, proj, rstd_kv, kv_a_norm_g, name="norm_kv_bwd", into=(dproj, off_ckv // KVL),
                                 width=KVL, col=off_ckv // KVL)
    g_wa = _mm(dproj, u, mode="tn", outs=[BF16], name="in_bwd_w")
    g_wkpe = _mm(dkpe, u, mode="tn", outs=[BF16], name="kpe_bwd_w")

    there = sorted(runs)
    g_parts = [g_wa, g_wkpe]
    g_w_in = jnp.stack([jnp.concatenate(
        [p for s0, w, d0 in there for a, b in [(max(s0, jj * c_sh), min(s0 + w, (jj + 1) * c_sh))] if a < b
         for p in take(g_parts, d0 + a - s0, b - a)] + [jnp.zeros((c_pad - c_sh, D), BF16)], axis=0)
        for jj in range(N_CHIPS)])
    gq = g_wq.reshape(QL, MH, 2 * LANES)[:, :, :QK_NOPE + QK_ROPE].reshape(QL, MH * (QK_NOPE + QK_ROPE))
    g3 = ("w_in", "w_q_b", "w_kv_b")
    pcs3, theirs3, sums3 = reduce_begin("g3", g3, (g_w_in, _split_cols(gq), _split_cols(g_wkv)))
    ssem3, rsem3, bufs3, token3 = scatter_begin("g3", sums3)
    du_a = _mm(dproj, wa, mode="nn", outs=[F32], name="in_bwd_x", tk=2816, deps=(token3,))
    du = _mm(dkpe, wkpe, mode="nn", outs=[F32], name="kpe_bwd_x", epi=lambda acc, r: (acc + r,), extras=(du_a,))
    dx, g_norm_mix = _rmsnorm_bwd(du, xs, rstd0, norm_mix_g, name="norm_mix_bwd", res=dh1)

    bufs1 = _split_wait(ssem1, rsem1, bufs1, dx, _scatter_plan(len(g1)), name="scatter_g1_wait")
    bufs2 = _split_wait(ssem2, rsem2, bufs2, dx, _scatter_plan(len(g2)), name="scatter_g2_wait")
    bufs3 = _split_wait(ssem3, rsem3, bufs3, dx, _scatter_plan(len(g3)), name="scatter_g3_wait")
    recv1, recv2, recv3 = bufs1[len(g1):], bufs2[len(g2):], bufs3[len(g3):]
    halves = {}
    for names, pcs, theirs, recv in ((g1, pcs1, theirs1, recv1), (g2, pcs2, theirs2, recv2), (g3, pcs3, theirs3, recv3)):
        for k, p, t, r in zip(names, pcs, theirs, recv):
            halves[k] = _sum_chips(p, t, r, place, name="sum_chips_" + k)
    joined = _join_halves([halves[k] for k in big], name="join_halves")
    g_shard = {k: g.reshape(2 * g.shape[1], g.shape[2]) for k, g in zip(big, joined)}

    small = ("norm_mix_g", "ret_norm_g", "q_a_norm_g", "kv_a_norm_g", "norm_mlp_g", "norm_f_g")
    g_small = [g_norm_mix, g_ret_norm, g_q_a, g_kv_a, g_norm_mlp, g_norm_f]
    red = _allreduce_small(g_small, loss11)
    loss = red[0, red.shape[1] - 1]
    w_small = [norm_mix_g, ret_norm_g, q_a_norm_g, kv_a_norm_g, norm_mlp_g, norm_f_g]
    m_small = [m_norm_mix_g, m_ret_norm_g, m_q_a_norm_g, m_kv_a_norm_g, m_norm_mlp_g, m_norm_f_g]
    v_small = [v_norm_mix_g, v_ret_norm_g, v_q_a_norm_g, v_kv_a_norm_g, v_norm_mlp_g, v_norm_f_g]
    row = lambda a: a.reshape(1, -1)
    upd = _adamw_small(red, [row(a) for a in w_small], [row(a) for a in m_small], [row(a) for a in v_small])
    out_g, out_d, out_m, out_v = {}, {}, {}, {}
    for k, wv, (g_, d_, m_, v_) in zip(small, w_small, upd):
        out_g[k], out_d[k], out_m[k], out_v[k] = [a.reshape(wv.shape) for a in (g_, d_, m_, v_)]

    for k in big:
        res = _rows_call(lambda w, g, m, v: (g,) + _adamw_vals(w, g, m, v),
                         [w_sh[k], g_shard[k], m_sh[k], v_sh[k]], [F32] * 4, name="adamw_" + k)
        if k == "w_in":
            res = [r.T for r in res]
        out_g[k], out_d[k], out_m[k], out_v[k] = [r[None] for r in res]

    order = ("norm_mix_g", "w_in", "ret_norm_g", "w_ret_o", "q_a_norm_g", "w_q_b", "kv_a_norm_g", "w_kv_b",
             "w_mla_o", "w_out", "norm_mlp_g", "w_up", "w_down", "norm_f_g")
    return (loss, dx.reshape(1, S, D), *[out_g[k] for k in order], *[out_d[k] for k in order],
            *[out_m[k] for k in order], *[out_v[k] for k in order])
```

```python
import math

import jax
import jax.numpy as jnp
from jax import lax
from jax.experimental import pallas as pl
from jax.experimental.pallas import tpu as pltpu

F32 = jnp.float32
BF16 = jnp.bfloat16

EPS = 1e-6
ROPE_THETA = 10000.0
CHUNK = 64
RET_QK = 128
RET_V = 256
RET_HEAD_COLS = 2 * RET_QK + 2 * RET_V
QK_NOPE = 128
QK_ROPE = 64
V_HEAD = 128
LANES = 128
LOG2E = math.log2(math.e)

ADAM_LR = 0.001
ADAM_B1 = 0.9
ADAM_B2 = 0.999
ADAM_EPS = 1e-08
ADAM_WD = 0.01
ADAM_STEP = 10

N_CHIPS = 4
VMEM_LIMIT = 56 * 1024 * 1024
MESH = pl.DeviceIdType.MESH
NEG = -1e30


def _pallas(body, **kw):
    return pl.pallas_call(body, **kw)


def _params(sem=None):
    return pltpu.CompilerParams(dimension_semantics=sem, vmem_limit_bytes=VMEM_LIMIT)


def _tile(n, want):
    t = min(n, want)
    while n % t:
        t //= 2
    return t


_ANY = pl.BlockSpec(memory_space=pl.ANY)
TN_BF16_TK = 4096


def _mm(a, b, *, mode, outs, name, epi=None, extras=(), deps=(), out_shards=False, more_outs=None,
        tm=1024, tn=1024, tk=2048):
    shards = b.shape[0] if b.ndim == 3 else 1
    brows, bcols = b.shape[-2], b.shape[-1] * shards
    if mode == "nn":
        (M, K), N = a.shape, bcols
    elif mode == "nt":
        (M, K), N = a.shape, brows
    else:
        (K, M), N = a.shape, bcols
    if mode == "tn" and a.dtype == BF16 and b.dtype == BF16:
        tk = max(tk, TN_BF16_TK)
    tm = _tile(M, tm)
    tn = _tile(N // (shards if mode == "nn" else 1) // (N_CHIPS if out_shards else 1), tn)
    tk = _tile(K // (shards if mode == "nt" else 1), tk)
    nk = K // tk
    if mode == "nn":
        a_spec = pl.BlockSpec((tm, tk), lambda i, j, k: (i, k))
        dims = (((1,), (0,)), ((), ()))
        if shards > 1:
            per = N // shards // tn
            b_spec = pl.BlockSpec((None, tk, tn), lambda i, j, k: (j // per, k, j % per))
        else:
            b_spec = pl.BlockSpec((tk, tn), lambda i, j, k: (k, j))
    elif mode == "nt":
        a_spec = pl.BlockSpec((tm, tk), lambda i, j, k: (i, k))
        dims = (((1,), (1,)), ((), ()))
        if shards > 1:
            per = K // shards // tk
            b_spec = pl.BlockSpec((None, tn, tk), lambda i, j, k: (k // per, j, k % per))
        else:
            b_spec = pl.BlockSpec((tn, tk), lambda i, j, k: (j, k))
    else:
        assert shards == 1
        a_spec = pl.BlockSpec((tk, tm), lambda i, j, k: (k, i))
        b_spec = pl.BlockSpec((tk, tn), lambda i, j, k: (k, j))
        dims = (((0,), (0,)), ((), ()))
    if out_shards:
        assert not extras
        oper = N // N_CHIPS // tn
        o_spec = pl.BlockSpec((None, tm, tn), lambda i, j, k: (j // oper, i, j % oper))
        o_shape = (N_CHIPS, M, N // N_CHIPS)
    else:
        o_spec = pl.BlockSpec((tm, tn), lambda i, j, k: (i, j))
        o_shape = (M, N)
    more = [] if more_outs is None else more_outs(tm, tn)
    ex_arrays = [e[0] if isinstance(e, tuple) else e for e in extras]
    ex_specs = [pl.BlockSpec((tm, tn), lambda i, j, k, off=e[1] // tn: (i, off + j)) if isinstance(e, tuple)
                else o_spec for e in extras]
    n_ex, n_out, n_dep = len(extras), len(outs) + len(more), len(deps)
    if epi is None:
        epi = lambda acc: (acc,)

    def body(*refs):
        a_ref, b_ref = refs[0], refs[1]
        ex_refs = refs[2:2 + n_ex]
        o_refs = refs[2 + n_ex + n_dep:2 + n_ex + n_dep + n_out]
        part = lax.dot_general(a_ref[...].astype(BF16), b_ref[...].astype(BF16), dims,
                               preferred_element_type=F32)

        def finish(acc):
            vals = epi(acc, *[r[...] for r in ex_refs])
            for r, v in zip(o_refs, vals):
                if isinstance(v, (list, tuple)):
                    for lead, piece in enumerate(v):
                        r[lead] = piece.astype(r.dtype)
                else:
                    r[...] = v.astype(r.dtype)

        if nk == 1:
            finish(part)
        else:
            acc_ref = refs[-1]
            k = pl.program_id(2)

            @pl.when(k == 0)
            def _():
                acc_ref[...] = part

            @pl.when(k > 0)
            def _():
                acc_ref[...] += part

            @pl.when(k == nk - 1)
            def _():
                finish(acc_ref[...])

    res = _pallas(
        body, name=name, grid=(M // tm, N // tn, nk),
        in_specs=[a_spec, b_spec] + ex_specs + [_ANY] * n_dep,
        out_specs=[o_spec] * len(outs) + [spec for _, spec in more],
        out_shape=[jax.ShapeDtypeStruct(o_shape, d) for d in outs] + [shape for shape, _ in more],
        scratch_shapes=[pltpu.VMEM((tm, tn), F32)] if nk > 1 else [],
        compiler_params=_params(("parallel", "parallel", "arbitrary")),
    )(a, b, *ex_arrays, *deps)
    return res[0] if n_out == 1 else res


def _rmsnorm_fwd(x, g, *, name, width=None, col=0, tr=256):
    S = x.shape[0]
    W = x.shape[1] if width is None else width
    tr = _tile(S, tr)

    def body(x_ref, g_ref, y_ref, r_ref):
        xv = x_ref[...]
        rstd = lax.rsqrt(jnp.mean(xv * xv, axis=-1, keepdims=True) + EPS)
        y_ref[...] = (xv * rstd * g_ref[...]).astype(BF16)
        r_ref[...] = rstd

    return _pallas(
        body, name=name, grid=(S // tr,),
        in_specs=[pl.BlockSpec((tr, W), lambda i: (i, col)), pl.BlockSpec((1, W), lambda i: (0, 0))],
        out_specs=[pl.BlockSpec((tr, W), lambda i: (i, 0)), pl.BlockSpec((tr, 1), lambda i: (i, 0))],
        out_shape=[jax.ShapeDtypeStruct((S, W), BF16), jax.ShapeDtypeStruct((S, 1), F32)],
        compiler_params=_params(("parallel",)),
    )(x, g)


def _rmsnorm_bwd(dy, x, rstd, g, *, name, res=None, into=None, deps=(), bf16_copy=0, width=None, col=0, tr=256):
    S = x.shape[0]
    W = x.shape[1] if width is None else width
    tr = _tile(S, tr)
    has_res = res is not None

    def body(*refs):
        dy_ref, x_ref, r_ref, g_ref = refs[:4]
        dx_ref, dg_ref = refs[-2 - bf16_copy], refs[-1 - bf16_copy]
        rstd_v = r_ref[...]
        xhat = x_ref[...] * rstd_v
        dyv = dy_ref[...].astype(F32)
        dyg = dyv * g_ref[...]
        dx = rstd_v * (dyg - xhat * jnp.mean(dyg * xhat, axis=-1, keepdims=True))
        if has_res:
            dx = dx + refs[4][...]
        dx_ref[...] = dx.astype(dx_ref.dtype)
        if bf16_copy:
            refs[-1][...] = dx.astype(BF16)
        part = jnp.sum(dyv * xhat, axis=0, keepdims=True)

        @pl.when(pl.program_id(0) == 0)
        def _():
            dg_ref[...] = part

        @pl.when(pl.program_id(0) > 0)
        def _():
            dg_ref[...] += part

    row = pl.BlockSpec((tr, W), lambda i: (i, 0))
    ins = [dy, x, rstd, g] + ([res] if has_res else [])
    in_specs = [row, pl.BlockSpec((tr, W), lambda i: (i, col)), pl.BlockSpec((tr, 1), lambda i: (i, 0)),
                pl.BlockSpec((1, W), lambda i: (0, 0))] + ([row] if has_res else [])
    if into is None:
        dx_spec, dx_shape, alias = row, jax.ShapeDtypeStruct((S, W), F32), {}
    else:
        buf, col_out = into
        ins.append(buf)
        in_specs.append(_ANY)
        dx_spec = pl.BlockSpec((tr, W), lambda i: (i, col_out))
        dx_shape = jax.ShapeDtypeStruct(buf.shape, buf.dtype)
        alias = {len(ins) - 1: 0}
    ins += list(deps)
    in_specs += [_ANY] * len(deps)
    return _pallas(
        body, name=name, grid=(S // tr,), in_specs=in_specs,
        out_specs=[dx_spec, pl.BlockSpec((1, W), lambda i: (0, 0))] + [row] * bf16_copy,
        out_shape=[dx_shape, jax.ShapeDtypeStruct((1, W), F32)] + [jax.ShapeDtypeStruct((S, W), BF16)] * bf16_copy,
        input_output_aliases=alias,
        compiler_params=_params(("arbitrary",)),
    )(*ins)


def _final_loss(h2, g, target, *, tr=256):
    S, D = h2.shape
    tr = _tile(S, tr)

    def body(h_ref, g_ref, t_ref, loss_ref, dh_ref, dhb_ref, dg_ref):
        hv = h_ref[...]
        rstd = lax.rsqrt(jnp.mean(hv * hv, axis=-1, keepdims=True) + EPS)
        xhat = hv * rstd
        e = xhat * g_ref[...] - t_ref[...]
        lpart = (0.5 / D) * jnp.sum(jnp.sum(e * e, axis=-1, keepdims=True), axis=0, keepdims=True)
        dy = e * (1.0 / D)
        dyg = dy * g_ref[...]
        dh = rstd * (dyg - xhat * jnp.mean(dyg * xhat, axis=-1, keepdims=True))
        dh_ref[...] = dh
        dhb_ref[...] = dh.astype(BF16)
        gpart = jnp.sum(dy * xhat, axis=0, keepdims=True)

        @pl.when(pl.program_id(0) == 0)
        def _():
            loss_ref[...] = lpart
            dg_ref[...] = gpart

        @pl.when(pl.program_id(0) > 0)
        def _():
            loss_ref[...] += lpart
            dg_ref[...] += gpart

    row = pl.BlockSpec((tr, D), lambda i: (i, 0))
    vec = pl.BlockSpec((1, D), lambda i: (0, 0))
    return _pallas(
        body, name="final_loss", grid=(S // tr,), in_specs=[row, vec, row],
        out_specs=[pl.BlockSpec((1, 1), lambda i: (0, 0)), row, row, vec],
        out_shape=[jax.ShapeDtypeStruct((1, 1), F32), jax.ShapeDtypeStruct((S, D), F32),
                   jax.ShapeDtypeStruct((S, D), BF16), jax.ShapeDtypeStruct((1, D), F32)],
        compiler_params=_params(("arbitrary",)),
    )(h2, g, target)


def _sigmoid(v):
    return 1.0 / (1.0 + jnp.exp(-v))


def _merge_bwd(dmerged, proj, y_ret, y_mla, D, off_gret, *, tr=256):
    S = y_ret.shape[0]
    tr = _tile(S, tr)
    b0 = off_gret // D

    def body(dm_ref, g_ref, yr_ref, ym_ref, dp_ref, dyr_ref, dym_ref):
        dm = dm_ref[...]
        sg = _sigmoid(g_ref[...])

        @pl.when(pl.program_id(1) == 0)
        def _():
            dyr_ref[...] = (dm * sg).astype(BF16)
            dp_ref[...] = (dm * yr_ref[...] * sg * (1.0 - sg)).astype(BF16)

        @pl.when(pl.program_id(1) == 1)
        def _():
            dym_ref[...] = (dm * sg).astype(BF16)
            dp_ref[...] = (dm * ym_ref[...] * sg * (1.0 - sg)).astype(BF16)

    blk = pl.BlockSpec((tr, D), lambda i, j: (i, 0))
    return _pallas(
        body, name="merge_bwd", grid=(S // tr, 2),
        in_specs=[blk, pl.BlockSpec((tr, D), lambda i, j: (i, b0 + j)), blk, blk],
        out_specs=[pl.BlockSpec((tr, D), lambda i, j: (i, b0 + j)), blk, blk],
        out_shape=[jax.ShapeDtypeStruct(proj.shape, BF16), jax.ShapeDtypeStruct((S, D), BF16),
                   jax.ShapeDtypeStruct((S, D), BF16)],
        compiler_params=_params(("parallel", "arbitrary")),
    )(dmerged, proj, y_ret, y_mla)


def _rope128(t, cos_full, sin_signed):
    return t * cos_full + pltpu.roll(t, RET_QK // 2, 1) * sin_signed


def _rope128_t(d, cos_full, sin_signed):
    return d * cos_full + pltpu.roll(d * sin_signed, RET_QK // 2, 1)


def _ret_consts(lg, T):
    pos = lax.broadcasted_iota(jnp.int32, (T, 1), 0).astype(F32)
    qd = jnp.exp(lg * (pos + 1.0))
    kd = jnp.exp(lg * (T - 1.0 - pos))
    n = lax.broadcasted_iota(jnp.int32, (T, T), 0)
    m = lax.broadcasted_iota(jnp.int32, (T, T), 1)
    vis = (m // CHUNK) <= (n // CHUNK)
    dist = jnp.abs(n - m).astype(F32)
    decay = jnp.where(vis, jnp.exp(lg * dist), 0.0)
    cdec = jnp.exp(lg * float(T))
    return qd, kd, decay, cdec


def _dot(a, b, dims):
    return lax.dot_general(a.astype(BF16), b.astype(BF16), (dims, ((), ())), preferred_element_type=F32)


NN = ((1,), (0,))
NT = ((1,), (1,))
TN = ((0,), (0,))
_RQ = slice(0, RET_QK)
_RK = slice(RET_QK, 2 * RET_QK)
_RV = slice(2 * RET_QK, 2 * RET_QK + RET_V)
_RG = slice(2 * RET_QK + RET_V, RET_HEAD_COLS)


RET_GROUP = 8


def _head_cols(h, part):
    return slice(h * RET_HEAD_COLS + part.start, h * RET_HEAD_COLS + part.stop)


def _ret_fwd(proj, cosr, sinr, lgam, gain, RH, *, T):
    S = proj.shape[0]
    nb = S // T
    G = _tile(RH, RET_GROUP)
    heads = range(G)
    scale = RET_QK ** -0.5

    def body(p_ref, cos_ref, sin_ref, lg_ref, gain_ref, ry_ref, gated_ref, st_ref, state):
        b = pl.program_id(1)

        @pl.when(b == 0)
        def _():
            state[...] = jnp.zeros_like(state)

        consts = [_ret_consts(lg_ref[h, 0:1, 0:1], T) for h in heads]
        cosv, sinv = cos_ref[...], sin_ref[...]
        q = [_rope128(p_ref[:, _head_cols(h, _RQ)], cosv, sinv) for h in heads]
        k = [_rope128(p_ref[:, _head_cols(h, _RK)], cosv, sinv) * scale for h in heads]
        v = [p_ref[:, _head_cols(h, _RV)] for h in heads]
        sprev = [state[h] for h in heads]
        for h in heads:
            st_ref[h] = sprev[h]
        a = [_dot(q[h], k[h], NT) for h in heads]
        qs = [_dot(q[h] * consts[h][0], sprev[h], NN) for h in heads]
        kv = [_dot(k[h] * consts[h][1], v[h], TN) for h in heads]
        o = [_dot(a[h] * consts[h][2], v[h], NN) + qs[h] for h in heads]
        for h in heads:
            state[h] = sprev[h] * consts[h][3] + kv[h]
            vals = slice(h * RET_V, (h + 1) * RET_V)
            ry_ref[:, vals] = o[h]
            mu = jnp.mean(o[h], axis=-1, keepdims=True)
            oc = o[h] - mu
            var = jnp.mean(oc * oc, axis=-1, keepdims=True)
            t = oc * lax.rsqrt(var + EPS) * gain_ref[:, vals]
            gv = p_ref[:, _head_cols(h, _RG)]
            gated_ref[:, vals] = (t * (gv * _sigmoid(gv))).astype(BF16)

    return _pallas(
        body, name="ret_fwd", grid=(RH // G, nb),
        in_specs=[pl.BlockSpec((T, G * RET_HEAD_COLS), lambda h, b: (b, h)),
                  pl.BlockSpec((T, RET_QK), lambda h, b: (b, 0)),
                  pl.BlockSpec((T, RET_QK), lambda h, b: (b, 0)),
                  pl.BlockSpec((G, 8, LANES), lambda h, b: (h, 0, 0)),
                  pl.BlockSpec((1, G * RET_V), lambda h, b: (0, h))],
        out_specs=[pl.BlockSpec((T, G * RET_V), lambda h, b: (b, h)),
                   pl.BlockSpec((T, G * RET_V), lambda h, b: (b, h)),
                   pl.BlockSpec((G, None, RET_QK, RET_V), lambda h, b: (h, b, 0, 0))],
        out_shape=[jax.ShapeDtypeStruct((S, RH * RET_V), F32), jax.ShapeDtypeStruct((S, RH * RET_V), BF16),
                   jax.ShapeDtypeStruct((RH, nb, RET_QK, RET_V), F32)],
        scratch_shapes=[pltpu.VMEM((G, RET_QK, RET_V), F32)],
        compiler_params=_params(("parallel", "arbitrary")),
    )(proj, cosr, sinr, lgam, gain)


def _ret_bwd(proj, cosr, sinr, lgam, gain, ry, dgated, states, dproj, RH, *, T):
    S = proj.shape[0]
    nb = S // T
    G = _tile(RH, RET_GROUP)
    heads = range(G)
    scale = RET_QK ** -0.5

    def body(p_ref, cos_ref, sin_ref, lg_ref, gain_ref, ry_ref, dg_ref, st_ref, _, dp_ref, dgain_ref, dstate):
        b = pl.program_id(1)

        @pl.when(b == 0)
        def _():
            dstate[...] = jnp.zeros_like(dstate)

        consts = [_ret_consts(lg_ref[h, 0:1, 0:1], T) for h in heads]
        qd, kd, decay, cdec = [[c[i] for c in consts] for i in range(4)]
        cosv, sinv = cos_ref[...], sin_ref[...]
        q = [_rope128(p_ref[:, _head_cols(h, _RQ)], cosv, sinv) for h in heads]
        k = [_rope128(p_ref[:, _head_cols(h, _RK)], cosv, sinv) * scale for h in heads]
        v = [p_ref[:, _head_cols(h, _RV)] for h in heads]
        sprev = [st_ref[h] for h in heads]
        ds_new = [dstate[h] for h in heads]
        a = [_dot(q[h], k[h], NT) for h in heads]
        do, gparts = [], []
        for h in heads:
            vals = slice(h * RET_V, (h + 1) * RET_V)
            o = ry_ref[:, vals]
            mu = jnp.mean(o, axis=-1, keepdims=True)
            oc = o - mu
            rstd = lax.rsqrt(jnp.mean(oc * oc, axis=-1, keepdims=True) + EPS)
            ryn = oc * rstd
            gainv = gain_ref[:, vals]
            gv = p_ref[:, _head_cols(h, _RG)]
            sg = _sigmoid(gv)
            dgt = dg_ref[:, vals]
            dt = dgt * (gv * sg)
            dp_ref[:, _head_cols(h, _RG)] = (dgt * (ryn * gainv) * (sg * (1.0 + gv * (1.0 - sg)))).astype(BF16)
            gparts.append(jnp.sum(dt * ryn, axis=0, keepdims=True))
            dryn = dt * gainv
            do.append(rstd * (dryn - jnp.mean(dryn, axis=-1, keepdims=True)
                              - ryn * jnp.mean(dryn * ryn, axis=-1, keepdims=True)))
        gpart = jnp.concatenate(gparts, axis=1)

        @pl.when(b == 0)
        def _():
            dgain_ref[...] = gpart

        @pl.when(b > 0)
        def _():
            dgain_ref[...] += gpart

        dpm = [_dot(do[h], v[h], NT) for h in heads]
        dq_s = [_dot(do[h], sprev[h], NT) for h in heads]
        dk_s = [_dot(v[h], ds_new[h], NT) for h in heads]
        dv_s = [_dot(k[h] * kd[h], ds_new[h], NN) for h in heads]
        dst = [_dot(q[h] * qd[h], do[h], TN) for h in heads]
        a = [a[h] * decay[h] for h in heads]
        dpm = [dpm[h] * decay[h] for h in heads]
        dv = [_dot(a[h], do[h], TN) + dv_s[h] for h in heads]
        dq = [_dot(dpm[h], k[h], NN) + dq_s[h] * qd[h] for h in heads]
        dk = [(_dot(dpm[h], q[h], TN) + dk_s[h] * kd[h]) * scale for h in heads]
        for h in heads:
            dstate[h] = ds_new[h] * cdec[h] + dst[h]
            dp_ref[:, _head_cols(h, _RV)] = dv[h].astype(BF16)
            dp_ref[:, _head_cols(h, _RQ)] = _rope128_t(dq[h], cosv, sinv).astype(BF16)
            dp_ref[:, _head_cols(h, _RK)] = _rope128_t(dk[h], cosv, sinv).astype(BF16)

    rb = lambda b: nb - 1 - b
    return _pallas(
        body, name="ret_bwd", grid=(RH // G, nb),
        in_specs=[pl.BlockSpec((T, G * RET_HEAD_COLS), lambda h, b: (rb(b), h)),
                  pl.BlockSpec((T, RET_QK), lambda h, b: (rb(b), 0)),
                  pl.BlockSpec((T, RET_QK), lambda h, b: (rb(b), 0)),
                  pl.BlockSpec((G, 8, LANES), lambda h, b: (h, 0, 0)),
                  pl.BlockSpec((1, G * RET_V), lambda h, b: (0, h)),
                  pl.BlockSpec((T, G * RET_V), lambda h, b: (rb(b), h)),
                  pl.BlockSpec((T, G * RET_V), lambda h, b: (rb(b), h)),
                  pl.BlockSpec((G, None, RET_QK, RET_V), lambda h, b: (h, rb(b), 0, 0)),
                  _ANY],
        out_specs=[pl.BlockSpec((T, G * RET_HEAD_COLS), lambda h, b: (rb(b), h)),
                   pl.BlockSpec((1, G * RET_V), lambda h, b: (0, h))],
        out_shape=[jax.ShapeDtypeStruct(dproj.shape, dproj.dtype), jax.ShapeDtypeStruct((1, RH * RET_V), F32)],
        scratch_shapes=[pltpu.VMEM((G, RET_QK, RET_V), F32)],
        input_output_aliases={8: 0},
        compiler_params=_params(("parallel", "arbitrary")),
    )(proj, cosr, sinr, lgam, gain, ry, dgated, states, dproj)


def _rope_pe(t, c, s1, s2):
    return t * c + pltpu.roll(t, LANES - QK_ROPE // 2, 1) * s1 + pltpu.roll(t, QK_ROPE // 2, 1) * s2


def _rope_pe_t(d, c, s1, s2):
    return d * c + pltpu.roll(d * s1, QK_ROPE // 2, 1) + pltpu.roll(d * s2, LANES - QK_ROPE // 2, 1)


ATTN_C2 = (QK_NOPE + QK_ROPE) ** -0.5 * LOG2E


def _qkv_proj(cqn, ckvn, wq, wkv, kpe, tabs, MH, *, tm=512, heads=4):
    S = cqn.shape[0]
    tm = _tile(S, tm)
    hb = _tile(MH, heads)
    W = 2 * LANES
    c_t, s1_t, s2_t = tabs

    def body(cq_ref, ckv_ref, wq_ref, wkv_ref, kpe_ref, c_ref, s1_ref, s2_ref, qf_ref, kf_ref, v_ref):
        c, s1, s2 = c_ref[...], s1_ref[...], s2_ref[...]
        q = _dot(cq_ref[...], wq_ref[...], NN)
        kv = _dot(ckv_ref[...], wkv_ref[...], NN)
        kper = _rope_pe(kpe_ref[...], c, s1, s2).astype(BF16)
        for h in range(hb):
            lo, mid, hi = h * W, h * W + QK_NOPE, (h + 1) * W
            qf_ref[:, lo:mid] = (q[:, lo:mid] * ATTN_C2).astype(BF16)
            qf_ref[:, mid:hi] = (_rope_pe(q[:, mid:hi], c, s1, s2) * ATTN_C2).astype(BF16)
            kf_ref[:, lo:mid] = kv[:, lo:mid].astype(BF16)
            kf_ref[:, mid:hi] = kper
            v_ref[:, h * V_HEAD:(h + 1) * V_HEAD] = kv[:, mid:hi].astype(BF16)

    tab = pl.BlockSpec((tm, LANES), lambda i, j: (i, 0))
    grp = pl.BlockSpec((tm, hb * W), lambda i, j: (i, j))
    return _pallas(
        body, name="qkv_proj", grid=(S // tm, MH // hb),
        in_specs=[pl.BlockSpec((tm, cqn.shape[1]), lambda i, j: (i, 0)),
                  pl.BlockSpec((tm, ckvn.shape[1]), lambda i, j: (i, 0)),
                  pl.BlockSpec((wq.shape[0], hb * W), lambda i, j: (0, j)),
                  pl.BlockSpec((wkv.shape[0], hb * W), lambda i, j: (0, j)), tab, tab, tab, tab],
        out_specs=[grp, grp, pl.BlockSpec((tm, hb * V_HEAD), lambda i, j: (i, j))],
        out_shape=[jax.ShapeDtypeStruct((S, MH * W), BF16)] * 2 + [jax.ShapeDtypeStruct((S, MH * V_HEAD), BF16)],
        compiler_params=_params(("parallel", "parallel")),
    )(cqn, ckvn, wq, wkv, kpe, c_t, s1_t, s2_t)


def _chunk_mask(T):
    n = lax.broadcasted_iota(jnp.int32, (T, T), 0)
    m = lax.broadcasted_iota(jnp.int32, (T, T), 1)
    return (m // CHUNK) <= (n // CHUNK)


def _lanes_to(v, width):
    return jnp.tile(v, (1, width // LANES))


def _attn_fwd(qf, kf, vb, MH, *, T):
    S = qf.shape[0]
    nt = S // T

    def body(q_ref, k_ref, v_ref, o_ref, ob_ref, lse_ref, m_sc, l_sc, acc_sc, s_a, s_b):
        qi = pl.program_id(1)
        m_sc[...] = jnp.full_like(m_sc, NEG)
        l_sc[...] = jnp.zeros_like(l_sc)
        acc_sc[...] = jnp.zeros_like(acc_sc)

        def rows_of(kt):
            return pl.ds(pl.multiple_of(kt * T, T), T)

        def scores(kt):
            return _dot(q_ref[...], k_ref[rows_of(kt), :], NT)

        def update(s, kt):
            m_prev = m_sc[...]
            m_new = jnp.maximum(m_prev, jnp.max(s, axis=-1, keepdims=True))
            alpha = jnp.exp2(m_prev - m_new)
            p = jnp.exp2(s - _lanes_to(m_new, T))
            l_sc[...] = alpha * l_sc[...] + jnp.sum(p, axis=-1, keepdims=True)
            acc_sc[...] = alpha * acc_sc[...] + _dot(p, v_ref[rows_of(kt), :], NN)
            m_sc[...] = m_new

        def masked(s):
            return jnp.where(_chunk_mask(T), s, NEG)

        @pl.when(qi == 0)
        def _():
            update(masked(scores(0)), 0)

        @pl.when(qi > 0)
        def _():
            s_a[...] = masked(scores(qi))
            s_b[...] = scores(0)
            update(s_a[...], qi)
            s_a[...] = scores(jnp.minimum(1, qi - 1))
            update(s_b[...], 0)

            def pair(j, carry):
                s_b[...] = scores(2 * j)
                update(s_a[...], 2 * j - 1)
                s_a[...] = scores(jnp.minimum(2 * j + 1, qi - 1))
                update(s_b[...], 2 * j)
                return carry

            lax.fori_loop(1, (qi + 1) // 2, pair, 0)

            @pl.when(qi % 2 == 0)
            def _():
                update(s_a[...], qi - 1)
        l = l_sc[...]
        o = acc_sc[...] / l
        o_ref[...] = o
        ob_ref[...] = o.astype(BF16)
        lse_ref[...] = m_sc[...] + jnp.log(l) * LOG2E

    return _pallas(
        body, name="attn_fwd", grid=(MH, nt),
        in_specs=[pl.BlockSpec((T, 2 * LANES), lambda h, i: (i, h)),
                  pl.BlockSpec((S, 2 * LANES), lambda h, i: (0, h)),
                  pl.BlockSpec((S, LANES), lambda h, i: (0, h))],
        out_specs=[pl.BlockSpec((T, LANES), lambda h, i: (i, h)), pl.BlockSpec((T, LANES), lambda h, i: (i, h)),
                   pl.BlockSpec((None, T, LANES), lambda h, i: (h, i, 0))],
        out_shape=[jax.ShapeDtypeStruct((S, MH * LANES), F32), jax.ShapeDtypeStruct((S, MH * LANES), BF16),
                   jax.ShapeDtypeStruct((MH, S, LANES), F32)],
        scratch_shapes=[pltpu.VMEM((T, LANES), F32), pltpu.VMEM((T, LANES), F32), pltpu.VMEM((T, LANES), F32),
                        pltpu.VMEM((T, T), F32), pltpu.VMEM((T, T), F32)],
        compiler_params=_params(("parallel", "parallel")),
    )(qf, kf, vb)


def _attn_bwd(qf, kf, vb, dob, lse2, delta, tabs, MH, *, T, deps=()):
    S = qf.shape[0]
    nt = S // T
    scale = (QK_NOPE + QK_ROPE) ** -0.5
    n_dep = len(deps)

    def body(q_ref, k_ref, v_ref, do_ref, lse_ref, dl_ref, c_ref, s1_ref, s2_ref, *rest):
        dqa_ref, dkv_ref, dkpe_ref, dq_ref, dk_sc, dv_sc, s_a, dp_a, s_b, dp_b = rest[n_dep:]
        kj = pl.program_id(1)

        @pl.when(kj == 0)
        def _():
            dq_ref[...] = jnp.zeros_like(dq_ref)

        dk_sc[...] = jnp.zeros_like(dk_sc)
        dv_sc[...] = jnp.zeros_like(dv_sc)

        def rows_of(qt):
            return pl.ds(pl.multiple_of(qt * T, T), T)

        def products(qt):
            rows = rows_of(qt)
            return _dot(q_ref[rows, :], k_ref[...], NT), _dot(do_ref[rows, :], v_ref[...], NT)

        def update(s, dp, qt):
            rows = rows_of(qt)
            q, dov = q_ref[rows, :], do_ref[rows, :]
            p = jnp.exp2(s - _lanes_to(lse_ref[rows, :], T))
            ds = p * (dp - _lanes_to(dl_ref[rows, :], T))
            dv_sc[...] += _dot(p, dov, TN)
            dk_sc[...] += _dot(ds, q, TN)
            dq_ref[rows, :] += _dot(ds, k_ref[...], NN)

        def masked(s):
            return jnp.where(_chunk_mask(T), s, NEG)

        @pl.when(kj == nt - 1)
        def _():
            s, dp = products(kj)
            update(masked(s), dp, kj)

        @pl.when(kj < nt - 1)
        def _():
            s, dp = products(kj)
            s_a[...], dp_a[...] = masked(s), dp
            s_b[...], dp_b[...] = products(kj + 1)
            update(s_a[...], dp_a[...], kj)
            s_a[...], dp_a[...] = products(jnp.minimum(kj + 2, nt - 1))
            update(s_b[...], dp_b[...], kj + 1)

            def pair(j, carry):
                t0 = kj + 2 * j
                s_b[...], dp_b[...] = products(t0 + 1)
                update(s_a[...], dp_a[...], t0)
                s_a[...], dp_a[...] = products(jnp.minimum(t0 + 2, nt - 1))
                update(s_b[...], dp_b[...], t0 + 1)
                return carry

            lax.fori_loop(1, (nt - kj) // 2, pair, 0)

            @pl.when((nt - kj) % 2 == 1)
            def _():
                update(s_a[...], dp_a[...], nt - 1)
        dkv_ref[:, :QK_NOPE] = (dk_sc[:, :QK_NOPE] * (1.0 / LOG2E)).astype(BF16)
        dkv_ref[:, QK_NOPE:] = dv_sc[...].astype(BF16)
        dkpe_ref[...] = dk_sc[:, QK_NOPE:] * (1.0 / LOG2E)

        @pl.when(kj == nt - 1)
        def _():
            dqa_ref[:, :QK_NOPE] = (dq_ref[:, :QK_NOPE] * scale).astype(BF16)
            dqa_ref[:, QK_NOPE:] = (_rope_pe_t(dq_ref[:, QK_NOPE:], c_ref[...], s1_ref[...], s2_ref[...])
                                    * scale).astype(BF16)

    stat = pl.BlockSpec((None, S, LANES), lambda h, j: (h, 0, 0))
    tab = pl.BlockSpec((S, LANES), lambda h, j: (0, 0))
    return _pallas(
        body, name="attn_bwd", grid=(MH, nt),
        in_specs=[pl.BlockSpec((S, 2 * LANES), lambda h, j: (0, h)),
                  pl.BlockSpec((T, 2 * LANES), lambda h, j: (j, h)),
                  pl.BlockSpec((T, LANES), lambda h, j: (j, h)),
                  pl.BlockSpec((S, LANES), lambda h, j: (0, h)), stat, stat, tab, tab, tab] + [_ANY] * n_dep,
        out_specs=[pl.BlockSpec((S, 2 * LANES), lambda h, j: (0, h)),
                   pl.BlockSpec((T, 2 * LANES), lambda h, j: (j, h)),
                   pl.BlockSpec((T, LANES), lambda h, j: (j, h))],
        out_shape=[jax.ShapeDtypeStruct((S, MH * 2 * LANES), BF16), jax.ShapeDtypeStruct((S, MH * 2 * LANES), BF16),
                   jax.ShapeDtypeStruct((S, MH * LANES), F32)],
        scratch_shapes=[pltpu.VMEM((S, 2 * LANES), F32), pltpu.VMEM((T, 2 * LANES), F32), pltpu.VMEM((T, LANES), F32)]
        + [pltpu.VMEM((T, T), F32)] * 4,
        compiler_params=_params(("parallel", "arbitrary")),
    )(qf, kf, vb, dob, lse2, delta, *tabs, *deps)


def _kpe_sum(dkpe_h, tabs, MH, *, tr=256):
    S = dkpe_h.shape[0]
    tr = _tile(S, tr)

    def body(dk_ref, c_ref, s1_ref, s2_ref, dkpe_ref):
        tot = dk_ref[:, :LANES]
        for h in range(1, MH):
            tot = tot + dk_ref[:, h * LANES:(h + 1) * LANES]
        dkpe_ref[...] = _rope_pe_t(tot, c_ref[...], s1_ref[...], s2_ref[...]).astype(BF16)

    tab = pl.BlockSpec((tr, LANES), lambda i: (i, 0))
    return _pallas(
        body, name="kpe_sum", grid=(S // tr,),
        in_specs=[pl.BlockSpec((tr, MH * LANES), lambda i: (i, 0)), tab, tab, tab],
        out_specs=tab, out_shape=jax.ShapeDtypeStruct((S, LANES), BF16),
        compiler_params=_params(("parallel",)),
    )(dkpe_h, *tabs)


ROW_ALIGN = 16


def _blk(R, C, block_bytes=2 << 20):
    cap = max(ROW_ALIGN, block_bytes // (C * 4))
    for t in range(min(R, cap) // ROW_ALIGN * ROW_ALIGN, LANES - 1, -ROW_ALIGN):
        if R % t == 0:
            return t, C
    if R <= cap:
        return R, C
    tc = C
    while R * tc * 4 > block_bytes and tc % (2 * LANES) == 0:
        tc //= 2
    return R, tc


def _rows_call(fn, ins, out_dtypes, *, name):
    R, C = ins[0].shape
    tr, tc = _blk(R, C)
    n_in = len(ins)

    def body(*refs):
        vals = fn(*[r[...] for r in refs[:n_in]])
        for r, v in zip(refs[n_in:], vals):
            r[...] = v.astype(r.dtype)

    blk = pl.BlockSpec((tr, tc), lambda i, j: (i, j))
    res = _pallas(
        body, name=name, grid=(R // tr, C // tc), in_specs=[blk] * n_in, out_specs=[blk] * len(out_dtypes),
        out_shape=[jax.ShapeDtypeStruct((R, C), d) for d in out_dtypes],
        compiler_params=_params(("parallel", "parallel")),
    )(*ins)
    return res


def _adamw_vals(w, g, m, v):
    m = ADAM_B1 * m + (1.0 - ADAM_B1) * g
    v = ADAM_B2 * v + (1.0 - ADAM_B2) * (g * g)
    m_hat = m / (1.0 - ADAM_B1 ** ADAM_STEP)
    v_hat = v / (1.0 - ADAM_B2 ** ADAM_STEP)
    delta = -ADAM_LR * (m_hat / (jnp.sqrt(v_hat) + ADAM_EPS) + ADAM_WD * w)
    return delta, m, v


def _sum_pair(p, theirs, place, *, name):
    _, R, C = p.shape
    R2 = R // 2
    tr, tc = _blk(R2, C)
    p4 = p.reshape(N_CHIPS, 2, R2, C)

    def body(place_ref, a_ref, b_ref, o_ref):
        o_ref[...] = (a_ref[...].astype(F32) + b_ref[...].astype(F32)).astype(BF16)

    spec = pltpu.PrefetchScalarGridSpec(
        num_scalar_prefetch=1, grid=(N_CHIPS, R2 // tr, C // tc),
        in_specs=[pl.BlockSpec((None, None, tr, tc), lambda q, i, j, pr: (q, pr[0], i, j)),
                  pl.BlockSpec((None, tr, tc), lambda q, i, j, pr: (q, i, j))],
        out_specs=pl.BlockSpec((None, tr, tc), lambda q, i, j, pr: (q, i, j)))
    return _pallas(body, name=name, grid_spec=spec, out_shape=jax.ShapeDtypeStruct((N_CHIPS, R2, C), BF16),
                   compiler_params=_params(("parallel", "parallel", "parallel")))(place, p4, theirs)


def _sum_chips(p, theirs, recv, place, *, name):
    _, R, C = p.shape
    R2 = R // 2
    tr, tc = _blk(R2, C)
    p4 = p.reshape(N_CHIPS, 2, R2, C)

    def body(place_ref, a_ref, b_ref, r0_ref, r1_ref, r2_ref, o_ref):
        own = a_ref[...].astype(F32) + b_ref[...].astype(F32)
        o_ref[...] = ((own + r0_ref[...].astype(F32)) + r1_ref[...].astype(F32)) + r2_ref[...].astype(F32)

    def slot(k):
        return pl.BlockSpec((None, tr, tc), lambda i, j, pr: (k, i, j))

    spec = pltpu.PrefetchScalarGridSpec(
        num_scalar_prefetch=1, grid=(R2 // tr, C // tc),
        in_specs=[pl.BlockSpec((None, None, tr, tc), lambda i, j, pr: (pr[1], pr[0], i, j)),
                  pl.BlockSpec((None, tr, tc), lambda i, j, pr: (pr[1], i, j)), slot(0), slot(1), slot(2)],
        out_specs=pl.BlockSpec((None, tr, tc), lambda i, j, pr: (pr[0], i, j)))
    return _pallas(body, name=name, grid_spec=spec, out_shape=jax.ShapeDtypeStruct((2, R2, C), F32),
                   compiler_params=_params(("parallel", "parallel")))(place, p4, theirs, recv, recv, recv)


def _me():
    return lax.axis_index("x"), lax.axis_index("y"), lax.axis_index("c")


def _other_chips(x, y):
    return [(1 - x, y), (x, 1 - y), (1 - x, 1 - y)]


def _rcopy(src, dst, ssem, rsem, dev):
    return pltpu.make_async_remote_copy(src_ref=src, dst_ref=dst, send_sem=ssem, recv_sem=rsem,
                                        device_id=dev, device_id_type=MESH)


def _cast_into_slot(w, place, *, name, rows=None, deps=()):
    R, C = w.shape
    rows = R if rows is None else rows
    tr, tc = _blk(R, C)

    def body(place_ref, w_ref, *rest):
        rest[-1][...] = w_ref[...].astype(BF16)

    spec = pltpu.PrefetchScalarGridSpec(
        num_scalar_prefetch=1, grid=(R // tr, C // tc),
        in_specs=[pl.BlockSpec((tr, tc), lambda i, j, pr: (i, j))] + [_ANY] * len(deps),
        out_specs=pl.BlockSpec((None, tr, tc), lambda i, j, pr: (pr[1], i, j)))
    out = _pallas(body, name=name, grid_spec=spec, out_shape=jax.ShapeDtypeStruct((N_CHIPS, rows, C), BF16),
                  compiler_params=_params(("parallel", "parallel")))(place, w, *deps)
    return out.reshape(N_CHIPS, 2, rows // 2, C)


def _gather_ici_plan(bufs):
    x, y, c = _me()
    j = 2 * x + y
    plan = []
    for i, buf in enumerate(bufs):
        for k, (px, py) in enumerate(_other_chips(x, y)):
            plan.append((3 * i + k, buf.at[j, c], buf.at[j, c], (px, py, c)))
    return plan


def _forward_halves(bufs, *, name):
    n = len(bufs)

    def body(*refs):
        outs = refs[n:2 * n]
        ssem, rsem = refs[2 * n:]
        x, y, c = _me()
        sib = (x, y, 1 - c)
        cps = []
        for i in range(n):
            for k, (px, py) in enumerate(_other_chips(x, y)):
                slot = outs[i].at[2 * px + py, c]
                r = _rcopy(slot, slot, ssem.at[3 * i + k], rsem.at[3 * i + k], sib)
                r.start()
                cps.append(r)
        for r in cps:
            r.wait()

    return _pallas(
        body, name=name, in_specs=[_ANY] * n, out_specs=[_ANY] * n,
        out_shape=[jax.ShapeDtypeStruct(b.shape, b.dtype) for b in bufs],
        scratch_shapes=[pltpu.SemaphoreType.DMA((3 * n,))] * 2,
        input_output_aliases={i: i for i in range(n)},
        compiler_params=pltpu.CompilerParams(has_side_effects=True),
    )(*bufs)


_HBM = pl.BlockSpec(memory_space=pltpu.HBM)
_SEM = pl.BlockSpec(memory_space=pltpu.SEMAPHORE)
_EFFECT = pltpu.SideEffectType.DATAFLOW_SIDE_EFFECTING


def _split_start(bufs, plan, n_copies, *, name):
    n = len(bufs)

    def body(*refs):
        ssem, rsem = refs[n], refs[n + 1]
        for s, src, dst, dev in plan(refs[:n]):
            _rcopy(src, dst, ssem.at[s], rsem.at[s], dev).start()
        refs[-1][...] = jnp.zeros_like(refs[-1])

    res = _pallas(
        body, name=name, in_specs=[_HBM] * n,
        out_specs=(_SEM, _SEM, *[_HBM] * n, pl.BlockSpec(memory_space=pltpu.VMEM)),
        out_shape=(pltpu.SemaphoreType.DMA((n_copies,)), pltpu.SemaphoreType.DMA((n_copies,)),
                   *[pltpu.HBM(b.shape, b.dtype) for b in bufs], jax.ShapeDtypeStruct((8, LANES), F32)),
        input_output_aliases={i: 2 + i for i in range(n)},
        compiler_params=pltpu.CompilerParams(has_side_effects=_EFFECT),
    )(*[pltpu.with_memory_space_constraint(b, pltpu.HBM) for b in bufs])
    return res[0], res[1], list(res[2:2 + n]), res[-1]


def _split_wait(ssem, rsem, bufs, after, plan, *, name):
    n = len(bufs)

    def body(*refs):
        ssem_ref, rsem_ref = refs[n], refs[n + 1]
        for s, src, dst, dev in plan(refs[:n]):
            cp = _rcopy(src, dst, ssem_ref.at[s], rsem_ref.at[s], dev)
            cp.wait_send()
            cp.wait_recv()

    return list(_pallas(
        body, name=name, in_specs=[_HBM] * n + [_SEM, _SEM, _ANY], out_specs=[_HBM] * n,
        out_shape=[pltpu.HBM(b.shape, b.dtype) for b in bufs],
        input_output_aliases={i: i for i in range(n)},
        compiler_params=pltpu.CompilerParams(has_side_effects=_EFFECT),
    )(*bufs, ssem, rsem, after))


def _swap_plan(n):
    def plan(bufs):
        x, y, c = _me()
        return [(i, bufs[i].at[:, 1 - c], bufs[n + i], (x, y, 1 - c)) for i in range(n)]
    return plan


def _scatter_plan(n):
    def plan(bufs):
        x, y, c = _me()
        out = []
        for i in range(n):
            for k, (px, py) in enumerate(_other_chips(x, y)):
                out.append((3 * i + k, bufs[i].at[2 * px + py], bufs[n + i].at[k], (px, py, c)))
        return out
    return plan


def _swap_halves(grads, *, name):
    n = len(grads)
    views = [g.reshape(N_CHIPS, 2, g.shape[1] // 2, g.shape[2]) for g in grads]

    def body(*refs):
        ins, outs = refs[:n], refs[n:2 * n]
        ssem, rsem = refs[2 * n:]
        x, y, c = _me()
        sib = (x, y, 1 - c)
        cps = []
        for i in range(n):
            r = _rcopy(ins[i].at[:, 1 - c], outs[i], ssem.at[i], rsem.at[i], sib)
            r.start()
            cps.append(r)
        for r in cps:
            r.wait()

    return _pallas(
        body, name=name, in_specs=[_ANY] * n, out_specs=[_ANY] * n,
        out_shape=[jax.ShapeDtypeStruct((N_CHIPS,) + v.shape[2:], v.dtype) for v in views],
        scratch_shapes=[pltpu.SemaphoreType.DMA((n,)), pltpu.SemaphoreType.DMA((n,))],
        compiler_params=pltpu.CompilerParams(has_side_effects=True),
    )(*views)


def _join_halves(halves, *, name):
    n = len(halves)

    def body(*refs):
        outs = refs[n:2 * n]
        ssem, rsem = refs[2 * n:]
        x, y, c = _me()
        sib = (x, y, 1 - c)
        cps = []
        for i in range(n):
            r = _rcopy(outs[i].at[c], outs[i].at[c], ssem.at[i], rsem.at[i], sib)
            r.start()
            cps.append(r)
        for r in cps:
            r.wait()

    return _pallas(
        body, name=name, in_specs=[_ANY] * n, out_specs=[_ANY] * n,
        out_shape=[jax.ShapeDtypeStruct(h.shape, h.dtype) for h in halves],
        scratch_shapes=[pltpu.SemaphoreType.DMA((n,)), pltpu.SemaphoreType.DMA((n,))],
        input_output_aliases={i: i for i in range(n)},
        compiler_params=pltpu.CompilerParams(has_side_effects=True),
    )(*halves)


def _allreduce_small(parts, loss11):
    n = len(parts)
    widths = [p.shape[1] for p in parts]
    total = sum(widths) + LANES

    def body(*refs):
        o_ref, mine, buf, ssem, rsem = refs[n + 1:]
        x, y, c = _me()
        me = 4 * x + 2 * y + c
        off = 0
        for r, w in zip(refs[:n], widths):
            mine[:, off:off + w] = r[...]
            off += w
        mine[:, off:] = jnp.broadcast_to(refs[n][...], (1, LANES))
        buf[me] = mine[...]
        cps = []
        for k in range(1, 8):
            peer = (x ^ (k >> 2), y ^ ((k >> 1) & 1), c ^ (k & 1))
            r = _rcopy(mine, buf.at[me], ssem.at[k - 1], rsem.at[k - 1], peer)
            r.start()
            cps.append(r)
        for k in range(1, 8):
            peer = (x ^ (k >> 2), y ^ ((k >> 1) & 1), c ^ (k & 1))
            pid = 4 * peer[0] + 2 * peer[1] + peer[2]
            _rcopy(mine, buf.at[pid], ssem.at[k - 1], rsem.at[k - 1], peer).wait_recv()
        for r in cps:
            r.wait_send()
        tot = buf[0]
        for d in range(1, 8):
            tot = tot + buf[d]
        o_ref[...] = tot

    vm = pl.BlockSpec(memory_space=pltpu.VMEM)
    return _pallas(
        body, name="allreduce_small", in_specs=[vm] * (n + 1), out_specs=vm,
        out_shape=jax.ShapeDtypeStruct((1, total), F32),
        scratch_shapes=[pltpu.VMEM((1, total), F32), pltpu.VMEM((8, 1, total), F32),
                        pltpu.SemaphoreType.DMA((7,)), pltpu.SemaphoreType.DMA((7,))],
        compiler_params=pltpu.CompilerParams(has_side_effects=True),
    )(*parts, loss11)


def _adamw_small(red, ws, ms, vs):
    n = len(ws)

    def body(*refs):
        red_ref = refs[0]
        outs = refs[1 + 3 * n:]
        off = 0
        for i in range(n):
            w = refs[1 + i].shape[1]
            g = red_ref[:, off:off + w]
            d, m, v = _adamw_vals(refs[1 + i][...], g, refs[1 + n + i][...], refs[1 + 2 * n + i][...])
            for o, val in zip(outs[4 * i:4 * i + 4], (g, d, m, v)):
                o[...] = val
            off += w

    vm = pl.BlockSpec(memory_space=pltpu.VMEM)
    res = _pallas(
        body, name="adamw_small", in_specs=[vm] * (1 + 3 * n), out_specs=[vm] * (4 * n),
        out_shape=[jax.ShapeDtypeStruct(w.shape, F32) for w in ws for _ in range(4)],
    )(red, *ws, *ms, *vs)
    return [res[4 * i:4 * i + 4] for i in range(n)]


def _rope_tables(positions, S):
    pos = positions.reshape(S, 1).astype(F32)
    half = RET_QK // 2
    inv = ROPE_THETA ** (-jnp.arange(half, dtype=F32) / half)
    ang = pos * inv
    cosr = jnp.concatenate([jnp.cos(ang), jnp.cos(ang)], axis=1)
    sinr = jnp.concatenate([-jnp.sin(ang), jnp.sin(ang)], axis=1)
    half = QK_ROPE // 2
    inv = ROPE_THETA ** (-jnp.arange(half, dtype=F32) / half)
    ang = pos * inv
    z = jnp.zeros((S, half), F32)
    c = jnp.concatenate([jnp.cos(ang), jnp.cos(ang), z, z], axis=1)
    s1 = jnp.concatenate([-jnp.sin(ang), z, z, z], axis=1)
    s2 = jnp.concatenate([z, jnp.sin(ang), z, z], axis=1)
    return cosr, sinr, (c, s1, s2)


def _cat_cols(g):
    return jnp.concatenate([g[j] for j in range(N_CHIPS)], axis=1)


def _split_cols(w):
    return jnp.stack(jnp.split(w, N_CHIPS, axis=1))


def kernel(x, positions, norm_mix_g, w_in, ret_norm_g, w_ret_o, q_a_norm_g, w_q_b, kv_a_norm_g, w_kv_b, w_mla_o, w_out, norm_mlp_g, w_up, w_down, norm_f_g, loss_target, m_norm_mix_g, m_w_in, m_ret_norm_g, m_w_ret_o, m_q_a_norm_g, m_w_q_b, m_kv_a_norm_g, m_w_kv_b, m_w_mla_o, m_w_out, m_norm_mlp_g, m_w_up, m_w_down, m_norm_f_g, v_norm_mix_g, v_w_in, v_ret_norm_g, v_w_ret_o, v_q_a_norm_g, v_w_q_b, v_kv_a_norm_g, v_w_kv_b, v_w_mla_o, v_w_out, v_norm_mlp_g, v_w_up, v_w_down, v_norm_f_g):
    S, D = x.shape[1], x.shape[2]
    RVW = w_ret_o.shape[1] * N_CHIPS
    RH = RVW // RET_V
    RQW = RH * RET_QK
    MVW = w_mla_o.shape[1] * N_CHIPS
    MH = MVW // V_HEAD
    QL, KVL = w_q_b.shape[1], w_kv_b.shape[1]
    T_RET = _tile(S, 256)
    T_ATT = _tile(S, 512)

    xs = x.reshape(S, D)
    tgt = loss_target.reshape(S, D)
    cosr, sinr, pe_tabs = _rope_tables(positions, S)
    lgam = jnp.log(1.0 - 2.0 ** (-5.0 - jnp.arange(RH, dtype=F32)))
    lgam = jnp.broadcast_to(lgam[:, None, None], (RH, 8, LANES))

    big = ("w_in", "w_ret_o", "w_q_b", "w_kv_b", "w_mla_o", "w_out", "w_up", "w_down")
    w_sh = dict(w_in=w_in[0].T, w_ret_o=w_ret_o[0], w_q_b=w_q_b[0], w_kv_b=w_kv_b[0], w_mla_o=w_mla_o[0],
                w_out=w_out[0], w_up=w_up[0], w_down=w_down[0])
    m_sh = dict(w_in=m_w_in[0].T, w_ret_o=m_w_ret_o[0], w_q_b=m_w_q_b[0], w_kv_b=m_w_kv_b[0],
                w_mla_o=m_w_mla_o[0], w_out=m_w_out[0], w_up=m_w_up[0], w_down=m_w_down[0])
    v_sh = dict(w_in=v_w_in[0].T, w_ret_o=v_w_ret_o[0], w_q_b=v_w_q_b[0], w_kv_b=v_w_kv_b[0],
                w_mla_o=v_w_mla_o[0], w_out=v_w_out[0], w_up=v_w_up[0], w_down=v_w_down[0])
    col_sharded = ("w_q_b", "w_kv_b", "w_up")
    c_sh = w_in.shape[2]
    c_pad = -(-c_sh // 64) * 64
    place = jnp.stack([lax.axis_index("c"), 2 * lax.axis_index("x") + lax.axis_index("y")]).astype(jnp.int32)

    def whole(k, g):
        g = g.reshape(N_CHIPS, w_sh[k].shape[0], w_sh[k].shape[1])
        if k == "w_up":
            return g
        return _cat_cols(g) if k in col_sharded else g.reshape(-1, g.shape[2])

    first = ("w_in", "w_q_b", "w_kv_b")
    later = ("w_ret_o", "w_mla_o", "w_out", "w_up", "w_down")
    first_bufs = [_cast_into_slot(w_sh[k], place, name="cast_" + k, rows=c_pad if k == "w_in" else None)
                  for k in first]
    first_ssem, first_rsem, first_bufs, first_token = _split_start(
        first_bufs, _gather_ici_plan, 3 * len(first), name="gather_first_start")
    later_bufs = [_cast_into_slot(w_sh[k], place, name="cast_" + k, deps=(first_token,)) for k in later[:-1]]
    first_bufs = _split_wait(first_ssem, first_rsem, first_bufs, later_bufs[-1], _gather_ici_plan,
                             name="gather_first_wait")
    got = _forward_halves(first_bufs, name="gather_first_forward")
    full = {k: whole(k, g) for k, g in zip(first[1:], got[1:])}
    later_bufs.append(_cast_into_slot(w_sh[later[-1]], place, name="cast_" + later[-1], deps=(got[0],)))
    later_ssem, later_rsem, later_bufs, later_token = _split_start(
        later_bufs, _gather_ici_plan, 3 * len(later), name="gather_later_start")

    o_rq, o_rk, o_rv, o_rg = 0, RQW, 2 * RQW, 2 * RQW + RVW
    o_cq = 2 * RQW + 2 * RVW
    o_ckv, o_kpe = o_cq + QL, o_cq + QL + KVL
    o_gr = o_kpe + QK_ROPE
    o_gm = o_gr + D
    n_ret = RH * RET_HEAD_COLS
    off_gret, off_gmla, off_cq, off_ckv = n_ret, n_ret + D, n_ret + 2 * D, n_ret + 2 * D + QL
    n_a = off_ckv + KVL
    runs = []
    for h in range(RH):
        base = h * RET_HEAD_COLS
        runs += [(o_rq + h * RET_QK, RET_QK, base), (o_rk + h * RET_QK, RET_QK, base + RET_QK),
                 (o_rv + h * RET_V, RET_V, base + 2 * RET_QK), (o_rg + h * RET_V, RET_V, base + 2 * RET_QK + RET_V)]
    runs += [(o_gr, D, off_gret), (o_gm, D, off_gmla), (o_cq, QL, off_cq), (o_ckv, KVL, off_ckv),
             (o_kpe, QK_ROPE, n_a)]

    def take(parts, start, width):
        out, lo = [], 0
        for p in parts:
            hi = lo + p.shape[0]
            a, b = max(start, lo), min(start + width, hi)
            if a < b:
                out.append(p[a - lo:b - lo])
            lo = hi
        return out

    wi = [got[0].reshape(N_CHIPS, c_pad, D)[jj, :c_sh] for jj in range(N_CHIPS)]
    here = sorted(runs, key=lambda r: r[2])
    wa = jnp.concatenate([p for s0, w, _ in here[:-1] for p in take(wi, s0, w)], axis=0)
    wkpe = jnp.concatenate(take(wi, o_kpe, QK_ROPE) + [jnp.zeros((LANES - QK_ROPE, D), BF16)], axis=0)
    wq = jnp.pad(full["w_q_b"].reshape(QL, MH, QK_NOPE + QK_ROPE),
                 ((0, 0), (0, 0), (0, LANES - QK_ROPE))).reshape(QL, MH * 2 * LANES)
    wkv = full["w_kv_b"]

    u, rstd0 = _rmsnorm_fwd(xs, norm_mix_g, name="norm_mix")
    proj = _mm(u, wa, mode="nt", outs=[F32], name="in_proj", deps=(later_token,))
    kpe = _mm(u, wkpe, mode="nt", outs=[F32], name="kpe_proj")
    ry, gated, states = _ret_fwd(proj, cosr, sinr, lgam, ret_norm_g, RH, T=T_RET)
    cqn, rstd_q = _rmsnorm_fwd(proj, q_a_norm_g, name="norm_q", width=QL, col=off_cq // QL)
    ckvn, rstd_kv = _rmsnorm_fwd(proj, kv_a_norm_g, name="norm_kv", width=KVL, col=off_ckv // KVL)
    qf, kf, vb = _qkv_proj(cqn, ckvn, wq, wkv, kpe, pe_tabs, MH)
    my, my_b, lse2 = _attn_fwd(qf, kf, vb, MH, T=T_ATT)
    later_bufs = _split_wait(later_ssem, later_rsem, later_bufs, my, _gather_ici_plan, name="gather_later_wait")
    later_bufs = _forward_halves(later_bufs, name="gather_later_forward")
    full.update({k: whole(k, g) for k, g in zip(later, later_bufs)})
    y_ret = _mm(gated, full["w_ret_o"], mode="nn", outs=[BF16], name="ret_o")
    y_mla, merged = _mm(my_b, full["w_mla_o"], mode="nn", outs=[BF16, BF16], name="mla_o",
                        epi=lambda acc, gr, gm, yr: (acc, _sigmoid(gr) * yr + _sigmoid(gm) * acc),
                        extras=((proj, off_gret), (proj, off_gmla), y_ret))
    h1 = _mm(merged, full["w_out"], mode="nn", outs=[F32], name="out_proj",
             epi=lambda acc, r: (acc + r,), extras=(xs,))
    n1, rstd1 = _rmsnorm_fwd(h1, norm_mlp_g, name="norm_mlp")

    def up_epi(acc):
        r = jnp.maximum(acc, 0.0)
        return acc, r * r

    z, act = _mm(n1, full["w_up"], mode="nn", outs=[F32, BF16], name="up_proj", epi=up_epi)
    h2 = _mm(act, full["w_down"], mode="nn", outs=[F32], name="down_proj",
             epi=lambda acc, r: (acc + r,), extras=(h1,))
    loss11, dh2, dh2_b, g_norm_f = _final_loss(h2, norm_f_g.reshape(1, D), tgt)

    dz = _mm(dh2_b, full["w_down"], mode="nt", outs=[BF16], name="down_bwd_x",
             epi=lambda acc, zz: (acc * (2.0 * jnp.maximum(zz, 0.0)),), extras=(z,))
    g_w_down = _mm(act, dh2_b, mode="tn", outs=[BF16], name="down_bwd_w")
    dn1 = _mm(dz, full["w_up"], mode="nt", outs=[F32], name="up_bwd_x")
    g_w_up = _mm(n1, dz, mode="tn", outs=[BF16], name="up_bwd_w", out_shards=True)

    def reduce_begin(tag, names, grads):
        pcs = [g if g.ndim == 3 else g.reshape(N_CHIPS, g.shape[0] // N_CHIPS, g.shape[1]) for g in grads]
        theirs = _swap_halves(pcs, name="swap_" + tag)
        sums = [_sum_pair(p, t, place, name="sum_pair_" + k) for k, p, t in zip(names, pcs, theirs)]
        return pcs, theirs, sums

    def scatter_begin(tag, sums):
        lands = [lax.empty((3,) + s.shape[1:], s.dtype) for s in sums]
        return _split_start(sums + lands, _scatter_plan(len(sums)), 3 * len(sums), name="scatter_" + tag + "_start")

    def swap_begin(tag, grads):
        views = [g if g.ndim == 3 else g.reshape(N_CHIPS, g.shape[0] // N_CHIPS, g.shape[1]) for g in grads]
        views = [v.reshape(N_CHIPS, 2, v.shape[1] // 2, v.shape[2]) for v in views]
        lands = [lax.empty((N_CHIPS,) + v.shape[2:], v.dtype) for v in views]
        return _split_start(views + lands, _swap_plan(len(views)), len(views), name="swap_" + tag + "_start")

    def swap_end(tag, names, handle, after):
        n = len(names)
        bufs = _split_wait(handle[0], handle[1], handle[2], after, _swap_plan(n), name="swap_" + tag + "_wait")
        pcs = [b.reshape(N_CHIPS, 2 * b.shape[2], b.shape[3]) for b in bufs[:n]]
        sums = [_sum_pair(p, t, place, name="sum_pair_" + k) for k, p, t in zip(names, pcs, bufs[n:])]
        return pcs, bufs[n:], sums

    g1 = ("w_up", "w_down")
    swap1 = swap_begin("g1", (g_w_up, g_w_down))
    dh1, g_norm_mlp, dh1_b = _rmsnorm_bwd(dn1, h1, rstd1, norm_mlp_g, name="norm_mlp_bwd", res=dh2,
                                          deps=(swap1[3],), bf16_copy=1)
    dmerged = _mm(dh1_b, full["w_out"], mode="nt", outs=[F32], name="out_bwd_x")
    pcs1, theirs1, sums1 = swap_end("g1", g1, swap1, dmerged)
    ssem1, rsem1, bufs1, token1 = scatter_begin("g1", sums1)
    g_w_out = _mm(merged, dh1_b, mode="tn", outs=[BF16], name="out_bwd_w", deps=(token1,))
    dproj, dy_ret, dy_mla = _merge_bwd(dmerged, proj, y_ret, y_mla, D, off_gret)
    dgated = _mm(dy_ret, full["w_ret_o"], mode="nt", outs=[F32], name="ret_o_bwd_x")
    g_w_ret_o = _mm(gated, dy_ret, mode="tn", outs=[BF16], name="ret_o_bwd_w")
    dproj, g_ret_norm = _ret_bwd(proj, cosr, sinr, lgam, ret_norm_g, ry, dgated, states, dproj, RH, T=T_RET)
    def delta_epi(acc, o):
        rows = acc.shape[0]
        return acc, [jnp.broadcast_to(jnp.sum(acc[:, lo:lo + V_HEAD] * o[:, lo:lo + V_HEAD], axis=-1, keepdims=True),
                                      (rows, LANES)) for lo in range(0, acc.shape[1], V_HEAD)]

    dob, delta = _mm(dy_mla, full["w_mla_o"], mode="nt", outs=[BF16], name="mla_o_bwd_x", epi=delta_epi,
                     extras=(my,), more_outs=lambda tm, tn: [
                         (jax.ShapeDtypeStruct((MH, S, LANES), F32),
                          pl.BlockSpec((tn // V_HEAD, tm, LANES), lambda i, j, k: (j, i, 0)))])
    g_w_mla_o = _mm(my_b, dy_mla, mode="tn", outs=[BF16], name="mla_o_bwd_w")
    g2 = ("w_out", "w_ret_o", "w_mla_o")
    swap2 = swap_begin("g2", (g_w_out, g_w_ret_o, g_w_mla_o))
    dq_all, dkv_all, dkpe_h = _attn_bwd(qf, kf, vb, dob, lse2, delta, pe_tabs, MH, T=T_ATT, deps=(swap2[3],))
    pcs2, theirs2, sums2 = swap_end("g2", g2, swap2, dkv_all)
    ssem2, rsem2, bufs2, token2 = scatter_begin("g2", sums2)
    dkpe = _kpe_sum(dkpe_h, pe_tabs, MH)
    dcqn = _mm(dq_all, wq, mode="nt", outs=[F32], name="q_bwd_x", deps=(token2,))
    g_wq = _mm(cqn, dq_all, mode="tn", outs=[BF16], name="q_bwd_w")
    dckvn = _mm(dkv_all, wkv, mode="nt", outs=[F32], name="kv_bwd_x")
    g_wkv = _mm(ckvn, dkv_all, mode="tn", outs=[BF16], name="kv_bwd_w")
    dproj, g_q_a = _rmsnorm_bwd(dcqn, proj, rstd_q, q_a_norm_g, name="norm_q_bwd", into=(dproj, off_cq // QL),
                                width=QL, col=off_cq // QL)
    dproj, g_kv_a = _rmsnorm_bwd(dckvn, proj, rstd_kv, kv_a_norm_g, name="norm_kv_bwd", into=(dproj, off_ckv // KVL),
                                 width=KVL, col=off_ckv // KVL)
    g_wa = _mm(dproj, u, mode="tn", outs=[BF16], name="in_bwd_w")
    g_wkpe = _mm(dkpe, u, mode="tn", outs=[BF16], name="kpe_bwd_w")

    there = sorted(runs)
    g_parts = [g_wa, g_wkpe]
    g_w_in = jnp.stack([jnp.concatenate(
        [p for s0, w, d0 in there for a, b in [(max(s0, jj * c_sh), min(s0 + w, (jj + 1) * c_sh))] if a < b
         for p in take(g_parts, d0 + a - s0, b - a)] + [jnp.zeros((c_pad - c_sh, D), BF16)], axis=0)
        for jj in range(N_CHIPS)])
    gq = g_wq.reshape(QL, MH, 2 * LANES)[:, :, :QK_NOPE + QK_ROPE].reshape(QL, MH * (QK_NOPE + QK_ROPE))
    g3 = ("w_in", "w_q_b", "w_kv_b")
    pcs3, theirs3, sums3 = reduce_begin("g3", g3, (g_w_in, _split_cols(gq), _split_cols(g_wkv)))
    ssem3, rsem3, bufs3, token3 = scatter_begin("g3", sums3)
    du_a = _mm(dproj, wa, mode="nn", outs=[F32], name="in_bwd_x", tk=2816, deps=(token3,))
    du = _mm(dkpe, wkpe, mode="nn", outs=[F32], name="kpe_bwd_x", epi=lambda acc, r: (acc + r,), extras=(du_a,))
    dx, g_norm_mix = _rmsnorm_bwd(du, xs, rstd0, norm_mix_g, name="norm_mix_bwd", res=dh1)

    bufs1 = _split_wait(ssem1, rsem1, bufs1, dx, _scatter_plan(len(g1)), name="scatter_g1_wait")
    bufs2 = _split_wait(ssem2, rsem2, bufs2, dx, _scatter_plan(len(g2)), name="scatter_g2_wait")
    bufs3 = _split_wait(ssem3, rsem3, bufs3, dx, _scatter_plan(len(g3)), name="scatter_g3_wait")
    recv1, recv2, recv3 = bufs1[len(g1):], bufs2[len(g2):], bufs3[len(g3):]
    halves = {}
    for names, pcs, theirs, recv in ((g1, pcs1, theirs1, recv1), (g2, pcs2, theirs2, recv2), (g3, pcs3, theirs3, recv3)):
        for k, p, t, r in zip(names, pcs, theirs, recv):
            halves[k] = _sum_chips(p, t, r, place, name="sum_chips_" + k)
    joined = _join_halves([halves[k] for k in big], name="join_halves")
    g_shard = {k: g.reshape(2 * g.shape[1], g.shape[2]) for k, g in zip(big, joined)}

    small = ("norm_mix_g", "ret_norm_g", "q_a_norm_g", "kv_a_norm_g", "norm_mlp_g", "norm_f_g")
    g_small = [g_norm_mix, g_ret_norm, g_q_a, g_kv_a, g_norm_mlp, g_norm_f]
    red = _allreduce_small(g_small, loss11)
    loss = red[0, red.shape[1] - 1]
    w_small = [norm_mix_g, ret_norm_g, q_a_norm_g, kv_a_norm_g, norm_mlp_g, norm_f_g]
    m_small = [m_norm_mix_g, m_ret_norm_g, m_q_a_norm_g, m_kv_a_norm_g, m_norm_mlp_g, m_norm_f_g]
    v_small = [v_norm_mix_g, v_ret_norm_g, v_q_a_norm_g, v_kv_a_norm_g, v_norm_mlp_g, v_norm_f_g]
    row = lambda a: a.reshape(1, -1)
    upd = _adamw_small(red, [row(a) for a in w_small], [row(a) for a in m_small], [row(a) for a in v_small])
    out_g, out_d, out_m, out_v = {}, {}, {}, {}
    for k, wv, (g_, d_, m_, v_) in zip(small, w_small, upd):
        out_g[k], out_d[k], out_m[k], out_v[k] = [a.reshape(wv.shape) for a in (g_, d_, m_, v_)]

    for k in big:
        res = _rows_call(lambda w, g, m, v: (g,) + _adamw_vals(w, g, m, v),
                         [w_sh[k], g_shard[k], m_sh[k], v_sh[k]], [F32] * 4, name="adamw_" + k)
        if k == "w_in":
            res = [r.T for r in res]
        out_g[k], out_d[k], out_m[k], out_v[k] = [r[None] for r in res]

    order = ("norm_mix_g", "w_in", "ret_norm_g", "w_ret_o", "q_a_norm_g", "w_q_b", "kv_a_norm_g", "w_kv_b",
             "w_mla_o", "w_out", "norm_mlp_g", "w_up", "w_down", "norm_f_g")
    return (loss, dx.reshape(1, S, D), *[out_g[k] for k in order], *[out_d[k] for k in order],
            *[out_m[k] for k in order], *[out_v[k] for k in order])
```

```python
import math

import jax
import jax.numpy as jnp
from jax import lax
from jax.experimental import pallas as pl
from jax.experimental.pallas import tpu as pltpu

F32 = jnp.float32
BF16 = jnp.bfloat16

EPS = 1e-6
ROPE_THETA = 10000.0
CHUNK = 64
RET_QK = 128
RET_V = 256
RET_HEAD_COLS = 2 * RET_QK + 2 * RET_V
QK_NOPE = 128
QK_ROPE = 64
V_HEAD = 128
LANES = 128
LOG2E = math.log2(math.e)

ADAM_LR = 0.001
ADAM_B1 = 0.9
ADAM_B2 = 0.999
ADAM_EPS = 1e-08
ADAM_WD = 0.01
ADAM_STEP = 10

N_CHIPS = 4
VMEM_LIMIT = 56 * 1024 * 1024
MESH = pl.DeviceIdType.MESH
NEG = -1e30


def _pallas(body, **kw):
    return pl.pallas_call(body, **kw)


def _params(sem=None):
    return pltpu.CompilerParams(dimension_semantics=sem, vmem_limit_bytes=VMEM_LIMIT)


def _tile(n, want):
    t = min(n, want)
    while n % t:
        t //= 2
    return t


_ANY = pl.BlockSpec(memory_space=pl.ANY)
TN_BF16_TK = 4096


def _mm(a, b, *, mode, outs, name, epi=None, extras=(), deps=(), out_shards=False, more_outs=None, tail=None,
        tm=1024, tn=1024, tk=2048):
    shards = b.shape[0] if b.ndim == 3 else 1
    brows, bcols = b.shape[-2], b.shape[-1] * shards
    if mode == "nn":
        (M, K), N = a.shape, bcols
    elif mode == "nt":
        (M, K), N = a.shape, brows
    else:
        (K, M), N = a.shape, bcols
    if mode == "tn" and a.dtype == BF16 and b.dtype == BF16:
        tk = max(tk, TN_BF16_TK)
    tm = _tile(M, tm)
    tn = _tile(N // (shards if mode == "nn" else 1) // (N_CHIPS if out_shards else 1), tn)
    tk = _tile(K // (shards if mode == "nt" else 1), tk)
    nk = K // tk
    if mode == "nn":
        a_spec = pl.BlockSpec((tm, tk), lambda i, j, k: (i, k))
        dims = (((1,), (0,)), ((), ()))
        if shards > 1:
            per = N // shards // tn
            b_spec = pl.BlockSpec((None, tk, tn), lambda i, j, k: (j // per, k, j % per))
        else:
            b_spec = pl.BlockSpec((tk, tn), lambda i, j, k: (k, j))
    elif mode == "nt":
        a_spec = pl.BlockSpec((tm, tk), lambda i, j, k: (i, k))
        dims = (((1,), (1,)), ((), ()))
        if shards > 1:
            per = K // shards // tk
            b_spec = pl.BlockSpec((None, tn, tk), lambda i, j, k: (k // per, j, k % per))
        else:
            b_spec = pl.BlockSpec((tn, tk), lambda i, j, k: (j, k))
    else:
        assert shards == 1
        a_spec = pl.BlockSpec((tk, tm), lambda i, j, k: (k, i))
        b_spec = pl.BlockSpec((tk, tn), lambda i, j, k: (k, j))
        dims = (((0,), (0,)), ((), ()))
    if out_shards:
        assert not extras
        oper = N // N_CHIPS // tn
        o_spec = pl.BlockSpec((None, tm, tn), lambda i, j, k: (j // oper, i, j % oper))
        o_shape = (N_CHIPS, M, N // N_CHIPS)
    else:
        o_spec = pl.BlockSpec((tm, tn), lambda i, j, k: (i, j))
        o_shape = (M, N)
    more = [] if more_outs is None else more_outs(tm, tn)
    ex_arrays = [e[0] if isinstance(e, tuple) else e for e in extras]
    ex_specs = [pl.BlockSpec((tm, tn), lambda i, j, k, off=e[1] // tn: (i, off + j)) if isinstance(e, tuple)
                else o_spec for e in extras]
    n_ex, n_out, n_dep = len(extras), len(outs) + len(more), len(deps)
    if epi is None:
        epi = lambda acc: (acc,)
    tails, tail_specs = [], []
    if tail is not None:
        assert mode == "nn"
        tails = list(tail)
        k2 = tail[0].shape[1]
        tail_specs = [pl.BlockSpec((tm, k2), lambda i, j, k: (i, 0)), pl.BlockSpec((k2, tn), lambda i, j, k: (0, j))]
    n_tail = len(tails)

    def body(*refs):
        a_ref, b_ref = refs[0], refs[1]
        ex_refs = refs[2:2 + n_ex]
        t_refs = refs[2 + n_ex:2 + n_ex + n_tail]
        first_out = 2 + n_ex + n_tail + n_dep
        o_refs = refs[first_out:first_out + n_out]
        part = lax.dot_general(a_ref[...].astype(BF16), b_ref[...].astype(BF16), dims,
                               preferred_element_type=F32)

        def finish(acc):
            if n_tail:
                acc = acc + lax.dot_general(t_refs[0][...].astype(BF16), t_refs[1][...].astype(BF16), dims,
                                            preferred_element_type=F32)
            vals = epi(acc, *[r[...] for r in ex_refs])
            for r, v in zip(o_refs, vals):
                if isinstance(v, (list, tuple)):
                    for lead, piece in enumerate(v):
                        r[lead] = piece.astype(r.dtype)
                else:
                    r[...] = v.astype(r.dtype)

        if nk == 1:
            finish(part)
        else:
            acc_ref = refs[-1]
            k = pl.program_id(2)

            @pl.when(k == 0)
            def _():
                acc_ref[...] = part

            @pl.when(k > 0)
            def _():
                acc_ref[...] += part

            @pl.when(k == nk - 1)
            def _():
                finish(acc_ref[...])

    res = _pallas(
        body, name=name, grid=(M // tm, N // tn, nk),
        in_specs=[a_spec, b_spec] + ex_specs + tail_specs + [_ANY] * n_dep,
        out_specs=[o_spec] * len(outs) + [spec for _, spec in more],
        out_shape=[jax.ShapeDtypeStruct(o_shape, d) for d in outs] + [shape for shape, _ in more],
        scratch_shapes=[pltpu.VMEM((tm, tn), F32)] if nk > 1 else [],
        compiler_params=_params(("parallel", "parallel", "arbitrary")),
    )(a, b, *ex_arrays, *tails, *deps)
    return res[0] if n_out == 1 else res


def _rmsnorm_fwd(x, g, *, name, width=None, col=0, tr=256):
    S = x.shape[0]
    W = x.shape[1] if width is None else width
    tr = _tile(S, tr)

    def body(x_ref, g_ref, y_ref, r_ref):
        xv = x_ref[...]
        rstd = lax.rsqrt(jnp.mean(xv * xv, axis=-1, keepdims=True) + EPS)
        y_ref[...] = (xv * rstd * g_ref[...]).astype(BF16)
        r_ref[...] = rstd

    return _pallas(
        body, name=name, grid=(S // tr,),
        in_specs=[pl.BlockSpec((tr, W), lambda i: (i, col)), pl.BlockSpec((1, W), lambda i: (0, 0))],
        out_specs=[pl.BlockSpec((tr, W), lambda i: (i, 0)), pl.BlockSpec((tr, 1), lambda i: (i, 0))],
        out_shape=[jax.ShapeDtypeStruct((S, W), BF16), jax.ShapeDtypeStruct((S, 1), F32)],
        compiler_params=_params(("parallel",)),
    )(x, g)


def _rmsnorm_bwd(dy, x, rstd, g, *, name, res=None, into=None, deps=(), bf16_copy=0, width=None, col=0, tr=256):
    S = x.shape[0]
    W = x.shape[1] if width is None else width
    tr = _tile(S, tr)
    has_res = res is not None

    def body(*refs):
        dy_ref, x_ref, r_ref, g_ref = refs[:4]
        dx_ref, dg_ref = refs[-2 - bf16_copy], refs[-1 - bf16_copy]
        rstd_v = r_ref[...]
        xhat = x_ref[...] * rstd_v
        dyv = dy_ref[...].astype(F32)
        dyg = dyv * g_ref[...]
        dx = rstd_v * (dyg - xhat * jnp.mean(dyg * xhat, axis=-1, keepdims=True))
        if has_res:
            dx = dx + refs[4][...]
        dx_ref[...] = dx.astype(dx_ref.dtype)
        if bf16_copy:
            refs[-1][...] = dx.astype(BF16)
        part = jnp.sum(dyv * xhat, axis=0, keepdims=True)

        @pl.when(pl.program_id(0) == 0)
        def _():
            dg_ref[...] = part

        @pl.when(pl.program_id(0) > 0)
        def _():
            dg_ref[...] += part

    row = pl.BlockSpec((tr, W), lambda i: (i, 0))
    ins = [dy, x, rstd, g] + ([res] if has_res else [])
    in_specs = [row, pl.BlockSpec((tr, W), lambda i: (i, col)), pl.BlockSpec((tr, 1), lambda i: (i, 0)),
                pl.BlockSpec((1, W), lambda i: (0, 0))] + ([row] if has_res else [])
    if into is None:
        dx_spec, dx_shape, alias = row, jax.ShapeDtypeStruct((S, W), F32), {}
    else:
        buf, col_out = into
        ins.append(buf)
        in_specs.append(_ANY)
        dx_spec = pl.BlockSpec((tr, W), lambda i: (i, col_out))
        dx_shape = jax.ShapeDtypeStruct(buf.shape, buf.dtype)
        alias = {len(ins) - 1: 0}
    ins += list(deps)
    in_specs += [_ANY] * len(deps)
    return _pallas(
        body, name=name, grid=(S // tr,), in_specs=in_specs,
        out_specs=[dx_spec, pl.BlockSpec((1, W), lambda i: (0, 0))] + [row] * bf16_copy,
        out_shape=[dx_shape, jax.ShapeDtypeStruct((1, W), F32)] + [jax.ShapeDtypeStruct((S, W), BF16)] * bf16_copy,
        input_output_aliases=alias,
        compiler_params=_params(("arbitrary",)),
    )(*ins)


def _final_loss(h2, g, target, *, tr=256):
    S, D = h2.shape
    tr = _tile(S, tr)

    def body(h_ref, g_ref, t_ref, loss_ref, dh_ref, dhb_ref, dg_ref):
        hv = h_ref[...]
        rstd = lax.rsqrt(jnp.mean(hv * hv, axis=-1, keepdims=True) + EPS)
        xhat = hv * rstd
        e = xhat * g_ref[...] - t_ref[...]
        lpart = (0.5 / D) * jnp.sum(jnp.sum(e * e, axis=-1, keepdims=True), axis=0, keepdims=True)
        dy = e * (1.0 / D)
        dyg = dy * g_ref[...]
        dh = rstd * (dyg - xhat * jnp.mean(dyg * xhat, axis=-1, keepdims=True))
        dh_ref[...] = dh
        dhb_ref[...] = dh.astype(BF16)
        gpart = jnp.sum(dy * xhat, axis=0, keepdims=True)

        @pl.when(pl.program_id(0) == 0)
        def _():
            loss_ref[...] = lpart
            dg_ref[...] = gpart

        @pl.when(pl.program_id(0) > 0)
        def _():
            loss_ref[...] += lpart
            dg_ref[...] += gpart

    row = pl.BlockSpec((tr, D), lambda i: (i, 0))
    vec = pl.BlockSpec((1, D), lambda i: (0, 0))
    return _pallas(
        body, name="final_loss", grid=(S // tr,), in_specs=[row, vec, row],
        out_specs=[pl.BlockSpec((1, 1), lambda i: (0, 0)), row, row, vec],
        out_shape=[jax.ShapeDtypeStruct((1, 1), F32), jax.ShapeDtypeStruct((S, D), F32),
                   jax.ShapeDtypeStruct((S, D), BF16), jax.ShapeDtypeStruct((1, D), F32)],
        compiler_params=_params(("arbitrary",)),
    )(h2, g, target)


def _sigmoid(v):
    return 1.0 / (1.0 + jnp.exp(-v))


def _merge_bwd(dmerged, proj, y_ret, y_mla, D, off_gret, *, tr=256):
    S = y_ret.shape[0]
    tr = _tile(S, tr)
    b0 = off_gret // D

    def body(dm_ref, g_ref, yr_ref, ym_ref, dp_ref, dyr_ref, dym_ref):
        dm = dm_ref[...]
        sg = _sigmoid(g_ref[...])

        @pl.when(pl.program_id(1) == 0)
        def _():
            dyr_ref[...] = (dm * sg).astype(BF16)
            dp_ref[...] = (dm * yr_ref[...] * sg * (1.0 - sg)).astype(BF16)

        @pl.when(pl.program_id(1) == 1)
        def _():
            dym_ref[...] = (dm * sg).astype(BF16)
            dp_ref[...] = (dm * ym_ref[...] * sg * (1.0 - sg)).astype(BF16)

    blk = pl.BlockSpec((tr, D), lambda i, j: (i, 0))
    return _pallas(
        body, name="merge_bwd", grid=(S // tr, 2),
        in_specs=[blk, pl.BlockSpec((tr, D), lambda i, j: (i, b0 + j)), blk, blk],
        out_specs=[pl.BlockSpec((tr, D), lambda i, j: (i, b0 + j)), blk, blk],
        out_shape=[jax.ShapeDtypeStruct(proj.shape, BF16), jax.ShapeDtypeStruct((S, D), BF16),
                   jax.ShapeDtypeStruct((S, D), BF16)],
        compiler_params=_params(("parallel", "arbitrary")),
    )(dmerged, proj, y_ret, y_mla)


def _rope128(t, cos_full, sin_signed):
    return t * cos_full + pltpu.roll(t, RET_QK // 2, 1) * sin_signed


def _rope128_t(d, cos_full, sin_signed):
    return d * cos_full + pltpu.roll(d * sin_signed, RET_QK // 2, 1)


def _ret_consts(lg, T):
    pos = lax.broadcasted_iota(jnp.int32, (T, 1), 0).astype(F32)
    qd = jnp.exp(lg * (pos + 1.0))
    kd = jnp.exp(lg * (T - 1.0 - pos))
    n = lax.broadcasted_iota(jnp.int32, (T, T), 0)
    m = lax.broadcasted_iota(jnp.int32, (T, T), 1)
    vis = (m // CHUNK) <= (n // CHUNK)
    dist = jnp.abs(n - m).astype(F32)
    decay = jnp.where(vis, jnp.exp(lg * dist), 0.0)
    cdec = jnp.exp(lg * float(T))
    return qd, kd, decay, cdec


def _dot(a, b, dims):
    return lax.dot_general(a.astype(BF16), b.astype(BF16), (dims, ((), ())), preferred_element_type=F32)


NN = ((1,), (0,))
NT = ((1,), (1,))
TN = ((0,), (0,))
_RQ = slice(0, RET_QK)
_RK = slice(RET_QK, 2 * RET_QK)
_RV = slice(2 * RET_QK, 2 * RET_QK + RET_V)
_RG = slice(2 * RET_QK + RET_V, RET_HEAD_COLS)


RET_GROUP = 8


def _head_cols(h, part):
    return slice(h * RET_HEAD_COLS + part.start, h * RET_HEAD_COLS + part.stop)


def _ret_fwd(proj, cosr, sinr, lgam, gain, RH, *, T):
    S = proj.shape[0]
    nb = S // T
    G = _tile(RH, RET_GROUP)
    heads = range(G)
    scale = RET_QK ** -0.5

    def body(p_ref, cos_ref, sin_ref, lg_ref, gain_ref, ry_ref, gated_ref, st_ref, state):
        b = pl.program_id(1)

        @pl.when(b == 0)
        def _():
            state[...] = jnp.zeros_like(state)

        consts = [_ret_consts(lg_ref[h, 0:1, 0:1], T) for h in heads]
        cosv, sinv = cos_ref[...], sin_ref[...]
        q = [_rope128(p_ref[:, _head_cols(h, _RQ)], cosv, sinv) for h in heads]
        k = [_rope128(p_ref[:, _head_cols(h, _RK)], cosv, sinv) * scale for h in heads]
        v = [p_ref[:, _head_cols(h, _RV)] for h in heads]
        sprev = [state[h] for h in heads]
        for h in heads:
            st_ref[h] = sprev[h]
        a = [_dot(q[h], k[h], NT) for h in heads]
        qs = [_dot(q[h] * consts[h][0], sprev[h], NN) for h in heads]
        kv = [_dot(k[h] * consts[h][1], v[h], TN) for h in heads]
        o = [_dot(a[h] * consts[h][2], v[h], NN) + qs[h] for h in heads]
        for h in heads:
            state[h] = sprev[h] * consts[h][3] + kv[h]
            vals = slice(h * RET_V, (h + 1) * RET_V)
            ry_ref[:, vals] = o[h]
            mu = jnp.mean(o[h], axis=-1, keepdims=True)
            oc = o[h] - mu
            var = jnp.mean(oc * oc, axis=-1, keepdims=True)
            t = oc * lax.rsqrt(var + EPS) * gain_ref[:, vals]
            gv = p_ref[:, _head_cols(h, _RG)]
            gated_ref[:, vals] = (t * (gv * _sigmoid(gv))).astype(BF16)

    return _pallas(
        body, name="ret_fwd", grid=(RH // G, nb),
        in_specs=[pl.BlockSpec((T, G * RET_HEAD_COLS), lambda h, b: (b, h)),
                  pl.BlockSpec((T, RET_QK), lambda h, b: (b, 0)),
                  pl.BlockSpec((T, RET_QK), lambda h, b: (b, 0)),
                  pl.BlockSpec((G, 8, LANES), lambda h, b: (h, 0, 0)),
                  pl.BlockSpec((1, G * RET_V), lambda h, b: (0, h))],
        out_specs=[pl.BlockSpec((T, G * RET_V), lambda h, b: (b, h)),
                   pl.BlockSpec((T, G * RET_V), lambda h, b: (b, h)),
                   pl.BlockSpec((G, None, RET_QK, RET_V), lambda h, b: (h, b, 0, 0))],
        out_shape=[jax.ShapeDtypeStruct((S, RH * RET_V), F32), jax.ShapeDtypeStruct((S, RH * RET_V), BF16),
                   jax.ShapeDtypeStruct((RH, nb, RET_QK, RET_V), F32)],
        scratch_shapes=[pltpu.VMEM((G, RET_QK, RET_V), F32)],
        compiler_params=_params(("parallel", "arbitrary")),
    )(proj, cosr, sinr, lgam, gain)


def _ret_bwd(proj, cosr, sinr, lgam, gain, ry, dgated, states, dproj, RH, *, T):
    S = proj.shape[0]
    nb = S // T
    G = _tile(RH, RET_GROUP)
    heads = range(G)
    scale = RET_QK ** -0.5

    def body(p_ref, cos_ref, sin_ref, lg_ref, gain_ref, ry_ref, dg_ref, st_ref, _, dp_ref, dgain_ref, dstate):
        b = pl.program_id(1)

        @pl.when(b == 0)
        def _():
            dstate[...] = jnp.zeros_like(dstate)

        consts = [_ret_consts(lg_ref[h, 0:1, 0:1], T) for h in heads]
        qd, kd, decay, cdec = [[c[i] for c in consts] for i in range(4)]
        cosv, sinv = cos_ref[...], sin_ref[...]
        q = [_rope128(p_ref[:, _head_cols(h, _RQ)], cosv, sinv) for h in heads]
        k = [_rope128(p_ref[:, _head_cols(h, _RK)], cosv, sinv) * scale for h in heads]
        v = [p_ref[:, _head_cols(h, _RV)] for h in heads]
        sprev = [st_ref[h] for h in heads]
        ds_new = [dstate[h] for h in heads]
        a = [_dot(q[h], k[h], NT) for h in heads]
        do, gparts = [], []
        for h in heads:
            vals = slice(h * RET_V, (h + 1) * RET_V)
            o = ry_ref[:, vals]
            mu = jnp.mean(o, axis=-1, keepdims=True)
            oc = o - mu
            rstd = lax.rsqrt(jnp.mean(oc * oc, axis=-1, keepdims=True) + EPS)
            ryn = oc * rstd
            gainv = gain_ref[:, vals]
            gv = p_ref[:, _head_cols(h, _RG)]
            sg = _sigmoid(gv)
            dgt = dg_ref[:, vals]
            dt = dgt * (gv * sg)
            dp_ref[:, _head_cols(h, _RG)] = (dgt * (ryn * gainv) * (sg * (1.0 + gv * (1.0 - sg)))).astype(BF16)
            gparts.append(jnp.sum(dt * ryn, axis=0, keepdims=True))
            dryn = dt * gainv
            do.append(rstd * (dryn - jnp.mean(dryn, axis=-1, keepdims=True)
                              - ryn * jnp.mean(dryn * ryn, axis=-1, keepdims=True)))
        gpart = jnp.concatenate(gparts, axis=1)

        @pl.when(b == 0)
        def _():
            dgain_ref[...] = gpart

        @pl.when(b > 0)
        def _():
            dgain_ref[...] += gpart

        dpm = [_dot(do[h], v[h], NT) for h in heads]
        dq_s = [_dot(do[h], sprev[h], NT) for h in heads]
        dk_s = [_dot(v[h], ds_new[h], NT) for h in heads]
        dv_s = [_dot(k[h] * kd[h], ds_new[h], NN) for h in heads]
        dst = [_dot(q[h] * qd[h], do[h], TN) for h in heads]
        a = [a[h] * decay[h] for h in heads]
        dpm = [dpm[h] * decay[h] for h in heads]
        dv = [_dot(a[h], do[h], TN) + dv_s[h] for h in heads]
        dq = [_dot(dpm[h], k[h], NN) + dq_s[h] * qd[h] for h in heads]
        dk = [(_dot(dpm[h], q[h], TN) + dk_s[h] * kd[h]) * scale for h in heads]
        for h in heads:
            dstate[h] = ds_new[h] * cdec[h] + dst[h]
            dp_ref[:, _head_cols(h, _RV)] = dv[h].astype(BF16)
            dp_ref[:, _head_cols(h, _RQ)] = _rope128_t(dq[h], cosv, sinv).astype(BF16)
            dp_ref[:, _head_cols(h, _RK)] = _rope128_t(dk[h], cosv, sinv).astype(BF16)

    rb = lambda b: nb - 1 - b
    return _pallas(
        body, name="ret_bwd", grid=(RH // G, nb),
        in_specs=[pl.BlockSpec((T, G * RET_HEAD_COLS), lambda h, b: (rb(b), h)),
                  pl.BlockSpec((T, RET_QK), lambda h, b: (rb(b), 0)),
                  pl.BlockSpec((T, RET_QK), lambda h, b: (rb(b), 0)),
                  pl.BlockSpec((G, 8, LANES), lambda h, b: (h, 0, 0)),
                  pl.BlockSpec((1, G * RET_V), lambda h, b: (0, h)),
                  pl.BlockSpec((T, G * RET_V), lambda h, b: (rb(b), h)),
                  pl.BlockSpec((T, G * RET_V), lambda h, b: (rb(b), h)),
                  pl.BlockSpec((G, None, RET_QK, RET_V), lambda h, b: (h, rb(b), 0, 0)),
                  _ANY],
        out_specs=[pl.BlockSpec((T, G * RET_HEAD_COLS), lambda h, b: (rb(b), h)),
                   pl.BlockSpec((1, G * RET_V), lambda h, b: (0, h))],
        out_shape=[jax.ShapeDtypeStruct(dproj.shape, dproj.dtype), jax.ShapeDtypeStruct((1, RH * RET_V), F32)],
        scratch_shapes=[pltpu.VMEM((G, RET_QK, RET_V), F32)],
        input_output_aliases={8: 0},
        compiler_params=_params(("parallel", "arbitrary")),
    )(proj, cosr, sinr, lgam, gain, ry, dgated, states, dproj)


def _rope_pe(t, c, s1, s2):
    return t * c + pltpu.roll(t, LANES - QK_ROPE // 2, 1) * s1 + pltpu.roll(t, QK_ROPE // 2, 1) * s2


def _rope_pe_t(d, c, s1, s2):
    return d * c + pltpu.roll(d * s1, QK_ROPE // 2, 1) + pltpu.roll(d * s2, LANES - QK_ROPE // 2, 1)


ATTN_C2 = (QK_NOPE + QK_ROPE) ** -0.5 * LOG2E


def _qkv_proj(cqn, ckvn, wq, wkv, kpe, tabs, MH, *, tm=512, heads=4):
    S = cqn.shape[0]
    tm = _tile(S, tm)
    hb = _tile(MH, heads)
    W = 2 * LANES
    c_t, s1_t, s2_t = tabs

    def body(cq_ref, ckv_ref, wq_ref, wkv_ref, kpe_ref, c_ref, s1_ref, s2_ref, qf_ref, kf_ref, v_ref):
        c, s1, s2 = c_ref[...], s1_ref[...], s2_ref[...]
        q = _dot(cq_ref[...], wq_ref[...], NN)
        kv = _dot(ckv_ref[...], wkv_ref[...], NN)
        kper = _rope_pe(kpe_ref[...], c, s1, s2).astype(BF16)
        for h in range(hb):
            lo, mid, hi = h * W, h * W + QK_NOPE, (h + 1) * W
            qf_ref[:, lo:mid] = (q[:, lo:mid] * ATTN_C2).astype(BF16)
            qf_ref[:, mid:hi] = (_rope_pe(q[:, mid:hi], c, s1, s2) * ATTN_C2).astype(BF16)
            kf_ref[:, lo:mid] = kv[:, lo:mid].astype(BF16)
            kf_ref[:, mid:hi] = kper
            v_ref[:, h * V_HEAD:(h + 1) * V_HEAD] = kv[:, mid:hi].astype(BF16)

    tab = pl.BlockSpec((tm, LANES), lambda i, j: (i, 0))
    grp = pl.BlockSpec((tm, hb * W), lambda i, j: (i, j))
    return _pallas(
        body, name="qkv_proj", grid=(S // tm, MH // hb),
        in_specs=[pl.BlockSpec((tm, cqn.shape[1]), lambda i, j: (i, 0)),
                  pl.BlockSpec((tm, ckvn.shape[1]), lambda i, j: (i, 0)),
                  pl.BlockSpec((wq.shape[0], hb * W), lambda i, j: (0, j)),
                  pl.BlockSpec((wkv.shape[0], hb * W), lambda i, j: (0, j)), tab, tab, tab, tab],
        out_specs=[grp, grp, pl.BlockSpec((tm, hb * V_HEAD), lambda i, j: (i, j))],
        out_shape=[jax.ShapeDtypeStruct((S, MH * W), BF16)] * 2 + [jax.ShapeDtypeStruct((S, MH * V_HEAD), BF16)],
        compiler_params=_params(("parallel", "parallel")),
    )(cqn, ckvn, wq, wkv, kpe, c_t, s1_t, s2_t)


def _chunk_mask(T):
    n = lax.broadcasted_iota(jnp.int32, (T, T), 0)
    m = lax.broadcasted_iota(jnp.int32, (T, T), 1)
    return (m // CHUNK) <= (n // CHUNK)


def _lanes_to(v, width):
    return jnp.tile(v, (1, width // LANES))


def _attn_fwd(qf, kf, vb, MH, *, T):
    S = qf.shape[0]
    nt = S // T

    def body(q_ref, k_ref, v_ref, o_ref, ob_ref, lse_ref, m_sc, l_sc, acc_sc, s_a, s_b):
        qi = pl.program_id(1)
        m_sc[...] = jnp.full_like(m_sc, NEG)
        l_sc[...] = jnp.zeros_like(l_sc)
        acc_sc[...] = jnp.zeros_like(acc_sc)

        def rows_of(kt):
            return pl.ds(pl.multiple_of(kt * T, T), T)

        def scores(kt):
            return _dot(q_ref[...], k_ref[rows_of(kt), :], NT)

        def update(s, kt):
            m_prev = m_sc[...]
            m_new = jnp.maximum(m_prev, jnp.max(s, axis=-1, keepdims=True))
            alpha = jnp.exp2(m_prev - m_new)
            p = jnp.exp2(s - _lanes_to(m_new, T))
            l_sc[...] = alpha * l_sc[...] + jnp.sum(p, axis=-1, keepdims=True)
            acc_sc[...] = alpha * acc_sc[...] + _dot(p, v_ref[rows_of(kt), :], NN)
            m_sc[...] = m_new

        def masked(s):
            return jnp.where(_chunk_mask(T), s, NEG)

        @pl.when(qi == 0)
        def _():
            update(masked(scores(0)), 0)

        @pl.when(qi > 0)
        def _():
            s_a[...] = masked(scores(qi))
            s_b[...] = scores(0)
            update(s_a[...], qi)
            s_a[...] = scores(jnp.minimum(1, qi - 1))
            update(s_b[...], 0)

            def pair(j, carry):
                s_b[...] = scores(2 * j)
                update(s_a[...], 2 * j - 1)
                s_a[...] = scores(jnp.minimum(2 * j + 1, qi - 1))
                update(s_b[...], 2 * j)
                return carry

            lax.fori_loop(1, (qi + 1) // 2, pair, 0)

            @pl.when(qi % 2 == 0)
            def _():
                update(s_a[...], qi - 1)
        l = l_sc[...]
        o = acc_sc[...] / l
        o_ref[...] = o
        ob_ref[...] = o.astype(BF16)
        lse_ref[...] = m_sc[...] + jnp.log(l) * LOG2E

    return _pallas(
        body, name="attn_fwd", grid=(MH, nt),
        in_specs=[pl.BlockSpec((T, 2 * LANES), lambda h, i: (i, h)),
                  pl.BlockSpec((S, 2 * LANES), lambda h, i: (0, h)),
                  pl.BlockSpec((S, LANES), lambda h, i: (0, h))],
        out_specs=[pl.BlockSpec((T, LANES), lambda h, i: (i, h)), pl.BlockSpec((T, LANES), lambda h, i: (i, h)),
                   pl.BlockSpec((None, T, LANES), lambda h, i: (h, i, 0))],
        out_shape=[jax.ShapeDtypeStruct((S, MH * LANES), F32), jax.ShapeDtypeStruct((S, MH * LANES), BF16),
                   jax.ShapeDtypeStruct((MH, S, LANES), F32)],
        scratch_shapes=[pltpu.VMEM((T, LANES), F32), pltpu.VMEM((T, LANES), F32), pltpu.VMEM((T, LANES), F32),
                        pltpu.VMEM((T, T), F32), pltpu.VMEM((T, T), F32)],
        compiler_params=_params(("parallel", "parallel")),
    )(qf, kf, vb)


def _attn_bwd(qf, kf, vb, dob, lse2, delta, tabs, MH, *, T, deps=()):
    S = qf.shape[0]
    nt = S // T
    scale = (QK_NOPE + QK_ROPE) ** -0.5
    n_dep = len(deps)

    def body(q_ref, k_ref, v_ref, do_ref, lse_ref, dl_ref, c_ref, s1_ref, s2_ref, *rest):
        dqa_ref, dkv_ref, dkpe_ref, dq_ref, dk_sc, dv_sc, s_a, dp_a, s_b, dp_b = rest[n_dep:]
        kj = pl.program_id(1)

        @pl.when(kj == 0)
        def _():
            dq_ref[...] = jnp.zeros_like(dq_ref)

        dk_sc[...] = jnp.zeros_like(dk_sc)
        dv_sc[...] = jnp.zeros_like(dv_sc)

        def rows_of(qt):
            return pl.ds(pl.multiple_of(qt * T, T), T)

        def products(qt):
            rows = rows_of(qt)
            return _dot(q_ref[rows, :], k_ref[...], NT), _dot(do_ref[rows, :], v_ref[...], NT)

        def update(s, dp, qt):
            rows = rows_of(qt)
            q, dov = q_ref[rows, :], do_ref[rows, :]
            p = jnp.exp2(s - _lanes_to(lse_ref[rows, :], T))
            ds = p * (dp - _lanes_to(dl_ref[rows, :], T))
            dv_sc[...] += _dot(p, dov, TN)
            dk_sc[...] += _dot(ds, q, TN)
            dq_ref[rows, :] += _dot(ds, k_ref[...], NN)

        def masked(s):
            return jnp.where(_chunk_mask(T), s, NEG)

        @pl.when(kj == nt - 1)
        def _():
            s, dp = products(kj)
            update(masked(s), dp, kj)

        @pl.when(kj < nt - 1)
        def _():
            s, dp = products(kj)
            s_a[...], dp_a[...] = masked(s), dp
            s_b[...], dp_b[...] = products(kj + 1)
            update(s_a[...], dp_a[...], kj)
            s_a[...], dp_a[...] = products(jnp.minimum(kj + 2, nt - 1))
            update(s_b[...], dp_b[...], kj + 1)

            def pair(j, carry):
                t0 = kj + 2 * j
                s_b[...], dp_b[...] = products(t0 + 1)
                update(s_a[...], dp_a[...], t0)
                s_a[...], dp_a[...] = products(jnp.minimum(t0 + 2, nt - 1))
                update(s_b[...], dp_b[...], t0 + 1)
                return carry

            lax.fori_loop(1, (nt - kj) // 2, pair, 0)

            @pl.when((nt - kj) % 2 == 1)
            def _():
                update(s_a[...], dp_a[...], nt - 1)
        dkv_ref[:, :QK_NOPE] = (dk_sc[:, :QK_NOPE] * (1.0 / LOG2E)).astype(BF16)
        dkv_ref[:, QK_NOPE:] = dv_sc[...].astype(BF16)
        dkpe_ref[...] = dk_sc[:, QK_NOPE:] * (1.0 / LOG2E)

        @pl.when(kj == nt - 1)
        def _():
            dqa_ref[:, :QK_NOPE] = (dq_ref[:, :QK_NOPE] * scale).astype(BF16)
            dqa_ref[:, QK_NOPE:] = (_rope_pe_t(dq_ref[:, QK_NOPE:], c_ref[...], s1_ref[...], s2_ref[...])
                                    * scale).astype(BF16)

    stat = pl.BlockSpec((None, S, LANES), lambda h, j: (h, 0, 0))
    tab = pl.BlockSpec((S, LANES), lambda h, j: (0, 0))
    return _pallas(
        body, name="attn_bwd", grid=(MH, nt),
        in_specs=[pl.BlockSpec((S, 2 * LANES), lambda h, j: (0, h)),
                  pl.BlockSpec((T, 2 * LANES), lambda h, j: (j, h)),
                  pl.BlockSpec((T, LANES), lambda h, j: (j, h)),
                  pl.BlockSpec((S, LANES), lambda h, j: (0, h)), stat, stat, tab, tab, tab] + [_ANY] * n_dep,
        out_specs=[pl.BlockSpec((S, 2 * LANES), lambda h, j: (0, h)),
                   pl.BlockSpec((T, 2 * LANES), lambda h, j: (j, h)),
                   pl.BlockSpec((T, LANES), lambda h, j: (j, h))],
        out_shape=[jax.ShapeDtypeStruct((S, MH * 2 * LANES), BF16), jax.ShapeDtypeStruct((S, MH * 2 * LANES), BF16),
                   jax.ShapeDtypeStruct((S, MH * LANES), F32)],
        scratch_shapes=[pltpu.VMEM((S, 2 * LANES), F32), pltpu.VMEM((T, 2 * LANES), F32), pltpu.VMEM((T, LANES), F32)]
        + [pltpu.VMEM((T, T), F32)] * 4,
        compiler_params=_params(("parallel", "arbitrary")),
    )(qf, kf, vb, dob, lse2, delta, *tabs, *deps)


def _kpe_sum(dkpe_h, tabs, MH, *, tr=256):
    S = dkpe_h.shape[0]
    tr = _tile(S, tr)

    def body(dk_ref, c_ref, s1_ref, s2_ref, dkpe_ref):
        tot = dk_ref[:, :LANES]
        for h in range(1, MH):
            tot = tot + dk_ref[:, h * LANES:(h + 1) * LANES]
        dkpe_ref[...] = _rope_pe_t(tot, c_ref[...], s1_ref[...], s2_ref[...]).astype(BF16)

    tab = pl.BlockSpec((tr, LANES), lambda i: (i, 0))
    return _pallas(
        body, name="kpe_sum", grid=(S // tr,),
        in_specs=[pl.BlockSpec((tr, MH * LANES), lambda i: (i, 0)), tab, tab, tab],
        out_specs=tab, out_shape=jax.ShapeDtypeStruct((S, LANES), BF16),
        compiler_params=_params(("parallel",)),
    )(dkpe_h, *tabs)


ROW_ALIGN = 16


def _blk(R, C, block_bytes=2 << 20):
    cap = max(ROW_ALIGN, block_bytes // (C * 4))
    for t in range(min(R, cap) // ROW_ALIGN * ROW_ALIGN, LANES - 1, -ROW_ALIGN):
        if R % t == 0:
            return t, C
    if R <= cap:
        return R, C
    tc = C
    while R * tc * 4 > block_bytes and tc % (2 * LANES) == 0:
        tc //= 2
    return R, tc


def _rows_call(fn, ins, out_dtypes, *, name):
    R, C = ins[0].shape
    tr, tc = _blk(R, C)
    n_in = len(ins)

    def body(*refs):
        vals = fn(*[r[...] for r in refs[:n_in]])
        for r, v in zip(refs[n_in:], vals):
            r[...] = v.astype(r.dtype)

    blk = pl.BlockSpec((tr, tc), lambda i, j: (i, j))
    res = _pallas(
        body, name=name, grid=(R // tr, C // tc), in_specs=[blk] * n_in, out_specs=[blk] * len(out_dtypes),
        out_shape=[jax.ShapeDtypeStruct((R, C), d) for d in out_dtypes],
        compiler_params=_params(("parallel", "parallel")),
    )(*ins)
    return res


def _adamw_vals(w, g, m, v):
    m = ADAM_B1 * m + (1.0 - ADAM_B1) * g
    v = ADAM_B2 * v + (1.0 - ADAM_B2) * (g * g)
    m_hat = m / (1.0 - ADAM_B1 ** ADAM_STEP)
    v_hat = v / (1.0 - ADAM_B2 ** ADAM_STEP)
    delta = -ADAM_LR * (m_hat / (jnp.sqrt(v_hat) + ADAM_EPS) + ADAM_WD * w)
    return delta, m, v


def _sum_pair(p, theirs, place, *, name):
    _, R, C = p.shape
    R2 = R // 2
    tr, tc = _blk(R2, C)
    p4 = p.reshape(N_CHIPS, 2, R2, C)

    def body(place_ref, a_ref, b_ref, o_ref):
        o_ref[...] = (a_ref[...].astype(F32) + b_ref[...].astype(F32)).astype(BF16)

    spec = pltpu.PrefetchScalarGridSpec(
        num_scalar_prefetch=1, grid=(N_CHIPS, R2 // tr, C // tc),
        in_specs=[pl.BlockSpec((None, None, tr, tc), lambda q, i, j, pr: (q, pr[0], i, j)),
                  pl.BlockSpec((None, tr, tc), lambda q, i, j, pr: (q, i, j))],
        out_specs=pl.BlockSpec((None, tr, tc), lambda q, i, j, pr: (q, i, j)))
    return _pallas(body, name=name, grid_spec=spec, out_shape=jax.ShapeDtypeStruct((N_CHIPS, R2, C), BF16),
                   compiler_params=_params(("parallel", "parallel", "parallel")))(place, p4, theirs)


def _sum_chips(p, theirs, recv, place, *, name):
    _, R, C = p.shape
    R2 = R // 2
    tr, tc = _blk(R2, C)
    p4 = p.reshape(N_CHIPS, 2, R2, C)

    def body(place_ref, a_ref, b_ref, r0_ref, r1_ref, r2_ref, o_ref):
        own = a_ref[...].astype(F32) + b_ref[...].astype(F32)
        o_ref[...] = ((own + r0_ref[...].astype(F32)) + r1_ref[...].astype(F32)) + r2_ref[...].astype(F32)

    def slot(k):
        return pl.BlockSpec((None, tr, tc), lambda i, j, pr: (k, i, j))

    spec = pltpu.PrefetchScalarGridSpec(
        num_scalar_prefetch=1, grid=(R2 // tr, C // tc),
        in_specs=[pl.BlockSpec((None, None, tr, tc), lambda i, j, pr: (pr[1], pr[0], i, j)),
                  pl.BlockSpec((None, tr, tc), lambda i, j, pr: (pr[1], i, j)), slot(0), slot(1), slot(2)],
        out_specs=pl.BlockSpec((None, tr, tc), lambda i, j, pr: (pr[0], i, j)))
    return _pallas(body, name=name, grid_spec=spec, out_shape=jax.ShapeDtypeStruct((2, R2, C), F32),
                   compiler_params=_params(("parallel", "parallel")))(place, p4, theirs, recv, recv, recv)


def _me():
    return lax.axis_index("x"), lax.axis_index("y"), lax.axis_index("c")


def _other_chips(x, y):
    return [(1 - x, y), (x, 1 - y), (1 - x, 1 - y)]


def _rcopy(src, dst, ssem, rsem, dev):
    return pltpu.make_async_remote_copy(src_ref=src, dst_ref=dst, send_sem=ssem, recv_sem=rsem,
                                        device_id=dev, device_id_type=MESH)


def _cast_into_slot(w, place, *, name, rows=None, deps=()):
    R, C = w.shape
    rows = R if rows is None else rows
    tr, tc = _blk(R, C)

    def body(place_ref, w_ref, *rest):
        rest[-1][...] = w_ref[...].astype(BF16)

    spec = pltpu.PrefetchScalarGridSpec(
        num_scalar_prefetch=1, grid=(R // tr, C // tc),
        in_specs=[pl.BlockSpec((tr, tc), lambda i, j, pr: (i, j))] + [_ANY] * len(deps),
        out_specs=pl.BlockSpec((None, tr, tc), lambda i, j, pr: (pr[1], i, j)))
    out = _pallas(body, name=name, grid_spec=spec, out_shape=jax.ShapeDtypeStruct((N_CHIPS, rows, C), BF16),
                  compiler_params=_params(("parallel", "parallel")))(place, w, *deps)
    return out.reshape(N_CHIPS, 2, rows // 2, C)


def _gather_ici_plan(bufs):
    x, y, c = _me()
    j = 2 * x + y
    plan = []
    for i, buf in enumerate(bufs):
        for k, (px, py) in enumerate(_other_chips(x, y)):
            plan.append((3 * i + k, buf.at[j, c], buf.at[j, c], (px, py, c)))
    return plan


def _forward_halves(bufs, *, name):
    n = len(bufs)

    def body(*refs):
        outs = refs[n:2 * n]
        ssem, rsem = refs[2 * n:]
        x, y, c = _me()
        sib = (x, y, 1 - c)
        cps = []
        for i in range(n):
            for k, (px, py) in enumerate(_other_chips(x, y)):
                slot = outs[i].at[2 * px + py, c]
                r = _rcopy(slot, slot, ssem.at[3 * i + k], rsem.at[3 * i + k], sib)
                r.start()
                cps.append(r)
        for r in cps:
            r.wait()

    return _pallas(
        body, name=name, in_specs=[_ANY] * n, out_specs=[_ANY] * n,
        out_shape=[jax.ShapeDtypeStruct(b.shape, b.dtype) for b in bufs],
        scratch_shapes=[pltpu.SemaphoreType.DMA((3 * n,))] * 2,
        input_output_aliases={i: i for i in range(n)},
        compiler_params=pltpu.CompilerParams(has_side_effects=True),
    )(*bufs)


_HBM = pl.BlockSpec(memory_space=pltpu.HBM)
_SEM = pl.BlockSpec(memory_space=pltpu.SEMAPHORE)
_EFFECT = pltpu.SideEffectType.DATAFLOW_SIDE_EFFECTING


def _split_start(bufs, plan, n_copies, *, name):
    n = len(bufs)

    def body(*refs):
        ssem, rsem = refs[n], refs[n + 1]
        for s, src, dst, dev in plan(refs[:n]):
            _rcopy(src, dst, ssem.at[s], rsem.at[s], dev).start()
        refs[-1][...] = jnp.zeros_like(refs[-1])

    res = _pallas(
        body, name=name, in_specs=[_HBM] * n,
        out_specs=(_SEM, _SEM, *[_HBM] * n, pl.BlockSpec(memory_space=pltpu.VMEM)),
        out_shape=(pltpu.SemaphoreType.DMA((n_copies,)), pltpu.SemaphoreType.DMA((n_copies,)),
                   *[pltpu.HBM(b.shape, b.dtype) for b in bufs], jax.ShapeDtypeStruct((8, LANES), F32)),
        input_output_aliases={i: 2 + i for i in range(n)},
        compiler_params=pltpu.CompilerParams(has_side_effects=_EFFECT),
    )(*[pltpu.with_memory_space_constraint(b, pltpu.HBM) for b in bufs])
    return res[0], res[1], list(res[2:2 + n]), res[-1]


def _split_wait(ssem, rsem, bufs, after, plan, *, name):
    n = len(bufs)

    def body(*refs):
        ssem_ref, rsem_ref = refs[n], refs[n + 1]
        for s, src, dst, dev in plan(refs[:n]):
            cp = _rcopy(src, dst, ssem_ref.at[s], rsem_ref.at[s], dev)
            cp.wait_send()
            cp.wait_recv()

    return list(_pallas(
        body, name=name, in_specs=[_HBM] * n + [_SEM, _SEM, _ANY], out_specs=[_HBM] * n,
        out_shape=[pltpu.HBM(b.shape, b.dtype) for b in bufs],
        input_output_aliases={i: i for i in range(n)},
        compiler_params=pltpu.CompilerParams(has_side_effects=_EFFECT),
    )(*bufs, ssem, rsem, after))


def _swap_plan(n):
    def plan(bufs):
        x, y, c = _me()
        return [(i, bufs[i].at[:, 1 - c], bufs[n + i], (x, y, 1 - c)) for i in range(n)]
    return plan


def _scatter_plan(n):
    def plan(bufs):
        x, y, c = _me()
        out = []
        for i in range(n):
            for k, (px, py) in enumerate(_other_chips(x, y)):
                out.append((3 * i + k, bufs[i].at[2 * px + py], bufs[n + i].at[k], (px, py, c)))
        return out
    return plan


def _swap_halves(grads, *, name):
    n = len(grads)
    views = [g.reshape(N_CHIPS, 2, g.shape[1] // 2, g.shape[2]) for g in grads]

    def body(*refs):
        ins, outs = refs[:n], refs[n:2 * n]
        ssem, rsem = refs[2 * n:]
        x, y, c = _me()
        sib = (x, y, 1 - c)
        cps = []
        for i in range(n):
            r = _rcopy(ins[i].at[:, 1 - c], outs[i], ssem.at[i], rsem.at[i], sib)
            r.start()
            cps.append(r)
        for r in cps:
            r.wait()

    return _pallas(
        body, name=name, in_specs=[_ANY] * n, out_specs=[_ANY] * n,
        out_shape=[jax.ShapeDtypeStruct((N_CHIPS,) + v.shape[2:], v.dtype) for v in views],
        scratch_shapes=[pltpu.SemaphoreType.DMA((n,)), pltpu.SemaphoreType.DMA((n,))],
        compiler_params=pltpu.CompilerParams(has_side_effects=True),
    )(*views)


def _join_halves(halves, *, name):
    n = len(halves)

    def body(*refs):
        outs = refs[n:2 * n]
        ssem, rsem = refs[2 * n:]
        x, y, c = _me()
        sib = (x, y, 1 - c)
        cps = []
        for i in range(n):
            r = _rcopy(outs[i].at[c], outs[i].at[c], ssem.at[i], rsem.at[i], sib)
            r.start()
            cps.append(r)
        for r in cps:
            r.wait()

    return _pallas(
        body, name=name, in_specs=[_ANY] * n, out_specs=[_ANY] * n,
        out_shape=[jax.ShapeDtypeStruct(h.shape, h.dtype) for h in halves],
        scratch_shapes=[pltpu.SemaphoreType.DMA((n,)), pltpu.SemaphoreType.DMA((n,))],
        input_output_aliases={i: i for i in range(n)},
        compiler_params=pltpu.CompilerParams(has_side_effects=True),
    )(*halves)


def _allreduce_small(parts, loss11):
    n = len(parts)
    widths = [p.shape[1] for p in parts]
    total = sum(widths) + LANES

    def body(*refs):
        o_ref, mine, buf, ssem, rsem = refs[n + 1:]
        x, y, c = _me()
        me = 4 * x + 2 * y + c
        off = 0
        for r, w in zip(refs[:n], widths):
            mine[:, off:off + w] = r[...]
            off += w
        mine[:, off:] = jnp.broadcast_to(refs[n][...], (1, LANES))
        buf[me] = mine[...]
        cps = []
        for k in range(1, 8):
            peer = (x ^ (k >> 2), y ^ ((k >> 1) & 1), c ^ (k & 1))
            r = _rcopy(mine, buf.at[me], ssem.at[k - 1], rsem.at[k - 1], peer)
            r.start()
            cps.append(r)
        for k in range(1, 8):
            peer = (x ^ (k >> 2), y ^ ((k >> 1) & 1), c ^ (k & 1))
            pid = 4 * peer[0] + 2 * peer[1] + peer[2]
            _rcopy(mine, buf.at[pid], ssem.at[k - 1], rsem.at[k - 1], peer).wait_recv()
        for r in cps:
            r.wait_send()
        tot = buf[0]
        for d in range(1, 8):
            tot = tot + buf[d]
        o_ref[...] = tot

    vm = pl.BlockSpec(memory_space=pltpu.VMEM)
    return _pallas(
        body, name="allreduce_small", in_specs=[vm] * (n + 1), out_specs=vm,
        out_shape=jax.ShapeDtypeStruct((1, total), F32),
        scratch_shapes=[pltpu.VMEM((1, total), F32), pltpu.VMEM((8, 1, total), F32),
                        pltpu.SemaphoreType.DMA((7,)), pltpu.SemaphoreType.DMA((7,))],
        compiler_params=pltpu.CompilerParams(has_side_effects=True),
    )(*parts, loss11)


def _adamw_small(red, ws, ms, vs):
    n = len(ws)

    def body(*refs):
        red_ref = refs[0]
        outs = refs[1 + 3 * n:]
        off = 0
        for i in range(n):
            w = refs[1 + i].shape[1]
            g = red_ref[:, off:off + w]
            d, m, v = _adamw_vals(refs[1 + i][...], g, refs[1 + n + i][...], refs[1 + 2 * n + i][...])
            for o, val in zip(outs[4 * i:4 * i + 4], (g, d, m, v)):
                o[...] = val
            off += w

    vm = pl.BlockSpec(memory_space=pltpu.VMEM)
    res = _pallas(
        body, name="adamw_small", in_specs=[vm] * (1 + 3 * n), out_specs=[vm] * (4 * n),
        out_shape=[jax.ShapeDtypeStruct(w.shape, F32) for w in ws for _ in range(4)],
    )(red, *ws, *ms, *vs)
    return [res[4 * i:4 * i + 4] for i in range(n)]


def _rope_tables(positions, S):
    pos = positions.reshape(S, 1).astype(F32)
    half = RET_QK // 2
    inv = ROPE_THETA ** (-jnp.arange(half, dtype=F32) / half)
    ang = pos * inv
    cosr = jnp.concatenate([jnp.cos(ang), jnp.cos(ang)], axis=1)
    sinr = jnp.concatenate([-jnp.sin(ang), jnp.sin(ang)], axis=1)
    half = QK_ROPE // 2
    inv = ROPE_THETA ** (-jnp.arange(half, dtype=F32) / half)
    ang = pos * inv
    z = jnp.zeros((S, half), F32)
    c = jnp.concatenate([jnp.cos(ang), jnp.cos(ang), z, z], axis=1)
    s1 = jnp.concatenate([-jnp.sin(ang), z, z, z], axis=1)
    s2 = jnp.concatenate([z, jnp.sin(ang), z, z], axis=1)
    return cosr, sinr, (c, s1, s2)


def _cat_cols(g):
    return jnp.concatenate([g[j] for j in range(N_CHIPS)], axis=1)


def _split_cols(w):
    return jnp.stack(jnp.split(w, N_CHIPS, axis=1))


def kernel(x, positions, norm_mix_g, w_in, ret_norm_g, w_ret_o, q_a_norm_g, w_q_b, kv_a_norm_g, w_kv_b, w_mla_o, w_out, norm_mlp_g, w_up, w_down, norm_f_g, loss_target, m_norm_mix_g, m_w_in, m_ret_norm_g, m_w_ret_o, m_q_a_norm_g, m_w_q_b, m_kv_a_norm_g, m_w_kv_b, m_w_mla_o, m_w_out, m_norm_mlp_g, m_w_up, m_w_down, m_norm_f_g, v_norm_mix_g, v_w_in, v_ret_norm_g, v_w_ret_o, v_q_a_norm_g, v_w_q_b, v_kv_a_norm_g, v_w_kv_b, v_w_mla_o, v_w_out, v_norm_mlp_g, v_w_up, v_w_down, v_norm_f_g):
    S, D = x.shape[1], x.shape[2]
    RVW = w_ret_o.shape[1] * N_CHIPS
    RH = RVW // RET_V
    RQW = RH * RET_QK
    MVW = w_mla_o.shape[1] * N_CHIPS
    MH = MVW // V_HEAD
    QL, KVL = w_q_b.shape[1], w_kv_b.shape[1]
    T_RET = _tile(S, 256)
    T_ATT = _tile(S, 512)

    xs = x.reshape(S, D)
    tgt = loss_target.reshape(S, D)
    cosr, sinr, pe_tabs = _rope_tables(positions, S)
    lgam = jnp.log(1.0 - 2.0 ** (-5.0 - jnp.arange(RH, dtype=F32)))
    lgam = jnp.broadcast_to(lgam[:, None, None], (RH, 8, LANES))

    big = ("w_in", "w_ret_o", "w_q_b", "w_kv_b", "w_mla_o", "w_out", "w_up", "w_down")
    w_sh = dict(w_in=w_in[0].T, w_ret_o=w_ret_o[0], w_q_b=w_q_b[0], w_kv_b=w_kv_b[0], w_mla_o=w_mla_o[0],
                w_out=w_out[0], w_up=w_up[0], w_down=w_down[0])
    m_sh = dict(w_in=m_w_in[0].T, w_ret_o=m_w_ret_o[0], w_q_b=m_w_q_b[0], w_kv_b=m_w_kv_b[0],
                w_mla_o=m_w_mla_o[0], w_out=m_w_out[0], w_up=m_w_up[0], w_down=m_w_down[0])
    v_sh = dict(w_in=v_w_in[0].T, w_ret_o=v_w_ret_o[0], w_q_b=v_w_q_b[0], w_kv_b=v_w_kv_b[0],
                w_mla_o=v_w_mla_o[0], w_out=v_w_out[0], w_up=v_w_up[0], w_down=v_w_down[0])
    col_sharded = ("w_q_b", "w_kv_b", "w_up")
    c_sh = w_in.shape[2]
    c_pad = -(-c_sh // 64) * 64
    place = jnp.stack([lax.axis_index("c"), 2 * lax.axis_index("x") + lax.axis_index("y")]).astype(jnp.int32)

    def whole(k, g):
        g = g.reshape(N_CHIPS, w_sh[k].shape[0], w_sh[k].shape[1])
        if k == "w_up":
            return g
        return _cat_cols(g) if k in col_sharded else g.reshape(-1, g.shape[2])

    first = ("w_in", "w_q_b", "w_kv_b")
    later = ("w_ret_o", "w_mla_o", "w_out", "w_up", "w_down")
    first_bufs = [_cast_into_slot(w_sh[k], place, name="cast_" + k, rows=c_pad if k == "w_in" else None)
                  for k in first]
    first_ssem, first_rsem, first_bufs, first_token = _split_start(
        first_bufs, _gather_ici_plan, 3 * len(first), name="gather_first_start")
    later_bufs = [_cast_into_slot(w_sh[k], place, name="cast_" + k, deps=(first_token,)) for k in later[:-1]]
    first_bufs = _split_wait(first_ssem, first_rsem, first_bufs, later_bufs[-1], _gather_ici_plan,
                             name="gather_first_wait")
    got = _forward_halves(first_bufs, name="gather_first_forward")
    full = {k: whole(k, g) for k, g in zip(first[1:], got[1:])}
    later_bufs.append(_cast_into_slot(w_sh[later[-1]], place, name="cast_" + later[-1], deps=(got[0],)))
    later_ssem, later_rsem, later_bufs, later_token = _split_start(
        later_bufs, _gather_ici_plan, 3 * len(later), name="gather_later_start")

    o_rq, o_rk, o_rv, o_rg = 0, RQW, 2 * RQW, 2 * RQW + RVW
    o_cq = 2 * RQW + 2 * RVW
    o_ckv, o_kpe = o_cq + QL, o_cq + QL + KVL
    o_gr = o_kpe + QK_ROPE
    o_gm = o_gr + D
    n_ret = RH * RET_HEAD_COLS
    off_gret, off_gmla, off_cq, off_ckv = n_ret, n_ret + D, n_ret + 2 * D, n_ret + 2 * D + QL
    n_a = off_ckv + KVL
    runs = []
    for h in range(RH):
        base = h * RET_HEAD_COLS
        runs += [(o_rq + h * RET_QK, RET_QK, base), (o_rk + h * RET_QK, RET_QK, base + RET_QK),
                 (o_rv + h * RET_V, RET_V, base + 2 * RET_QK), (o_rg + h * RET_V, RET_V, base + 2 * RET_QK + RET_V)]
    runs += [(o_gr, D, off_gret), (o_gm, D, off_gmla), (o_cq, QL, off_cq), (o_ckv, KVL, off_ckv),
             (o_kpe, QK_ROPE, n_a)]

    def take(parts, start, width):
        out, lo = [], 0
        for p in parts:
            hi = lo + p.shape[0]
            a, b = max(start, lo), min(start + width, hi)
            if a < b:
                out.append(p[a - lo:b - lo])
            lo = hi
        return out

    wi = [got[0].reshape(N_CHIPS, c_pad, D)[jj, :c_sh] for jj in range(N_CHIPS)]
    here = sorted(runs, key=lambda r: r[2])
    wa = jnp.concatenate([p for s0, w, _ in here[:-1] for p in take(wi, s0, w)], axis=0)
    wkpe = jnp.concatenate(take(wi, o_kpe, QK_ROPE) + [jnp.zeros((LANES - QK_ROPE, D), BF16)], axis=0)
    wq = jnp.pad(full["w_q_b"].reshape(QL, MH, QK_NOPE + QK_ROPE),
                 ((0, 0), (0, 0), (0, LANES - QK_ROPE))).reshape(QL, MH * 2 * LANES)
    wkv = full["w_kv_b"]

    u, rstd0 = _rmsnorm_fwd(xs, norm_mix_g, name="norm_mix")
    proj = _mm(u, wa, mode="nt", outs=[F32], name="in_proj", deps=(later_token,))
    kpe = _mm(u, wkpe, mode="nt", outs=[F32], name="kpe_proj")
    ry, gated, states = _ret_fwd(proj, cosr, sinr, lgam, ret_norm_g, RH, T=T_RET)
    cqn, rstd_q = _rmsnorm_fwd(proj, q_a_norm_g, name="norm_q", width=QL, col=off_cq // QL)
    ckvn, rstd_kv = _rmsnorm_fwd(proj, kv_a_norm_g, name="norm_kv", width=KVL, col=off_ckv // KVL)
    qf, kf, vb = _qkv_proj(cqn, ckvn, wq, wkv, kpe, pe_tabs, MH)
    my, my_b, lse2 = _attn_fwd(qf, kf, vb, MH, T=T_ATT)
    later_bufs = _split_wait(later_ssem, later_rsem, later_bufs, my, _gather_ici_plan, name="gather_later_wait")
    later_bufs = _forward_halves(later_bufs, name="gather_later_forward")
    full.update({k: whole(k, g) for k, g in zip(later, later_bufs)})
    y_ret = _mm(gated, full["w_ret_o"], mode="nn", outs=[BF16], name="ret_o")
    y_mla, merged = _mm(my_b, full["w_mla_o"], mode="nn", outs=[BF16, BF16], name="mla_o",
                        epi=lambda acc, gr, gm, yr: (acc, _sigmoid(gr) * yr + _sigmoid(gm) * acc),
                        extras=((proj, off_gret), (proj, off_gmla), y_ret))
    h1 = _mm(merged, full["w_out"], mode="nn", outs=[F32], name="out_proj",
             epi=lambda acc, r: (acc + r,), extras=(xs,))
    n1, rstd1 = _rmsnorm_fwd(h1, norm_mlp_g, name="norm_mlp")

    def up_epi(acc):
        r = jnp.maximum(acc, 0.0)
        return acc, r * r

    z, act = _mm(n1, full["w_up"], mode="nn", outs=[F32, BF16], name="up_proj", epi=up_epi)
    h2 = _mm(act, full["w_down"], mode="nn", outs=[F32], name="down_proj",
             epi=lambda acc, r: (acc + r,), extras=(h1,), tm=512, tk=4096)
    loss11, dh2, dh2_b, g_norm_f = _final_loss(h2, norm_f_g.reshape(1, D), tgt)

    dz = _mm(dh2_b, full["w_down"], mode="nt", outs=[BF16], name="down_bwd_x",
             epi=lambda acc, zz: (acc * (2.0 * jnp.maximum(zz, 0.0)),), extras=(z,))
    g_w_down = _mm(act, dh2_b, mode="tn", outs=[BF16], name="down_bwd_w")
    dn1 = _mm(dz, full["w_up"], mode="nt", outs=[F32], name="up_bwd_x")
    g_w_up = _mm(n1, dz, mode="tn", outs=[BF16], name="up_bwd_w", out_shards=True)

    def reduce_begin(tag, names, grads):
        pcs = [g if g.ndim == 3 else g.reshape(N_CHIPS, g.shape[0] // N_CHIPS, g.shape[1]) for g in grads]
        theirs = _swap_halves(pcs, name="swap_" + tag)
        sums = [_sum_pair(p, t, place, name="sum_pair_" + k) for k, p, t in zip(names, pcs, theirs)]
        return pcs, theirs, sums

    def scatter_begin(tag, sums):
        lands = [lax.empty((3,) + s.shape[1:], s.dtype) for s in sums]
        return _split_start(sums + lands, _scatter_plan(len(sums)), 3 * len(sums), name="scatter_" + tag + "_start")

    def swap_begin(tag, grads):
        views = [g if g.ndim == 3 else g.reshape(N_CHIPS, g.shape[0] // N_CHIPS, g.shape[1]) for g in grads]
        views = [v.reshape(N_CHIPS, 2, v.shape[1] // 2, v.shape[2]) for v in views]
        lands = [lax.empty((N_CHIPS,) + v.shape[2:], v.dtype) for v in views]
        return _split_start(views + lands, _swap_plan(len(views)), len(views), name="swap_" + tag + "_start")

    def swap_end(tag, names, handle, after):
        n = len(names)
        bufs = _split_wait(handle[0], handle[1], handle[2], after, _swap_plan(n), name="swap_" + tag + "_wait")
        pcs = [b.reshape(N_CHIPS, 2 * b.shape[2], b.shape[3]) for b in bufs[:n]]
        sums = [_sum_pair(p, t, place, name="sum_pair_" + k) for k, p, t in zip(names, pcs, bufs[n:])]
        return pcs, bufs[n:], sums

    g1 = ("w_up", "w_down")
    swap1 = swap_begin("g1", (g_w_up, g_w_down))
    dh1, g_norm_mlp, dh1_b = _rmsnorm_bwd(dn1, h1, rstd1, norm_mlp_g, name="norm_mlp_bwd", res=dh2,
                                          deps=(swap1[3],), bf16_copy=1)
    dmerged = _mm(dh1_b, full["w_out"], mode="nt", outs=[F32], name="out_bwd_x")
    pcs1, theirs1, sums1 = swap_end("g1", g1, swap1, dmerged)
    ssem1, rsem1, bufs1, token1 = scatter_begin("g1", sums1)
    g_w_out = _mm(merged, dh1_b, mode="tn", outs=[BF16], name="out_bwd_w", deps=(token1,))
    dproj, dy_ret, dy_mla = _merge_bwd(dmerged, proj, y_ret, y_mla, D, off_gret)
    dgated = _mm(dy_ret, full["w_ret_o"], mode="nt", outs=[F32], name="ret_o_bwd_x")
    g_w_ret_o = _mm(gated, dy_ret, mode="tn", outs=[BF16], name="ret_o_bwd_w")
    dproj, g_ret_norm = _ret_bwd(proj, cosr, sinr, lgam, ret_norm_g, ry, dgated, states, dproj, RH, T=T_RET)
    def delta_epi(acc, o):
        rows = acc.shape[0]
        return acc, [jnp.broadcast_to(jnp.sum(acc[:, lo:lo + V_HEAD] * o[:, lo:lo + V_HEAD], axis=-1, keepdims=True),
                                      (rows, LANES)) for lo in range(0, acc.shape[1], V_HEAD)]

    dob, delta = _mm(dy_mla, full["w_mla_o"], mode="nt", outs=[BF16], name="mla_o_bwd_x", epi=delta_epi,
                     extras=(my,), more_outs=lambda tm, tn: [
                         (jax.ShapeDtypeStruct((MH, S, LANES), F32),
                          pl.BlockSpec((tn // V_HEAD, tm, LANES), lambda i, j, k: (j, i, 0)))])
    g_w_mla_o = _mm(my_b, dy_mla, mode="tn", outs=[BF16], name="mla_o_bwd_w")
    g2 = ("w_out", "w_ret_o", "w_mla_o")
    swap2 = swap_begin("g2", (g_w_out, g_w_ret_o, g_w_mla_o))
    dq_all, dkv_all, dkpe_h = _attn_bwd(qf, kf, vb, dob, lse2, delta, pe_tabs, MH, T=T_ATT, deps=(swap2[3],))
    pcs2, theirs2, sums2 = swap_end("g2", g2, swap2, dkv_all)
    ssem2, rsem2, bufs2, token2 = scatter_begin("g2", sums2)
    dkpe = _kpe_sum(dkpe_h, pe_tabs, MH)
    dcqn = _mm(dq_all, wq, mode="nt", outs=[F32], name="q_bwd_x", deps=(token2,))
    g_wq = _mm(cqn, dq_all, mode="tn", outs=[BF16], name="q_bwd_w")
    dckvn = _mm(dkv_all, wkv, mode="nt", outs=[F32], name="kv_bwd_x")
    g_wkv = _mm(ckvn, dkv_all, mode="tn", outs=[BF16], name="kv_bwd_w")
    dproj, g_q_a = _rmsnorm_bwd(dcqn, proj, rstd_q, q_a_norm_g, name="norm_q_bwd", into=(dproj, off_cq // QL),
                                width=QL, col=off_cq // QL)
    dproj, g_kv_a = _rmsnorm_bwd(dckvn, proj, rstd_kv, kv_a_norm_g, name="norm_kv_bwd", into=(dproj, off_ckv // KVL),
                                 width=KVL, col=off_ckv // KVL)
    g_wa = _mm(dproj, u, mode="tn", outs=[BF16], name="in_bwd_w")
    g_wkpe = _mm(dkpe, u, mode="tn", outs=[BF16], name="kpe_bwd_w")

    there = sorted(runs)
    g_parts = [g_wa, g_wkpe]
    g_w_in = jnp.stack([jnp.concatenate(
        [p for s0, w, d0 in there for a, b in [(max(s0, jj * c_sh), min(s0 + w, (jj + 1) * c_sh))] if a < b
         for p in take(g_parts, d0 + a - s0, b - a)] + [jnp.zeros((c_pad - c_sh, D), BF16)], axis=0)
        for jj in range(N_CHIPS)])
    gq = g_wq.reshape(QL, MH, 2 * LANES)[:, :, :QK_NOPE + QK_ROPE].reshape(QL, MH * (QK_NOPE + QK_ROPE))
    g3 = ("w_in", "w_q_b", "w_kv_b")
    pcs3, theirs3, sums3 = reduce_begin("g3", g3, (g_w_in, _split_cols(gq), _split_cols(g_wkv)))
    ssem3, rsem3, bufs3, token3 = scatter_begin("g3", sums3)
    du = _mm(dproj, wa, mode="nn", outs=[F32], name="in_bwd_x", tk=2816, tail=(dkpe, wkpe), deps=(token3,))
    dx, g_norm_mix = _rmsnorm_bwd(du, xs, rstd0, norm_mix_g, name="norm_mix_bwd", res=dh1)

    bufs1 = _split_wait(ssem1, rsem1, bufs1, dx, _scatter_plan(len(g1)), name="scatter_g1_wait")
    bufs2 = _split_wait(ssem2, rsem2, bufs2, dx, _scatter_plan(len(g2)), name="scatter_g2_wait")
    bufs3 = _split_wait(ssem3, rsem3, bufs3, dx, _scatter_plan(len(g3)), name="scatter_g3_wait")
    recv1, recv2, recv3 = bufs1[len(g1):], bufs2[len(g2):], bufs3[len(g3):]
    halves = {}
    for names, pcs, theirs, recv in ((g1, pcs1, theirs1, recv1), (g2, pcs2, theirs2, recv2), (g3, pcs3, theirs3, recv3)):
        for k, p, t, r in zip(names, pcs, theirs, recv):
            halves[k] = _sum_chips(p, t, r, place, name="sum_chips_" + k)
    joined = _join_halves([halves[k] for k in big], name="join_halves")
    g_shard = {k: g.reshape(2 * g.shape[1], g.shape[2]) for k, g in zip(big, joined)}

    small = ("norm_mix_g", "ret_norm_g", "q_a_norm_g", "kv_a_norm_g", "norm_mlp_g", "norm_f_g")
    g_small = [g_norm_mix, g_ret_norm, g_q_a, g_kv_a, g_norm_mlp, g_norm_f]
    red = _allreduce_small(g_small, loss11)
    loss = red[0, red.shape[1] - 1]
    w_small = [norm_mix_g, ret_norm_g, q_a_norm_g, kv_a_norm_g, norm_mlp_g, norm_f_g]
    m_small = [m_norm_mix_g, m_ret_norm_g, m_q_a_norm_g, m_kv_a_norm_g, m_norm_mlp_g, m_norm_f_g]
    v_small = [v_norm_mix_g, v_ret_norm_g, v_q_a_norm_g, v_kv_a_norm_g, v_norm_mlp_g, v_norm_f_g]
    row = lambda a: a.reshape(1, -1)
    upd = _adamw_small(red, [row(a) for a in w_small], [row(a) for a in m_small], [row(a) for a in v_small])
    out_g, out_d, out_m, out_v = {}, {}, {}, {}
    for k, wv, (g_, d_, m_, v_) in zip(small, w_small, upd):
        out_g[k], out_d[k], out_m[k], out_v[k] = [a.reshape(wv.shape) for a in (g_, d_, m_, v_)]

    for k in big:
        res = _rows_call(lambda w, g, m, v: (g,) + _adamw_vals(w, g, m, v),
                         [w_sh[k], g_shard[k], m_sh[k], v_sh[k]], [F32] * 4, name="adamw_" + k)
        if k == "w_in":
            res = [r.T for r in res]
        out_g[k], out_d[k], out_m[k], out_v[k] = [r[None] for r in res]

    order = ("norm_mix_g", "w_in", "ret_norm_g", "w_ret_o", "q_a_norm_g", "w_q_b", "kv_a_norm_g", "w_kv_b",
             "w_mla_o", "w_out", "norm_mlp_g", "w_up", "w_down", "norm_f_g")
    return (loss, dx.reshape(1, S, D), *[out_g[k] for k in order], *[out_d[k] for k in order],
            *[out_m[k] for k in order], *[out_v[k] for k in order])
```

```python
import math

import jax
import jax.numpy as jnp
from jax import lax
from jax.experimental import pallas as pl
from jax.experimental.pallas import tpu as pltpu

F32 = jnp.float32
BF16 = jnp.bfloat16

EPS = 1e-6
ROPE_THETA = 10000.0
CHUNK = 64
RET_QK = 128
RET_V = 256
RET_HEAD_COLS = 2 * RET_QK + 2 * RET_V
QK_NOPE = 128
QK_ROPE = 64
V_HEAD = 128
LANES = 128
LOG2E = math.log2(math.e)

ADAM_LR = 0.001
ADAM_B1 = 0.9
ADAM_B2 = 0.999
ADAM_EPS = 1e-08
ADAM_WD = 0.01
ADAM_STEP = 10

N_CHIPS = 4
VMEM_LIMIT = 56 * 1024 * 1024
MESH = pl.DeviceIdType.MESH
NEG = -1e30


def _pallas(body, **kw):
    return pl.pallas_call(body, **kw)


def _params(sem=None):
    return pltpu.CompilerParams(dimension_semantics=sem, vmem_limit_bytes=VMEM_LIMIT)


def _tile(n, want):
    t = min(n, want)
    while n % t:
        t //= 2
    return t


_ANY = pl.BlockSpec(memory_space=pl.ANY)
TN_BF16_TK = 4096


def _mm(a, b, *, mode, outs, name, epi=None, extras=(), deps=(), out_shards=False, more_outs=None, tail=None,
        tm=1024, tn=1024, tk=2048):
    shards = b.shape[0] if b.ndim == 3 else 1
    brows, bcols = b.shape[-2], b.shape[-1] * shards
    if mode == "nn":
        (M, K), N = a.shape, bcols
    elif mode == "nt":
        (M, K), N = a.shape, brows
    else:
        (K, M), N = a.shape, bcols
    if mode == "tn" and a.dtype == BF16 and b.dtype == BF16:
        tk = max(tk, TN_BF16_TK)
    tm = _tile(M, tm)
    tn = _tile(N // (shards if mode == "nn" else 1) // (N_CHIPS if out_shards else 1), tn)
    tk = _tile(K // (shards if mode == "nt" else 1), tk)
    nk = K // tk
    if mode == "nn":
        a_spec = pl.BlockSpec((tm, tk), lambda i, j, k: (i, k))
        dims = (((1,), (0,)), ((), ()))
        if shards > 1:
            per = N // shards // tn
            b_spec = pl.BlockSpec((None, tk, tn), lambda i, j, k: (j // per, k, j % per))
        else:
            b_spec = pl.BlockSpec((tk, tn), lambda i, j, k: (k, j))
    elif mode == "nt":
        a_spec = pl.BlockSpec((tm, tk), lambda i, j, k: (i, k))
        dims = (((1,), (1,)), ((), ()))
        if shards > 1:
            per = K // shards // tk
            b_spec = pl.BlockSpec((None, tn, tk), lambda i, j, k: (k // per, j, k % per))
        else:
            b_spec = pl.BlockSpec((tn, tk), lambda i, j, k: (j, k))
    else:
        assert shards == 1
        a_spec = pl.BlockSpec((tk, tm), lambda i, j, k: (k, i))
        b_spec = pl.BlockSpec((tk, tn), lambda i, j, k: (k, j))
        dims = (((0,), (0,)), ((), ()))
    if out_shards:
        assert not extras
        oper = N // N_CHIPS // tn
        o_spec = pl.BlockSpec((None, tm, tn), lambda i, j, k: (j // oper, i, j % oper))
        o_shape = (N_CHIPS, M, N // N_CHIPS)
    else:
        o_spec = pl.BlockSpec((tm, tn), lambda i, j, k: (i, j))
        o_shape = (M, N)
    more = [] if more_outs is None else more_outs(tm, tn)
    ex_arrays = [e[0] if isinstance(e, tuple) else e for e in extras]
    ex_specs = [pl.BlockSpec((tm, tn), lambda i, j, k, off=e[1] // tn: (i, off + j)) if isinstance(e, tuple)
                else o_spec for e in extras]
    n_ex, n_out, n_dep = len(extras), len(outs) + len(more), len(deps)
    if epi is None:
        epi = lambda acc: (acc,)
    tails, tail_specs = [], []
    if tail is not None:
        assert mode == "nn"
        tails = list(tail)
        k2 = tail[0].shape[1]
        tail_specs = [pl.BlockSpec((tm, k2), lambda i, j, k: (i, 0)), pl.BlockSpec((k2, tn), lambda i, j, k: (0, j))]
    n_tail = len(tails)

    def body(*refs):
        a_ref, b_ref = refs[0], refs[1]
        ex_refs = refs[2:2 + n_ex]
        t_refs = refs[2 + n_ex:2 + n_ex + n_tail]
        first_out = 2 + n_ex + n_tail + n_dep
        o_refs = refs[first_out:first_out + n_out]
        part = lax.dot_general(a_ref[...].astype(BF16), b_ref[...].astype(BF16), dims,
                               preferred_element_type=F32)

        def finish(acc):
            if n_tail:
                acc = acc + lax.dot_general(t_refs[0][...].astype(BF16), t_refs[1][...].astype(BF16), dims,
                                            preferred_element_type=F32)
            vals = epi(acc, *[r[...] for r in ex_refs])
            for r, v in zip(o_refs, vals):
                if isinstance(v, (list, tuple)):
                    for lead, piece in enumerate(v):
                        r[lead] = piece.astype(r.dtype)
                else:
                    r[...] = v.astype(r.dtype)

        if nk == 1:
            finish(part)
        else:
            acc_ref = refs[-1]
            k = pl.program_id(2)

            @pl.when(k == 0)
            def _():
                acc_ref[...] = part

            @pl.when(k > 0)
            def _():
                acc_ref[...] += part

            @pl.when(k == nk - 1)
            def _():
                finish(acc_ref[...])

    res = _pallas(
        body, name=name, grid=(M // tm, N // tn, nk),
        in_specs=[a_spec, b_spec] + ex_specs + tail_specs + [_ANY] * n_dep,
        out_specs=[o_spec] * len(outs) + [spec for _, spec in more],
        out_shape=[jax.ShapeDtypeStruct(o_shape, d) for d in outs] + [shape for shape, _ in more],
        scratch_shapes=[pltpu.VMEM((tm, tn), F32)] if nk > 1 else [],
        compiler_params=_params(("parallel", "parallel", "arbitrary")),
    )(a, b, *ex_arrays, *tails, *deps)
    return res[0] if n_out == 1 else res


def _rmsnorm_fwd(x, g, *, name, width=None, col=0, tr=256):
    S = x.shape[0]
    W = x.shape[1] if width is None else width
    tr = _tile(S, tr)

    def body(x_ref, g_ref, y_ref, r_ref):
        xv = x_ref[...]
        rstd = lax.rsqrt(jnp.mean(xv * xv, axis=-1, keepdims=True) + EPS)
        y_ref[...] = (xv * rstd * g_ref[...]).astype(BF16)
        r_ref[...] = rstd

    return _pallas(
        body, name=name, grid=(S // tr,),
        in_specs=[pl.BlockSpec((tr, W), lambda i: (i, col)), pl.BlockSpec((1, W), lambda i: (0, 0))],
        out_specs=[pl.BlockSpec((tr, W), lambda i: (i, 0)), pl.BlockSpec((tr, 1), lambda i: (i, 0))],
        out_shape=[jax.ShapeDtypeStruct((S, W), BF16), jax.ShapeDtypeStruct((S, 1), F32)],
        compiler_params=_params(("parallel",)),
    )(x, g)


def _rmsnorm_bwd(dy, x, rstd, g, *, name, res=None, into=None, deps=(), bf16_copy=0, width=None, col=0, tr=256):
    S = x.shape[0]
    W = x.shape[1] if width is None else width
    tr = _tile(S, tr)
    has_res = res is not None

    def body(*refs):
        dy_ref, x_ref, r_ref, g_ref = refs[:4]
        dx_ref, dg_ref = refs[-2 - bf16_copy], refs[-1 - bf16_copy]
        rstd_v = r_ref[...]
        xhat = x_ref[...] * rstd_v
        dyv = dy_ref[...].astype(F32)
        dyg = dyv * g_ref[...]
        dx = rstd_v * (dyg - xhat * jnp.mean(dyg * xhat, axis=-1, keepdims=True))
        if has_res:
            dx = dx + refs[4][...]
        dx_ref[...] = dx.astype(dx_ref.dtype)
        if bf16_copy:
            refs[-1][...] = dx.astype(BF16)
        part = jnp.sum(dyv * xhat, axis=0, keepdims=True)

        @pl.when(pl.program_id(0) == 0)
        def _():
            dg_ref[...] = part

        @pl.when(pl.program_id(0) > 0)
        def _():
            dg_ref[...] += part

    row = pl.BlockSpec((tr, W), lambda i: (i, 0))
    ins = [dy, x, rstd, g] + ([res] if has_res else [])
    in_specs = [row, pl.BlockSpec((tr, W), lambda i: (i, col)), pl.BlockSpec((tr, 1), lambda i: (i, 0)),
                pl.BlockSpec((1, W), lambda i: (0, 0))] + ([row] if has_res else [])
    if into is None:
        dx_spec, dx_shape, alias = row, jax.ShapeDtypeStruct((S, W), F32), {}
    else:
        buf, col_out = into
        ins.append(buf)
        in_specs.append(_ANY)
        dx_spec = pl.BlockSpec((tr, W), lambda i: (i, col_out))
        dx_shape = jax.ShapeDtypeStruct(buf.shape, buf.dtype)
        alias = {len(ins) - 1: 0}
    ins += list(deps)
    in_specs += [_ANY] * len(deps)
    return _pallas(
        body, name=name, grid=(S // tr,), in_specs=in_specs,
        out_specs=[dx_spec, pl.BlockSpec((1, W), lambda i: (0, 0))] + [row] * bf16_copy,
        out_shape=[dx_shape, jax.ShapeDtypeStruct((1, W), F32)] + [jax.ShapeDtypeStruct((S, W), BF16)] * bf16_copy,
        input_output_aliases=alias,
        compiler_params=_params(("arbitrary",)),
    )(*ins)


def _final_loss(h2, g, target, *, tr=256):
    S, D = h2.shape
    tr = _tile(S, tr)

    def body(h_ref, g_ref, t_ref, loss_ref, dh_ref, dhb_ref, dg_ref):
        hv = h_ref[...]
        rstd = lax.rsqrt(jnp.mean(hv * hv, axis=-1, keepdims=True) + EPS)
        xhat = hv * rstd
        e = xhat * g_ref[...] - t_ref[...]
        lpart = (0.5 / D) * jnp.sum(jnp.sum(e * e, axis=-1, keepdims=True), axis=0, keepdims=True)
        dy = e * (1.0 / D)
        dyg = dy * g_ref[...]
        dh = rstd * (dyg - xhat * jnp.mean(dyg * xhat, axis=-1, keepdims=True))
        dh_ref[...] = dh
        dhb_ref[...] = dh.astype(BF16)
        gpart = jnp.sum(dy * xhat, axis=0, keepdims=True)

        @pl.when(pl.program_id(0) == 0)
        def _():
            loss_ref[...] = lpart
            dg_ref[...] = gpart

        @pl.when(pl.program_id(0) > 0)
        def _():
            loss_ref[...] += lpart
            dg_ref[...] += gpart

    row = pl.BlockSpec((tr, D), lambda i: (i, 0))
    vec = pl.BlockSpec((1, D), lambda i: (0, 0))
    return _pallas(
        body, name="final_loss", grid=(S // tr,), in_specs=[row, vec, row],
        out_specs=[pl.BlockSpec((1, 1), lambda i: (0, 0)), row, row, vec],
        out_shape=[jax.ShapeDtypeStruct((1, 1), F32), jax.ShapeDtypeStruct((S, D), F32),
                   jax.ShapeDtypeStruct((S, D), BF16), jax.ShapeDtypeStruct((1, D), F32)],
        compiler_params=_params(("arbitrary",)),
    )(h2, g, target)


def _sigmoid(v):
    return 1.0 / (1.0 + jnp.exp(-v))


def _merge_bwd(dmerged, proj, y_ret, y_mla, D, off_gret, *, tr=256):
    S = y_ret.shape[0]
    tr = _tile(S, tr)
    b0 = off_gret // D

    def body(dm_ref, g_ref, yr_ref, ym_ref, dp_ref, dyr_ref, dym_ref):
        dm = dm_ref[...]
        sg = _sigmoid(g_ref[...])

        @pl.when(pl.program_id(1) == 0)
        def _():
            dyr_ref[...] = (dm * sg).astype(BF16)
            dp_ref[...] = (dm * yr_ref[...] * sg * (1.0 - sg)).astype(BF16)

        @pl.when(pl.program_id(1) == 1)
        def _():
            dym_ref[...] = (dm * sg).astype(BF16)
            dp_ref[...] = (dm * ym_ref[...] * sg * (1.0 - sg)).astype(BF16)

    blk = pl.BlockSpec((tr, D), lambda i, j: (i, 0))
    return _pallas(
        body, name="merge_bwd", grid=(S // tr, 2),
        in_specs=[blk, pl.BlockSpec((tr, D), lambda i, j: (i, b0 + j)), blk, blk],
        out_specs=[pl.BlockSpec((tr, D), lambda i, j: (i, b0 + j)), blk, blk],
        out_shape=[jax.ShapeDtypeStruct(proj.shape, BF16), jax.ShapeDtypeStruct((S, D), BF16),
                   jax.ShapeDtypeStruct((S, D), BF16)],
        compiler_params=_params(("parallel", "arbitrary")),
    )(dmerged, proj, y_ret, y_mla)


def _rope128(t, cos_full, sin_signed):
    return t * cos_full + pltpu.roll(t, RET_QK // 2, 1) * sin_signed


def _rope128_t(d, cos_full, sin_signed):
    return d * cos_full + pltpu.roll(d * sin_signed, RET_QK // 2, 1)


def _ret_consts(lg, T):
    pos = lax.broadcasted_iota(jnp.int32, (T, 1), 0).astype(F32)
    qd = jnp.exp(lg * (pos + 1.0))
    kd = jnp.exp(lg * (T - 1.0 - pos))
    n = lax.broadcasted_iota(jnp.int32, (T, T), 0)
    m = lax.broadcasted_iota(jnp.int32, (T, T), 1)
    vis = (m // CHUNK) <= (n // CHUNK)
    dist = jnp.abs(n - m).astype(F32)
    decay = jnp.where(vis, jnp.exp(lg * dist), 0.0)
    cdec = jnp.exp(lg * float(T))
    return qd, kd, decay, cdec


def _dot(a, b, dims):
    return lax.dot_general(a.astype(BF16), b.astype(BF16), (dims, ((), ())), preferred_element_type=F32)


NN = ((1,), (0,))
NT = ((1,), (1,))
TN = ((0,), (0,))
_RQ = slice(0, RET_QK)
_RK = slice(RET_QK, 2 * RET_QK)
_RV = slice(2 * RET_QK, 2 * RET_QK + RET_V)
_RG = slice(2 * RET_QK + RET_V, RET_HEAD_COLS)


RET_GROUP = 8


def _head_cols(h, part):
    return slice(h * RET_HEAD_COLS + part.start, h * RET_HEAD_COLS + part.stop)


def _ret_fwd(proj, cosr, sinr, lgam, gain, RH, *, T):
    S = proj.shape[0]
    nb = S // T
    G = _tile(RH, RET_GROUP)
    heads = range(G)
    scale = RET_QK ** -0.5

    def body(p_ref, cos_ref, sin_ref, lg_ref, gain_ref, ry_ref, gated_ref, st_ref, state):
        b = pl.program_id(1)

        @pl.when(b == 0)
        def _():
            state[...] = jnp.zeros_like(state)

        consts = [_ret_consts(lg_ref[h, 0:1, 0:1], T) for h in heads]
        cosv, sinv = cos_ref[...], sin_ref[...]
        q = [_rope128(p_ref[:, _head_cols(h, _RQ)], cosv, sinv) for h in heads]
        k = [_rope128(p_ref[:, _head_cols(h, _RK)], cosv, sinv) * scale for h in heads]
        v = [p_ref[:, _head_cols(h, _RV)] for h in heads]
        sprev = [state[h] for h in heads]
        for h in heads:
            st_ref[h] = sprev[h]
        a = [_dot(q[h], k[h], NT) for h in heads]
        qs = [_dot(q[h] * consts[h][0], sprev[h], NN) for h in heads]
        kv = [_dot(k[h] * consts[h][1], v[h], TN) for h in heads]
        o = [_dot(a[h] * consts[h][2], v[h], NN) + qs[h] for h in heads]
        for h in heads:
            state[h] = sprev[h] * consts[h][3] + kv[h]
            vals = slice(h * RET_V, (h + 1) * RET_V)
            ry_ref[:, vals] = o[h]
            mu = jnp.mean(o[h], axis=-1, keepdims=True)
            oc = o[h] - mu
            var = jnp.mean(oc * oc, axis=-1, keepdims=True)
            t = oc * lax.rsqrt(var + EPS) * gain_ref[:, vals]
            gv = p_ref[:, _head_cols(h, _RG)]
            gated_ref[:, vals] = (t * (gv * _sigmoid(gv))).astype(BF16)

    return _pallas(
        body, name="ret_fwd", grid=(RH // G, nb),
        in_specs=[pl.BlockSpec((T, G * RET_HEAD_COLS), lambda h, b: (b, h)),
                  pl.BlockSpec((T, RET_QK), lambda h, b: (b, 0)),
                  pl.BlockSpec((T, RET_QK), lambda h, b: (b, 0)),
                  pl.BlockSpec((G, 8, LANES), lambda h, b: (h, 0, 0)),
                  pl.BlockSpec((1, G * RET_V), lambda h, b: (0, h))],
        out_specs=[pl.BlockSpec((T, G * RET_V), lambda h, b: (b, h)),
                   pl.BlockSpec((T, G * RET_V), lambda h, b: (b, h)),
                   pl.BlockSpec((G, None, RET_QK, RET_V), lambda h, b: (h, b, 0, 0))],
        out_shape=[jax.ShapeDtypeStruct((S, RH * RET_V), F32), jax.ShapeDtypeStruct((S, RH * RET_V), BF16),
                   jax.ShapeDtypeStruct((RH, nb, RET_QK, RET_V), F32)],
        scratch_shapes=[pltpu.VMEM((G, RET_QK, RET_V), F32)],
        compiler_params=_params(("parallel", "arbitrary")),
    )(proj, cosr, sinr, lgam, gain)


def _ret_bwd(proj, cosr, sinr, lgam, gain, ry, dgated, states, dproj, RH, *, T):
    S = proj.shape[0]
    nb = S // T
    G = _tile(RH, RET_GROUP)
    heads = range(G)
    scale = RET_QK ** -0.5

    def body(p_ref, cos_ref, sin_ref, lg_ref, gain_ref, ry_ref, dg_ref, st_ref, _, dp_ref, dgain_ref, dstate):
        b = pl.program_id(1)

        @pl.when(b == 0)
        def _():
            dstate[...] = jnp.zeros_like(dstate)

        consts = [_ret_consts(lg_ref[h, 0:1, 0:1], T) for h in heads]
        qd, kd, decay, cdec = [[c[i] for c in consts] for i in range(4)]
        cosv, sinv = cos_ref[...], sin_ref[...]
        q = [_rope128(p_ref[:, _head_cols(h, _RQ)], cosv, sinv) for h in heads]
        k = [_rope128(p_ref[:, _head_cols(h, _RK)], cosv, sinv) * scale for h in heads]
        v = [p_ref[:, _head_cols(h, _RV)] for h in heads]
        sprev = [st_ref[h] for h in heads]
        ds_new = [dstate[h] for h in heads]
        a = [_dot(q[h], k[h], NT) for h in heads]
        do, gparts = [], []
        for h in heads:
            vals = slice(h * RET_V, (h + 1) * RET_V)
            o = ry_ref[:, vals]
            mu = jnp.mean(o, axis=-1, keepdims=True)
            oc = o - mu
            rstd = lax.rsqrt(jnp.mean(oc * oc, axis=-1, keepdims=True) + EPS)
            ryn = oc * rstd
            gainv = gain_ref[:, vals]
            gv = p_ref[:, _head_cols(h, _RG)]
            sg = _sigmoid(gv)
            dgt = dg_ref[:, vals]
            dt = dgt * (gv * sg)
            dp_ref[:, _head_cols(h, _RG)] = (dgt * (ryn * gainv) * (sg * (1.0 + gv * (1.0 - sg)))).astype(BF16)
            gparts.append(jnp.sum(dt * ryn, axis=0, keepdims=True))
            dryn = dt * gainv
            do.append(rstd * (dryn - jnp.mean(dryn, axis=-1, keepdims=True)
                              - ryn * jnp.mean(dryn * ryn, axis=-1, keepdims=True)))
        gpart = jnp.concatenate(gparts, axis=1)

        @pl.when(b == 0)
        def _():
            dgain_ref[...] = gpart

        @pl.when(b > 0)
        def _():
            dgain_ref[...] += gpart

        dpm = [_dot(do[h], v[h], NT) for h in heads]
        dq_s = [_dot(do[h], sprev[h], NT) for h in heads]
        dk_s = [_dot(v[h], ds_new[h], NT) for h in heads]
        dv_s = [_dot(k[h] * kd[h], ds_new[h], NN) for h in heads]
        dst = [_dot(q[h] * qd[h], do[h], TN) for h in heads]
        a = [a[h] * decay[h] for h in heads]
        dpm = [dpm[h] * decay[h] for h in heads]
        dv = [_dot(a[h], do[h], TN) + dv_s[h] for h in heads]
        dq = [_dot(dpm[h], k[h], NN) + dq_s[h] * qd[h] for h in heads]
        dk = [(_dot(dpm[h], q[h], TN) + dk_s[h] * kd[h]) * scale for h in heads]
        for h in heads:
            dstate[h] = ds_new[h] * cdec[h] + dst[h]
            dp_ref[:, _head_cols(h, _RV)] = dv[h].astype(BF16)
            dp_ref[:, _head_cols(h, _RQ)] = _rope128_t(dq[h], cosv, sinv).astype(BF16)
            dp_ref[:, _head_cols(h, _RK)] = _rope128_t(dk[h], cosv, sinv).astype(BF16)

    rb = lambda b: nb - 1 - b
    return _pallas(
        body, name="ret_bwd", grid=(RH // G, nb),
        in_specs=[pl.BlockSpec((T, G * RET_HEAD_COLS), lambda h, b: (rb(b), h)),
                  pl.BlockSpec((T, RET_QK), lambda h, b: (rb(b), 0)),
                  pl.BlockSpec((T, RET_QK), lambda h, b: (rb(b), 0)),
                  pl.BlockSpec((G, 8, LANES), lambda h, b: (h, 0, 0)),
                  pl.BlockSpec((1, G * RET_V), lambda h, b: (0, h)),
                  pl.BlockSpec((T, G * RET_V), lambda h, b: (rb(b), h)),
                  pl.BlockSpec((T, G * RET_V), lambda h, b: (rb(b), h)),
                  pl.BlockSpec((G, None, RET_QK, RET_V), lambda h, b: (h, rb(b), 0, 0)),
                  _ANY],
        out_specs=[pl.BlockSpec((T, G * RET_HEAD_COLS), lambda h, b: (rb(b), h)),
                   pl.BlockSpec((1, G * RET_V), lambda h, b: (0, h))],
        out_shape=[jax.ShapeDtypeStruct(dproj.shape, dproj.dtype), jax.ShapeDtypeStruct((1, RH * RET_V), F32)],
        scratch_shapes=[pltpu.VMEM((G, RET_QK, RET_V), F32)],
        input_output_aliases={8: 0},
        compiler_params=_params(("parallel", "arbitrary")),
    )(proj, cosr, sinr, lgam, gain, ry, dgated, states, dproj)


def _rope_pe(t, c, s1, s2):
    return t * c + pltpu.roll(t, LANES - QK_ROPE // 2, 1) * s1 + pltpu.roll(t, QK_ROPE // 2, 1) * s2


def _rope_pe_t(d, c, s1, s2):
    return d * c + pltpu.roll(d * s1, QK_ROPE // 2, 1) + pltpu.roll(d * s2, LANES - QK_ROPE // 2, 1)


ATTN_C2 = (QK_NOPE + QK_ROPE) ** -0.5 * LOG2E


def _qkv_proj(cqn, ckvn, wq, wkv, kpe, tabs, MH, *, tm=512, heads=4):
    S = cqn.shape[0]
    tm = _tile(S, tm)
    hb = _tile(MH, heads)
    W = 2 * LANES
    c_t, s1_t, s2_t = tabs

    def body(cq_ref, ckv_ref, wq_ref, wkv_ref, kpe_ref, c_ref, s1_ref, s2_ref, qf_ref, kf_ref, v_ref):
        c, s1, s2 = c_ref[...], s1_ref[...], s2_ref[...]
        q = _dot(cq_ref[...], wq_ref[...], NN)
        kv = _dot(ckv_ref[...], wkv_ref[...], NN)
        kper = _rope_pe(kpe_ref[...], c, s1, s2).astype(BF16)
        for h in range(hb):
            lo, mid, hi = h * W, h * W + QK_NOPE, (h + 1) * W
            qf_ref[:, lo:mid] = (q[:, lo:mid] * ATTN_C2).astype(BF16)
            qf_ref[:, mid:hi] = (_rope_pe(q[:, mid:hi], c, s1, s2) * ATTN_C2).astype(BF16)
            kf_ref[:, lo:mid] = kv[:, lo:mid].astype(BF16)
            kf_ref[:, mid:hi] = kper
            v_ref[:, h * V_HEAD:(h + 1) * V_HEAD] = kv[:, mid:hi].astype(BF16)

    tab = pl.BlockSpec((tm, LANES), lambda i, j: (i, 0))
    grp = pl.BlockSpec((tm, hb * W), lambda i, j: (i, j))
    return _pallas(
        body, name="qkv_proj", grid=(S // tm, MH // hb),
        in_specs=[pl.BlockSpec((tm, cqn.shape[1]), lambda i, j: (i, 0)),
                  pl.BlockSpec((tm, ckvn.shape[1]), lambda i, j: (i, 0)),
                  pl.BlockSpec((wq.shape[0], hb * W), lambda i, j: (0, j)),
                  pl.BlockSpec((wkv.shape[0], hb * W), lambda i, j: (0, j)), tab, tab, tab, tab],
        out_specs=[grp, grp, pl.BlockSpec((tm, hb * V_HEAD), lambda i, j: (i, j))],
        out_shape=[jax.ShapeDtypeStruct((S, MH * W), BF16)] * 2 + [jax.ShapeDtypeStruct((S, MH * V_HEAD), BF16)],
        compiler_params=_params(("parallel", "parallel")),
    )(cqn, ckvn, wq, wkv, kpe, c_t, s1_t, s2_t)


def _chunk_mask(T):
    n = lax.broadcasted_iota(jnp.int32, (T, T), 0)
    m = lax.broadcasted_iota(jnp.int32, (T, T), 1)
    return (m // CHUNK) <= (n // CHUNK)


def _lanes_to(v, width):
    return jnp.tile(v, (1, width // LANES))


def _attn_fwd(qf, kf, vb, MH, *, T):
    S = qf.shape[0]
    nt = S // T

    def body(q_ref, k_ref, v_ref, o_ref, ob_ref, lse_ref, m_sc, l_sc, acc_sc, s_a, s_b):
        qi = pl.program_id(1)
        m_sc[...] = jnp.full_like(m_sc, NEG)
        l_sc[...] = jnp.zeros_like(l_sc)
        acc_sc[...] = jnp.zeros_like(acc_sc)

        def rows_of(kt):
            return pl.ds(pl.multiple_of(kt * T, T), T)

        def scores(kt):
            return _dot(q_ref[...], k_ref[rows_of(kt), :], NT)

        def update(s, kt):
            m_prev = m_sc[...]
            m_new = jnp.maximum(m_prev, jnp.max(s, axis=-1, keepdims=True))
            alpha = jnp.exp2(m_prev - m_new)
            p = jnp.exp2(s - _lanes_to(m_new, T))
            l_sc[...] = alpha * l_sc[...] + jnp.sum(p, axis=-1, keepdims=True)
            acc_sc[...] = alpha * acc_sc[...] + _dot(p, v_ref[rows_of(kt), :], NN)
            m_sc[...] = m_new

        def masked(s):
            return jnp.where(_chunk_mask(T), s, NEG)

        @pl.when(qi == 0)
        def _():
            update(masked(scores(0)), 0)

        @pl.when(qi > 0)
        def _():
            s_a[...] = masked(scores(qi))
            s_b[...] = scores(0)
            update(s_a[...], qi)
            s_a[...] = scores(jnp.minimum(1, qi - 1))
            update(s_b[...], 0)

            def pair(j, carry):
                s_b[...] = scores(2 * j)
                update(s_a[...], 2 * j - 1)
                s_a[...] = scores(jnp.minimum(2 * j + 1, qi - 1))
                update(s_b[...], 2 * j)
                return carry

            lax.fori_loop(1, (qi + 1) // 2, pair, 0)

            @pl.when(qi % 2 == 0)
            def _():
                update(s_a[...], qi - 1)
        l = l_sc[...]
        o = acc_sc[...] / l
        o_ref[...] = o
        ob_ref[...] = o.astype(BF16)
        lse_ref[...] = m_sc[...] + jnp.log(l) * LOG2E

    return _pallas(
        body, name="attn_fwd", grid=(MH, nt),
        in_specs=[pl.BlockSpec((T, 2 * LANES), lambda h, i: (i, h)),
                  pl.BlockSpec((S, 2 * LANES), lambda h, i: (0, h)),
                  pl.BlockSpec((S, LANES), lambda h, i: (0, h))],
        out_specs=[pl.BlockSpec((T, LANES), lambda h, i: (i, h)), pl.BlockSpec((T, LANES), lambda h, i: (i, h)),
                   pl.BlockSpec((None, T, LANES), lambda h, i: (h, i, 0))],
        out_shape=[jax.ShapeDtypeStruct((S, MH * LANES), F32), jax.ShapeDtypeStruct((S, MH * LANES), BF16),
                   jax.ShapeDtypeStruct((MH, S, LANES), F32)],
        scratch_shapes=[pltpu.VMEM((T, LANES), F32), pltpu.VMEM((T, LANES), F32), pltpu.VMEM((T, LANES), F32),
                        pltpu.VMEM((T, T), F32), pltpu.VMEM((T, T), F32)],
        compiler_params=_params(("parallel", "parallel")),
    )(qf, kf, vb)


def _attn_bwd(qf, kf, vb, dob, lse2, delta, tabs, MH, *, T, deps=()):
    S = qf.shape[0]
    nt = S // T
    scale = (QK_NOPE + QK_ROPE) ** -0.5
    n_dep = len(deps)

    def body(q_ref, k_ref, v_ref, do_ref, lse_ref, dl_ref, c_ref, s1_ref, s2_ref, *rest):
        dqa_ref, dkv_ref, dkpe_ref, dq_ref, dk_sc, dv_sc, s_a, dp_a, s_b, dp_b = rest[n_dep:]
        kj = pl.program_id(1)

        @pl.when(kj == 0)
        def _():
            dq_ref[...] = jnp.zeros_like(dq_ref)

        dk_sc[...] = jnp.zeros_like(dk_sc)
        dv_sc[...] = jnp.zeros_like(dv_sc)

        def rows_of(qt):
            return pl.ds(pl.multiple_of(qt * T, T), T)

        def products(qt):
            rows = rows_of(qt)
            return _dot(q_ref[rows, :], k_ref[...], NT), _dot(do_ref[rows, :], v_ref[...], NT)

        def update(s, dp, qt):
            rows = rows_of(qt)
            q, dov = q_ref[rows, :], do_ref[rows, :]
            p = jnp.exp2(s - _lanes_to(lse_ref[rows, :], T))
            ds = p * (dp - _lanes_to(dl_ref[rows, :], T))
            dv_sc[...] += _dot(p, dov, TN)
            dk_sc[...] += _dot(ds, q, TN)
            dq_ref[rows, :] += _dot(ds, k_ref[...], NN)

        def masked(s):
            return jnp.where(_chunk_mask(T), s, NEG)

        @pl.when(kj == nt - 1)
        def _():
            s, dp = products(kj)
            update(masked(s), dp, kj)

        @pl.when(kj < nt - 1)
        def _():
            s, dp = products(kj)
            s_a[...], dp_a[...] = masked(s), dp
            s_b[...], dp_b[...] = products(kj + 1)
            update(s_a[...], dp_a[...], kj)
            s_a[...], dp_a[...] = products(jnp.minimum(kj + 2, nt - 1))
            update(s_b[...], dp_b[...], kj + 1)

            def pair(j, carry):
                t0 = kj + 2 * j
                s_b[...], dp_b[...] = products(t0 + 1)
                update(s_a[...], dp_a[...], t0)
                s_a[...], dp_a[...] = products(jnp.minimum(t0 + 2, nt - 1))
                update(s_b[...], dp_b[...], t0 + 1)
                return carry

            lax.fori_loop(1, (nt - kj) // 2, pair, 0)

            @pl.when((nt - kj) % 2 == 1)
            def _():
                update(s_a[...], dp_a[...], nt - 1)
        dkv_ref[:, :QK_NOPE] = (dk_sc[:, :QK_NOPE] * (1.0 / LOG2E)).astype(BF16)
        dkv_ref[:, QK_NOPE:] = dv_sc[...].astype(BF16)
        dkpe_ref[...] = dk_sc[:, QK_NOPE:] * (1.0 / LOG2E)

        @pl.when(kj == nt - 1)
        def _():
            dqa_ref[:, :QK_NOPE] = (dq_ref[:, :QK_NOPE] * scale).astype(BF16)
            dqa_ref[:, QK_NOPE:] = (_rope_pe_t(dq_ref[:, QK_NOPE:], c_ref[...], s1_ref[...], s2_ref[...])
                                    * scale).astype(BF16)

    stat = pl.BlockSpec((None, S, LANES), lambda h, j: (h, 0, 0))
    tab = pl.BlockSpec((S, LANES), lambda h, j: (0, 0))
    return _pallas(
        body, name="attn_bwd", grid=(MH, nt),
        in_specs=[pl.BlockSpec((S, 2 * LANES), lambda h, j: (0, h)),
                  pl.BlockSpec((T, 2 * LANES), lambda h, j: (j, h)),
                  pl.BlockSpec((T, LANES), lambda h, j: (j, h)),
                  pl.BlockSpec((S, LANES), lambda h, j: (0, h)), stat, stat, tab, tab, tab] + [_ANY] * n_dep,
        out_specs=[pl.BlockSpec((S, 2 * LANES), lambda h, j: (0, h)),
                   pl.BlockSpec((T, 2 * LANES), lambda h, j: (j, h)),
                   pl.BlockSpec((T, LANES), lambda h, j: (j, h))],
        out_shape=[jax.ShapeDtypeStruct((S, MH * 2 * LANES), BF16), jax.ShapeDtypeStruct((S, MH * 2 * LANES), BF16),
                   jax.ShapeDtypeStruct((S, MH * LANES), F32)],
        scratch_shapes=[pltpu.VMEM((S, 2 * LANES), F32), pltpu.VMEM((T, 2 * LANES), F32), pltpu.VMEM((T, LANES), F32)]
        + [pltpu.VMEM((T, T), F32)] * 4,
        compiler_params=_params(("parallel", "arbitrary")),
    )(qf, kf, vb, dob, lse2, delta, *tabs, *deps)


def _kpe_sum(dkpe_h, tabs, MH, *, tr=256):
    S = dkpe_h.shape[0]
    tr = _tile(S, tr)

    def body(dk_ref, c_ref, s1_ref, s2_ref, dkpe_ref):
        tot = dk_ref[:, :LANES]
        for h in range(1, MH):
            tot = tot + dk_ref[:, h * LANES:(h + 1) * LANES]
        dkpe_ref[...] = _rope_pe_t(tot, c_ref[...], s1_ref[...], s2_ref[...]).astype(BF16)

    tab = pl.BlockSpec((tr, LANES), lambda i: (i, 0))
    return _pallas(
        body, name="kpe_sum", grid=(S // tr,),
        in_specs=[pl.BlockSpec((tr, MH * LANES), lambda i: (i, 0)), tab, tab, tab],
        out_specs=tab, out_shape=jax.ShapeDtypeStruct((S, LANES), BF16),
        compiler_params=_params(("parallel",)),
    )(dkpe_h, *tabs)


ROW_ALIGN = 16


def _blk(R, C, block_bytes=2 << 20):
    cap = max(ROW_ALIGN, block_bytes // (C * 4))
    for t in range(min(R, cap) // ROW_ALIGN * ROW_ALIGN, LANES - 1, -ROW_ALIGN):
        if R % t == 0:
            return t, C
    if R <= cap:
        return R, C
    tc = C
    while R * tc * 4 > block_bytes and tc % (2 * LANES) == 0:
        tc //= 2
    return R, tc


def _rows_call(fn, ins, out_dtypes, *, name):
    R, C = ins[0].shape
    tr, tc = _blk(R, C)
    n_in = len(ins)

    def body(*refs):
        vals = fn(*[r[...] for r in refs[:n_in]])
        for r, v in zip(refs[n_in:], vals):
            r[...] = v.astype(r.dtype)

    blk = pl.BlockSpec((tr, tc), lambda i, j: (i, j))
    res = _pallas(
        body, name=name, grid=(R // tr, C // tc), in_specs=[blk] * n_in, out_specs=[blk] * len(out_dtypes),
        out_shape=[jax.ShapeDtypeStruct((R, C), d) for d in out_dtypes],
        compiler_params=_params(("parallel", "parallel")),
    )(*ins)
    return res


def _adamw_vals(w, g, m, v):
    m = ADAM_B1 * m + (1.0 - ADAM_B1) * g
    v = ADAM_B2 * v + (1.0 - ADAM_B2) * (g * g)
    m_hat = m / (1.0 - ADAM_B1 ** ADAM_STEP)
    v_hat = v / (1.0 - ADAM_B2 ** ADAM_STEP)
    delta = -ADAM_LR * (m_hat / (jnp.sqrt(v_hat) + ADAM_EPS) + ADAM_WD * w)
    return delta, m, v


def _sum_pair(p, theirs, place, *, name):
    _, R, C = p.shape
    R2 = R // 2
    tr, tc = _blk(R2, C)
    p4 = p.reshape(N_CHIPS, 2, R2, C)

    def body(place_ref, a_ref, b_ref, o_ref):
        o_ref[...] = (a_ref[...].astype(F32) + b_ref[...].astype(F32)).astype(BF16)

    spec = pltpu.PrefetchScalarGridSpec(
        num_scalar_prefetch=1, grid=(N_CHIPS, R2 // tr, C // tc),
        in_specs=[pl.BlockSpec((None, None, tr, tc), lambda q, i, j, pr: (q, pr[0], i, j)),
                  pl.BlockSpec((None, tr, tc), lambda q, i, j, pr: (q, i, j))],
        out_specs=pl.BlockSpec((None, tr, tc), lambda q, i, j, pr: (q, i, j)))
    return _pallas(body, name=name, grid_spec=spec, out_shape=jax.ShapeDtypeStruct((N_CHIPS, R2, C), BF16),
                   compiler_params=_params(("parallel", "parallel", "parallel")))(place, p4, theirs)


def _sum_chips(p, theirs, recv, place, *, name):
    _, R, C = p.shape
    R2 = R // 2
    tr, tc = _blk(R2, C)
    p4 = p.reshape(N_CHIPS, 2, R2, C)

    def body(place_ref, a_ref, b_ref, r0_ref, r1_ref, r2_ref, o_ref):
        own = a_ref[...].astype(F32) + b_ref[...].astype(F32)
        o_ref[...] = ((own + r0_ref[...].astype(F32)) + r1_ref[...].astype(F32)) + r2_ref[...].astype(F32)

    def slot(k):
        return pl.BlockSpec((None, tr, tc), lambda i, j, pr: (k, i, j))

    spec = pltpu.PrefetchScalarGridSpec(
        num_scalar_prefetch=1, grid=(R2 // tr, C // tc),
        in_specs=[pl.BlockSpec((None, None, tr, tc), lambda i, j, pr: (pr[1], pr[0], i, j)),
                  pl.BlockSpec((None, tr, tc), lambda i, j, pr: (pr[1], i, j)), slot(0), slot(1), slot(2)],
        out_specs=pl.BlockSpec((None, tr, tc), lambda i, j, pr: (pr[0], i, j)))
    return _pallas(body, name=name, grid_spec=spec, out_shape=jax.ShapeDtypeStruct((2, R2, C), F32),
                   compiler_params=_params(("parallel", "parallel")))(place, p4, theirs, recv, recv, recv)


def _me():
    return lax.axis_index("x"), lax.axis_index("y"), lax.axis_index("c")


def _other_chips(x, y):
    return [(1 - x, y), (x, 1 - y), (1 - x, 1 - y)]


def _rcopy(src, dst, ssem, rsem, dev):
    return pltpu.make_async_remote_copy(src_ref=src, dst_ref=dst, send_sem=ssem, recv_sem=rsem,
                                        device_id=dev, device_id_type=MESH)


def _cast_into_slot(w, place, *, name, rows=None, deps=()):
    R, C = w.shape
    rows = R if rows is None else rows
    tr, tc = _blk(R, C)

    def body(place_ref, w_ref, *rest):
        rest[-1][...] = w_ref[...].astype(BF16)

    spec = pltpu.PrefetchScalarGridSpec(
        num_scalar_prefetch=1, grid=(R // tr, C // tc),
        in_specs=[pl.BlockSpec((tr, tc), lambda i, j, pr: (i, j))] + [_ANY] * len(deps),
        out_specs=pl.BlockSpec((None, tr, tc), lambda i, j, pr: (pr[1], i, j)))
    out = _pallas(body, name=name, grid_spec=spec, out_shape=jax.ShapeDtypeStruct((N_CHIPS, rows, C), BF16),
                  compiler_params=_params(("parallel", "parallel")))(place, w, *deps)
    return out.reshape(N_CHIPS, 2, rows // 2, C)


def _gather_ici_plan(bufs):
    x, y, c = _me()
    j = 2 * x + y
    plan = []
    for i, buf in enumerate(bufs):
        for k, (px, py) in enumerate(_other_chips(x, y)):
            plan.append((3 * i + k, buf.at[j, c], buf.at[j, c], (px, py, c)))
    return plan


def _forward_halves(bufs, *, name):
    n = len(bufs)

    def body(*refs):
        outs = refs[n:2 * n]
        ssem, rsem = refs[2 * n:]
        x, y, c = _me()
        sib = (x, y, 1 - c)
        cps = []
        for i in range(n):
            for k, (px, py) in enumerate(_other_chips(x, y)):
                slot = outs[i].at[2 * px + py, c]
                r = _rcopy(slot, slot, ssem.at[3 * i + k], rsem.at[3 * i + k], sib)
                r.start()
                cps.append(r)
        for r in cps:
            r.wait()

    return _pallas(
        body, name=name, in_specs=[_ANY] * n, out_specs=[_ANY] * n,
        out_shape=[jax.ShapeDtypeStruct(b.shape, b.dtype) for b in bufs],
        scratch_shapes=[pltpu.SemaphoreType.DMA((3 * n,))] * 2,
        input_output_aliases={i: i for i in range(n)},
        compiler_params=pltpu.CompilerParams(has_side_effects=True),
    )(*bufs)


_HBM = pl.BlockSpec(memory_space=pltpu.HBM)
_SEM = pl.BlockSpec(memory_space=pltpu.SEMAPHORE)
_EFFECT = pltpu.SideEffectType.DATAFLOW_SIDE_EFFECTING


def _split_start(bufs, plan, n_copies, *, name):
    n = len(bufs)

    def body(*refs):
        ssem, rsem = refs[n], refs[n + 1]
        for s, src, dst, dev in plan(refs[:n]):
            _rcopy(src, dst, ssem.at[s], rsem.at[s], dev).start()
        refs[-1][...] = jnp.zeros_like(refs[-1])

    res = _pallas(
        body, name=name, in_specs=[_HBM] * n,
        out_specs=(_SEM, _SEM, *[_HBM] * n, pl.BlockSpec(memory_space=pltpu.VMEM)),
        out_shape=(pltpu.SemaphoreType.DMA((n_copies,)), pltpu.SemaphoreType.DMA((n_copies,)),
                   *[pltpu.HBM(b.shape, b.dtype) for b in bufs], jax.ShapeDtypeStruct((8, LANES), F32)),
        input_output_aliases={i: 2 + i for i in range(n)},
        compiler_params=pltpu.CompilerParams(has_side_effects=_EFFECT),
    )(*[pltpu.with_memory_space_constraint(b, pltpu.HBM) for b in bufs])
    return res[0], res[1], list(res[2:2 + n]), res[-1]


def _split_wait(ssem, rsem, bufs, after, plan, *, name):
    n = len(bufs)

    def body(*refs):
        ssem_ref, rsem_ref = refs[n], refs[n + 1]
        for s, src, dst, dev in plan(refs[:n]):
            cp = _rcopy(src, dst, ssem_ref.at[s], rsem_ref.at[s], dev)
            cp.wait_send()
            cp.wait_recv()

    return list(_pallas(
        body, name=name, in_specs=[_HBM] * n + [_SEM, _SEM, _ANY], out_specs=[_HBM] * n,
        out_shape=[pltpu.HBM(b.shape, b.dtype) for b in bufs],
        input_output_aliases={i: i for i in range(n)},
        compiler_params=pltpu.CompilerParams(has_side_effects=_EFFECT),
    )(*bufs, ssem, rsem, after))


def _swap_plan(n):
    def plan(bufs):
        x, y, c = _me()
        return [(i, bufs[i].at[:, 1 - c], bufs[n + i], (x, y, 1 - c)) for i in range(n)]
    return plan


def _scatter_plan(n):
    def plan(bufs):
        x, y, c = _me()
        out = []
        for i in range(n):
            for k, (px, py) in enumerate(_other_chips(x, y)):
                out.append((3 * i + k, bufs[i].at[2 * px + py], bufs[n + i].at[k], (px, py, c)))
        return out
    return plan


def _swap_halves(grads, *, name):
    n = len(grads)
    views = [g.reshape(N_CHIPS, 2, g.shape[1] // 2, g.shape[2]) for g in grads]

    def body(*refs):
        ins, outs = refs[:n], refs[n:2 * n]
        ssem, rsem = refs[2 * n:]
        x, y, c = _me()
        sib = (x, y, 1 - c)
        cps = []
        for i in range(n):
            r = _rcopy(ins[i].at[:, 1 - c], outs[i], ssem.at[i], rsem.at[i], sib)
            r.start()
            cps.append(r)
        for r in cps:
            r.wait()

    return _pallas(
        body, name=name, in_specs=[_ANY] * n, out_specs=[_ANY] * n,
        out_shape=[jax.ShapeDtypeStruct((N_CHIPS,) + v.shape[2:], v.dtype) for v in views],
        scratch_shapes=[pltpu.SemaphoreType.DMA((n,)), pltpu.SemaphoreType.DMA((n,))],
        compiler_params=pltpu.CompilerParams(has_side_effects=True),
    )(*views)


def _join_halves(halves, *, name):
    n = len(halves)

    def body(*refs):
        outs = refs[n:2 * n]
        ssem, rsem = refs[2 * n:]
        x, y, c = _me()
        sib = (x, y, 1 - c)
        cps = []
        for i in range(n):
            r = _rcopy(outs[i].at[c], outs[i].at[c], ssem.at[i], rsem.at[i], sib)
            r.start()
            cps.append(r)
        for r in cps:
            r.wait()

    return _pallas(
        body, name=name, in_specs=[_ANY] * n, out_specs=[_ANY] * n,
        out_shape=[jax.ShapeDtypeStruct(h.shape, h.dtype) for h in halves],
        scratch_shapes=[pltpu.SemaphoreType.DMA((n,)), pltpu.SemaphoreType.DMA((n,))],
        input_output_aliases={i: i for i in range(n)},
        compiler_params=pltpu.CompilerParams(has_side_effects=True),
    )(*halves)


def _allreduce_small(parts, loss11):
    n = len(parts)
    widths = [p.shape[1] for p in parts]
    total = sum(widths) + LANES

    def body(*refs):
        o_ref, mine, buf, ssem, rsem = refs[n + 1:]
        x, y, c = _me()
        me = 4 * x + 2 * y + c
        off = 0
        for r, w in zip(refs[:n], widths):
            mine[:, off:off + w] = r[...]
            off += w
        mine[:, off:] = jnp.broadcast_to(refs[n][...], (1, LANES))
        buf[me] = mine[...]
        cps = []
        for k in range(1, 8):
            peer = (x ^ (k >> 2), y ^ ((k >> 1) & 1), c ^ (k & 1))
            r = _rcopy(mine, buf.at[me], ssem.at[k - 1], rsem.at[k - 1], peer)
            r.start()
            cps.append(r)
        for k in range(1, 8):
            peer = (x ^ (k >> 2), y ^ ((k >> 1) & 1), c ^ (k & 1))
            pid = 4 * peer[0] + 2 * peer[1] + peer[2]
            _rcopy(mine, buf.at[pid], ssem.at[k - 1], rsem.at[k - 1], peer).wait_recv()
        for r in cps:
            r.wait_send()
        tot = buf[0]
        for d in range(1, 8):
            tot = tot + buf[d]
        o_ref[...] = tot

    vm = pl.BlockSpec(memory_space=pltpu.VMEM)
    return _pallas(
        body, name="allreduce_small", in_specs=[vm] * (n + 1), out_specs=vm,
        out_shape=jax.ShapeDtypeStruct((1, total), F32),
        scratch_shapes=[pltpu.VMEM((1, total), F32), pltpu.VMEM((8, 1, total), F32),
                        pltpu.SemaphoreType.DMA((7,)), pltpu.SemaphoreType.DMA((7,))],
        compiler_params=pltpu.CompilerParams(has_side_effects=True),
    )(*parts, loss11)


def _adamw_small(red, ws, ms, vs):
    n = len(ws)

    def body(*refs):
        red_ref = refs[0]
        outs = refs[1 + 3 * n:]
        off = 0
        for i in range(n):
            w = refs[1 + i].shape[1]
            g = red_ref[:, off:off + w]
            d, m, v = _adamw_vals(refs[1 + i][...], g, refs[1 + n + i][...], refs[1 + 2 * n + i][...])
            for o, val in zip(outs[4 * i:4 * i + 4], (g, d, m, v)):
                o[...] = val
            off += w

    vm = pl.BlockSpec(memory_space=pltpu.VMEM)
    res = _pallas(
        body, name="adamw_small", in_specs=[vm] * (1 + 3 * n), out_specs=[vm] * (4 * n),
        out_shape=[jax.ShapeDtypeStruct(w.shape, F32) for w in ws for _ in range(4)],
    )(red, *ws, *ms, *vs)
    return [res[4 * i:4 * i + 4] for i in range(n)]


def _rope_tables(positions, S):
    pos = positions.reshape(S, 1).astype(F32)
    half = RET_QK // 2
    inv = ROPE_THETA ** (-jnp.arange(half, dtype=F32) / half)
    ang = pos * inv
    cosr = jnp.concatenate([jnp.cos(ang), jnp.cos(ang)], axis=1)
    sinr = jnp.concatenate([-jnp.sin(ang), jnp.sin(ang)], axis=1)
    half = QK_ROPE // 2
    inv = ROPE_THETA ** (-jnp.arange(half, dtype=F32) / half)
    ang = pos * inv
    z = jnp.zeros((S, half), F32)
    c = jnp.concatenate([jnp.cos(ang), jnp.cos(ang), z, z], axis=1)
    s1 = jnp.concatenate([-jnp.sin(ang), z, z, z], axis=1)
    s2 = jnp.concatenate([z, jnp.sin(ang), z, z], axis=1)
    return cosr, sinr, (c, s1, s2)


def _cat_cols(g):
    return jnp.concatenate([g[j] for j in range(N_CHIPS)], axis=1)


def _split_cols(w):
    return jnp.stack(jnp.split(w, N_CHIPS, axis=1))


def kernel(x, positions, norm_mix_g, w_in, ret_norm_g, w_ret_o, q_a_norm_g, w_q_b, kv_a_norm_g, w_kv_b, w_mla_o, w_out, norm_mlp_g, w_up, w_down, norm_f_g, loss_target, m_norm_mix_g, m_w_in, m_ret_norm_g, m_w_ret_o, m_q_a_norm_g, m_w_q_b, m_kv_a_norm_g, m_w_kv_b, m_w_mla_o, m_w_out, m_norm_mlp_g, m_w_up, m_w_down, m_norm_f_g, v_norm_mix_g, v_w_in, v_ret_norm_g, v_w_ret_o, v_q_a_norm_g, v_w_q_b, v_kv_a_norm_g, v_w_kv_b, v_w_mla_o, v_w_out, v_norm_mlp_g, v_w_up, v_w_down, v_norm_f_g):
    S, D = x.shape[1], x.shape[2]
    RVW = w_ret_o.shape[1] * N_CHIPS
    RH = RVW // RET_V
    RQW = RH * RET_QK
    MVW = w_mla_o.shape[1] * N_CHIPS
    MH = MVW // V_HEAD
    QL, KVL = w_q_b.shape[1], w_kv_b.shape[1]
    T_RET = _tile(S, 256)
    T_ATT = _tile(S, 512)

    xs = x.reshape(S, D)
    tgt = loss_target.reshape(S, D)
    cosr, sinr, pe_tabs = _rope_tables(positions, S)
    lgam = jnp.log(1.0 - 2.0 ** (-5.0 - jnp.arange(RH, dtype=F32)))
    lgam = jnp.broadcast_to(lgam[:, None, None], (RH, 8, LANES))

    big = ("w_in", "w_ret_o", "w_q_b", "w_kv_b", "w_mla_o", "w_out", "w_up", "w_down")
    w_sh = dict(w_in=w_in[0].T, w_ret_o=w_ret_o[0], w_q_b=w_q_b[0], w_kv_b=w_kv_b[0], w_mla_o=w_mla_o[0],
                w_out=w_out[0], w_up=w_up[0], w_down=w_down[0])
    m_sh = dict(w_in=m_w_in[0].T, w_ret_o=m_w_ret_o[0], w_q_b=m_w_q_b[0], w_kv_b=m_w_kv_b[0],
                w_mla_o=m_w_mla_o[0], w_out=m_w_out[0], w_up=m_w_up[0], w_down=m_w_down[0])
    v_sh = dict(w_in=v_w_in[0].T, w_ret_o=v_w_ret_o[0], w_q_b=v_w_q_b[0], w_kv_b=v_w_kv_b[0],
                w_mla_o=v_w_mla_o[0], w_out=v_w_out[0], w_up=v_w_up[0], w_down=v_w_down[0])
    col_sharded = ("w_q_b", "w_kv_b", "w_up")
    c_sh = w_in.shape[2]
    c_pad = -(-c_sh // 64) * 64
    place = jnp.stack([lax.axis_index("c"), 2 * lax.axis_index("x") + lax.axis_index("y")]).astype(jnp.int32)

    def whole(k, g):
        g = g.reshape(N_CHIPS, w_sh[k].shape[0], w_sh[k].shape[1])
        if k == "w_up":
            return g
        return _cat_cols(g) if k in col_sharded else g.reshape(-1, g.shape[2])

    first = ("w_in", "w_q_b", "w_kv_b")
    later = ("w_ret_o", "w_mla_o", "w_out", "w_up", "w_down")
    first_bufs = [_cast_into_slot(w_sh[k], place, name="cast_" + k, rows=c_pad if k == "w_in" else None)
                  for k in first]
    first_ssem, first_rsem, first_bufs, first_token = _split_start(
        first_bufs, _gather_ici_plan, 3 * len(first), name="gather_first_start")
    later_bufs = [_cast_into_slot(w_sh[k], place, name="cast_" + k, deps=(first_token,)) for k in later[:-1]]
    first_bufs = _split_wait(first_ssem, first_rsem, first_bufs, later_bufs[-1], _gather_ici_plan,
                             name="gather_first_wait")
    got = _forward_halves(first_bufs, name="gather_first_forward")
    full = {k: whole(k, g) for k, g in zip(first[1:], got[1:])}
    later_bufs.append(_cast_into_slot(w_sh[later[-1]], place, name="cast_" + later[-1], deps=(got[0],)))
    later_ssem, later_rsem, later_bufs, later_token = _split_start(
        later_bufs, _gather_ici_plan, 3 * len(later), name="gather_later_start")

    o_rq, o_rk, o_rv, o_rg = 0, RQW, 2 * RQW, 2 * RQW + RVW
    o_cq = 2 * RQW + 2 * RVW
    o_ckv, o_kpe = o_cq + QL, o_cq + QL + KVL
    o_gr = o_kpe + QK_ROPE
    o_gm = o_gr + D
    n_ret = RH * RET_HEAD_COLS
    off_gret, off_gmla, off_cq, off_ckv = n_ret, n_ret + D, n_ret + 2 * D, n_ret + 2 * D + QL
    n_a = off_ckv + KVL
    wi = got[0].reshape(N_CHIPS, c_pad, D)[:, :c_sh].reshape(N_CHIPS * c_sh, D)
    ret_rows = jnp.concatenate(
        [wi[o_rq:o_rk].reshape(RH, RET_QK, D), wi[o_rk:o_rv].reshape(RH, RET_QK, D),
         wi[o_rv:o_rg].reshape(RH, RET_V, D), wi[o_rg:o_cq].reshape(RH, RET_V, D)], axis=1).reshape(n_ret, D)
    wa = jnp.concatenate([ret_rows, wi[o_gr:o_gm], wi[o_gm:], wi[o_cq:o_ckv], wi[o_ckv:o_kpe]], axis=0)
    wkpe = jnp.concatenate([wi[o_kpe:o_gr], jnp.zeros((LANES - QK_ROPE, D), BF16)], axis=0)
    wq = jnp.pad(full["w_q_b"].reshape(QL, MH, QK_NOPE + QK_ROPE),
                 ((0, 0), (0, 0), (0, LANES - QK_ROPE))).reshape(QL, MH * 2 * LANES)
    wkv = full["w_kv_b"]

    u, rstd0 = _rmsnorm_fwd(xs, norm_mix_g, name="norm_mix")
    proj = _mm(u, wa, mode="nt", outs=[F32], name="in_proj", deps=(later_token,))
    kpe = _mm(u, wkpe, mode="nt", outs=[F32], name="kpe_proj")
    ry, gated, states = _ret_fwd(proj, cosr, sinr, lgam, ret_norm_g, RH, T=T_RET)
    cqn, rstd_q = _rmsnorm_fwd(proj, q_a_norm_g, name="norm_q", width=QL, col=off_cq // QL)
    ckvn, rstd_kv = _rmsnorm_fwd(proj, kv_a_norm_g, name="norm_kv", width=KVL, col=off_ckv // KVL)
    qf, kf, vb = _qkv_proj(cqn, ckvn, wq, wkv, kpe, pe_tabs, MH)
    my, my_b, lse2 = _attn_fwd(qf, kf, vb, MH, T=T_ATT)
    later_bufs = _split_wait(later_ssem, later_rsem, later_bufs, my, _gather_ici_plan, name="gather_later_wait")
    later_bufs = _forward_halves(later_bufs, name="gather_later_forward")
    full.update({k: whole(k, g) for k, g in zip(later, later_bufs)})
    y_ret = _mm(gated, full["w_ret_o"], mode="nn", outs=[BF16], name="ret_o")
    y_mla, merged = _mm(my_b, full["w_mla_o"], mode="nn", outs=[BF16, BF16], name="mla_o",
                        epi=lambda acc, gr, gm, yr: (acc, _sigmoid(gr) * yr + _sigmoid(gm) * acc),
                        extras=((proj, off_gret), (proj, off_gmla), y_ret))
    h1 = _mm(merged, full["w_out"], mode="nn", outs=[F32], name="out_proj",
             epi=lambda acc, r: (acc + r,), extras=(xs,))
    n1, rstd1 = _rmsnorm_fwd(h1, norm_mlp_g, name="norm_mlp")

    def up_epi(acc):
        r = jnp.maximum(acc, 0.0)
        return acc, r * r

    z, act = _mm(n1, full["w_up"], mode="nn", outs=[F32, BF16], name="up_proj", epi=up_epi)
    h2 = _mm(act, full["w_down"], mode="nn", outs=[F32], name="down_proj",
             epi=lambda acc, r: (acc + r,), extras=(h1,))
    loss11, dh2, dh2_b, g_norm_f = _final_loss(h2, norm_f_g.reshape(1, D), tgt)

    dz = _mm(dh2_b, full["w_down"], mode="nt", outs=[BF16], name="down_bwd_x",
             epi=lambda acc, zz: (acc * (2.0 * jnp.maximum(zz, 0.0)),), extras=(z,))
    g_w_down = _mm(act, dh2_b, mode="tn", outs=[BF16], name="down_bwd_w")
    dn1 = _mm(dz, full["w_up"], mode="nt", outs=[F32], name="up_bwd_x")
    g_w_up = _mm(n1, dz, mode="tn", outs=[BF16], name="up_bwd_w", out_shards=True)

    def reduce_begin(tag, names, grads):
        pcs = [g if g.ndim == 3 else g.reshape(N_CHIPS, g.shape[0] // N_CHIPS, g.shape[1]) for g in grads]
        theirs = _swap_halves(pcs, name="swap_" + tag)
        sums = [_sum_pair(p, t, place, name="sum_pair_" + k) for k, p, t in zip(names, pcs, theirs)]
        return pcs, theirs, sums

    def scatter_begin(tag, sums):
        lands = [lax.empty((3,) + s.shape[1:], s.dtype) for s in sums]
        return _split_start(sums + lands, _scatter_plan(len(sums)), 3 * len(sums), name="scatter_" + tag + "_start")

    def swap_begin(tag, grads):
        views = [g if g.ndim == 3 else g.reshape(N_CHIPS, g.shape[0] // N_CHIPS, g.shape[1]) for g in grads]
        views = [v.reshape(N_CHIPS, 2, v.shape[1] // 2, v.shape[2]) for v in views]
        lands = [lax.empty((N_CHIPS,) + v.shape[2:], v.dtype) for v in views]
        return _split_start(views + lands, _swap_plan(len(views)), len(views), name="swap_" + tag + "_start")

    def swap_end(tag, names, handle, after):
        n = len(names)
        bufs = _split_wait(handle[0], handle[1], handle[2], after, _swap_plan(n), name="swap_" + tag + "_wait")
        pcs = [b.reshape(N_CHIPS, 2 * b.shape[2], b.shape[3]) for b in bufs[:n]]
        sums = [_sum_pair(p, t, place, name="sum_pair_" + k) for k, p, t in zip(names, pcs, bufs[n:])]
        return pcs, bufs[n:], sums

    g1 = ("w_up", "w_down")
    swap1 = swap_begin("g1", (g_w_up, g_w_down))
    dh1, g_norm_mlp, dh1_b = _rmsnorm_bwd(dn1, h1, rstd1, norm_mlp_g, name="norm_mlp_bwd", res=dh2,
                                          deps=(swap1[3],), bf16_copy=1)
    dmerged = _mm(dh1_b, full["w_out"], mode="nt", outs=[F32], name="out_bwd_x")
    pcs1, theirs1, sums1 = swap_end("g1", g1, swap1, dmerged)
    ssem1, rsem1, bufs1, token1 = scatter_begin("g1", sums1)
    g_w_out = _mm(merged, dh1_b, mode="tn", outs=[BF16], name="out_bwd_w", deps=(token1,))
    dproj, dy_ret, dy_mla = _merge_bwd(dmerged, proj, y_ret, y_mla, D, off_gret)
    dgated = _mm(dy_ret, full["w_ret_o"], mode="nt", outs=[F32], name="ret_o_bwd_x")
    g_w_ret_o = _mm(gated, dy_ret, mode="tn", outs=[BF16], name="ret_o_bwd_w")
    dproj, g_ret_norm = _ret_bwd(proj, cosr, sinr, lgam, ret_norm_g, ry, dgated, states, dproj, RH, T=T_RET)
    def delta_epi(acc, o):
        rows = acc.shape[0]
        return acc, [jnp.broadcast_to(jnp.sum(acc[:, lo:lo + V_HEAD] * o[:, lo:lo + V_HEAD], axis=-1, keepdims=True),
                                      (rows, LANES)) for lo in range(0, acc.shape[1], V_HEAD)]

    dob, delta = _mm(dy_mla, full["w_mla_o"], mode="nt", outs=[BF16], name="mla_o_bwd_x", epi=delta_epi,
                     extras=(my,), more_outs=lambda tm, tn: [
                         (jax.ShapeDtypeStruct((MH, S, LANES), F32),
                          pl.BlockSpec((tn // V_HEAD, tm, LANES), lambda i, j, k: (j, i, 0)))])
    g_w_mla_o = _mm(my_b, dy_mla, mode="tn", outs=[BF16], name="mla_o_bwd_w")
    g2 = ("w_out", "w_ret_o", "w_mla_o")
    swap2 = swap_begin("g2", (g_w_out, g_w_ret_o, g_w_mla_o))
    dq_all, dkv_all, dkpe_h = _attn_bwd(qf, kf, vb, dob, lse2, delta, pe_tabs, MH, T=T_ATT, deps=(swap2[3],))
    pcs2, theirs2, sums2 = swap_end("g2", g2, swap2, dkv_all)
    ssem2, rsem2, bufs2, token2 = scatter_begin("g2", sums2)
    dkpe = _kpe_sum(dkpe_h, pe_tabs, MH)
    dcqn = _mm(dq_all, wq, mode="nt", outs=[F32], name="q_bwd_x", deps=(token2,))
    g_wq = _mm(cqn, dq_all, mode="tn", outs=[BF16], name="q_bwd_w")
    dckvn = _mm(dkv_all, wkv, mode="nt", outs=[F32], name="kv_bwd_x")
    g_wkv = _mm(ckvn, dkv_all, mode="tn", outs=[BF16], name="kv_bwd_w")
    dproj, g_q_a = _rmsnorm_bwd(dcqn, proj, rstd_q, q_a_norm_g, name="norm_q_bwd", into=(dproj, off_cq // QL),
                                width=QL, col=off_cq // QL)
    dproj, g_kv_a = _rmsnorm_bwd(dckvn, proj, rstd_kv, kv_a_norm_g, name="norm_kv_bwd", into=(dproj, off_ckv // KVL),
                                 width=KVL, col=off_ckv // KVL)
    g_wa = _mm(dproj, u, mode="tn", outs=[BF16], name="in_bwd_w")
    g_wkpe = _mm(dkpe, u, mode="tn", outs=[BF16], name="kpe_bwd_w")

    gr = g_wa[:n_ret].reshape(RH, RET_HEAD_COLS, D)
    g_w_in = jnp.concatenate(
        [gr[:, _RQ].reshape(RQW, D), gr[:, _RK].reshape(RQW, D), gr[:, _RV].reshape(RVW, D),
         gr[:, _RG].reshape(RVW, D), g_wa[off_cq:], g_wkpe[:QK_ROPE], g_wa[off_gret:off_cq]], axis=0)
    g_w_in = jnp.pad(g_w_in.reshape(N_CHIPS, c_sh, D), ((0, 0), (0, c_pad - c_sh), (0, 0)))
    gq = g_wq.reshape(QL, MH, 2 * LANES)[:, :, :QK_NOPE + QK_ROPE].reshape(QL, MH * (QK_NOPE + QK_ROPE))
    g3 = ("w_in", "w_q_b", "w_kv_b")
    pcs3, theirs3, sums3 = reduce_begin("g3", g3, (g_w_in, _split_cols(gq), _split_cols(g_wkv)))
    ssem3, rsem3, bufs3, token3 = scatter_begin("g3", sums3)
    du = _mm(dproj, wa, mode="nn", outs=[F32], name="in_bwd_x", tk=2816, tail=(dkpe, wkpe), deps=(token3,))
    dx, g_norm_mix = _rmsnorm_bwd(du, xs, rstd0, norm_mix_g, name="norm_mix_bwd", res=dh1)

    bufs1 = _split_wait(ssem1, rsem1, bufs1, dx, _scatter_plan(len(g1)), name="scatter_g1_wait")
    bufs2 = _split_wait(ssem2, rsem2, bufs2, dx, _scatter_plan(len(g2)), name="scatter_g2_wait")
    bufs3 = _split_wait(ssem3, rsem3, bufs3, dx, _scatter_plan(len(g3)), name="scatter_g3_wait")
    recv1, recv2, recv3 = bufs1[len(g1):], bufs2[len(g2):], bufs3[len(g3):]
    halves = {}
    for names, pcs, theirs, recv in ((g1, pcs1, theirs1, recv1), (g2, pcs2, theirs2, recv2), (g3, pcs3, theirs3, recv3)):
        for k, p, t, r in zip(names, pcs, theirs, recv):
            halves[k] = _sum_chips(p, t, r, place, name="sum_chips_" + k)
    joined = _join_halves([halves[k] for k in big], name="join_halves")
    g_shard = {k: g.reshape(2 * g.shape[1], g.shape[2]) for k, g in zip(big, joined)}

    small = ("norm_mix_g", "ret_norm_g", "q_a_norm_g", "kv_a_norm_g", "norm_mlp_g", "norm_f_g")
    g_small = [g_norm_mix, g_ret_norm, g_q_a, g_kv_a, g_norm_mlp, g_norm_f]
    red = _allreduce_small(g_small, loss11)
    loss = red[0, red.shape[1] - 1]
    w_small = [norm_mix_g, ret_norm_g, q_a_norm_g, kv_a_norm_g, norm_mlp_g, norm_f_g]
    m_small = [m_norm_mix_g, m_ret_norm_g, m_q_a_norm_g, m_kv_a_norm_g, m_norm_mlp_g, m_norm_f_g]
    v_small = [v_norm_mix_g, v_ret_norm_g, v_q_a_norm_g, v_kv_a_norm_g, v_norm_mlp_g, v_norm_f_g]
    row = lambda a: a.reshape(1, -1)
    upd = _adamw_small(red, [row(a) for a in w_small], [row(a) for a in m_small], [row(a) for a in v_small])
    out_g, out_d, out_m, out_v = {}, {}, {}, {}
    for k, wv, (g_, d_, m_, v_) in zip(small, w_small, upd):
        out_g[k], out_d[k], out_m[k], out_v[k] = [a.reshape(wv.shape) for a in (g_, d_, m_, v_)]

    for k in big:
        res = _rows_call(lambda w, g, m, v: (g,) + _adamw_vals(w, g, m, v),
                         [w_sh[k], g_shard[k], m_sh[k], v_sh[k]], [F32] * 4, name="adamw_" + k)
        if k == "w_in":
            res = [r.T for r in res]
        out_g[k], out_d[k], out_m[k], out_v[k] = [r[None] for r in res]

    order = ("norm_mix_g", "w_in", "ret_norm_g", "w_ret_o", "q_a_norm_g", "w_q_b", "kv_a_norm_g", "w_kv_b",
             "w_mla_o", "w_out", "norm_mlp_g", "w_up", "w_down", "norm_f_g")
    return (loss, dx.reshape(1, S, D), *[out_g[k] for k in order], *[out_d[k] for k in order],
            *[out_m[k] for k in order], *[out_v[k] for k in order])
```

```python
import math

import jax
import jax.numpy as jnp
from jax import lax
from jax.experimental import pallas as pl
from jax.experimental.pallas import tpu as pltpu

F32 = jnp.float32
BF16 = jnp.bfloat16

EPS = 1e-6
ROPE_THETA = 10000.0
CHUNK = 64
RET_QK = 128
RET_V = 256
RET_HEAD_COLS = 2 * RET_QK + 2 * RET_V
QK_NOPE = 128
QK_ROPE = 64
V_HEAD = 128
LANES = 128
LOG2E = math.log2(math.e)

ADAM_LR = 0.001
ADAM_B1 = 0.9
ADAM_B2 = 0.999
ADAM_EPS = 1e-08
ADAM_WD = 0.01
ADAM_STEP = 10

N_CHIPS = 4
VMEM_LIMIT = 56 * 1024 * 1024
MESH = pl.DeviceIdType.MESH
NEG = -1e30


def _pallas(body, **kw):
    return pl.pallas_call(body, **kw)


def _params(sem=None):
    return pltpu.CompilerParams(dimension_semantics=sem, vmem_limit_bytes=VMEM_LIMIT)


def _tile(n, want):
    t = min(n, want)
    while n % t:
        t //= 2
    return t


_ANY = pl.BlockSpec(memory_space=pl.ANY)
TN_BF16_TK = 4096


def _mm(a, b, *, mode, outs, name, epi=None, extras=(), deps=(), out_shards=False, more_outs=None, tail=None,
        tm=1024, tn=1024, tk=2048):
    shards = b.shape[0] if b.ndim == 3 else 1
    brows, bcols = b.shape[-2], b.shape[-1] * shards
    if mode == "nn":
        (M, K), N = a.shape, bcols
    elif mode == "nt":
        (M, K), N = a.shape, brows
    else:
        (K, M), N = a.shape, bcols
    if mode == "tn" and a.dtype == BF16 and b.dtype == BF16:
        tk = max(tk, TN_BF16_TK)
    tm = _tile(M, tm)
    tn = _tile(N // (shards if mode == "nn" else 1) // (N_CHIPS if out_shards else 1), tn)
    tk = _tile(K // (shards if mode == "nt" else 1), tk)
    nk = K // tk
    if mode == "nn":
        a_spec = pl.BlockSpec((tm, tk), lambda i, j, k: (i, k))
        dims = (((1,), (0,)), ((), ()))
        if shards > 1:
            per = N // shards // tn
            b_spec = pl.BlockSpec((None, tk, tn), lambda i, j, k: (j // per, k, j % per))
        else:
            b_spec = pl.BlockSpec((tk, tn), lambda i, j, k: (k, j))
    elif mode == "nt":
        a_spec = pl.BlockSpec((tm, tk), lambda i, j, k: (i, k))
        dims = (((1,), (1,)), ((), ()))
        if shards > 1:
            per = K // shards // tk
            b_spec = pl.BlockSpec((None, tn, tk), lambda i, j, k: (k // per, j, k % per))
        else:
            b_spec = pl.BlockSpec((tn, tk), lambda i, j, k: (j, k))
    else:
        assert shards == 1
        a_spec = pl.BlockSpec((tk, tm), lambda i, j, k: (k, i))
        b_spec = pl.BlockSpec((tk, tn), lambda i, j, k: (k, j))
        dims = (((0,), (0,)), ((), ()))
    if out_shards:
        assert not extras
        oper = N // N_CHIPS // tn
        o_spec = pl.BlockSpec((None, tm, tn), lambda i, j, k: (j // oper, i, j % oper))
        o_shape = (N_CHIPS, M, N // N_CHIPS)
    else:
        o_spec = pl.BlockSpec((tm, tn), lambda i, j, k: (i, j))
        o_shape = (M, N)
    more = [] if more_outs is None else more_outs(tm, tn)
    ex_arrays = [e[0] if isinstance(e, tuple) else e for e in extras]
    ex_specs = [pl.BlockSpec((tm, tn), lambda i, j, k, off=e[1] // tn: (i, off + j)) if isinstance(e, tuple)
                else o_spec for e in extras]
    n_ex, n_out, n_dep = len(extras), len(outs) + len(more), len(deps)
    if epi is None:
        epi = lambda acc: (acc,)
    tails, tail_specs = [], []
    if tail is not None:
        assert mode == "nn"
        tails = list(tail)
        k2 = tail[0].shape[1]
        tail_specs = [pl.BlockSpec((tm, k2), lambda i, j, k: (i, 0)), pl.BlockSpec((k2, tn), lambda i, j, k: (0, j))]
    n_tail = len(tails)

    def body(*refs):
        a_ref, b_ref = refs[0], refs[1]
        ex_refs = refs[2:2 + n_ex]
        t_refs = refs[2 + n_ex:2 + n_ex + n_tail]
        first_out = 2 + n_ex + n_tail + n_dep
        o_refs = refs[first_out:first_out + n_out]
        part = lax.dot_general(a_ref[...].astype(BF16), b_ref[...].astype(BF16), dims,
                               preferred_element_type=F32)

        def finish(acc):
            if n_tail:
                acc = acc + lax.dot_general(t_refs[0][...].astype(BF16), t_refs[1][...].astype(BF16), dims,
                                            preferred_element_type=F32)
            vals = epi(acc, *[r[...] for r in ex_refs])
            for r, v in zip(o_refs, vals):
                if isinstance(v, (list, tuple)):
                    for lead, piece in enumerate(v):
                        r[lead] = piece.astype(r.dtype)
                else:
                    r[...] = v.astype(r.dtype)

        if nk == 1:
            finish(part)
        else:
            acc_ref = refs[-1]
            k = pl.program_id(2)

            @pl.when(k == 0)
            def _():
                acc_ref[...] = part

            @pl.when(k > 0)
            def _():
                acc_ref[...] += part

            @pl.when(k == nk - 1)
            def _():
                finish(acc_ref[...])

    res = _pallas(
        body, name=name, grid=(M // tm, N // tn, nk),
        in_specs=[a_spec, b_spec] + ex_specs + tail_specs + [_ANY] * n_dep,
        out_specs=[o_spec] * len(outs) + [spec for _, spec in more],
        out_shape=[jax.ShapeDtypeStruct(o_shape, d) for d in outs] + [shape for shape, _ in more],
        scratch_shapes=[pltpu.VMEM((tm, tn), F32)] if nk > 1 else [],
        compiler_params=_params(("parallel", "parallel", "arbitrary")),
    )(a, b, *ex_arrays, *tails, *deps)
    return res[0] if n_out == 1 else res


def _rmsnorm_fwd(x, g, *, name, width=None, col=0, tr=256):
    S = x.shape[0]
    W = x.shape[1] if width is None else width
    tr = _tile(S, tr)

    def body(x_ref, g_ref, y_ref, r_ref):
        xv = x_ref[...]
        rstd = lax.rsqrt(jnp.mean(xv * xv, axis=-1, keepdims=True) + EPS)
        y_ref[...] = (xv * rstd * g_ref[...]).astype(BF16)
        r_ref[...] = rstd

    return _pallas(
        body, name=name, grid=(S // tr,),
        in_specs=[pl.BlockSpec((tr, W), lambda i: (i, col)), pl.BlockSpec((1, W), lambda i: (0, 0))],
        out_specs=[pl.BlockSpec((tr, W), lambda i: (i, 0)), pl.BlockSpec((tr, 1), lambda i: (i, 0))],
        out_shape=[jax.ShapeDtypeStruct((S, W), BF16), jax.ShapeDtypeStruct((S, 1), F32)],
        compiler_params=_params(("parallel",)),
    )(x, g)


def _rmsnorm_bwd(dy, x, rstd, g, *, name, res=None, into=None, deps=(), bf16_copy=0, width=None, col=0, tr=256):
    S = x.shape[0]
    W = x.shape[1] if width is None else width
    tr = _tile(S, tr)
    has_res = res is not None

    def body(*refs):
        dy_ref, x_ref, r_ref, g_ref = refs[:4]
        dx_ref, dg_ref = refs[-2 - bf16_copy], refs[-1 - bf16_copy]
        rstd_v = r_ref[...]
        xhat = x_ref[...] * rstd_v
        dyv = dy_ref[...].astype(F32)
        dyg = dyv * g_ref[...]
        dx = rstd_v * (dyg - xhat * jnp.mean(dyg * xhat, axis=-1, keepdims=True))
        if has_res:
            dx = dx + refs[4][...]
        dx_ref[...] = dx.astype(dx_ref.dtype)
        if bf16_copy:
            refs[-1][...] = dx.astype(BF16)
        part = jnp.sum(dyv * xhat, axis=0, keepdims=True)

        @pl.when(pl.program_id(0) == 0)
        def _():
            dg_ref[...] = part

        @pl.when(pl.program_id(0) > 0)
        def _():
            dg_ref[...] += part

    row = pl.BlockSpec((tr, W), lambda i: (i, 0))
    ins = [dy, x, rstd, g] + ([res] if has_res else [])
    in_specs = [row, pl.BlockSpec((tr, W), lambda i: (i, col)), pl.BlockSpec((tr, 1), lambda i: (i, 0)),
                pl.BlockSpec((1, W), lambda i: (0, 0))] + ([row] if has_res else [])
    if into is None:
        dx_spec, dx_shape, alias = row, jax.ShapeDtypeStruct((S, W), F32), {}
    else:
        buf, col_out = into
        ins.append(buf)
        in_specs.append(_ANY)
        dx_spec = pl.BlockSpec((tr, W), lambda i: (i, col_out))
        dx_shape = jax.ShapeDtypeStruct(buf.shape, buf.dtype)
        alias = {len(ins) - 1: 0}
    ins += list(deps)
    in_specs += [_ANY] * len(deps)
    return _pallas(
        body, name=name, grid=(S // tr,), in_specs=in_specs,
        out_specs=[dx_spec, pl.BlockSpec((1, W), lambda i: (0, 0))] + [row] * bf16_copy,
        out_shape=[dx_shape, jax.ShapeDtypeStruct((1, W), F32)] + [jax.ShapeDtypeStruct((S, W), BF16)] * bf16_copy,
        input_output_aliases=alias,
        compiler_params=_params(("arbitrary",)),
    )(*ins)


def _final_loss(h2, g, target, *, tr=256):
    S, D = h2.shape
    tr = _tile(S, tr)

    def body(h_ref, g_ref, t_ref, loss_ref, dh_ref, dhb_ref, dg_ref):
        hv = h_ref[...]
        rstd = lax.rsqrt(jnp.mean(hv * hv, axis=-1, keepdims=True) + EPS)
        xhat = hv * rstd
        e = xhat * g_ref[...] - t_ref[...]
        lpart = (0.5 / D) * jnp.sum(jnp.sum(e * e, axis=-1, keepdims=True), axis=0, keepdims=True)
        dy = e * (1.0 / D)
        dyg = dy * g_ref[...]
        dh = rstd * (dyg - xhat * jnp.mean(dyg * xhat, axis=-1, keepdims=True))
        dh_ref[...] = dh
        dhb_ref[...] = dh.astype(BF16)
        gpart = jnp.sum(dy * xhat, axis=0, keepdims=True)

        @pl.when(pl.program_id(0) == 0)
        def _():
            loss_ref[...] = lpart
            dg_ref[...] = gpart

        @pl.when(pl.program_id(0) > 0)
        def _():
            loss_ref[...] += lpart
            dg_ref[...] += gpart

    row = pl.BlockSpec((tr, D), lambda i: (i, 0))
    vec = pl.BlockSpec((1, D), lambda i: (0, 0))
    return _pallas(
        body, name="final_loss", grid=(S // tr,), in_specs=[row, vec, row],
        out_specs=[pl.BlockSpec((1, 1), lambda i: (0, 0)), row, row, vec],
        out_shape=[jax.ShapeDtypeStruct((1, 1), F32), jax.ShapeDtypeStruct((S, D), F32),
                   jax.ShapeDtypeStruct((S, D), BF16), jax.ShapeDtypeStruct((1, D), F32)],
        compiler_params=_params(("arbitrary",)),
    )(h2, g, target)


def _sigmoid(v):
    return 1.0 / (1.0 + jnp.exp(-v))


def _merge_bwd(dmerged, proj, y_ret, y_mla, D, off_gret, *, tr=256):
    S = y_ret.shape[0]
    tr = _tile(S, tr)
    b0 = off_gret // D

    def body(dm_ref, g_ref, yr_ref, ym_ref, dp_ref, dyr_ref, dym_ref):
        dm = dm_ref[...]
        sg = _sigmoid(g_ref[...])

        @pl.when(pl.program_id(1) == 0)
        def _():
            dyr_ref[...] = (dm * sg).astype(BF16)
            dp_ref[...] = (dm * yr_ref[...] * sg * (1.0 - sg)).astype(BF16)

        @pl.when(pl.program_id(1) == 1)
        def _():
            dym_ref[...] = (dm * sg).astype(BF16)
            dp_ref[...] = (dm * ym_ref[...] * sg * (1.0 - sg)).astype(BF16)

    blk = pl.BlockSpec((tr, D), lambda i, j: (i, 0))
    return _pallas(
        body, name="merge_bwd", grid=(S // tr, 2),
        in_specs=[blk, pl.BlockSpec((tr, D), lambda i, j: (i, b0 + j)), blk, blk],
        out_specs=[pl.BlockSpec((tr, D), lambda i, j: (i, b0 + j)), blk, blk],
        out_shape=[jax.ShapeDtypeStruct(proj.shape, BF16), jax.ShapeDtypeStruct((S, D), BF16),
                   jax.ShapeDtypeStruct((S, D), BF16)],
        compiler_params=_params(("parallel", "arbitrary")),
    )(dmerged, proj, y_ret, y_mla)


def _rope128(t, cos_full, sin_signed):
    return t * cos_full + pltpu.roll(t, RET_QK // 2, 1) * sin_signed


def _rope128_t(d, cos_full, sin_signed):
    return d * cos_full + pltpu.roll(d * sin_signed, RET_QK // 2, 1)


def _ret_consts(lg, T):
    pos = lax.broadcasted_iota(jnp.int32, (T, 1), 0).astype(F32)
    qd = jnp.exp(lg * (pos + 1.0))
    kd = jnp.exp(lg * (T - 1.0 - pos))
    n = lax.broadcasted_iota(jnp.int32, (T, T), 0)
    m = lax.broadcasted_iota(jnp.int32, (T, T), 1)
    vis = (m // CHUNK) <= (n // CHUNK)
    dist = jnp.abs(n - m).astype(F32)
    decay = jnp.where(vis, jnp.exp(lg * dist), 0.0)
    cdec = jnp.exp(lg * float(T))
    return qd, kd, decay, cdec


def _dot(a, b, dims):
    return lax.dot_general(a.astype(BF16), b.astype(BF16), (dims, ((), ())), preferred_element_type=F32)


NN = ((1,), (0,))
NT = ((1,), (1,))
TN = ((0,), (0,))
_RQ = slice(0, RET_QK)
_RK = slice(RET_QK, 2 * RET_QK)
_RV = slice(2 * RET_QK, 2 * RET_QK + RET_V)
_RG = slice(2 * RET_QK + RET_V, RET_HEAD_COLS)


RET_GROUP = 8


def _head_cols(h, part):
    return slice(h * RET_HEAD_COLS + part.start, h * RET_HEAD_COLS + part.stop)


def _ret_fwd(proj, cosr, sinr, lgam, gain, RH, *, T):
    S = proj.shape[0]
    nb = S // T
    G = _tile(RH, RET_GROUP)
    heads = range(G)
    scale = RET_QK ** -0.5

    def body(p_ref, cos_ref, sin_ref, lg_ref, gain_ref, ry_ref, gated_ref, st_ref, state):
        b = pl.program_id(1)

        @pl.when(b == 0)
        def _():
            state[...] = jnp.zeros_like(state)

        consts = [_ret_consts(lg_ref[h, 0:1, 0:1], T) for h in heads]
        cosv, sinv = cos_ref[...], sin_ref[...]
        q = [_rope128(p_ref[:, _head_cols(h, _RQ)], cosv, sinv) for h in heads]
        k = [_rope128(p_ref[:, _head_cols(h, _RK)], cosv, sinv) * scale for h in heads]
        v = [p_ref[:, _head_cols(h, _RV)] for h in heads]
        sprev = [state[h] for h in heads]
        for h in heads:
            st_ref[h] = sprev[h]
        a = [_dot(q[h], k[h], NT) for h in heads]
        qs = [_dot(q[h] * consts[h][0], sprev[h], NN) for h in heads]
        kv = [_dot(k[h] * consts[h][1], v[h], TN) for h in heads]
        o = [_dot(a[h] * consts[h][2], v[h], NN) + qs[h] for h in heads]
        for h in heads:
            state[h] = sprev[h] * consts[h][3] + kv[h]
            vals = slice(h * RET_V, (h + 1) * RET_V)
            ry_ref[:, vals] = o[h]
            mu = jnp.mean(o[h], axis=-1, keepdims=True)
            oc = o[h] - mu
            var = jnp.mean(oc * oc, axis=-1, keepdims=True)
            t = oc * lax.rsqrt(var + EPS) * gain_ref[:, vals]
            gv = p_ref[:, _head_cols(h, _RG)]
            gated_ref[:, vals] = (t * (gv * _sigmoid(gv))).astype(BF16)

    return _pallas(
        body, name="ret_fwd", grid=(RH // G, nb),
        in_specs=[pl.BlockSpec((T, G * RET_HEAD_COLS), lambda h, b: (b, h)),
                  pl.BlockSpec((T, RET_QK), lambda h, b: (b, 0)),
                  pl.BlockSpec((T, RET_QK), lambda h, b: (b, 0)),
                  pl.BlockSpec((G, 8, LANES), lambda h, b: (h, 0, 0)),
                  pl.BlockSpec((1, G * RET_V), lambda h, b: (0, h))],
        out_specs=[pl.BlockSpec((T, G * RET_V), lambda h, b: (b, h)),
                   pl.BlockSpec((T, G * RET_V), lambda h, b: (b, h)),
                   pl.BlockSpec((G, None, RET_QK, RET_V), lambda h, b: (h, b, 0, 0))],
        out_shape=[jax.ShapeDtypeStruct((S, RH * RET_V), F32), jax.ShapeDtypeStruct((S, RH * RET_V), BF16),
                   jax.ShapeDtypeStruct((RH, nb, RET_QK, RET_V), F32)],
        scratch_shapes=[pltpu.VMEM((G, RET_QK, RET_V), F32)],
        compiler_params=_params(("parallel", "arbitrary")),
    )(proj, cosr, sinr, lgam, gain)


def _ret_bwd(proj, cosr, sinr, lgam, gain, ry, dgated, states, dproj, RH, *, T):
    S = proj.shape[0]
    nb = S // T
    G = _tile(RH, RET_GROUP)
    heads = range(G)
    scale = RET_QK ** -0.5

    def body(p_ref, cos_ref, sin_ref, lg_ref, gain_ref, ry_ref, dg_ref, st_ref, _, dp_ref, dgain_ref, dstate):
        b = pl.program_id(1)

        @pl.when(b == 0)
        def _():
            dstate[...] = jnp.zeros_like(dstate)

        consts = [_ret_consts(lg_ref[h, 0:1, 0:1], T) for h in heads]
        qd, kd, decay, cdec = [[c[i] for c in consts] for i in range(4)]
        cosv, sinv = cos_ref[...], sin_ref[...]
        q = [_rope128(p_ref[:, _head_cols(h, _RQ)], cosv, sinv) for h in heads]
        k = [_rope128(p_ref[:, _head_cols(h, _RK)], cosv, sinv) * scale for h in heads]
        v = [p_ref[:, _head_cols(h, _RV)] for h in heads]
        sprev = [st_ref[h] for h in heads]
        ds_new = [dstate[h] for h in heads]
        a = [_dot(q[h], k[h], NT) for h in heads]
        do, gparts = [], []
        for h in heads:
            vals = slice(h * RET_V, (h + 1) * RET_V)
            o = ry_ref[:, vals]
            mu = jnp.mean(o, axis=-1, keepdims=True)
            oc = o - mu
            rstd = lax.rsqrt(jnp.mean(oc * oc, axis=-1, keepdims=True) + EPS)
            ryn = oc * rstd
            gainv = gain_ref[:, vals]
            gv = p_ref[:, _head_cols(h, _RG)]
            sg = _sigmoid(gv)
            dgt = dg_ref[:, vals]
            dt = dgt * (gv * sg)
            dp_ref[:, _head_cols(h, _RG)] = (dgt * (ryn * gainv) * (sg * (1.0 + gv * (1.0 - sg)))).astype(BF16)
            gparts.append(jnp.sum(dt * ryn, axis=0, keepdims=True))
            dryn = dt * gainv
            do.append(rstd * (dryn - jnp.mean(dryn, axis=-1, keepdims=True)
                              - ryn * jnp.mean(dryn * ryn, axis=-1, keepdims=True)))
        gpart = jnp.concatenate(gparts, axis=1)

        @pl.when(b == 0)
        def _():
            dgain_ref[...] = gpart

        @pl.when(b > 0)
        def _():
            dgain_ref[...] += gpart

        dpm = [_dot(do[h], v[h], NT) for h in heads]
        dq_s = [_dot(do[h], sprev[h], NT) for h in heads]
        dk_s = [_dot(v[h], ds_new[h], NT) for h in heads]
        dv_s = [_dot(k[h] * kd[h], ds_new[h], NN) for h in heads]
        dst = [_dot(q[h] * qd[h], do[h], TN) for h in heads]
        a = [a[h] * decay[h] for h in heads]
        dpm = [dpm[h] * decay[h] for h in heads]
        dv = [_dot(a[h], do[h], TN) + dv_s[h] for h in heads]
        dq = [_dot(dpm[h], k[h], NN) + dq_s[h] * qd[h] for h in heads]
        dk = [(_dot(dpm[h], q[h], TN) + dk_s[h] * kd[h]) * scale for h in heads]
        for h in heads:
            dstate[h] = ds_new[h] * cdec[h] + dst[h]
            dp_ref[:, _head_cols(h, _RV)] = dv[h].astype(BF16)
            dp_ref[:, _head_cols(h, _RQ)] = _rope128_t(dq[h], cosv, sinv).astype(BF16)
            dp_ref[:, _head_cols(h, _RK)] = _rope128_t(dk[h], cosv, sinv).astype(BF16)

    rb = lambda b: nb - 1 - b
    return _pallas(
        body, name="ret_bwd", grid=(RH // G, nb),
        in_specs=[pl.BlockSpec((T, G * RET_HEAD_COLS), lambda h, b: (rb(b), h)),
                  pl.BlockSpec((T, RET_QK), lambda h, b: (rb(b), 0)),
                  pl.BlockSpec((T, RET_QK), lambda h, b: (rb(b), 0)),
                  pl.BlockSpec((G, 8, LANES), lambda h, b: (h, 0, 0)),
                  pl.BlockSpec((1, G * RET_V), lambda h, b: (0, h)),
                  pl.BlockSpec((T, G * RET_V), lambda h, b: (rb(b), h)),
                  pl.BlockSpec((T, G * RET_V), lambda h, b: (rb(b), h)),
                  pl.BlockSpec((G, None, RET_QK, RET_V), lambda h, b: (h, rb(b), 0, 0)),
                  _ANY],
        out_specs=[pl.BlockSpec((T, G * RET_HEAD_COLS), lambda h, b: (rb(b), h)),
                   pl.BlockSpec((1, G * RET_V), lambda h, b: (0, h))],
        out_shape=[jax.ShapeDtypeStruct(dproj.shape, dproj.dtype), jax.ShapeDtypeStruct((1, RH * RET_V), F32)],
        scratch_shapes=[pltpu.VMEM((G, RET_QK, RET_V), F32)],
        input_output_aliases={8: 0},
        compiler_params=_params(("parallel", "arbitrary")),
    )(proj, cosr, sinr, lgam, gain, ry, dgated, states, dproj)


def _rope_pe(t, c, s1, s2):
    return t * c + pltpu.roll(t, LANES - QK_ROPE // 2, 1) * s1 + pltpu.roll(t, QK_ROPE // 2, 1) * s2


def _rope_pe_t(d, c, s1, s2):
    return d * c + pltpu.roll(d * s1, QK_ROPE // 2, 1) + pltpu.roll(d * s2, LANES - QK_ROPE // 2, 1)


ATTN_C2 = (QK_NOPE + QK_ROPE) ** -0.5 * LOG2E


def _qkv_proj(cqn, ckvn, wq, wkv, kpe, tabs, MH, *, tm=512, heads=4):
    S = cqn.shape[0]
    tm = _tile(S, tm)
    hb = _tile(MH, heads)
    W = 2 * LANES
    c_t, s1_t, s2_t = tabs

    def body(cq_ref, ckv_ref, wq_ref, wkv_ref, kpe_ref, c_ref, s1_ref, s2_ref, qf_ref, kf_ref, v_ref):
        c, s1, s2 = c_ref[...], s1_ref[...], s2_ref[...]
        q = _dot(cq_ref[...], wq_ref[...], NN)
        kv = _dot(ckv_ref[...], wkv_ref[...], NN)
        kper = _rope_pe(kpe_ref[...], c, s1, s2).astype(BF16)
        for h in range(hb):
            lo, mid, hi = h * W, h * W + QK_NOPE, (h + 1) * W
            qf_ref[:, lo:mid] = (q[:, lo:mid] * ATTN_C2).astype(BF16)
            qf_ref[:, mid:hi] = (_rope_pe(q[:, mid:hi], c, s1, s2) * ATTN_C2).astype(BF16)
            kf_ref[:, lo:mid] = kv[:, lo:mid].astype(BF16)
            kf_ref[:, mid:hi] = kper
            v_ref[:, h * V_HEAD:(h + 1) * V_HEAD] = kv[:, mid:hi].astype(BF16)

    tab = pl.BlockSpec((tm, LANES), lambda i, j: (i, 0))
    grp = pl.BlockSpec((tm, hb * W), lambda i, j: (i, j))
    return _pallas(
        body, name="qkv_proj", grid=(S // tm, MH // hb),
        in_specs=[pl.BlockSpec((tm, cqn.shape[1]), lambda i, j: (i, 0)),
                  pl.BlockSpec((tm, ckvn.shape[1]), lambda i, j: (i, 0)),
                  pl.BlockSpec((wq.shape[0], hb * W), lambda i, j: (0, j)),
                  pl.BlockSpec((wkv.shape[0], hb * W), lambda i, j: (0, j)), tab, tab, tab, tab],
        out_specs=[grp, grp, pl.BlockSpec((tm, hb * V_HEAD), lambda i, j: (i, j))],
        out_shape=[jax.ShapeDtypeStruct((S, MH * W), BF16)] * 2 + [jax.ShapeDtypeStruct((S, MH * V_HEAD), BF16)],
        compiler_params=_params(("parallel", "parallel")),
    )(cqn, ckvn, wq, wkv, kpe, c_t, s1_t, s2_t)


def _chunk_mask(T):
    n = lax.broadcasted_iota(jnp.int32, (T, T), 0)
    m = lax.broadcasted_iota(jnp.int32, (T, T), 1)
    return (m // CHUNK) <= (n // CHUNK)


def _lanes_to(v, width):
    return jnp.tile(v, (1, width // LANES))


def _attn_fwd(qf, kf, vb, MH, *, T, heads, name, prev=(), deps=()):
    S = qf.shape[0]
    nt = S // T
    n_skip = len(prev) + len(deps)

    def body(q_ref, k_ref, v_ref, *rest):
        o_ref, ob_ref, lse_ref, m_sc, l_sc, acc_sc, s_a, s_b = rest[n_skip:]
        qi = pl.program_id(1)
        m_sc[...] = jnp.full_like(m_sc, NEG)
        l_sc[...] = jnp.zeros_like(l_sc)
        acc_sc[...] = jnp.zeros_like(acc_sc)

        def rows_of(kt):
            return pl.ds(pl.multiple_of(kt * T, T), T)

        def scores(kt):
            return _dot(q_ref[...], k_ref[rows_of(kt), :], NT)

        def update(s, kt):
            m_prev = m_sc[...]
            m_new = jnp.maximum(m_prev, jnp.max(s, axis=-1, keepdims=True))
            alpha = jnp.exp2(m_prev - m_new)
            p = jnp.exp2(s - _lanes_to(m_new, T))
            l_sc[...] = alpha * l_sc[...] + jnp.sum(p, axis=-1, keepdims=True)
            acc_sc[...] = alpha * acc_sc[...] + _dot(p, v_ref[rows_of(kt), :], NN)
            m_sc[...] = m_new

        def masked(s):
            return jnp.where(_chunk_mask(T), s, NEG)

        @pl.when(qi == 0)
        def _():
            update(masked(scores(0)), 0)

        @pl.when(qi > 0)
        def _():
            s_a[...] = masked(scores(qi))
            s_b[...] = scores(0)
            update(s_a[...], qi)
            s_a[...] = scores(jnp.minimum(1, qi - 1))
            update(s_b[...], 0)

            def pair(j, carry):
                s_b[...] = scores(2 * j)
                update(s_a[...], 2 * j - 1)
                s_a[...] = scores(jnp.minimum(2 * j + 1, qi - 1))
                update(s_b[...], 2 * j)
                return carry

            lax.fori_loop(1, (qi + 1) // 2, pair, 0)

            @pl.when(qi % 2 == 0)
            def _():
                update(s_a[...], qi - 1)
        l = l_sc[...]
        o = acc_sc[...] / l
        o_ref[...] = o
        ob_ref[...] = o.astype(BF16)
        lse_ref[...] = m_sc[...] + jnp.log(l) * LOG2E

    h0, h1 = heads
    out_shape = [jax.ShapeDtypeStruct((S, MH * LANES), F32), jax.ShapeDtypeStruct((S, MH * LANES), BF16),
                 jax.ShapeDtypeStruct((MH, S, LANES), F32)]
    row = pl.BlockSpec((T, LANES), lambda h, i: (i, h0 + h))
    return _pallas(
        body, name=name, grid=(h1 - h0, nt),
        in_specs=[pl.BlockSpec((T, 2 * LANES), lambda h, i: (i, h0 + h)),
                  pl.BlockSpec((S, 2 * LANES), lambda h, i: (0, h0 + h)),
                  pl.BlockSpec((S, LANES), lambda h, i: (0, h0 + h))] + [_ANY] * (len(prev) + len(deps)),
        out_specs=[row, row, pl.BlockSpec((None, T, LANES), lambda h, i: (h0 + h, i, 0))],
        out_shape=out_shape,
        scratch_shapes=[pltpu.VMEM((T, LANES), F32), pltpu.VMEM((T, LANES), F32), pltpu.VMEM((T, LANES), F32),
                        pltpu.VMEM((T, T), F32), pltpu.VMEM((T, T), F32)],
        input_output_aliases={3 + i: i for i in range(len(prev))},
        compiler_params=_params(("parallel", "parallel")),
    )(qf, kf, vb, *prev, *deps)


def _attn_bwd(qf, kf, vb, dob, lse2, delta, tabs, MH, *, T, deps=()):
    S = qf.shape[0]
    nt = S // T
    scale = (QK_NOPE + QK_ROPE) ** -0.5
    n_dep = len(deps)

    def body(q_ref, k_ref, v_ref, do_ref, lse_ref, dl_ref, c_ref, s1_ref, s2_ref, *rest):
        dqa_ref, dkv_ref, dkpe_ref, dq_ref, dk_sc, dv_sc, s_a, dp_a, s_b, dp_b = rest[n_dep:]
        kj = pl.program_id(1)

        @pl.when(kj == 0)
        def _():
            dq_ref[...] = jnp.zeros_like(dq_ref)

        dk_sc[...] = jnp.zeros_like(dk_sc)
        dv_sc[...] = jnp.zeros_like(dv_sc)

        def rows_of(qt):
            return pl.ds(pl.multiple_of(qt * T, T), T)

        def products(qt):
            rows = rows_of(qt)
            return _dot(q_ref[rows, :], k_ref[...], NT), _dot(do_ref[rows, :], v_ref[...], NT)

        def update(s, dp, qt):
            rows = rows_of(qt)
            q, dov = q_ref[rows, :], do_ref[rows, :]
            p = jnp.exp2(s - _lanes_to(lse_ref[rows, :], T))
            ds = p * (dp - _lanes_to(dl_ref[rows, :], T))
            dv_sc[...] += _dot(p, dov, TN)
            dk_sc[...] += _dot(ds, q, TN)
            dq_ref[rows, :] += _dot(ds, k_ref[...], NN)

        def masked(s):
            return jnp.where(_chunk_mask(T), s, NEG)

        @pl.when(kj == nt - 1)
        def _():
            s, dp = products(kj)
            update(masked(s), dp, kj)

        @pl.when(kj < nt - 1)
        def _():
            s, dp = products(kj)
            s_a[...], dp_a[...] = masked(s), dp
            s_b[...], dp_b[...] = products(kj + 1)
            update(s_a[...], dp_a[...], kj)
            s_a[...], dp_a[...] = products(jnp.minimum(kj + 2, nt - 1))
            update(s_b[...], dp_b[...], kj + 1)

            def pair(j, carry):
                t0 = kj + 2 * j
                s_b[...], dp_b[...] = products(t0 + 1)
                update(s_a[...], dp_a[...], t0)
                s_a[...], dp_a[...] = products(jnp.minimum(t0 + 2, nt - 1))
                update(s_b[...], dp_b[...], t0 + 1)
                return carry

            lax.fori_loop(1, (nt - kj) // 2, pair, 0)

            @pl.when((nt - kj) % 2 == 1)
            def _():
                update(s_a[...], dp_a[...], nt - 1)
        dkv_ref[:, :QK_NOPE] = (dk_sc[:, :QK_NOPE] * (1.0 / LOG2E)).astype(BF16)
        dkv_ref[:, QK_NOPE:] = dv_sc[...].astype(BF16)
        dkpe_ref[...] = dk_sc[:, QK_NOPE:] * (1.0 / LOG2E)

        @pl.when(kj == nt - 1)
        def _():
            dqa_ref[:, :QK_NOPE] = (dq_ref[:, :QK_NOPE] * scale).astype(BF16)
            dqa_ref[:, QK_NOPE:] = (_rope_pe_t(dq_ref[:, QK_NOPE:], c_ref[...], s1_ref[...], s2_ref[...])
                                    * scale).astype(BF16)

    stat = pl.BlockSpec((None, S, LANES), lambda h, j: (h, 0, 0))
    tab = pl.BlockSpec((S, LANES), lambda h, j: (0, 0))
    return _pallas(
        body, name="attn_bwd", grid=(MH, nt),
        in_specs=[pl.BlockSpec((S, 2 * LANES), lambda h, j: (0, h)),
                  pl.BlockSpec((T, 2 * LANES), lambda h, j: (j, h)),
                  pl.BlockSpec((T, LANES), lambda h, j: (j, h)),
                  pl.BlockSpec((S, LANES), lambda h, j: (0, h)), stat, stat, tab, tab, tab] + [_ANY] * n_dep,
        out_specs=[pl.BlockSpec((S, 2 * LANES), lambda h, j: (0, h)),
                   pl.BlockSpec((T, 2 * LANES), lambda h, j: (j, h)),
                   pl.BlockSpec((T, LANES), lambda h, j: (j, h))],
        out_shape=[jax.ShapeDtypeStruct((S, MH * 2 * LANES), BF16), jax.ShapeDtypeStruct((S, MH * 2 * LANES), BF16),
                   jax.ShapeDtypeStruct((S, MH * LANES), F32)],
        scratch_shapes=[pltpu.VMEM((S, 2 * LANES), F32), pltpu.VMEM((T, 2 * LANES), F32), pltpu.VMEM((T, LANES), F32)]
        + [pltpu.VMEM((T, T), F32)] * 4,
        compiler_params=_params(("parallel", "arbitrary")),
    )(qf, kf, vb, dob, lse2, delta, *tabs, *deps)


def _kpe_sum(dkpe_h, tabs, MH, *, tr=256):
    S = dkpe_h.shape[0]
    tr = _tile(S, tr)

    def body(dk_ref, c_ref, s1_ref, s2_ref, dkpe_ref):
        tot = dk_ref[:, :LANES]
        for h in range(1, MH):
            tot = tot + dk_ref[:, h * LANES:(h + 1) * LANES]
        dkpe_ref[...] = _rope_pe_t(tot, c_ref[...], s1_ref[...], s2_ref[...]).astype(BF16)

    tab = pl.BlockSpec((tr, LANES), lambda i: (i, 0))
    return _pallas(
        body, name="kpe_sum", grid=(S // tr,),
        in_specs=[pl.BlockSpec((tr, MH * LANES), lambda i: (i, 0)), tab, tab, tab],
        out_specs=tab, out_shape=jax.ShapeDtypeStruct((S, LANES), BF16),
        compiler_params=_params(("parallel",)),
    )(dkpe_h, *tabs)


ROW_ALIGN = 16


def _blk(R, C, block_bytes=2 << 20):
    cap = max(ROW_ALIGN, block_bytes // (C * 4))
    for t in range(min(R, cap) // ROW_ALIGN * ROW_ALIGN, LANES - 1, -ROW_ALIGN):
        if R % t == 0:
            return t, C
    if R <= cap:
        return R, C
    tc = C
    while R * tc * 4 > block_bytes and tc % (2 * LANES) == 0:
        tc //= 2
    return R, tc


def _rows_call(fn, ins, out_dtypes, *, name):
    R, C = ins[0].shape
    tr, tc = _blk(R, C)
    n_in = len(ins)

    def body(*refs):
        vals = fn(*[r[...] for r in refs[:n_in]])
        for r, v in zip(refs[n_in:], vals):
            r[...] = v.astype(r.dtype)

    blk = pl.BlockSpec((tr, tc), lambda i, j: (i, j))
    res = _pallas(
        body, name=name, grid=(R // tr, C // tc), in_specs=[blk] * n_in, out_specs=[blk] * len(out_dtypes),
        out_shape=[jax.ShapeDtypeStruct((R, C), d) for d in out_dtypes],
        compiler_params=_params(("parallel", "parallel")),
    )(*ins)
    return res


def _adamw_vals(w, g, m, v):
    m = ADAM_B1 * m + (1.0 - ADAM_B1) * g
    v = ADAM_B2 * v + (1.0 - ADAM_B2) * (g * g)
    m_hat = m / (1.0 - ADAM_B1 ** ADAM_STEP)
    v_hat = v / (1.0 - ADAM_B2 ** ADAM_STEP)
    delta = -ADAM_LR * (m_hat / (jnp.sqrt(v_hat) + ADAM_EPS) + ADAM_WD * w)
    return delta, m, v


def _sum_pair(p, theirs, place, *, name):
    _, R, C = p.shape
    R2 = R // 2
    tr, tc = _blk(R2, C)
    p4 = p.reshape(N_CHIPS, 2, R2, C)

    def body(place_ref, a_ref, b_ref, o_ref):
        o_ref[...] = (a_ref[...].astype(F32) + b_ref[...].astype(F32)).astype(BF16)

    spec = pltpu.PrefetchScalarGridSpec(
        num_scalar_prefetch=1, grid=(N_CHIPS, R2 // tr, C // tc),
        in_specs=[pl.BlockSpec((None, None, tr, tc), lambda q, i, j, pr: (q, pr[0], i, j)),
                  pl.BlockSpec((None, tr, tc), lambda q, i, j, pr: (q, i, j))],
        out_specs=pl.BlockSpec((None, tr, tc), lambda q, i, j, pr: (q, i, j)))
    return _pallas(body, name=name, grid_spec=spec, out_shape=jax.ShapeDtypeStruct((N_CHIPS, R2, C), BF16),
                   compiler_params=_params(("parallel", "parallel", "parallel")))(place, p4, theirs)


def _sum_chips(p, theirs, recv, place, *, name):
    _, R, C = p.shape
    R2 = R // 2
    tr, tc = _blk(R2, C)
    p4 = p.reshape(N_CHIPS, 2, R2, C)

    def body(place_ref, a_ref, b_ref, r0_ref, r1_ref, r2_ref, o_ref):
        own = a_ref[...].astype(F32) + b_ref[...].astype(F32)
        o_ref[...] = ((own + r0_ref[...].astype(F32)) + r1_ref[...].astype(F32)) + r2_ref[...].astype(F32)

    def slot(k):
        return pl.BlockSpec((None, tr, tc), lambda i, j, pr: (k, i, j))

    spec = pltpu.PrefetchScalarGridSpec(
        num_scalar_prefetch=1, grid=(R2 // tr, C // tc),
        in_specs=[pl.BlockSpec((None, None, tr, tc), lambda i, j, pr: (pr[1], pr[0], i, j)),
                  pl.BlockSpec((None, tr, tc), lambda i, j, pr: (pr[1], i, j)), slot(0), slot(1), slot(2)],
        out_specs=pl.BlockSpec((None, tr, tc), lambda i, j, pr: (pr[0], i, j)))
    return _pallas(body, name=name, grid_spec=spec, out_shape=jax.ShapeDtypeStruct((2, R2, C), F32),
                   compiler_params=_params(("parallel", "parallel")))(place, p4, theirs, recv, recv, recv)


def _me():
    return lax.axis_index("x"), lax.axis_index("y"), lax.axis_index("c")


def _other_chips(x, y):
    return [(1 - x, y), (x, 1 - y), (1 - x, 1 - y)]


def _rcopy(src, dst, ssem, rsem, dev):
    return pltpu.make_async_remote_copy(src_ref=src, dst_ref=dst, send_sem=ssem, recv_sem=rsem,
                                        device_id=dev, device_id_type=MESH)


def _cast_into_slot(w, place, *, name, rows=None, deps=()):
    R, C = w.shape
    rows = R if rows is None else rows
    tr, tc = _blk(R, C)

    def body(place_ref, w_ref, *rest):
        rest[-1][...] = w_ref[...].astype(BF16)

    spec = pltpu.PrefetchScalarGridSpec(
        num_scalar_prefetch=1, grid=(R // tr, C // tc),
        in_specs=[pl.BlockSpec((tr, tc), lambda i, j, pr: (i, j))] + [_ANY] * len(deps),
        out_specs=pl.BlockSpec((None, tr, tc), lambda i, j, pr: (pr[1], i, j)))
    out = _pallas(body, name=name, grid_spec=spec, out_shape=jax.ShapeDtypeStruct((N_CHIPS, rows, C), BF16),
                  compiler_params=_params(("parallel", "parallel")))(place, w, *deps)
    return out.reshape(N_CHIPS, 2, rows // 2, C)


def _gather_ici_plan(bufs):
    x, y, c = _me()
    j = 2 * x + y
    plan = []
    for i, buf in enumerate(bufs):
        for k, (px, py) in enumerate(_other_chips(x, y)):
            plan.append((3 * i + k, buf.at[j, c], buf.at[j, c], (px, py, c)))
    return plan


def _forward_halves(bufs, *, name):
    n = len(bufs)

    def body(*refs):
        outs = refs[n:2 * n]
        ssem, rsem = refs[2 * n:]
        x, y, c = _me()
        sib = (x, y, 1 - c)
        cps = []
        for i in range(n):
            for k, (px, py) in enumerate(_other_chips(x, y)):
                slot = outs[i].at[2 * px + py, c]
                r = _rcopy(slot, slot, ssem.at[3 * i + k], rsem.at[3 * i + k], sib)
                r.start()
                cps.append(r)
        for r in cps:
            r.wait()

    return _pallas(
        body, name=name, in_specs=[_ANY] * n, out_specs=[_ANY] * n,
        out_shape=[jax.ShapeDtypeStruct(b.shape, b.dtype) for b in bufs],
        scratch_shapes=[pltpu.SemaphoreType.DMA((3 * n,))] * 2,
        input_output_aliases={i: i for i in range(n)},
        compiler_params=pltpu.CompilerParams(has_side_effects=True),
    )(*bufs)


_HBM = pl.BlockSpec(memory_space=pltpu.HBM)
_SEM = pl.BlockSpec(memory_space=pltpu.SEMAPHORE)
_EFFECT = pltpu.SideEffectType.DATAFLOW_SIDE_EFFECTING


def _split_start(bufs, plan, n_copies, *, name):
    n = len(bufs)

    def body(*refs):
        ssem, rsem = refs[n], refs[n + 1]
        for s, src, dst, dev in plan(refs[:n]):
            _rcopy(src, dst, ssem.at[s], rsem.at[s], dev).start()
        refs[-1][...] = jnp.zeros_like(refs[-1])

    res = _pallas(
        body, name=name, in_specs=[_HBM] * n,
        out_specs=(_SEM, _SEM, *[_HBM] * n, pl.BlockSpec(memory_space=pltpu.VMEM)),
        out_shape=(pltpu.SemaphoreType.DMA((n_copies,)), pltpu.SemaphoreType.DMA((n_copies,)),
                   *[pltpu.HBM(b.shape, b.dtype) for b in bufs], jax.ShapeDtypeStruct((8, LANES), F32)),
        input_output_aliases={i: 2 + i for i in range(n)},
        compiler_params=pltpu.CompilerParams(has_side_effects=_EFFECT),
    )(*[pltpu.with_memory_space_constraint(b, pltpu.HBM) for b in bufs])
    return res[0], res[1], list(res[2:2 + n]), res[-1]


def _split_wait(ssem, rsem, bufs, after, plan, *, name):
    n = len(bufs)

    def body(*refs):
        ssem_ref, rsem_ref = refs[n], refs[n + 1]
        for s, src, dst, dev in plan(refs[:n]):
            cp = _rcopy(src, dst, ssem_ref.at[s], rsem_ref.at[s], dev)
            cp.wait_send()
            cp.wait_recv()

    return list(_pallas(
        body, name=name, in_specs=[_HBM] * n + [_SEM, _SEM, _ANY], out_specs=[_HBM] * n,
        out_shape=[pltpu.HBM(b.shape, b.dtype) for b in bufs],
        input_output_aliases={i: i for i in range(n)},
        compiler_params=pltpu.CompilerParams(has_side_effects=_EFFECT),
    )(*bufs, ssem, rsem, after))


def _forward_plan(bufs):
    x, y, c = _me()
    plan = []
    for i, buf in enumerate(bufs):
        for k, (px, py) in enumerate(_other_chips(x, y)):
            plan.append((3 * i + k, buf.at[2 * px + py, c], buf.at[2 * px + py, c], (x, y, 1 - c)))
    return plan


def _join_plan(bufs):
    x, y, c = _me()
    return [(i, buf.at[c], buf.at[c], (x, y, 1 - c)) for i, buf in enumerate(bufs)]


def _swap_plan(n):
    def plan(bufs):
        x, y, c = _me()
        return [(i, bufs[i].at[:, 1 - c], bufs[n + i], (x, y, 1 - c)) for i in range(n)]
    return plan


def _scatter_plan(n):
    def plan(bufs):
        x, y, c = _me()
        out = []
        for i in range(n):
            for k, (px, py) in enumerate(_other_chips(x, y)):
                out.append((3 * i + k, bufs[i].at[2 * px + py], bufs[n + i].at[k], (px, py, c)))
        return out
    return plan


def _swap_halves(grads, *, name):
    n = len(grads)
    views = [g.reshape(N_CHIPS, 2, g.shape[1] // 2, g.shape[2]) for g in grads]

    def body(*refs):
        ins, outs = refs[:n], refs[n:2 * n]
        ssem, rsem = refs[2 * n:]
        x, y, c = _me()
        sib = (x, y, 1 - c)
        cps = []
        for i in range(n):
            r = _rcopy(ins[i].at[:, 1 - c], outs[i], ssem.at[i], rsem.at[i], sib)
            r.start()
            cps.append(r)
        for r in cps:
            r.wait()

    return _pallas(
        body, name=name, in_specs=[_ANY] * n, out_specs=[_ANY] * n,
        out_shape=[jax.ShapeDtypeStruct((N_CHIPS,) + v.shape[2:], v.dtype) for v in views],
        scratch_shapes=[pltpu.SemaphoreType.DMA((n,)), pltpu.SemaphoreType.DMA((n,))],
        compiler_params=pltpu.CompilerParams(has_side_effects=True),
    )(*views)


def _join_halves(halves, *, name):
    n = len(halves)

    def body(*refs):
        outs = refs[n:2 * n]
        ssem, rsem = refs[2 * n:]
        x, y, c = _me()
        sib = (x, y, 1 - c)
        cps = []
        for i in range(n):
            r = _rcopy(outs[i].at[c], outs[i].at[c], ssem.at[i], rsem.at[i], sib)
            r.start()
            cps.append(r)
        for r in cps:
            r.wait()

    return _pallas(
        body, name=name, in_specs=[_ANY] * n, out_specs=[_ANY] * n,
        out_shape=[jax.ShapeDtypeStruct(h.shape, h.dtype) for h in halves],
        scratch_shapes=[pltpu.SemaphoreType.DMA((n,)), pltpu.SemaphoreType.DMA((n,))],
        input_output_aliases={i: i for i in range(n)},
        compiler_params=pltpu.CompilerParams(has_side_effects=True),
    )(*halves)


def _allreduce_small(parts, loss11):
    n = len(parts)
    widths = [p.shape[1] for p in parts]
    total = sum(widths) + LANES

    def body(*refs):
        o_ref, mine, buf, ssem, rsem = refs[n + 1:]
        x, y, c = _me()
        me = 4 * x + 2 * y + c
        off = 0
        for r, w in zip(refs[:n], widths):
            mine[:, off:off + w] = r[...]
            off += w
        mine[:, off:] = jnp.broadcast_to(refs[n][...], (1, LANES))
        buf[me] = mine[...]
        cps = []
        for k in range(1, 8):
            peer = (x ^ (k >> 2), y ^ ((k >> 1) & 1), c ^ (k & 1))
            r = _rcopy(mine, buf.at[me], ssem.at[k - 1], rsem.at[k - 1], peer)
            r.start()
            cps.append(r)
        for k in range(1, 8):
            peer = (x ^ (k >> 2), y ^ ((k >> 1) & 1), c ^ (k & 1))
            pid = 4 * peer[0] + 2 * peer[1] + peer[2]
            _rcopy(mine, buf.at[pid], ssem.at[k - 1], rsem.at[k - 1], peer).wait_recv()
        for r in cps:
            r.wait_send()
        tot = buf[0]
        for d in range(1, 8):
            tot = tot + buf[d]
        o_ref[...] = tot

    vm = pl.BlockSpec(memory_space=pltpu.VMEM)
    return _pallas(
        body, name="allreduce_small", in_specs=[vm] * (n + 1), out_specs=vm,
        out_shape=jax.ShapeDtypeStruct((1, total), F32),
        scratch_shapes=[pltpu.VMEM((1, total), F32), pltpu.VMEM((8, 1, total), F32),
                        pltpu.SemaphoreType.DMA((7,)), pltpu.SemaphoreType.DMA((7,))],
        compiler_params=pltpu.CompilerParams(has_side_effects=True),
    )(*parts, loss11)


def _adamw_small(red, ws, ms, vs):
    n = len(ws)

    def body(*refs):
        red_ref = refs[0]
        outs = refs[1 + 3 * n:]
        off = 0
        for i in range(n):
            w = refs[1 + i].shape[1]
            g = red_ref[:, off:off + w]
            d, m, v = _adamw_vals(refs[1 + i][...], g, refs[1 + n + i][...], refs[1 + 2 * n + i][...])
            for o, val in zip(outs[4 * i:4 * i + 4], (g, d, m, v)):
                o[...] = val
            off += w

    vm = pl.BlockSpec(memory_space=pltpu.VMEM)
    res = _pallas(
        body, name="adamw_small", in_specs=[vm] * (1 + 3 * n), out_specs=[vm] * (4 * n),
        out_shape=[jax.ShapeDtypeStruct(w.shape, F32) for w in ws for _ in range(4)],
    )(red, *ws, *ms, *vs)
    return [res[4 * i:4 * i + 4] for i in range(n)]


def _rope_tables(positions, S):
    pos = positions.reshape(S, 1).astype(F32)
    half = RET_QK // 2
    inv = ROPE_THETA ** (-jnp.arange(half, dtype=F32) / half)
    ang = pos * inv
    cosr = jnp.concatenate([jnp.cos(ang), jnp.cos(ang)], axis=1)
    sinr = jnp.concatenate([-jnp.sin(ang), jnp.sin(ang)], axis=1)
    half = QK_ROPE // 2
    inv = ROPE_THETA ** (-jnp.arange(half, dtype=F32) / half)
    ang = pos * inv
    z = jnp.zeros((S, half), F32)
    c = jnp.concatenate([jnp.cos(ang), jnp.cos(ang), z, z], axis=1)
    s1 = jnp.concatenate([-jnp.sin(ang), z, z, z], axis=1)
    s2 = jnp.concatenate([z, jnp.sin(ang), z, z], axis=1)
    return cosr, sinr, (c, s1, s2)


def _cat_cols(g):
    return jnp.concatenate([g[j] for j in range(N_CHIPS)], axis=1)


def _split_cols(w):
    return jnp.stack(jnp.split(w, N_CHIPS, axis=1))


def kernel(x, positions, norm_mix_g, w_in, ret_norm_g, w_ret_o, q_a_norm_g, w_q_b, kv_a_norm_g, w_kv_b, w_mla_o, w_out, norm_mlp_g, w_up, w_down, norm_f_g, loss_target, m_norm_mix_g, m_w_in, m_ret_norm_g, m_w_ret_o, m_q_a_norm_g, m_w_q_b, m_kv_a_norm_g, m_w_kv_b, m_w_mla_o, m_w_out, m_norm_mlp_g, m_w_up, m_w_down, m_norm_f_g, v_norm_mix_g, v_w_in, v_ret_norm_g, v_w_ret_o, v_q_a_norm_g, v_w_q_b, v_kv_a_norm_g, v_w_kv_b, v_w_mla_o, v_w_out, v_norm_mlp_g, v_w_up, v_w_down, v_norm_f_g):
    S, D = x.shape[1], x.shape[2]
    RVW = w_ret_o.shape[1] * N_CHIPS
    RH = RVW // RET_V
    RQW = RH * RET_QK
    MVW = w_mla_o.shape[1] * N_CHIPS
    MH = MVW // V_HEAD
    QL, KVL = w_q_b.shape[1], w_kv_b.shape[1]
    T_RET = _tile(S, 256)
    T_ATT = _tile(S, 512)

    xs = x.reshape(S, D)
    tgt = loss_target.reshape(S, D)
    cosr, sinr, pe_tabs = _rope_tables(positions, S)
    lgam = jnp.log(1.0 - 2.0 ** (-5.0 - jnp.arange(RH, dtype=F32)))
    lgam = jnp.broadcast_to(lgam[:, None, None], (RH, 8, LANES))

    big = ("w_in", "w_ret_o", "w_q_b", "w_kv_b", "w_mla_o", "w_out", "w_up", "w_down")
    w_sh = dict(w_in=w_in[0].T, w_ret_o=w_ret_o[0], w_q_b=w_q_b[0], w_kv_b=w_kv_b[0], w_mla_o=w_mla_o[0],
                w_out=w_out[0], w_up=w_up[0], w_down=w_down[0])
    m_sh = dict(w_in=m_w_in[0].T, w_ret_o=m_w_ret_o[0], w_q_b=m_w_q_b[0], w_kv_b=m_w_kv_b[0],
                w_mla_o=m_w_mla_o[0], w_out=m_w_out[0], w_up=m_w_up[0], w_down=m_w_down[0])
    v_sh = dict(w_in=v_w_in[0].T, w_ret_o=v_w_ret_o[0], w_q_b=v_w_q_b[0], w_kv_b=v_w_kv_b[0],
                w_mla_o=v_w_mla_o[0], w_out=v_w_out[0], w_up=v_w_up[0], w_down=v_w_down[0])
    col_sharded = ("w_q_b", "w_kv_b", "w_up")
    c_sh = w_in.shape[2]
    c_pad = -(-c_sh // 64) * 64
    place = jnp.stack([lax.axis_index("c"), 2 * lax.axis_index("x") + lax.axis_index("y")]).astype(jnp.int32)

    def whole(k, g):
        g = g.reshape(N_CHIPS, w_sh[k].shape[0], w_sh[k].shape[1])
        if k == "w_up":
            return g
        return _cat_cols(g) if k in col_sharded else g.reshape(-1, g.shape[2])

    first = ("w_in", "w_q_b", "w_kv_b")
    later = ("w_ret_o", "w_mla_o", "w_out", "w_up", "w_down")
    first_bufs = [_cast_into_slot(w_sh[k], place, name="cast_" + k, rows=c_pad if k == "w_in" else None)
                  for k in first]
    first_ssem, first_rsem, first_bufs, first_token = _split_start(
        first_bufs, _gather_ici_plan, 3 * len(first), name="gather_first_start")
    later_bufs = [_cast_into_slot(w_sh[k], place, name="cast_" + k, deps=(first_token,)) for k in later[:-1]]
    first_bufs = _split_wait(first_ssem, first_rsem, first_bufs, later_bufs[-1], _gather_ici_plan,
                             name="gather_first_wait")
    got = _forward_halves(first_bufs, name="gather_first_forward")
    full = {k: whole(k, g) for k, g in zip(first[1:], got[1:])}
    later_bufs.append(_cast_into_slot(w_sh[later[-1]], place, name="cast_" + later[-1], deps=(got[0],)))
    later_ssem, later_rsem, later_bufs, later_token = _split_start(
        later_bufs, _gather_ici_plan, 3 * len(later), name="gather_later_start")

    o_rq, o_rk, o_rv, o_rg = 0, RQW, 2 * RQW, 2 * RQW + RVW
    o_cq = 2 * RQW + 2 * RVW
    o_ckv, o_kpe = o_cq + QL, o_cq + QL + KVL
    o_gr = o_kpe + QK_ROPE
    o_gm = o_gr + D
    n_ret = RH * RET_HEAD_COLS
    off_gret, off_gmla, off_cq, off_ckv = n_ret, n_ret + D, n_ret + 2 * D, n_ret + 2 * D + QL
    n_a = off_ckv + KVL
    runs = []
    for h in range(RH):
        base = h * RET_HEAD_COLS
        runs += [(o_rq + h * RET_QK, RET_QK, base), (o_rk + h * RET_QK, RET_QK, base + RET_QK),
                 (o_rv + h * RET_V, RET_V, base + 2 * RET_QK), (o_rg + h * RET_V, RET_V, base + 2 * RET_QK + RET_V)]
    runs += [(o_gr, D, off_gret), (o_gm, D, off_gmla), (o_cq, QL, off_cq), (o_ckv, KVL, off_ckv),
             (o_kpe, QK_ROPE, n_a)]

    def take(parts, start, width):
        out, lo = [], 0
        for p in parts:
            hi = lo + p.shape[0]
            a, b = max(start, lo), min(start + width, hi)
            if a < b:
                out.append(p[a - lo:b - lo])
            lo = hi
        return out

    wi = [got[0].reshape(N_CHIPS, c_pad, D)[jj, :c_sh] for jj in range(N_CHIPS)]
    here = sorted(runs, key=lambda r: r[2])
    wa = jnp.concatenate([p for s0, w, _ in here[:-1] for p in take(wi, s0, w)], axis=0)
    wkpe = jnp.concatenate(take(wi, o_kpe, QK_ROPE) + [jnp.zeros((LANES - QK_ROPE, D), BF16)], axis=0)
    wq = jnp.pad(full["w_q_b"].reshape(QL, MH, QK_NOPE + QK_ROPE),
                 ((0, 0), (0, 0), (0, LANES - QK_ROPE))).reshape(QL, MH * 2 * LANES)
    wkv = full["w_kv_b"]

    u, rstd0 = _rmsnorm_fwd(xs, norm_mix_g, name="norm_mix")
    proj = _mm(u, wa, mode="nt", outs=[F32], name="in_proj", deps=(later_token,))
    kpe = _mm(u, wkpe, mode="nt", outs=[F32], name="kpe_proj")
    ry, gated, states = _ret_fwd(proj, cosr, sinr, lgam, ret_norm_g, RH, T=T_RET)
    cqn, rstd_q = _rmsnorm_fwd(proj, q_a_norm_g, name="norm_q", width=QL, col=off_cq // QL)
    ckvn, rstd_kv = _rmsnorm_fwd(proj, kv_a_norm_g, name="norm_kv", width=KVL, col=off_ckv // KVL)
    qf, kf, vb = _qkv_proj(cqn, ckvn, wq, wkv, kpe, pe_tabs, MH)
    first_half = _attn_fwd(qf, kf, vb, MH, T=T_ATT, heads=(0, MH // 2), name="attn_fwd_a")
    later_bufs = _split_wait(later_ssem, later_rsem, later_bufs, first_half[0], _gather_ici_plan,
                             name="gather_later_wait")
    fwd_ssem, fwd_rsem, later_bufs, fwd_token = _split_start(
        later_bufs, _forward_plan, 3 * len(later), name="gather_later_forward_start")
    my, my_b, lse2 = _attn_fwd(qf, kf, vb, MH, T=T_ATT, heads=(MH // 2, MH), name="attn_fwd_b",
                               prev=first_half, deps=(fwd_token,))
    later_bufs = _split_wait(fwd_ssem, fwd_rsem, later_bufs, my, _forward_plan, name="gather_later_forward_wait")
    full.update({k: whole(k, g) for k, g in zip(later, later_bufs)})
    y_ret = _mm(gated, full["w_ret_o"], mode="nn", outs=[BF16], name="ret_o")
    y_mla, merged = _mm(my_b, full["w_mla_o"], mode="nn", outs=[BF16, BF16], name="mla_o",
                        epi=lambda acc, gr, gm, yr: (acc, _sigmoid(gr) * yr + _sigmoid(gm) * acc),
                        extras=((proj, off_gret), (proj, off_gmla), y_ret))
    h1 = _mm(merged, full["w_out"], mode="nn", outs=[F32], name="out_proj",
             epi=lambda acc, r: (acc + r,), extras=(xs,))
    n1, rstd1 = _rmsnorm_fwd(h1, norm_mlp_g, name="norm_mlp")

    def up_epi(acc):
        r = jnp.maximum(acc, 0.0)
        return acc, r * r

    z, act = _mm(n1, full["w_up"], mode="nn", outs=[F32, BF16], name="up_proj", epi=up_epi)
    h2 = _mm(act, full["w_down"], mode="nn", outs=[F32], name="down_proj",
             epi=lambda acc, r: (acc + r,), extras=(h1,))
    loss11, dh2, dh2_b, g_norm_f = _final_loss(h2, norm_f_g.reshape(1, D), tgt)

    dz = _mm(dh2_b, full["w_down"], mode="nt", outs=[BF16], name="down_bwd_x",
             epi=lambda acc, zz: (acc * (2.0 * jnp.maximum(zz, 0.0)),), extras=(z,))
    g_w_down = _mm(act, dh2_b, mode="tn", outs=[BF16], name="down_bwd_w")
    dn1 = _mm(dz, full["w_up"], mode="nt", outs=[F32], name="up_bwd_x")
    g_w_up = _mm(n1, dz, mode="tn", outs=[BF16], name="up_bwd_w", out_shards=True)

    def reduce_begin(tag, names, grads):
        pcs = [g if g.ndim == 3 else g.reshape(N_CHIPS, g.shape[0] // N_CHIPS, g.shape[1]) for g in grads]
        theirs = _swap_halves(pcs, name="swap_" + tag)
        sums = [_sum_pair(p, t, place, name="sum_pair_" + k) for k, p, t in zip(names, pcs, theirs)]
        return pcs, theirs, sums

    def scatter_begin(tag, sums):
        lands = [lax.empty((3,) + s.shape[1:], s.dtype) for s in sums]
        return _split_start(sums + lands, _scatter_plan(len(sums)), 3 * len(sums), name="scatter_" + tag + "_start")

    def swap_begin(tag, grads):
        views = [g if g.ndim == 3 else g.reshape(N_CHIPS, g.shape[0] // N_CHIPS, g.shape[1]) for g in grads]
        views = [v.reshape(N_CHIPS, 2, v.shape[1] // 2, v.shape[2]) for v in views]
        lands = [lax.empty((N_CHIPS,) + v.shape[2:], v.dtype) for v in views]
        return _split_start(views + lands, _swap_plan(len(views)), len(views), name="swap_" + tag + "_start")

    def swap_end(tag, names, handle, after):
        n = len(names)
        bufs = _split_wait(handle[0], handle[1], handle[2], after, _swap_plan(n), name="swap_" + tag + "_wait")
        pcs = [b.reshape(N_CHIPS, 2 * b.shape[2], b.shape[3]) for b in bufs[:n]]
        sums = [_sum_pair(p, t, place, name="sum_pair_" + k) for k, p, t in zip(names, pcs, bufs[n:])]
        return pcs, bufs[n:], sums

    g1 = ("w_up", "w_down")
    swap1 = swap_begin("g1", (g_w_up, g_w_down))
    dh1, g_norm_mlp, dh1_b = _rmsnorm_bwd(dn1, h1, rstd1, norm_mlp_g, name="norm_mlp_bwd", res=dh2,
                                          deps=(swap1[3],), bf16_copy=1)
    dmerged = _mm(dh1_b, full["w_out"], mode="nt", outs=[F32], name="out_bwd_x")
    pcs1, theirs1, sums1 = swap_end("g1", g1, swap1, dmerged)
    ssem1, rsem1, bufs1, token1 = scatter_begin("g1", sums1)
    g_w_out = _mm(merged, dh1_b, mode="tn", outs=[BF16], name="out_bwd_w", deps=(token1,))
    dproj, dy_ret, dy_mla = _merge_bwd(dmerged, proj, y_ret, y_mla, D, off_gret)
    dgated = _mm(dy_ret, full["w_ret_o"], mode="nt", outs=[F32], name="ret_o_bwd_x")
    g_w_ret_o = _mm(gated, dy_ret, mode="tn", outs=[BF16], name="ret_o_bwd_w")
    dproj, g_ret_norm = _ret_bwd(proj, cosr, sinr, lgam, ret_norm_g, ry, dgated, states, dproj, RH, T=T_RET)
    def delta_epi(acc, o):
        rows = acc.shape[0]
        return acc, [jnp.broadcast_to(jnp.sum(acc[:, lo:lo + V_HEAD] * o[:, lo:lo + V_HEAD], axis=-1, keepdims=True),
                                      (rows, LANES)) for lo in range(0, acc.shape[1], V_HEAD)]

    dob, delta = _mm(dy_mla, full["w_mla_o"], mode="nt", outs=[BF16], name="mla_o_bwd_x", epi=delta_epi,
                     extras=(my,), more_outs=lambda tm, tn: [
                         (jax.ShapeDtypeStruct((MH, S, LANES), F32),
                          pl.BlockSpec((tn // V_HEAD, tm, LANES), lambda i, j, k: (j, i, 0)))])
    g_w_mla_o = _mm(my_b, dy_mla, mode="tn", outs=[BF16], name="mla_o_bwd_w")
    g2 = ("w_out", "w_ret_o", "w_mla_o")
    swap2 = swap_begin("g2", (g_w_out, g_w_ret_o, g_w_mla_o))
    dq_all, dkv_all, dkpe_h = _attn_bwd(qf, kf, vb, dob, lse2, delta, pe_tabs, MH, T=T_ATT, deps=(swap2[3],))
    pcs2, theirs2, sums2 = swap_end("g2", g2, swap2, dkv_all)
    ssem2, rsem2, bufs2, token2 = scatter_begin("g2", sums2)
    dkpe = _kpe_sum(dkpe_h, pe_tabs, MH)
    dcqn = _mm(dq_all, wq, mode="nt", outs=[F32], name="q_bwd_x", deps=(token2,))
    g_wq = _mm(cqn, dq_all, mode="tn", outs=[BF16], name="q_bwd_w")
    dckvn = _mm(dkv_all, wkv, mode="nt", outs=[F32], name="kv_bwd_x")
    g_wkv = _mm(ckvn, dkv_all, mode="tn", outs=[BF16], name="kv_bwd_w")
    dproj, g_q_a = _rmsnorm_bwd(dcqn, proj, rstd_q, q_a_norm_g, name="norm_q_bwd", into=(dproj, off_cq // QL),
                                width=QL, col=off_cq // QL)
    dproj, g_kv_a = _rmsnorm_bwd(dckvn, proj, rstd_kv, kv_a_norm_g, name="norm_kv_bwd", into=(dproj, off_ckv // KVL),
                                 width=KVL, col=off_ckv // KVL)
    g_wa = _mm(dproj, u, mode="tn", outs=[BF16], name="in_bwd_w")
    g_wkpe = _mm(dkpe, u, mode="tn", outs=[BF16], name="kpe_bwd_w")

    there = sorted(runs)
    g_parts = [g_wa, g_wkpe]
    g_w_in = jnp.stack([jnp.concatenate(
        [p for s0, w, d0 in there for a, b in [(max(s0, jj * c_sh), min(s0 + w, (jj + 1) * c_sh))] if a < b
         for p in take(g_parts, d0 + a - s0, b - a)] + [jnp.zeros((c_pad - c_sh, D), BF16)], axis=0)
        for jj in range(N_CHIPS)])
    gq = g_wq.reshape(QL, MH, 2 * LANES)[:, :, :QK_NOPE + QK_ROPE].reshape(QL, MH * (QK_NOPE + QK_ROPE))
    g3 = ("w_in", "w_q_b", "w_kv_b")
    pcs3, theirs3, sums3 = reduce_begin("g3", g3, (g_w_in, _split_cols(gq), _split_cols(g_wkv)))
    ssem3, rsem3, bufs3, token3 = scatter_begin("g3", sums3)

    def chip_sums(names, pcs, theirs, recv):
        return [_sum_chips(p, t, r, place, name="sum_chips_" + k) for k, p, t, r in zip(names, pcs, theirs, recv)]

    bufs1 = _split_wait(ssem1, rsem1, bufs1, token3, _scatter_plan(len(g1)), name="scatter_g1_wait")
    bufs2 = _split_wait(ssem2, rsem2, bufs2, token3, _scatter_plan(len(g2)), name="scatter_g2_wait")
    halves12 = chip_sums(g1, pcs1, theirs1, bufs1[len(g1):]) + chip_sums(g2, pcs2, theirs2, bufs2[len(g2):])
    jssem, jrsem, halves12, join_token = _split_start(halves12, _join_plan, len(halves12), name="join_g12_start")
    du = _mm(dproj, wa, mode="nn", outs=[F32], name="in_bwd_x", tk=2816, tail=(dkpe, wkpe), deps=(join_token,))
    dx, g_norm_mix = _rmsnorm_bwd(du, xs, rstd0, norm_mix_g, name="norm_mix_bwd", res=dh1)

    bufs3 = _split_wait(ssem3, rsem3, bufs3, dx, _scatter_plan(len(g3)), name="scatter_g3_wait")
    halves12 = _split_wait(jssem, jrsem, halves12, dx, _join_plan, name="join_g12_wait")
    halves3 = _join_halves(chip_sums(g3, pcs3, theirs3, bufs3[len(g3):]), name="join_g3")
    g_shard = {k: g.reshape(2 * g.shape[1], g.shape[2]) for k, g in zip(g1 + g2 + g3, list(halves12) + list(halves3))}

    small = ("norm_mix_g", "ret_norm_g", "q_a_norm_g", "kv_a_norm_g", "norm_mlp_g", "norm_f_g")
    g_small = [g_norm_mix, g_ret_norm, g_q_a, g_kv_a, g_norm_mlp, g_norm_f]
    red = _allreduce_small(g_small, loss11)
    loss = red[0, red.shape[1] - 1]
    w_small = [norm_mix_g, ret_norm_g, q_a_norm_g, kv_a_norm_g, norm_mlp_g, norm_f_g]
    m_small = [m_norm_mix_g, m_ret_norm_g, m_q_a_norm_g, m_kv_a_norm_g, m_norm_mlp_g, m_norm_f_g]
    v_small = [v_norm_mix_g, v_ret_norm_g, v_q_a_norm_g, v_kv_a_norm_g, v_norm_mlp_g, v_norm_f_g]
    row = lambda a: a.reshape(1, -1)
    upd = _adamw_small(red, [row(a) for a in w_small], [row(a) for a in m_small], [row(a) for a in v_small])
    out_g, out_d, out_m, out_v = {}, {}, {}, {}
    for k, wv, (g_, d_, m_, v_) in zip(small, w_small, upd):
        out_g[k], out_d[k], out_m[k], out_v[k] = [a.reshape(wv.shape) for a in (g_, d_, m_, v_)]

    for k in big:
        res = _rows_call(lambda w, g, m, v: (g,) + _adamw_vals(w, g, m, v),
                         [w_sh[k], g_shard[k], m_sh[k], v_sh[k]], [F32] * 4, name="adamw_" + k)
        if k == "w_in":
            res = [r.T for r in res]
        out_g[k], out_d[k], out_m[k], out_v[k] = [r[None] for r in res]

    order = ("norm_mix_g", "w_in", "ret_norm_g", "w_ret_o", "q_a_norm_g", "w_q_b", "kv_a_norm_g", "w_kv_b",
             "w_mla_o", "w_out", "norm_mlp_g", "w_up", "w_down", "norm_f_g")
    return (loss, dx.reshape(1, S, D), *[out_g[k] for k in order], *[out_d[k] for k in order],
            *[out_m[k] for k in order], *[out_v[k] for k in order])
```

```python
import math

import jax
import jax.numpy as jnp
from jax import lax
from jax.experimental import pallas as pl
from jax.experimental.pallas import tpu as pltpu

F32 = jnp.float32
BF16 = jnp.bfloat16

EPS = 1e-6
ROPE_THETA = 10000.0
CHUNK = 64
RET_QK = 128
RET_V = 256
RET_HEAD_COLS = 2 * RET_QK + 2 * RET_V
QK_NOPE = 128
QK_ROPE = 64
V_HEAD = 128
LANES = 128
LOG2E = math.log2(math.e)

ADAM_LR = 0.001
ADAM_B1 = 0.9
ADAM_B2 = 0.999
ADAM_EPS = 1e-08
ADAM_WD = 0.01
ADAM_STEP = 10

N_CHIPS = 4
VMEM_LIMIT = 56 * 1024 * 1024
MESH = pl.DeviceIdType.MESH
NEG = -1e30


def _pallas(body, **kw):
    return pl.pallas_call(body, **kw)


def _params(sem=None):
    return pltpu.CompilerParams(dimension_semantics=sem, vmem_limit_bytes=VMEM_LIMIT)


def _tile(n, want):
    t = min(n, want)
    while n % t:
        t //= 2
    return t


_ANY = pl.BlockSpec(memory_space=pl.ANY)
TN_BF16_TK = 4096


def _mm(a, b, *, mode, outs, name, epi=None, extras=(), deps=(), out_shards=False, more_outs=None, tail=None,
        tm=1024, tn=1024, tk=2048):
    shards = b.shape[0] if b.ndim == 3 else 1
    brows, bcols = b.shape[-2], b.shape[-1] * shards
    if mode == "nn":
        (M, K), N = a.shape, bcols
    elif mode == "nt":
        (M, K), N = a.shape, brows
    else:
        (K, M), N = a.shape, bcols
    if mode == "tn" and a.dtype == BF16 and b.dtype == BF16:
        tk = max(tk, TN_BF16_TK)
    tm = _tile(M, tm)
    tn = _tile(N // (shards if mode == "nn" else 1) // (N_CHIPS if out_shards else 1), tn)
    tk = _tile(K // (shards if mode == "nt" else 1), tk)
    nk = K // tk
    if mode == "nn":
        a_spec = pl.BlockSpec((tm, tk), lambda i, j, k: (i, k))
        dims = (((1,), (0,)), ((), ()))
        if shards > 1:
            per = N // shards // tn
            b_spec = pl.BlockSpec((None, tk, tn), lambda i, j, k: (j // per, k, j % per))
        else:
            b_spec = pl.BlockSpec((tk, tn), lambda i, j, k: (k, j))
    elif mode == "nt":
        a_spec = pl.BlockSpec((tm, tk), lambda i, j, k: (i, k))
        dims = (((1,), (1,)), ((), ()))
        if shards > 1:
            per = K // shards // tk
            b_spec = pl.BlockSpec((None, tn, tk), lambda i, j, k: (k // per, j, k % per))
        else:
            b_spec = pl.BlockSpec((tn, tk), lambda i, j, k: (j, k))
    else:
        assert shards == 1
        a_spec = pl.BlockSpec((tk, tm), lambda i, j, k: (k, i))
        b_spec = pl.BlockSpec((tk, tn), lambda i, j, k: (k, j))
        dims = (((0,), (0,)), ((), ()))
    if out_shards:
        assert not extras
        oper = N // N_CHIPS // tn
        o_spec = pl.BlockSpec((None, tm, tn), lambda i, j, k: (j // oper, i, j % oper))
        o_shape = (N_CHIPS, M, N // N_CHIPS)
    else:
        o_spec = pl.BlockSpec((tm, tn), lambda i, j, k: (i, j))
        o_shape = (M, N)
    more = [] if more_outs is None else more_outs(tm, tn)
    ex_arrays = [e[0] if isinstance(e, tuple) else e for e in extras]
    ex_specs = [pl.BlockSpec((tm, tn), lambda i, j, k, off=e[1] // tn: (i, off + j)) if isinstance(e, tuple)
                else o_spec for e in extras]
    n_ex, n_out, n_dep = len(extras), len(outs) + len(more), len(deps)
    if epi is None:
        epi = lambda acc: (acc,)
    tails, tail_specs = [], []
    if tail is not None:
        assert mode == "nn"
        tails = list(tail)
        k2 = tail[0].shape[1]
        tail_specs = [pl.BlockSpec((tm, k2), lambda i, j, k: (i, 0)), pl.BlockSpec((k2, tn), lambda i, j, k: (0, j))]
    n_tail = len(tails)

    def body(*refs):
        a_ref, b_ref = refs[0], refs[1]
        ex_refs = refs[2:2 + n_ex]
        t_refs = refs[2 + n_ex:2 + n_ex + n_tail]
        first_out = 2 + n_ex + n_tail + n_dep
        o_refs = refs[first_out:first_out + n_out]
        part = lax.dot_general(a_ref[...].astype(BF16), b_ref[...].astype(BF16), dims,
                               preferred_element_type=F32)

        def finish(acc):
            if n_tail:
                acc = acc + lax.dot_general(t_refs[0][...].astype(BF16), t_refs[1][...].astype(BF16), dims,
                                            preferred_element_type=F32)
            vals = epi(acc, *[r[...] for r in ex_refs])
            for r, v in zip(o_refs, vals):
                if isinstance(v, (list, tuple)):
                    for lead, piece in enumerate(v):
                        r[lead] = piece.astype(r.dtype)
                else:
                    r[...] = v.astype(r.dtype)

        if nk == 1:
            finish(part)
        else:
            acc_ref = refs[-1]
            k = pl.program_id(2)

            @pl.when(k == 0)
            def _():
                acc_ref[...] = part

            @pl.when(k > 0)
            def _():
                acc_ref[...] += part

            @pl.when(k == nk - 1)
            def _():
                finish(acc_ref[...])

    res = _pallas(
        body, name=name, grid=(M // tm, N // tn, nk),
        in_specs=[a_spec, b_spec] + ex_specs + tail_specs + [_ANY] * n_dep,
        out_specs=[o_spec] * len(outs) + [spec for _, spec in more],
        out_shape=[jax.ShapeDtypeStruct(o_shape, d) for d in outs] + [shape for shape, _ in more],
        scratch_shapes=[pltpu.VMEM((tm, tn), F32)] if nk > 1 else [],
        compiler_params=_params(("parallel", "parallel", "arbitrary")),
    )(a, b, *ex_arrays, *tails, *deps)
    return res[0] if n_out == 1 else res


def _rmsnorm_fwd(x, g, *, name, width=None, col=0, tr=256):
    S = x.shape[0]
    W = x.shape[1] if width is None else width
    tr = _tile(S, tr)

    def body(x_ref, g_ref, y_ref, r_ref):
        xv = x_ref[...]
        rstd = lax.rsqrt(jnp.mean(xv * xv, axis=-1, keepdims=True) + EPS)
        y_ref[...] = (xv * rstd * g_ref[...]).astype(BF16)
        r_ref[...] = rstd

    return _pallas(
        body, name=name, grid=(S // tr,),
        in_specs=[pl.BlockSpec((tr, W), lambda i: (i, col)), pl.BlockSpec((1, W), lambda i: (0, 0))],
        out_specs=[pl.BlockSpec((tr, W), lambda i: (i, 0)), pl.BlockSpec((tr, 1), lambda i: (i, 0))],
        out_shape=[jax.ShapeDtypeStruct((S, W), BF16), jax.ShapeDtypeStruct((S, 1), F32)],
        compiler_params=_params(("parallel",)),
    )(x, g)


def _rmsnorm_bwd(dy, x, rstd, g, *, name, res=None, into=None, deps=(), bf16_copy=0, width=None, col=0, tr=256):
    S = x.shape[0]
    W = x.shape[1] if width is None else width
    tr = _tile(S, tr)
    has_res = res is not None

    def body(*refs):
        dy_ref, x_ref, r_ref, g_ref = refs[:4]
        dx_ref, dg_ref = refs[-2 - bf16_copy], refs[-1 - bf16_copy]
        rstd_v = r_ref[...]
        xhat = x_ref[...] * rstd_v
        dyv = dy_ref[...].astype(F32)
        dyg = dyv * g_ref[...]
        dx = rstd_v * (dyg - xhat * jnp.mean(dyg * xhat, axis=-1, keepdims=True))
        if has_res:
            dx = dx + refs[4][...]
        dx_ref[...] = dx.astype(dx_ref.dtype)
        if bf16_copy:
            refs[-1][...] = dx.astype(BF16)
        part = jnp.sum(dyv * xhat, axis=0, keepdims=True)

        @pl.when(pl.program_id(0) == 0)
        def _():
            dg_ref[...] = part

        @pl.when(pl.program_id(0) > 0)
        def _():
            dg_ref[...] += part

    row = pl.BlockSpec((tr, W), lambda i: (i, 0))
    ins = [dy, x, rstd, g] + ([res] if has_res else [])
    in_specs = [row, pl.BlockSpec((tr, W), lambda i: (i, col)), pl.BlockSpec((tr, 1), lambda i: (i, 0)),
                pl.BlockSpec((1, W), lambda i: (0, 0))] + ([row] if has_res else [])
    if into is None:
        dx_spec, dx_shape, alias = row, jax.ShapeDtypeStruct((S, W), F32), {}
    else:
        buf, col_out = into
        ins.append(buf)
        in_specs.append(_ANY)
        dx_spec = pl.BlockSpec((tr, W), lambda i: (i, col_out))
        dx_shape = jax.ShapeDtypeStruct(buf.shape, buf.dtype)
        alias = {len(ins) - 1: 0}
    ins += list(deps)
    in_specs += [_ANY] * len(deps)
    return _pallas(
        body, name=name, grid=(S // tr,), in_specs=in_specs,
        out_specs=[dx_spec, pl.BlockSpec((1, W), lambda i: (0, 0))] + [row] * bf16_copy,
        out_shape=[dx_shape, jax.ShapeDtypeStruct((1, W), F32)] + [jax.ShapeDtypeStruct((S, W), BF16)] * bf16_copy,
        input_output_aliases=alias,
        compiler_params=_params(("arbitrary",)),
    )(*ins)


def _final_loss(h2, g, target, *, tr=256):
    S, D = h2.shape
    tr = _tile(S, tr)

    def body(h_ref, g_ref, t_ref, loss_ref, dh_ref, dhb_ref, dg_ref):
        hv = h_ref[...]
        rstd = lax.rsqrt(jnp.mean(hv * hv, axis=-1, keepdims=True) + EPS)
        xhat = hv * rstd
        e = xhat * g_ref[...] - t_ref[...]
        lpart = (0.5 / D) * jnp.sum(jnp.sum(e * e, axis=-1, keepdims=True), axis=0, keepdims=True)
        dy = e * (1.0 / D)
        dyg = dy * g_ref[...]
        dh = rstd * (dyg - xhat * jnp.mean(dyg * xhat, axis=-1, keepdims=True))
        dh_ref[...] = dh
        dhb_ref[...] = dh.astype(BF16)
        gpart = jnp.sum(dy * xhat, axis=0, keepdims=True)

        @pl.when(pl.program_id(0) == 0)
        def _():
            loss_ref[...] = lpart
            dg_ref[...] = gpart

        @pl.when(pl.program_id(0) > 0)
        def _():
            loss_ref[...] += lpart
            dg_ref[...] += gpart

    row = pl.BlockSpec((tr, D), lambda i: (i, 0))
    vec = pl.BlockSpec((1, D), lambda i: (0, 0))
    return _pallas(
        body, name="final_loss", grid=(S // tr,), in_specs=[row, vec, row],
        out_specs=[pl.BlockSpec((1, 1), lambda i: (0, 0)), row, row, vec],
        out_shape=[jax.ShapeDtypeStruct((1, 1), F32), jax.ShapeDtypeStruct((S, D), F32),
                   jax.ShapeDtypeStruct((S, D), BF16), jax.ShapeDtypeStruct((1, D), F32)],
        compiler_params=_params(("arbitrary",)),
    )(h2, g, target)


def _sigmoid(v):
    return 1.0 / (1.0 + jnp.exp(-v))


def _merge_bwd(dmerged, proj, y_ret, y_mla, D, off_gret, *, tr=256):
    S = y_ret.shape[0]
    tr = _tile(S, tr)
    b0 = off_gret // D

    def body(dm_ref, g_ref, yr_ref, ym_ref, dp_ref, dyr_ref, dym_ref):
        dm = dm_ref[...]
        sg = _sigmoid(g_ref[...])

        @pl.when(pl.program_id(1) == 0)
        def _():
            dyr_ref[...] = (dm * sg).astype(BF16)
            dp_ref[...] = (dm * yr_ref[...] * sg * (1.0 - sg)).astype(BF16)

        @pl.when(pl.program_id(1) == 1)
        def _():
            dym_ref[...] = (dm * sg).astype(BF16)
            dp_ref[...] = (dm * ym_ref[...] * sg * (1.0 - sg)).astype(BF16)

    blk = pl.BlockSpec((tr, D), lambda i, j: (i, 0))
    return _pallas(
        body, name="merge_bwd", grid=(S // tr, 2),
        in_specs=[blk, pl.BlockSpec((tr, D), lambda i, j: (i, b0 + j)), blk, blk],
        out_specs=[pl.BlockSpec((tr, D), lambda i, j: (i, b0 + j)), blk, blk],
        out_shape=[jax.ShapeDtypeStruct(proj.shape, BF16), jax.ShapeDtypeStruct((S, D), BF16),
                   jax.ShapeDtypeStruct((S, D), BF16)],
        compiler_params=_params(("parallel", "arbitrary")),
    )(dmerged, proj, y_ret, y_mla)


def _rope128(t, cos_full, sin_signed):
    return t * cos_full + pltpu.roll(t, RET_QK // 2, 1) * sin_signed


def _rope128_t(d, cos_full, sin_signed):
    return d * cos_full + pltpu.roll(d * sin_signed, RET_QK // 2, 1)


def _ret_consts(lg, T):
    pos = lax.broadcasted_iota(jnp.int32, (T, 1), 0).astype(F32)
    qd = jnp.exp(lg * (pos + 1.0))
    kd = jnp.exp(lg * (T - 1.0 - pos))
    n = lax.broadcasted_iota(jnp.int32, (T, T), 0)
    m = lax.broadcasted_iota(jnp.int32, (T, T), 1)
    vis = (m // CHUNK) <= (n // CHUNK)
    dist = jnp.abs(n - m).astype(F32)
    decay = jnp.where(vis, jnp.exp(lg * dist), 0.0)
    cdec = jnp.exp(lg * float(T))
    return qd, kd, decay, cdec


def _dot(a, b, dims):
    return lax.dot_general(a.astype(BF16), b.astype(BF16), (dims, ((), ())), preferred_element_type=F32)


NN = ((1,), (0,))
NT = ((1,), (1,))
TN = ((0,), (0,))
_RQ = slice(0, RET_QK)
_RK = slice(RET_QK, 2 * RET_QK)
_RV = slice(2 * RET_QK, 2 * RET_QK + RET_V)
_RG = slice(2 * RET_QK + RET_V, RET_HEAD_COLS)


RET_GROUP = 8


def _head_cols(h, part):
    return slice(h * RET_HEAD_COLS + part.start, h * RET_HEAD_COLS + part.stop)


def _ret_fwd(proj, cosr, sinr, lgam, gain, RH, *, T):
    S = proj.shape[0]
    nb = S // T
    G = _tile(RH, RET_GROUP)
    heads = range(G)
    scale = RET_QK ** -0.5

    def body(p_ref, cos_ref, sin_ref, lg_ref, gain_ref, ry_ref, gated_ref, st_ref, state):
        b = pl.program_id(1)

        @pl.when(b == 0)
        def _():
            state[...] = jnp.zeros_like(state)

        consts = [_ret_consts(lg_ref[h, 0:1, 0:1], T) for h in heads]
        cosv, sinv = cos_ref[...], sin_ref[...]
        q = [_rope128(p_ref[:, _head_cols(h, _RQ)], cosv, sinv) for h in heads]
        k = [_rope128(p_ref[:, _head_cols(h, _RK)], cosv, sinv) * scale for h in heads]
        v = [p_ref[:, _head_cols(h, _RV)] for h in heads]
        sprev = [state[h] for h in heads]
        for h in heads:
            st_ref[h] = sprev[h]
        a = [_dot(q[h], k[h], NT) for h in heads]
        qs = [_dot(q[h] * consts[h][0], sprev[h], NN) for h in heads]
        kv = [_dot(k[h] * consts[h][1], v[h], TN) for h in heads]
        o = [_dot(a[h] * consts[h][2], v[h], NN) + qs[h] for h in heads]
        for h in heads:
            state[h] = sprev[h] * consts[h][3] + kv[h]
            vals = slice(h * RET_V, (h + 1) * RET_V)
            ry_ref[:, vals] = o[h]
            mu = jnp.mean(o[h], axis=-1, keepdims=True)
            oc = o[h] - mu
            var = jnp.mean(oc * oc, axis=-1, keepdims=True)
            t = oc * lax.rsqrt(var + EPS) * gain_ref[:, vals]
            gv = p_ref[:, _head_cols(h, _RG)]
            gated_ref[:, vals] = (t * (gv * _sigmoid(gv))).astype(BF16)

    return _pallas(
        body, name="ret_fwd", grid=(RH // G, nb),
        in_specs=[pl.BlockSpec((T, G * RET_HEAD_COLS), lambda h, b: (b, h)),
                  pl.BlockSpec((T, RET_QK), lambda h, b: (b, 0)),
                  pl.BlockSpec((T, RET_QK), lambda h, b: (b, 0)),
                  pl.BlockSpec((G, 8, LANES), lambda h, b: (h, 0, 0)),
                  pl.BlockSpec((1, G * RET_V), lambda h, b: (0, h))],
        out_specs=[pl.BlockSpec((T, G * RET_V), lambda h, b: (b, h)),
                   pl.BlockSpec((T, G * RET_V), lambda h, b: (b, h)),
                   pl.BlockSpec((G, None, RET_QK, RET_V), lambda h, b: (h, b, 0, 0))],
        out_shape=[jax.ShapeDtypeStruct((S, RH * RET_V), F32), jax.ShapeDtypeStruct((S, RH * RET_V), BF16),
                   jax.ShapeDtypeStruct((RH, nb, RET_QK, RET_V), F32)],
        scratch_shapes=[pltpu.VMEM((G, RET_QK, RET_V), F32)],
        compiler_params=_params(("parallel", "arbitrary")),
    )(proj, cosr, sinr, lgam, gain)


def _ret_bwd(proj, cosr, sinr, lgam, gain, ry, dgated, states, dproj, RH, *, T):
    S = proj.shape[0]
    nb = S // T
    G = _tile(RH, RET_GROUP)
    heads = range(G)
    scale = RET_QK ** -0.5

    def body(p_ref, cos_ref, sin_ref, lg_ref, gain_ref, ry_ref, dg_ref, st_ref, _, dp_ref, dgain_ref, dstate):
        b = pl.program_id(1)

        @pl.when(b == 0)
        def _():
            dstate[...] = jnp.zeros_like(dstate)

        consts = [_ret_consts(lg_ref[h, 0:1, 0:1], T) for h in heads]
        qd, kd, decay, cdec = [[c[i] for c in consts] for i in range(4)]
        cosv, sinv = cos_ref[...], sin_ref[...]
        q = [_rope128(p_ref[:, _head_cols(h, _RQ)], cosv, sinv) for h in heads]
        k = [_rope128(p_ref[:, _head_cols(h, _RK)], cosv, sinv) * scale for h in heads]
        v = [p_ref[:, _head_cols(h, _RV)] for h in heads]
        sprev = [st_ref[h] for h in heads]
        ds_new = [dstate[h] for h in heads]
        a = [_dot(q[h], k[h], NT) for h in heads]
        do, gparts = [], []
        for h in heads:
            vals = slice(h * RET_V, (h + 1) * RET_V)
            o = ry_ref[:, vals]
            mu = jnp.mean(o, axis=-1, keepdims=True)
            oc = o - mu
            rstd = lax.rsqrt(jnp.mean(oc * oc, axis=-1, keepdims=True) + EPS)
            ryn = oc * rstd
            gainv = gain_ref[:, vals]
            gv = p_ref[:, _head_cols(h, _RG)]
            sg = _sigmoid(gv)
            dgt = dg_ref[:, vals]
            dt = dgt * (gv * sg)
            dp_ref[:, _head_cols(h, _RG)] = (dgt * (ryn * gainv) * (sg * (1.0 + gv * (1.0 - sg)))).astype(BF16)
            gparts.append(jnp.sum(dt * ryn, axis=0, keepdims=True))
            dryn = dt * gainv
            do.append(rstd * (dryn - jnp.mean(dryn, axis=-1, keepdims=True)
                              - ryn * jnp.mean(dryn * ryn, axis=-1, keepdims=True)))
        gpart = jnp.concatenate(gparts, axis=1)

        @pl.when(b == 0)
        def _():
            dgain_ref[...] = gpart

        @pl.when(b > 0)
        def _():
            dgain_ref[...] += gpart

        dpm = [_dot(do[h], v[h], NT) for h in heads]
        dq_s = [_dot(do[h], sprev[h], NT) for h in heads]
        dk_s = [_dot(v[h], ds_new[h], NT) for h in heads]
        dv_s = [_dot(k[h] * kd[h], ds_new[h], NN) for h in heads]
        dst = [_dot(q[h] * qd[h], do[h], TN) for h in heads]
        a = [a[h] * decay[h] for h in heads]
        dpm = [dpm[h] * decay[h] for h in heads]
        dv = [_dot(a[h], do[h], TN) + dv_s[h] for h in heads]
        dq = [_dot(dpm[h], k[h], NN) + dq_s[h] * qd[h] for h in heads]
        dk = [(_dot(dpm[h], q[h], TN) + dk_s[h] * kd[h]) * scale for h in heads]
        for h in heads:
            dstate[h] = ds_new[h] * cdec[h] + dst[h]
            dp_ref[:, _head_cols(h, _RV)] = dv[h].astype(BF16)
            dp_ref[:, _head_cols(h, _RQ)] = _rope128_t(dq[h], cosv, sinv).astype(BF16)
            dp_ref[:, _head_cols(h, _RK)] = _rope128_t(dk[h], cosv, sinv).astype(BF16)

    rb = lambda b: nb - 1 - b
    return _pallas(
        body, name="ret_bwd", grid=(RH // G, nb),
        in_specs=[pl.BlockSpec((T, G * RET_HEAD_COLS), lambda h, b: (rb(b), h)),
                  pl.BlockSpec((T, RET_QK), lambda h, b: (rb(b), 0)),
                  pl.BlockSpec((T, RET_QK), lambda h, b: (rb(b), 0)),
                  pl.BlockSpec((G, 8, LANES), lambda h, b: (h, 0, 0)),
                  pl.BlockSpec((1, G * RET_V), lambda h, b: (0, h)),
                  pl.BlockSpec((T, G * RET_V), lambda h, b: (rb(b), h)),
                  pl.BlockSpec((T, G * RET_V), lambda h, b: (rb(b), h)),
                  pl.BlockSpec((G, None, RET_QK, RET_V), lambda h, b: (h, rb(b), 0, 0)),
                  _ANY],
        out_specs=[pl.BlockSpec((T, G * RET_HEAD_COLS), lambda h, b: (rb(b), h)),
                   pl.BlockSpec((1, G * RET_V), lambda h, b: (0, h))],
        out_shape=[jax.ShapeDtypeStruct(dproj.shape, dproj.dtype), jax.ShapeDtypeStruct((1, RH * RET_V), F32)],
        scratch_shapes=[pltpu.VMEM((G, RET_QK, RET_V), F32)],
        input_output_aliases={8: 0},
        compiler_params=_params(("parallel", "arbitrary")),
    )(proj, cosr, sinr, lgam, gain, ry, dgated, states, dproj)


def _rope_pe(t, c, s1, s2):
    return t * c + pltpu.roll(t, LANES - QK_ROPE // 2, 1) * s1 + pltpu.roll(t, QK_ROPE // 2, 1) * s2


def _rope_pe_t(d, c, s1, s2):
    return d * c + pltpu.roll(d * s1, QK_ROPE // 2, 1) + pltpu.roll(d * s2, LANES - QK_ROPE // 2, 1)


ATTN_C2 = (QK_NOPE + QK_ROPE) ** -0.5 * LOG2E


def _qkv_proj(cqn, ckvn, wq, wkv, kpe, tabs, MH, *, tm=512, heads=4):
    S = cqn.shape[0]
    tm = _tile(S, tm)
    hb = _tile(MH, heads)
    W = 2 * LANES
    c_t, s1_t, s2_t = tabs

    def body(cq_ref, ckv_ref, wq_ref, wkv_ref, kpe_ref, c_ref, s1_ref, s2_ref, qf_ref, kf_ref, v_ref):
        c, s1, s2 = c_ref[...], s1_ref[...], s2_ref[...]
        q = _dot(cq_ref[...], wq_ref[...], NN)
        kv = _dot(ckv_ref[...], wkv_ref[...], NN)
        kper = _rope_pe(kpe_ref[...], c, s1, s2).astype(BF16)
        for h in range(hb):
            lo, mid, hi = h * W, h * W + QK_NOPE, (h + 1) * W
            qf_ref[:, lo:mid] = (q[:, lo:mid] * ATTN_C2).astype(BF16)
            qf_ref[:, mid:hi] = (_rope_pe(q[:, mid:hi], c, s1, s2) * ATTN_C2).astype(BF16)
            kf_ref[:, lo:mid] = kv[:, lo:mid].astype(BF16)
            kf_ref[:, mid:hi] = kper
            v_ref[:, h * V_HEAD:(h + 1) * V_HEAD] = kv[:, mid:hi].astype(BF16)

    tab = pl.BlockSpec((tm, LANES), lambda i, j: (i, 0))
    grp = pl.BlockSpec((tm, hb * W), lambda i, j: (i, j))
    return _pallas(
        body, name="qkv_proj", grid=(S // tm, MH // hb),
        in_specs=[pl.BlockSpec((tm, cqn.shape[1]), lambda i, j: (i, 0)),
                  pl.BlockSpec((tm, ckvn.shape[1]), lambda i, j: (i, 0)),
                  pl.BlockSpec((wq.shape[0], hb * W), lambda i, j: (0, j)),
                  pl.BlockSpec((wkv.shape[0], hb * W), lambda i, j: (0, j)), tab, tab, tab, tab],
        out_specs=[grp, grp, pl.BlockSpec((tm, hb * V_HEAD), lambda i, j: (i, j))],
        out_shape=[jax.ShapeDtypeStruct((S, MH * W), BF16)] * 2 + [jax.ShapeDtypeStruct((S, MH * V_HEAD), BF16)],
        compiler_params=_params(("parallel", "parallel")),
    )(cqn, ckvn, wq, wkv, kpe, c_t, s1_t, s2_t)


def _chunk_mask(T):
    n = lax.broadcasted_iota(jnp.int32, (T, T), 0)
    m = lax.broadcasted_iota(jnp.int32, (T, T), 1)
    return (m // CHUNK) <= (n // CHUNK)


def _lanes_to(v, width):
    return jnp.tile(v, (1, width // LANES))


def _attn_fwd(qf, kf, vb, MH, *, T, heads, name, prev=(), deps=()):
    S = qf.shape[0]
    nt = S // T
    n_skip = len(prev) + len(deps)

    def body(q_ref, k_ref, v_ref, *rest):
        o_ref, ob_ref, lse_ref, m_sc, l_sc, acc_sc, s_a, s_b = rest[n_skip:]
        qi = pl.program_id(1)
        m_sc[...] = jnp.full_like(m_sc, NEG)
        l_sc[...] = jnp.zeros_like(l_sc)
        acc_sc[...] = jnp.zeros_like(acc_sc)

        def rows_of(kt):
            return pl.ds(pl.multiple_of(kt * T, T), T)

        def scores(kt):
            return _dot(q_ref[...], k_ref[rows_of(kt), :], NT)

        def update(s, kt):
            m_prev = m_sc[...]
            m_new = jnp.maximum(m_prev, jnp.max(s, axis=-1, keepdims=True))
            alpha = jnp.exp2(m_prev - m_new)
            p = jnp.exp2(s - _lanes_to(m_new, T))
            l_sc[...] = alpha * l_sc[...] + jnp.sum(p, axis=-1, keepdims=True)
            acc_sc[...] = alpha * acc_sc[...] + _dot(p, v_ref[rows_of(kt), :], NN)
            m_sc[...] = m_new

        def masked(s):
            return jnp.where(_chunk_mask(T), s, NEG)

        @pl.when(qi == 0)
        def _():
            update(masked(scores(0)), 0)

        @pl.when(qi > 0)
        def _():
            s_a[...] = masked(scores(qi))
            s_b[...] = scores(0)
            update(s_a[...], qi)
            s_a[...] = scores(jnp.minimum(1, qi - 1))
            update(s_b[...], 0)

            def pair(j, carry):
                s_b[...] = scores(2 * j)
                update(s_a[...], 2 * j - 1)
                s_a[...] = scores(jnp.minimum(2 * j + 1, qi - 1))
                update(s_b[...], 2 * j)
                return carry

            lax.fori_loop(1, (qi + 1) // 2, pair, 0)

            @pl.when(qi % 2 == 0)
            def _():
                update(s_a[...], qi - 1)
        l = l_sc[...]
        o = acc_sc[...] / l
        o_ref[...] = o
        ob_ref[...] = o.astype(BF16)
        lse_ref[...] = m_sc[...] + jnp.log(l) * LOG2E

    h0, h1 = heads
    out_shape = [jax.ShapeDtypeStruct((S, MH * LANES), F32), jax.ShapeDtypeStruct((S, MH * LANES), BF16),
                 jax.ShapeDtypeStruct((MH, S, LANES), F32)]
    row = pl.BlockSpec((T, LANES), lambda h, i: (i, h0 + h))
    return _pallas(
        body, name=name, grid=(h1 - h0, nt),
        in_specs=[pl.BlockSpec((T, 2 * LANES), lambda h, i: (i, h0 + h)),
                  pl.BlockSpec((S, 2 * LANES), lambda h, i: (0, h0 + h)),
                  pl.BlockSpec((S, LANES), lambda h, i: (0, h0 + h))] + [_ANY] * (len(prev) + len(deps)),
        out_specs=[row, row, pl.BlockSpec((None, T, LANES), lambda h, i: (h0 + h, i, 0))],
        out_shape=out_shape,
        scratch_shapes=[pltpu.VMEM((T, LANES), F32), pltpu.VMEM((T, LANES), F32), pltpu.VMEM((T, LANES), F32),
                        pltpu.VMEM((T, T), F32), pltpu.VMEM((T, T), F32)],
        input_output_aliases={3 + i: i for i in range(len(prev))},
        compiler_params=_params(("parallel", "parallel")),
    )(qf, kf, vb, *prev, *deps)


def _attn_bwd(qf, kf, vb, dob, lse2, delta, tabs, MH, *, T, deps=()):
    S = qf.shape[0]
    nt = S // T
    scale = (QK_NOPE + QK_ROPE) ** -0.5
    n_dep = len(deps)

    def body(q_ref, k_ref, v_ref, do_ref, lse_ref, dl_ref, c_ref, s1_ref, s2_ref, *rest):
        dqa_ref, dkv_ref, dkpe_ref, dq_ref, dk_sc, dv_sc, s_a, dp_a, s_b, dp_b = rest[n_dep:]
        kj = pl.program_id(1)

        @pl.when(kj == 0)
        def _():
            dq_ref[...] = jnp.zeros_like(dq_ref)

        dk_sc[...] = jnp.zeros_like(dk_sc)
        dv_sc[...] = jnp.zeros_like(dv_sc)

        def rows_of(qt):
            return pl.ds(pl.multiple_of(qt * T, T), T)

        def products(qt):
            rows = rows_of(qt)
            return _dot(q_ref[rows, :], k_ref[...], NT), _dot(do_ref[rows, :], v_ref[...], NT)

        def update(s, dp, qt):
            rows = rows_of(qt)
            q, dov = q_ref[rows, :], do_ref[rows, :]
            p = jnp.exp2(s - _lanes_to(lse_ref[rows, :], T))
            ds = p * (dp - _lanes_to(dl_ref[rows, :], T))
            dv_sc[...] += _dot(p, dov, TN)
            dk_sc[...] += _dot(ds, q, TN)
            dq_ref[rows, :] += _dot(ds, k_ref[...], NN)

        def masked(s):
            return jnp.where(_chunk_mask(T), s, NEG)

        @pl.when(kj == nt - 1)
        def _():
            s, dp = products(kj)
            update(masked(s), dp, kj)

        @pl.when(kj < nt - 1)
        def _():
            s, dp = products(kj)
            s_a[...], dp_a[...] = masked(s), dp
            s_b[...], dp_b[...] = products(kj + 1)
            update(s_a[...], dp_a[...], kj)
            s_a[...], dp_a[...] = products(jnp.minimum(kj + 2, nt - 1))
            update(s_b[...], dp_b[...], kj + 1)

            def pair(j, carry):
                t0 = kj + 2 * j
                s_b[...], dp_b[...] = products(t0 + 1)
                update(s_a[...], dp_a[...], t0)
                s_a[...], dp_a[...] = products(jnp.minimum(t0 + 2, nt - 1))
                update(s_b[...], dp_b[...], t0 + 1)
                return carry

            lax.fori_loop(1, (nt - kj) // 2, pair, 0)

            @pl.when((nt - kj) % 2 == 1)
            def _():
                update(s_a[...], dp_a[...], nt - 1)
        dkv_ref[:, :QK_NOPE] = (dk_sc[:, :QK_NOPE] * (1.0 / LOG2E)).astype(BF16)
        dkv_ref[:, QK_NOPE:] = dv_sc[...].astype(BF16)
        dkpe_ref[...] = dk_sc[:, QK_NOPE:] * (1.0 / LOG2E)

        @pl.when(kj == nt - 1)
        def _():
            dqa_ref[:, :QK_NOPE] = (dq_ref[:, :QK_NOPE] * scale).astype(BF16)
            dqa_ref[:, QK_NOPE:] = (_rope_pe_t(dq_ref[:, QK_NOPE:], c_ref[...], s1_ref[...], s2_ref[...])
                                    * scale).astype(BF16)

    stat = pl.BlockSpec((None, S, LANES), lambda h, j: (h, 0, 0))
    tab = pl.BlockSpec((S, LANES), lambda h, j: (0, 0))
    return _pallas(
        body, name="attn_bwd", grid=(MH, nt),
        in_specs=[pl.BlockSpec((S, 2 * LANES), lambda h, j: (0, h)),
                  pl.BlockSpec((T, 2 * LANES), lambda h, j: (j, h)),
                  pl.BlockSpec((T, LANES), lambda h, j: (j, h)),
                  pl.BlockSpec((S, LANES), lambda h, j: (0, h)), stat, stat, tab, tab, tab] + [_ANY] * n_dep,
        out_specs=[pl.BlockSpec((S, 2 * LANES), lambda h, j: (0, h)),
                   pl.BlockSpec((T, 2 * LANES), lambda h, j: (j, h)),
                   pl.BlockSpec((T, LANES), lambda h, j: (j, h))],
        out_shape=[jax.ShapeDtypeStruct((S, MH * 2 * LANES), BF16), jax.ShapeDtypeStruct((S, MH * 2 * LANES), BF16),
                   jax.ShapeDtypeStruct((S, MH * LANES), F32)],
        scratch_shapes=[pltpu.VMEM((S, 2 * LANES), F32), pltpu.VMEM((T, 2 * LANES), F32), pltpu.VMEM((T, LANES), F32)]
        + [pltpu.VMEM((T, T), F32)] * 4,
        compiler_params=_params(("parallel", "arbitrary")),
    )(qf, kf, vb, dob, lse2, delta, *tabs, *deps)


def _kpe_sum(dkpe_h, tabs, MH, *, tr=256):
    S = dkpe_h.shape[0]
    tr = _tile(S, tr)

    def body(dk_ref, c_ref, s1_ref, s2_ref, dkpe_ref):
        tot = dk_ref[:, :LANES]
        for h in range(1, MH):
            tot = tot + dk_ref[:, h * LANES:(h + 1) * LANES]
        dkpe_ref[...] = _rope_pe_t(tot, c_ref[...], s1_ref[...], s2_ref[...]).astype(BF16)

    tab = pl.BlockSpec((tr, LANES), lambda i: (i, 0))
    return _pallas(
        body, name="kpe_sum", grid=(S // tr,),
        in_specs=[pl.BlockSpec((tr, MH * LANES), lambda i: (i, 0)), tab, tab, tab],
        out_specs=tab, out_shape=jax.ShapeDtypeStruct((S, LANES), BF16),
        compiler_params=_params(("parallel",)),
    )(dkpe_h, *tabs)


ROW_ALIGN = 16


def _blk(R, C, block_bytes=2 << 20):
    cap = max(ROW_ALIGN, block_bytes // (C * 4))
    for t in range(min(R, cap) // ROW_ALIGN * ROW_ALIGN, LANES - 1, -ROW_ALIGN):
        if R % t == 0:
            return t, C
    if R <= cap:
        return R, C
    tc = C
    while R * tc * 4 > block_bytes and tc % (2 * LANES) == 0:
        tc //= 2
    return R, tc


def _rows_call(fn, ins, out_dtypes, *, name, deps=()):
    R, C = ins[0].shape
    tr, tc = _blk(R, C)
    n_in, n_dep = len(ins), len(deps)

    def body(*refs):
        vals = fn(*[r[...] for r in refs[:n_in]])
        for r, v in zip(refs[n_in + n_dep:], vals):
            r[...] = v.astype(r.dtype)

    blk = pl.BlockSpec((tr, tc), lambda i, j: (i, j))
    res = _pallas(
        body, name=name, grid=(R // tr, C // tc), in_specs=[blk] * n_in + [_ANY] * n_dep,
        out_specs=[blk] * len(out_dtypes),
        out_shape=[jax.ShapeDtypeStruct((R, C), d) for d in out_dtypes],
        compiler_params=_params(("parallel", "parallel")),
    )(*ins, *deps)
    return res


def _adamw_vals(w, g, m, v):
    m = ADAM_B1 * m + (1.0 - ADAM_B1) * g
    v = ADAM_B2 * v + (1.0 - ADAM_B2) * (g * g)
    m_hat = m / (1.0 - ADAM_B1 ** ADAM_STEP)
    v_hat = v / (1.0 - ADAM_B2 ** ADAM_STEP)
    delta = -ADAM_LR * (m_hat / (jnp.sqrt(v_hat) + ADAM_EPS) + ADAM_WD * w)
    return delta, m, v


def _sum_pair(p, theirs, place, *, name):
    _, R, C = p.shape
    R2 = R // 2
    tr, tc = _blk(R2, C)
    p4 = p.reshape(N_CHIPS, 2, R2, C)

    def body(place_ref, a_ref, b_ref, o_ref):
        o_ref[...] = (a_ref[...].astype(F32) + b_ref[...].astype(F32)).astype(BF16)

    spec = pltpu.PrefetchScalarGridSpec(
        num_scalar_prefetch=1, grid=(N_CHIPS, R2 // tr, C // tc),
        in_specs=[pl.BlockSpec((None, None, tr, tc), lambda q, i, j, pr: (q, pr[0], i, j)),
                  pl.BlockSpec((None, tr, tc), lambda q, i, j, pr: (q, i, j))],
        out_specs=pl.BlockSpec((None, tr, tc), lambda q, i, j, pr: (q, i, j)))
    return _pallas(body, name=name, grid_spec=spec, out_shape=jax.ShapeDtypeStruct((N_CHIPS, R2, C), BF16),
                   compiler_params=_params(("parallel", "parallel", "parallel")))(place, p4, theirs)


def _sum_chips(p, theirs, recv, place, *, name):
    _, R, C = p.shape
    R2 = R // 2
    tr, tc = _blk(R2, C)
    p4 = p.reshape(N_CHIPS, 2, R2, C)

    def body(place_ref, a_ref, b_ref, r0_ref, r1_ref, r2_ref, o_ref):
        own = a_ref[...].astype(F32) + b_ref[...].astype(F32)
        o_ref[...] = ((own + r0_ref[...].astype(F32)) + r1_ref[...].astype(F32)) + r2_ref[...].astype(F32)

    def slot(k):
        return pl.BlockSpec((None, tr, tc), lambda i, j, pr: (k, i, j))

    spec = pltpu.PrefetchScalarGridSpec(
        num_scalar_prefetch=1, grid=(R2 // tr, C // tc),
        in_specs=[pl.BlockSpec((None, None, tr, tc), lambda i, j, pr: (pr[1], pr[0], i, j)),
                  pl.BlockSpec((None, tr, tc), lambda i, j, pr: (pr[1], i, j)), slot(0), slot(1), slot(2)],
        out_specs=pl.BlockSpec((None, tr, tc), lambda i, j, pr: (pr[0], i, j)))
    return _pallas(body, name=name, grid_spec=spec, out_shape=jax.ShapeDtypeStruct((2, R2, C), F32),
                   compiler_params=_params(("parallel", "parallel")))(place, p4, theirs, recv, recv, recv)


def _me():
    return lax.axis_index("x"), lax.axis_index("y"), lax.axis_index("c")


def _other_chips(x, y):
    return [(1 - x, y), (x, 1 - y), (1 - x, 1 - y)]


def _rcopy(src, dst, ssem, rsem, dev):
    return pltpu.make_async_remote_copy(src_ref=src, dst_ref=dst, send_sem=ssem, recv_sem=rsem,
                                        device_id=dev, device_id_type=MESH)


def _cast_into_slot(w, place, *, name, rows=None, deps=()):
    R, C = w.shape
    rows = R if rows is None else rows
    tr, tc = _blk(R, C)

    def body(place_ref, w_ref, *rest):
        rest[-1][...] = w_ref[...].astype(BF16)

    spec = pltpu.PrefetchScalarGridSpec(
        num_scalar_prefetch=1, grid=(R // tr, C // tc),
        in_specs=[pl.BlockSpec((tr, tc), lambda i, j, pr: (i, j))] + [_ANY] * len(deps),
        out_specs=pl.BlockSpec((None, tr, tc), lambda i, j, pr: (pr[1], i, j)))
    out = _pallas(body, name=name, grid_spec=spec, out_shape=jax.ShapeDtypeStruct((N_CHIPS, rows, C), BF16),
                  compiler_params=_params(("parallel", "parallel")))(place, w, *deps)
    return out.reshape(N_CHIPS, 2, rows // 2, C)


def _gather_ici_plan(bufs):
    x, y, c = _me()
    j = 2 * x + y
    plan = []
    for i, buf in enumerate(bufs):
        for k, (px, py) in enumerate(_other_chips(x, y)):
            plan.append((3 * i + k, buf.at[j, c], buf.at[j, c], (px, py, c)))
    return plan


def _forward_halves(bufs, *, name):
    n = len(bufs)

    def body(*refs):
        outs = refs[n:2 * n]
        ssem, rsem = refs[2 * n:]
        x, y, c = _me()
        sib = (x, y, 1 - c)
        cps = []
        for i in range(n):
            for k, (px, py) in enumerate(_other_chips(x, y)):
                slot = outs[i].at[2 * px + py, c]
                r = _rcopy(slot, slot, ssem.at[3 * i + k], rsem.at[3 * i + k], sib)
                r.start()
                cps.append(r)
        for r in cps:
            r.wait()

    return _pallas(
        body, name=name, in_specs=[_ANY] * n, out_specs=[_ANY] * n,
        out_shape=[jax.ShapeDtypeStruct(b.shape, b.dtype) for b in bufs],
        scratch_shapes=[pltpu.SemaphoreType.DMA((3 * n,))] * 2,
        input_output_aliases={i: i for i in range(n)},
        compiler_params=pltpu.CompilerParams(has_side_effects=True),
    )(*bufs)


_HBM = pl.BlockSpec(memory_space=pltpu.HBM)
_SEM = pl.BlockSpec(memory_space=pltpu.SEMAPHORE)
_EFFECT = pltpu.SideEffectType.DATAFLOW_SIDE_EFFECTING


def _split_start(bufs, plan, n_copies, *, name):
    n = len(bufs)

    def body(*refs):
        ssem, rsem = refs[n], refs[n + 1]
        for s, src, dst, dev in plan(refs[:n]):
            _rcopy(src, dst, ssem.at[s], rsem.at[s], dev).start()
        refs[-1][...] = jnp.zeros_like(refs[-1])

    res = _pallas(
        body, name=name, in_specs=[_HBM] * n,
        out_specs=(_SEM, _SEM, *[_HBM] * n, pl.BlockSpec(memory_space=pltpu.VMEM)),
        out_shape=(pltpu.SemaphoreType.DMA((n_copies,)), pltpu.SemaphoreType.DMA((n_copies,)),
                   *[pltpu.HBM(b.shape, b.dtype) for b in bufs], jax.ShapeDtypeStruct((8, LANES), F32)),
        input_output_aliases={i: 2 + i for i in range(n)},
        compiler_params=pltpu.CompilerParams(has_side_effects=_EFFECT),
    )(*[pltpu.with_memory_space_constraint(b, pltpu.HBM) for b in bufs])
    return res[0], res[1], list(res[2:2 + n]), res[-1]


def _split_wait(ssem, rsem, bufs, after, plan, *, name):
    n = len(bufs)

    def body(*refs):
        ssem_ref, rsem_ref = refs[n], refs[n + 1]
        for s, src, dst, dev in plan(refs[:n]):
            cp = _rcopy(src, dst, ssem_ref.at[s], rsem_ref.at[s], dev)
            cp.wait_send()
            cp.wait_recv()

    return list(_pallas(
        body, name=name, in_specs=[_HBM] * n + [_SEM, _SEM, _ANY], out_specs=[_HBM] * n,
        out_shape=[pltpu.HBM(b.shape, b.dtype) for b in bufs],
        input_output_aliases={i: i for i in range(n)},
        compiler_params=pltpu.CompilerParams(has_side_effects=_EFFECT),
    )(*bufs, ssem, rsem, after))


def _forward_plan(bufs):
    x, y, c = _me()
    plan = []
    for i, buf in enumerate(bufs):
        for k, (px, py) in enumerate(_other_chips(x, y)):
            plan.append((3 * i + k, buf.at[2 * px + py, c], buf.at[2 * px + py, c], (x, y, 1 - c)))
    return plan


def _join_plan(bufs):
    x, y, c = _me()
    return [(i, buf.at[c], buf.at[c], (x, y, 1 - c)) for i, buf in enumerate(bufs)]


def _swap_plan(n):
    def plan(bufs):
        x, y, c = _me()
        return [(i, bufs[i].at[:, 1 - c], bufs[n + i], (x, y, 1 - c)) for i in range(n)]
    return plan


def _scatter_plan(n):
    def plan(bufs):
        x, y, c = _me()
        out = []
        for i in range(n):
            for k, (px, py) in enumerate(_other_chips(x, y)):
                out.append((3 * i + k, bufs[i].at[2 * px + py], bufs[n + i].at[k], (px, py, c)))
        return out
    return plan


def _allreduce_small(parts, loss11):
    n = len(parts)
    widths = [p.shape[1] for p in parts]
    total = sum(widths) + LANES

    def body(*refs):
        o_ref, mine, buf, ssem, rsem = refs[n + 1:]
        x, y, c = _me()
        me = 4 * x + 2 * y + c
        off = 0
        for r, w in zip(refs[:n], widths):
            mine[:, off:off + w] = r[...]
            off += w
        mine[:, off:] = jnp.broadcast_to(refs[n][...], (1, LANES))
        buf[me] = mine[...]
        cps = []
        for k in range(1, 8):
            peer = (x ^ (k >> 2), y ^ ((k >> 1) & 1), c ^ (k & 1))
            r = _rcopy(mine, buf.at[me], ssem.at[k - 1], rsem.at[k - 1], peer)
            r.start()
            cps.append(r)
        for k in range(1, 8):
            peer = (x ^ (k >> 2), y ^ ((k >> 1) & 1), c ^ (k & 1))
            pid = 4 * peer[0] + 2 * peer[1] + peer[2]
            _rcopy(mine, buf.at[pid], ssem.at[k - 1], rsem.at[k - 1], peer).wait_recv()
        for r in cps:
            r.wait_send()
        tot = buf[0]
        for d in range(1, 8):
            tot = tot + buf[d]
        o_ref[...] = tot

    vm = pl.BlockSpec(memory_space=pltpu.VMEM)
    return _pallas(
        body, name="allreduce_small", in_specs=[vm] * (n + 1), out_specs=vm,
        out_shape=jax.ShapeDtypeStruct((1, total), F32),
        scratch_shapes=[pltpu.VMEM((1, total), F32), pltpu.VMEM((8, 1, total), F32),
                        pltpu.SemaphoreType.DMA((7,)), pltpu.SemaphoreType.DMA((7,))],
        compiler_params=pltpu.CompilerParams(has_side_effects=True),
    )(*parts, loss11)


def _adamw_small(red, ws, ms, vs):
    n = len(ws)

    def body(*refs):
        red_ref = refs[0]
        outs = refs[1 + 3 * n:]
        off = 0
        for i in range(n):
            w = refs[1 + i].shape[1]
            g = red_ref[:, off:off + w]
            d, m, v = _adamw_vals(refs[1 + i][...], g, refs[1 + n + i][...], refs[1 + 2 * n + i][...])
            for o, val in zip(outs[4 * i:4 * i + 4], (g, d, m, v)):
                o[...] = val
            off += w

    vm = pl.BlockSpec(memory_space=pltpu.VMEM)
    res = _pallas(
        body, name="adamw_small", in_specs=[vm] * (1 + 3 * n), out_specs=[vm] * (4 * n),
        out_shape=[jax.ShapeDtypeStruct(w.shape, F32) for w in ws for _ in range(4)],
    )(red, *ws, *ms, *vs)
    return [res[4 * i:4 * i + 4] for i in range(n)]


def _rope_tables(positions, S):
    pos = positions.reshape(S, 1).astype(F32)
    half = RET_QK // 2
    inv = ROPE_THETA ** (-jnp.arange(half, dtype=F32) / half)
    ang = pos * inv
    cosr = jnp.concatenate([jnp.cos(ang), jnp.cos(ang)], axis=1)
    sinr = jnp.concatenate([-jnp.sin(ang), jnp.sin(ang)], axis=1)
    half = QK_ROPE // 2
    inv = ROPE_THETA ** (-jnp.arange(half, dtype=F32) / half)
    ang = pos * inv
    z = jnp.zeros((S, half), F32)
    c = jnp.concatenate([jnp.cos(ang), jnp.cos(ang), z, z], axis=1)
    s1 = jnp.concatenate([-jnp.sin(ang), z, z, z], axis=1)
    s2 = jnp.concatenate([z, jnp.sin(ang), z, z], axis=1)
    return cosr, sinr, (c, s1, s2)


def _cat_cols(g):
    return jnp.concatenate([g[j] for j in range(N_CHIPS)], axis=1)


def _split_cols(w):
    return jnp.stack(jnp.split(w, N_CHIPS, axis=1))


def kernel(x, positions, norm_mix_g, w_in, ret_norm_g, w_ret_o, q_a_norm_g, w_q_b, kv_a_norm_g, w_kv_b, w_mla_o, w_out, norm_mlp_g, w_up, w_down, norm_f_g, loss_target, m_norm_mix_g, m_w_in, m_ret_norm_g, m_w_ret_o, m_q_a_norm_g, m_w_q_b, m_kv_a_norm_g, m_w_kv_b, m_w_mla_o, m_w_out, m_norm_mlp_g, m_w_up, m_w_down, m_norm_f_g, v_norm_mix_g, v_w_in, v_ret_norm_g, v_w_ret_o, v_q_a_norm_g, v_w_q_b, v_kv_a_norm_g, v_w_kv_b, v_w_mla_o, v_w_out, v_norm_mlp_g, v_w_up, v_w_down, v_norm_f_g):
    S, D = x.shape[1], x.shape[2]
    RVW = w_ret_o.shape[1] * N_CHIPS
    RH = RVW // RET_V
    RQW = RH * RET_QK
    MVW = w_mla_o.shape[1] * N_CHIPS
    MH = MVW // V_HEAD
    QL, KVL = w_q_b.shape[1], w_kv_b.shape[1]
    T_RET = _tile(S, 256)
    T_ATT = _tile(S, 512)

    xs = x.reshape(S, D)
    tgt = loss_target.reshape(S, D)
    cosr, sinr, pe_tabs = _rope_tables(positions, S)
    lgam = jnp.log(1.0 - 2.0 ** (-5.0 - jnp.arange(RH, dtype=F32)))
    lgam = jnp.broadcast_to(lgam[:, None, None], (RH, 8, LANES))

    big = ("w_in", "w_ret_o", "w_q_b", "w_kv_b", "w_mla_o", "w_out", "w_up", "w_down")
    w_sh = dict(w_in=w_in[0].T, w_ret_o=w_ret_o[0], w_q_b=w_q_b[0], w_kv_b=w_kv_b[0], w_mla_o=w_mla_o[0],
                w_out=w_out[0], w_up=w_up[0], w_down=w_down[0])
    m_sh = dict(w_in=m_w_in[0].T, w_ret_o=m_w_ret_o[0], w_q_b=m_w_q_b[0], w_kv_b=m_w_kv_b[0],
                w_mla_o=m_w_mla_o[0], w_out=m_w_out[0], w_up=m_w_up[0], w_down=m_w_down[0])
    v_sh = dict(w_in=v_w_in[0].T, w_ret_o=v_w_ret_o[0], w_q_b=v_w_q_b[0], w_kv_b=v_w_kv_b[0],
                w_mla_o=v_w_mla_o[0], w_out=v_w_out[0], w_up=v_w_up[0], w_down=v_w_down[0])
    col_sharded = ("w_q_b", "w_kv_b", "w_up")
    c_sh = w_in.shape[2]
    c_pad = -(-c_sh // 64) * 64
    place = jnp.stack([lax.axis_index("c"), 2 * lax.axis_index("x") + lax.axis_index("y")]).astype(jnp.int32)

    def whole(k, g):
        g = g.reshape(N_CHIPS, w_sh[k].shape[0], w_sh[k].shape[1])
        if k == "w_up":
            return g
        return _cat_cols(g) if k in col_sharded else g.reshape(-1, g.shape[2])

    first = ("w_in", "w_q_b", "w_kv_b")
    later = ("w_ret_o", "w_mla_o", "w_out", "w_up", "w_down")
    first_bufs = [_cast_into_slot(w_sh[k], place, name="cast_" + k, rows=c_pad if k == "w_in" else None)
                  for k in first]
    first_ssem, first_rsem, first_bufs, first_token = _split_start(
        first_bufs, _gather_ici_plan, 3 * len(first), name="gather_first_start")
    later_bufs = [_cast_into_slot(w_sh[k], place, name="cast_" + k, deps=(first_token,)) for k in later[:-1]]
    first_bufs = _split_wait(first_ssem, first_rsem, first_bufs, later_bufs[-1], _gather_ici_plan,
                             name="gather_first_wait")
    got = _forward_halves(first_bufs, name="gather_first_forward")
    full = {k: whole(k, g) for k, g in zip(first[1:], got[1:])}
    later_bufs.append(_cast_into_slot(w_sh[later[-1]], place, name="cast_" + later[-1], deps=(got[0],)))
    later_ssem, later_rsem, later_bufs, later_token = _split_start(
        later_bufs, _gather_ici_plan, 3 * len(later), name="gather_later_start")

    o_rq, o_rk, o_rv, o_rg = 0, RQW, 2 * RQW, 2 * RQW + RVW
    o_cq = 2 * RQW + 2 * RVW
    o_ckv, o_kpe = o_cq + QL, o_cq + QL + KVL
    o_gr = o_kpe + QK_ROPE
    o_gm = o_gr + D
    n_ret = RH * RET_HEAD_COLS
    off_gret, off_gmla, off_cq, off_ckv = n_ret, n_ret + D, n_ret + 2 * D, n_ret + 2 * D + QL
    n_a = off_ckv + KVL
    runs = []
    for h in range(RH):
        base = h * RET_HEAD_COLS
        runs += [(o_rq + h * RET_QK, RET_QK, base), (o_rk + h * RET_QK, RET_QK, base + RET_QK),
                 (o_rv + h * RET_V, RET_V, base + 2 * RET_QK), (o_rg + h * RET_V, RET_V, base + 2 * RET_QK + RET_V)]
    runs += [(o_gr, D, off_gret), (o_gm, D, off_gmla), (o_cq, QL, off_cq), (o_ckv, KVL, off_ckv),
             (o_kpe, QK_ROPE, n_a)]

    def take(parts, start, width):
        out, lo = [], 0
        for p in parts:
            hi = lo + p.shape[0]
            a, b = max(start, lo), min(start + width, hi)
            if a < b:
                out.append(p[a - lo:b - lo])
            lo = hi
        return out

    wi = [got[0].reshape(N_CHIPS, c_pad, D)[jj, :c_sh] for jj in range(N_CHIPS)]
    here = sorted(runs, key=lambda r: r[2])
    wa = jnp.concatenate([p for s0, w, _ in here[:-1] for p in take(wi, s0, w)], axis=0)
    wkpe = jnp.concatenate(take(wi, o_kpe, QK_ROPE) + [jnp.zeros((LANES - QK_ROPE, D), BF16)], axis=0)
    wq = jnp.pad(full["w_q_b"].reshape(QL, MH, QK_NOPE + QK_ROPE),
                 ((0, 0), (0, 0), (0, LANES - QK_ROPE))).reshape(QL, MH * 2 * LANES)
    wkv = full["w_kv_b"]

    u, rstd0 = _rmsnorm_fwd(xs, norm_mix_g, name="norm_mix")
    proj = _mm(u, wa, mode="nt", outs=[F32], name="in_proj", deps=(later_token,))
    kpe = _mm(u, wkpe, mode="nt", outs=[F32], name="kpe_proj")
    ry, gated, states = _ret_fwd(proj, cosr, sinr, lgam, ret_norm_g, RH, T=T_RET)
    cqn, rstd_q = _rmsnorm_fwd(proj, q_a_norm_g, name="norm_q", width=QL, col=off_cq // QL)
    ckvn, rstd_kv = _rmsnorm_fwd(proj, kv_a_norm_g, name="norm_kv", width=KVL, col=off_ckv // KVL)
    qf, kf, vb = _qkv_proj(cqn, ckvn, wq, wkv, kpe, pe_tabs, MH)
    first_half = _attn_fwd(qf, kf, vb, MH, T=T_ATT, heads=(0, MH // 2), name="attn_fwd_a")
    later_bufs = _split_wait(later_ssem, later_rsem, later_bufs, first_half[0], _gather_ici_plan,
                             name="gather_later_wait")
    fwd_ssem, fwd_rsem, later_bufs, fwd_token = _split_start(
        later_bufs, _forward_plan, 3 * len(later), name="gather_later_forward_start")
    my, my_b, lse2 = _attn_fwd(qf, kf, vb, MH, T=T_ATT, heads=(MH // 2, MH), name="attn_fwd_b",
                               prev=first_half, deps=(fwd_token,))
    later_bufs = _split_wait(fwd_ssem, fwd_rsem, later_bufs, my, _forward_plan, name="gather_later_forward_wait")
    full.update({k: whole(k, g) for k, g in zip(later, later_bufs)})
    y_ret = _mm(gated, full["w_ret_o"], mode="nn", outs=[BF16], name="ret_o")
    y_mla, merged = _mm(my_b, full["w_mla_o"], mode="nn", outs=[BF16, BF16], name="mla_o",
                        epi=lambda acc, gr, gm, yr: (acc, _sigmoid(gr) * yr + _sigmoid(gm) * acc),
                        extras=((proj, off_gret), (proj, off_gmla), y_ret))
    h1 = _mm(merged, full["w_out"], mode="nn", outs=[F32], name="out_proj",
             epi=lambda acc, r: (acc + r,), extras=(xs,))
    n1, rstd1 = _rmsnorm_fwd(h1, norm_mlp_g, name="norm_mlp")

    def up_epi(acc):
        r = jnp.maximum(acc, 0.0)
        return acc, r * r

    z, act = _mm(n1, full["w_up"], mode="nn", outs=[F32, BF16], name="up_proj", epi=up_epi)
    h2 = _mm(act, full["w_down"], mode="nn", outs=[F32], name="down_proj",
             epi=lambda acc, r: (acc + r,), extras=(h1,))
    loss11, dh2, dh2_b, g_norm_f = _final_loss(h2, norm_f_g.reshape(1, D), tgt)

    dz = _mm(dh2_b, full["w_down"], mode="nt", outs=[BF16], name="down_bwd_x",
             epi=lambda acc, zz: (acc * (2.0 * jnp.maximum(zz, 0.0)),), extras=(z,))
    g_w_down = _mm(act, dh2_b, mode="tn", outs=[BF16], name="down_bwd_w")
    dn1 = _mm(dz, full["w_up"], mode="nt", outs=[F32], name="up_bwd_x")
    g_w_up = _mm(n1, dz, mode="tn", outs=[BF16], name="up_bwd_w", out_shards=True)

    def scatter_begin(tag, sums):
        lands = [lax.empty((3,) + s.shape[1:], s.dtype) for s in sums]
        return _split_start(sums + lands, _scatter_plan(len(sums)), 3 * len(sums), name="scatter_" + tag + "_start")

    def swap_begin(tag, grads):
        views = [g if g.ndim == 3 else g.reshape(N_CHIPS, g.shape[0] // N_CHIPS, g.shape[1]) for g in grads]
        views = [v.reshape(N_CHIPS, 2, v.shape[1] // 2, v.shape[2]) for v in views]
        lands = [lax.empty((N_CHIPS,) + v.shape[2:], v.dtype) for v in views]
        return _split_start(views + lands, _swap_plan(len(views)), len(views), name="swap_" + tag + "_start")

    def swap_end(tag, names, handle, after):
        n = len(names)
        bufs = _split_wait(handle[0], handle[1], handle[2], after, _swap_plan(n), name="swap_" + tag + "_wait")
        pcs = [b.reshape(N_CHIPS, 2 * b.shape[2], b.shape[3]) for b in bufs[:n]]
        sums = [_sum_pair(p, t, place, name="sum_pair_" + k) for k, p, t in zip(names, pcs, bufs[n:])]
        return pcs, bufs[n:], sums

    g1 = ("w_up", "w_down")
    swap1 = swap_begin("g1", (g_w_up, g_w_down))
    dh1, g_norm_mlp, dh1_b = _rmsnorm_bwd(dn1, h1, rstd1, norm_mlp_g, name="norm_mlp_bwd", res=dh2,
                                          deps=(swap1[3],), bf16_copy=1)
    dmerged = _mm(dh1_b, full["w_out"], mode="nt", outs=[F32], name="out_bwd_x")
    pcs1, theirs1, sums1 = swap_end("g1", g1, swap1, dmerged)
    ssem1, rsem1, bufs1, token1 = scatter_begin("g1", sums1)
    g_w_out = _mm(merged, dh1_b, mode="tn", outs=[BF16], name="out_bwd_w", deps=(token1,))
    dproj, dy_ret, dy_mla = _merge_bwd(dmerged, proj, y_ret, y_mla, D, off_gret)
    dgated = _mm(dy_ret, full["w_ret_o"], mode="nt", outs=[F32], name="ret_o_bwd_x")
    g_w_ret_o = _mm(gated, dy_ret, mode="tn", outs=[BF16], name="ret_o_bwd_w")
    dproj, g_ret_norm = _ret_bwd(proj, cosr, sinr, lgam, ret_norm_g, ry, dgated, states, dproj, RH, T=T_RET)
    def delta_epi(acc, o):
        rows = acc.shape[0]
        return acc, [jnp.broadcast_to(jnp.sum(acc[:, lo:lo + V_HEAD] * o[:, lo:lo + V_HEAD], axis=-1, keepdims=True),
                                      (rows, LANES)) for lo in range(0, acc.shape[1], V_HEAD)]

    dob, delta = _mm(dy_mla, full["w_mla_o"], mode="nt", outs=[BF16], name="mla_o_bwd_x", epi=delta_epi,
                     extras=(my,), more_outs=lambda tm, tn: [
                         (jax.ShapeDtypeStruct((MH, S, LANES), F32),
                          pl.BlockSpec((tn // V_HEAD, tm, LANES), lambda i, j, k: (j, i, 0)))])
    g_w_mla_o = _mm(my_b, dy_mla, mode="tn", outs=[BF16], name="mla_o_bwd_w")
    g2 = ("w_out", "w_ret_o", "w_mla_o")
    swap2 = swap_begin("g2", (g_w_out, g_w_ret_o, g_w_mla_o))
    dq_all, dkv_all, dkpe_h = _attn_bwd(qf, kf, vb, dob, lse2, delta, pe_tabs, MH, T=T_ATT, deps=(swap2[3],))
    pcs2, theirs2, sums2 = swap_end("g2", g2, swap2, dkv_all)
    ssem2, rsem2, bufs2, token2 = scatter_begin("g2", sums2)
    dkpe = _kpe_sum(dkpe_h, pe_tabs, MH)
    dcqn = _mm(dq_all, wq, mode="nt", outs=[F32], name="q_bwd_x", deps=(token2,))
    g_wq = _mm(cqn, dq_all, mode="tn", outs=[BF16], name="q_bwd_w")
    dckvn = _mm(dkv_all, wkv, mode="nt", outs=[F32], name="kv_bwd_x")
    g_wkv = _mm(ckvn, dkv_all, mode="tn", outs=[BF16], name="kv_bwd_w")
    dproj, g_q_a = _rmsnorm_bwd(dcqn, proj, rstd_q, q_a_norm_g, name="norm_q_bwd", into=(dproj, off_cq // QL),
                                width=QL, col=off_cq // QL)
    dproj, g_kv_a = _rmsnorm_bwd(dckvn, proj, rstd_kv, kv_a_norm_g, name="norm_kv_bwd", into=(dproj, off_ckv // KVL),
                                 width=KVL, col=off_ckv // KVL)
    g_wa = _mm(dproj, u, mode="tn", outs=[BF16], name="in_bwd_w")
    g_wkpe = _mm(dkpe, u, mode="tn", outs=[BF16], name="kpe_bwd_w")

    there = sorted(runs)
    g_parts = [g_wa, g_wkpe]
    g_w_in = jnp.stack([jnp.concatenate(
        [p for s0, w, d0 in there for a, b in [(max(s0, jj * c_sh), min(s0 + w, (jj + 1) * c_sh))] if a < b
         for p in take(g_parts, d0 + a - s0, b - a)] + [jnp.zeros((c_pad - c_sh, D), BF16)], axis=0)
        for jj in range(N_CHIPS)])
    gq = g_wq.reshape(QL, MH, 2 * LANES)[:, :, :QK_NOPE + QK_ROPE].reshape(QL, MH * (QK_NOPE + QK_ROPE))
    g3 = ("w_in", "w_q_b", "w_kv_b")
    swap3 = swap_begin("g3", (g_w_in, _split_cols(gq), _split_cols(g_wkv)))

    def chip_sums(names, pcs, theirs, recv):
        return [_sum_chips(p, t, r, place, name="sum_chips_" + k) for k, p, t, r in zip(names, pcs, theirs, recv)]

    bufs1 = _split_wait(ssem1, rsem1, bufs1, swap3[3], _scatter_plan(len(g1)), name="scatter_g1_wait")
    bufs2 = _split_wait(ssem2, rsem2, bufs2, swap3[3], _scatter_plan(len(g2)), name="scatter_g2_wait")
    halves12 = chip_sums(g1, pcs1, theirs1, bufs1[len(g1):]) + chip_sums(g2, pcs2, theirs2, bufs2[len(g2):])
    jssem, jrsem, halves12, join_token = _split_start(halves12, _join_plan, len(halves12), name="join_g12_start")
    pcs3, theirs3, sums3 = swap_end("g3", g3, swap3, join_token)
    ssem3, rsem3, bufs3, token3 = scatter_begin("g3", sums3)
    du = _mm(dproj, wa, mode="nn", outs=[F32], name="in_bwd_x", tk=2816, tail=(dkpe, wkpe), deps=(token3,))
    dx, g_norm_mix = _rmsnorm_bwd(du, xs, rstd0, norm_mix_g, name="norm_mix_bwd", res=dh1)

    bufs3 = _split_wait(ssem3, rsem3, bufs3, dx, _scatter_plan(len(g3)), name="scatter_g3_wait")
    halves12 = _split_wait(jssem, jrsem, halves12, dx, _join_plan, name="join_g12_wait")
    j3ssem, j3rsem, halves3, join3_token = _split_start(
        chip_sums(g3, pcs3, theirs3, bufs3[len(g3):]), _join_plan, len(g3), name="join_g3_start")

    small = ("norm_mix_g", "ret_norm_g", "q_a_norm_g", "kv_a_norm_g", "norm_mlp_g", "norm_f_g")
    g_small = [g_norm_mix, g_ret_norm, g_q_a, g_kv_a, g_norm_mlp, g_norm_f]
    red = _allreduce_small(g_small, loss11)
    loss = red[0, red.shape[1] - 1]
    w_small = [norm_mix_g, ret_norm_g, q_a_norm_g, kv_a_norm_g, norm_mlp_g, norm_f_g]
    m_small = [m_norm_mix_g, m_ret_norm_g, m_q_a_norm_g, m_kv_a_norm_g, m_norm_mlp_g, m_norm_f_g]
    v_small = [v_norm_mix_g, v_ret_norm_g, v_q_a_norm_g, v_kv_a_norm_g, v_norm_mlp_g, v_norm_f_g]
    row = lambda a: a.reshape(1, -1)
    upd = _adamw_small(red, [row(a) for a in w_small], [row(a) for a in m_small], [row(a) for a in v_small])
    out_g, out_d, out_m, out_v = {}, {}, {}, {}
    for k, wv, (g_, d_, m_, v_) in zip(small, w_small, upd):
        out_g[k], out_d[k], out_m[k], out_v[k] = [a.reshape(wv.shape) for a in (g_, d_, m_, v_)]

    def adamw_shard(k, joined, deps=()):
        g = joined.reshape(2 * joined.shape[1], joined.shape[2])
        res = _rows_call(lambda w, g, m, v: (g,) + _adamw_vals(w, g, m, v),
                         [w_sh[k], g, m_sh[k], v_sh[k]], [F32] * 4, name="adamw_" + k, deps=deps)
        if k == "w_in":
            res = [r.T for r in res]
        out_g[k], out_d[k], out_m[k], out_v[k] = [r[None] for r in res]
        return res[0]

    for k, joined in zip(g1 + g2, halves12):
        last = adamw_shard(k, joined, deps=(join3_token,))
    halves3 = _split_wait(j3ssem, j3rsem, halves3, last, _join_plan, name="join_g3_wait")
    for k, joined in zip(g3, halves3):
        adamw_shard(k, joined)

    order = ("norm_mix_g", "w_in", "ret_norm_g", "w_ret_o", "q_a_norm_g", "w_q_b", "kv_a_norm_g", "w_kv_b",
             "w_mla_o", "w_out", "norm_mlp_g", "w_up", "w_down", "norm_f_g")
    return (loss, dx.reshape(1, S, D), *[out_g[k] for k in order], *[out_d[k] for k in order],
            *[out_m[k] for k in order], *[out_v[k] for k in order])
```

```python
import math

import jax
import jax.numpy as jnp
from jax import lax
from jax.experimental import pallas as pl
from jax.experimental.pallas import tpu as pltpu

F32 = jnp.float32
BF16 = jnp.bfloat16

EPS = 1e-6
ROPE_THETA = 10000.0
CHUNK = 64
RET_QK = 128
RET_V = 256
RET_HEAD_COLS = 2 * RET_QK + 2 * RET_V
QK_NOPE = 128
QK_ROPE = 64
V_HEAD = 128
LANES = 128
LOG2E = math.log2(math.e)

ADAM_LR = 0.001
ADAM_B1 = 0.9
ADAM_B2 = 0.999
ADAM_EPS = 1e-08
ADAM_WD = 0.01
ADAM_STEP = 10

N_CHIPS = 4
VMEM_LIMIT = 56 * 1024 * 1024
MESH = pl.DeviceIdType.MESH
NEG = -1e30


def _pallas(body, **kw):
    return pl.pallas_call(body, **kw)


def _params(sem=None):
    return pltpu.CompilerParams(dimension_semantics=sem, vmem_limit_bytes=VMEM_LIMIT)


def _tile(n, want):
    t = min(n, want)
    while n % t:
        t //= 2
    return t


_ANY = pl.BlockSpec(memory_space=pl.ANY)
TN_BF16_TK = 4096


def _mm(a, b, *, mode, outs, name, epi=None, extras=(), deps=(), out_shards=False, more_outs=None, tail=None,
        tm=1024, tn=1024, tk=2048):
    shards = b.shape[0] if b.ndim == 3 else 1
    brows, bcols = b.shape[-2], b.shape[-1] * shards
    if mode == "nn":
        (M, K), N = a.shape, bcols
    elif mode == "nt":
        (M, K), N = a.shape, brows
    else:
        (K, M), N = a.shape, bcols
    if mode == "tn" and a.dtype == BF16 and b.dtype == BF16:
        tk = max(tk, TN_BF16_TK)
    tm = _tile(M, tm)
    tn = _tile(N // (shards if mode == "nn" else 1) // (N_CHIPS if out_shards else 1), tn)
    tk = _tile(K // (shards if mode == "nt" else 1), tk)
    nk = K // tk
    if mode == "nn":
        a_spec = pl.BlockSpec((tm, tk), lambda i, j, k: (i, k))
        dims = (((1,), (0,)), ((), ()))
        if shards > 1:
            per = N // shards // tn
            b_spec = pl.BlockSpec((None, tk, tn), lambda i, j, k: (j // per, k, j % per))
        else:
            b_spec = pl.BlockSpec((tk, tn), lambda i, j, k: (k, j))
    elif mode == "nt":
        a_spec = pl.BlockSpec((tm, tk), lambda i, j, k: (i, k))
        dims = (((1,), (1,)), ((), ()))
        if shards > 1:
            per = K // shards // tk
            b_spec = pl.BlockSpec((None, tn, tk), lambda i, j, k: (k // per, j, k % per))
        else:
            b_spec = pl.BlockSpec((tn, tk), lambda i, j, k: (j, k))
    else:
        assert shards == 1
        a_spec = pl.BlockSpec((tk, tm), lambda i, j, k: (k, i))
        b_spec = pl.BlockSpec((tk, tn), lambda i, j, k: (k, j))
        dims = (((0,), (0,)), ((), ()))
    if out_shards:
        assert not extras
        oper = N // N_CHIPS // tn
        o_spec = pl.BlockSpec((None, tm, tn), lambda i, j, k: (j // oper, i, j % oper))
        o_shape = (N_CHIPS, M, N // N_CHIPS)
    else:
        o_spec = pl.BlockSpec((tm, tn), lambda i, j, k: (i, j))
        o_shape = (M, N)
    more = [] if more_outs is None else more_outs(tm, tn)
    ex_arrays = [e[0] if isinstance(e, tuple) else e for e in extras]
    ex_specs = [pl.BlockSpec((tm, tn), lambda i, j, k, off=e[1] // tn: (i, off + j)) if isinstance(e, tuple)
                else o_spec for e in extras]
    n_ex, n_out, n_dep = len(extras), len(outs) + len(more), len(deps)
    if epi is None:
        epi = lambda acc: (acc,)
    tails, tail_specs = [], []
    if tail is not None:
        assert mode == "nn"
        tails = list(tail)
        k2 = tail[0].shape[1]
        tail_specs = [pl.BlockSpec((tm, k2), lambda i, j, k: (i, 0)), pl.BlockSpec((k2, tn), lambda i, j, k: (0, j))]
    n_tail = len(tails)

    def body(*refs):
        a_ref, b_ref = refs[0], refs[1]
        ex_refs = refs[2:2 + n_ex]
        t_refs = refs[2 + n_ex:2 + n_ex + n_tail]
        first_out = 2 + n_ex + n_tail + n_dep
        o_refs = refs[first_out:first_out + n_out]
        part = lax.dot_general(a_ref[...].astype(BF16), b_ref[...].astype(BF16), dims,
                               preferred_element_type=F32)

        def finish(acc):
            if n_tail:
                acc = acc + lax.dot_general(t_refs[0][...].astype(BF16), t_refs[1][...].astype(BF16), dims,
                                            preferred_element_type=F32)
            vals = epi(acc, *[r[...] for r in ex_refs])
            for r, v in zip(o_refs, vals):
                if isinstance(v, (list, tuple)):
                    for lead, piece in enumerate(v):
                        r[lead] = piece.astype(r.dtype)
                else:
                    r[...] = v.astype(r.dtype)

        if nk == 1:
            finish(part)
        else:
            acc_ref = refs[-1]
            k = pl.program_id(2)

            @pl.when(k == 0)
            def _():
                acc_ref[...] = part

            @pl.when(k > 0)
            def _():
                acc_ref[...] += part

            @pl.when(k == nk - 1)
            def _():
                finish(acc_ref[...])

    res = _pallas(
        body, name=name, grid=(M // tm, N // tn, nk),
        in_specs=[a_spec, b_spec] + ex_specs + tail_specs + [_ANY] * n_dep,
        out_specs=[o_spec] * len(outs) + [spec for _, spec in more],
        out_shape=[jax.ShapeDtypeStruct(o_shape, d) for d in outs] + [shape for shape, _ in more],
        scratch_shapes=[pltpu.VMEM((tm, tn), F32)] if nk > 1 else [],
        compiler_params=_params(("parallel", "parallel", "arbitrary")),
    )(a, b, *ex_arrays, *tails, *deps)
    return res[0] if n_out == 1 else res


def _rmsnorm_fwd(x, g, *, name, tr=256):
    S, W = x.shape
    tr = _tile(S, tr)

    def body(x_ref, g_ref, y_ref, r_ref):
        xv = x_ref[...]
        rstd = lax.rsqrt(jnp.mean(xv * xv, axis=-1, keepdims=True) + EPS)
        y_ref[...] = (xv * rstd * g_ref[...]).astype(BF16)
        r_ref[...] = rstd

    return _pallas(
        body, name=name, grid=(S // tr,),
        in_specs=[pl.BlockSpec((tr, W), lambda i: (i, 0)), pl.BlockSpec((1, W), lambda i: (0, 0))],
        out_specs=[pl.BlockSpec((tr, W), lambda i: (i, 0)), pl.BlockSpec((tr, 1), lambda i: (i, 0))],
        out_shape=[jax.ShapeDtypeStruct((S, W), BF16), jax.ShapeDtypeStruct((S, 1), F32)],
        compiler_params=_params(("parallel",)),
    )(x, g)


def _rmsnorm_bwd(dy, x, rstd, g, *, name, res=None, deps=(), bf16_copy=0, tr=256):
    S, W = x.shape
    tr = _tile(S, tr)
    has_res = res is not None

    def body(*refs):
        dy_ref, x_ref, r_ref, g_ref = refs[:4]
        dx_ref, dg_ref = refs[-2 - bf16_copy], refs[-1 - bf16_copy]
        rstd_v = r_ref[...]
        xhat = x_ref[...] * rstd_v
        dyv = dy_ref[...].astype(F32)
        dyg = dyv * g_ref[...]
        dx = rstd_v * (dyg - xhat * jnp.mean(dyg * xhat, axis=-1, keepdims=True))
        if has_res:
            dx = dx + refs[4][...]
        dx_ref[...] = dx.astype(dx_ref.dtype)
        if bf16_copy:
            refs[-1][...] = dx.astype(BF16)
        part = jnp.sum(dyv * xhat, axis=0, keepdims=True)

        @pl.when(pl.program_id(0) == 0)
        def _():
            dg_ref[...] = part

        @pl.when(pl.program_id(0) > 0)
        def _():
            dg_ref[...] += part

    row = pl.BlockSpec((tr, W), lambda i: (i, 0))
    ins = [dy, x, rstd, g] + ([res] if has_res else [])
    in_specs = [row, row, pl.BlockSpec((tr, 1), lambda i: (i, 0)),
                pl.BlockSpec((1, W), lambda i: (0, 0))] + ([row] if has_res else [])
    ins += list(deps)
    in_specs += [_ANY] * len(deps)
    return _pallas(
        body, name=name, grid=(S // tr,), in_specs=in_specs,
        out_specs=[row, pl.BlockSpec((1, W), lambda i: (0, 0))] + [row] * bf16_copy,
        out_shape=[jax.ShapeDtypeStruct((S, W), F32), jax.ShapeDtypeStruct((1, W), F32)]
        + [jax.ShapeDtypeStruct((S, W), BF16)] * bf16_copy,
        compiler_params=_params(("arbitrary",)),
    )(*ins)


def _norm_pair_fwd(proj, g_a, g_b, off, *, tr=512):
    S = proj.shape[0]
    wa_, wb_ = g_a.shape[1], g_b.shape[1]
    W = wa_ + wb_
    tr = _tile(S, tr)

    def body(x_ref, ga_ref, gb_ref, ya_ref, yb_ref, ra_ref, rb_ref):
        for lo, hi, g_ref, y_ref, r_ref in ((0, wa_, ga_ref, ya_ref, ra_ref), (wa_, W, gb_ref, yb_ref, rb_ref)):
            xv = x_ref[:, lo:hi]
            rstd = lax.rsqrt(jnp.mean(xv * xv, axis=-1, keepdims=True) + EPS)
            y_ref[...] = (xv * rstd * g_ref[...]).astype(BF16)
            r_ref[...] = rstd

    one = pl.BlockSpec((tr, 1), lambda i: (i, 0))
    return _pallas(
        body, name="norm_qkv", grid=(S // tr,),
        in_specs=[pl.BlockSpec((tr, W), lambda i: (i, off // W)), pl.BlockSpec((1, wa_), lambda i: (0, 0)),
                  pl.BlockSpec((1, wb_), lambda i: (0, 0))],
        out_specs=[pl.BlockSpec((tr, wa_), lambda i: (i, 0)), pl.BlockSpec((tr, wb_), lambda i: (i, 0)), one, one],
        out_shape=[jax.ShapeDtypeStruct((S, wa_), BF16), jax.ShapeDtypeStruct((S, wb_), BF16),
                   jax.ShapeDtypeStruct((S, 1), F32), jax.ShapeDtypeStruct((S, 1), F32)],
        compiler_params=_params(("parallel",)),
    )(proj, g_a, g_b)


def _norm_pair_bwd(dy_a, dy_b, proj, r_a, r_b, g_a, g_b, dproj, off, *, tr=512):
    S = proj.shape[0]
    wa_, wb_ = g_a.shape[1], g_b.shape[1]
    W = wa_ + wb_
    tr = _tile(S, tr)

    def body(dya_ref, dyb_ref, x_ref, ra_ref, rb_ref, ga_ref, gb_ref, _, dx_ref, dga_ref, dgb_ref):
        first = pl.program_id(0) == 0
        for lo, hi, dy_ref, r_ref, g_ref, dg_ref in ((0, wa_, dya_ref, ra_ref, ga_ref, dga_ref),
                                                    (wa_, W, dyb_ref, rb_ref, gb_ref, dgb_ref)):
            rstd = r_ref[...]
            xhat = x_ref[:, lo:hi] * rstd
            dyv = dy_ref[...]
            dyg = dyv * g_ref[...]
            dx_ref[:, lo:hi] = (rstd * (dyg - xhat * jnp.mean(dyg * xhat, axis=-1, keepdims=True))).astype(dx_ref.dtype)
            part = jnp.sum(dyv * xhat, axis=0, keepdims=True)

            @pl.when(first)
            def _():
                dg_ref[...] = part

            @pl.when(jnp.logical_not(first))
            def _():
                dg_ref[...] += part

    one = pl.BlockSpec((tr, 1), lambda i: (i, 0))
    cols = pl.BlockSpec((tr, W), lambda i: (i, off // W))
    va, vb = pl.BlockSpec((1, wa_), lambda i: (0, 0)), pl.BlockSpec((1, wb_), lambda i: (0, 0))
    return _pallas(
        body, name="norm_qkv_bwd", grid=(S // tr,),
        in_specs=[pl.BlockSpec((tr, wa_), lambda i: (i, 0)), pl.BlockSpec((tr, wb_), lambda i: (i, 0)), cols,
                  one, one, va, vb, _ANY],
        out_specs=[cols, va, vb],
        out_shape=[jax.ShapeDtypeStruct(dproj.shape, dproj.dtype), jax.ShapeDtypeStruct((1, wa_), F32),
                   jax.ShapeDtypeStruct((1, wb_), F32)],
        input_output_aliases={7: 0},
        compiler_params=_params(("arbitrary",)),
    )(dy_a, dy_b, proj, r_a, r_b, g_a, g_b, dproj)


def _final_loss(h2, g, target, *, tr=256):
    S, D = h2.shape
    tr = _tile(S, tr)

    def body(h_ref, g_ref, t_ref, loss_ref, dh_ref, dhb_ref, dg_ref):
        hv = h_ref[...]
        rstd = lax.rsqrt(jnp.mean(hv * hv, axis=-1, keepdims=True) + EPS)
        xhat = hv * rstd
        e = xhat * g_ref[...] - t_ref[...]
        lpart = (0.5 / D) * jnp.sum(jnp.sum(e * e, axis=-1, keepdims=True), axis=0, keepdims=True)
        dy = e * (1.0 / D)
        dyg = dy * g_ref[...]
        dh = rstd * (dyg - xhat * jnp.mean(dyg * xhat, axis=-1, keepdims=True))
        dh_ref[...] = dh
        dhb_ref[...] = dh.astype(BF16)
        gpart = jnp.sum(dy * xhat, axis=0, keepdims=True)

        @pl.when(pl.program_id(0) == 0)
        def _():
            loss_ref[...] = lpart
            dg_ref[...] = gpart

        @pl.when(pl.program_id(0) > 0)
        def _():
            loss_ref[...] += lpart
            dg_ref[...] += gpart

    row = pl.BlockSpec((tr, D), lambda i: (i, 0))
    vec = pl.BlockSpec((1, D), lambda i: (0, 0))
    return _pallas(
        body, name="final_loss", grid=(S // tr,), in_specs=[row, vec, row],
        out_specs=[pl.BlockSpec((1, 1), lambda i: (0, 0)), row, row, vec],
        out_shape=[jax.ShapeDtypeStruct((1, 1), F32), jax.ShapeDtypeStruct((S, D), F32),
                   jax.ShapeDtypeStruct((S, D), BF16), jax.ShapeDtypeStruct((1, D), F32)],
        compiler_params=_params(("arbitrary",)),
    )(h2, g, target)


def _sigmoid(v):
    return 1.0 / (1.0 + jnp.exp(-v))


def _merge_bwd(dmerged, proj, y_ret, y_mla, D, off_gret, *, tr=256):
    S = y_ret.shape[0]
    tr = _tile(S, tr)
    b0 = off_gret // D

    def body(dm_ref, g_ref, yr_ref, ym_ref, dp_ref, dyr_ref, dym_ref):
        dm = dm_ref[...]
        sg = _sigmoid(g_ref[...])

        @pl.when(pl.program_id(1) == 0)
        def _():
            dyr_ref[...] = (dm * sg).astype(BF16)
            dp_ref[...] = (dm * yr_ref[...] * sg * (1.0 - sg)).astype(BF16)

        @pl.when(pl.program_id(1) == 1)
        def _():
            dym_ref[...] = (dm * sg).astype(BF16)
            dp_ref[...] = (dm * ym_ref[...] * sg * (1.0 - sg)).astype(BF16)

    blk = pl.BlockSpec((tr, D), lambda i, j: (i, 0))
    return _pallas(
        body, name="merge_bwd", grid=(S // tr, 2),
        in_specs=[blk, pl.BlockSpec((tr, D), lambda i, j: (i, b0 + j)), blk, blk],
        out_specs=[pl.BlockSpec((tr, D), lambda i, j: (i, b0 + j)), blk, blk],
        out_shape=[jax.ShapeDtypeStruct(proj.shape, BF16), jax.ShapeDtypeStruct((S, D), BF16),
                   jax.ShapeDtypeStruct((S, D), BF16)],
        compiler_params=_params(("parallel", "arbitrary")),
    )(dmerged, proj, y_ret, y_mla)


def _rope128(t, cos_full, sin_signed):
    return t * cos_full + pltpu.roll(t, RET_QK // 2, 1) * sin_signed


def _rope128_t(d, cos_full, sin_signed):
    return d * cos_full + pltpu.roll(d * sin_signed, RET_QK // 2, 1)


def _ret_consts(lg, T):
    pos = lax.broadcasted_iota(jnp.int32, (T, 1), 0).astype(F32)
    qd = jnp.exp(lg * (pos + 1.0))
    kd = jnp.exp(lg * (T - 1.0 - pos))
    n = lax.broadcasted_iota(jnp.int32, (T, T), 0)
    m = lax.broadcasted_iota(jnp.int32, (T, T), 1)
    vis = (m // CHUNK) <= (n // CHUNK)
    dist = jnp.abs(n - m).astype(F32)
    decay = jnp.where(vis, jnp.exp(lg * dist), 0.0)
    cdec = jnp.exp(lg * float(T))
    return qd, kd, decay, cdec


def _dot(a, b, dims):
    return lax.dot_general(a.astype(BF16), b.astype(BF16), (dims, ((), ())), preferred_element_type=F32)


NN = ((1,), (0,))
NT = ((1,), (1,))
TN = ((0,), (0,))
_RQ = slice(0, RET_QK)
_RK = slice(RET_QK, 2 * RET_QK)
_RV = slice(2 * RET_QK, 2 * RET_QK + RET_V)
_RG = slice(2 * RET_QK + RET_V, RET_HEAD_COLS)


RET_GROUP = 8


def _head_cols(h, part):
    return slice(h * RET_HEAD_COLS + part.start, h * RET_HEAD_COLS + part.stop)


def _ret_fwd(proj, cosr, sinr, lgam, gain, RH, *, T):
    S = proj.shape[0]
    nb = S // T
    G = _tile(RH, RET_GROUP)
    heads = range(G)
    scale = RET_QK ** -0.5

    def body(p_ref, cos_ref, sin_ref, lg_ref, gain_ref, ry_ref, gated_ref, st_ref, state):
        b = pl.program_id(1)

        @pl.when(b == 0)
        def _():
            state[...] = jnp.zeros_like(state)

        consts = [_ret_consts(lg_ref[h, 0:1, 0:1], T) for h in heads]
        cosv, sinv = cos_ref[...], sin_ref[...]
        q = [_rope128(p_ref[:, _head_cols(h, _RQ)], cosv, sinv) for h in heads]
        k = [_rope128(p_ref[:, _head_cols(h, _RK)], cosv, sinv) * scale for h in heads]
        v = [p_ref[:, _head_cols(h, _RV)] for h in heads]
        sprev = [state[h] for h in heads]
        for h in heads:
            st_ref[h] = sprev[h]
        a = [_dot(q[h], k[h], NT) for h in heads]
        qs = [_dot(q[h] * consts[h][0], sprev[h], NN) for h in heads]
        kv = [_dot(k[h] * consts[h][1], v[h], TN) for h in heads]
        o = [_dot(a[h] * consts[h][2], v[h], NN) + qs[h] for h in heads]
        for h in heads:
            state[h] = sprev[h] * consts[h][3] + kv[h]
            vals = slice(h * RET_V, (h + 1) * RET_V)
            ry_ref[:, vals] = o[h]
            mu = jnp.mean(o[h], axis=-1, keepdims=True)
            oc = o[h] - mu
            var = jnp.mean(oc * oc, axis=-1, keepdims=True)
            t = oc * lax.rsqrt(var + EPS) * gain_ref[:, vals]
            gv = p_ref[:, _head_cols(h, _RG)]
            gated_ref[:, vals] = (t * (gv * _sigmoid(gv))).astype(BF16)

    return _pallas(
        body, name="ret_fwd", grid=(RH // G, nb),
        in_specs=[pl.BlockSpec((T, G * RET_HEAD_COLS), lambda h, b: (b, h)),
                  pl.BlockSpec((T, RET_QK), lambda h, b: (b, 0)),
                  pl.BlockSpec((T, RET_QK), lambda h, b: (b, 0)),
                  pl.BlockSpec((G, 8, LANES), lambda h, b: (h, 0, 0)),
                  pl.BlockSpec((1, G * RET_V), lambda h, b: (0, h))],
        out_specs=[pl.BlockSpec((T, G * RET_V), lambda h, b: (b, h)),
                   pl.BlockSpec((T, G * RET_V), lambda h, b: (b, h)),
                   pl.BlockSpec((G, None, RET_QK, RET_V), lambda h, b: (h, b, 0, 0))],
        out_shape=[jax.ShapeDtypeStruct((S, RH * RET_V), F32), jax.ShapeDtypeStruct((S, RH * RET_V), BF16),
                   jax.ShapeDtypeStruct((RH, nb, RET_QK, RET_V), F32)],
        scratch_shapes=[pltpu.VMEM((G, RET_QK, RET_V), F32)],
        compiler_params=_params(("parallel", "arbitrary")),
    )(proj, cosr, sinr, lgam, gain)


def _ret_bwd(proj, cosr, sinr, lgam, gain, ry, dgated, states, dproj, RH, *, T):
    S = proj.shape[0]
    nb = S // T
    G = _tile(RH, RET_GROUP)
    heads = range(G)
    scale = RET_QK ** -0.5

    def body(p_ref, cos_ref, sin_ref, lg_ref, gain_ref, ry_ref, dg_ref, st_ref, _, dp_ref, dgain_ref, dstate):
        b = pl.program_id(1)

        @pl.when(b == 0)
        def _():
            dstate[...] = jnp.zeros_like(dstate)

        consts = [_ret_consts(lg_ref[h, 0:1, 0:1], T) for h in heads]
        qd, kd, decay, cdec = [[c[i] for c in consts] for i in range(4)]
        cosv, sinv = cos_ref[...], sin_ref[...]
        q = [_rope128(p_ref[:, _head_cols(h, _RQ)], cosv, sinv) for h in heads]
        k = [_rope128(p_ref[:, _head_cols(h, _RK)], cosv, sinv) * scale for h in heads]
        v = [p_ref[:, _head_cols(h, _RV)] for h in heads]
        sprev = [st_ref[h] for h in heads]
        ds_new = [dstate[h] for h in heads]
        a = [_dot(q[h], k[h], NT) for h in heads]
        do, gparts = [], []
        for h in heads:
            vals = slice(h * RET_V, (h + 1) * RET_V)
            o = ry_ref[:, vals]
            mu = jnp.mean(o, axis=-1, keepdims=True)
            oc = o - mu
            rstd = lax.rsqrt(jnp.mean(oc * oc, axis=-1, keepdims=True) + EPS)
            ryn = oc * rstd
            gainv = gain_ref[:, vals]
            gv = p_ref[:, _head_cols(h, _RG)]
            sg = _sigmoid(gv)
            dgt = dg_ref[:, vals]
            dt = dgt * (gv * sg)
            dp_ref[:, _head_cols(h, _RG)] = (dgt * (ryn * gainv) * (sg * (1.0 + gv * (1.0 - sg)))).astype(BF16)
            gparts.append(jnp.sum(dt * ryn, axis=0, keepdims=True))
            dryn = dt * gainv
            do.append(rstd * (dryn - jnp.mean(dryn, axis=-1, keepdims=True)
                              - ryn * jnp.mean(dryn * ryn, axis=-1, keepdims=True)))
        gpart = jnp.concatenate(gparts, axis=1)

        @pl.when(b == 0)
        def _():
            dgain_ref[...] = gpart

        @pl.when(b > 0)
        def _():
            dgain_ref[...] += gpart

        dpm = [_dot(do[h], v[h], NT) for h in heads]
        dq_s = [_dot(do[h], sprev[h], NT) for h in heads]
        dk_s = [_dot(v[h], ds_new[h], NT) for h in heads]
        dv_s = [_dot(k[h] * kd[h], ds_new[h], NN) for h in heads]
        dst = [_dot(q[h] * qd[h], do[h], TN) for h in heads]
        a = [a[h] * decay[h] for h in heads]
        dpm = [dpm[h] * decay[h] for h in heads]
        dv = [_dot(a[h], do[h], TN) + dv_s[h] for h in heads]
        dq = [_dot(dpm[h], k[h], NN) + dq_s[h] * qd[h] for h in heads]
        dk = [(_dot(dpm[h], q[h], TN) + dk_s[h] * kd[h]) * scale for h in heads]
        for h in heads:
            dstate[h] = ds_new[h] * cdec[h] + dst[h]
            dp_ref[:, _head_cols(h, _RV)] = dv[h].astype(BF16)
            dp_ref[:, _head_cols(h, _RQ)] = _rope128_t(dq[h], cosv, sinv).astype(BF16)
            dp_ref[:, _head_cols(h, _RK)] = _rope128_t(dk[h], cosv, sinv).astype(BF16)

    rb = lambda b: nb - 1 - b
    return _pallas(
        body, name="ret_bwd", grid=(RH // G, nb),
        in_specs=[pl.BlockSpec((T, G * RET_HEAD_COLS), lambda h, b: (rb(b), h)),
                  pl.BlockSpec((T, RET_QK), lambda h, b: (rb(b), 0)),
                  pl.BlockSpec((T, RET_QK), lambda h, b: (rb(b), 0)),
                  pl.BlockSpec((G, 8, LANES), lambda h, b: (h, 0, 0)),
                  pl.BlockSpec((1, G * RET_V), lambda h, b: (0, h)),
                  pl.BlockSpec((T, G * RET_V), lambda h, b: (rb(b), h)),
                  pl.BlockSpec((T, G * RET_V), lambda h, b: (rb(b), h)),
                  pl.BlockSpec((G, None, RET_QK, RET_V), lambda h, b: (h, rb(b), 0, 0)),
                  _ANY],
        out_specs=[pl.BlockSpec((T, G * RET_HEAD_COLS), lambda h, b: (rb(b), h)),
                   pl.BlockSpec((1, G * RET_V), lambda h, b: (0, h))],
        out_shape=[jax.ShapeDtypeStruct(dproj.shape, dproj.dtype), jax.ShapeDtypeStruct((1, RH * RET_V), F32)],
        scratch_shapes=[pltpu.VMEM((G, RET_QK, RET_V), F32)],
        input_output_aliases={8: 0},
        compiler_params=_params(("parallel", "arbitrary")),
    )(proj, cosr, sinr, lgam, gain, ry, dgated, states, dproj)


def _rope_pe(t, c, s1, s2):
    return t * c + pltpu.roll(t, LANES - QK_ROPE // 2, 1) * s1 + pltpu.roll(t, QK_ROPE // 2, 1) * s2


def _rope_pe_t(d, c, s1, s2):
    return d * c + pltpu.roll(d * s1, QK_ROPE // 2, 1) + pltpu.roll(d * s2, LANES - QK_ROPE // 2, 1)


ATTN_C2 = (QK_NOPE + QK_ROPE) ** -0.5 * LOG2E


def _qkv_proj(cqn, ckvn, wq, wkv, kpe, tabs, MH, *, tm=512, heads=4):
    S = cqn.shape[0]
    tm = _tile(S, tm)
    hb = _tile(MH, heads)
    W = 2 * LANES
    c_t, s1_t, s2_t = tabs

    def body(cq_ref, ckv_ref, wq_ref, wkv_ref, kpe_ref, c_ref, s1_ref, s2_ref, qf_ref, kf_ref, v_ref):
        c, s1, s2 = c_ref[...], s1_ref[...], s2_ref[...]
        q = _dot(cq_ref[...], wq_ref[...], NN)
        kv = _dot(ckv_ref[...], wkv_ref[...], NN)
        kper = _rope_pe(kpe_ref[...], c, s1, s2).astype(BF16)
        for h in range(hb):
            lo, mid, hi = h * W, h * W + QK_NOPE, (h + 1) * W
            qf_ref[:, lo:mid] = (q[:, lo:mid] * ATTN_C2).astype(BF16)
            qf_ref[:, mid:hi] = (_rope_pe(q[:, mid:hi], c, s1, s2) * ATTN_C2).astype(BF16)
            kf_ref[:, lo:mid] = kv[:, lo:mid].astype(BF16)
            kf_ref[:, mid:hi] = kper
            v_ref[:, h * V_HEAD:(h + 1) * V_HEAD] = kv[:, mid:hi].astype(BF16)

    tab = pl.BlockSpec((tm, LANES), lambda i, j: (i, 0))
    grp = pl.BlockSpec((tm, hb * W), lambda i, j: (i, j))
    return _pallas(
        body, name="qkv_proj", grid=(S // tm, MH // hb),
        in_specs=[pl.BlockSpec((tm, cqn.shape[1]), lambda i, j: (i, 0)),
                  pl.BlockSpec((tm, ckvn.shape[1]), lambda i, j: (i, 0)),
                  pl.BlockSpec((wq.shape[0], hb * W), lambda i, j: (0, j)),
                  pl.BlockSpec((wkv.shape[0], hb * W), lambda i, j: (0, j)), tab, tab, tab, tab],
        out_specs=[grp, grp, pl.BlockSpec((tm, hb * V_HEAD), lambda i, j: (i, j))],
        out_shape=[jax.ShapeDtypeStruct((S, MH * W), BF16)] * 2 + [jax.ShapeDtypeStruct((S, MH * V_HEAD), BF16)],
        compiler_params=_params(("parallel", "parallel")),
    )(cqn, ckvn, wq, wkv, kpe, c_t, s1_t, s2_t)


def _chunk_mask(T):
    n = lax.broadcasted_iota(jnp.int32, (T, T), 0)
    m = lax.broadcasted_iota(jnp.int32, (T, T), 1)
    return (m // CHUNK) <= (n // CHUNK)


def _lanes_to(v, width):
    return jnp.tile(v, (1, width // LANES))


def _attn_fwd(qf, kf, vb, MH, *, T, heads, name, prev=(), deps=()):
    S = qf.shape[0]
    nt = S // T
    n_skip = len(prev) + len(deps)

    def body(q_ref, k_ref, v_ref, *rest):
        o_ref, ob_ref, lse_ref, m_sc, l_sc, acc_sc, s_a, s_b = rest[n_skip:]
        qi = pl.program_id(1)
        m_sc[...] = jnp.full_like(m_sc, NEG)
        l_sc[...] = jnp.zeros_like(l_sc)
        acc_sc[...] = jnp.zeros_like(acc_sc)

        def rows_of(kt):
            return pl.ds(pl.multiple_of(kt * T, T), T)

        def scores(kt):
            return _dot(q_ref[...], k_ref[rows_of(kt), :], NT)

        def update(s, kt):
            m_prev = m_sc[...]
            m_new = jnp.maximum(m_prev, jnp.max(s, axis=-1, keepdims=True))
            alpha = jnp.exp2(m_prev - m_new)
            p = jnp.exp2(s - _lanes_to(m_new, T))
            l_sc[...] = alpha * l_sc[...] + jnp.sum(p, axis=-1, keepdims=True)
            acc_sc[...] = alpha * acc_sc[...] + _dot(p, v_ref[rows_of(kt), :], NN)
            m_sc[...] = m_new

        def masked(s):
            return jnp.where(_chunk_mask(T), s, NEG)

        @pl.when(qi == 0)
        def _():
            update(masked(scores(0)), 0)

        @pl.when(qi > 0)
        def _():
            s_a[...] = masked(scores(qi))
            s_b[...] = scores(0)
            update(s_a[...], qi)
            s_a[...] = scores(jnp.minimum(1, qi - 1))
            update(s_b[...], 0)

            def pair(j, carry):
                s_b[...] = scores(2 * j)
                update(s_a[...], 2 * j - 1)
                s_a[...] = scores(jnp.minimum(2 * j + 1, qi - 1))
                update(s_b[...], 2 * j)
                return carry

            lax.fori_loop(1, (qi + 1) // 2, pair, 0)

            @pl.when(qi % 2 == 0)
            def _():
                update(s_a[...], qi - 1)
        l = l_sc[...]
        o = acc_sc[...] / l
        o_ref[...] = o
        ob_ref[...] = o.astype(BF16)
        lse_ref[...] = m_sc[...] + jnp.log(l) * LOG2E

    h0, h1 = heads
    out_shape = [jax.ShapeDtypeStruct((S, MH * LANES), F32), jax.ShapeDtypeStruct((S, MH * LANES), BF16),
                 jax.ShapeDtypeStruct((MH, S, LANES), F32)]
    row = pl.BlockSpec((T, LANES), lambda h, i: (i, h0 + h))
    return _pallas(
        body, name=name, grid=(h1 - h0, nt),
        in_specs=[pl.BlockSpec((T, 2 * LANES), lambda h, i: (i, h0 + h)),
                  pl.BlockSpec((S, 2 * LANES), lambda h, i: (0, h0 + h)),
                  pl.BlockSpec((S, LANES), lambda h, i: (0, h0 + h))] + [_ANY] * (len(prev) + len(deps)),
        out_specs=[row, row, pl.BlockSpec((None, T, LANES), lambda h, i: (h0 + h, i, 0))],
        out_shape=out_shape,
        scratch_shapes=[pltpu.VMEM((T, LANES), F32), pltpu.VMEM((T, LANES), F32), pltpu.VMEM((T, LANES), F32),
                        pltpu.VMEM((T, T), F32), pltpu.VMEM((T, T), F32)],
        input_output_aliases={3 + i: i for i in range(len(prev))},
        compiler_params=_params(("parallel", "parallel")),
    )(qf, kf, vb, *prev, *deps)


def _attn_bwd(qf, kf, vb, dob, lse2, delta, tabs, MH, *, T, deps=()):
    S = qf.shape[0]
    nt = S // T
    scale = (QK_NOPE + QK_ROPE) ** -0.5
    n_dep = len(deps)

    def body(q_ref, k_ref, v_ref, do_ref, lse_ref, dl_ref, c_ref, s1_ref, s2_ref, *rest):
        dqa_ref, dkv_ref, dkpe_ref, dq_ref, dk_sc, dv_sc, s_a, dp_a, s_b, dp_b = rest[n_dep:]
        kj = pl.program_id(1)

        @pl.when(kj == 0)
        def _():
            dq_ref[...] = jnp.zeros_like(dq_ref)

        dk_sc[...] = jnp.zeros_like(dk_sc)
        dv_sc[...] = jnp.zeros_like(dv_sc)

        def rows_of(qt):
            return pl.ds(pl.multiple_of(qt * T, T), T)

        def products(qt):
            rows = rows_of(qt)
            return _dot(q_ref[rows, :], k_ref[...], NT), _dot(do_ref[rows, :], v_ref[...], NT)

        def update(s, dp, qt):
            rows = rows_of(qt)
            q, dov = q_ref[rows, :], do_ref[rows, :]
            p = jnp.exp2(s - _lanes_to(lse_ref[rows, :], T))
            ds = p * (dp - _lanes_to(dl_ref[rows, :], T))
            dv_sc[...] += _dot(p, dov, TN)
            dk_sc[...] += _dot(ds, q, TN)
            dq_ref[rows, :] += _dot(ds, k_ref[...], NN)

        def masked(s):
            return jnp.where(_chunk_mask(T), s, NEG)

        @pl.when(kj == nt - 1)
        def _():
            s, dp = products(kj)
            update(masked(s), dp, kj)

        @pl.when(kj < nt - 1)
        def _():
            s, dp = products(kj)
            s_a[...], dp_a[...] = masked(s), dp
            s_b[...], dp_b[...] = products(kj + 1)
            update(s_a[...], dp_a[...], kj)
            s_a[...], dp_a[...] = products(jnp.minimum(kj + 2, nt - 1))
            update(s_b[...], dp_b[...], kj + 1)

            def pair(j, carry):
                t0 = kj + 2 * j
                s_b[...], dp_b[...] = products(t0 + 1)
                update(s_a[...], dp_a[...], t0)
                s_a[...], dp_a[...] = products(jnp.minimum(t0 + 2, nt - 1))
                update(s_b[...], dp_b[...], t0 + 1)
                return carry

            lax.fori_loop(1, (nt - kj) // 2, pair, 0)

            @pl.when((nt - kj) % 2 == 1)
            def _():
                update(s_a[...], dp_a[...], nt - 1)
        dkv_ref[:, :QK_NOPE] = (dk_sc[:, :QK_NOPE] * (1.0 / LOG2E)).astype(BF16)
        dkv_ref[:, QK_NOPE:] = dv_sc[...].astype(BF16)
        dkpe_ref[...] = dk_sc[:, QK_NOPE:] * (1.0 / LOG2E)

        @pl.when(kj == nt - 1)
        def _():
            dqa_ref[:, :QK_NOPE] = (dq_ref[:, :QK_NOPE] * scale).astype(BF16)
            dqa_ref[:, QK_NOPE:] = (_rope_pe_t(dq_ref[:, QK_NOPE:], c_ref[...], s1_ref[...], s2_ref[...])
                                    * scale).astype(BF16)

    stat = pl.BlockSpec((None, S, LANES), lambda h, j: (h, 0, 0))
    tab = pl.BlockSpec((S, LANES), lambda h, j: (0, 0))
    return _pallas(
        body, name="attn_bwd", grid=(MH, nt),
        in_specs=[pl.BlockSpec((S, 2 * LANES), lambda h, j: (0, h)),
                  pl.BlockSpec((T, 2 * LANES), lambda h, j: (j, h)),
                  pl.BlockSpec((T, LANES), lambda h, j: (j, h)),
                  pl.BlockSpec((S, LANES), lambda h, j: (0, h)), stat, stat, tab, tab, tab] + [_ANY] * n_dep,
        out_specs=[pl.BlockSpec((S, 2 * LANES), lambda h, j: (0, h)),
                   pl.BlockSpec((T, 2 * LANES), lambda h, j: (j, h)),
                   pl.BlockSpec((T, LANES), lambda h, j: (j, h))],
        out_shape=[jax.ShapeDtypeStruct((S, MH * 2 * LANES), BF16), jax.ShapeDtypeStruct((S, MH * 2 * LANES), BF16),
                   jax.ShapeDtypeStruct((S, MH * LANES), F32)],
        scratch_shapes=[pltpu.VMEM((S, 2 * LANES), F32), pltpu.VMEM((T, 2 * LANES), F32), pltpu.VMEM((T, LANES), F32)]
        + [pltpu.VMEM((T, T), F32)] * 4,
        compiler_params=_params(("parallel", "arbitrary")),
    )(qf, kf, vb, dob, lse2, delta, *tabs, *deps)


def _kpe_sum(dkpe_h, tabs, MH, *, tr=256):
    S = dkpe_h.shape[0]
    tr = _tile(S, tr)

    def body(dk_ref, c_ref, s1_ref, s2_ref, dkpe_ref):
        tot = dk_ref[:, :LANES]
        for h in range(1, MH):
            tot = tot + dk_ref[:, h * LANES:(h + 1) * LANES]
        dkpe_ref[...] = _rope_pe_t(tot, c_ref[...], s1_ref[...], s2_ref[...]).astype(BF16)

    tab = pl.BlockSpec((tr, LANES), lambda i: (i, 0))
    return _pallas(
        body, name="kpe_sum", grid=(S // tr,),
        in_specs=[pl.BlockSpec((tr, MH * LANES), lambda i: (i, 0)), tab, tab, tab],
        out_specs=tab, out_shape=jax.ShapeDtypeStruct((S, LANES), BF16),
        compiler_params=_params(("parallel",)),
    )(dkpe_h, *tabs)


ROW_ALIGN = 16


def _blk(R, C, block_bytes=2 << 20):
    cap = max(ROW_ALIGN, block_bytes // (C * 4))
    for t in range(min(R, cap) // ROW_ALIGN * ROW_ALIGN, LANES - 1, -ROW_ALIGN):
        if R % t == 0:
            return t, C
    if R <= cap:
        return R, C
    tc = C
    while R * tc * 4 > block_bytes and tc % (2 * LANES) == 0:
        tc //= 2
    return R, tc


def _rows_call(fn, ins, out_dtypes, *, name, deps=()):
    R, C = ins[0].shape
    tr, tc = _blk(R, C)
    n_in, n_dep = len(ins), len(deps)

    def body(*refs):
        vals = fn(*[r[...] for r in refs[:n_in]])
        for r, v in zip(refs[n_in + n_dep:], vals):
            r[...] = v.astype(r.dtype)

    blk = pl.BlockSpec((tr, tc), lambda i, j: (i, j))
    res = _pallas(
        body, name=name, grid=(R // tr, C // tc), in_specs=[blk] * n_in + [_ANY] * n_dep,
        out_specs=[blk] * len(out_dtypes),
        out_shape=[jax.ShapeDtypeStruct((R, C), d) for d in out_dtypes],
        compiler_params=_params(("parallel", "parallel")),
    )(*ins, *deps)
    return res


def _adamw_vals(w, g, m, v):
    m = ADAM_B1 * m + (1.0 - ADAM_B1) * g
    v = ADAM_B2 * v + (1.0 - ADAM_B2) * (g * g)
    m_hat = m / (1.0 - ADAM_B1 ** ADAM_STEP)
    v_hat = v / (1.0 - ADAM_B2 ** ADAM_STEP)
    delta = -ADAM_LR * (m_hat / (jnp.sqrt(v_hat) + ADAM_EPS) + ADAM_WD * w)
    return delta, m, v


def _sum_pair(p, theirs, place, *, name):
    _, R, C = p.shape
    R2 = R // 2
    tr, tc = _blk(R2, C)
    p4 = p.reshape(N_CHIPS, 2, R2, C)

    def body(place_ref, a_ref, b_ref, o_ref):
        o_ref[...] = (a_ref[...].astype(F32) + b_ref[...].astype(F32)).astype(BF16)

    spec = pltpu.PrefetchScalarGridSpec(
        num_scalar_prefetch=1, grid=(N_CHIPS, R2 // tr, C // tc),
        in_specs=[pl.BlockSpec((None, None, tr, tc), lambda q, i, j, pr: (q, pr[0], i, j)),
                  pl.BlockSpec((None, tr, tc), lambda q, i, j, pr: (q, i, j))],
        out_specs=pl.BlockSpec((None, tr, tc), lambda q, i, j, pr: (q, i, j)))
    return _pallas(body, name=name, grid_spec=spec, out_shape=jax.ShapeDtypeStruct((N_CHIPS, R2, C), BF16),
                   compiler_params=_params(("parallel", "parallel", "parallel")))(place, p4, theirs)


def _sum_chips(p, theirs, recv, place, *, name):
    _, R, C = p.shape
    R2 = R // 2
    tr, tc = _blk(R2, C)
    p4 = p.reshape(N_CHIPS, 2, R2, C)

    def body(place_ref, a_ref, b_ref, r0_ref, r1_ref, r2_ref, o_ref):
        own = a_ref[...].astype(F32) + b_ref[...].astype(F32)
        o_ref[...] = ((own + r0_ref[...].astype(F32)) + r1_ref[...].astype(F32)) + r2_ref[...].astype(F32)

    def slot(k):
        return pl.BlockSpec((None, tr, tc), lambda i, j, pr: (k, i, j))

    spec = pltpu.PrefetchScalarGridSpec(
        num_scalar_prefetch=1, grid=(R2 // tr, C // tc),
        in_specs=[pl.BlockSpec((None, None, tr, tc), lambda i, j, pr: (pr[1], pr[0], i, j)),
                  pl.BlockSpec((None, tr, tc), lambda i, j, pr: (pr[1], i, j)), slot(0), slot(1), slot(2)],
        out_specs=pl.BlockSpec((None, tr, tc), lambda i, j, pr: (pr[0], i, j)))
    return _pallas(body, name=name, grid_spec=spec, out_shape=jax.ShapeDtypeStruct((2, R2, C), F32),
                   compiler_params=_params(("parallel", "parallel")))(place, p4, theirs, recv, recv, recv)


def _me():
    return lax.axis_index("x"), lax.axis_index("y"), lax.axis_index("c")


def _other_chips(x, y):
    return [(1 - x, y), (x, 1 - y), (1 - x, 1 - y)]


def _rcopy(src, dst, ssem, rsem, dev):
    return pltpu.make_async_remote_copy(src_ref=src, dst_ref=dst, send_sem=ssem, recv_sem=rsem,
                                        device_id=dev, device_id_type=MESH)


def _cast_into_slot(w, place, *, name, rows=None, deps=()):
    R, C = w.shape
    rows = R if rows is None else rows
    tr, tc = _blk(R, C)

    def body(place_ref, w_ref, *rest):
        rest[-1][...] = w_ref[...].astype(BF16)

    spec = pltpu.PrefetchScalarGridSpec(
        num_scalar_prefetch=1, grid=(R // tr, C // tc),
        in_specs=[pl.BlockSpec((tr, tc), lambda i, j, pr: (i, j))] + [_ANY] * len(deps),
        out_specs=pl.BlockSpec((None, tr, tc), lambda i, j, pr: (pr[1], i, j)))
    out = _pallas(body, name=name, grid_spec=spec, out_shape=jax.ShapeDtypeStruct((N_CHIPS, rows, C), BF16),
                  compiler_params=_params(("parallel", "parallel")))(place, w, *deps)
    return out.reshape(N_CHIPS, 2, rows // 2, C)


def _gather_ici_plan(bufs):
    x, y, c = _me()
    j = 2 * x + y
    plan = []
    for i, buf in enumerate(bufs):
        for k, (px, py) in enumerate(_other_chips(x, y)):
            plan.append((3 * i + k, buf.at[j, c], buf.at[j, c], (px, py, c)))
    return plan


def _forward_halves(bufs, *, name):
    n = len(bufs)

    def body(*refs):
        outs = refs[n:2 * n]
        ssem, rsem = refs[2 * n:]
        x, y, c = _me()
        sib = (x, y, 1 - c)
        cps = []
        for i in range(n):
            for k, (px, py) in enumerate(_other_chips(x, y)):
                slot = outs[i].at[2 * px + py, c]
                r = _rcopy(slot, slot, ssem.at[3 * i + k], rsem.at[3 * i + k], sib)
                r.start()
                cps.append(r)
        for r in cps:
            r.wait()

    return _pallas(
        body, name=name, in_specs=[_ANY] * n, out_specs=[_ANY] * n,
        out_shape=[jax.ShapeDtypeStruct(b.shape, b.dtype) for b in bufs],
        scratch_shapes=[pltpu.SemaphoreType.DMA((3 * n,))] * 2,
        input_output_aliases={i: i for i in range(n)},
        compiler_params=pltpu.CompilerParams(has_side_effects=True),
    )(*bufs)


_HBM = pl.BlockSpec(memory_space=pltpu.HBM)
_SEM = pl.BlockSpec(memory_space=pltpu.SEMAPHORE)
_EFFECT = pltpu.SideEffectType.DATAFLOW_SIDE_EFFECTING


def _split_start(bufs, plan, n_copies, *, name):
    n = len(bufs)

    def body(*refs):
        ssem, rsem = refs[n], refs[n + 1]
        for s, src, dst, dev in plan(refs[:n]):
            _rcopy(src, dst, ssem.at[s], rsem.at[s], dev).start()
        refs[-1][...] = jnp.zeros_like(refs[-1])

    res = _pallas(
        body, name=name, in_specs=[_HBM] * n,
        out_specs=(_SEM, _SEM, *[_HBM] * n, pl.BlockSpec(memory_space=pltpu.VMEM)),
        out_shape=(pltpu.SemaphoreType.DMA((n_copies,)), pltpu.SemaphoreType.DMA((n_copies,)),
                   *[pltpu.HBM(b.shape, b.dtype) for b in bufs], jax.ShapeDtypeStruct((8, LANES), F32)),
        input_output_aliases={i: 2 + i for i in range(n)},
        compiler_params=pltpu.CompilerParams(has_side_effects=_EFFECT),
    )(*[pltpu.with_memory_space_constraint(b, pltpu.HBM) for b in bufs])
    return res[0], res[1], list(res[2:2 + n]), res[-1]


def _split_wait(ssem, rsem, bufs, after, plan, *, name):
    n = len(bufs)

    def body(*refs):
        ssem_ref, rsem_ref = refs[n], refs[n + 1]
        for s, src, dst, dev in plan(refs[:n]):
            cp = _rcopy(src, dst, ssem_ref.at[s], rsem_ref.at[s], dev)
            cp.wait_send()
            cp.wait_recv()

    return list(_pallas(
        body, name=name, in_specs=[_HBM] * n + [_SEM, _SEM, _ANY], out_specs=[_HBM] * n,
        out_shape=[pltpu.HBM(b.shape, b.dtype) for b in bufs],
        input_output_aliases={i: i for i in range(n)},
        compiler_params=pltpu.CompilerParams(has_side_effects=_EFFECT),
    )(*bufs, ssem, rsem, after))


def _forward_plan(bufs):
    x, y, c = _me()
    plan = []
    for i, buf in enumerate(bufs):
        for k, (px, py) in enumerate(_other_chips(x, y)):
            plan.append((3 * i + k, buf.at[2 * px + py, c], buf.at[2 * px + py, c], (x, y, 1 - c)))
    return plan


def _join_plan(bufs):
    x, y, c = _me()
    return [(i, buf.at[c], buf.at[c], (x, y, 1 - c)) for i, buf in enumerate(bufs)]


def _swap_plan(n):
    def plan(bufs):
        x, y, c = _me()
        return [(i, bufs[i].at[:, 1 - c], bufs[n + i], (x, y, 1 - c)) for i in range(n)]
    return plan


def _scatter_plan(n):
    def plan(bufs):
        x, y, c = _me()
        out = []
        for i in range(n):
            for k, (px, py) in enumerate(_other_chips(x, y)):
                out.append((3 * i + k, bufs[i].at[2 * px + py], bufs[n + i].at[k], (px, py, c)))
        return out
    return plan


def _allreduce_small(parts, loss11):
    n = len(parts)
    widths = [p.shape[1] for p in parts]
    total = sum(widths) + LANES

    def body(*refs):
        o_ref, mine, buf, ssem, rsem = refs[n + 1:]
        x, y, c = _me()
        me = 4 * x + 2 * y + c
        off = 0
        for r, w in zip(refs[:n], widths):
            mine[:, off:off + w] = r[...]
            off += w
        mine[:, off:] = jnp.broadcast_to(refs[n][...], (1, LANES))
        buf[me] = mine[...]
        cps = []
        for k in range(1, 8):
            peer = (x ^ (k >> 2), y ^ ((k >> 1) & 1), c ^ (k & 1))
            r = _rcopy(mine, buf.at[me], ssem.at[k - 1], rsem.at[k - 1], peer)
            r.start()
            cps.append(r)
        for k in range(1, 8):
            peer = (x ^ (k >> 2), y ^ ((k >> 1) & 1), c ^ (k & 1))
            pid = 4 * peer[0] + 2 * peer[1] + peer[2]
            _rcopy(mine, buf.at[pid], ssem.at[k - 1], rsem.at[k - 1], peer).wait_recv()
        for r in cps:
            r.wait_send()
        tot = buf[0]
        for d in range(1, 8):
            tot = tot + buf[d]
        o_ref[...] = tot

    vm = pl.BlockSpec(memory_space=pltpu.VMEM)
    return _pallas(
        body, name="allreduce_small", in_specs=[vm] * (n + 1), out_specs=vm,
        out_shape=jax.ShapeDtypeStruct((1, total), F32),
        scratch_shapes=[pltpu.VMEM((1, total), F32), pltpu.VMEM((8, 1, total), F32),
                        pltpu.SemaphoreType.DMA((7,)), pltpu.SemaphoreType.DMA((7,))],
        compiler_params=pltpu.CompilerParams(has_side_effects=True),
    )(*parts, loss11)


def _adamw_small(red, ws, ms, vs):
    n = len(ws)

    def body(*refs):
        red_ref = refs[0]
        outs = refs[1 + 3 * n:]
        off = 0
        for i in range(n):
            w = refs[1 + i].shape[1]
            g = red_ref[:, off:off + w]
            d, m, v = _adamw_vals(refs[1 + i][...], g, refs[1 + n + i][...], refs[1 + 2 * n + i][...])
            for o, val in zip(outs[4 * i:4 * i + 4], (g, d, m, v)):
                o[...] = val
            off += w

    vm = pl.BlockSpec(memory_space=pltpu.VMEM)
    res = _pallas(
        body, name="adamw_small", in_specs=[vm] * (1 + 3 * n), out_specs=[vm] * (4 * n),
        out_shape=[jax.ShapeDtypeStruct(w.shape, F32) for w in ws for _ in range(4)],
    )(red, *ws, *ms, *vs)
    return [res[4 * i:4 * i + 4] for i in range(n)]


def _rope_tables(positions, S):
    pos = positions.reshape(S, 1).astype(F32)
    half = RET_QK // 2
    inv = ROPE_THETA ** (-jnp.arange(half, dtype=F32) / half)
    ang = pos * inv
    cosr = jnp.concatenate([jnp.cos(ang), jnp.cos(ang)], axis=1)
    sinr = jnp.concatenate([-jnp.sin(ang), jnp.sin(ang)], axis=1)
    half = QK_ROPE // 2
    inv = ROPE_THETA ** (-jnp.arange(half, dtype=F32) / half)
    ang = pos * inv
    z = jnp.zeros((S, half), F32)
    c = jnp.concatenate([jnp.cos(ang), jnp.cos(ang), z, z], axis=1)
    s1 = jnp.concatenate([-jnp.sin(ang), z, z, z], axis=1)
    s2 = jnp.concatenate([z, jnp.sin(ang), z, z], axis=1)
    return cosr, sinr, (c, s1, s2)


def _cat_cols(g):
    return jnp.concatenate([g[j] for j in range(N_CHIPS)], axis=1)


def _split_cols(w):
    return jnp.stack(jnp.split(w, N_CHIPS, axis=1))


def kernel(x, positions, norm_mix_g, w_in, ret_norm_g, w_ret_o, q_a_norm_g, w_q_b, kv_a_norm_g, w_kv_b, w_mla_o, w_out, norm_mlp_g, w_up, w_down, norm_f_g, loss_target, m_norm_mix_g, m_w_in, m_ret_norm_g, m_w_ret_o, m_q_a_norm_g, m_w_q_b, m_kv_a_norm_g, m_w_kv_b, m_w_mla_o, m_w_out, m_norm_mlp_g, m_w_up, m_w_down, m_norm_f_g, v_norm_mix_g, v_w_in, v_ret_norm_g, v_w_ret_o, v_q_a_norm_g, v_w_q_b, v_kv_a_norm_g, v_w_kv_b, v_w_mla_o, v_w_out, v_norm_mlp_g, v_w_up, v_w_down, v_norm_f_g):
    S, D = x.shape[1], x.shape[2]
    RVW = w_ret_o.shape[1] * N_CHIPS
    RH = RVW // RET_V
    RQW = RH * RET_QK
    MVW = w_mla_o.shape[1] * N_CHIPS
    MH = MVW // V_HEAD
    QL, KVL = w_q_b.shape[1], w_kv_b.shape[1]
    T_RET = _tile(S, 256)
    T_ATT = _tile(S, 512)

    xs = x.reshape(S, D)
    tgt = loss_target.reshape(S, D)
    cosr, sinr, pe_tabs = _rope_tables(positions, S)
    lgam = jnp.log(1.0 - 2.0 ** (-5.0 - jnp.arange(RH, dtype=F32)))
    lgam = jnp.broadcast_to(lgam[:, None, None], (RH, 8, LANES))

    big = ("w_in", "w_ret_o", "w_q_b", "w_kv_b", "w_mla_o", "w_out", "w_up", "w_down")
    w_sh = dict(w_in=w_in[0].T, w_ret_o=w_ret_o[0], w_q_b=w_q_b[0], w_kv_b=w_kv_b[0], w_mla_o=w_mla_o[0],
                w_out=w_out[0], w_up=w_up[0], w_down=w_down[0])
    m_sh = dict(w_in=m_w_in[0].T, w_ret_o=m_w_ret_o[0], w_q_b=m_w_q_b[0], w_kv_b=m_w_kv_b[0],
                w_mla_o=m_w_mla_o[0], w_out=m_w_out[0], w_up=m_w_up[0], w_down=m_w_down[0])
    v_sh = dict(w_in=v_w_in[0].T, w_ret_o=v_w_ret_o[0], w_q_b=v_w_q_b[0], w_kv_b=v_w_kv_b[0],
                w_mla_o=v_w_mla_o[0], w_out=v_w_out[0], w_up=v_w_up[0], w_down=v_w_down[0])
    col_sharded = ("w_q_b", "w_kv_b", "w_up")
    c_sh = w_in.shape[2]
    c_pad = -(-c_sh // 64) * 64
    place = jnp.stack([lax.axis_index("c"), 2 * lax.axis_index("x") + lax.axis_index("y")]).astype(jnp.int32)

    def whole(k, g):
        g = g.reshape(N_CHIPS, w_sh[k].shape[0], w_sh[k].shape[1])
        if k == "w_up":
            return g
        return _cat_cols(g) if k in col_sharded else g.reshape(-1, g.shape[2])

    first = ("w_in", "w_q_b", "w_kv_b")
    later = ("w_ret_o", "w_mla_o", "w_out", "w_up", "w_down")
    first_bufs = [_cast_into_slot(w_sh[k], place, name="cast_" + k, rows=c_pad if k == "w_in" else None)
                  for k in first]
    first_ssem, first_rsem, first_bufs, first_token = _split_start(
        first_bufs, _gather_ici_plan, 3 * len(first), name="gather_first_start")
    later_bufs = [_cast_into_slot(w_sh[k], place, name="cast_" + k, deps=(first_token,)) for k in later[:-1]]
    first_bufs = _split_wait(first_ssem, first_rsem, first_bufs, later_bufs[-1], _gather_ici_plan,
                             name="gather_first_wait")
    got = _forward_halves(first_bufs, name="gather_first_forward")
    full = {k: whole(k, g) for k, g in zip(first[1:], got[1:])}
    later_bufs.append(_cast_into_slot(w_sh[later[-1]], place, name="cast_" + later[-1], deps=(got[0],)))
    later_ssem, later_rsem, later_bufs, later_token = _split_start(
        later_bufs, _gather_ici_plan, 3 * len(later), name="gather_later_start")

    o_rq, o_rk, o_rv, o_rg = 0, RQW, 2 * RQW, 2 * RQW + RVW
    o_cq = 2 * RQW + 2 * RVW
    o_ckv, o_kpe = o_cq + QL, o_cq + QL + KVL
    o_gr = o_kpe + QK_ROPE
    o_gm = o_gr + D
    n_ret = RH * RET_HEAD_COLS
    off_gret, off_gmla, off_cq, off_ckv = n_ret, n_ret + D, n_ret + 2 * D, n_ret + 2 * D + QL
    n_a = off_ckv + KVL
    runs = []
    for h in range(RH):
        base = h * RET_HEAD_COLS
        runs += [(o_rq + h * RET_QK, RET_QK, base), (o_rk + h * RET_QK, RET_QK, base + RET_QK),
                 (o_rv + h * RET_V, RET_V, base + 2 * RET_QK), (o_rg + h * RET_V, RET_V, base + 2 * RET_QK + RET_V)]
    runs += [(o_gr, D, off_gret), (o_gm, D, off_gmla), (o_cq, QL, off_cq), (o_ckv, KVL, off_ckv),
             (o_kpe, QK_ROPE, n_a)]

    def take(parts, start, width):
        out, lo = [], 0
        for p in parts:
            hi = lo + p.shape[0]
            a, b = max(start, lo), min(start + width, hi)
            if a < b:
                out.append(p[a - lo:b - lo])
            lo = hi
        return out

    wi = [got[0].reshape(N_CHIPS, c_pad, D)[jj, :c_sh] for jj in range(N_CHIPS)]
    here = sorted(runs, key=lambda r: r[2])
    wa = jnp.concatenate([p for s0, w, _ in here[:-1] for p in take(wi, s0, w)], axis=0)
    wkpe = jnp.concatenate(take(wi, o_kpe, QK_ROPE) + [jnp.zeros((LANES - QK_ROPE, D), BF16)], axis=0)
    wq = jnp.pad(full["w_q_b"].reshape(QL, MH, QK_NOPE + QK_ROPE),
                 ((0, 0), (0, 0), (0, LANES - QK_ROPE))).reshape(QL, MH * 2 * LANES)
    wkv = full["w_kv_b"]

    u, rstd0 = _rmsnorm_fwd(xs, norm_mix_g, name="norm_mix")
    proj = _mm(u, wa, mode="nt", outs=[F32], name="in_proj", deps=(later_token,))
    kpe = _mm(u, wkpe, mode="nt", outs=[F32], name="kpe_proj")
    ry, gated, states = _ret_fwd(proj, cosr, sinr, lgam, ret_norm_g, RH, T=T_RET)
    cqn, ckvn, rstd_q, rstd_kv = _norm_pair_fwd(proj, q_a_norm_g, kv_a_norm_g, off_cq)
    qf, kf, vb = _qkv_proj(cqn, ckvn, wq, wkv, kpe, pe_tabs, MH)
    first_half = _attn_fwd(qf, kf, vb, MH, T=T_ATT, heads=(0, MH // 2), name="attn_fwd_a")
    later_bufs = _split_wait(later_ssem, later_rsem, later_bufs, first_half[0], _gather_ici_plan,
                             name="gather_later_wait")
    fwd_ssem, fwd_rsem, later_bufs, fwd_token = _split_start(
        later_bufs, _forward_plan, 3 * len(later), name="gather_later_forward_start")
    my, my_b, lse2 = _attn_fwd(qf, kf, vb, MH, T=T_ATT, heads=(MH // 2, MH), name="attn_fwd_b",
                               prev=first_half, deps=(fwd_token,))
    later_bufs = _split_wait(fwd_ssem, fwd_rsem, later_bufs, my, _forward_plan, name="gather_later_forward_wait")
    full.update({k: whole(k, g) for k, g in zip(later, later_bufs)})
    y_ret = _mm(gated, full["w_ret_o"], mode="nn", outs=[BF16], name="ret_o")
    y_mla, merged = _mm(my_b, full["w_mla_o"], mode="nn", outs=[BF16, BF16], name="mla_o",
                        epi=lambda acc, gr, gm, yr: (acc, _sigmoid(gr) * yr + _sigmoid(gm) * acc),
                        extras=((proj, off_gret), (proj, off_gmla), y_ret))
    h1 = _mm(merged, full["w_out"], mode="nn", outs=[F32], name="out_proj",
             epi=lambda acc, r: (acc + r,), extras=(xs,))
    n1, rstd1 = _rmsnorm_fwd(h1, norm_mlp_g, name="norm_mlp")

    def up_epi(acc):
        r = jnp.maximum(acc, 0.0)
        return acc, r * r

    z, act = _mm(n1, full["w_up"], mode="nn", outs=[F32, BF16], name="up_proj", epi=up_epi)
    h2 = _mm(act, full["w_down"], mode="nn", outs=[F32], name="down_proj",
             epi=lambda acc, r: (acc + r,), extras=(h1,))
    loss11, dh2, dh2_b, g_norm_f = _final_loss(h2, norm_f_g.reshape(1, D), tgt)

    dz = _mm(dh2_b, full["w_down"], mode="nt", outs=[BF16], name="down_bwd_x",
             epi=lambda acc, zz: (acc * (2.0 * jnp.maximum(zz, 0.0)),), extras=(z,))
    g_w_down = _mm(act, dh2_b, mode="tn", outs=[BF16], name="down_bwd_w")
    dn1 = _mm(dz, full["w_up"], mode="nt", outs=[F32], name="up_bwd_x")
    g_w_up = _mm(n1, dz, mode="tn", outs=[BF16], name="up_bwd_w", out_shards=True)

    def scatter_begin(tag, sums):
        lands = [lax.empty((3,) + s.shape[1:], s.dtype) for s in sums]
        return _split_start(sums + lands, _scatter_plan(len(sums)), 3 * len(sums), name="scatter_" + tag + "_start")

    def swap_begin(tag, grads):
        views = [g if g.ndim == 3 else g.reshape(N_CHIPS, g.shape[0] // N_CHIPS, g.shape[1]) for g in grads]
        views = [v.reshape(N_CHIPS, 2, v.shape[1] // 2, v.shape[2]) for v in views]
        lands = [lax.empty((N_CHIPS,) + v.shape[2:], v.dtype) for v in views]
        return _split_start(views + lands, _swap_plan(len(views)), len(views), name="swap_" + tag + "_start")

    def swap_end(tag, names, handle, after):
        n = len(names)
        bufs = _split_wait(handle[0], handle[1], handle[2], after, _swap_plan(n), name="swap_" + tag + "_wait")
        pcs = [b.reshape(N_CHIPS, 2 * b.shape[2], b.shape[3]) for b in bufs[:n]]
        sums = [_sum_pair(p, t, place, name="sum_pair_" + k) for k, p, t in zip(names, pcs, bufs[n:])]
        return pcs, bufs[n:], sums

    g1 = ("w_up", "w_down")
    swap1 = swap_begin("g1", (g_w_up, g_w_down))
    dh1, g_norm_mlp, dh1_b = _rmsnorm_bwd(dn1, h1, rstd1, norm_mlp_g, name="norm_mlp_bwd", res=dh2,
                                          deps=(swap1[3],), bf16_copy=1)
    dmerged = _mm(dh1_b, full["w_out"], mode="nt", outs=[F32], name="out_bwd_x")
    pcs1, theirs1, sums1 = swap_end("g1", g1, swap1, dmerged)
    ssem1, rsem1, bufs1, token1 = scatter_begin("g1", sums1)
    g_w_out = _mm(merged, dh1_b, mode="tn", outs=[BF16], name="out_bwd_w", deps=(token1,))
    dproj, dy_ret, dy_mla = _merge_bwd(dmerged, proj, y_ret, y_mla, D, off_gret)
    dgated = _mm(dy_ret, full["w_ret_o"], mode="nt", outs=[F32], name="ret_o_bwd_x")
    g_w_ret_o = _mm(gated, dy_ret, mode="tn", outs=[BF16], name="ret_o_bwd_w")
    dproj, g_ret_norm = _ret_bwd(proj, cosr, sinr, lgam, ret_norm_g, ry, dgated, states, dproj, RH, T=T_RET)
    def delta_epi(acc, o):
        rows = acc.shape[0]
        return acc, [jnp.broadcast_to(jnp.sum(acc[:, lo:lo + V_HEAD] * o[:, lo:lo + V_HEAD], axis=-1, keepdims=True),
                                      (rows, LANES)) for lo in range(0, acc.shape[1], V_HEAD)]

    dob, delta = _mm(dy_mla, full["w_mla_o"], mode="nt", outs=[BF16], name="mla_o_bwd_x", epi=delta_epi,
                     extras=(my,), more_outs=lambda tm, tn: [
                         (jax.ShapeDtypeStruct((MH, S, LANES), F32),
                          pl.BlockSpec((tn // V_HEAD, tm, LANES), lambda i, j, k: (j, i, 0)))])
    g_w_mla_o = _mm(my_b, dy_mla, mode="tn", outs=[BF16], name="mla_o_bwd_w")
    g2 = ("w_out", "w_ret_o", "w_mla_o")
    swap2 = swap_begin("g2", (g_w_out, g_w_ret_o, g_w_mla_o))
    dq_all, dkv_all, dkpe_h = _attn_bwd(qf, kf, vb, dob, lse2, delta, pe_tabs, MH, T=T_ATT, deps=(swap2[3],))
    pcs2, theirs2, sums2 = swap_end("g2", g2, swap2, dkv_all)
    ssem2, rsem2, bufs2, token2 = scatter_begin("g2", sums2)
    dkpe = _kpe_sum(dkpe_h, pe_tabs, MH)
    dcqn = _mm(dq_all, wq, mode="nt", outs=[F32], name="q_bwd_x", deps=(token2,))
    g_wq = _mm(cqn, dq_all, mode="tn", outs=[BF16], name="q_bwd_w")
    dckvn = _mm(dkv_all, wkv, mode="nt", outs=[F32], name="kv_bwd_x")
    g_wkv = _mm(ckvn, dkv_all, mode="tn", outs=[BF16], name="kv_bwd_w")
    dproj, g_q_a, g_kv_a = _norm_pair_bwd(dcqn, dckvn, proj, rstd_q, rstd_kv, q_a_norm_g, kv_a_norm_g, dproj, off_cq)
    g_wa = _mm(dproj, u, mode="tn", outs=[BF16], name="in_bwd_w")
    g_wkpe = _mm(dkpe, u, mode="tn", outs=[BF16], name="kpe_bwd_w")

    there = sorted(runs)
    g_parts = [g_wa, g_wkpe]
    g_w_in = jnp.stack([jnp.concatenate(
        [p for s0, w, d0 in there for a, b in [(max(s0, jj * c_sh), min(s0 + w, (jj + 1) * c_sh))] if a < b
         for p in take(g_parts, d0 + a - s0, b - a)] + [jnp.zeros((c_pad - c_sh, D), BF16)], axis=0)
        for jj in range(N_CHIPS)])
    gq = g_wq.reshape(QL, MH, 2 * LANES)[:, :, :QK_NOPE + QK_ROPE].reshape(QL, MH * (QK_NOPE + QK_ROPE))
    g3 = ("w_in", "w_q_b", "w_kv_b")
    swap3 = swap_begin("g3", (g_w_in, _split_cols(gq), _split_cols(g_wkv)))
    pcs3, theirs3, sums3 = swap_end("g3", g3, swap3, swap3[3])
    ssem3, rsem3, bufs3, token3 = scatter_begin("g3", sums3)

    def chip_sums(names, pcs, theirs, recv):
        return [_sum_chips(p, t, r, place, name="sum_chips_" + k) for k, p, t, r in zip(names, pcs, theirs, recv)]

    bufs1 = _split_wait(ssem1, rsem1, bufs1, token3, _scatter_plan(len(g1)), name="scatter_g1_wait")
    bufs2 = _split_wait(ssem2, rsem2, bufs2, token3, _scatter_plan(len(g2)), name="scatter_g2_wait")
    halves12 = chip_sums(g1, pcs1, theirs1, bufs1[len(g1):]) + chip_sums(g2, pcs2, theirs2, bufs2[len(g2):])
    jssem, jrsem, halves12, join_token = _split_start(halves12, _join_plan, len(halves12), name="join_g12_start")
    du = _mm(dproj, wa, mode="nn", outs=[F32], name="in_bwd_x", tk=2816, tail=(dkpe, wkpe), deps=(join_token,))
    dx, g_norm_mix = _rmsnorm_bwd(du, xs, rstd0, norm_mix_g, name="norm_mix_bwd", res=dh1)

    bufs3 = _split_wait(ssem3, rsem3, bufs3, dx, _scatter_plan(len(g3)), name="scatter_g3_wait")
    halves12 = _split_wait(jssem, jrsem, halves12, dx, _join_plan, name="join_g12_wait")
    j3ssem, j3rsem, halves3, join3_token = _split_start(
        chip_sums(g3, pcs3, theirs3, bufs3[len(g3):]), _join_plan, len(g3), name="join_g3_start")

    small = ("norm_mix_g", "ret_norm_g", "q_a_norm_g", "kv_a_norm_g", "norm_mlp_g", "norm_f_g")
    g_small = [g_norm_mix, g_ret_norm, g_q_a, g_kv_a, g_norm_mlp, g_norm_f]
    red = _allreduce_small(g_small, loss11)
    loss = red[0, red.shape[1] - 1]
    w_small = [norm_mix_g, ret_norm_g, q_a_norm_g, kv_a_norm_g, norm_mlp_g, norm_f_g]
    m_small = [m_norm_mix_g, m_ret_norm_g, m_q_a_norm_g, m_kv_a_norm_g, m_norm_mlp_g, m_norm_f_g]
    v_small = [v_norm_mix_g, v_ret_norm_g, v_q_a_norm_g, v_kv_a_norm_g, v_norm_mlp_g, v_norm_f_g]
    row = lambda a: a.reshape(1, -1)
    upd = _adamw_small(red, [row(a) for a in w_small], [row(a) for a in m_small], [row(a) for a in v_small])
    out_g, out_d, out_m, out_v = {}, {}, {}, {}
    for k, wv, (g_, d_, m_, v_) in zip(small, w_small, upd):
        out_g[k], out_d[k], out_m[k], out_v[k] = [a.reshape(wv.shape) for a in (g_, d_, m_, v_)]

    def adamw_shard(k, joined, deps=()):
        g = joined.reshape(2 * joined.shape[1], joined.shape[2])
        res = _rows_call(lambda w, g, m, v: (g,) + _adamw_vals(w, g, m, v),
                         [w_sh[k], g, m_sh[k], v_sh[k]], [F32] * 4, name="adamw_" + k, deps=deps)
        if k == "w_in":
            res = [r.T for r in res]
        out_g[k], out_d[k], out_m[k], out_v[k] = [r[None] for r in res]
        return res[0]

    for k, joined in zip(g1 + g2, halves12):
        last = adamw_shard(k, joined, deps=(join3_token,))
    halves3 = _split_wait(j3ssem, j3rsem, halves3, last, _join_plan, name="join_g3_wait")
    for k, joined in zip(g3, halves3):
        adamw_shard(k, joined)

    order = ("norm_mix_g", "w_in", "ret_norm_g", "w_ret_o", "q_a_norm_g", "w_q_b", "kv_a_norm_g", "w_kv_b",
             "w_mla_o", "w_out", "norm_mlp_g", "w_up", "w_down", "norm_f_g")
    return (loss, dx.reshape(1, S, D), *[out_g[k] for k in order], *[out_d[k] for k in order],
            *[out_m[k] for k in order], *[out_v[k] for k in order])
```

```python
import math

import jax
import jax.numpy as jnp
from jax import lax
from jax.experimental import pallas as pl
from jax.experimental.pallas import tpu as pltpu

F32 = jnp.float32
BF16 = jnp.bfloat16

EPS = 1e-6
ROPE_THETA = 10000.0
CHUNK = 64
RET_QK = 128
RET_V = 256
RET_HEAD_COLS = 2 * RET_QK + 2 * RET_V
QK_NOPE = 128
QK_ROPE = 64
V_HEAD = 128
LANES = 128
LOG2E = math.log2(math.e)

ADAM_LR = 0.001
ADAM_B1 = 0.9
ADAM_B2 = 0.999
ADAM_EPS = 1e-08
ADAM_WD = 0.01
ADAM_STEP = 10

N_CHIPS = 4
VMEM_LIMIT = 56 * 1024 * 1024
MESH = pl.DeviceIdType.MESH
NEG = -1e30


def _pallas(body, **kw):
    return pl.pallas_call(body, **kw)


def _params(sem=None):
    return pltpu.CompilerParams(dimension_semantics=sem, vmem_limit_bytes=VMEM_LIMIT)


def _tile(n, want):
    t = min(n, want)
    while n % t:
        t //= 2
    return t


_ANY = pl.BlockSpec(memory_space=pl.ANY)
TN_BF16_TK = 4096


def _mm(a, b, *, mode, outs, name, epi=None, extras=(), deps=(), out_shards=False, more_outs=None, tail=None,
        tm=1024, tn=1024, tk=2048):
    shards = b.shape[0] if b.ndim == 3 else 1
    brows, bcols = b.shape[-2], b.shape[-1] * shards
    if mode == "nn":
        (M, K), N = a.shape, bcols
    elif mode == "nt":
        (M, K), N = a.shape, brows
    else:
        (K, M), N = a.shape, bcols
    if mode == "tn" and a.dtype == BF16 and b.dtype == BF16:
        tk = max(tk, TN_BF16_TK)
    tm = _tile(M, tm)
    tn = _tile(N // (shards if mode == "nn" else 1) // (N_CHIPS if out_shards else 1), tn)
    tk = _tile(K // (shards if mode == "nt" else 1), tk)
    nk = K // tk
    if mode == "nn":
        a_spec = pl.BlockSpec((tm, tk), lambda i, j, k: (i, k))
        dims = (((1,), (0,)), ((), ()))
        if shards > 1:
            per = N // shards // tn
            b_spec = pl.BlockSpec((None, tk, tn), lambda i, j, k: (j // per, k, j % per))
        else:
            b_spec = pl.BlockSpec((tk, tn), lambda i, j, k: (k, j))
    elif mode == "nt":
        a_spec = pl.BlockSpec((tm, tk), lambda i, j, k: (i, k))
        dims = (((1,), (1,)), ((), ()))
        if shards > 1:
            per = K // shards // tk
            b_spec = pl.BlockSpec((None, tn, tk), lambda i, j, k: (k // per, j, k % per))
        else:
            b_spec = pl.BlockSpec((tn, tk), lambda i, j, k: (j, k))
    else:
        assert shards == 1
        a_spec = pl.BlockSpec((tk, tm), lambda i, j, k: (k, i))
        b_spec = pl.BlockSpec((tk, tn), lambda i, j, k: (k, j))
        dims = (((0,), (0,)), ((), ()))
    if out_shards:
        assert not extras
        oper = N // N_CHIPS // tn
        o_spec = pl.BlockSpec((None, tm, tn), lambda i, j, k: (j // oper, i, j % oper))
        o_shape = (N_CHIPS, M, N // N_CHIPS)
    else:
        o_spec = pl.BlockSpec((tm, tn), lambda i, j, k: (i, j))
        o_shape = (M, N)
    more = [] if more_outs is None else more_outs(tm, tn)
    ex_arrays = [e[0] if isinstance(e, tuple) else e for e in extras]
    ex_specs = [pl.BlockSpec((tm, tn), lambda i, j, k, off=e[1] // tn: (i, off + j)) if isinstance(e, tuple)
                else o_spec for e in extras]
    n_ex, n_out, n_dep = len(extras), len(outs) + len(more), len(deps)
    if epi is None:
        epi = lambda acc: (acc,)
    tails, tail_specs = [], []
    if tail is not None:
        assert mode == "nn"
        tails = list(tail)
        k2 = tail[0].shape[1]
        tail_specs = [pl.BlockSpec((tm, k2), lambda i, j, k: (i, 0)), pl.BlockSpec((k2, tn), lambda i, j, k: (0, j))]
    n_tail = len(tails)

    def body(*refs):
        a_ref, b_ref = refs[0], refs[1]
        ex_refs = refs[2:2 + n_ex]
        t_refs = refs[2 + n_ex:2 + n_ex + n_tail]
        first_out = 2 + n_ex + n_tail + n_dep
        o_refs = refs[first_out:first_out + n_out]
        part = lax.dot_general(a_ref[...].astype(BF16), b_ref[...].astype(BF16), dims,
                               preferred_element_type=F32)

        def finish(acc):
            if n_tail:
                acc = acc + lax.dot_general(t_refs[0][...].astype(BF16), t_refs[1][...].astype(BF16), dims,
                                            preferred_element_type=F32)
            vals = epi(acc, *[r[...] for r in ex_refs])
            for r, v in zip(o_refs, vals):
                if isinstance(v, (list, tuple)):
                    for lead, piece in enumerate(v):
                        r[lead] = piece.astype(r.dtype)
                else:
                    r[...] = v.astype(r.dtype)

        if nk == 1:
            finish(part)
        else:
            acc_ref = refs[-1]
            k = pl.program_id(2)

            @pl.when(k == 0)
            def _():
                acc_ref[...] = part

            @pl.when(k > 0)
            def _():
                acc_ref[...] += part

            @pl.when(k == nk - 1)
            def _():
                finish(acc_ref[...])

    res = _pallas(
        body, name=name, grid=(M // tm, N // tn, nk),
        in_specs=[a_spec, b_spec] + ex_specs + tail_specs + [_ANY] * n_dep,
        out_specs=[o_spec] * len(outs) + [spec for _, spec in more],
        out_shape=[jax.ShapeDtypeStruct(o_shape, d) for d in outs] + [shape for shape, _ in more],
        scratch_shapes=[pltpu.VMEM((tm, tn), F32)] if nk > 1 else [],
        compiler_params=_params(("parallel", "parallel", "arbitrary")),
    )(a, b, *ex_arrays, *tails, *deps)
    return res[0] if n_out == 1 else res


def _rmsnorm_fwd(x, g, *, name, tr=256):
    S, W = x.shape
    tr = _tile(S, tr)

    def body(x_ref, g_ref, y_ref, r_ref):
        xv = x_ref[...]
        rstd = lax.rsqrt(jnp.mean(xv * xv, axis=-1, keepdims=True) + EPS)
        y_ref[...] = (xv * rstd * g_ref[...]).astype(BF16)
        r_ref[...] = rstd

    return _pallas(
        body, name=name, grid=(S // tr,),
        in_specs=[pl.BlockSpec((tr, W), lambda i: (i, 0)), pl.BlockSpec((1, W), lambda i: (0, 0))],
        out_specs=[pl.BlockSpec((tr, W), lambda i: (i, 0)), pl.BlockSpec((tr, 1), lambda i: (i, 0))],
        out_shape=[jax.ShapeDtypeStruct((S, W), BF16), jax.ShapeDtypeStruct((S, 1), F32)],
        compiler_params=_params(("parallel",)),
    )(x, g)


def _rmsnorm_bwd(dy, x, rstd, g, *, name, res=None, deps=(), bf16_copy=0, tr=256):
    S, W = x.shape
    tr = _tile(S, tr)
    has_res = res is not None

    def body(*refs):
        dy_ref, x_ref, r_ref, g_ref = refs[:4]
        dx_ref, dg_ref = refs[-2 - bf16_copy], refs[-1 - bf16_copy]
        rstd_v = r_ref[...]
        xhat = x_ref[...] * rstd_v
        dyv = dy_ref[...].astype(F32)
        dyg = dyv * g_ref[...]
        dx = rstd_v * (dyg - xhat * jnp.mean(dyg * xhat, axis=-1, keepdims=True))
        if has_res:
            dx = dx + refs[4][...]
        dx_ref[...] = dx.astype(dx_ref.dtype)
        if bf16_copy:
            refs[-1][...] = dx.astype(BF16)
        part = jnp.sum(dyv * xhat, axis=0, keepdims=True)

        @pl.when(pl.program_id(0) == 0)
        def _():
            dg_ref[...] = part

        @pl.when(pl.program_id(0) > 0)
        def _():
            dg_ref[...] += part

    row = pl.BlockSpec((tr, W), lambda i: (i, 0))
    ins = [dy, x, rstd, g] + ([res] if has_res else [])
    in_specs = [row, row, pl.BlockSpec((tr, 1), lambda i: (i, 0)),
                pl.BlockSpec((1, W), lambda i: (0, 0))] + ([row] if has_res else [])
    ins += list(deps)
    in_specs += [_ANY] * len(deps)
    return _pallas(
        body, name=name, grid=(S // tr,), in_specs=in_specs,
        out_specs=[row, pl.BlockSpec((1, W), lambda i: (0, 0))] + [row] * bf16_copy,
        out_shape=[jax.ShapeDtypeStruct((S, W), F32), jax.ShapeDtypeStruct((1, W), F32)]
        + [jax.ShapeDtypeStruct((S, W), BF16)] * bf16_copy,
        compiler_params=_params(("arbitrary",)),
    )(*ins)


def _norm_pair_fwd(proj, g_a, g_b, off, *, tr=512):
    S = proj.shape[0]
    wa_, wb_ = g_a.shape[1], g_b.shape[1]
    W = wa_ + wb_
    tr = _tile(S, tr)

    def body(x_ref, ga_ref, gb_ref, ya_ref, yb_ref, ra_ref, rb_ref):
        for lo, hi, g_ref, y_ref, r_ref in ((0, wa_, ga_ref, ya_ref, ra_ref), (wa_, W, gb_ref, yb_ref, rb_ref)):
            xv = x_ref[:, lo:hi]
            rstd = lax.rsqrt(jnp.mean(xv * xv, axis=-1, keepdims=True) + EPS)
            y_ref[...] = (xv * rstd * g_ref[...]).astype(BF16)
            r_ref[...] = rstd

    one = pl.BlockSpec((tr, 1), lambda i: (i, 0))
    return _pallas(
        body, name="norm_qkv", grid=(S // tr,),
        in_specs=[pl.BlockSpec((tr, W), lambda i: (i, off // W)), pl.BlockSpec((1, wa_), lambda i: (0, 0)),
                  pl.BlockSpec((1, wb_), lambda i: (0, 0))],
        out_specs=[pl.BlockSpec((tr, wa_), lambda i: (i, 0)), pl.BlockSpec((tr, wb_), lambda i: (i, 0)), one, one],
        out_shape=[jax.ShapeDtypeStruct((S, wa_), BF16), jax.ShapeDtypeStruct((S, wb_), BF16),
                   jax.ShapeDtypeStruct((S, 1), F32), jax.ShapeDtypeStruct((S, 1), F32)],
        compiler_params=_params(("parallel",)),
    )(proj, g_a, g_b)


def _norm_pair_bwd(dy_a, dy_b, proj, r_a, r_b, g_a, g_b, dproj, off, *, tr=512):
    S = proj.shape[0]
    wa_, wb_ = g_a.shape[1], g_b.shape[1]
    W = wa_ + wb_
    tr = _tile(S, tr)

    def body(dya_ref, dyb_ref, x_ref, ra_ref, rb_ref, ga_ref, gb_ref, _, dx_ref, dga_ref, dgb_ref):
        first = pl.program_id(0) == 0
        for lo, hi, dy_ref, r_ref, g_ref, dg_ref in ((0, wa_, dya_ref, ra_ref, ga_ref, dga_ref),
                                                    (wa_, W, dyb_ref, rb_ref, gb_ref, dgb_ref)):
            rstd = r_ref[...]
            xhat = x_ref[:, lo:hi] * rstd
            dyv = dy_ref[...]
            dyg = dyv * g_ref[...]
            dx_ref[:, lo:hi] = (rstd * (dyg - xhat * jnp.mean(dyg * xhat, axis=-1, keepdims=True))).astype(dx_ref.dtype)
            part = jnp.sum(dyv * xhat, axis=0, keepdims=True)

            @pl.when(first)
            def _():
                dg_ref[...] = part

            @pl.when(jnp.logical_not(first))
            def _():
                dg_ref[...] += part

    one = pl.BlockSpec((tr, 1), lambda i: (i, 0))
    cols = pl.BlockSpec((tr, W), lambda i: (i, off // W))
    va, vb = pl.BlockSpec((1, wa_), lambda i: (0, 0)), pl.BlockSpec((1, wb_), lambda i: (0, 0))
    return _pallas(
        body, name="norm_qkv_bwd", grid=(S // tr,),
        in_specs=[pl.BlockSpec((tr, wa_), lambda i: (i, 0)), pl.BlockSpec((tr, wb_), lambda i: (i, 0)), cols,
                  one, one, va, vb, _ANY],
        out_specs=[cols, va, vb],
        out_shape=[jax.ShapeDtypeStruct(dproj.shape, dproj.dtype), jax.ShapeDtypeStruct((1, wa_), F32),
                   jax.ShapeDtypeStruct((1, wb_), F32)],
        input_output_aliases={7: 0},
        compiler_params=_params(("arbitrary",)),
    )(dy_a, dy_b, proj, r_a, r_b, g_a, g_b, dproj)


def _final_loss(h2, g, target, *, tr=256):
    S, D = h2.shape
    tr = _tile(S, tr)

    def body(h_ref, g_ref, t_ref, loss_ref, dh_ref, dhb_ref, dg_ref):
        hv = h_ref[...]
        rstd = lax.rsqrt(jnp.mean(hv * hv, axis=-1, keepdims=True) + EPS)
        xhat = hv * rstd
        e = xhat * g_ref[...] - t_ref[...]
        lpart = (0.5 / D) * jnp.sum(jnp.sum(e * e, axis=-1, keepdims=True), axis=0, keepdims=True)
        dy = e * (1.0 / D)
        dyg = dy * g_ref[...]
        dh = rstd * (dyg - xhat * jnp.mean(dyg * xhat, axis=-1, keepdims=True))
        dh_ref[...] = dh
        dhb_ref[...] = dh.astype(BF16)
        gpart = jnp.sum(dy * xhat, axis=0, keepdims=True)

        @pl.when(pl.program_id(0) == 0)
        def _():
            loss_ref[...] = lpart
            dg_ref[...] = gpart

        @pl.when(pl.program_id(0) > 0)
        def _():
            loss_ref[...] += lpart
            dg_ref[...] += gpart

    row = pl.BlockSpec((tr, D), lambda i: (i, 0))
    vec = pl.BlockSpec((1, D), lambda i: (0, 0))
    return _pallas(
        body, name="final_loss", grid=(S // tr,), in_specs=[row, vec, row],
        out_specs=[pl.BlockSpec((1, 1), lambda i: (0, 0)), row, row, vec],
        out_shape=[jax.ShapeDtypeStruct((1, 1), F32), jax.ShapeDtypeStruct((S, D), F32),
                   jax.ShapeDtypeStruct((S, D), BF16), jax.ShapeDtypeStruct((1, D), F32)],
        compiler_params=_params(("arbitrary",)),
    )(h2, g, target)


def _sigmoid(v):
    return 1.0 / (1.0 + jnp.exp(-v))


def _merge_bwd(dmerged, proj, y_ret, y_mla, D, off_gret, *, tr=256):
    S = y_ret.shape[0]
    tr = _tile(S, tr)
    b0 = off_gret // D

    def body(dm_ref, g_ref, yr_ref, ym_ref, dp_ref, dyr_ref, dym_ref):
        dm = dm_ref[...]
        sg = _sigmoid(g_ref[...])

        @pl.when(pl.program_id(1) == 0)
        def _():
            dyr_ref[...] = (dm * sg).astype(BF16)
            dp_ref[...] = (dm * yr_ref[...] * sg * (1.0 - sg)).astype(BF16)

        @pl.when(pl.program_id(1) == 1)
        def _():
            dym_ref[...] = (dm * sg).astype(BF16)
            dp_ref[...] = (dm * ym_ref[...] * sg * (1.0 - sg)).astype(BF16)

    blk = pl.BlockSpec((tr, D), lambda i, j: (i, 0))
    return _pallas(
        body, name="merge_bwd", grid=(S // tr, 2),
        in_specs=[blk, pl.BlockSpec((tr, D), lambda i, j: (i, b0 + j)), blk, blk],
        out_specs=[pl.BlockSpec((tr, D), lambda i, j: (i, b0 + j)), blk, blk],
        out_shape=[jax.ShapeDtypeStruct(proj.shape, BF16), jax.ShapeDtypeStruct((S, D), BF16),
                   jax.ShapeDtypeStruct((S, D), BF16)],
        compiler_params=_params(("parallel", "arbitrary")),
    )(dmerged, proj, y_ret, y_mla)


def _rope128(t, cos_full, sin_signed):
    return t * cos_full + pltpu.roll(t, RET_QK // 2, 1) * sin_signed


def _rope128_t(d, cos_full, sin_signed):
    return d * cos_full + pltpu.roll(d * sin_signed, RET_QK // 2, 1)


def _ret_consts(lg, T):
    pos = lax.broadcasted_iota(jnp.int32, (T, 1), 0).astype(F32)
    qd = jnp.exp(lg * (pos + 1.0))
    kd = jnp.exp(lg * (T - 1.0 - pos))
    n = lax.broadcasted_iota(jnp.int32, (T, T), 0)
    m = lax.broadcasted_iota(jnp.int32, (T, T), 1)
    vis = (m // CHUNK) <= (n // CHUNK)
    dist = jnp.abs(n - m).astype(F32)
    decay = jnp.where(vis, jnp.exp(lg * dist), 0.0)
    cdec = jnp.exp(lg * float(T))
    return qd, kd, decay, cdec


def _dot(a, b, dims):
    return lax.dot_general(a.astype(BF16), b.astype(BF16), (dims, ((), ())), preferred_element_type=F32)


NN = ((1,), (0,))
NT = ((1,), (1,))
TN = ((0,), (0,))
_RQ = slice(0, RET_QK)
_RK = slice(RET_QK, 2 * RET_QK)
_RV = slice(2 * RET_QK, 2 * RET_QK + RET_V)
_RG = slice(2 * RET_QK + RET_V, RET_HEAD_COLS)


RET_GROUP = 8


def _head_cols(h, part):
    return slice(h * RET_HEAD_COLS + part.start, h * RET_HEAD_COLS + part.stop)


def _ret_fwd(proj, cosr, sinr, lgam, gain, RH, *, T):
    S = proj.shape[0]
    nb = S // T
    G = _tile(RH, RET_GROUP)
    heads = range(G)
    scale = RET_QK ** -0.5

    def body(p_ref, cos_ref, sin_ref, lg_ref, gain_ref, ry_ref, gated_ref, st_ref, state):
        b = pl.program_id(1)

        @pl.when(b == 0)
        def _():
            state[...] = jnp.zeros_like(state)

        consts = [_ret_consts(lg_ref[h, 0:1, 0:1], T) for h in heads]
        cosv, sinv = cos_ref[...], sin_ref[...]
        q = [_rope128(p_ref[:, _head_cols(h, _RQ)], cosv, sinv) for h in heads]
        k = [_rope128(p_ref[:, _head_cols(h, _RK)], cosv, sinv) * scale for h in heads]
        v = [p_ref[:, _head_cols(h, _RV)] for h in heads]
        sprev = [state[h] for h in heads]
        for h in heads:
            st_ref[h] = sprev[h]
        a = [_dot(q[h], k[h], NT) for h in heads]
        qs = [_dot(q[h] * consts[h][0], sprev[h], NN) for h in heads]
        kv = [_dot(k[h] * consts[h][1], v[h], TN) for h in heads]
        o = [_dot(a[h] * consts[h][2], v[h], NN) + qs[h] for h in heads]
        for h in heads:
            state[h] = sprev[h] * consts[h][3] + kv[h]
            vals = slice(h * RET_V, (h + 1) * RET_V)
            ry_ref[:, vals] = o[h]
            mu = jnp.mean(o[h], axis=-1, keepdims=True)
            oc = o[h] - mu
            var = jnp.mean(oc * oc, axis=-1, keepdims=True)
            t = oc * lax.rsqrt(var + EPS) * gain_ref[:, vals]
            gv = p_ref[:, _head_cols(h, _RG)]
            gated_ref[:, vals] = (t * (gv * _sigmoid(gv))).astype(BF16)

    return _pallas(
        body, name="ret_fwd", grid=(RH // G, nb),
        in_specs=[pl.BlockSpec((T, G * RET_HEAD_COLS), lambda h, b: (b, h)),
                  pl.BlockSpec((T, RET_QK), lambda h, b: (b, 0)),
                  pl.BlockSpec((T, RET_QK), lambda h, b: (b, 0)),
                  pl.BlockSpec((G, 8, LANES), lambda h, b: (h, 0, 0)),
                  pl.BlockSpec((1, G * RET_V), lambda h, b: (0, h))],
        out_specs=[pl.BlockSpec((T, G * RET_V), lambda h, b: (b, h)),
                   pl.BlockSpec((T, G * RET_V), lambda h, b: (b, h)),
                   pl.BlockSpec((G, None, RET_QK, RET_V), lambda h, b: (h, b, 0, 0))],
        out_shape=[jax.ShapeDtypeStruct((S, RH * RET_V), F32), jax.ShapeDtypeStruct((S, RH * RET_V), BF16),
                   jax.ShapeDtypeStruct((RH, nb, RET_QK, RET_V), F32)],
        scratch_shapes=[pltpu.VMEM((G, RET_QK, RET_V), F32)],
        compiler_params=_params(("parallel", "arbitrary")),
    )(proj, cosr, sinr, lgam, gain)


def _ret_bwd(proj, cosr, sinr, lgam, gain, ry, dgated, states, dproj, RH, *, T):
    S = proj.shape[0]
    nb = S // T
    G = _tile(RH, RET_GROUP)
    heads = range(G)
    scale = RET_QK ** -0.5

    def body(p_ref, cos_ref, sin_ref, lg_ref, gain_ref, ry_ref, dg_ref, st_ref, _, dp_ref, dgain_ref, dstate):
        b = pl.program_id(1)

        @pl.when(b == 0)
        def _():
            dstate[...] = jnp.zeros_like(dstate)

        consts = [_ret_consts(lg_ref[h, 0:1, 0:1], T) for h in heads]
        qd, kd, decay, cdec = [[c[i] for c in consts] for i in range(4)]
        cosv, sinv = cos_ref[...], sin_ref[...]
        q = [_rope128(p_ref[:, _head_cols(h, _RQ)], cosv, sinv) for h in heads]
        k = [_rope128(p_ref[:, _head_cols(h, _RK)], cosv, sinv) * scale for h in heads]
        v = [p_ref[:, _head_cols(h, _RV)] for h in heads]
        sprev = [st_ref[h] for h in heads]
        ds_new = [dstate[h] for h in heads]
        a = [_dot(q[h], k[h], NT) for h in heads]
        do, gparts = [], []
        for h in heads:
            vals = slice(h * RET_V, (h + 1) * RET_V)
            o = ry_ref[:, vals]
            mu = jnp.mean(o, axis=-1, keepdims=True)
            oc = o - mu
            rstd = lax.rsqrt(jnp.mean(oc * oc, axis=-1, keepdims=True) + EPS)
            ryn = oc * rstd
            gainv = gain_ref[:, vals]
            gv = p_ref[:, _head_cols(h, _RG)]
            sg = _sigmoid(gv)
            dgt = dg_ref[:, vals]
            dt = dgt * (gv * sg)
            dp_ref[:, _head_cols(h, _RG)] = (dgt * (ryn * gainv) * (sg * (1.0 + gv * (1.0 - sg)))).astype(BF16)
            gparts.append(jnp.sum(dt * ryn, axis=0, keepdims=True))
            dryn = dt * gainv
            do.append(rstd * (dryn - jnp.mean(dryn, axis=-1, keepdims=True)
                              - ryn * jnp.mean(dryn * ryn, axis=-1, keepdims=True)))
        gpart = jnp.concatenate(gparts, axis=1)

        @pl.when(b == 0)
        def _():
            dgain_ref[...] = gpart

        @pl.when(b > 0)
        def _():
            dgain_ref[...] += gpart

        dpm = [_dot(do[h], v[h], NT) for h in heads]
        dq_s = [_dot(do[h], sprev[h], NT) for h in heads]
        dk_s = [_dot(v[h], ds_new[h], NT) for h in heads]
        dv_s = [_dot(k[h] * kd[h], ds_new[h], NN) for h in heads]
        dst = [_dot(q[h] * qd[h], do[h], TN) for h in heads]
        a = [a[h] * decay[h] for h in heads]
        dpm = [dpm[h] * decay[h] for h in heads]
        dv = [_dot(a[h], do[h], TN) + dv_s[h] for h in heads]
        dq = [_dot(dpm[h], k[h], NN) + dq_s[h] * qd[h] for h in heads]
        dk = [(_dot(dpm[h], q[h], TN) + dk_s[h] * kd[h]) * scale for h in heads]
        for h in heads:
            dstate[h] = ds_new[h] * cdec[h] + dst[h]
            dp_ref[:, _head_cols(h, _RV)] = dv[h].astype(BF16)
            dp_ref[:, _head_cols(h, _RQ)] = _rope128_t(dq[h], cosv, sinv).astype(BF16)
            dp_ref[:, _head_cols(h, _RK)] = _rope128_t(dk[h], cosv, sinv).astype(BF16)

    rb = lambda b: nb - 1 - b
    return _pallas(
        body, name="ret_bwd", grid=(RH // G, nb),
        in_specs=[pl.BlockSpec((T, G * RET_HEAD_COLS), lambda h, b: (rb(b), h)),
                  pl.BlockSpec((T, RET_QK), lambda h, b: (rb(b), 0)),
                  pl.BlockSpec((T, RET_QK), lambda h, b: (rb(b), 0)),
                  pl.BlockSpec((G, 8, LANES), lambda h, b: (h, 0, 0)),
                  pl.BlockSpec((1, G * RET_V), lambda h, b: (0, h)),
                  pl.BlockSpec((T, G * RET_V), lambda h, b: (rb(b), h)),
                  pl.BlockSpec((T, G * RET_V), lambda h, b: (rb(b), h)),
                  pl.BlockSpec((G, None, RET_QK, RET_V), lambda h, b: (h, rb(b), 0, 0)),
                  _ANY],
        out_specs=[pl.BlockSpec((T, G * RET_HEAD_COLS), lambda h, b: (rb(b), h)),
                   pl.BlockSpec((1, G * RET_V), lambda h, b: (0, h))],
        out_shape=[jax.ShapeDtypeStruct(dproj.shape, dproj.dtype), jax.ShapeDtypeStruct((1, RH * RET_V), F32)],
        scratch_shapes=[pltpu.VMEM((G, RET_QK, RET_V), F32)],
        input_output_aliases={8: 0},
        compiler_params=_params(("parallel", "arbitrary")),
    )(proj, cosr, sinr, lgam, gain, ry, dgated, states, dproj)


def _rope_pe(t, c, s1, s2):
    return t * c + pltpu.roll(t, LANES - QK_ROPE // 2, 1) * s1 + pltpu.roll(t, QK_ROPE // 2, 1) * s2


def _rope_pe_t(d, c, s1, s2):
    return d * c + pltpu.roll(d * s1, QK_ROPE // 2, 1) + pltpu.roll(d * s2, LANES - QK_ROPE // 2, 1)


ATTN_C2 = (QK_NOPE + QK_ROPE) ** -0.5 * LOG2E


def _qkv_proj(cqn, ckvn, wq, wkv, kpe, tabs, MH, *, tm=512, heads=4):
    S = cqn.shape[0]
    tm = _tile(S, tm)
    hb = _tile(MH, heads)
    W = 2 * LANES
    c_t, s1_t, s2_t = tabs

    def body(cq_ref, ckv_ref, wq_ref, wkv_ref, kpe_ref, c_ref, s1_ref, s2_ref, qf_ref, kf_ref, v_ref):
        c, s1, s2 = c_ref[...], s1_ref[...], s2_ref[...]
        q = _dot(cq_ref[...], wq_ref[...], NN)
        kv = _dot(ckv_ref[...], wkv_ref[...], NN)
        kper = _rope_pe(kpe_ref[...], c, s1, s2).astype(BF16)
        for h in range(hb):
            lo, mid, hi = h * W, h * W + QK_NOPE, (h + 1) * W
            qf_ref[:, lo:mid] = (q[:, lo:mid] * ATTN_C2).astype(BF16)
            qf_ref[:, mid:hi] = (_rope_pe(q[:, mid:hi], c, s1, s2) * ATTN_C2).astype(BF16)
            kf_ref[:, lo:mid] = kv[:, lo:mid].astype(BF16)
            kf_ref[:, mid:hi] = kper
            v_ref[:, h * V_HEAD:(h + 1) * V_HEAD] = kv[:, mid:hi].astype(BF16)

    tab = pl.BlockSpec((tm, LANES), lambda i, j: (i, 0))
    grp = pl.BlockSpec((tm, hb * W), lambda i, j: (i, j))
    return _pallas(
        body, name="qkv_proj", grid=(S // tm, MH // hb),
        in_specs=[pl.BlockSpec((tm, cqn.shape[1]), lambda i, j: (i, 0)),
                  pl.BlockSpec((tm, ckvn.shape[1]), lambda i, j: (i, 0)),
                  pl.BlockSpec((wq.shape[0], hb * W), lambda i, j: (0, j)),
                  pl.BlockSpec((wkv.shape[0], hb * W), lambda i, j: (0, j)), tab, tab, tab, tab],
        out_specs=[grp, grp, pl.BlockSpec((tm, hb * V_HEAD), lambda i, j: (i, j))],
        out_shape=[jax.ShapeDtypeStruct((S, MH * W), BF16)] * 2 + [jax.ShapeDtypeStruct((S, MH * V_HEAD), BF16)],
        compiler_params=_params(("parallel", "parallel")),
    )(cqn, ckvn, wq, wkv, kpe, c_t, s1_t, s2_t)


def _chunk_mask(T):
    n = lax.broadcasted_iota(jnp.int32, (T, T), 0)
    m = lax.broadcasted_iota(jnp.int32, (T, T), 1)
    return (m // CHUNK) <= (n // CHUNK)


def _lanes_to(v, width):
    return jnp.tile(v, (1, width // LANES))


def _attn_fwd(qf, kf, vb, MH, *, T, heads, name, prev=(), deps=()):
    S = qf.shape[0]
    nt = S // T
    n_skip = len(prev) + len(deps)

    def body(q_ref, k_ref, v_ref, *rest):
        o_ref, ob_ref, lse_ref, m_sc, l_sc, acc_sc, s_a, s_b = rest[n_skip:]
        qi = pl.program_id(1)
        m_sc[...] = jnp.full_like(m_sc, NEG)
        l_sc[...] = jnp.zeros_like(l_sc)
        acc_sc[...] = jnp.zeros_like(acc_sc)

        def rows_of(kt):
            return pl.ds(pl.multiple_of(kt * T, T), T)

        def scores(kt):
            return _dot(q_ref[...], k_ref[rows_of(kt), :], NT)

        def update(s, kt):
            m_prev = m_sc[...]
            m_new = jnp.maximum(m_prev, jnp.max(s, axis=-1, keepdims=True))
            alpha = jnp.exp2(m_prev - m_new)
            p = jnp.exp2(s - _lanes_to(m_new, T))
            l_sc[...] = alpha * l_sc[...] + jnp.sum(p, axis=-1, keepdims=True)
            acc_sc[...] = alpha * acc_sc[...] + _dot(p, v_ref[rows_of(kt), :], NN)
            m_sc[...] = m_new

        def masked(s):
            return jnp.where(_chunk_mask(T), s, NEG)

        @pl.when(qi == 0)
        def _():
            update(masked(scores(0)), 0)

        @pl.when(qi > 0)
        def _():
            s_a[...] = masked(scores(qi))
            s_b[...] = scores(0)
            update(s_a[...], qi)
            s_a[...] = scores(jnp.minimum(1, qi - 1))
            update(s_b[...], 0)

            def pair(j, carry):
                s_b[...] = scores(2 * j)
                update(s_a[...], 2 * j - 1)
                s_a[...] = scores(jnp.minimum(2 * j + 1, qi - 1))
                update(s_b[...], 2 * j)
                return carry

            lax.fori_loop(1, (qi + 1) // 2, pair, 0)

            @pl.when(qi % 2 == 0)
            def _():
                update(s_a[...], qi - 1)
        l = l_sc[...]
        o = acc_sc[...] / l
        o_ref[...] = o
        ob_ref[...] = o.astype(BF16)
        lse_ref[...] = m_sc[...] + jnp.log(l) * LOG2E

    h0, h1 = heads
    out_shape = [jax.ShapeDtypeStruct((S, MH * LANES), F32), jax.ShapeDtypeStruct((S, MH * LANES), BF16),
                 jax.ShapeDtypeStruct((MH, S, LANES), F32)]
    row = pl.BlockSpec((T, LANES), lambda h, i: (i, h0 + h))
    return _pallas(
        body, name=name, grid=(h1 - h0, nt),
        in_specs=[pl.BlockSpec((T, 2 * LANES), lambda h, i: (i, h0 + h)),
                  pl.BlockSpec((S, 2 * LANES), lambda h, i: (0, h0 + h)),
                  pl.BlockSpec((S, LANES), lambda h, i: (0, h0 + h))] + [_ANY] * (len(prev) + len(deps)),
        out_specs=[row, row, pl.BlockSpec((None, T, LANES), lambda h, i: (h0 + h, i, 0))],
        out_shape=out_shape,
        scratch_shapes=[pltpu.VMEM((T, LANES), F32), pltpu.VMEM((T, LANES), F32), pltpu.VMEM((T, LANES), F32),
                        pltpu.VMEM((T, T), F32), pltpu.VMEM((T, T), F32)],
        input_output_aliases={3 + i: i for i in range(len(prev))},
        compiler_params=_params(("parallel", "parallel")),
    )(qf, kf, vb, *prev, *deps)


def _attn_bwd(qf, kf, vb, dob, lse2, delta, tabs, MH, *, T, deps=()):
    S = qf.shape[0]
    nt = S // T
    scale = (QK_NOPE + QK_ROPE) ** -0.5
    n_dep = len(deps)

    def body(q_ref, k_ref, v_ref, do_ref, lse_ref, dl_ref, c_ref, s1_ref, s2_ref, *rest):
        dqa_ref, dkv_ref, dkpe_ref, dq_ref, dk_sc, dv_sc, s_a, dp_a, s_b, dp_b = rest[n_dep:]
        kj = pl.program_id(1)

        @pl.when(kj == 0)
        def _():
            dq_ref[...] = jnp.zeros_like(dq_ref)

        dk_sc[...] = jnp.zeros_like(dk_sc)
        dv_sc[...] = jnp.zeros_like(dv_sc)

        def rows_of(qt):
            return pl.ds(pl.multiple_of(qt * T, T), T)

        def products(qt):
            rows = rows_of(qt)
            return _dot(q_ref[rows, :], k_ref[...], NT), _dot(do_ref[rows, :], v_ref[...], NT)

        def update(s, dp, qt):
            rows = rows_of(qt)
            q, dov = q_ref[rows, :], do_ref[rows, :]
            p = jnp.exp2(s - _lanes_to(lse_ref[rows, :], T))
            ds = p * (dp - _lanes_to(dl_ref[rows, :], T))
            dv_sc[...] += _dot(p, dov, TN)
            dk_sc[...] += _dot(ds, q, TN)
            dq_ref[rows, :] += _dot(ds, k_ref[...], NN)

        def masked(s):
            return jnp.where(_chunk_mask(T), s, NEG)

        @pl.when(kj == nt - 1)
        def _():
            s, dp = products(kj)
            update(masked(s), dp, kj)

        @pl.when(kj < nt - 1)
        def _():
            s, dp = products(kj)
            s_a[...], dp_a[...] = masked(s), dp
            s_b[...], dp_b[...] = products(kj + 1)
            update(s_a[...], dp_a[...], kj)
            s_a[...], dp_a[...] = products(jnp.minimum(kj + 2, nt - 1))
            update(s_b[...], dp_b[...], kj + 1)

            def pair(j, carry):
                t0 = kj + 2 * j
                s_b[...], dp_b[...] = products(t0 + 1)
                update(s_a[...], dp_a[...], t0)
                s_a[...], dp_a[...] = products(jnp.minimum(t0 + 2, nt - 1))
                update(s_b[...], dp_b[...], t0 + 1)
                return carry

            lax.fori_loop(1, (nt - kj) // 2, pair, 0)

            @pl.when((nt - kj) % 2 == 1)
            def _():
                update(s_a[...], dp_a[...], nt - 1)
        dkv_ref[:, :QK_NOPE] = (dk_sc[:, :QK_NOPE] * (1.0 / LOG2E)).astype(BF16)
        dkv_ref[:, QK_NOPE:] = dv_sc[...].astype(BF16)
        dkpe_ref[...] = dk_sc[:, QK_NOPE:] * (1.0 / LOG2E)

        @pl.when(kj == nt - 1)
        def _():
            dqa_ref[:, :QK_NOPE] = (dq_ref[:, :QK_NOPE] * scale).astype(BF16)
            dqa_ref[:, QK_NOPE:] = (_rope_pe_t(dq_ref[:, QK_NOPE:], c_ref[...], s1_ref[...], s2_ref[...])
                                    * scale).astype(BF16)

    stat = pl.BlockSpec((None, S, LANES), lambda h, j: (h, 0, 0))
    tab = pl.BlockSpec((S, LANES), lambda h, j: (0, 0))
    return _pallas(
        body, name="attn_bwd", grid=(MH, nt),
        in_specs=[pl.BlockSpec((S, 2 * LANES), lambda h, j: (0, h)),
                  pl.BlockSpec((T, 2 * LANES), lambda h, j: (j, h)),
                  pl.BlockSpec((T, LANES), lambda h, j: (j, h)),
                  pl.BlockSpec((S, LANES), lambda h, j: (0, h)), stat, stat, tab, tab, tab] + [_ANY] * n_dep,
        out_specs=[pl.BlockSpec((S, 2 * LANES), lambda h, j: (0, h)),
                   pl.BlockSpec((T, 2 * LANES), lambda h, j: (j, h)),
                   pl.BlockSpec((T, LANES), lambda h, j: (j, h))],
        out_shape=[jax.ShapeDtypeStruct((S, MH * 2 * LANES), BF16), jax.ShapeDtypeStruct((S, MH * 2 * LANES), BF16),
                   jax.ShapeDtypeStruct((S, MH * LANES), F32)],
        scratch_shapes=[pltpu.VMEM((S, 2 * LANES), F32), pltpu.VMEM((T, 2 * LANES), F32), pltpu.VMEM((T, LANES), F32)]
        + [pltpu.VMEM((T, T), F32)] * 4,
        compiler_params=_params(("parallel", "arbitrary")),
    )(qf, kf, vb, dob, lse2, delta, *tabs, *deps)


def _kpe_sum(dkpe_h, tabs, MH, *, tr=256):
    S = dkpe_h.shape[0]
    tr = _tile(S, tr)

    def body(dk_ref, c_ref, s1_ref, s2_ref, dkpe_ref):
        tot = dk_ref[:, :LANES]
        for h in range(1, MH):
            tot = tot + dk_ref[:, h * LANES:(h + 1) * LANES]
        dkpe_ref[...] = _rope_pe_t(tot, c_ref[...], s1_ref[...], s2_ref[...]).astype(BF16)

    tab = pl.BlockSpec((tr, LANES), lambda i: (i, 0))
    return _pallas(
        body, name="kpe_sum", grid=(S // tr,),
        in_specs=[pl.BlockSpec((tr, MH * LANES), lambda i: (i, 0)), tab, tab, tab],
        out_specs=tab, out_shape=jax.ShapeDtypeStruct((S, LANES), BF16),
        compiler_params=_params(("parallel",)),
    )(dkpe_h, *tabs)


ROW_ALIGN = 16


def _blk(R, C, block_bytes=2 << 20):
    cap = max(ROW_ALIGN, block_bytes // (C * 4))
    for t in range(min(R, cap) // ROW_ALIGN * ROW_ALIGN, LANES - 1, -ROW_ALIGN):
        if R % t == 0:
            return t, C
    if R <= cap:
        return R, C
    tc = C
    while R * tc * 4 > block_bytes and tc % (2 * LANES) == 0:
        tc //= 2
    return R, tc


def _rows_call(fn, ins, out_dtypes, *, name, deps=()):
    R, C = ins[0].shape
    tr, tc = _blk(R, C)
    n_in, n_dep = len(ins), len(deps)

    def body(*refs):
        vals = fn(*[r[...] for r in refs[:n_in]])
        for r, v in zip(refs[n_in + n_dep:], vals):
            r[...] = v.astype(r.dtype)

    blk = pl.BlockSpec((tr, tc), lambda i, j: (i, j))
    res = _pallas(
        body, name=name, grid=(R // tr, C // tc), in_specs=[blk] * n_in + [_ANY] * n_dep,
        out_specs=[blk] * len(out_dtypes),
        out_shape=[jax.ShapeDtypeStruct((R, C), d) for d in out_dtypes],
        compiler_params=_params(("parallel", "parallel")),
    )(*ins, *deps)
    return res


def _adamw_vals(w, g, m, v):
    m = ADAM_B1 * m + (1.0 - ADAM_B1) * g
    v = ADAM_B2 * v + (1.0 - ADAM_B2) * (g * g)
    m_hat = m / (1.0 - ADAM_B1 ** ADAM_STEP)
    v_hat = v / (1.0 - ADAM_B2 ** ADAM_STEP)
    delta = -ADAM_LR * (m_hat / (jnp.sqrt(v_hat) + ADAM_EPS) + ADAM_WD * w)
    return delta, m, v


def _sum_pair(p, theirs, place, *, name):
    _, R, C = p.shape
    R2 = R // 2
    tr, tc = _blk(R2, C)
    p4 = p.reshape(N_CHIPS, 2, R2, C)

    def body(place_ref, a_ref, b_ref, o_ref):
        o_ref[...] = (a_ref[...].astype(F32) + b_ref[...].astype(F32)).astype(BF16)

    spec = pltpu.PrefetchScalarGridSpec(
        num_scalar_prefetch=1, grid=(N_CHIPS, R2 // tr, C // tc),
        in_specs=[pl.BlockSpec((None, None, tr, tc), lambda q, i, j, pr: (q, pr[0], i, j)),
                  pl.BlockSpec((None, tr, tc), lambda q, i, j, pr: (q, i, j))],
        out_specs=pl.BlockSpec((None, tr, tc), lambda q, i, j, pr: (q, i, j)))
    return _pallas(body, name=name, grid_spec=spec, out_shape=jax.ShapeDtypeStruct((N_CHIPS, R2, C), BF16),
                   compiler_params=_params(("parallel", "parallel", "parallel")))(place, p4, theirs)


def _sum_chips(p, theirs, recv, place, *, name):
    _, R, C = p.shape
    R2 = R // 2
    tr, tc = _blk(R2, C)
    p4 = p.reshape(N_CHIPS, 2, R2, C)

    def body(place_ref, a_ref, b_ref, r0_ref, r1_ref, r2_ref, o_ref):
        own = a_ref[...].astype(F32) + b_ref[...].astype(F32)
        o_ref[...] = ((own + r0_ref[...].astype(F32)) + r1_ref[...].astype(F32)) + r2_ref[...].astype(F32)

    def slot(k):
        return pl.BlockSpec((None, tr, tc), lambda i, j, pr: (k, i, j))

    spec = pltpu.PrefetchScalarGridSpec(
        num_scalar_prefetch=1, grid=(R2 // tr, C // tc),
        in_specs=[pl.BlockSpec((None, None, tr, tc), lambda i, j, pr: (pr[1], pr[0], i, j)),
                  pl.BlockSpec((None, tr, tc), lambda i, j, pr: (pr[1], i, j)), slot(0), slot(1), slot(2)],
        out_specs=pl.BlockSpec((None, tr, tc), lambda i, j, pr: (pr[0], i, j)))
    return _pallas(body, name=name, grid_spec=spec, out_shape=jax.ShapeDtypeStruct((2, R2, C), F32),
                   compiler_params=_params(("parallel", "parallel")))(place, p4, theirs, recv, recv, recv)


def _me():
    return lax.axis_index("x"), lax.axis_index("y"), lax.axis_index("c")


def _other_chips(x, y):
    return [(1 - x, y), (x, 1 - y), (1 - x, 1 - y)]


def _rcopy(src, dst, ssem, rsem, dev):
    return pltpu.make_async_remote_copy(src_ref=src, dst_ref=dst, send_sem=ssem, recv_sem=rsem,
                                        device_id=dev, device_id_type=MESH)


def _cast_into_slot(w, place, *, name, rows=None, deps=()):
    R, C = w.shape
    rows = R if rows is None else rows
    tr, tc = _blk(R, C)

    def body(place_ref, w_ref, *rest):
        rest[-1][...] = w_ref[...].astype(BF16)

    spec = pltpu.PrefetchScalarGridSpec(
        num_scalar_prefetch=1, grid=(R // tr, C // tc),
        in_specs=[pl.BlockSpec((tr, tc), lambda i, j, pr: (i, j))] + [_ANY] * len(deps),
        out_specs=pl.BlockSpec((None, tr, tc), lambda i, j, pr: (pr[1], i, j)))
    out = _pallas(body, name=name, grid_spec=spec, out_shape=jax.ShapeDtypeStruct((N_CHIPS, rows, C), BF16),
                  compiler_params=_params(("parallel", "parallel")))(place, w, *deps)
    return out.reshape(N_CHIPS, 2, rows // 2, C)


def _gather_ici_plan(bufs):
    x, y, c = _me()
    j = 2 * x + y
    plan = []
    for i, buf in enumerate(bufs):
        for k, (px, py) in enumerate(_other_chips(x, y)):
            plan.append((3 * i + k, buf.at[j, c], buf.at[j, c], (px, py, c)))
    return plan


def _forward_halves(bufs, *, name):
    n = len(bufs)

    def body(*refs):
        outs = refs[n:2 * n]
        ssem, rsem = refs[2 * n:]
        x, y, c = _me()
        sib = (x, y, 1 - c)
        cps = []
        for i in range(n):
            for k, (px, py) in enumerate(_other_chips(x, y)):
                slot = outs[i].at[2 * px + py, c]
                r = _rcopy(slot, slot, ssem.at[3 * i + k], rsem.at[3 * i + k], sib)
                r.start()
                cps.append(r)
        for r in cps:
            r.wait()

    return _pallas(
        body, name=name, in_specs=[_ANY] * n, out_specs=[_ANY] * n,
        out_shape=[jax.ShapeDtypeStruct(b.shape, b.dtype) for b in bufs],
        scratch_shapes=[pltpu.SemaphoreType.DMA((3 * n,))] * 2,
        input_output_aliases={i: i for i in range(n)},
        compiler_params=pltpu.CompilerParams(has_side_effects=True),
    )(*bufs)


_HBM = pl.BlockSpec(memory_space=pltpu.HBM)
_SEM = pl.BlockSpec(memory_space=pltpu.SEMAPHORE)
_EFFECT = pltpu.SideEffectType.DATAFLOW_SIDE_EFFECTING


def _split_start(bufs, plan, n_copies, *, name):
    n = len(bufs)

    def body(*refs):
        ssem, rsem = refs[n], refs[n + 1]
        for s, src, dst, dev in plan(refs[:n]):
            _rcopy(src, dst, ssem.at[s], rsem.at[s], dev).start()
        refs[-1][...] = jnp.zeros_like(refs[-1])

    res = _pallas(
        body, name=name, in_specs=[_HBM] * n,
        out_specs=(_SEM, _SEM, *[_HBM] * n, pl.BlockSpec(memory_space=pltpu.VMEM)),
        out_shape=(pltpu.SemaphoreType.DMA((n_copies,)), pltpu.SemaphoreType.DMA((n_copies,)),
                   *[pltpu.HBM(b.shape, b.dtype) for b in bufs], jax.ShapeDtypeStruct((8, LANES), F32)),
        input_output_aliases={i: 2 + i for i in range(n)},
        compiler_params=pltpu.CompilerParams(has_side_effects=_EFFECT),
    )(*[pltpu.with_memory_space_constraint(b, pltpu.HBM) for b in bufs])
    return res[0], res[1], list(res[2:2 + n]), res[-1]


def _split_wait(ssem, rsem, bufs, after, plan, *, name):
    n = len(bufs)

    def body(*refs):
        ssem_ref, rsem_ref = refs[n], refs[n + 1]
        for s, src, dst, dev in plan(refs[:n]):
            cp = _rcopy(src, dst, ssem_ref.at[s], rsem_ref.at[s], dev)
            cp.wait_send()
            cp.wait_recv()

    return list(_pallas(
        body, name=name, in_specs=[_HBM] * n + [_SEM, _SEM, _ANY], out_specs=[_HBM] * n,
        out_shape=[pltpu.HBM(b.shape, b.dtype) for b in bufs],
        input_output_aliases={i: i for i in range(n)},
        compiler_params=pltpu.CompilerParams(has_side_effects=_EFFECT),
    )(*bufs, ssem, rsem, after))


def _forward_plan(bufs):
    x, y, c = _me()
    plan = []
    for i, buf in enumerate(bufs):
        for k, (px, py) in enumerate(_other_chips(x, y)):
            plan.append((3 * i + k, buf.at[2 * px + py, c], buf.at[2 * px + py, c], (x, y, 1 - c)))
    return plan


def _join_plan(bufs):
    x, y, c = _me()
    return [(i, buf.at[c], buf.at[c], (x, y, 1 - c)) for i, buf in enumerate(bufs)]


def _swap_plan(n):
    def plan(bufs):
        x, y, c = _me()
        return [(i, bufs[i].at[:, 1 - c], bufs[n + i], (x, y, 1 - c)) for i in range(n)]
    return plan


def _scatter_plan(n):
    def plan(bufs):
        x, y, c = _me()
        out = []
        for i in range(n):
            for k, (px, py) in enumerate(_other_chips(x, y)):
                out.append((3 * i + k, bufs[i].at[2 * px + py], bufs[n + i].at[k], (px, py, c)))
        return out
    return plan


def _allreduce_small(parts, loss11):
    n = len(parts)
    widths = [p.shape[1] for p in parts]
    total = sum(widths) + LANES

    def body(*refs):
        o_ref, mine, buf, ssem, rsem = refs[n + 1:]
        x, y, c = _me()
        me = 4 * x + 2 * y + c
        off = 0
        for r, w in zip(refs[:n], widths):
            mine[:, off:off + w] = r[...]
            off += w
        mine[:, off:] = jnp.broadcast_to(refs[n][...], (1, LANES))
        buf[me] = mine[...]
        cps = []
        for k in range(1, 8):
            peer = (x ^ (k >> 2), y ^ ((k >> 1) & 1), c ^ (k & 1))
            r = _rcopy(mine, buf.at[me], ssem.at[k - 1], rsem.at[k - 1], peer)
            r.start()
            cps.append(r)
        for k in range(1, 8):
            peer = (x ^ (k >> 2), y ^ ((k >> 1) & 1), c ^ (k & 1))
            pid = 4 * peer[0] + 2 * peer[1] + peer[2]
            _rcopy(mine, buf.at[pid], ssem.at[k - 1], rsem.at[k - 1], peer).wait_recv()
        for r in cps:
            r.wait_send()
        tot = buf[0]
        for d in range(1, 8):
            tot = tot + buf[d]
        o_ref[...] = tot

    vm = pl.BlockSpec(memory_space=pltpu.VMEM)
    return _pallas(
        body, name="allreduce_small", in_specs=[vm] * (n + 1), out_specs=vm,
        out_shape=jax.ShapeDtypeStruct((1, total), F32),
        scratch_shapes=[pltpu.VMEM((1, total), F32), pltpu.VMEM((8, 1, total), F32),
                        pltpu.SemaphoreType.DMA((7,)), pltpu.SemaphoreType.DMA((7,))],
        compiler_params=pltpu.CompilerParams(has_side_effects=True),
    )(*parts, loss11)


def _adamw_small(red, ws, ms, vs):
    n = len(ws)

    def body(*refs):
        red_ref = refs[0]
        outs = refs[1 + 3 * n:]
        off = 0
        for i in range(n):
            w = refs[1 + i].shape[1]
            g = red_ref[:, off:off + w]
            d, m, v = _adamw_vals(refs[1 + i][...], g, refs[1 + n + i][...], refs[1 + 2 * n + i][...])
            for o, val in zip(outs[4 * i:4 * i + 4], (g, d, m, v)):
                o[...] = val
            off += w

    vm = pl.BlockSpec(memory_space=pltpu.VMEM)
    res = _pallas(
        body, name="adamw_small", in_specs=[vm] * (1 + 3 * n), out_specs=[vm] * (4 * n),
        out_shape=[jax.ShapeDtypeStruct(w.shape, F32) for w in ws for _ in range(4)],
    )(red, *ws, *ms, *vs)
    return [res[4 * i:4 * i + 4] for i in range(n)]


def _rope_tables(positions, S):
    pos = positions.reshape(S, 1).astype(F32)
    half = RET_QK // 2
    inv = ROPE_THETA ** (-jnp.arange(half, dtype=F32) / half)
    ang = pos * inv
    cosr = jnp.concatenate([jnp.cos(ang), jnp.cos(ang)], axis=1)
    sinr = jnp.concatenate([-jnp.sin(ang), jnp.sin(ang)], axis=1)
    half = QK_ROPE // 2
    inv = ROPE_THETA ** (-jnp.arange(half, dtype=F32) / half)
    ang = pos * inv
    z = jnp.zeros((S, half), F32)
    c = jnp.concatenate([jnp.cos(ang), jnp.cos(ang), z, z], axis=1)
    s1 = jnp.concatenate([-jnp.sin(ang), z, z, z], axis=1)
    s2 = jnp.concatenate([z, jnp.sin(ang), z, z], axis=1)
    return cosr, sinr, (c, s1, s2)


def _cat_cols(g):
    return jnp.concatenate([g[j] for j in range(N_CHIPS)], axis=1)


def _split_cols(w):
    return jnp.stack(jnp.split(w, N_CHIPS, axis=1))


def kernel(x, positions, norm_mix_g, w_in, ret_norm_g, w_ret_o, q_a_norm_g, w_q_b, kv_a_norm_g, w_kv_b, w_mla_o, w_out, norm_mlp_g, w_up, w_down, norm_f_g, loss_target, m_norm_mix_g, m_w_in, m_ret_norm_g, m_w_ret_o, m_q_a_norm_g, m_w_q_b, m_kv_a_norm_g, m_w_kv_b, m_w_mla_o, m_w_out, m_norm_mlp_g, m_w_up, m_w_down, m_norm_f_g, v_norm_mix_g, v_w_in, v_ret_norm_g, v_w_ret_o, v_q_a_norm_g, v_w_q_b, v_kv_a_norm_g, v_w_kv_b, v_w_mla_o, v_w_out, v_norm_mlp_g, v_w_up, v_w_down, v_norm_f_g):
    S, D = x.shape[1], x.shape[2]
    RVW = w_ret_o.shape[1] * N_CHIPS
    RH = RVW // RET_V
    RQW = RH * RET_QK
    MVW = w_mla_o.shape[1] * N_CHIPS
    MH = MVW // V_HEAD
    QL, KVL = w_q_b.shape[1], w_kv_b.shape[1]
    T_RET = _tile(S, 256)
    T_ATT = _tile(S, 512)

    xs = x.reshape(S, D)
    tgt = loss_target.reshape(S, D)
    cosr, sinr, pe_tabs = _rope_tables(positions, S)
    lgam = jnp.log(1.0 - 2.0 ** (-5.0 - jnp.arange(RH, dtype=F32)))
    lgam = jnp.broadcast_to(lgam[:, None, None], (RH, 8, LANES))

    big = ("w_in", "w_ret_o", "w_q_b", "w_kv_b", "w_mla_o", "w_out", "w_up", "w_down")
    w_sh = dict(w_in=w_in[0].T, w_ret_o=w_ret_o[0], w_q_b=w_q_b[0], w_kv_b=w_kv_b[0], w_mla_o=w_mla_o[0],
                w_out=w_out[0], w_up=w_up[0], w_down=w_down[0])
    m_sh = dict(w_in=m_w_in[0].T, w_ret_o=m_w_ret_o[0], w_q_b=m_w_q_b[0], w_kv_b=m_w_kv_b[0],
                w_mla_o=m_w_mla_o[0], w_out=m_w_out[0], w_up=m_w_up[0], w_down=m_w_down[0])
    v_sh = dict(w_in=v_w_in[0].T, w_ret_o=v_w_ret_o[0], w_q_b=v_w_q_b[0], w_kv_b=v_w_kv_b[0],
                w_mla_o=v_w_mla_o[0], w_out=v_w_out[0], w_up=v_w_up[0], w_down=v_w_down[0])
    col_sharded = ("w_q_b", "w_kv_b", "w_up")
    c_sh = w_in.shape[2]
    c_pad = -(-c_sh // 64) * 64
    place = jnp.stack([lax.axis_index("c"), 2 * lax.axis_index("x") + lax.axis_index("y")]).astype(jnp.int32)

    def whole(k, g):
        g = g.reshape(N_CHIPS, w_sh[k].shape[0], w_sh[k].shape[1])
        if k == "w_up":
            return g
        return _cat_cols(g) if k in col_sharded else g.reshape(-1, g.shape[2])

    first, mid = ("w_in",), ("w_q_b", "w_kv_b")
    later = ("w_ret_o", "w_mla_o", "w_out", "w_up", "w_down")
    first_bufs = [_cast_into_slot(w_sh[k], place, name="cast_" + k, rows=c_pad) for k in first]
    first_ssem, first_rsem, first_bufs, first_token = _split_start(
        first_bufs, _gather_ici_plan, 3 * len(first), name="gather_first_start")
    mid_bufs = [_cast_into_slot(w_sh[k], place, name="cast_" + k, deps=(first_token,)) for k in mid[:-1]]
    later_bufs = [_cast_into_slot(w_sh[k], place, name="cast_" + k, deps=(first_token,)) for k in later[:-1]]
    first_bufs = _split_wait(first_ssem, first_rsem, first_bufs, later_bufs[-1], _gather_ici_plan,
                             name="gather_first_wait")
    got = _forward_halves(first_bufs, name="gather_first_forward")
    mid_bufs.append(_cast_into_slot(w_sh[mid[-1]], place, name="cast_" + mid[-1], deps=(got[0],)))
    mid_ssem, mid_rsem, mid_bufs, mid_token = _split_start(
        mid_bufs, _gather_ici_plan, 3 * len(mid), name="gather_mid_start")
    later_bufs.append(_cast_into_slot(w_sh[later[-1]], place, name="cast_" + later[-1], deps=(got[0],)))
    later_ssem, later_rsem, later_bufs, later_token = _split_start(
        later_bufs, _gather_ici_plan, 3 * len(later), name="gather_later_start")
    full = {}

    o_rq, o_rk, o_rv, o_rg = 0, RQW, 2 * RQW, 2 * RQW + RVW
    o_cq = 2 * RQW + 2 * RVW
    o_ckv, o_kpe = o_cq + QL, o_cq + QL + KVL
    o_gr = o_kpe + QK_ROPE
    o_gm = o_gr + D
    n_ret = RH * RET_HEAD_COLS
    off_gret, off_gmla, off_cq, off_ckv = n_ret, n_ret + D, n_ret + 2 * D, n_ret + 2 * D + QL
    n_a = off_ckv + KVL
    runs = []
    for h in range(RH):
        base = h * RET_HEAD_COLS
        runs += [(o_rq + h * RET_QK, RET_QK, base), (o_rk + h * RET_QK, RET_QK, base + RET_QK),
                 (o_rv + h * RET_V, RET_V, base + 2 * RET_QK), (o_rg + h * RET_V, RET_V, base + 2 * RET_QK + RET_V)]
    runs += [(o_gr, D, off_gret), (o_gm, D, off_gmla), (o_cq, QL, off_cq), (o_ckv, KVL, off_ckv),
             (o_kpe, QK_ROPE, n_a)]

    def take(parts, start, width):
        out, lo = [], 0
        for p in parts:
            hi = lo + p.shape[0]
            a, b = max(start, lo), min(start + width, hi)
            if a < b:
                out.append(p[a - lo:b - lo])
            lo = hi
        return out

    wi = [got[0].reshape(N_CHIPS, c_pad, D)[jj, :c_sh] for jj in range(N_CHIPS)]
    here = sorted(runs, key=lambda r: r[2])
    wa = jnp.concatenate([p for s0, w, _ in here[:-1] for p in take(wi, s0, w)], axis=0)
    wkpe = jnp.concatenate(take(wi, o_kpe, QK_ROPE) + [jnp.zeros((LANES - QK_ROPE, D), BF16)], axis=0)

    u, rstd0 = _rmsnorm_fwd(xs, norm_mix_g, name="norm_mix")
    proj = _mm(u, wa, mode="nt", outs=[F32], name="in_proj", deps=(mid_token, later_token))
    kpe = _mm(u, wkpe, mode="nt", outs=[F32], name="kpe_proj")
    ry, gated, states = _ret_fwd(proj, cosr, sinr, lgam, ret_norm_g, RH, T=T_RET)
    cqn, ckvn, rstd_q, rstd_kv = _norm_pair_fwd(proj, q_a_norm_g, kv_a_norm_g, off_cq)
    mid_bufs = _split_wait(mid_ssem, mid_rsem, mid_bufs, cqn, _gather_ici_plan, name="gather_mid_wait")
    mid_bufs = _forward_halves(mid_bufs, name="gather_mid_forward")
    full.update({k: whole(k, g) for k, g in zip(mid, mid_bufs)})
    wq = jnp.pad(full["w_q_b"].reshape(QL, MH, QK_NOPE + QK_ROPE),
                 ((0, 0), (0, 0), (0, LANES - QK_ROPE))).reshape(QL, MH * 2 * LANES)
    wkv = full["w_kv_b"]
    qf, kf, vb = _qkv_proj(cqn, ckvn, wq, wkv, kpe, pe_tabs, MH)
    first_half = _attn_fwd(qf, kf, vb, MH, T=T_ATT, heads=(0, MH // 2), name="attn_fwd_a")
    later_bufs = _split_wait(later_ssem, later_rsem, later_bufs, first_half[0], _gather_ici_plan,
                             name="gather_later_wait")
    fwd_ssem, fwd_rsem, later_bufs, fwd_token = _split_start(
        later_bufs, _forward_plan, 3 * len(later), name="gather_later_forward_start")
    my, my_b, lse2 = _attn_fwd(qf, kf, vb, MH, T=T_ATT, heads=(MH // 2, MH), name="attn_fwd_b",
                               prev=first_half, deps=(fwd_token,))
    later_bufs = _split_wait(fwd_ssem, fwd_rsem, later_bufs, my, _forward_plan, name="gather_later_forward_wait")
    full.update({k: whole(k, g) for k, g in zip(later, later_bufs)})
    y_ret = _mm(gated, full["w_ret_o"], mode="nn", outs=[BF16], name="ret_o")
    y_mla, merged = _mm(my_b, full["w_mla_o"], mode="nn", outs=[BF16, BF16], name="mla_o",
                        epi=lambda acc, gr, gm, yr: (acc, _sigmoid(gr) * yr + _sigmoid(gm) * acc),
                        extras=((proj, off_gret), (proj, off_gmla), y_ret))
    h1 = _mm(merged, full["w_out"], mode="nn", outs=[F32], name="out_proj",
             epi=lambda acc, r: (acc + r,), extras=(xs,))
    n1, rstd1 = _rmsnorm_fwd(h1, norm_mlp_g, name="norm_mlp")

    def up_epi(acc):
        r = jnp.maximum(acc, 0.0)
        return acc, r * r

    z, act = _mm(n1, full["w_up"], mode="nn", outs=[F32, BF16], name="up_proj", epi=up_epi)
    h2 = _mm(act, full["w_down"], mode="nn", outs=[F32], name="down_proj",
             epi=lambda acc, r: (acc + r,), extras=(h1,))
    loss11, dh2, dh2_b, g_norm_f = _final_loss(h2, norm_f_g.reshape(1, D), tgt)

    dz = _mm(dh2_b, full["w_down"], mode="nt", outs=[BF16], name="down_bwd_x",
             epi=lambda acc, zz: (acc * (2.0 * jnp.maximum(zz, 0.0)),), extras=(z,))
    g_w_down = _mm(act, dh2_b, mode="tn", outs=[BF16], name="down_bwd_w")
    dn1 = _mm(dz, full["w_up"], mode="nt", outs=[F32], name="up_bwd_x")
    g_w_up = _mm(n1, dz, mode="tn", outs=[BF16], name="up_bwd_w", out_shards=True)

    def scatter_begin(tag, sums):
        lands = [lax.empty((3,) + s.shape[1:], s.dtype) for s in sums]
        return _split_start(sums + lands, _scatter_plan(len(sums)), 3 * len(sums), name="scatter_" + tag + "_start")

    def swap_begin(tag, grads):
        views = [g if g.ndim == 3 else g.reshape(N_CHIPS, g.shape[0] // N_CHIPS, g.shape[1]) for g in grads]
        views = [v.reshape(N_CHIPS, 2, v.shape[1] // 2, v.shape[2]) for v in views]
        lands = [lax.empty((N_CHIPS,) + v.shape[2:], v.dtype) for v in views]
        return _split_start(views + lands, _swap_plan(len(views)), len(views), name="swap_" + tag + "_start")

    def swap_end(tag, names, handle, after):
        n = len(names)
        bufs = _split_wait(handle[0], handle[1], handle[2], after, _swap_plan(n), name="swap_" + tag + "_wait")
        pcs = [b.reshape(N_CHIPS, 2 * b.shape[2], b.shape[3]) for b in bufs[:n]]
        sums = [_sum_pair(p, t, place, name="sum_pair_" + k) for k, p, t in zip(names, pcs, bufs[n:])]
        return pcs, bufs[n:], sums

    g1 = ("w_up", "w_down")
    swap1 = swap_begin("g1", (g_w_up, g_w_down))
    dh1, g_norm_mlp, dh1_b = _rmsnorm_bwd(dn1, h1, rstd1, norm_mlp_g, name="norm_mlp_bwd", res=dh2,
                                          deps=(swap1[3],), bf16_copy=1)
    dmerged = _mm(dh1_b, full["w_out"], mode="nt", outs=[F32], name="out_bwd_x")
    pcs1, theirs1, sums1 = swap_end("g1", g1, swap1, dmerged)
    ssem1, rsem1, bufs1, token1 = scatter_begin("g1", sums1)
    g_w_out = _mm(merged, dh1_b, mode="tn", outs=[BF16], name="out_bwd_w", deps=(token1,))
    dproj, dy_ret, dy_mla = _merge_bwd(dmerged, proj, y_ret, y_mla, D, off_gret)
    dgated = _mm(dy_ret, full["w_ret_o"], mode="nt", outs=[F32], name="ret_o_bwd_x")
    g_w_ret_o = _mm(gated, dy_ret, mode="tn", outs=[BF16], name="ret_o_bwd_w")
    dproj, g_ret_norm = _ret_bwd(proj, cosr, sinr, lgam, ret_norm_g, ry, dgated, states, dproj, RH, T=T_RET)
    def delta_epi(acc, o):
        rows = acc.shape[0]
        return acc, [jnp.broadcast_to(jnp.sum(acc[:, lo:lo + V_HEAD] * o[:, lo:lo + V_HEAD], axis=-1, keepdims=True),
                                      (rows, LANES)) for lo in range(0, acc.shape[1], V_HEAD)]

    dob, delta = _mm(dy_mla, full["w_mla_o"], mode="nt", outs=[BF16], name="mla_o_bwd_x", epi=delta_epi,
                     extras=(my,), more_outs=lambda tm, tn: [
                         (jax.ShapeDtypeStruct((MH, S, LANES), F32),
                          pl.BlockSpec((tn // V_HEAD, tm, LANES), lambda i, j, k: (j, i, 0)))])
    g_w_mla_o = _mm(my_b, dy_mla, mode="tn", outs=[BF16], name="mla_o_bwd_w")
    g2 = ("w_out", "w_ret_o", "w_mla_o")
    swap2 = swap_begin("g2", (g_w_out, g_w_ret_o, g_w_mla_o))
    dq_all, dkv_all, dkpe_h = _attn_bwd(qf, kf, vb, dob, lse2, delta, pe_tabs, MH, T=T_ATT, deps=(swap2[3],))
    pcs2, theirs2, sums2 = swap_end("g2", g2, swap2, dkv_all)
    ssem2, rsem2, bufs2, token2 = scatter_begin("g2", sums2)
    dkpe = _kpe_sum(dkpe_h, pe_tabs, MH)
    dcqn = _mm(dq_all, wq, mode="nt", outs=[F32], name="q_bwd_x", deps=(token2,))
    g_wq = _mm(cqn, dq_all, mode="tn", outs=[BF16], name="q_bwd_w")
    dckvn = _mm(dkv_all, wkv, mode="nt", outs=[F32], name="kv_bwd_x")
    g_wkv = _mm(ckvn, dkv_all, mode="tn", outs=[BF16], name="kv_bwd_w")
    dproj, g_q_a, g_kv_a = _norm_pair_bwd(dcqn, dckvn, proj, rstd_q, rstd_kv, q_a_norm_g, kv_a_norm_g, dproj, off_cq)
    g_wa = _mm(dproj, u, mode="tn", outs=[BF16], name="in_bwd_w")
    g_wkpe = _mm(dkpe, u, mode="tn", outs=[BF16], name="kpe_bwd_w")

    there = sorted(runs)
    g_parts = [g_wa, g_wkpe]
    g_w_in = jnp.stack([jnp.concatenate(
        [p for s0, w, d0 in there for a, b in [(max(s0, jj * c_sh), min(s0 + w, (jj + 1) * c_sh))] if a < b
         for p in take(g_parts, d0 + a - s0, b - a)] + [jnp.zeros((c_pad - c_sh, D), BF16)], axis=0)
        for jj in range(N_CHIPS)])
    gq = g_wq.reshape(QL, MH, 2 * LANES)[:, :, :QK_NOPE + QK_ROPE].reshape(QL, MH * (QK_NOPE + QK_ROPE))
    g3 = ("w_in", "w_q_b", "w_kv_b")
    swap3 = swap_begin("g3", (g_w_in, _split_cols(gq), _split_cols(g_wkv)))
    pcs3, theirs3, sums3 = swap_end("g3", g3, swap3, swap3[3])
    ssem3, rsem3, bufs3, token3 = scatter_begin("g3", sums3)

    def chip_sums(names, pcs, theirs, recv):
        return [_sum_chips(p, t, r, place, name="sum_chips_" + k) for k, p, t, r in zip(names, pcs, theirs, recv)]

    bufs1 = _split_wait(ssem1, rsem1, bufs1, token3, _scatter_plan(len(g1)), name="scatter_g1_wait")
    bufs2 = _split_wait(ssem2, rsem2, bufs2, token3, _scatter_plan(len(g2)), name="scatter_g2_wait")
    halves12 = chip_sums(g1, pcs1, theirs1, bufs1[len(g1):]) + chip_sums(g2, pcs2, theirs2, bufs2[len(g2):])
    jssem, jrsem, halves12, join_token = _split_start(halves12, _join_plan, len(halves12), name="join_g12_start")
    du = _mm(dproj, wa, mode="nn", outs=[F32], name="in_bwd_x", tk=n_a // 4, tail=(dkpe, wkpe), deps=(join_token,))
    dx, g_norm_mix = _rmsnorm_bwd(du, xs, rstd0, norm_mix_g, name="norm_mix_bwd", res=dh1)

    bufs3 = _split_wait(ssem3, rsem3, bufs3, dx, _scatter_plan(len(g3)), name="scatter_g3_wait")
    halves12 = _split_wait(jssem, jrsem, halves12, dx, _join_plan, name="join_g12_wait")
    j3ssem, j3rsem, halves3, join3_token = _split_start(
        chip_sums(g3, pcs3, theirs3, bufs3[len(g3):]), _join_plan, len(g3), name="join_g3_start")

    small = ("norm_mix_g", "ret_norm_g", "q_a_norm_g", "kv_a_norm_g", "norm_mlp_g", "norm_f_g")
    g_small = [g_norm_mix, g_ret_norm, g_q_a, g_kv_a, g_norm_mlp, g_norm_f]
    red = _allreduce_small(g_small, loss11)
    loss = red[0, red.shape[1] - 1]
    w_small = [norm_mix_g, ret_norm_g, q_a_norm_g, kv_a_norm_g, norm_mlp_g, norm_f_g]
    m_small = [m_norm_mix_g, m_ret_norm_g, m_q_a_norm_g, m_kv_a_norm_g, m_norm_mlp_g, m_norm_f_g]
    v_small = [v_norm_mix_g, v_ret_norm_g, v_q_a_norm_g, v_kv_a_norm_g, v_norm_mlp_g, v_norm_f_g]
    row = lambda a: a.reshape(1, -1)
    upd = _adamw_small(red, [row(a) for a in w_small], [row(a) for a in m_small], [row(a) for a in v_small])
    out_g, out_d, out_m, out_v = {}, {}, {}, {}
    for k, wv, (g_, d_, m_, v_) in zip(small, w_small, upd):
        out_g[k], out_d[k], out_m[k], out_v[k] = [a.reshape(wv.shape) for a in (g_, d_, m_, v_)]

    def adamw_shard(k, joined, deps=()):
        g = joined.reshape(2 * joined.shape[1], joined.shape[2])
        res = _rows_call(lambda w, g, m, v: (g,) + _adamw_vals(w, g, m, v),
                         [w_sh[k], g, m_sh[k], v_sh[k]], [F32] * 4, name="adamw_" + k, deps=deps)
        if k == "w_in":
            res = [r.T for r in res]
        out_g[k], out_d[k], out_m[k], out_v[k] = [r[None] for r in res]
        return res[0]

    for k, joined in zip(g1 + g2, halves12):
        last = adamw_shard(k, joined, deps=(join3_token,))
    halves3 = _split_wait(j3ssem, j3rsem, halves3, last, _join_plan, name="join_g3_wait")
    for k, joined in zip(g3, halves3):
        adamw_shard(k, joined)

    order = ("norm_mix_g", "w_in", "ret_norm_g", "w_ret_o", "q_a_norm_g", "w_q_b", "kv_a_norm_g", "w_kv_b",
             "w_mla_o", "w_out", "norm_mlp_g", "w_up", "w_down", "norm_f_g")
    return (loss, dx.reshape(1, S, D), *[out_g[k] for k in order], *[out_d[k] for k in order],
            *[out_m[k] for k in order], *[out_v[k] for k in order])
```

```python
import math

import jax
import jax.numpy as jnp
from jax import lax
from jax.experimental import pallas as pl
from jax.experimental.pallas import tpu as pltpu

F32 = jnp.float32
BF16 = jnp.bfloat16

EPS = 1e-6
ROPE_THETA = 10000.0
CHUNK = 64
RET_QK = 128
RET_V = 256
RET_HEAD_COLS = 2 * RET_QK + 2 * RET_V
QK_NOPE = 128
QK_ROPE = 64
V_HEAD = 128
LANES = 128
LOG2E = math.log2(math.e)

ADAM_LR = 0.001
ADAM_B1 = 0.9
ADAM_B2 = 0.999
ADAM_EPS = 1e-08
ADAM_WD = 0.01
ADAM_STEP = 10

N_CHIPS = 4
VMEM_LIMIT = 56 * 1024 * 1024
MESH = pl.DeviceIdType.MESH
NEG = -1e30


def _pallas(body, **kw):
    return pl.pallas_call(body, **kw)


def _params(sem=None):
    return pltpu.CompilerParams(dimension_semantics=sem, vmem_limit_bytes=VMEM_LIMIT)


def _tile(n, want):
    t = min(n, want)
    while n % t:
        t //= 2
    return t


_ANY = pl.BlockSpec(memory_space=pl.ANY)
TN_BF16_TK = 4096


def _mm(a, b, *, mode, outs, name, epi=None, extras=(), deps=(), out_shards=False, more_outs=None, tail=None,
        tm=1024, tn=1024, tk=2048):
    shards = b.shape[0] if b.ndim == 3 else 1
    brows, bcols = b.shape[-2], b.shape[-1] * shards
    if mode == "nn":
        (M, K), N = a.shape, bcols
    elif mode == "nt":
        (M, K), N = a.shape, brows
    else:
        (K, M), N = a.shape, bcols
    if mode == "tn" and a.dtype == BF16 and b.dtype == BF16:
        tk = max(tk, TN_BF16_TK)
    tm = _tile(M, tm)
    tn = _tile(N // (shards if mode == "nn" else 1) // (N_CHIPS if out_shards else 1), tn)
    tk = _tile(K // (shards if mode == "nt" else 1), tk)
    nk = K // tk
    if mode == "nn":
        a_spec = pl.BlockSpec((tm, tk), lambda i, j, k: (i, k))
        dims = (((1,), (0,)), ((), ()))
        if shards > 1:
            per = N // shards // tn
            b_spec = pl.BlockSpec((None, tk, tn), lambda i, j, k: (j // per, k, j % per))
        else:
            b_spec = pl.BlockSpec((tk, tn), lambda i, j, k: (k, j))
    elif mode == "nt":
        a_spec = pl.BlockSpec((tm, tk), lambda i, j, k: (i, k))
        dims = (((1,), (1,)), ((), ()))
        if shards > 1:
            per = K // shards // tk
            b_spec = pl.BlockSpec((None, tn, tk), lambda i, j, k: (k // per, j, k % per))
        else:
            b_spec = pl.BlockSpec((tn, tk), lambda i, j, k: (j, k))
    else:
        assert shards == 1
        a_spec = pl.BlockSpec((tk, tm), lambda i, j, k: (k, i))
        b_spec = pl.BlockSpec((tk, tn), lambda i, j, k: (k, j))
        dims = (((0,), (0,)), ((), ()))
    if out_shards:
        assert not extras
        oper = N // N_CHIPS // tn
        o_spec = pl.BlockSpec((None, tm, tn), lambda i, j, k: (j // oper, i, j % oper))
        o_shape = (N_CHIPS, M, N // N_CHIPS)
    else:
        o_spec = pl.BlockSpec((tm, tn), lambda i, j, k: (i, j))
        o_shape = (M, N)
    more = [] if more_outs is None else more_outs(tm, tn)
    ex_arrays = [e[0] if isinstance(e, tuple) else e for e in extras]
    ex_specs = [pl.BlockSpec((tm, tn), lambda i, j, k, off=e[1] // tn, st=e[2]: (i, off + st * j))
                if isinstance(e, tuple) else o_spec for e in extras]
    n_ex, n_out, n_dep = len(extras), len(outs) + len(more), len(deps)
    if epi is None:
        epi = lambda acc: (acc,)
    tails, tail_specs = [], []
    if tail is not None:
        assert mode == "nn"
        tails = list(tail)
        k2 = tail[0].shape[1]
        tail_specs = [pl.BlockSpec((tm, k2), lambda i, j, k: (i, 0)), pl.BlockSpec((k2, tn), lambda i, j, k: (0, j))]
    n_tail = len(tails)

    def body(*refs):
        a_ref, b_ref = refs[0], refs[1]
        ex_refs = refs[2:2 + n_ex]
        t_refs = refs[2 + n_ex:2 + n_ex + n_tail]
        first_out = 2 + n_ex + n_tail + n_dep
        o_refs = refs[first_out:first_out + n_out]
        part = lax.dot_general(a_ref[...].astype(BF16), b_ref[...].astype(BF16), dims,
                               preferred_element_type=F32)

        def finish(acc):
            if n_tail:
                acc = acc + lax.dot_general(t_refs[0][...].astype(BF16), t_refs[1][...].astype(BF16), dims,
                                            preferred_element_type=F32)
            vals = epi(acc, *[r[...] for r in ex_refs])
            for r, v in zip(o_refs, vals):
                if isinstance(v, (list, tuple)):
                    for lead, piece in enumerate(v):
                        r[lead] = piece.astype(r.dtype)
                else:
                    r[...] = v.astype(r.dtype)

        if nk == 1:
            finish(part)
        else:
            acc_ref = refs[-1]
            k = pl.program_id(2)

            @pl.when(k == 0)
            def _():
                acc_ref[...] = part

            @pl.when(k > 0)
            def _():
                acc_ref[...] += part

            @pl.when(k == nk - 1)
            def _():
                finish(acc_ref[...])

    res = _pallas(
        body, name=name, grid=(M // tm, N // tn, nk),
        in_specs=[a_spec, b_spec] + ex_specs + tail_specs + [_ANY] * n_dep,
        out_specs=[o_spec] * len(outs) + [spec for _, spec in more],
        out_shape=[jax.ShapeDtypeStruct(o_shape, d) for d in outs] + [shape for shape, _ in more],
        scratch_shapes=[pltpu.VMEM((tm, tn), F32)] if nk > 1 else [],
        compiler_params=_params(("parallel", "parallel", "arbitrary")),
    )(a, b, *ex_arrays, *tails, *deps)
    return res[0] if n_out == 1 else res


def _rmsnorm_fwd(x, g, *, name, tr=256):
    S, W = x.shape
    tr = _tile(S, tr)

    def body(x_ref, g_ref, y_ref, r_ref):
        xv = x_ref[...]
        rstd = lax.rsqrt(jnp.mean(xv * xv, axis=-1, keepdims=True) + EPS)
        y_ref[...] = (xv * rstd * g_ref[...]).astype(BF16)
        r_ref[...] = rstd

    return _pallas(
        body, name=name, grid=(S // tr,),
        in_specs=[pl.BlockSpec((tr, W), lambda i: (i, 0)), pl.BlockSpec((1, W), lambda i: (0, 0))],
        out_specs=[pl.BlockSpec((tr, W), lambda i: (i, 0)), pl.BlockSpec((tr, 1), lambda i: (i, 0))],
        out_shape=[jax.ShapeDtypeStruct((S, W), BF16), jax.ShapeDtypeStruct((S, 1), F32)],
        compiler_params=_params(("parallel",)),
    )(x, g)


def _rmsnorm_bwd(dy, x, rstd, g, *, name, res=None, deps=(), bf16_copy=0, tr=256):
    S, W = x.shape
    tr = _tile(S, tr)
    has_res = res is not None

    def body(*refs):
        dy_ref, x_ref, r_ref, g_ref = refs[:4]
        dx_ref, dg_ref = refs[-2 - bf16_copy], refs[-1 - bf16_copy]
        rstd_v = r_ref[...]
        xhat = x_ref[...] * rstd_v
        dyv = dy_ref[...].astype(F32)
        dyg = dyv * g_ref[...]
        dx = rstd_v * (dyg - xhat * jnp.mean(dyg * xhat, axis=-1, keepdims=True))
        if has_res:
            dx = dx + refs[4][...]
        dx_ref[...] = dx.astype(dx_ref.dtype)
        if bf16_copy:
            refs[-1][...] = dx.astype(BF16)
        part = jnp.sum(dyv * xhat, axis=0, keepdims=True)

        @pl.when(pl.program_id(0) == 0)
        def _():
            dg_ref[...] = part

        @pl.when(pl.program_id(0) > 0)
        def _():
            dg_ref[...] += part

    row = pl.BlockSpec((tr, W), lambda i: (i, 0))
    ins = [dy, x, rstd, g] + ([res] if has_res else [])
    in_specs = [row, row, pl.BlockSpec((tr, 1), lambda i: (i, 0)),
                pl.BlockSpec((1, W), lambda i: (0, 0))] + ([row] if has_res else [])
    ins += list(deps)
    in_specs += [_ANY] * len(deps)
    return _pallas(
        body, name=name, grid=(S // tr,), in_specs=in_specs,
        out_specs=[row, pl.BlockSpec((1, W), lambda i: (0, 0))] + [row] * bf16_copy,
        out_shape=[jax.ShapeDtypeStruct((S, W), F32), jax.ShapeDtypeStruct((1, W), F32)]
        + [jax.ShapeDtypeStruct((S, W), BF16)] * bf16_copy,
        compiler_params=_params(("arbitrary",)),
    )(*ins)


def _norm_pair_fwd(proj, g_a, g_b, off, *, tr=512):
    S = proj.shape[0]
    wa_, wb_ = g_a.shape[1], g_b.shape[1]
    W = wa_ + wb_
    tr = _tile(S, tr)

    def body(x_ref, ga_ref, gb_ref, ya_ref, yb_ref, ra_ref, rb_ref):
        for lo, hi, g_ref, y_ref, r_ref in ((0, wa_, ga_ref, ya_ref, ra_ref), (wa_, W, gb_ref, yb_ref, rb_ref)):
            xv = x_ref[:, lo:hi]
            rstd = lax.rsqrt(jnp.mean(xv * xv, axis=-1, keepdims=True) + EPS)
            y_ref[...] = (xv * rstd * g_ref[...]).astype(BF16)
            r_ref[...] = rstd

    one = pl.BlockSpec((tr, 1), lambda i: (i, 0))
    return _pallas(
        body, name="norm_qkv", grid=(S // tr,),
        in_specs=[pl.BlockSpec((tr, W), lambda i: (i, off // W)), pl.BlockSpec((1, wa_), lambda i: (0, 0)),
                  pl.BlockSpec((1, wb_), lambda i: (0, 0))],
        out_specs=[pl.BlockSpec((tr, wa_), lambda i: (i, 0)), pl.BlockSpec((tr, wb_), lambda i: (i, 0)), one, one],
        out_shape=[jax.ShapeDtypeStruct((S, wa_), BF16), jax.ShapeDtypeStruct((S, wb_), BF16),
                   jax.ShapeDtypeStruct((S, 1), F32), jax.ShapeDtypeStruct((S, 1), F32)],
        compiler_params=_params(("parallel",)),
    )(proj, g_a, g_b)


def _norm_pair_bwd(dy_a, dy_b, proj, r_a, r_b, g_a, g_b, dproj, off, *, tr=512):
    S = proj.shape[0]
    wa_, wb_ = g_a.shape[1], g_b.shape[1]
    W = wa_ + wb_
    tr = _tile(S, tr)

    def body(dya_ref, dyb_ref, x_ref, ra_ref, rb_ref, ga_ref, gb_ref, _, dx_ref, dga_ref, dgb_ref):
        first = pl.program_id(0) == 0
        for lo, hi, dy_ref, r_ref, g_ref, dg_ref in ((0, wa_, dya_ref, ra_ref, ga_ref, dga_ref),
                                                    (wa_, W, dyb_ref, rb_ref, gb_ref, dgb_ref)):
            rstd = r_ref[...]
            xhat = x_ref[:, lo:hi] * rstd
            dyv = dy_ref[...]
            dyg = dyv * g_ref[...]
            dx_ref[:, lo:hi] = (rstd * (dyg - xhat * jnp.mean(dyg * xhat, axis=-1, keepdims=True))).astype(dx_ref.dtype)
            part = jnp.sum(dyv * xhat, axis=0, keepdims=True)

            @pl.when(first)
            def _():
                dg_ref[...] = part

            @pl.when(jnp.logical_not(first))
            def _():
                dg_ref[...] += part

    one = pl.BlockSpec((tr, 1), lambda i: (i, 0))
    cols = pl.BlockSpec((tr, W), lambda i: (i, off // W))
    va, vb = pl.BlockSpec((1, wa_), lambda i: (0, 0)), pl.BlockSpec((1, wb_), lambda i: (0, 0))
    return _pallas(
        body, name="norm_qkv_bwd", grid=(S // tr,),
        in_specs=[pl.BlockSpec((tr, wa_), lambda i: (i, 0)), pl.BlockSpec((tr, wb_), lambda i: (i, 0)), cols,
                  one, one, va, vb, _ANY],
        out_specs=[cols, va, vb],
        out_shape=[jax.ShapeDtypeStruct(dproj.shape, dproj.dtype), jax.ShapeDtypeStruct((1, wa_), F32),
                   jax.ShapeDtypeStruct((1, wb_), F32)],
        input_output_aliases={7: 0},
        compiler_params=_params(("arbitrary",)),
    )(dy_a, dy_b, proj, r_a, r_b, g_a, g_b, dproj)


def _final_loss(h2, g, target, *, tr=256):
    S, D = h2.shape
    tr = _tile(S, tr)

    def body(h_ref, g_ref, t_ref, loss_ref, dh_ref, dhb_ref, dg_ref):
        hv = h_ref[...]
        rstd = lax.rsqrt(jnp.mean(hv * hv, axis=-1, keepdims=True) + EPS)
        xhat = hv * rstd
        e = xhat * g_ref[...] - t_ref[...]
        lpart = (0.5 / D) * jnp.sum(jnp.sum(e * e, axis=-1, keepdims=True), axis=0, keepdims=True)
        dy = e * (1.0 / D)
        dyg = dy * g_ref[...]
        dh = rstd * (dyg - xhat * jnp.mean(dyg * xhat, axis=-1, keepdims=True))
        dh_ref[...] = dh
        dhb_ref[...] = dh.astype(BF16)
        gpart = jnp.sum(dy * xhat, axis=0, keepdims=True)

        @pl.when(pl.program_id(0) == 0)
        def _():
            loss_ref[...] = lpart
            dg_ref[...] = gpart

        @pl.when(pl.program_id(0) > 0)
        def _():
            loss_ref[...] += lpart
            dg_ref[...] += gpart

    row = pl.BlockSpec((tr, D), lambda i: (i, 0))
    vec = pl.BlockSpec((1, D), lambda i: (0, 0))
    return _pallas(
        body, name="final_loss", grid=(S // tr,), in_specs=[row, vec, row],
        out_specs=[pl.BlockSpec((1, 1), lambda i: (0, 0)), row, row, vec],
        out_shape=[jax.ShapeDtypeStruct((1, 1), F32), jax.ShapeDtypeStruct((S, D), F32),
                   jax.ShapeDtypeStruct((S, D), BF16), jax.ShapeDtypeStruct((1, D), F32)],
        compiler_params=_params(("arbitrary",)),
    )(h2, g, target)


def _sigmoid(v):
    return 1.0 / (1.0 + jnp.exp(-v))


def _rope128(t, cos_full, sin_signed):
    return t * cos_full + pltpu.roll(t, RET_QK // 2, 1) * sin_signed


def _rope128_t(d, cos_full, sin_signed):
    return d * cos_full + pltpu.roll(d * sin_signed, RET_QK // 2, 1)


def _ret_consts(lg, T):
    pos = lax.broadcasted_iota(jnp.int32, (T, 1), 0).astype(F32)
    qd = jnp.exp(lg * (pos + 1.0))
    kd = jnp.exp(lg * (T - 1.0 - pos))
    n = lax.broadcasted_iota(jnp.int32, (T, T), 0)
    m = lax.broadcasted_iota(jnp.int32, (T, T), 1)
    vis = (m // CHUNK) <= (n // CHUNK)
    dist = jnp.abs(n - m).astype(F32)
    decay = jnp.where(vis, jnp.exp(lg * dist), 0.0)
    cdec = jnp.exp(lg * float(T))
    return qd, kd, decay, cdec


def _dot(a, b, dims):
    return lax.dot_general(a.astype(BF16), b.astype(BF16), (dims, ((), ())), preferred_element_type=F32)


NN = ((1,), (0,))
NT = ((1,), (1,))
TN = ((0,), (0,))
_RQ = slice(0, RET_QK)
_RK = slice(RET_QK, 2 * RET_QK)
_RV = slice(2 * RET_QK, 2 * RET_QK + RET_V)
_RG = slice(2 * RET_QK + RET_V, RET_HEAD_COLS)


RET_GROUP = 8


def _head_cols(h, part):
    return slice(h * RET_HEAD_COLS + part.start, h * RET_HEAD_COLS + part.stop)


def _ret_fwd(proj, cosr, sinr, lgam, gain, RH, *, T):
    S = proj.shape[0]
    nb = S // T
    G = _tile(RH, RET_GROUP)
    heads = range(G)
    scale = RET_QK ** -0.5

    def body(p_ref, cos_ref, sin_ref, lg_ref, gain_ref, ry_ref, gated_ref, st_ref, state):
        b = pl.program_id(1)

        @pl.when(b == 0)
        def _():
            state[...] = jnp.zeros_like(state)

        consts = [_ret_consts(lg_ref[h, 0:1, 0:1], T) for h in heads]
        cosv, sinv = cos_ref[...], sin_ref[...]
        q = [_rope128(p_ref[:, _head_cols(h, _RQ)], cosv, sinv) for h in heads]
        k = [_rope128(p_ref[:, _head_cols(h, _RK)], cosv, sinv) * scale for h in heads]
        v = [p_ref[:, _head_cols(h, _RV)] for h in heads]
        sprev = [state[h] for h in heads]
        for h in heads:
            st_ref[h] = sprev[h]
        a = [_dot(q[h], k[h], NT) for h in heads]
        qs = [_dot(q[h] * consts[h][0], sprev[h], NN) for h in heads]
        kv = [_dot(k[h] * consts[h][1], v[h], TN) for h in heads]
        o = [_dot(a[h] * consts[h][2], v[h], NN) + qs[h] for h in heads]
        for h in heads:
            state[h] = sprev[h] * consts[h][3] + kv[h]
            vals = slice(h * RET_V, (h + 1) * RET_V)
            ry_ref[:, vals] = o[h]
            mu = jnp.mean(o[h], axis=-1, keepdims=True)
            oc = o[h] - mu
            var = jnp.mean(oc * oc, axis=-1, keepdims=True)
            t = oc * lax.rsqrt(var + EPS) * gain_ref[:, vals]
            gv = p_ref[:, _head_cols(h, _RG)]
            gated_ref[:, vals] = (t * (gv * _sigmoid(gv))).astype(BF16)

    return _pallas(
        body, name="ret_fwd", grid=(RH // G, nb),
        in_specs=[pl.BlockSpec((T, G * RET_HEAD_COLS), lambda h, b: (b, h)),
                  pl.BlockSpec((T, RET_QK), lambda h, b: (b, 0)),
                  pl.BlockSpec((T, RET_QK), lambda h, b: (b, 0)),
                  pl.BlockSpec((G, 8, LANES), lambda h, b: (h, 0, 0)),
                  pl.BlockSpec((1, G * RET_V), lambda h, b: (0, h))],
        out_specs=[pl.BlockSpec((T, G * RET_V), lambda h, b: (b, h)),
                   pl.BlockSpec((T, G * RET_V), lambda h, b: (b, h)),
                   pl.BlockSpec((G, None, RET_QK, RET_V), lambda h, b: (h, b, 0, 0))],
        out_shape=[jax.ShapeDtypeStruct((S, RH * RET_V), F32), jax.ShapeDtypeStruct((S, RH * RET_V), BF16),
                   jax.ShapeDtypeStruct((RH, nb, RET_QK, RET_V), F32)],
        scratch_shapes=[pltpu.VMEM((G, RET_QK, RET_V), F32)],
        compiler_params=_params(("parallel", "arbitrary")),
    )(proj, cosr, sinr, lgam, gain)


def _ret_bwd(proj, cosr, sinr, lgam, gain, ry, dgated, states, dproj, RH, *, T):
    S = proj.shape[0]
    nb = S // T
    G = _tile(RH, RET_GROUP)
    heads = range(G)
    scale = RET_QK ** -0.5

    def body(p_ref, cos_ref, sin_ref, lg_ref, gain_ref, ry_ref, dg_ref, st_ref, _, dp_ref, dgain_ref, dstate):
        b = pl.program_id(1)

        @pl.when(b == 0)
        def _():
            dstate[...] = jnp.zeros_like(dstate)

        consts = [_ret_consts(lg_ref[h, 0:1, 0:1], T) for h in heads]
        qd, kd, decay, cdec = [[c[i] for c in consts] for i in range(4)]
        cosv, sinv = cos_ref[...], sin_ref[...]
        q = [_rope128(p_ref[:, _head_cols(h, _RQ)], cosv, sinv) for h in heads]
        k = [_rope128(p_ref[:, _head_cols(h, _RK)], cosv, sinv) * scale for h in heads]
        v = [p_ref[:, _head_cols(h, _RV)] for h in heads]
        sprev = [st_ref[h] for h in heads]
        ds_new = [dstate[h] for h in heads]
        a = [_dot(q[h], k[h], NT) for h in heads]
        do, gparts = [], []
        for h in heads:
            vals = slice(h * RET_V, (h + 1) * RET_V)
            o = ry_ref[:, vals]
            mu = jnp.mean(o, axis=-1, keepdims=True)
            oc = o - mu
            rstd = lax.rsqrt(jnp.mean(oc * oc, axis=-1, keepdims=True) + EPS)
            ryn = oc * rstd
            gainv = gain_ref[:, vals]
            gv = p_ref[:, _head_cols(h, _RG)]
            sg = _sigmoid(gv)
            dgt = dg_ref[:, vals]
            dt = dgt * (gv * sg)
            dp_ref[:, _head_cols(h, _RG)] = (dgt * (ryn * gainv) * (sg * (1.0 + gv * (1.0 - sg)))).astype(BF16)
            gparts.append(jnp.sum(dt * ryn, axis=0, keepdims=True))
            dryn = dt * gainv
            do.append(rstd * (dryn - jnp.mean(dryn, axis=-1, keepdims=True)
                              - ryn * jnp.mean(dryn * ryn, axis=-1, keepdims=True)))
        gpart = jnp.concatenate(gparts, axis=1)

        @pl.when(b == 0)
        def _():
            dgain_ref[...] = gpart

        @pl.when(b > 0)
        def _():
            dgain_ref[...] += gpart

        dpm = [_dot(do[h], v[h], NT) for h in heads]
        dq_s = [_dot(do[h], sprev[h], NT) for h in heads]
        dk_s = [_dot(v[h], ds_new[h], NT) for h in heads]
        dv_s = [_dot(k[h] * kd[h], ds_new[h], NN) for h in heads]
        dst = [_dot(q[h] * qd[h], do[h], TN) for h in heads]
        a = [a[h] * decay[h] for h in heads]
        dpm = [dpm[h] * decay[h] for h in heads]
        dv = [_dot(a[h], do[h], TN) + dv_s[h] for h in heads]
        dq = [_dot(dpm[h], k[h], NN) + dq_s[h] * qd[h] for h in heads]
        dk = [(_dot(dpm[h], q[h], TN) + dk_s[h] * kd[h]) * scale for h in heads]
        for h in heads:
            dstate[h] = ds_new[h] * cdec[h] + dst[h]
            dp_ref[:, _head_cols(h, _RV)] = dv[h].astype(BF16)
            dp_ref[:, _head_cols(h, _RQ)] = _rope128_t(dq[h], cosv, sinv).astype(BF16)
            dp_ref[:, _head_cols(h, _RK)] = _rope128_t(dk[h], cosv, sinv).astype(BF16)

    rb = lambda b: nb - 1 - b
    return _pallas(
        body, name="ret_bwd", grid=(RH // G, nb),
        in_specs=[pl.BlockSpec((T, G * RET_HEAD_COLS), lambda h, b: (rb(b), h)),
                  pl.BlockSpec((T, RET_QK), lambda h, b: (rb(b), 0)),
                  pl.BlockSpec((T, RET_QK), lambda h, b: (rb(b), 0)),
                  pl.BlockSpec((G, 8, LANES), lambda h, b: (h, 0, 0)),
                  pl.BlockSpec((1, G * RET_V), lambda h, b: (0, h)),
                  pl.BlockSpec((T, G * RET_V), lambda h, b: (rb(b), h)),
                  pl.BlockSpec((T, G * RET_V), lambda h, b: (rb(b), h)),
                  pl.BlockSpec((G, None, RET_QK, RET_V), lambda h, b: (h, rb(b), 0, 0)),
                  _ANY],
        out_specs=[pl.BlockSpec((T, G * RET_HEAD_COLS), lambda h, b: (rb(b), h)),
                   pl.BlockSpec((1, G * RET_V), lambda h, b: (0, h))],
        out_shape=[jax.ShapeDtypeStruct(dproj.shape, dproj.dtype), jax.ShapeDtypeStruct((1, RH * RET_V), F32)],
        scratch_shapes=[pltpu.VMEM((G, RET_QK, RET_V), F32)],
        input_output_aliases={8: 0},
        compiler_params=_params(("parallel", "arbitrary")),
    )(proj, cosr, sinr, lgam, gain, ry, dgated, states, dproj)


def _rope_pe(t, c, s1, s2):
    return t * c + pltpu.roll(t, LANES - QK_ROPE // 2, 1) * s1 + pltpu.roll(t, QK_ROPE // 2, 1) * s2


def _rope_pe_t(d, c, s1, s2):
    return d * c + pltpu.roll(d * s1, QK_ROPE // 2, 1) + pltpu.roll(d * s2, LANES - QK_ROPE // 2, 1)


ATTN_C2 = (QK_NOPE + QK_ROPE) ** -0.5 * LOG2E


def _qkv_proj(cqn, ckvn, wq, wkv, kpe, tabs, MH, *, tm=512, heads=4):
    S = cqn.shape[0]
    tm = _tile(S, tm)
    hb = _tile(MH, heads)
    W = 2 * LANES
    c_t, s1_t, s2_t = tabs

    def body(cq_ref, ckv_ref, wq_ref, wkv_ref, kpe_ref, c_ref, s1_ref, s2_ref, qf_ref, kf_ref, v_ref):
        c, s1, s2 = c_ref[...], s1_ref[...], s2_ref[...]
        q = _dot(cq_ref[...], wq_ref[...], NN)
        kv = _dot(ckv_ref[...], wkv_ref[...], NN)
        kper = _rope_pe(kpe_ref[...], c, s1, s2).astype(BF16)
        for h in range(hb):
            lo, mid, hi = h * W, h * W + QK_NOPE, (h + 1) * W
            qf_ref[:, lo:mid] = (q[:, lo:mid] * ATTN_C2).astype(BF16)
            qf_ref[:, mid:hi] = (_rope_pe(q[:, mid:hi], c, s1, s2) * ATTN_C2).astype(BF16)
            kf_ref[:, lo:mid] = kv[:, lo:mid].astype(BF16)
            kf_ref[:, mid:hi] = kper
            v_ref[:, h * V_HEAD:(h + 1) * V_HEAD] = kv[:, mid:hi].astype(BF16)

    tab = pl.BlockSpec((tm, LANES), lambda i, j: (i, 0))
    grp = pl.BlockSpec((tm, hb * W), lambda i, j: (i, j))
    return _pallas(
        body, name="qkv_proj", grid=(S // tm, MH // hb),
        in_specs=[pl.BlockSpec((tm, cqn.shape[1]), lambda i, j: (i, 0)),
                  pl.BlockSpec((tm, ckvn.shape[1]), lambda i, j: (i, 0)),
                  pl.BlockSpec((wq.shape[0], hb * W), lambda i, j: (0, j)),
                  pl.BlockSpec((wkv.shape[0], hb * W), lambda i, j: (0, j)), tab, tab, tab, tab],
        out_specs=[grp, grp, pl.BlockSpec((tm, hb * V_HEAD), lambda i, j: (i, j))],
        out_shape=[jax.ShapeDtypeStruct((S, MH * W), BF16)] * 2 + [jax.ShapeDtypeStruct((S, MH * V_HEAD), BF16)],
        compiler_params=_params(("parallel", "parallel")),
    )(cqn, ckvn, wq, wkv, kpe, c_t, s1_t, s2_t)


def _chunk_mask(T):
    n = lax.broadcasted_iota(jnp.int32, (T, T), 0)
    m = lax.broadcasted_iota(jnp.int32, (T, T), 1)
    return (m // CHUNK) <= (n // CHUNK)


def _lanes_to(v, width):
    return jnp.tile(v, (1, width // LANES))


def _attn_fwd(qf, kf, vb, MH, *, T, heads, name, prev=(), deps=()):
    S = qf.shape[0]
    nt = S // T
    n_skip = len(prev) + len(deps)

    def body(q_ref, k_ref, v_ref, *rest):
        o_ref, ob_ref, lse_ref, m_sc, l_sc, acc_sc, s_a, s_b = rest[n_skip:]
        qi = pl.program_id(1)
        m_sc[...] = jnp.full_like(m_sc, NEG)
        l_sc[...] = jnp.zeros_like(l_sc)
        acc_sc[...] = jnp.zeros_like(acc_sc)

        def rows_of(kt):
            return pl.ds(pl.multiple_of(kt * T, T), T)

        def scores(kt):
            return _dot(q_ref[...], k_ref[rows_of(kt), :], NT)

        def update(s, kt):
            m_prev = m_sc[...]
            m_new = jnp.maximum(m_prev, jnp.max(s, axis=-1, keepdims=True))
            alpha = jnp.exp2(m_prev - m_new)
            p = jnp.exp2(s - _lanes_to(m_new, T))
            l_sc[...] = alpha * l_sc[...] + jnp.sum(p, axis=-1, keepdims=True)
            acc_sc[...] = alpha * acc_sc[...] + _dot(p, v_ref[rows_of(kt), :], NN)
            m_sc[...] = m_new

        def masked(s):
            return jnp.where(_chunk_mask(T), s, NEG)

        @pl.when(qi == 0)
        def _():
            update(masked(scores(0)), 0)

        @pl.when(qi > 0)
        def _():
            s_a[...] = masked(scores(qi))
            s_b[...] = scores(0)
            update(s_a[...], qi)
            s_a[...] = scores(jnp.minimum(1, qi - 1))
            update(s_b[...], 0)

            def pair(j, carry):
                s_b[...] = scores(2 * j)
                update(s_a[...], 2 * j - 1)
                s_a[...] = scores(jnp.minimum(2 * j + 1, qi - 1))
                update(s_b[...], 2 * j)
                return carry

            lax.fori_loop(1, (qi + 1) // 2, pair, 0)

            @pl.when(qi % 2 == 0)
            def _():
                update(s_a[...], qi - 1)
        l = l_sc[...]
        o = acc_sc[...] / l
        o_ref[...] = o
        ob_ref[...] = o.astype(BF16)
        lse_ref[...] = m_sc[...] + jnp.log(l) * LOG2E

    h0, h1 = heads
    out_shape = [jax.ShapeDtypeStruct((S, MH * LANES), F32), jax.ShapeDtypeStruct((S, MH * LANES), BF16),
                 jax.ShapeDtypeStruct((MH, S, LANES), F32)]
    row = pl.BlockSpec((T, LANES), lambda h, i: (i, h0 + h))
    return _pallas(
        body, name=name, grid=(h1 - h0, nt),
        in_specs=[pl.BlockSpec((T, 2 * LANES), lambda h, i: (i, h0 + h)),
                  pl.BlockSpec((S, 2 * LANES), lambda h, i: (0, h0 + h)),
                  pl.BlockSpec((S, LANES), lambda h, i: (0, h0 + h))] + [_ANY] * (len(prev) + len(deps)),
        out_specs=[row, row, pl.BlockSpec((None, T, LANES), lambda h, i: (h0 + h, i, 0))],
        out_shape=out_shape,
        scratch_shapes=[pltpu.VMEM((T, LANES), F32), pltpu.VMEM((T, LANES), F32), pltpu.VMEM((T, LANES), F32),
                        pltpu.VMEM((T, T), F32), pltpu.VMEM((T, T), F32)],
        input_output_aliases={3 + i: i for i in range(len(prev))},
        compiler_params=_params(("parallel", "parallel")),
    )(qf, kf, vb, *prev, *deps)


def _attn_bwd(qf, kf, vb, dob, lse2, delta, tabs, MH, *, T, deps=()):
    S = qf.shape[0]
    nt = S // T
    scale = (QK_NOPE + QK_ROPE) ** -0.5
    n_dep = len(deps)

    def body(q_ref, k_ref, v_ref, do_ref, lse_ref, dl_ref, c_ref, s1_ref, s2_ref, *rest):
        dqa_ref, dkv_ref, dkpe_ref, dq_ref, dk_sc, dv_sc, s_a, dp_a, s_b, dp_b = rest[n_dep:]
        kj = pl.program_id(1)

        @pl.when(kj == 0)
        def _():
            dq_ref[...] = jnp.zeros_like(dq_ref)

        dk_sc[...] = jnp.zeros_like(dk_sc)
        dv_sc[...] = jnp.zeros_like(dv_sc)

        def rows_of(qt):
            return pl.ds(pl.multiple_of(qt * T, T), T)

        def products(qt):
            rows = rows_of(qt)
            return _dot(q_ref[rows, :], k_ref[...], NT), _dot(do_ref[rows, :], v_ref[...], NT)

        def update(s, dp, qt):
            rows = rows_of(qt)
            q, dov = q_ref[rows, :], do_ref[rows, :]
            p = jnp.exp2(s - _lanes_to(lse_ref[rows, :], T))
            ds = p * (dp - _lanes_to(dl_ref[rows, :], T))
            dv_sc[...] += _dot(p, dov, TN)
            dk_sc[...] += _dot(ds, q, TN)
            dq_ref[rows, :] += _dot(ds, k_ref[...], NN)

        def masked(s):
            return jnp.where(_chunk_mask(T), s, NEG)

        @pl.when(kj == nt - 1)
        def _():
            s, dp = products(kj)
            update(masked(s), dp, kj)

        @pl.when(kj < nt - 1)
        def _():
            s, dp = products(kj)
            s_a[...], dp_a[...] = masked(s), dp
            s_b[...], dp_b[...] = products(kj + 1)
            update(s_a[...], dp_a[...], kj)
            s_a[...], dp_a[...] = products(jnp.minimum(kj + 2, nt - 1))
            update(s_b[...], dp_b[...], kj + 1)

            def pair(j, carry):
                t0 = kj + 2 * j
                s_b[...], dp_b[...] = products(t0 + 1)
                update(s_a[...], dp_a[...], t0)
                s_a[...], dp_a[...] = products(jnp.minimum(t0 + 2, nt - 1))
                update(s_b[...], dp_b[...], t0 + 1)
                return carry

            lax.fori_loop(1, (nt - kj) // 2, pair, 0)

            @pl.when((nt - kj) % 2 == 1)
            def _():
                update(s_a[...], dp_a[...], nt - 1)
        dkv_ref[:, :QK_NOPE] = (dk_sc[:, :QK_NOPE] * (1.0 / LOG2E)).astype(BF16)
        dkv_ref[:, QK_NOPE:] = dv_sc[...].astype(BF16)
        dkpe_ref[...] = dk_sc[:, QK_NOPE:] * (1.0 / LOG2E)

        @pl.when(kj == nt - 1)
        def _():
            dqa_ref[:, :QK_NOPE] = (dq_ref[:, :QK_NOPE] * scale).astype(BF16)
            dqa_ref[:, QK_NOPE:] = (_rope_pe_t(dq_ref[:, QK_NOPE:], c_ref[...], s1_ref[...], s2_ref[...])
                                    * scale).astype(BF16)

    stat = pl.BlockSpec((None, S, LANES), lambda h, j: (h, 0, 0))
    tab = pl.BlockSpec((S, LANES), lambda h, j: (0, 0))
    return _pallas(
        body, name="attn_bwd", grid=(MH, nt),
        in_specs=[pl.BlockSpec((S, 2 * LANES), lambda h, j: (0, h)),
                  pl.BlockSpec((T, 2 * LANES), lambda h, j: (j, h)),
                  pl.BlockSpec((T, LANES), lambda h, j: (j, h)),
                  pl.BlockSpec((S, LANES), lambda h, j: (0, h)), stat, stat, tab, tab, tab] + [_ANY] * n_dep,
        out_specs=[pl.BlockSpec((S, 2 * LANES), lambda h, j: (0, h)),
                   pl.BlockSpec((T, 2 * LANES), lambda h, j: (j, h)),
                   pl.BlockSpec((T, LANES), lambda h, j: (j, h))],
        out_shape=[jax.ShapeDtypeStruct((S, MH * 2 * LANES), BF16), jax.ShapeDtypeStruct((S, MH * 2 * LANES), BF16),
                   jax.ShapeDtypeStruct((S, MH * LANES), F32)],
        scratch_shapes=[pltpu.VMEM((S, 2 * LANES), F32), pltpu.VMEM((T, 2 * LANES), F32), pltpu.VMEM((T, LANES), F32)]
        + [pltpu.VMEM((T, T), F32)] * 4,
        compiler_params=_params(("parallel", "arbitrary")),
    )(qf, kf, vb, dob, lse2, delta, *tabs, *deps)


def _kpe_sum(dkpe_h, tabs, MH, *, tr=256):
    S = dkpe_h.shape[0]
    tr = _tile(S, tr)

    def body(dk_ref, c_ref, s1_ref, s2_ref, dkpe_ref):
        tot = dk_ref[:, :LANES]
        for h in range(1, MH):
            tot = tot + dk_ref[:, h * LANES:(h + 1) * LANES]
        dkpe_ref[...] = _rope_pe_t(tot, c_ref[...], s1_ref[...], s2_ref[...]).astype(BF16)

    tab = pl.BlockSpec((tr, LANES), lambda i: (i, 0))
    return _pallas(
        body, name="kpe_sum", grid=(S // tr,),
        in_specs=[pl.BlockSpec((tr, MH * LANES), lambda i: (i, 0)), tab, tab, tab],
        out_specs=tab, out_shape=jax.ShapeDtypeStruct((S, LANES), BF16),
        compiler_params=_params(("parallel",)),
    )(dkpe_h, *tabs)


ROW_ALIGN = 16


def _blk(R, C, block_bytes=2 << 20):
    cap = max(ROW_ALIGN, block_bytes // (C * 4))
    for t in range(min(R, cap) // ROW_ALIGN * ROW_ALIGN, LANES - 1, -ROW_ALIGN):
        if R % t == 0:
            return t, C
    if R <= cap:
        return R, C
    tc = C
    while R * tc * 4 > block_bytes and tc % (2 * LANES) == 0:
        tc //= 2
    return R, tc


def _rows_call(fn, ins, out_dtypes, *, name, deps=()):
    R, C = ins[0].shape
    tr, tc = _blk(R, C)
    n_in, n_dep = len(ins), len(deps)

    def body(*refs):
        vals = fn(*[r[...] for r in refs[:n_in]])
        for r, v in zip(refs[n_in + n_dep:], vals):
            r[...] = v.astype(r.dtype)

    blk = pl.BlockSpec((tr, tc), lambda i, j: (i, j))
    res = _pallas(
        body, name=name, grid=(R // tr, C // tc), in_specs=[blk] * n_in + [_ANY] * n_dep,
        out_specs=[blk] * len(out_dtypes),
        out_shape=[jax.ShapeDtypeStruct((R, C), d) for d in out_dtypes],
        compiler_params=_params(("parallel", "parallel")),
    )(*ins, *deps)
    return res


def _adamw_vals(w, g, m, v):
    m = ADAM_B1 * m + (1.0 - ADAM_B1) * g
    v = ADAM_B2 * v + (1.0 - ADAM_B2) * (g * g)
    m_hat = m / (1.0 - ADAM_B1 ** ADAM_STEP)
    v_hat = v / (1.0 - ADAM_B2 ** ADAM_STEP)
    delta = -ADAM_LR * (m_hat / (jnp.sqrt(v_hat) + ADAM_EPS) + ADAM_WD * w)
    return delta, m, v


def _sum_pair(p, theirs, place, *, name):
    _, R, C = p.shape
    R2 = R // 2
    tr, tc = _blk(R2, C)
    p4 = p.reshape(N_CHIPS, 2, R2, C)

    def body(place_ref, a_ref, b_ref, o_ref):
        o_ref[...] = (a_ref[...].astype(F32) + b_ref[...].astype(F32)).astype(BF16)

    spec = pltpu.PrefetchScalarGridSpec(
        num_scalar_prefetch=1, grid=(N_CHIPS, R2 // tr, C // tc),
        in_specs=[pl.BlockSpec((None, None, tr, tc), lambda q, i, j, pr: (q, pr[0], i, j)),
                  pl.BlockSpec((None, tr, tc), lambda q, i, j, pr: (q, i, j))],
        out_specs=pl.BlockSpec((None, tr, tc), lambda q, i, j, pr: (q, i, j)))
    return _pallas(body, name=name, grid_spec=spec, out_shape=jax.ShapeDtypeStruct((N_CHIPS, R2, C), BF16),
                   compiler_params=_params(("parallel", "parallel", "parallel")))(place, p4, theirs)


def _sum_chips(p, theirs, recv, place, *, name):
    _, R, C = p.shape
    R2 = R // 2
    tr, tc = _blk(R2, C)
    p4 = p.reshape(N_CHIPS, 2, R2, C)

    def body(place_ref, a_ref, b_ref, r0_ref, r1_ref, r2_ref, o_ref):
        own = a_ref[...].astype(F32) + b_ref[...].astype(F32)
        o_ref[...] = ((own + r0_ref[...].astype(F32)) + r1_ref[...].astype(F32)) + r2_ref[...].astype(F32)

    def slot(k):
        return pl.BlockSpec((None, tr, tc), lambda i, j, pr: (k, i, j))

    spec = pltpu.PrefetchScalarGridSpec(
        num_scalar_prefetch=1, grid=(R2 // tr, C // tc),
        in_specs=[pl.BlockSpec((None, None, tr, tc), lambda i, j, pr: (pr[1], pr[0], i, j)),
                  pl.BlockSpec((None, tr, tc), lambda i, j, pr: (pr[1], i, j)), slot(0), slot(1), slot(2)],
        out_specs=pl.BlockSpec((None, tr, tc), lambda i, j, pr: (pr[0], i, j)))
    return _pallas(body, name=name, grid_spec=spec, out_shape=jax.ShapeDtypeStruct((2, R2, C), F32),
                   compiler_params=_params(("parallel", "parallel")))(place, p4, theirs, recv, recv, recv)


def _me():
    return lax.axis_index("x"), lax.axis_index("y"), lax.axis_index("c")


def _other_chips(x, y):
    return [(1 - x, y), (x, 1 - y), (1 - x, 1 - y)]


def _rcopy(src, dst, ssem, rsem, dev):
    return pltpu.make_async_remote_copy(src_ref=src, dst_ref=dst, send_sem=ssem, recv_sem=rsem,
                                        device_id=dev, device_id_type=MESH)


def _cast_into_slot(w, place, *, name, rows=None, deps=()):
    R, C = w.shape
    rows = R if rows is None else rows
    tr, tc = _blk(R, C)

    def body(place_ref, w_ref, *rest):
        rest[-1][...] = w_ref[...].astype(BF16)

    spec = pltpu.PrefetchScalarGridSpec(
        num_scalar_prefetch=1, grid=(R // tr, C // tc),
        in_specs=[pl.BlockSpec((tr, tc), lambda i, j, pr: (i, j))] + [_ANY] * len(deps),
        out_specs=pl.BlockSpec((None, tr, tc), lambda i, j, pr: (pr[1], i, j)))
    out = _pallas(body, name=name, grid_spec=spec, out_shape=jax.ShapeDtypeStruct((N_CHIPS, rows, C), BF16),
                  compiler_params=_params(("parallel", "parallel")))(place, w, *deps)
    return out.reshape(N_CHIPS, 2, rows // 2, C)


def _gather_ici_plan(bufs):
    x, y, c = _me()
    j = 2 * x + y
    plan = []
    for i, buf in enumerate(bufs):
        for k, (px, py) in enumerate(_other_chips(x, y)):
            plan.append((3 * i + k, buf.at[j, c], buf.at[j, c], (px, py, c)))
    return plan


def _forward_halves(bufs, *, name):
    n = len(bufs)

    def body(*refs):
        outs = refs[n:2 * n]
        ssem, rsem = refs[2 * n:]
        x, y, c = _me()
        sib = (x, y, 1 - c)
        cps = []
        for i in range(n):
            for k, (px, py) in enumerate(_other_chips(x, y)):
                slot = outs[i].at[2 * px + py, c]
                r = _rcopy(slot, slot, ssem.at[3 * i + k], rsem.at[3 * i + k], sib)
                r.start()
                cps.append(r)
        for r in cps:
            r.wait()

    return _pallas(
        body, name=name, in_specs=[_ANY] * n, out_specs=[_ANY] * n,
        out_shape=[jax.ShapeDtypeStruct(b.shape, b.dtype) for b in bufs],
        scratch_shapes=[pltpu.SemaphoreType.DMA((3 * n,))] * 2,
        input_output_aliases={i: i for i in range(n)},
        compiler_params=pltpu.CompilerParams(has_side_effects=True),
    )(*bufs)


_HBM = pl.BlockSpec(memory_space=pltpu.HBM)
_SEM = pl.BlockSpec(memory_space=pltpu.SEMAPHORE)
_EFFECT = pltpu.SideEffectType.DATAFLOW_SIDE_EFFECTING


def _split_start(bufs, plan, n_copies, *, name):
    n = len(bufs)

    def body(*refs):
        ssem, rsem = refs[n], refs[n + 1]
        for s, src, dst, dev in plan(refs[:n]):
            _rcopy(src, dst, ssem.at[s], rsem.at[s], dev).start()
        refs[-1][...] = jnp.zeros_like(refs[-1])

    res = _pallas(
        body, name=name, in_specs=[_HBM] * n,
        out_specs=(_SEM, _SEM, *[_HBM] * n, pl.BlockSpec(memory_space=pltpu.VMEM)),
        out_shape=(pltpu.SemaphoreType.DMA((n_copies,)), pltpu.SemaphoreType.DMA((n_copies,)),
                   *[pltpu.HBM(b.shape, b.dtype) for b in bufs], jax.ShapeDtypeStruct((8, LANES), F32)),
        input_output_aliases={i: 2 + i for i in range(n)},
        compiler_params=pltpu.CompilerParams(has_side_effects=_EFFECT),
    )(*[pltpu.with_memory_space_constraint(b, pltpu.HBM) for b in bufs])
    return res[0], res[1], list(res[2:2 + n]), res[-1]


def _split_wait(ssem, rsem, bufs, after, plan, *, name):
    n = len(bufs)

    def body(*refs):
        ssem_ref, rsem_ref = refs[n], refs[n + 1]
        for s, src, dst, dev in plan(refs[:n]):
            cp = _rcopy(src, dst, ssem_ref.at[s], rsem_ref.at[s], dev)
            cp.wait_send()
            cp.wait_recv()

    return list(_pallas(
        body, name=name, in_specs=[_HBM] * n + [_SEM, _SEM, _ANY], out_specs=[_HBM] * n,
        out_shape=[pltpu.HBM(b.shape, b.dtype) for b in bufs],
        input_output_aliases={i: i for i in range(n)},
        compiler_params=pltpu.CompilerParams(has_side_effects=_EFFECT),
    )(*bufs, ssem, rsem, after))


def _forward_plan(bufs):
    x, y, c = _me()
    plan = []
    for i, buf in enumerate(bufs):
        for k, (px, py) in enumerate(_other_chips(x, y)):
            plan.append((3 * i + k, buf.at[2 * px + py, c], buf.at[2 * px + py, c], (x, y, 1 - c)))
    return plan


def _join_plan(bufs):
    x, y, c = _me()
    return [(i, buf.at[c], buf.at[c], (x, y, 1 - c)) for i, buf in enumerate(bufs)]


def _swap_plan(n):
    def plan(bufs):
        x, y, c = _me()
        return [(i, bufs[i].at[:, 1 - c], bufs[n + i], (x, y, 1 - c)) for i in range(n)]
    return plan


def _scatter_plan(n):
    def plan(bufs):
        x, y, c = _me()
        out = []
        for i in range(n):
            for k, (px, py) in enumerate(_other_chips(x, y)):
                out.append((3 * i + k, bufs[i].at[2 * px + py], bufs[n + i].at[k], (px, py, c)))
        return out
    return plan


def _allreduce_small(parts, loss11):
    n = len(parts)
    widths = [p.shape[1] for p in parts]
    total = sum(widths) + LANES

    def body(*refs):
        o_ref, mine, buf, ssem, rsem = refs[n + 1:]
        x, y, c = _me()
        me = 4 * x + 2 * y + c
        off = 0
        for r, w in zip(refs[:n], widths):
            mine[:, off:off + w] = r[...]
            off += w
        mine[:, off:] = jnp.broadcast_to(refs[n][...], (1, LANES))
        buf[me] = mine[...]
        cps = []
        for k in range(1, 8):
            peer = (x ^ (k >> 2), y ^ ((k >> 1) & 1), c ^ (k & 1))
            r = _rcopy(mine, buf.at[me], ssem.at[k - 1], rsem.at[k - 1], peer)
            r.start()
            cps.append(r)
        for k in range(1, 8):
            peer = (x ^ (k >> 2), y ^ ((k >> 1) & 1), c ^ (k & 1))
            pid = 4 * peer[0] + 2 * peer[1] + peer[2]
            _rcopy(mine, buf.at[pid], ssem.at[k - 1], rsem.at[k - 1], peer).wait_recv()
        for r in cps:
            r.wait_send()
        tot = buf[0]
        for d in range(1, 8):
            tot = tot + buf[d]
        o_ref[...] = tot

    vm = pl.BlockSpec(memory_space=pltpu.VMEM)
    return _pallas(
        body, name="allreduce_small", in_specs=[vm] * (n + 1), out_specs=vm,
        out_shape=jax.ShapeDtypeStruct((1, total), F32),
        scratch_shapes=[pltpu.VMEM((1, total), F32), pltpu.VMEM((8, 1, total), F32),
                        pltpu.SemaphoreType.DMA((7,)), pltpu.SemaphoreType.DMA((7,))],
        compiler_params=pltpu.CompilerParams(has_side_effects=True),
    )(*parts, loss11)


def _adamw_small(red, ws, ms, vs):
    n = len(ws)

    def body(*refs):
        red_ref = refs[0]
        outs = refs[1 + 3 * n:]
        off = 0
        for i in range(n):
            w = refs[1 + i].shape[1]
            g = red_ref[:, off:off + w]
            d, m, v = _adamw_vals(refs[1 + i][...], g, refs[1 + n + i][...], refs[1 + 2 * n + i][...])
            for o, val in zip(outs[4 * i:4 * i + 4], (g, d, m, v)):
                o[...] = val
            off += w

    vm = pl.BlockSpec(memory_space=pltpu.VMEM)
    res = _pallas(
        body, name="adamw_small", in_specs=[vm] * (1 + 3 * n), out_specs=[vm] * (4 * n),
        out_shape=[jax.ShapeDtypeStruct(w.shape, F32) for w in ws for _ in range(4)],
    )(red, *ws, *ms, *vs)
    return [res[4 * i:4 * i + 4] for i in range(n)]


def _rope_tables(positions, S):
    pos = positions.reshape(S, 1).astype(F32)
    half = RET_QK // 2
    inv = ROPE_THETA ** (-jnp.arange(half, dtype=F32) / half)
    ang = pos * inv
    cosr = jnp.concatenate([jnp.cos(ang), jnp.cos(ang)], axis=1)
    sinr = jnp.concatenate([-jnp.sin(ang), jnp.sin(ang)], axis=1)
    half = QK_ROPE // 2
    inv = ROPE_THETA ** (-jnp.arange(half, dtype=F32) / half)
    ang = pos * inv
    z = jnp.zeros((S, half), F32)
    c = jnp.concatenate([jnp.cos(ang), jnp.cos(ang), z, z], axis=1)
    s1 = jnp.concatenate([-jnp.sin(ang), z, z, z], axis=1)
    s2 = jnp.concatenate([z, jnp.sin(ang), z, z], axis=1)
    return cosr, sinr, (c, s1, s2)


def _cat_cols(g):
    return jnp.concatenate([g[j] for j in range(N_CHIPS)], axis=1)


def _split_cols(w):
    return jnp.stack(jnp.split(w, N_CHIPS, axis=1))


def kernel(x, positions, norm_mix_g, w_in, ret_norm_g, w_ret_o, q_a_norm_g, w_q_b, kv_a_norm_g, w_kv_b, w_mla_o, w_out, norm_mlp_g, w_up, w_down, norm_f_g, loss_target, m_norm_mix_g, m_w_in, m_ret_norm_g, m_w_ret_o, m_q_a_norm_g, m_w_q_b, m_kv_a_norm_g, m_w_kv_b, m_w_mla_o, m_w_out, m_norm_mlp_g, m_w_up, m_w_down, m_norm_f_g, v_norm_mix_g, v_w_in, v_ret_norm_g, v_w_ret_o, v_q_a_norm_g, v_w_q_b, v_kv_a_norm_g, v_w_kv_b, v_w_mla_o, v_w_out, v_norm_mlp_g, v_w_up, v_w_down, v_norm_f_g):
    S, D = x.shape[1], x.shape[2]
    RVW = w_ret_o.shape[1] * N_CHIPS
    RH = RVW // RET_V
    RQW = RH * RET_QK
    MVW = w_mla_o.shape[1] * N_CHIPS
    MH = MVW // V_HEAD
    QL, KVL = w_q_b.shape[1], w_kv_b.shape[1]
    T_RET = _tile(S, 256)
    T_ATT = _tile(S, 512)

    xs = x.reshape(S, D)
    tgt = loss_target.reshape(S, D)
    cosr, sinr, pe_tabs = _rope_tables(positions, S)
    lgam = jnp.log(1.0 - 2.0 ** (-5.0 - jnp.arange(RH, dtype=F32)))
    lgam = jnp.broadcast_to(lgam[:, None, None], (RH, 8, LANES))

    big = ("w_in", "w_ret_o", "w_q_b", "w_kv_b", "w_mla_o", "w_out", "w_up", "w_down")
    w_sh = dict(w_in=w_in[0].T, w_ret_o=w_ret_o[0], w_q_b=w_q_b[0], w_kv_b=w_kv_b[0], w_mla_o=w_mla_o[0],
                w_out=w_out[0], w_up=w_up[0], w_down=w_down[0])
    m_sh = dict(w_in=m_w_in[0].T, w_ret_o=m_w_ret_o[0], w_q_b=m_w_q_b[0], w_kv_b=m_w_kv_b[0],
                w_mla_o=m_w_mla_o[0], w_out=m_w_out[0], w_up=m_w_up[0], w_down=m_w_down[0])
    v_sh = dict(w_in=v_w_in[0].T, w_ret_o=v_w_ret_o[0], w_q_b=v_w_q_b[0], w_kv_b=v_w_kv_b[0],
                w_mla_o=v_w_mla_o[0], w_out=v_w_out[0], w_up=v_w_up[0], w_down=v_w_down[0])
    col_sharded = ("w_q_b", "w_kv_b", "w_up")
    c_sh = w_in.shape[2]
    c_pad = -(-c_sh // 64) * 64
    place = jnp.stack([lax.axis_index("c"), 2 * lax.axis_index("x") + lax.axis_index("y")]).astype(jnp.int32)

    def whole(k, g):
        g = g.reshape(N_CHIPS, w_sh[k].shape[0], w_sh[k].shape[1])
        if k == "w_up":
            return g
        return _cat_cols(g) if k in col_sharded else g.reshape(-1, g.shape[2])

    first = ("w_in", "w_q_b", "w_kv_b")
    later = ("w_ret_o", "w_mla_o", "w_out", "w_up", "w_down")
    first_bufs = [_cast_into_slot(w_sh[k], place, name="cast_" + k, rows=c_pad if k == "w_in" else None)
                  for k in first]
    first_ssem, first_rsem, first_bufs, first_token = _split_start(
        first_bufs, _gather_ici_plan, 3 * len(first), name="gather_first_start")
    later_bufs = [_cast_into_slot(w_sh[k], place, name="cast_" + k, deps=(first_token,)) for k in later[:-1]]
    first_bufs = _split_wait(first_ssem, first_rsem, first_bufs, later_bufs[-1], _gather_ici_plan,
                             name="gather_first_wait")
    got = _forward_halves(first_bufs, name="gather_first_forward")
    full = {k: whole(k, g) for k, g in zip(first[1:], got[1:])}
    later_bufs.append(_cast_into_slot(w_sh[later[-1]], place, name="cast_" + later[-1], deps=(got[0],)))
    later_ssem, later_rsem, later_bufs, later_token = _split_start(
        later_bufs, _gather_ici_plan, 3 * len(later), name="gather_later_start")

    o_rq, o_rk, o_rv, o_rg = 0, RQW, 2 * RQW, 2 * RQW + RVW
    o_cq = 2 * RQW + 2 * RVW
    o_ckv, o_kpe = o_cq + QL, o_cq + QL + KVL
    o_gr = o_kpe + QK_ROPE
    o_gm = o_gr + D
    n_ret = RH * RET_HEAD_COLS
    off_gate, off_cq, off_ckv = n_ret, n_ret + 2 * D, n_ret + 2 * D + QL
    gate_tile = _tile(D, 1024)
    n_a = off_ckv + KVL
    runs = []
    for h in range(RH):
        base = h * RET_HEAD_COLS
        runs += [(o_rq + h * RET_QK, RET_QK, base), (o_rk + h * RET_QK, RET_QK, base + RET_QK),
                 (o_rv + h * RET_V, RET_V, base + 2 * RET_QK), (o_rg + h * RET_V, RET_V, base + 2 * RET_QK + RET_V)]
    for t in range(D // gate_tile):
        runs += [(o_gr + t * gate_tile, gate_tile, off_gate + 2 * t * gate_tile),
                 (o_gm + t * gate_tile, gate_tile, off_gate + (2 * t + 1) * gate_tile)]
    runs += [(o_cq, QL, off_cq), (o_ckv, KVL, off_ckv), (o_kpe, QK_ROPE, n_a)]

    def take(parts, start, width):
        out, lo = [], 0
        for p in parts:
            hi = lo + p.shape[0]
            a, b = max(start, lo), min(start + width, hi)
            if a < b:
                out.append(p[a - lo:b - lo])
            lo = hi
        return out

    wi = [got[0].reshape(N_CHIPS, c_pad, D)[jj, :c_sh] for jj in range(N_CHIPS)]
    here = sorted(runs, key=lambda r: r[2])
    wa = jnp.concatenate([p for s0, w, _ in here[:-1] for p in take(wi, s0, w)], axis=0)
    wkpe = jnp.concatenate(take(wi, o_kpe, QK_ROPE) + [jnp.zeros((LANES - QK_ROPE, D), BF16)], axis=0)
    wq = jnp.pad(full["w_q_b"].reshape(QL, MH, QK_NOPE + QK_ROPE),
                 ((0, 0), (0, 0), (0, LANES - QK_ROPE))).reshape(QL, MH * 2 * LANES)
    wkv = full["w_kv_b"]

    u, rstd0 = _rmsnorm_fwd(xs, norm_mix_g, name="norm_mix")
    proj = _mm(u, wa, mode="nt", outs=[F32], name="in_proj", deps=(later_token,))
    kpe = _mm(u, wkpe, mode="nt", outs=[F32], name="kpe_proj")
    ry, gated, states = _ret_fwd(proj, cosr, sinr, lgam, ret_norm_g, RH, T=T_RET)
    cqn, ckvn, rstd_q, rstd_kv = _norm_pair_fwd(proj, q_a_norm_g, kv_a_norm_g, off_cq)
    qf, kf, vb = _qkv_proj(cqn, ckvn, wq, wkv, kpe, pe_tabs, MH)
    first_half = _attn_fwd(qf, kf, vb, MH, T=T_ATT, heads=(0, MH // 2), name="attn_fwd_a")
    later_bufs = _split_wait(later_ssem, later_rsem, later_bufs, first_half[0], _gather_ici_plan,
                             name="gather_later_wait")
    fwd_ssem, fwd_rsem, later_bufs, fwd_token = _split_start(
        later_bufs, _forward_plan, 3 * len(later), name="gather_later_forward_start")
    my, my_b, lse2 = _attn_fwd(qf, kf, vb, MH, T=T_ATT, heads=(MH // 2, MH), name="attn_fwd_b",
                               prev=first_half, deps=(fwd_token,))
    later_bufs = _split_wait(fwd_ssem, fwd_rsem, later_bufs, my, _forward_plan, name="gather_later_forward_wait")
    full.update({k: whole(k, g) for k, g in zip(later, later_bufs)})
    y_ret = _mm(gated, full["w_ret_o"], mode="nn", outs=[BF16], name="ret_o")
    y_mla, merged = _mm(my_b, full["w_mla_o"], mode="nn", outs=[BF16, BF16], name="mla_o",
                        epi=lambda acc, gr, gm, yr: (acc, _sigmoid(gr) * yr + _sigmoid(gm) * acc),
                        extras=((proj, off_gate, 2), (proj, off_gate + gate_tile, 2), y_ret), tn=gate_tile)
    h1 = _mm(merged, full["w_out"], mode="nn", outs=[F32], name="out_proj",
             epi=lambda acc, r: (acc + r,), extras=(xs,))
    n1, rstd1 = _rmsnorm_fwd(h1, norm_mlp_g, name="norm_mlp")

    def up_epi(acc):
        r = jnp.maximum(acc, 0.0)
        return acc, r * r

    z, act = _mm(n1, full["w_up"], mode="nn", outs=[F32, BF16], name="up_proj", epi=up_epi)
    h2 = _mm(act, full["w_down"], mode="nn", outs=[F32], name="down_proj",
             epi=lambda acc, r: (acc + r,), extras=(h1,))
    loss11, dh2, dh2_b, g_norm_f = _final_loss(h2, norm_f_g.reshape(1, D), tgt)

    dz = _mm(dh2_b, full["w_down"], mode="nt", outs=[BF16], name="down_bwd_x",
             epi=lambda acc, zz: (acc * (2.0 * jnp.maximum(zz, 0.0)),), extras=(z,))
    g_w_down = _mm(act, dh2_b, mode="tn", outs=[BF16], name="down_bwd_w")
    dn1 = _mm(dz, full["w_up"], mode="nt", outs=[F32], name="up_bwd_x")
    g_w_up = _mm(n1, dz, mode="tn", outs=[BF16], name="up_bwd_w", out_shards=True)

    def scatter_begin(tag, sums):
        lands = [lax.empty((3,) + s.shape[1:], s.dtype) for s in sums]
        return _split_start(sums + lands, _scatter_plan(len(sums)), 3 * len(sums), name="scatter_" + tag + "_start")

    def swap_begin(tag, grads):
        views = [g if g.ndim == 3 else g.reshape(N_CHIPS, g.shape[0] // N_CHIPS, g.shape[1]) for g in grads]
        views = [v.reshape(N_CHIPS, 2, v.shape[1] // 2, v.shape[2]) for v in views]
        lands = [lax.empty((N_CHIPS,) + v.shape[2:], v.dtype) for v in views]
        return _split_start(views + lands, _swap_plan(len(views)), len(views), name="swap_" + tag + "_start")

    def swap_end(tag, names, handle, after):
        n = len(names)
        bufs = _split_wait(handle[0], handle[1], handle[2], after, _swap_plan(n), name="swap_" + tag + "_wait")
        pcs = [b.reshape(N_CHIPS, 2 * b.shape[2], b.shape[3]) for b in bufs[:n]]
        sums = [_sum_pair(p, t, place, name="sum_pair_" + k) for k, p, t in zip(names, pcs, bufs[n:])]
        return pcs, bufs[n:], sums

    g1 = ("w_up", "w_down")
    swap1 = swap_begin("g1", (g_w_up, g_w_down))
    dh1, g_norm_mlp, dh1_b = _rmsnorm_bwd(dn1, h1, rstd1, norm_mlp_g, name="norm_mlp_bwd", res=dh2,
                                          deps=(swap1[3],), bf16_copy=1)
    def merge_bwd_epi(dm, gr, gm, yr, ym):
        sr, sm = _sigmoid(gr), _sigmoid(gm)
        return dm * sr, dm * sm, jnp.concatenate([dm * yr * sr * (1.0 - sr), dm * ym * sm * (1.0 - sm)], axis=1)

    dy_ret, dy_mla, dproj = _mm(
        dh1_b, full["w_out"], mode="nt", outs=[BF16, BF16], name="out_bwd_x", epi=merge_bwd_epi,
        extras=((proj, off_gate, 2), (proj, off_gate + gate_tile, 2), y_ret, y_mla), tm=512, tn=gate_tile,
        more_outs=lambda tm, tn: [(jax.ShapeDtypeStruct(proj.shape, BF16),
                                   pl.BlockSpec((tm, 2 * tn), lambda i, j, k: (i, off_gate // (2 * tn) + j)))])
    pcs1, theirs1, sums1 = swap_end("g1", g1, swap1, dy_ret)
    ssem1, rsem1, bufs1, token1 = scatter_begin("g1", sums1)
    g_w_out = _mm(merged, dh1_b, mode="tn", outs=[BF16], name="out_bwd_w", deps=(token1,))
    dgated = _mm(dy_ret, full["w_ret_o"], mode="nt", outs=[F32], name="ret_o_bwd_x")
    g_w_ret_o = _mm(gated, dy_ret, mode="tn", outs=[BF16], name="ret_o_bwd_w")
    dproj, g_ret_norm = _ret_bwd(proj, cosr, sinr, lgam, ret_norm_g, ry, dgated, states, dproj, RH, T=T_RET)
    def delta_epi(acc, o):
        rows = acc.shape[0]
        return acc, [jnp.broadcast_to(jnp.sum(acc[:, lo:lo + V_HEAD] * o[:, lo:lo + V_HEAD], axis=-1, keepdims=True),
                                      (rows, LANES)) for lo in range(0, acc.shape[1], V_HEAD)]

    dob, delta = _mm(dy_mla, full["w_mla_o"], mode="nt", outs=[BF16], name="mla_o_bwd_x", epi=delta_epi,
                     extras=(my,), more_outs=lambda tm, tn: [
                         (jax.ShapeDtypeStruct((MH, S, LANES), F32),
                          pl.BlockSpec((tn // V_HEAD, tm, LANES), lambda i, j, k: (j, i, 0)))])
    g_w_mla_o = _mm(my_b, dy_mla, mode="tn", outs=[BF16], name="mla_o_bwd_w")
    g2 = ("w_out", "w_ret_o", "w_mla_o")
    swap2 = swap_begin("g2", (g_w_out, g_w_ret_o, g_w_mla_o))
    dq_all, dkv_all, dkpe_h = _attn_bwd(qf, kf, vb, dob, lse2, delta, pe_tabs, MH, T=T_ATT, deps=(swap2[3],))
    pcs2, theirs2, sums2 = swap_end("g2", g2, swap2, dkv_all)
    ssem2, rsem2, bufs2, token2 = scatter_begin("g2", sums2)
    dkpe = _kpe_sum(dkpe_h, pe_tabs, MH)
    dcqn = _mm(dq_all, wq, mode="nt", outs=[F32], name="q_bwd_x", deps=(token2,))
    g_wq = _mm(cqn, dq_all, mode="tn", outs=[BF16], name="q_bwd_w")
    dckvn = _mm(dkv_all, wkv, mode="nt", outs=[F32], name="kv_bwd_x")
    g_wkv = _mm(ckvn, dkv_all, mode="tn", outs=[BF16], name="kv_bwd_w")
    dproj, g_q_a, g_kv_a = _norm_pair_bwd(dcqn, dckvn, proj, rstd_q, rstd_kv, q_a_norm_g, kv_a_norm_g, dproj, off_cq)
    g_wa = _mm(dproj, u, mode="tn", outs=[BF16], name="in_bwd_w")
    g_wkpe = _mm(dkpe, u, mode="tn", outs=[BF16], name="kpe_bwd_w")

    there = sorted(runs)
    g_parts = [g_wa, g_wkpe]
    g_w_in = jnp.stack([jnp.concatenate(
        [p for s0, w, d0 in there for a, b in [(max(s0, jj * c_sh), min(s0 + w, (jj + 1) * c_sh))] if a < b
         for p in take(g_parts, d0 + a - s0, b - a)] + [jnp.zeros((c_pad - c_sh, D), BF16)], axis=0)
        for jj in range(N_CHIPS)])
    gq = g_wq.reshape(QL, MH, 2 * LANES)[:, :, :QK_NOPE + QK_ROPE].reshape(QL, MH * (QK_NOPE + QK_ROPE))
    g3 = ("w_in", "w_q_b", "w_kv_b")
    swap3 = swap_begin("g3", (g_w_in, _split_cols(gq), _split_cols(g_wkv)))
    pcs3, theirs3, sums3 = swap_end("g3", g3, swap3, swap3[3])
    ssem3, rsem3, bufs3, token3 = scatter_begin("g3", sums3)

    def chip_sums(names, pcs, theirs, recv):
        return [_sum_chips(p, t, r, place, name="sum_chips_" + k) for k, p, t, r in zip(names, pcs, theirs, recv)]

    bufs1 = _split_wait(ssem1, rsem1, bufs1, token3, _scatter_plan(len(g1)), name="scatter_g1_wait")
    bufs2 = _split_wait(ssem2, rsem2, bufs2, token3, _scatter_plan(len(g2)), name="scatter_g2_wait")
    halves12 = chip_sums(g1, pcs1, theirs1, bufs1[len(g1):]) + chip_sums(g2, pcs2, theirs2, bufs2[len(g2):])
    jssem, jrsem, halves12, join_token = _split_start(halves12, _join_plan, len(halves12), name="join_g12_start")
    du = _mm(dproj, wa, mode="nn", outs=[F32], name="in_bwd_x", tk=2816, tail=(dkpe, wkpe), deps=(join_token,))
    dx, g_norm_mix = _rmsnorm_bwd(du, xs, rstd0, norm_mix_g, name="norm_mix_bwd", res=dh1)

    bufs3 = _split_wait(ssem3, rsem3, bufs3, dx, _scatter_plan(len(g3)), name="scatter_g3_wait")
    halves12 = _split_wait(jssem, jrsem, halves12, dx, _join_plan, name="join_g12_wait")
    j3ssem, j3rsem, halves3, join3_token = _split_start(
        chip_sums(g3, pcs3, theirs3, bufs3[len(g3):]), _join_plan, len(g3), name="join_g3_start")

    small = ("norm_mix_g", "ret_norm_g", "q_a_norm_g", "kv_a_norm_g", "norm_mlp_g", "norm_f_g")
    g_small = [g_norm_mix, g_ret_norm, g_q_a, g_kv_a, g_norm_mlp, g_norm_f]
    red = _allreduce_small(g_small, loss11)
    loss = red[0, red.shape[1] - 1]
    w_small = [norm_mix_g, ret_norm_g, q_a_norm_g, kv_a_norm_g, norm_mlp_g, norm_f_g]
    m_small = [m_norm_mix_g, m_ret_norm_g, m_q_a_norm_g, m_kv_a_norm_g, m_norm_mlp_g, m_norm_f_g]
    v_small = [v_norm_mix_g, v_ret_norm_g, v_q_a_norm_g, v_kv_a_norm_g, v_norm_mlp_g, v_norm_f_g]
    row = lambda a: a.reshape(1, -1)
    upd = _adamw_small(red, [row(a) for a in w_small], [row(a) for a in m_small], [row(a) for a in v_small])
    out_g, out_d, out_m, out_v = {}, {}, {}, {}
    for k, wv, (g_, d_, m_, v_) in zip(small, w_small, upd):
        out_g[k], out_d[k], out_m[k], out_v[k] = [a.reshape(wv.shape) for a in (g_, d_, m_, v_)]

    def adamw_shard(k, joined, deps=()):
        g = joined.reshape(2 * joined.shape[1], joined.shape[2])
        res = _rows_call(lambda w, g, m, v: (g,) + _adamw_vals(w, g, m, v),
                         [w_sh[k], g, m_sh[k], v_sh[k]], [F32] * 4, name="adamw_" + k, deps=deps)
        if k == "w_in":
            res = [r.T for r in res]
        out_g[k], out_d[k], out_m[k], out_v[k] = [r[None] for r in res]
        return res[0]

    for k, joined in zip(g1 + g2, halves12):
        last = adamw_shard(k, joined, deps=(join3_token,))
    halves3 = _split_wait(j3ssem, j3rsem, halves3, last, _join_plan, name="join_g3_wait")
    for k, joined in zip(g3, halves3):
        adamw_shard(k, joined)

    order = ("norm_mix_g", "w_in", "ret_norm_g", "w_ret_o", "q_a_norm_g", "w_q_b", "kv_a_norm_g", "w_kv_b",
             "w_mla_o", "w_out", "norm_mlp_g", "w_up", "w_down", "norm_f_g")
    return (loss, dx.reshape(1, S, D), *[out_g[k] for k in order], *[out_d[k] for k in order],
            *[out_m[k] for k in order], *[out_v[k] for k in order])
```

```python
import math

import jax
import jax.numpy as jnp
from jax import lax
from jax.experimental import pallas as pl
from jax.experimental.pallas import tpu as pltpu

F32 = jnp.float32
BF16 = jnp.bfloat16

EPS = 1e-6
ROPE_THETA = 10000.0
CHUNK = 64
RET_QK = 128
RET_V = 256
RET_HEAD_COLS = 2 * RET_QK + 2 * RET_V
QK_NOPE = 128
QK_ROPE = 64
V_HEAD = 128
LANES = 128
LOG2E = math.log2(math.e)

ADAM_LR = 0.001
ADAM_B1 = 0.9
ADAM_B2 = 0.999
ADAM_EPS = 1e-08
ADAM_WD = 0.01
ADAM_STEP = 10

N_CHIPS = 4
VMEM_LIMIT = 56 * 1024 * 1024
MESH = pl.DeviceIdType.MESH
NEG = -1e30


def _pallas(body, **kw):
    return pl.pallas_call(body, **kw)


def _params(sem=None):
    return pltpu.CompilerParams(dimension_semantics=sem, vmem_limit_bytes=VMEM_LIMIT)


def _tile(n, want):
    t = min(n, want)
    while n % t:
        t //= 2
    return t


_ANY = pl.BlockSpec(memory_space=pl.ANY)
TN_BF16_TK = 4096


def _mm(a, b, *, mode, outs, name, epi=None, extras=(), deps=(), out_shards=False, more_outs=None, tail=None,
        tm=1024, tn=1024, tk=2048):
    shards = b.shape[0] if b.ndim == 3 else 1
    brows, bcols = b.shape[-2], b.shape[-1] * shards
    if mode == "nn":
        (M, K), N = a.shape, bcols
    elif mode == "nt":
        (M, K), N = a.shape, brows
    else:
        (K, M), N = a.shape, bcols
    if mode == "tn" and a.dtype == BF16 and b.dtype == BF16:
        tk = max(tk, TN_BF16_TK)
    tm = _tile(M, tm)
    tn = _tile(N // (shards if mode == "nn" else 1) // (N_CHIPS if out_shards else 1), tn)
    tk = _tile(K // (shards if mode == "nt" else 1), tk)
    nk = K // tk
    if mode == "nn":
        a_spec = pl.BlockSpec((tm, tk), lambda i, j, k: (i, k))
        dims = (((1,), (0,)), ((), ()))
        if shards > 1:
            per = N // shards // tn
            b_spec = pl.BlockSpec((None, tk, tn), lambda i, j, k: (j // per, k, j % per))
        else:
            b_spec = pl.BlockSpec((tk, tn), lambda i, j, k: (k, j))
    elif mode == "nt":
        a_spec = pl.BlockSpec((tm, tk), lambda i, j, k: (i, k))
        dims = (((1,), (1,)), ((), ()))
        if shards > 1:
            per = K // shards // tk
            b_spec = pl.BlockSpec((None, tn, tk), lambda i, j, k: (k // per, j, k % per))
        else:
            b_spec = pl.BlockSpec((tn, tk), lambda i, j, k: (j, k))
    else:
        assert shards == 1
        a_spec = pl.BlockSpec((tk, tm), lambda i, j, k: (k, i))
        b_spec = pl.BlockSpec((tk, tn), lambda i, j, k: (k, j))
        dims = (((0,), (0,)), ((), ()))
    if out_shards:
        assert not extras
        oper = N // N_CHIPS // tn
        o_spec = pl.BlockSpec((None, tm, tn), lambda i, j, k: (j // oper, i, j % oper))
        o_shape = (N_CHIPS, M, N // N_CHIPS)
    else:
        o_spec = pl.BlockSpec((tm, tn), lambda i, j, k: (i, j))
        o_shape = (M, N)
    more = [] if more_outs is None else more_outs(tm, tn)
    ex_arrays = [e[0] if isinstance(e, tuple) else e for e in extras]
    ex_specs = [pl.BlockSpec((tm, tn), lambda i, j, k, off=e[1] // tn, st=e[2]: (i, off + st * j))
                if isinstance(e, tuple) else o_spec for e in extras]
    n_ex, n_out, n_dep = len(extras), len(outs) + len(more), len(deps)
    if epi is None:
        epi = lambda acc: (acc,)
    tails, tail_specs = [], []
    if tail is not None:
        assert mode == "nn"
        tails = list(tail)
        k2 = tail[0].shape[1]
        tail_specs = [pl.BlockSpec((tm, k2), lambda i, j, k: (i, 0)), pl.BlockSpec((k2, tn), lambda i, j, k: (0, j))]
    n_tail = len(tails)

    def body(*refs):
        a_ref, b_ref = refs[0], refs[1]
        ex_refs = refs[2:2 + n_ex]
        t_refs = refs[2 + n_ex:2 + n_ex + n_tail]
        first_out = 2 + n_ex + n_tail + n_dep
        o_refs = refs[first_out:first_out + n_out]
        part = lax.dot_general(a_ref[...].astype(BF16), b_ref[...].astype(BF16), dims,
                               preferred_element_type=F32)

        def finish(acc):
            if n_tail:
                acc = acc + lax.dot_general(t_refs[0][...].astype(BF16), t_refs[1][...].astype(BF16), dims,
                                            preferred_element_type=F32)
            vals = epi(acc, *[r[...] for r in ex_refs])
            for r, v in zip(o_refs, vals):
                if isinstance(v, (list, tuple)):
                    for lead, piece in enumerate(v):
                        r[lead] = piece.astype(r.dtype)
                else:
                    r[...] = v.astype(r.dtype)

        if nk == 1:
            finish(part)
        else:
            acc_ref = refs[-1]
            k = pl.program_id(2)

            @pl.when(k == 0)
            def _():
                acc_ref[...] = part

            @pl.when(k > 0)
            def _():
                acc_ref[...] += part

            @pl.when(k == nk - 1)
            def _():
                finish(acc_ref[...])

    res = _pallas(
        body, name=name, grid=(M // tm, N // tn, nk),
        in_specs=[a_spec, b_spec] + ex_specs + tail_specs + [_ANY] * n_dep,
        out_specs=[o_spec] * len(outs) + [spec for _, spec in more],
        out_shape=[jax.ShapeDtypeStruct(o_shape, d) for d in outs] + [shape for shape, _ in more],
        scratch_shapes=[pltpu.VMEM((tm, tn), F32)] if nk > 1 else [],
        compiler_params=_params(("parallel", "parallel", "arbitrary")),
    )(a, b, *ex_arrays, *tails, *deps)
    return res[0] if n_out == 1 else res


def _rmsnorm_fwd(x, g, *, name, tr=256):
    S, W = x.shape
    tr = _tile(S, tr)

    def body(x_ref, g_ref, y_ref, r_ref):
        xv = x_ref[...]
        rstd = lax.rsqrt(jnp.mean(xv * xv, axis=-1, keepdims=True) + EPS)
        y_ref[...] = (xv * rstd * g_ref[...]).astype(BF16)
        r_ref[...] = rstd

    return _pallas(
        body, name=name, grid=(S // tr,),
        in_specs=[pl.BlockSpec((tr, W), lambda i: (i, 0)), pl.BlockSpec((1, W), lambda i: (0, 0))],
        out_specs=[pl.BlockSpec((tr, W), lambda i: (i, 0)), pl.BlockSpec((tr, 1), lambda i: (i, 0))],
        out_shape=[jax.ShapeDtypeStruct((S, W), BF16), jax.ShapeDtypeStruct((S, 1), F32)],
        compiler_params=_params(("parallel",)),
    )(x, g)


def _rmsnorm_bwd(dy, x, rstd, g, *, name, res=None, deps=(), bf16_copy=0, tr=256):
    S, W = x.shape
    tr = _tile(S, tr)
    has_res = res is not None

    def body(*refs):
        dy_ref, x_ref, r_ref, g_ref = refs[:4]
        dx_ref, dg_ref = refs[-2 - bf16_copy], refs[-1 - bf16_copy]
        rstd_v = r_ref[...]
        xhat = x_ref[...] * rstd_v
        dyv = dy_ref[...].astype(F32)
        dyg = dyv * g_ref[...]
        dx = rstd_v * (dyg - xhat * jnp.mean(dyg * xhat, axis=-1, keepdims=True))
        if has_res:
            dx = dx + refs[4][...]
        dx_ref[...] = dx.astype(dx_ref.dtype)
        if bf16_copy:
            refs[-1][...] = dx.astype(BF16)
        part = jnp.sum(dyv * xhat, axis=0, keepdims=True)

        @pl.when(pl.program_id(0) == 0)
        def _():
            dg_ref[...] = part

        @pl.when(pl.program_id(0) > 0)
        def _():
            dg_ref[...] += part

    row = pl.BlockSpec((tr, W), lambda i: (i, 0))
    ins = [dy, x, rstd, g] + ([res] if has_res else [])
    in_specs = [row, row, pl.BlockSpec((tr, 1), lambda i: (i, 0)),
                pl.BlockSpec((1, W), lambda i: (0, 0))] + ([row] if has_res else [])
    ins += list(deps)
    in_specs += [_ANY] * len(deps)
    return _pallas(
        body, name=name, grid=(S // tr,), in_specs=in_specs,
        out_specs=[row, pl.BlockSpec((1, W), lambda i: (0, 0))] + [row] * bf16_copy,
        out_shape=[jax.ShapeDtypeStruct((S, W), F32), jax.ShapeDtypeStruct((1, W), F32)]
        + [jax.ShapeDtypeStruct((S, W), BF16)] * bf16_copy,
        compiler_params=_params(("arbitrary",)),
    )(*ins)


def _norm_pair_fwd(proj, g_a, g_b, off, *, tr=512):
    S = proj.shape[0]
    wa_, wb_ = g_a.shape[1], g_b.shape[1]
    W = wa_ + wb_
    tr = _tile(S, tr)

    def body(x_ref, ga_ref, gb_ref, ya_ref, yb_ref, ra_ref, rb_ref):
        for lo, hi, g_ref, y_ref, r_ref in ((0, wa_, ga_ref, ya_ref, ra_ref), (wa_, W, gb_ref, yb_ref, rb_ref)):
            xv = x_ref[:, lo:hi]
            rstd = lax.rsqrt(jnp.mean(xv * xv, axis=-1, keepdims=True) + EPS)
            y_ref[...] = (xv * rstd * g_ref[...]).astype(BF16)
            r_ref[...] = rstd

    one = pl.BlockSpec((tr, 1), lambda i: (i, 0))
    return _pallas(
        body, name="norm_qkv", grid=(S // tr,),
        in_specs=[pl.BlockSpec((tr, W), lambda i: (i, off // W)), pl.BlockSpec((1, wa_), lambda i: (0, 0)),
                  pl.BlockSpec((1, wb_), lambda i: (0, 0))],
        out_specs=[pl.BlockSpec((tr, wa_), lambda i: (i, 0)), pl.BlockSpec((tr, wb_), lambda i: (i, 0)), one, one],
        out_shape=[jax.ShapeDtypeStruct((S, wa_), BF16), jax.ShapeDtypeStruct((S, wb_), BF16),
                   jax.ShapeDtypeStruct((S, 1), F32), jax.ShapeDtypeStruct((S, 1), F32)],
        compiler_params=_params(("parallel",)),
    )(proj, g_a, g_b)


def _norm_pair_bwd(dy_a, dy_b, proj, r_a, r_b, g_a, g_b, dproj, off, *, tr=512):
    S = proj.shape[0]
    wa_, wb_ = g_a.shape[1], g_b.shape[1]
    W = wa_ + wb_
    tr = _tile(S, tr)

    def body(dya_ref, dyb_ref, x_ref, ra_ref, rb_ref, ga_ref, gb_ref, _, dx_ref, dga_ref, dgb_ref):
        first = pl.program_id(0) == 0
        for lo, hi, dy_ref, r_ref, g_ref, dg_ref in ((0, wa_, dya_ref, ra_ref, ga_ref, dga_ref),
                                                    (wa_, W, dyb_ref, rb_ref, gb_ref, dgb_ref)):
            rstd = r_ref[...]
            xhat = x_ref[:, lo:hi] * rstd
            dyv = dy_ref[...]
            dyg = dyv * g_ref[...]
            dx_ref[:, lo:hi] = (rstd * (dyg - xhat * jnp.mean(dyg * xhat, axis=-1, keepdims=True))).astype(dx_ref.dtype)
            part = jnp.sum(dyv * xhat, axis=0, keepdims=True)

            @pl.when(first)
            def _():
                dg_ref[...] = part

            @pl.when(jnp.logical_not(first))
            def _():
                dg_ref[...] += part

    one = pl.BlockSpec((tr, 1), lambda i: (i, 0))
    cols = pl.BlockSpec((tr, W), lambda i: (i, off // W))
    va, vb = pl.BlockSpec((1, wa_), lambda i: (0, 0)), pl.BlockSpec((1, wb_), lambda i: (0, 0))
    return _pallas(
        body, name="norm_qkv_bwd", grid=(S // tr,),
        in_specs=[pl.BlockSpec((tr, wa_), lambda i: (i, 0)), pl.BlockSpec((tr, wb_), lambda i: (i, 0)), cols,
                  one, one, va, vb, _ANY],
        out_specs=[cols, va, vb],
        out_shape=[jax.ShapeDtypeStruct(dproj.shape, dproj.dtype), jax.ShapeDtypeStruct((1, wa_), F32),
                   jax.ShapeDtypeStruct((1, wb_), F32)],
        input_output_aliases={7: 0},
        compiler_params=_params(("arbitrary",)),
    )(dy_a, dy_b, proj, r_a, r_b, g_a, g_b, dproj)


def _final_loss(h2, g, target, *, tr=256):
    S, D = h2.shape
    tr = _tile(S, tr)

    def body(h_ref, g_ref, t_ref, loss_ref, dh_ref, dhb_ref, dg_ref):
        hv = h_ref[...]
        rstd = lax.rsqrt(jnp.mean(hv * hv, axis=-1, keepdims=True) + EPS)
        xhat = hv * rstd
        e = xhat * g_ref[...] - t_ref[...]
        lpart = (0.5 / D) * jnp.sum(jnp.sum(e * e, axis=-1, keepdims=True), axis=0, keepdims=True)
        dy = e * (1.0 / D)
        dyg = dy * g_ref[...]
        dh = rstd * (dyg - xhat * jnp.mean(dyg * xhat, axis=-1, keepdims=True))
        dh_ref[...] = dh
        dhb_ref[...] = dh.astype(BF16)
        gpart = jnp.sum(dy * xhat, axis=0, keepdims=True)

        @pl.when(pl.program_id(0) == 0)
        def _():
            loss_ref[...] = lpart
            dg_ref[...] = gpart

        @pl.when(pl.program_id(0) > 0)
        def _():
            loss_ref[...] += lpart
            dg_ref[...] += gpart

    row = pl.BlockSpec((tr, D), lambda i: (i, 0))
    vec = pl.BlockSpec((1, D), lambda i: (0, 0))
    return _pallas(
        body, name="final_loss", grid=(S // tr,), in_specs=[row, vec, row],
        out_specs=[pl.BlockSpec((1, 1), lambda i: (0, 0)), row, row, vec],
        out_shape=[jax.ShapeDtypeStruct((1, 1), F32), jax.ShapeDtypeStruct((S, D), F32),
                   jax.ShapeDtypeStruct((S, D), BF16), jax.ShapeDtypeStruct((1, D), F32)],
        compiler_params=_params(("arbitrary",)),
    )(h2, g, target)


def _sigmoid(v):
    return 1.0 / (1.0 + jnp.exp(-v))


def _rope128(t, cos_full, sin_signed):
    return t * cos_full + pltpu.roll(t, RET_QK // 2, 1) * sin_signed


def _rope128_t(d, cos_full, sin_signed):
    return d * cos_full + pltpu.roll(d * sin_signed, RET_QK // 2, 1)


def _ret_consts(lg, T):
    pos = lax.broadcasted_iota(jnp.int32, (T, 1), 0).astype(F32)
    qd = jnp.exp(lg * (pos + 1.0))
    kd = jnp.exp(lg * (T - 1.0 - pos))
    n = lax.broadcasted_iota(jnp.int32, (T, T), 0)
    m = lax.broadcasted_iota(jnp.int32, (T, T), 1)
    vis = (m // CHUNK) <= (n // CHUNK)
    dist = jnp.abs(n - m).astype(F32)
    decay = jnp.where(vis, jnp.exp(lg * dist), 0.0)
    cdec = jnp.exp(lg * float(T))
    return qd, kd, decay, cdec


def _dot(a, b, dims):
    return lax.dot_general(a.astype(BF16), b.astype(BF16), (dims, ((), ())), preferred_element_type=F32)


NN = ((1,), (0,))
NT = ((1,), (1,))
TN = ((0,), (0,))
_RQ = slice(0, RET_QK)
_RK = slice(RET_QK, 2 * RET_QK)
_RV = slice(2 * RET_QK, 2 * RET_QK + RET_V)
_RG = slice(2 * RET_QK + RET_V, RET_HEAD_COLS)


RET_GROUP = 8


def _head_cols(h, part):
    return slice(h * RET_HEAD_COLS + part.start, h * RET_HEAD_COLS + part.stop)


def _ret_fwd(proj, cosr, sinr, lgam, gain, RH, *, T):
    S = proj.shape[0]
    nb = S // T
    G = _tile(RH, RET_GROUP)
    heads = range(G)
    scale = RET_QK ** -0.5

    def body(p_ref, cos_ref, sin_ref, lg_ref, gain_ref, ry_ref, gated_ref, st_ref, state):
        b = pl.program_id(1)

        @pl.when(b == 0)
        def _():
            state[...] = jnp.zeros_like(state)

        consts = [_ret_consts(lg_ref[h, 0:1, 0:1], T) for h in heads]
        cosv, sinv = cos_ref[...], sin_ref[...]
        q = [_rope128(p_ref[:, _head_cols(h, _RQ)], cosv, sinv) for h in heads]
        k = [_rope128(p_ref[:, _head_cols(h, _RK)], cosv, sinv) * scale for h in heads]
        v = [p_ref[:, _head_cols(h, _RV)] for h in heads]
        sprev = [state[h] for h in heads]
        for h in heads:
            st_ref[h] = sprev[h]
        a = [_dot(q[h], k[h], NT) for h in heads]
        qs = [_dot(q[h] * consts[h][0], sprev[h], NN) for h in heads]
        kv = [_dot(k[h] * consts[h][1], v[h], TN) for h in heads]
        o = [_dot(a[h] * consts[h][2], v[h], NN) + qs[h] for h in heads]
        for h in heads:
            state[h] = sprev[h] * consts[h][3] + kv[h]
            vals = slice(h * RET_V, (h + 1) * RET_V)
            ry_ref[:, vals] = o[h]
            mu = jnp.mean(o[h], axis=-1, keepdims=True)
            oc = o[h] - mu
            var = jnp.mean(oc * oc, axis=-1, keepdims=True)
            t = oc * lax.rsqrt(var + EPS) * gain_ref[:, vals]
            gv = p_ref[:, _head_cols(h, _RG)]
            gated_ref[:, vals] = (t * (gv * _sigmoid(gv))).astype(BF16)

    return _pallas(
        body, name="ret_fwd", grid=(RH // G, nb),
        in_specs=[pl.BlockSpec((T, G * RET_HEAD_COLS), lambda h, b: (b, h)),
                  pl.BlockSpec((T, RET_QK), lambda h, b: (b, 0)),
                  pl.BlockSpec((T, RET_QK), lambda h, b: (b, 0)),
                  pl.BlockSpec((G, 8, LANES), lambda h, b: (h, 0, 0)),
                  pl.BlockSpec((1, G * RET_V), lambda h, b: (0, h))],
        out_specs=[pl.BlockSpec((T, G * RET_V), lambda h, b: (b, h)),
                   pl.BlockSpec((T, G * RET_V), lambda h, b: (b, h)),
                   pl.BlockSpec((G, None, RET_QK, RET_V), lambda h, b: (h, b, 0, 0))],
        out_shape=[jax.ShapeDtypeStruct((S, RH * RET_V), F32), jax.ShapeDtypeStruct((S, RH * RET_V), BF16),
                   jax.ShapeDtypeStruct((RH, nb, RET_QK, RET_V), F32)],
        scratch_shapes=[pltpu.VMEM((G, RET_QK, RET_V), F32)],
        compiler_params=_params(("parallel", "arbitrary")),
    )(proj, cosr, sinr, lgam, gain)


def _ret_bwd(proj, cosr, sinr, lgam, gain, ry, dgated, states, dproj, RH, *, T):
    S = proj.shape[0]
    nb = S // T
    G = _tile(RH, RET_GROUP)
    heads = range(G)
    scale = RET_QK ** -0.5

    def body(p_ref, cos_ref, sin_ref, lg_ref, gain_ref, ry_ref, dg_ref, st_ref, _, dp_ref, dgain_ref, dstate):
        b = pl.program_id(1)

        @pl.when(b == 0)
        def _():
            dstate[...] = jnp.zeros_like(dstate)

        consts = [_ret_consts(lg_ref[h, 0:1, 0:1], T) for h in heads]
        qd, kd, decay, cdec = [[c[i] for c in consts] for i in range(4)]
        cosv, sinv = cos_ref[...], sin_ref[...]
        q = [_rope128(p_ref[:, _head_cols(h, _RQ)], cosv, sinv) for h in heads]
        k = [_rope128(p_ref[:, _head_cols(h, _RK)], cosv, sinv) * scale for h in heads]
        v = [p_ref[:, _head_cols(h, _RV)] for h in heads]
        sprev = [st_ref[h] for h in heads]
        ds_new = [dstate[h] for h in heads]
        a = [_dot(q[h], k[h], NT) for h in heads]
        do, gparts = [], []
        for h in heads:
            vals = slice(h * RET_V, (h + 1) * RET_V)
            o = ry_ref[:, vals]
            mu = jnp.mean(o, axis=-1, keepdims=True)
            oc = o - mu
            rstd = lax.rsqrt(jnp.mean(oc * oc, axis=-1, keepdims=True) + EPS)
            ryn = oc * rstd
            gainv = gain_ref[:, vals]
            gv = p_ref[:, _head_cols(h, _RG)]
            sg = _sigmoid(gv)
            dgt = dg_ref[:, vals]
            dt = dgt * (gv * sg)
            dp_ref[:, _head_cols(h, _RG)] = (dgt * (ryn * gainv) * (sg * (1.0 + gv * (1.0 - sg)))).astype(BF16)
            gparts.append(jnp.sum(dt * ryn, axis=0, keepdims=True))
            dryn = dt * gainv
            do.append(rstd * (dryn - jnp.mean(dryn, axis=-1, keepdims=True)
                              - ryn * jnp.mean(dryn * ryn, axis=-1, keepdims=True)))
        gpart = jnp.concatenate(gparts, axis=1)

        @pl.when(b == 0)
        def _():
            dgain_ref[...] = gpart

        @pl.when(b > 0)
        def _():
            dgain_ref[...] += gpart

        dpm = [_dot(do[h], v[h], NT) for h in heads]
        dq_s = [_dot(do[h], sprev[h], NT) for h in heads]
        dk_s = [_dot(v[h], ds_new[h], NT) for h in heads]
        dv_s = [_dot(k[h] * kd[h], ds_new[h], NN) for h in heads]
        dst = [_dot(q[h] * qd[h], do[h], TN) for h in heads]
        a = [a[h] * decay[h] for h in heads]
        dpm = [dpm[h] * decay[h] for h in heads]
        dv = [_dot(a[h], do[h], TN) + dv_s[h] for h in heads]
        dq = [_dot(dpm[h], k[h], NN) + dq_s[h] * qd[h] for h in heads]
        dk = [(_dot(dpm[h], q[h], TN) + dk_s[h] * kd[h]) * scale for h in heads]
        for h in heads:
            dstate[h] = ds_new[h] * cdec[h] + dst[h]
            dp_ref[:, _head_cols(h, _RV)] = dv[h].astype(BF16)
            dp_ref[:, _head_cols(h, _RQ)] = _rope128_t(dq[h], cosv, sinv).astype(BF16)
            dp_ref[:, _head_cols(h, _RK)] = _rope128_t(dk[h], cosv, sinv).astype(BF16)

    rb = lambda b: nb - 1 - b
    return _pallas(
        body, name="ret_bwd", grid=(RH // G, nb),
        in_specs=[pl.BlockSpec((T, G * RET_HEAD_COLS), lambda h, b: (rb(b), h)),
                  pl.BlockSpec((T, RET_QK), lambda h, b: (rb(b), 0)),
                  pl.BlockSpec((T, RET_QK), lambda h, b: (rb(b), 0)),
                  pl.BlockSpec((G, 8, LANES), lambda h, b: (h, 0, 0)),
                  pl.BlockSpec((1, G * RET_V), lambda h, b: (0, h)),
                  pl.BlockSpec((T, G * RET_V), lambda h, b: (rb(b), h)),
                  pl.BlockSpec((T, G * RET_V), lambda h, b: (rb(b), h)),
                  pl.BlockSpec((G, None, RET_QK, RET_V), lambda h, b: (h, rb(b), 0, 0)),
                  _ANY],
        out_specs=[pl.BlockSpec((T, G * RET_HEAD_COLS), lambda h, b: (rb(b), h)),
                   pl.BlockSpec((1, G * RET_V), lambda h, b: (0, h))],
        out_shape=[jax.ShapeDtypeStruct(dproj.shape, dproj.dtype), jax.ShapeDtypeStruct((1, RH * RET_V), F32)],
        scratch_shapes=[pltpu.VMEM((G, RET_QK, RET_V), F32)],
        input_output_aliases={8: 0},
        compiler_params=_params(("parallel", "arbitrary")),
    )(proj, cosr, sinr, lgam, gain, ry, dgated, states, dproj)


def _rope_pe(t, c, s1, s2):
    return t * c + pltpu.roll(t, LANES - QK_ROPE // 2, 1) * s1 + pltpu.roll(t, QK_ROPE // 2, 1) * s2


def _rope_pe_t(d, c, s1, s2):
    return d * c + pltpu.roll(d * s1, QK_ROPE // 2, 1) + pltpu.roll(d * s2, LANES - QK_ROPE // 2, 1)


ATTN_C2 = (QK_NOPE + QK_ROPE) ** -0.5 * LOG2E


def _qkv_proj(cqn, ckvn, wq, wkv, kpe, tabs, MH, *, tm=512, heads=4):
    S = cqn.shape[0]
    tm = _tile(S, tm)
    hb = _tile(MH, heads)
    W = 2 * LANES
    c_t, s1_t, s2_t = tabs

    def body(cq_ref, ckv_ref, wq_ref, wkv_ref, kpe_ref, c_ref, s1_ref, s2_ref, qf_ref, kf_ref, v_ref):
        c, s1, s2 = c_ref[...], s1_ref[...], s2_ref[...]
        q = _dot(cq_ref[...], wq_ref[...], NN)
        kv = _dot(ckv_ref[...], wkv_ref[...], NN)
        kper = _rope_pe(kpe_ref[...], c, s1, s2).astype(BF16)
        for h in range(hb):
            lo, mid, hi = h * W, h * W + QK_NOPE, (h + 1) * W
            qf_ref[:, lo:mid] = (q[:, lo:mid] * ATTN_C2).astype(BF16)
            qf_ref[:, mid:hi] = (_rope_pe(q[:, mid:hi], c, s1, s2) * ATTN_C2).astype(BF16)
            kf_ref[:, lo:mid] = kv[:, lo:mid].astype(BF16)
            kf_ref[:, mid:hi] = kper
            v_ref[:, h * V_HEAD:(h + 1) * V_HEAD] = kv[:, mid:hi].astype(BF16)

    tab = pl.BlockSpec((tm, LANES), lambda i, j: (i, 0))
    grp = pl.BlockSpec((tm, hb * W), lambda i, j: (i, j))
    return _pallas(
        body, name="qkv_proj", grid=(S // tm, MH // hb),
        in_specs=[pl.BlockSpec((tm, cqn.shape[1]), lambda i, j: (i, 0)),
                  pl.BlockSpec((tm, ckvn.shape[1]), lambda i, j: (i, 0)),
                  pl.BlockSpec((wq.shape[0], hb * W), lambda i, j: (0, j)),
                  pl.BlockSpec((wkv.shape[0], hb * W), lambda i, j: (0, j)), tab, tab, tab, tab],
        out_specs=[grp, grp, pl.BlockSpec((tm, hb * V_HEAD), lambda i, j: (i, j))],
        out_shape=[jax.ShapeDtypeStruct((S, MH * W), BF16)] * 2 + [jax.ShapeDtypeStruct((S, MH * V_HEAD), BF16)],
        compiler_params=_params(("parallel", "parallel")),
    )(cqn, ckvn, wq, wkv, kpe, c_t, s1_t, s2_t)


def _chunk_mask(T):
    n = lax.broadcasted_iota(jnp.int32, (T, T), 0)
    m = lax.broadcasted_iota(jnp.int32, (T, T), 1)
    return (m // CHUNK) <= (n // CHUNK)


def _lanes_to(v, width):
    return jnp.tile(v, (1, width // LANES))


def _attn_fwd(qf, kf, vb, MH, *, T, heads, name, prev=(), deps=()):
    S = qf.shape[0]
    nt = S // T
    n_skip = len(prev) + len(deps)

    def body(q_ref, k_ref, v_ref, *rest):
        o_ref, ob_ref, lse_ref, m_sc, l_sc, acc_sc, s_a, s_b = rest[n_skip:]
        qi = pl.program_id(1)
        m_sc[...] = jnp.full_like(m_sc, NEG)
        l_sc[...] = jnp.zeros_like(l_sc)
        acc_sc[...] = jnp.zeros_like(acc_sc)

        def rows_of(kt):
            return pl.ds(pl.multiple_of(kt * T, T), T)

        def scores(kt):
            return _dot(q_ref[...], k_ref[rows_of(kt), :], NT)

        def update(s, kt):
            m_prev = m_sc[...]
            m_new = jnp.maximum(m_prev, jnp.max(s, axis=-1, keepdims=True))
            alpha = jnp.exp2(m_prev - m_new)
            p = jnp.exp2(s - _lanes_to(m_new, T))
            l_sc[...] = alpha * l_sc[...] + jnp.sum(p, axis=-1, keepdims=True)
            acc_sc[...] = alpha * acc_sc[...] + _dot(p, v_ref[rows_of(kt), :], NN)
            m_sc[...] = m_new

        def masked(s):
            return jnp.where(_chunk_mask(T), s, NEG)

        @pl.when(qi == 0)
        def _():
            update(masked(scores(0)), 0)

        @pl.when(qi > 0)
        def _():
            s_a[...] = masked(scores(qi))
            s_b[...] = scores(0)
            update(s_a[...], qi)
            s_a[...] = scores(jnp.minimum(1, qi - 1))
            update(s_b[...], 0)

            def pair(j, carry):
                s_b[...] = scores(2 * j)
                update(s_a[...], 2 * j - 1)
                s_a[...] = scores(jnp.minimum(2 * j + 1, qi - 1))
                update(s_b[...], 2 * j)
                return carry

            lax.fori_loop(1, (qi + 1) // 2, pair, 0)

            @pl.when(qi % 2 == 0)
            def _():
                update(s_a[...], qi - 1)
        l = l_sc[...]
        o = acc_sc[...] / l
        o_ref[...] = o
        ob_ref[...] = o.astype(BF16)
        lse_ref[...] = m_sc[...] + jnp.log(l) * LOG2E

    h0, h1 = heads
    out_shape = [jax.ShapeDtypeStruct((S, MH * LANES), F32), jax.ShapeDtypeStruct((S, MH * LANES), BF16),
                 jax.ShapeDtypeStruct((MH, S, LANES), F32)]
    row = pl.BlockSpec((T, LANES), lambda h, i: (i, h0 + h))
    return _pallas(
        body, name=name, grid=(h1 - h0, nt),
        in_specs=[pl.BlockSpec((T, 2 * LANES), lambda h, i: (i, h0 + h)),
                  pl.BlockSpec((S, 2 * LANES), lambda h, i: (0, h0 + h)),
                  pl.BlockSpec((S, LANES), lambda h, i: (0, h0 + h))] + [_ANY] * (len(prev) + len(deps)),
        out_specs=[row, row, pl.BlockSpec((None, T, LANES), lambda h, i: (h0 + h, i, 0))],
        out_shape=out_shape,
        scratch_shapes=[pltpu.VMEM((T, LANES), F32), pltpu.VMEM((T, LANES), F32), pltpu.VMEM((T, LANES), F32),
                        pltpu.VMEM((T, T), F32), pltpu.VMEM((T, T), F32)],
        input_output_aliases={3 + i: i for i in range(len(prev))},
        compiler_params=_params(("parallel", "parallel")),
    )(qf, kf, vb, *prev, *deps)


def _attn_bwd(qf, kf, vb, dob, lse2, delta, tabs, MH, *, T, deps=()):
    S = qf.shape[0]
    nt = S // T
    scale = (QK_NOPE + QK_ROPE) ** -0.5
    n_dep = len(deps)

    def body(q_ref, k_ref, v_ref, do_ref, lse_ref, dl_ref, c_ref, s1_ref, s2_ref, *rest):
        dqa_ref, dkv_ref, dkpe_ref, dq_ref, dk_sc, dv_sc, s_a, dp_a, s_b, dp_b = rest[n_dep:]
        kj = pl.program_id(1)

        @pl.when(kj == 0)
        def _():
            dq_ref[...] = jnp.zeros_like(dq_ref)

        dk_sc[...] = jnp.zeros_like(dk_sc)
        dv_sc[...] = jnp.zeros_like(dv_sc)

        def rows_of(qt):
            return pl.ds(pl.multiple_of(qt * T, T), T)

        def products(qt):
            rows = rows_of(qt)
            return _dot(q_ref[rows, :], k_ref[...], NT), _dot(do_ref[rows, :], v_ref[...], NT)

        def update(s, dp, qt):
            rows = rows_of(qt)
            q, dov = q_ref[rows, :], do_ref[rows, :]
            p = jnp.exp2(s - _lanes_to(lse_ref[rows, :], T))
            ds = p * (dp - _lanes_to(dl_ref[rows, :], T))
            dv_sc[...] += _dot(p, dov, TN)
            dk_sc[...] += _dot(ds, q, TN)
            dq_ref[rows, :] += _dot(ds, k_ref[...], NN)

        def masked(s):
            return jnp.where(_chunk_mask(T), s, NEG)

        @pl.when(kj == nt - 1)
        def _():
            s, dp = products(kj)
            update(masked(s), dp, kj)

        @pl.when(kj < nt - 1)
        def _():
            s, dp = products(kj)
            s_a[...], dp_a[...] = masked(s), dp
            s_b[...], dp_b[...] = products(kj + 1)
            update(s_a[...], dp_a[...], kj)
            s_a[...], dp_a[...] = products(jnp.minimum(kj + 2, nt - 1))
            update(s_b[...], dp_b[...], kj + 1)

            def pair(j, carry):
                t0 = kj + 2 * j
                s_b[...], dp_b[...] = products(t0 + 1)
                update(s_a[...], dp_a[...], t0)
                s_a[...], dp_a[...] = products(jnp.minimum(t0 + 2, nt - 1))
                update(s_b[...], dp_b[...], t0 + 1)
                return carry

            lax.fori_loop(1, (nt - kj) // 2, pair, 0)

            @pl.when((nt - kj) % 2 == 1)
            def _():
                update(s_a[...], dp_a[...], nt - 1)
        dkv_ref[:, :QK_NOPE] = (dk_sc[:, :QK_NOPE] * (1.0 / LOG2E)).astype(BF16)
        dkv_ref[:, QK_NOPE:] = dv_sc[...].astype(BF16)
        dkpe_ref[...] = dk_sc[:, QK_NOPE:] * (1.0 / LOG2E)

        @pl.when(kj == nt - 1)
        def _():
            dqa_ref[:, :QK_NOPE] = (dq_ref[:, :QK_NOPE] * scale).astype(BF16)
            dqa_ref[:, QK_NOPE:] = (_rope_pe_t(dq_ref[:, QK_NOPE:], c_ref[...], s1_ref[...], s2_ref[...])
                                    * scale).astype(BF16)

    stat = pl.BlockSpec((None, S, LANES), lambda h, j: (h, 0, 0))
    tab = pl.BlockSpec((S, LANES), lambda h, j: (0, 0))
    return _pallas(
        body, name="attn_bwd", grid=(MH, nt),
        in_specs=[pl.BlockSpec((S, 2 * LANES), lambda h, j: (0, h)),
                  pl.BlockSpec((T, 2 * LANES), lambda h, j: (j, h)),
                  pl.BlockSpec((T, LANES), lambda h, j: (j, h)),
                  pl.BlockSpec((S, LANES), lambda h, j: (0, h)), stat, stat, tab, tab, tab] + [_ANY] * n_dep,
        out_specs=[pl.BlockSpec((S, 2 * LANES), lambda h, j: (0, h)),
                   pl.BlockSpec((T, 2 * LANES), lambda h, j: (j, h)),
                   pl.BlockSpec((T, LANES), lambda h, j: (j, h))],
        out_shape=[jax.ShapeDtypeStruct((S, MH * 2 * LANES), BF16), jax.ShapeDtypeStruct((S, MH * 2 * LANES), BF16),
                   jax.ShapeDtypeStruct((S, MH * LANES), F32)],
        scratch_shapes=[pltpu.VMEM((S, 2 * LANES), F32), pltpu.VMEM((T, 2 * LANES), F32), pltpu.VMEM((T, LANES), F32)]
        + [pltpu.VMEM((T, T), F32)] * 4,
        compiler_params=_params(("parallel", "arbitrary")),
    )(qf, kf, vb, dob, lse2, delta, *tabs, *deps)


def _kpe_sum(dkpe_h, tabs, MH, *, tr=256):
    S = dkpe_h.shape[0]
    tr = _tile(S, tr)

    def body(dk_ref, c_ref, s1_ref, s2_ref, dkpe_ref):
        tot = dk_ref[:, :LANES]
        for h in range(1, MH):
            tot = tot + dk_ref[:, h * LANES:(h + 1) * LANES]
        dkpe_ref[...] = _rope_pe_t(tot, c_ref[...], s1_ref[...], s2_ref[...]).astype(BF16)

    tab = pl.BlockSpec((tr, LANES), lambda i: (i, 0))
    return _pallas(
        body, name="kpe_sum", grid=(S // tr,),
        in_specs=[pl.BlockSpec((tr, MH * LANES), lambda i: (i, 0)), tab, tab, tab],
        out_specs=tab, out_shape=jax.ShapeDtypeStruct((S, LANES), BF16),
        compiler_params=_params(("parallel",)),
    )(dkpe_h, *tabs)


ROW_ALIGN = 16


def _blk(R, C, block_bytes=2 << 20):
    cap = max(ROW_ALIGN, block_bytes // (C * 4))
    for t in range(min(R, cap) // ROW_ALIGN * ROW_ALIGN, LANES - 1, -ROW_ALIGN):
        if R % t == 0:
            return t, C
    if R <= cap:
        return R, C
    tc = C
    while R * tc * 4 > block_bytes and tc % (2 * LANES) == 0:
        tc //= 2
    return R, tc


def _rows_call(fn, ins, out_dtypes, *, name, deps=()):
    R, C = ins[0].shape
    tr, tc = _blk(R, C)
    n_in, n_dep = len(ins), len(deps)

    def body(*refs):
        vals = fn(*[r[...] for r in refs[:n_in]])
        for r, v in zip(refs[n_in + n_dep:], vals):
            r[...] = v.astype(r.dtype)

    blk = pl.BlockSpec((tr, tc), lambda i, j: (i, j))
    res = _pallas(
        body, name=name, grid=(R // tr, C // tc), in_specs=[blk] * n_in + [_ANY] * n_dep,
        out_specs=[blk] * len(out_dtypes),
        out_shape=[jax.ShapeDtypeStruct((R, C), d) for d in out_dtypes],
        compiler_params=_params(("parallel", "parallel")),
    )(*ins, *deps)
    return res


def _adamw_vals(w, g, m, v):
    m = ADAM_B1 * m + (1.0 - ADAM_B1) * g
    v = ADAM_B2 * v + (1.0 - ADAM_B2) * (g * g)
    m_hat = m / (1.0 - ADAM_B1 ** ADAM_STEP)
    v_hat = v / (1.0 - ADAM_B2 ** ADAM_STEP)
    delta = -ADAM_LR * (m_hat / (jnp.sqrt(v_hat) + ADAM_EPS) + ADAM_WD * w)
    return delta, m, v


def _sum_pair(p, theirs, place, *, name):
    _, R, C = p.shape
    R2 = R // 2
    tr, tc = _blk(R2, C)
    p4 = p.reshape(N_CHIPS, 2, R2, C)

    def body(place_ref, a_ref, b_ref, o_ref):
        o_ref[...] = (a_ref[...].astype(F32) + b_ref[...].astype(F32)).astype(BF16)

    spec = pltpu.PrefetchScalarGridSpec(
        num_scalar_prefetch=1, grid=(N_CHIPS, R2 // tr, C // tc),
        in_specs=[pl.BlockSpec((None, None, tr, tc), lambda q, i, j, pr: (q, pr[0], i, j)),
                  pl.BlockSpec((None, tr, tc), lambda q, i, j, pr: (q, i, j))],
        out_specs=pl.BlockSpec((None, tr, tc), lambda q, i, j, pr: (q, i, j)))
    return _pallas(body, name=name, grid_spec=spec, out_shape=jax.ShapeDtypeStruct((N_CHIPS, R2, C), BF16),
                   compiler_params=_params(("parallel", "parallel", "parallel")))(place, p4, theirs)


def _sum_chips(p, theirs, recv, place, *, name):
    _, R, C = p.shape
    R2 = R // 2
    tr, tc = _blk(R2, C)
    p4 = p.reshape(N_CHIPS, 2, R2, C)

    def body(place_ref, a_ref, b_ref, r0_ref, r1_ref, r2_ref, o_ref):
        own = a_ref[...].astype(F32) + b_ref[...].astype(F32)
        o_ref[...] = ((own + r0_ref[...].astype(F32)) + r1_ref[...].astype(F32)) + r2_ref[...].astype(F32)

    def slot(k):
        return pl.BlockSpec((None, tr, tc), lambda i, j, pr: (k, i, j))

    spec = pltpu.PrefetchScalarGridSpec(
        num_scalar_prefetch=1, grid=(R2 // tr, C // tc),
        in_specs=[pl.BlockSpec((None, None, tr, tc), lambda i, j, pr: (pr[1], pr[0], i, j)),
                  pl.BlockSpec((None, tr, tc), lambda i, j, pr: (pr[1], i, j)), slot(0), slot(1), slot(2)],
        out_specs=pl.BlockSpec((None, tr, tc), lambda i, j, pr: (pr[0], i, j)))
    return _pallas(body, name=name, grid_spec=spec, out_shape=jax.ShapeDtypeStruct((2, R2, C), F32),
                   compiler_params=_params(("parallel", "parallel")))(place, p4, theirs, recv, recv, recv)


def _me():
    return lax.axis_index("x"), lax.axis_index("y"), lax.axis_index("c")


def _other_chips(x, y):
    return [(1 - x, y), (x, 1 - y), (1 - x, 1 - y)]


def _rcopy(src, dst, ssem, rsem, dev):
    return pltpu.make_async_remote_copy(src_ref=src, dst_ref=dst, send_sem=ssem, recv_sem=rsem,
                                        device_id=dev, device_id_type=MESH)


def _cast_into_slot(w, place, *, name, rows=None, deps=()):
    R, C = w.shape
    rows = R if rows is None else rows
    tr, tc = _blk(R, C)

    def body(place_ref, w_ref, *rest):
        rest[-1][...] = w_ref[...].astype(BF16)

    spec = pltpu.PrefetchScalarGridSpec(
        num_scalar_prefetch=1, grid=(R // tr, C // tc),
        in_specs=[pl.BlockSpec((tr, tc), lambda i, j, pr: (i, j))] + [_ANY] * len(deps),
        out_specs=pl.BlockSpec((None, tr, tc), lambda i, j, pr: (pr[1], i, j)))
    out = _pallas(body, name=name, grid_spec=spec, out_shape=jax.ShapeDtypeStruct((N_CHIPS, rows, C), BF16),
                  compiler_params=_params(("parallel", "parallel")))(place, w, *deps)
    return out.reshape(N_CHIPS, 2, rows // 2, C)


def _gather_ici_plan(bufs):
    x, y, c = _me()
    j = 2 * x + y
    plan = []
    for i, buf in enumerate(bufs):
        for k, (px, py) in enumerate(_other_chips(x, y)):
            plan.append((3 * i + k, buf.at[j, c], buf.at[j, c], (px, py, c)))
    return plan


def _forward_halves(bufs, *, name):
    n = len(bufs)

    def body(*refs):
        outs = refs[n:2 * n]
        ssem, rsem = refs[2 * n:]
        x, y, c = _me()
        sib = (x, y, 1 - c)
        cps = []
        for i in range(n):
            for k, (px, py) in enumerate(_other_chips(x, y)):
                slot = outs[i].at[2 * px + py, c]
                r = _rcopy(slot, slot, ssem.at[3 * i + k], rsem.at[3 * i + k], sib)
                r.start()
                cps.append(r)
        for r in cps:
            r.wait()

    return _pallas(
        body, name=name, in_specs=[_ANY] * n, out_specs=[_ANY] * n,
        out_shape=[jax.ShapeDtypeStruct(b.shape, b.dtype) for b in bufs],
        scratch_shapes=[pltpu.SemaphoreType.DMA((3 * n,))] * 2,
        input_output_aliases={i: i for i in range(n)},
        compiler_params=pltpu.CompilerParams(has_side_effects=True),
    )(*bufs)


_HBM = pl.BlockSpec(memory_space=pltpu.HBM)
_SEM = pl.BlockSpec(memory_space=pltpu.SEMAPHORE)
_EFFECT = pltpu.SideEffectType.DATAFLOW_SIDE_EFFECTING


def _split_start(bufs, plan, n_copies, *, name):
    n = len(bufs)

    def body(*refs):
        ssem, rsem = refs[n], refs[n + 1]
        for s, src, dst, dev in plan(refs[:n]):
            _rcopy(src, dst, ssem.at[s], rsem.at[s], dev).start()
        refs[-1][...] = jnp.zeros_like(refs[-1])

    res = _pallas(
        body, name=name, in_specs=[_HBM] * n,
        out_specs=(_SEM, _SEM, *[_HBM] * n, pl.BlockSpec(memory_space=pltpu.VMEM)),
        out_shape=(pltpu.SemaphoreType.DMA((n_copies,)), pltpu.SemaphoreType.DMA((n_copies,)),
                   *[pltpu.HBM(b.shape, b.dtype) for b in bufs], jax.ShapeDtypeStruct((8, LANES), F32)),
        input_output_aliases={i: 2 + i for i in range(n)},
        compiler_params=pltpu.CompilerParams(has_side_effects=_EFFECT),
    )(*[pltpu.with_memory_space_constraint(b, pltpu.HBM) for b in bufs])
    return res[0], res[1], list(res[2:2 + n]), res[-1]


def _split_wait(ssem, rsem, bufs, after, plan, *, name):
    n = len(bufs)

    def body(*refs):
        ssem_ref, rsem_ref = refs[n], refs[n + 1]
        for s, src, dst, dev in plan(refs[:n]):
            cp = _rcopy(src, dst, ssem_ref.at[s], rsem_ref.at[s], dev)
            cp.wait_send()
            cp.wait_recv()

    return list(_pallas(
        body, name=name, in_specs=[_HBM] * n + [_SEM, _SEM, _ANY], out_specs=[_HBM] * n,
        out_shape=[pltpu.HBM(b.shape, b.dtype) for b in bufs],
        input_output_aliases={i: i for i in range(n)},
        compiler_params=pltpu.CompilerParams(has_side_effects=_EFFECT),
    )(*bufs, ssem, rsem, after))


def _forward_plan(bufs):
    x, y, c = _me()
    plan = []
    for i, buf in enumerate(bufs):
        for k, (px, py) in enumerate(_other_chips(x, y)):
            plan.append((3 * i + k, buf.at[2 * px + py, c], buf.at[2 * px + py, c], (x, y, 1 - c)))
    return plan


def _join_plan(bufs):
    x, y, c = _me()
    return [(i, buf.at[c], buf.at[c], (x, y, 1 - c)) for i, buf in enumerate(bufs)]


def _swap_plan(n):
    def plan(bufs):
        x, y, c = _me()
        return [(i, bufs[i].at[:, 1 - c], bufs[n + i], (x, y, 1 - c)) for i in range(n)]
    return plan


def _scatter_plan(n):
    def plan(bufs):
        x, y, c = _me()
        out = []
        for i in range(n):
            for k, (px, py) in enumerate(_other_chips(x, y)):
                out.append((3 * i + k, bufs[i].at[2 * px + py], bufs[n + i].at[k], (px, py, c)))
        return out
    return plan


def _allreduce_small(parts, loss11):
    n = len(parts)
    widths = [p.shape[1] for p in parts]
    total = sum(widths) + LANES

    def body(*refs):
        o_ref, mine, buf, ssem, rsem = refs[n + 1:]
        x, y, c = _me()
        me = 4 * x + 2 * y + c
        off = 0
        for r, w in zip(refs[:n], widths):
            mine[:, off:off + w] = r[...]
            off += w
        mine[:, off:] = jnp.broadcast_to(refs[n][...], (1, LANES))
        buf[me] = mine[...]
        cps = []
        for k in range(1, 8):
            peer = (x ^ (k >> 2), y ^ ((k >> 1) & 1), c ^ (k & 1))
            r = _rcopy(mine, buf.at[me], ssem.at[k - 1], rsem.at[k - 1], peer)
            r.start()
            cps.append(r)
        for k in range(1, 8):
            peer = (x ^ (k >> 2), y ^ ((k >> 1) & 1), c ^ (k & 1))
            pid = 4 * peer[0] + 2 * peer[1] + peer[2]
            _rcopy(mine, buf.at[pid], ssem.at[k - 1], rsem.at[k - 1], peer).wait_recv()
        for r in cps:
            r.wait_send()
        tot = buf[0]
        for d in range(1, 8):
            tot = tot + buf[d]
        o_ref[...] = tot

    vm = pl.BlockSpec(memory_space=pltpu.VMEM)
    return _pallas(
        body, name="allreduce_small", in_specs=[vm] * (n + 1), out_specs=vm,
        out_shape=jax.ShapeDtypeStruct((1, total), F32),
        scratch_shapes=[pltpu.VMEM((1, total), F32), pltpu.VMEM((8, 1, total), F32),
                        pltpu.SemaphoreType.DMA((7,)), pltpu.SemaphoreType.DMA((7,))],
        compiler_params=pltpu.CompilerParams(has_side_effects=True),
    )(*parts, loss11)


def _adamw_small(red, ws, ms, vs):
    n = len(ws)

    def body(*refs):
        red_ref = refs[0]
        outs = refs[1 + 3 * n:]
        off = 0
        for i in range(n):
            w = refs[1 + i].shape[1]
            g = red_ref[:, off:off + w]
            d, m, v = _adamw_vals(refs[1 + i][...], g, refs[1 + n + i][...], refs[1 + 2 * n + i][...])
            for o, val in zip(outs[4 * i:4 * i + 4], (g, d, m, v)):
                o[...] = val
            off += w

    vm = pl.BlockSpec(memory_space=pltpu.VMEM)
    res = _pallas(
        body, name="adamw_small", in_specs=[vm] * (1 + 3 * n), out_specs=[vm] * (4 * n),
        out_shape=[jax.ShapeDtypeStruct(w.shape, F32) for w in ws for _ in range(4)],
    )(red, *ws, *ms, *vs)
    return [res[4 * i:4 * i + 4] for i in range(n)]


def _rope_tables(positions, S):
    pos = positions.reshape(S, 1).astype(F32)
    half = RET_QK // 2
    inv = ROPE_THETA ** (-jnp.arange(half, dtype=F32) / half)
    ang = pos * inv
    cosr = jnp.concatenate([jnp.cos(ang), jnp.cos(ang)], axis=1)
    sinr = jnp.concatenate([-jnp.sin(ang), jnp.sin(ang)], axis=1)
    half = QK_ROPE // 2
    inv = ROPE_THETA ** (-jnp.arange(half, dtype=F32) / half)
    ang = pos * inv
    z = jnp.zeros((S, half), F32)
    c = jnp.concatenate([jnp.cos(ang), jnp.cos(ang), z, z], axis=1)
    s1 = jnp.concatenate([-jnp.sin(ang), z, z, z], axis=1)
    s2 = jnp.concatenate([z, jnp.sin(ang), z, z], axis=1)
    return cosr, sinr, (c, s1, s2)


def _cat_cols(g):
    return jnp.concatenate([g[j] for j in range(N_CHIPS)], axis=1)


def _split_cols(w):
    return jnp.stack(jnp.split(w, N_CHIPS, axis=1))


def kernel(x, positions, norm_mix_g, w_in, ret_norm_g, w_ret_o, q_a_norm_g, w_q_b, kv_a_norm_g, w_kv_b, w_mla_o, w_out, norm_mlp_g, w_up, w_down, norm_f_g, loss_target, m_norm_mix_g, m_w_in, m_ret_norm_g, m_w_ret_o, m_q_a_norm_g, m_w_q_b, m_kv_a_norm_g, m_w_kv_b, m_w_mla_o, m_w_out, m_norm_mlp_g, m_w_up, m_w_down, m_norm_f_g, v_norm_mix_g, v_w_in, v_ret_norm_g, v_w_ret_o, v_q_a_norm_g, v_w_q_b, v_kv_a_norm_g, v_w_kv_b, v_w_mla_o, v_w_out, v_norm_mlp_g, v_w_up, v_w_down, v_norm_f_g):
    S, D = x.shape[1], x.shape[2]
    RVW = w_ret_o.shape[1] * N_CHIPS
    RH = RVW // RET_V
    RQW = RH * RET_QK
    MVW = w_mla_o.shape[1] * N_CHIPS
    MH = MVW // V_HEAD
    QL, KVL = w_q_b.shape[1], w_kv_b.shape[1]
    T_RET = _tile(S, 256)
    T_ATT = _tile(S, 512)

    xs = x.reshape(S, D)
    tgt = loss_target.reshape(S, D)
    cosr, sinr, pe_tabs = _rope_tables(positions, S)
    lgam = jnp.log(1.0 - 2.0 ** (-5.0 - jnp.arange(RH, dtype=F32)))
    lgam = jnp.broadcast_to(lgam[:, None, None], (RH, 8, LANES))

    big = ("w_in", "w_ret_o", "w_q_b", "w_kv_b", "w_mla_o", "w_out", "w_up", "w_down")
    w_sh = dict(w_in=w_in[0].T, w_ret_o=w_ret_o[0], w_q_b=w_q_b[0], w_kv_b=w_kv_b[0], w_mla_o=w_mla_o[0],
                w_out=w_out[0], w_up=w_up[0], w_down=w_down[0])
    m_sh = dict(w_in=m_w_in[0].T, w_ret_o=m_w_ret_o[0], w_q_b=m_w_q_b[0], w_kv_b=m_w_kv_b[0],
                w_mla_o=m_w_mla_o[0], w_out=m_w_out[0], w_up=m_w_up[0], w_down=m_w_down[0])
    v_sh = dict(w_in=v_w_in[0].T, w_ret_o=v_w_ret_o[0], w_q_b=v_w_q_b[0], w_kv_b=v_w_kv_b[0],
                w_mla_o=v_w_mla_o[0], w_out=v_w_out[0], w_up=v_w_up[0], w_down=v_w_down[0])
    col_sharded = ("w_q_b", "w_kv_b", "w_up")
    c_sh = w_in.shape[2]
    c_pad = -(-c_sh // 64) * 64
    place = jnp.stack([lax.axis_index("c"), 2 * lax.axis_index("x") + lax.axis_index("y")]).astype(jnp.int32)

    def whole(k, g):
        g = g.reshape(N_CHIPS, w_sh[k].shape[0], w_sh[k].shape[1])
        if k == "w_up":
            return g
        return _cat_cols(g) if k in col_sharded else g.reshape(-1, g.shape[2])

    first = ("w_in", "w_q_b", "w_kv_b")
    later = ("w_ret_o", "w_mla_o", "w_out", "w_up", "w_down")
    first_bufs = [_cast_into_slot(w_sh[k], place, name="cast_" + k, rows=c_pad if k == "w_in" else None)
                  for k in first]
    first_ssem, first_rsem, first_bufs, first_token = _split_start(
        first_bufs, _gather_ici_plan, 3 * len(first), name="gather_first_start")
    later_bufs = [_cast_into_slot(w_sh[k], place, name="cast_" + k, deps=(first_token,)) for k in later[:-1]]
    first_bufs = _split_wait(first_ssem, first_rsem, first_bufs, later_bufs[-1], _gather_ici_plan,
                             name="gather_first_wait")
    got = _forward_halves(first_bufs, name="gather_first_forward")
    full = {k: whole(k, g) for k, g in zip(first[1:], got[1:])}
    later_bufs.append(_cast_into_slot(w_sh[later[-1]], place, name="cast_" + later[-1], deps=(got[0],)))
    later_ssem, later_rsem, later_bufs, later_token = _split_start(
        later_bufs, _gather_ici_plan, 3 * len(later), name="gather_later_start")

    o_rq, o_rk, o_rv, o_rg = 0, RQW, 2 * RQW, 2 * RQW + RVW
    o_cq = 2 * RQW + 2 * RVW
    o_ckv, o_kpe = o_cq + QL, o_cq + QL + KVL
    o_gr = o_kpe + QK_ROPE
    o_gm = o_gr + D
    n_ret = RH * RET_HEAD_COLS
    off_gate, off_cq, off_ckv = n_ret, n_ret + 2 * D, n_ret + 2 * D + QL
    gate_tile = _tile(D, 1024)
    n_a = off_ckv + KVL
    runs = []
    for h in range(RH):
        base = h * RET_HEAD_COLS
        runs += [(o_rq + h * RET_QK, RET_QK, base), (o_rk + h * RET_QK, RET_QK, base + RET_QK),
                 (o_rv + h * RET_V, RET_V, base + 2 * RET_QK), (o_rg + h * RET_V, RET_V, base + 2 * RET_QK + RET_V)]
    for t in range(D // gate_tile):
        runs += [(o_gr + t * gate_tile, gate_tile, off_gate + 2 * t * gate_tile),
                 (o_gm + t * gate_tile, gate_tile, off_gate + (2 * t + 1) * gate_tile)]
    runs += [(o_cq, QL, off_cq), (o_ckv, KVL, off_ckv), (o_kpe, QK_ROPE, n_a)]

    def take(parts, start, width):
        out, lo = [], 0
        for p in parts:
            hi = lo + p.shape[0]
            a, b = max(start, lo), min(start + width, hi)
            if a < b:
                out.append(p[a - lo:b - lo])
            lo = hi
        return out

    wi = [got[0].reshape(N_CHIPS, c_pad, D)[jj, :c_sh] for jj in range(N_CHIPS)]
    here = sorted(runs, key=lambda r: r[2])
    wa = jnp.concatenate([p for s0, w, _ in here[:-1] for p in take(wi, s0, w)], axis=0)
    wkpe = jnp.concatenate(take(wi, o_kpe, QK_ROPE) + [jnp.zeros((LANES - QK_ROPE, D), BF16)], axis=0)
    wq = jnp.pad(full["w_q_b"].reshape(QL, MH, QK_NOPE + QK_ROPE),
                 ((0, 0), (0, 0), (0, LANES - QK_ROPE))).reshape(QL, MH * 2 * LANES)
    wkv = full["w_kv_b"]

    u, rstd0 = _rmsnorm_fwd(xs, norm_mix_g, name="norm_mix")
    proj = _mm(u, wa, mode="nt", outs=[F32], name="in_proj", deps=(later_token,))
    kpe = _mm(u, wkpe, mode="nt", outs=[F32], name="kpe_proj")
    ry, gated, states = _ret_fwd(proj, cosr, sinr, lgam, ret_norm_g, RH, T=T_RET)
    cqn, ckvn, rstd_q, rstd_kv = _norm_pair_fwd(proj, q_a_norm_g, kv_a_norm_g, off_cq)
    qf, kf, vb = _qkv_proj(cqn, ckvn, wq, wkv, kpe, pe_tabs, MH)
    first_half = _attn_fwd(qf, kf, vb, MH, T=T_ATT, heads=(0, MH // 2), name="attn_fwd_a")
    later_bufs = _split_wait(later_ssem, later_rsem, later_bufs, first_half[0], _gather_ici_plan,
                             name="gather_later_wait")
    fwd_ssem, fwd_rsem, later_bufs, fwd_token = _split_start(
        later_bufs, _forward_plan, 3 * len(later), name="gather_later_forward_start")
    my, my_b, lse2 = _attn_fwd(qf, kf, vb, MH, T=T_ATT, heads=(MH // 2, MH), name="attn_fwd_b",
                               prev=first_half, deps=(fwd_token,))
    later_bufs = _split_wait(fwd_ssem, fwd_rsem, later_bufs, my, _forward_plan, name="gather_later_forward_wait")
    full.update({k: whole(k, g) for k, g in zip(later, later_bufs)})
    y_ret = _mm(gated, full["w_ret_o"], mode="nn", outs=[BF16], name="ret_o")
    y_mla, merged = _mm(my_b, full["w_mla_o"], mode="nn", outs=[BF16, BF16], name="mla_o",
                        epi=lambda acc, gr, gm, yr: (acc, _sigmoid(gr) * yr + _sigmoid(gm) * acc),
                        extras=((proj, off_gate, 2), (proj, off_gate + gate_tile, 2), y_ret), tn=gate_tile)
    h1 = _mm(merged, full["w_out"], mode="nn", outs=[F32], name="out_proj",
             epi=lambda acc, r: (acc + r,), extras=(xs,))
    n1, rstd1 = _rmsnorm_fwd(h1, norm_mlp_g, name="norm_mlp")

    def up_epi(acc):
        r = jnp.maximum(acc, 0.0)
        return acc, r * r

    z, act = _mm(n1, full["w_up"], mode="nn", outs=[F32, BF16], name="up_proj", epi=up_epi)
    h2 = _mm(act, full["w_down"], mode="nn", outs=[F32], name="down_proj",
             epi=lambda acc, r: (acc + r,), extras=(h1,))
    loss11, dh2, dh2_b, g_norm_f = _final_loss(h2, norm_f_g.reshape(1, D), tgt)

    dz = _mm(dh2_b, full["w_down"], mode="nt", outs=[BF16], name="down_bwd_x",
             epi=lambda acc, zz: (acc * (2.0 * jnp.maximum(zz, 0.0)),), extras=(z,))
    g_w_down = _mm(act, dh2_b, mode="tn", outs=[BF16], name="down_bwd_w")
    dn1 = _mm(dz, full["w_up"], mode="nt", outs=[F32], name="up_bwd_x")
    g_w_up = _mm(n1, dz, mode="tn", outs=[BF16], name="up_bwd_w", out_shards=True)

    def scatter_begin(tag, sums):
        lands = [lax.empty((3,) + s.shape[1:], s.dtype) for s in sums]
        return _split_start(sums + lands, _scatter_plan(len(sums)), 3 * len(sums), name="scatter_" + tag + "_start")

    def swap_begin(tag, grads):
        views = [g if g.ndim == 3 else g.reshape(N_CHIPS, g.shape[0] // N_CHIPS, g.shape[1]) for g in grads]
        views = [v.reshape(N_CHIPS, 2, v.shape[1] // 2, v.shape[2]) for v in views]
        lands = [lax.empty((N_CHIPS,) + v.shape[2:], v.dtype) for v in views]
        return _split_start(views + lands, _swap_plan(len(views)), len(views), name="swap_" + tag + "_start")

    def swap_end(tag, names, handle, after):
        n = len(names)
        bufs = _split_wait(handle[0], handle[1], handle[2], after, _swap_plan(n), name="swap_" + tag + "_wait")
        pcs = [b.reshape(N_CHIPS, 2 * b.shape[2], b.shape[3]) for b in bufs[:n]]
        sums = [_sum_pair(p, t, place, name="sum_pair_" + k) for k, p, t in zip(names, pcs, bufs[n:])]
        return pcs, bufs[n:], sums

    g1 = ("w_up", "w_down")
    swap1 = swap_begin("g1", (g_w_up, g_w_down))
    dh1, g_norm_mlp, dh1_b = _rmsnorm_bwd(dn1, h1, rstd1, norm_mlp_g, name="norm_mlp_bwd", res=dh2,
                                          deps=(swap1[3],), bf16_copy=1)
    def merge_bwd_epi(dm, gr, gm, yr, ym):
        sr, sm = _sigmoid(gr), _sigmoid(gm)
        return dm * sr, dm * sm, jnp.concatenate([dm * yr * sr * (1.0 - sr), dm * ym * sm * (1.0 - sm)], axis=1)

    dy_ret, dy_mla, dproj = _mm(
        dh1_b, full["w_out"], mode="nt", outs=[BF16, BF16], name="out_bwd_x", epi=merge_bwd_epi,
        extras=((proj, off_gate, 2), (proj, off_gate + gate_tile, 2), y_ret, y_mla), tm=512, tn=gate_tile,
        more_outs=lambda tm, tn: [(jax.ShapeDtypeStruct(proj.shape, BF16),
                                   pl.BlockSpec((tm, 2 * tn), lambda i, j, k: (i, off_gate // (2 * tn) + j)))])
    pcs1, theirs1, sums1 = swap_end("g1", g1, swap1, dy_ret)
    ssem1, rsem1, bufs1, token1 = scatter_begin("g1", sums1)
    g_w_out = _mm(merged, dh1_b, mode="tn", outs=[BF16], name="out_bwd_w", deps=(token1,))
    dgated = _mm(dy_ret, full["w_ret_o"], mode="nt", outs=[F32], name="ret_o_bwd_x")
    g_w_ret_o = _mm(gated, dy_ret, mode="tn", outs=[BF16], name="ret_o_bwd_w")
    dproj, g_ret_norm = _ret_bwd(proj, cosr, sinr, lgam, ret_norm_g, ry, dgated, states, dproj, RH, T=T_RET)
    def delta_epi(acc, o):
        rows = acc.shape[0]
        return acc, [jnp.broadcast_to(jnp.sum(acc[:, lo:lo + V_HEAD] * o[:, lo:lo + V_HEAD], axis=-1, keepdims=True),
                                      (rows, LANES)) for lo in range(0, acc.shape[1], V_HEAD)]

    dob, delta = _mm(dy_mla, full["w_mla_o"], mode="nt", outs=[BF16], name="mla_o_bwd_x", epi=delta_epi,
                     extras=(my,), more_outs=lambda tm, tn: [
                         (jax.ShapeDtypeStruct((MH, S, LANES), F32),
                          pl.BlockSpec((tn // V_HEAD, tm, LANES), lambda i, j, k: (j, i, 0)))])
    g_w_mla_o = _mm(my_b, dy_mla, mode="tn", outs=[BF16], name="mla_o_bwd_w")
    g2 = ("w_out", "w_ret_o", "w_mla_o")
    swap2 = swap_begin("g2", (g_w_out, g_w_ret_o, g_w_mla_o))
    dq_all, dkv_all, dkpe_h = _attn_bwd(qf, kf, vb, dob, lse2, delta, pe_tabs, MH, T=T_ATT, deps=(swap2[3],))
    pcs2, theirs2, sums2 = swap_end("g2", g2, swap2, dkv_all)
    ssem2, rsem2, bufs2, token2 = scatter_begin("g2", sums2)
    dkpe = _kpe_sum(dkpe_h, pe_tabs, MH)
    dcqn = _mm(dq_all, wq, mode="nt", outs=[F32], name="q_bwd_x", deps=(token2,))
    g_wq = _mm(cqn, dq_all, mode="tn", outs=[BF16], name="q_bwd_w")
    dckvn = _mm(dkv_all, wkv, mode="nt", outs=[F32], name="kv_bwd_x")
    g_wkv = _mm(ckvn, dkv_all, mode="tn", outs=[BF16], name="kv_bwd_w")
    dproj, g_q_a, g_kv_a = _norm_pair_bwd(dcqn, dckvn, proj, rstd_q, rstd_kv, q_a_norm_g, kv_a_norm_g, dproj, off_cq)
    g_wa = _mm(dproj, u, mode="tn", outs=[BF16], name="in_bwd_w")
    g_wkpe = _mm(dkpe, u, mode="tn", outs=[BF16], name="kpe_bwd_w")

    there = sorted(runs)
    g_parts = [g_wa, g_wkpe]
    g_w_in = jnp.stack([jnp.concatenate(
        [p for s0, w, d0 in there for a, b in [(max(s0, jj * c_sh), min(s0 + w, (jj + 1) * c_sh))] if a < b
         for p in take(g_parts, d0 + a - s0, b - a)] + [jnp.zeros((c_pad - c_sh, D), BF16)], axis=0)
        for jj in range(N_CHIPS)])
    gq = g_wq.reshape(QL, MH, 2 * LANES)[:, :, :QK_NOPE + QK_ROPE].reshape(QL, MH * (QK_NOPE + QK_ROPE))
    g3 = ("w_in", "w_q_b", "w_kv_b")
    def chip_sums(names, pcs, theirs, recv):
        return [_sum_chips(p, t, r, place, name="sum_chips_" + k) for k, p, t, r in zip(names, pcs, theirs, recv)]

    swap3 = swap_begin("g3", (g_w_in, _split_cols(gq), _split_cols(g_wkv)))
    bufs1 = _split_wait(ssem1, rsem1, bufs1, swap3[3], _scatter_plan(len(g1)), name="scatter_g1_wait")
    halves1 = chip_sums(g1, pcs1, theirs1, bufs1[len(g1):])
    pcs3, theirs3, sums3 = swap_end("g3", g3, swap3, halves1[-1])
    ssem3, rsem3, bufs3, token3 = scatter_begin("g3", sums3)
    bufs2 = _split_wait(ssem2, rsem2, bufs2, token3, _scatter_plan(len(g2)), name="scatter_g2_wait")
    halves12 = halves1 + chip_sums(g2, pcs2, theirs2, bufs2[len(g2):])
    jssem, jrsem, halves12, join_token = _split_start(halves12, _join_plan, len(halves12), name="join_g12_start")
    du = _mm(dproj, wa, mode="nn", outs=[F32], name="in_bwd_x", tk=2816, tail=(dkpe, wkpe), deps=(join_token,))
    dx, g_norm_mix = _rmsnorm_bwd(du, xs, rstd0, norm_mix_g, name="norm_mix_bwd", res=dh1)

    bufs3 = _split_wait(ssem3, rsem3, bufs3, dx, _scatter_plan(len(g3)), name="scatter_g3_wait")
    halves12 = _split_wait(jssem, jrsem, halves12, dx, _join_plan, name="join_g12_wait")
    j3ssem, j3rsem, halves3, join3_token = _split_start(
        chip_sums(g3, pcs3, theirs3, bufs3[len(g3):]), _join_plan, len(g3), name="join_g3_start")

    small = ("norm_mix_g", "ret_norm_g", "q_a_norm_g", "kv_a_norm_g", "norm_mlp_g", "norm_f_g")
    g_small = [g_norm_mix, g_ret_norm, g_q_a, g_kv_a, g_norm_mlp, g_norm_f]
    red = _allreduce_small(g_small, loss11)
    loss = red[0, red.shape[1] - 1]
    w_small = [norm_mix_g, ret_norm_g, q_a_norm_g, kv_a_norm_g, norm_mlp_g, norm_f_g]
    m_small = [m_norm_mix_g, m_ret_norm_g, m_q_a_norm_g, m_kv_a_norm_g, m_norm_mlp_g, m_norm_f_g]
    v_small = [v_norm_mix_g, v_ret_norm_g, v_q_a_norm_g, v_kv_a_norm_g, v_norm_mlp_g, v_norm_f_g]
    row = lambda a: a.reshape(1, -1)
    upd = _adamw_small(red, [row(a) for a in w_small], [row(a) for a in m_small], [row(a) for a in v_small])
    out_g, out_d, out_m, out_v = {}, {}, {}, {}
    for k, wv, (g_, d_, m_, v_) in zip(small, w_small, upd):
        out_g[k], out_d[k], out_m[k], out_v[k] = [a.reshape(wv.shape) for a in (g_, d_, m_, v_)]

    def adamw_shard(k, joined, deps=()):
        g = joined.reshape(2 * joined.shape[1], joined.shape[2])
        res = _rows_call(lambda w, g, m, v: (g,) + _adamw_vals(w, g, m, v),
                         [w_sh[k], g, m_sh[k], v_sh[k]], [F32] * 4, name="adamw_" + k, deps=deps)
        if k == "w_in":
            res = [r.T for r in res]
        out_g[k], out_d[k], out_m[k], out_v[k] = [r[None] for r in res]
        return res[0]

    for k, joined in zip(g1 + g2, halves12):
        last = adamw_shard(k, joined, deps=(join3_token,))
    halves3 = _split_wait(j3ssem, j3rsem, halves3, last, _join_plan, name="join_g3_wait")
    for k, joined in zip(g3, halves3):
        adamw_shard(k, joined)

    order = ("norm_mix_g", "w_in", "ret_norm_g", "w_ret_o", "q_a_norm_g", "w_q_b", "kv_a_norm_g", "w_kv_b",
             "w_mla_o", "w_out", "norm_mlp_g", "w_up", "w_down", "norm_f_g")
    return (loss, dx.reshape(1, S, D), *[out_g[k] for k in order], *[out_d[k] for k in order],
            *[out_m[k] for k in order], *[out_v[k] for k in order])
```

```python
import math

import jax
import jax.numpy as jnp
from jax import lax
from jax.experimental import pallas as pl
from jax.experimental.pallas import tpu as pltpu

F32 = jnp.float32
BF16 = jnp.bfloat16

EPS = 1e-6
ROPE_THETA = 10000.0
CHUNK = 64
RET_QK = 128
RET_V = 256
RET_HEAD_COLS = 2 * RET_QK + 2 * RET_V
QK_NOPE = 128
QK_ROPE = 64
V_HEAD = 128
LANES = 128
LOG2E = math.log2(math.e)

ADAM_LR = 0.001
ADAM_B1 = 0.9
ADAM_B2 = 0.999
ADAM_EPS = 1e-08
ADAM_WD = 0.01
ADAM_STEP = 10

N_CHIPS = 4
VMEM_LIMIT = 56 * 1024 * 1024
MESH = pl.DeviceIdType.MESH
NEG = -1e30


def _pallas(body, **kw):
    return pl.pallas_call(body, **kw)


def _params(sem=None):
    return pltpu.CompilerParams(dimension_semantics=sem, vmem_limit_bytes=VMEM_LIMIT)


def _tile(n, want):
    t = min(n, want)
    while n % t:
        t //= 2
    return t


_ANY = pl.BlockSpec(memory_space=pl.ANY)
TN_BF16_TK = 4096


def _mm(a, b, *, mode, outs, name, epi=None, extras=(), deps=(), out_shards=False, more_outs=None, tail=None,
        tm=1024, tn=1024, tk=2048):
    shards = b.shape[0] if b.ndim == 3 else 1
    brows, bcols = b.shape[-2], b.shape[-1] * shards
    if mode == "nn":
        (M, K), N = a.shape, bcols
    elif mode == "nt":
        (M, K), N = a.shape, brows
    else:
        (K, M), N = a.shape, bcols
    if mode == "tn" and a.dtype == BF16 and b.dtype == BF16:
        tk = max(tk, TN_BF16_TK)
    tm = _tile(M, tm)
    tn = _tile(N // (shards if mode == "nn" else 1) // (N_CHIPS if out_shards else 1), tn)
    tk = _tile(K // (shards if mode == "nt" else 1), tk)
    nk = K // tk
    if mode == "nn":
        a_spec = pl.BlockSpec((tm, tk), lambda i, j, k: (i, k))
        dims = (((1,), (0,)), ((), ()))
        if shards > 1:
            per = N // shards // tn
            b_spec = pl.BlockSpec((None, tk, tn), lambda i, j, k: (j // per, k, j % per))
        else:
            b_spec = pl.BlockSpec((tk, tn), lambda i, j, k: (k, j))
    elif mode == "nt":
        a_spec = pl.BlockSpec((tm, tk), lambda i, j, k: (i, k))
        dims = (((1,), (1,)), ((), ()))
        if shards > 1:
            per = K // shards // tk
            b_spec = pl.BlockSpec((None, tn, tk), lambda i, j, k: (k // per, j, k % per))
        else:
            b_spec = pl.BlockSpec((tn, tk), lambda i, j, k: (j, k))
    else:
        assert shards == 1
        a_spec = pl.BlockSpec((tk, tm), lambda i, j, k: (k, i))
        b_spec = pl.BlockSpec((tk, tn), lambda i, j, k: (k, j))
        dims = (((0,), (0,)), ((), ()))
    if out_shards:
        assert not extras
        oper = N // N_CHIPS // tn
        o_spec = pl.BlockSpec((None, tm, tn), lambda i, j, k: (j // oper, i, j % oper))
        o_shape = (N_CHIPS, M, N // N_CHIPS)
    else:
        o_spec = pl.BlockSpec((tm, tn), lambda i, j, k: (i, j))
        o_shape = (M, N)
    more = [] if more_outs is None else more_outs(tm, tn)
    ex_arrays = [e[0] if isinstance(e, tuple) else e for e in extras]
    ex_specs = [pl.BlockSpec((tm, tn), lambda i, j, k, off=e[1] // tn, st=e[2]: (i, off + st * j))
                if isinstance(e, tuple) else o_spec for e in extras]
    n_ex, n_out, n_dep = len(extras), len(outs) + len(more), len(deps)
    if epi is None:
        epi = lambda acc: (acc,)
    tails, tail_specs = [], []
    if tail is not None:
        assert mode == "nn"
        tails = list(tail)
        k2 = tail[0].shape[1]
        tail_specs = [pl.BlockSpec((tm, k2), lambda i, j, k: (i, 0)), pl.BlockSpec((k2, tn), lambda i, j, k: (0, j))]
    n_tail = len(tails)

    def body(*refs):
        a_ref, b_ref = refs[0], refs[1]
        ex_refs = refs[2:2 + n_ex]
        t_refs = refs[2 + n_ex:2 + n_ex + n_tail]
        first_out = 2 + n_ex + n_tail + n_dep
        o_refs = refs[first_out:first_out + n_out]
        part = lax.dot_general(a_ref[...].astype(BF16), b_ref[...].astype(BF16), dims,
                               preferred_element_type=F32)

        def finish(acc):
            if n_tail:
                acc = acc + lax.dot_general(t_refs[0][...].astype(BF16), t_refs[1][...].astype(BF16), dims,
                                            preferred_element_type=F32)
            vals = epi(acc, *[r[...] for r in ex_refs])
            for r, v in zip(o_refs, vals):
                if isinstance(v, (list, tuple)):
                    for lead, piece in enumerate(v):
                        r[lead] = piece.astype(r.dtype)
                else:
                    r[...] = v.astype(r.dtype)

        if nk == 1:
            finish(part)
        else:
            acc_ref = refs[-1]
            k = pl.program_id(2)

            @pl.when(k == 0)
            def _():
                acc_ref[...] = part

            @pl.when(k > 0)
            def _():
                acc_ref[...] += part

            @pl.when(k == nk - 1)
            def _():
                finish(acc_ref[...])

    res = _pallas(
        body, name=name, grid=(M // tm, N // tn, nk),
        in_specs=[a_spec, b_spec] + ex_specs + tail_specs + [_ANY] * n_dep,
        out_specs=[o_spec] * len(outs) + [spec for _, spec in more],
        out_shape=[jax.ShapeDtypeStruct(o_shape, d) for d in outs] + [shape for shape, _ in more],
        scratch_shapes=[pltpu.VMEM((tm, tn), F32)] if nk > 1 else [],
        compiler_params=_params(("parallel", "parallel", "arbitrary")),
    )(a, b, *ex_arrays, *tails, *deps)
    return res[0] if n_out == 1 else res


def _rmsnorm_fwd(x, g, *, name, tr=512):
    S, W = x.shape
    tr = _tile(S, tr)

    def body(x_ref, g_ref, y_ref, r_ref):
        xv = x_ref[...]
        rstd = lax.rsqrt(jnp.mean(xv * xv, axis=-1, keepdims=True) + EPS)
        y_ref[...] = (xv * rstd * g_ref[...]).astype(BF16)
        r_ref[...] = rstd

    return _pallas(
        body, name=name, grid=(S // tr,),
        in_specs=[pl.BlockSpec((tr, W), lambda i: (i, 0)), pl.BlockSpec((1, W), lambda i: (0, 0))],
        out_specs=[pl.BlockSpec((tr, W), lambda i: (i, 0)), pl.BlockSpec((tr, 1), lambda i: (i, 0))],
        out_shape=[jax.ShapeDtypeStruct((S, W), BF16), jax.ShapeDtypeStruct((S, 1), F32)],
        compiler_params=_params(("parallel",)),
    )(x, g)


def _rmsnorm_bwd(dy, x, rstd, g, *, name, res=None, deps=(), bf16_copy=0, tr=512):
    S, W = x.shape
    tr = _tile(S, tr)
    has_res = res is not None

    def body(*refs):
        dy_ref, x_ref, r_ref, g_ref = refs[:4]
        dx_ref, dg_ref = refs[-2 - bf16_copy], refs[-1 - bf16_copy]
        rstd_v = r_ref[...]
        xhat = x_ref[...] * rstd_v
        dyv = dy_ref[...].astype(F32)
        dyg = dyv * g_ref[...]
        dx = rstd_v * (dyg - xhat * jnp.mean(dyg * xhat, axis=-1, keepdims=True))
        if has_res:
            dx = dx + refs[4][...]
        dx_ref[...] = dx.astype(dx_ref.dtype)
        if bf16_copy:
            refs[-1][...] = dx.astype(BF16)
        part = jnp.sum(dyv * xhat, axis=0, keepdims=True)

        @pl.when(pl.program_id(0) == 0)
        def _():
            dg_ref[...] = part

        @pl.when(pl.program_id(0) > 0)
        def _():
            dg_ref[...] += part

    row = pl.BlockSpec((tr, W), lambda i: (i, 0))
    ins = [dy, x, rstd, g] + ([res] if has_res else [])
    in_specs = [row, row, pl.BlockSpec((tr, 1), lambda i: (i, 0)),
                pl.BlockSpec((1, W), lambda i: (0, 0))] + ([row] if has_res else [])
    ins += list(deps)
    in_specs += [_ANY] * len(deps)
    return _pallas(
        body, name=name, grid=(S // tr,), in_specs=in_specs,
        out_specs=[row, pl.BlockSpec((1, W), lambda i: (0, 0))] + [row] * bf16_copy,
        out_shape=[jax.ShapeDtypeStruct((S, W), F32), jax.ShapeDtypeStruct((1, W), F32)]
        + [jax.ShapeDtypeStruct((S, W), BF16)] * bf16_copy,
        compiler_params=_params(("arbitrary",)),
    )(*ins)


def _norm_pair_fwd(proj, g_a, g_b, off, *, tr=512):
    S = proj.shape[0]
    wa_, wb_ = g_a.shape[1], g_b.shape[1]
    W = wa_ + wb_
    tr = _tile(S, tr)

    def body(x_ref, ga_ref, gb_ref, ya_ref, yb_ref, ra_ref, rb_ref):
        for lo, hi, g_ref, y_ref, r_ref in ((0, wa_, ga_ref, ya_ref, ra_ref), (wa_, W, gb_ref, yb_ref, rb_ref)):
            xv = x_ref[:, lo:hi]
            rstd = lax.rsqrt(jnp.mean(xv * xv, axis=-1, keepdims=True) + EPS)
            y_ref[...] = (xv * rstd * g_ref[...]).astype(BF16)
            r_ref[...] = rstd

    one = pl.BlockSpec((tr, 1), lambda i: (i, 0))
    return _pallas(
        body, name="norm_qkv", grid=(S // tr,),
        in_specs=[pl.BlockSpec((tr, W), lambda i: (i, off // W)), pl.BlockSpec((1, wa_), lambda i: (0, 0)),
                  pl.BlockSpec((1, wb_), lambda i: (0, 0))],
        out_specs=[pl.BlockSpec((tr, wa_), lambda i: (i, 0)), pl.BlockSpec((tr, wb_), lambda i: (i, 0)), one, one],
        out_shape=[jax.ShapeDtypeStruct((S, wa_), BF16), jax.ShapeDtypeStruct((S, wb_), BF16),
                   jax.ShapeDtypeStruct((S, 1), F32), jax.ShapeDtypeStruct((S, 1), F32)],
        compiler_params=_params(("parallel",)),
    )(proj, g_a, g_b)


def _norm_pair_bwd(dy_a, dy_b, proj, r_a, r_b, g_a, g_b, dproj, off, *, tr=512):
    S = proj.shape[0]
    wa_, wb_ = g_a.shape[1], g_b.shape[1]
    W = wa_ + wb_
    tr = _tile(S, tr)

    def body(dya_ref, dyb_ref, x_ref, ra_ref, rb_ref, ga_ref, gb_ref, _, dx_ref, dga_ref, dgb_ref):
        first = pl.program_id(0) == 0
        for lo, hi, dy_ref, r_ref, g_ref, dg_ref in ((0, wa_, dya_ref, ra_ref, ga_ref, dga_ref),
                                                    (wa_, W, dyb_ref, rb_ref, gb_ref, dgb_ref)):
            rstd = r_ref[...]
            xhat = x_ref[:, lo:hi] * rstd
            dyv = dy_ref[...]
            dyg = dyv * g_ref[...]
            dx_ref[:, lo:hi] = (rstd * (dyg - xhat * jnp.mean(dyg * xhat, axis=-1, keepdims=True))).astype(dx_ref.dtype)
            part = jnp.sum(dyv * xhat, axis=0, keepdims=True)

            @pl.when(first)
            def _():
                dg_ref[...] = part

            @pl.when(jnp.logical_not(first))
            def _():
                dg_ref[...] += part

    one = pl.BlockSpec((tr, 1), lambda i: (i, 0))
    cols = pl.BlockSpec((tr, W), lambda i: (i, off // W))
    va, vb = pl.BlockSpec((1, wa_), lambda i: (0, 0)), pl.BlockSpec((1, wb_), lambda i: (0, 0))
    return _pallas(
        body, name="norm_qkv_bwd", grid=(S // tr,),
        in_specs=[pl.BlockSpec((tr, wa_), lambda i: (i, 0)), pl.BlockSpec((tr, wb_), lambda i: (i, 0)), cols,
                  one, one, va, vb, _ANY],
        out_specs=[cols, va, vb],
        out_shape=[jax.ShapeDtypeStruct(dproj.shape, dproj.dtype), jax.ShapeDtypeStruct((1, wa_), F32),
                   jax.ShapeDtypeStruct((1, wb_), F32)],
        input_output_aliases={7: 0},
        compiler_params=_params(("arbitrary",)),
    )(dy_a, dy_b, proj, r_a, r_b, g_a, g_b, dproj)


def _final_loss(h2, g, target, *, tr=512):
    S, D = h2.shape
    tr = _tile(S, tr)

    def body(h_ref, g_ref, t_ref, loss_ref, dh_ref, dhb_ref, dg_ref):
        hv = h_ref[...]
        rstd = lax.rsqrt(jnp.mean(hv * hv, axis=-1, keepdims=True) + EPS)
        xhat = hv * rstd
        e = xhat * g_ref[...] - t_ref[...]
        lpart = (0.5 / D) * jnp.sum(jnp.sum(e * e, axis=-1, keepdims=True), axis=0, keepdims=True)
        dy = e * (1.0 / D)
        dyg = dy * g_ref[...]
        dh = rstd * (dyg - xhat * jnp.mean(dyg * xhat, axis=-1, keepdims=True))
        dh_ref[...] = dh
        dhb_ref[...] = dh.astype(BF16)
        gpart = jnp.sum(dy * xhat, axis=0, keepdims=True)

        @pl.when(pl.program_id(0) == 0)
        def _():
            loss_ref[...] = lpart
            dg_ref[...] = gpart

        @pl.when(pl.program_id(0) > 0)
        def _():
            loss_ref[...] += lpart
            dg_ref[...] += gpart

    row = pl.BlockSpec((tr, D), lambda i: (i, 0))
    vec = pl.BlockSpec((1, D), lambda i: (0, 0))
    return _pallas(
        body, name="final_loss", grid=(S // tr,), in_specs=[row, vec, row],
        out_specs=[pl.BlockSpec((1, 1), lambda i: (0, 0)), row, row, vec],
        out_shape=[jax.ShapeDtypeStruct((1, 1), F32), jax.ShapeDtypeStruct((S, D), F32),
                   jax.ShapeDtypeStruct((S, D), BF16), jax.ShapeDtypeStruct((1, D), F32)],
        compiler_params=_params(("arbitrary",)),
    )(h2, g, target)


def _sigmoid(v):
    return 1.0 / (1.0 + jnp.exp(-v))


def _rope128(t, cos_full, sin_signed):
    return t * cos_full + pltpu.roll(t, RET_QK // 2, 1) * sin_signed


def _rope128_t(d, cos_full, sin_signed):
    return d * cos_full + pltpu.roll(d * sin_signed, RET_QK // 2, 1)


def _ret_consts(lg, T):
    pos = lax.broadcasted_iota(jnp.int32, (T, 1), 0).astype(F32)
    qd = jnp.exp(lg * (pos + 1.0))
    kd = jnp.exp(lg * (T - 1.0 - pos))
    n = lax.broadcasted_iota(jnp.int32, (T, T), 0)
    m = lax.broadcasted_iota(jnp.int32, (T, T), 1)
    vis = (m // CHUNK) <= (n // CHUNK)
    dist = jnp.abs(n - m).astype(F32)
    decay = jnp.where(vis, jnp.exp(lg * dist), 0.0)
    cdec = jnp.exp(lg * float(T))
    return qd, kd, decay, cdec


def _dot(a, b, dims):
    return lax.dot_general(a.astype(BF16), b.astype(BF16), (dims, ((), ())), preferred_element_type=F32)


NN = ((1,), (0,))
NT = ((1,), (1,))
TN = ((0,), (0,))
_RQ = slice(0, RET_QK)
_RK = slice(RET_QK, 2 * RET_QK)
_RV = slice(2 * RET_QK, 2 * RET_QK + RET_V)
_RG = slice(2 * RET_QK + RET_V, RET_HEAD_COLS)


RET_GROUP = 8


def _head_cols(h, part):
    return slice(h * RET_HEAD_COLS + part.start, h * RET_HEAD_COLS + part.stop)


def _ret_fwd(proj, cosr, sinr, lgam, gain, RH, *, T):
    S = proj.shape[0]
    nb = S // T
    G = _tile(RH, RET_GROUP)
    heads = range(G)
    scale = RET_QK ** -0.5

    def body(p_ref, cos_ref, sin_ref, lg_ref, gain_ref, ry_ref, gated_ref, st_ref, state):
        b = pl.program_id(1)

        @pl.when(b == 0)
        def _():
            state[...] = jnp.zeros_like(state)

        consts = [_ret_consts(lg_ref[h, 0:1, 0:1], T) for h in heads]
        cosv, sinv = cos_ref[...], sin_ref[...]
        q = [_rope128(p_ref[:, _head_cols(h, _RQ)], cosv, sinv) for h in heads]
        k = [_rope128(p_ref[:, _head_cols(h, _RK)], cosv, sinv) * scale for h in heads]
        v = [p_ref[:, _head_cols(h, _RV)] for h in heads]
        sprev = [state[h] for h in heads]
        for h in heads:
            st_ref[h] = sprev[h]
        a = [_dot(q[h], k[h], NT) for h in heads]
        qs = [_dot(q[h] * consts[h][0], sprev[h], NN) for h in heads]
        kv = [_dot(k[h] * consts[h][1], v[h], TN) for h in heads]
        o = [_dot(a[h] * consts[h][2], v[h], NN) + qs[h] for h in heads]
        for h in heads:
            state[h] = sprev[h] * consts[h][3] + kv[h]
            vals = slice(h * RET_V, (h + 1) * RET_V)
            ry_ref[:, vals] = o[h]
            mu = jnp.mean(o[h], axis=-1, keepdims=True)
            oc = o[h] - mu
            var = jnp.mean(oc * oc, axis=-1, keepdims=True)
            t = oc * lax.rsqrt(var + EPS) * gain_ref[:, vals]
            gv = p_ref[:, _head_cols(h, _RG)]
            gated_ref[:, vals] = (t * (gv * _sigmoid(gv))).astype(BF16)

    return _pallas(
        body, name="ret_fwd", grid=(RH // G, nb),
        in_specs=[pl.BlockSpec((T, G * RET_HEAD_COLS), lambda h, b: (b, h)),
                  pl.BlockSpec((T, RET_QK), lambda h, b: (b, 0)),
                  pl.BlockSpec((T, RET_QK), lambda h, b: (b, 0)),
                  pl.BlockSpec((G, 8, LANES), lambda h, b: (h, 0, 0)),
                  pl.BlockSpec((1, G * RET_V), lambda h, b: (0, h))],
        out_specs=[pl.BlockSpec((T, G * RET_V), lambda h, b: (b, h)),
                   pl.BlockSpec((T, G * RET_V), lambda h, b: (b, h)),
                   pl.BlockSpec((G, None, RET_QK, RET_V), lambda h, b: (h, b, 0, 0))],
        out_shape=[jax.ShapeDtypeStruct((S, RH * RET_V), F32), jax.ShapeDtypeStruct((S, RH * RET_V), BF16),
                   jax.ShapeDtypeStruct((RH, nb, RET_QK, RET_V), F32)],
        scratch_shapes=[pltpu.VMEM((G, RET_QK, RET_V), F32)],
        compiler_params=_params(("parallel", "arbitrary")),
    )(proj, cosr, sinr, lgam, gain)


def _ret_bwd(proj, cosr, sinr, lgam, gain, ry, dgated, states, dproj, RH, *, T):
    S = proj.shape[0]
    nb = S // T
    G = _tile(RH, RET_GROUP)
    heads = range(G)
    scale = RET_QK ** -0.5

    def body(p_ref, cos_ref, sin_ref, lg_ref, gain_ref, ry_ref, dg_ref, st_ref, _, dp_ref, dgain_ref, dstate):
        b = pl.program_id(1)

        @pl.when(b == 0)
        def _():
            dstate[...] = jnp.zeros_like(dstate)

        consts = [_ret_consts(lg_ref[h, 0:1, 0:1], T) for h in heads]
        qd, kd, decay, cdec = [[c[i] for c in consts] for i in range(4)]
        cosv, sinv = cos_ref[...], sin_ref[...]
        q = [_rope128(p_ref[:, _head_cols(h, _RQ)], cosv, sinv) for h in heads]
        k = [_rope128(p_ref[:, _head_cols(h, _RK)], cosv, sinv) * scale for h in heads]
        v = [p_ref[:, _head_cols(h, _RV)] for h in heads]
        sprev = [st_ref[h] for h in heads]
        ds_new = [dstate[h] for h in heads]
        a = [_dot(q[h], k[h], NT) for h in heads]
        do, gparts = [], []
        for h in heads:
            vals = slice(h * RET_V, (h + 1) * RET_V)
            o = ry_ref[:, vals]
            mu = jnp.mean(o, axis=-1, keepdims=True)
            oc = o - mu
            rstd = lax.rsqrt(jnp.mean(oc * oc, axis=-1, keepdims=True) + EPS)
            ryn = oc * rstd
            gainv = gain_ref[:, vals]
            gv = p_ref[:, _head_cols(h, _RG)]
            sg = _sigmoid(gv)
            dgt = dg_ref[:, vals]
            dt = dgt * (gv * sg)
            dp_ref[:, _head_cols(h, _RG)] = (dgt * (ryn * gainv) * (sg * (1.0 + gv * (1.0 - sg)))).astype(BF16)
            gparts.append(jnp.sum(dt * ryn, axis=0, keepdims=True))
            dryn = dt * gainv
            do.append(rstd * (dryn - jnp.mean(dryn, axis=-1, keepdims=True)
                              - ryn * jnp.mean(dryn * ryn, axis=-1, keepdims=True)))
        gpart = jnp.concatenate(gparts, axis=1)

        @pl.when(b == 0)
        def _():
            dgain_ref[...] = gpart

        @pl.when(b > 0)
        def _():
            dgain_ref[...] += gpart

        dpm = [_dot(do[h], v[h], NT) for h in heads]
        dq_s = [_dot(do[h], sprev[h], NT) for h in heads]
        dk_s = [_dot(v[h], ds_new[h], NT) for h in heads]
        dv_s = [_dot(k[h] * kd[h], ds_new[h], NN) for h in heads]
        dst = [_dot(q[h] * qd[h], do[h], TN) for h in heads]
        a = [a[h] * decay[h] for h in heads]
        dpm = [dpm[h] * decay[h] for h in heads]
        dv = [_dot(a[h], do[h], TN) + dv_s[h] for h in heads]
        dq = [_dot(dpm[h], k[h], NN) + dq_s[h] * qd[h] for h in heads]
        dk = [(_dot(dpm[h], q[h], TN) + dk_s[h] * kd[h]) * scale for h in heads]
        for h in heads:
            dstate[h] = ds_new[h] * cdec[h] + dst[h]
            dp_ref[:, _head_cols(h, _RV)] = dv[h].astype(BF16)
            dp_ref[:, _head_cols(h, _RQ)] = _rope128_t(dq[h], cosv, sinv).astype(BF16)
            dp_ref[:, _head_cols(h, _RK)] = _rope128_t(dk[h], cosv, sinv).astype(BF16)

    rb = lambda b: nb - 1 - b
    return _pallas(
        body, name="ret_bwd", grid=(RH // G, nb),
        in_specs=[pl.BlockSpec((T, G * RET_HEAD_COLS), lambda h, b: (rb(b), h)),
                  pl.BlockSpec((T, RET_QK), lambda h, b: (rb(b), 0)),
                  pl.BlockSpec((T, RET_QK), lambda h, b: (rb(b), 0)),
                  pl.BlockSpec((G, 8, LANES), lambda h, b: (h, 0, 0)),
                  pl.BlockSpec((1, G * RET_V), lambda h, b: (0, h)),
                  pl.BlockSpec((T, G * RET_V), lambda h, b: (rb(b), h)),
                  pl.BlockSpec((T, G * RET_V), lambda h, b: (rb(b), h)),
                  pl.BlockSpec((G, None, RET_QK, RET_V), lambda h, b: (h, rb(b), 0, 0)),
                  _ANY],
        out_specs=[pl.BlockSpec((T, G * RET_HEAD_COLS), lambda h, b: (rb(b), h)),
                   pl.BlockSpec((1, G * RET_V), lambda h, b: (0, h))],
        out_shape=[jax.ShapeDtypeStruct(dproj.shape, dproj.dtype), jax.ShapeDtypeStruct((1, RH * RET_V), F32)],
        scratch_shapes=[pltpu.VMEM((G, RET_QK, RET_V), F32)],
        input_output_aliases={8: 0},
        compiler_params=_params(("parallel", "arbitrary")),
    )(proj, cosr, sinr, lgam, gain, ry, dgated, states, dproj)


def _rope_pe(t, c, s1, s2):
    return t * c + pltpu.roll(t, LANES - QK_ROPE // 2, 1) * s1 + pltpu.roll(t, QK_ROPE // 2, 1) * s2


def _rope_pe_t(d, c, s1, s2):
    return d * c + pltpu.roll(d * s1, QK_ROPE // 2, 1) + pltpu.roll(d * s2, LANES - QK_ROPE // 2, 1)


ATTN_C2 = (QK_NOPE + QK_ROPE) ** -0.5 * LOG2E


def _qkv_proj(cqn, ckvn, wq, wkv, kpe, tabs, MH, *, tm=512, heads=4):
    S = cqn.shape[0]
    tm = _tile(S, tm)
    hb = _tile(MH, heads)
    W = 2 * LANES
    c_t, s1_t, s2_t = tabs

    def body(cq_ref, ckv_ref, wq_ref, wkv_ref, kpe_ref, c_ref, s1_ref, s2_ref, qf_ref, kf_ref, v_ref):
        c, s1, s2 = c_ref[...], s1_ref[...], s2_ref[...]
        q = _dot(cq_ref[...], wq_ref[...], NN)
        kv = _dot(ckv_ref[...], wkv_ref[...], NN)
        kper = _rope_pe(kpe_ref[...], c, s1, s2).astype(BF16)
        for h in range(hb):
            lo, mid, hi = h * W, h * W + QK_NOPE, (h + 1) * W
            qf_ref[:, lo:mid] = (q[:, lo:mid] * ATTN_C2).astype(BF16)
            qf_ref[:, mid:hi] = (_rope_pe(q[:, mid:hi], c, s1, s2) * ATTN_C2).astype(BF16)
            kf_ref[:, lo:mid] = kv[:, lo:mid].astype(BF16)
            kf_ref[:, mid:hi] = kper
            v_ref[:, h * V_HEAD:(h + 1) * V_HEAD] = kv[:, mid:hi].astype(BF16)

    tab = pl.BlockSpec((tm, LANES), lambda i, j: (i, 0))
    grp = pl.BlockSpec((tm, hb * W), lambda i, j: (i, j))
    return _pallas(
        body, name="qkv_proj", grid=(S // tm, MH // hb),
        in_specs=[pl.BlockSpec((tm, cqn.shape[1]), lambda i, j: (i, 0)),
                  pl.BlockSpec((tm, ckvn.shape[1]), lambda i, j: (i, 0)),
                  pl.BlockSpec((wq.shape[0], hb * W), lambda i, j: (0, j)),
                  pl.BlockSpec((wkv.shape[0], hb * W), lambda i, j: (0, j)), tab, tab, tab, tab],
        out_specs=[grp, grp, pl.BlockSpec((tm, hb * V_HEAD), lambda i, j: (i, j))],
        out_shape=[jax.ShapeDtypeStruct((S, MH * W), BF16)] * 2 + [jax.ShapeDtypeStruct((S, MH * V_HEAD), BF16)],
        compiler_params=_params(("parallel", "parallel")),
    )(cqn, ckvn, wq, wkv, kpe, c_t, s1_t, s2_t)


def _chunk_mask(T):
    n = lax.broadcasted_iota(jnp.int32, (T, T), 0)
    m = lax.broadcasted_iota(jnp.int32, (T, T), 1)
    return (m // CHUNK) <= (n // CHUNK)


def _lanes_to(v, width):
    return jnp.tile(v, (1, width // LANES))


def _attn_fwd(qf, kf, vb, MH, *, T, heads, name, prev=(), deps=()):
    S = qf.shape[0]
    nt = S // T
    n_skip = len(prev) + len(deps)

    def body(q_ref, k_ref, v_ref, *rest):
        o_ref, ob_ref, lse_ref, m_sc, l_sc, acc_sc, s_a, s_b = rest[n_skip:]
        qi = pl.program_id(1)
        m_sc[...] = jnp.full_like(m_sc, NEG)
        l_sc[...] = jnp.zeros_like(l_sc)
        acc_sc[...] = jnp.zeros_like(acc_sc)

        def rows_of(kt):
            return pl.ds(pl.multiple_of(kt * T, T), T)

        def scores(kt):
            return _dot(q_ref[...], k_ref[rows_of(kt), :], NT)

        def update(s, kt):
            m_prev = m_sc[...]
            m_new = jnp.maximum(m_prev, jnp.max(s, axis=-1, keepdims=True))
            alpha = jnp.exp2(m_prev - m_new)
            p = jnp.exp2(s - _lanes_to(m_new, T))
            l_sc[...] = alpha * l_sc[...] + jnp.sum(p, axis=-1, keepdims=True)
            acc_sc[...] = alpha * acc_sc[...] + _dot(p, v_ref[rows_of(kt), :], NN)
            m_sc[...] = m_new

        def masked(s):
            return jnp.where(_chunk_mask(T), s, NEG)

        @pl.when(qi == 0)
        def _():
            update(masked(scores(0)), 0)

        @pl.when(qi > 0)
        def _():
            s_a[...] = masked(scores(qi))
            s_b[...] = scores(0)
            update(s_a[...], qi)
            s_a[...] = scores(jnp.minimum(1, qi - 1))
            update(s_b[...], 0)

            def pair(j, carry):
                s_b[...] = scores(2 * j)
                update(s_a[...], 2 * j - 1)
                s_a[...] = scores(jnp.minimum(2 * j + 1, qi - 1))
                update(s_b[...], 2 * j)
                return carry

            lax.fori_loop(1, (qi + 1) // 2, pair, 0)

            @pl.when(qi % 2 == 0)
            def _():
                update(s_a[...], qi - 1)
        l = l_sc[...]
        o = acc_sc[...] / l
        o_ref[...] = o
        ob_ref[...] = o.astype(BF16)
        lse_ref[...] = m_sc[...] + jnp.log(l) * LOG2E

    h0, h1 = heads
    out_shape = [jax.ShapeDtypeStruct((S, MH * LANES), F32), jax.ShapeDtypeStruct((S, MH * LANES), BF16),
                 jax.ShapeDtypeStruct((MH, S, LANES), F32)]
    row = pl.BlockSpec((T, LANES), lambda h, i: (i, h0 + h))
    return _pallas(
        body, name=name, grid=(h1 - h0, nt),
        in_specs=[pl.BlockSpec((T, 2 * LANES), lambda h, i: (i, h0 + h)),
                  pl.BlockSpec((S, 2 * LANES), lambda h, i: (0, h0 + h)),
                  pl.BlockSpec((S, LANES), lambda h, i: (0, h0 + h))] + [_ANY] * (len(prev) + len(deps)),
        out_specs=[row, row, pl.BlockSpec((None, T, LANES), lambda h, i: (h0 + h, i, 0))],
        out_shape=out_shape,
        scratch_shapes=[pltpu.VMEM((T, LANES), F32), pltpu.VMEM((T, LANES), F32), pltpu.VMEM((T, LANES), F32),
                        pltpu.VMEM((T, T), F32), pltpu.VMEM((T, T), F32)],
        input_output_aliases={3 + i: i for i in range(len(prev))},
        compiler_params=_params(("parallel", "parallel")),
    )(qf, kf, vb, *prev, *deps)


def _attn_bwd(qf, kf, vb, dob, lse2, delta, tabs, MH, *, T, deps=()):
    S = qf.shape[0]
    nt = S // T
    scale = (QK_NOPE + QK_ROPE) ** -0.5
    n_dep = len(deps)

    def body(q_ref, k_ref, v_ref, do_ref, lse_ref, dl_ref, c_ref, s1_ref, s2_ref, *rest):
        dqa_ref, dkv_ref, dkpe_ref, dq_ref, dk_sc, dv_sc, s_a, dp_a, s_b, dp_b = rest[n_dep:]
        kj = pl.program_id(1)

        @pl.when(kj == 0)
        def _():
            dq_ref[...] = jnp.zeros_like(dq_ref)

        dk_sc[...] = jnp.zeros_like(dk_sc)
        dv_sc[...] = jnp.zeros_like(dv_sc)

        def rows_of(qt):
            return pl.ds(pl.multiple_of(qt * T, T), T)

        def products(qt):
            rows = rows_of(qt)
            return _dot(q_ref[rows, :], k_ref[...], NT), _dot(do_ref[rows, :], v_ref[...], NT)

        def update(s, dp, qt):
            rows = rows_of(qt)
            q, dov = q_ref[rows, :], do_ref[rows, :]
            p = jnp.exp2(s - _lanes_to(lse_ref[rows, :], T))
            ds = p * (dp - _lanes_to(dl_ref[rows, :], T))
            dv_sc[...] += _dot(p, dov, TN)
            dk_sc[...] += _dot(ds, q, TN)
            dq_ref[rows, :] += _dot(ds, k_ref[...], NN)

        def masked(s):
            return jnp.where(_chunk_mask(T), s, NEG)

        @pl.when(kj == nt - 1)
        def _():
            s, dp = products(kj)
            update(masked(s), dp, kj)

        @pl.when(kj < nt - 1)
        def _():
            s, dp = products(kj)
            s_a[...], dp_a[...] = masked(s), dp
            s_b[...], dp_b[...] = products(kj + 1)
            update(s_a[...], dp_a[...], kj)
            s_a[...], dp_a[...] = products(jnp.minimum(kj + 2, nt - 1))
            update(s_b[...], dp_b[...], kj + 1)

            def pair(j, carry):
                t0 = kj + 2 * j
                s_b[...], dp_b[...] = products(t0 + 1)
                update(s_a[...], dp_a[...], t0)
                s_a[...], dp_a[...] = products(jnp.minimum(t0 + 2, nt - 1))
                update(s_b[...], dp_b[...], t0 + 1)
                return carry

            lax.fori_loop(1, (nt - kj) // 2, pair, 0)

            @pl.when((nt - kj) % 2 == 1)
            def _():
                update(s_a[...], dp_a[...], nt - 1)
        dkv_ref[:, :QK_NOPE] = (dk_sc[:, :QK_NOPE] * (1.0 / LOG2E)).astype(BF16)
        dkv_ref[:, QK_NOPE:] = dv_sc[...].astype(BF16)
        dkpe_ref[...] = dk_sc[:, QK_NOPE:] * (1.0 / LOG2E)

        @pl.when(kj == nt - 1)
        def _():
            dqa_ref[:, :QK_NOPE] = (dq_ref[:, :QK_NOPE] * scale).astype(BF16)
            dqa_ref[:, QK_NOPE:] = (_rope_pe_t(dq_ref[:, QK_NOPE:], c_ref[...], s1_ref[...], s2_ref[...])
                                    * scale).astype(BF16)

    stat = pl.BlockSpec((None, S, LANES), lambda h, j: (h, 0, 0))
    tab = pl.BlockSpec((S, LANES), lambda h, j: (0, 0))
    return _pallas(
        body, name="attn_bwd", grid=(MH, nt),
        in_specs=[pl.BlockSpec((S, 2 * LANES), lambda h, j: (0, h)),
                  pl.BlockSpec((T, 2 * LANES), lambda h, j: (j, h)),
                  pl.BlockSpec((T, LANES), lambda h, j: (j, h)),
                  pl.BlockSpec((S, LANES), lambda h, j: (0, h)), stat, stat, tab, tab, tab] + [_ANY] * n_dep,
        out_specs=[pl.BlockSpec((S, 2 * LANES), lambda h, j: (0, h)),
                   pl.BlockSpec((T, 2 * LANES), lambda h, j: (j, h)),
                   pl.BlockSpec((T, LANES), lambda h, j: (j, h))],
        out_shape=[jax.ShapeDtypeStruct((S, MH * 2 * LANES), BF16), jax.ShapeDtypeStruct((S, MH * 2 * LANES), BF16),
                   jax.ShapeDtypeStruct((S, MH * LANES), F32)],
        scratch_shapes=[pltpu.VMEM((S, 2 * LANES), F32), pltpu.VMEM((T, 2 * LANES), F32), pltpu.VMEM((T, LANES), F32)]
        + [pltpu.VMEM((T, T), F32)] * 4,
        compiler_params=_params(("parallel", "arbitrary")),
    )(qf, kf, vb, dob, lse2, delta, *tabs, *deps)


def _kpe_sum(dkpe_h, tabs, MH, *, tr=256):
    S = dkpe_h.shape[0]
    tr = _tile(S, tr)

    def body(dk_ref, c_ref, s1_ref, s2_ref, dkpe_ref):
        tot = dk_ref[:, :LANES]
        for h in range(1, MH):
            tot = tot + dk_ref[:, h * LANES:(h + 1) * LANES]
        dkpe_ref[...] = _rope_pe_t(tot, c_ref[...], s1_ref[...], s2_ref[...]).astype(BF16)

    tab = pl.BlockSpec((tr, LANES), lambda i: (i, 0))
    return _pallas(
        body, name="kpe_sum", grid=(S // tr,),
        in_specs=[pl.BlockSpec((tr, MH * LANES), lambda i: (i, 0)), tab, tab, tab],
        out_specs=tab, out_shape=jax.ShapeDtypeStruct((S, LANES), BF16),
        compiler_params=_params(("parallel",)),
    )(dkpe_h, *tabs)


ROW_ALIGN = 16


def _blk(R, C, block_bytes=2 << 20):
    cap = max(ROW_ALIGN, block_bytes // (C * 4))
    for t in range(min(R, cap) // ROW_ALIGN * ROW_ALIGN, LANES - 1, -ROW_ALIGN):
        if R % t == 0:
            return t, C
    if R <= cap:
        return R, C
    tc = C
    while R * tc * 4 > block_bytes and tc % (2 * LANES) == 0:
        tc //= 2
    return R, tc


def _rows_call(fn, ins, out_dtypes, *, name, deps=()):
    R, C = ins[0].shape
    tr, tc = _blk(R, C)
    n_in, n_dep = len(ins), len(deps)

    def body(*refs):
        vals = fn(*[r[...] for r in refs[:n_in]])
        for r, v in zip(refs[n_in + n_dep:], vals):
            r[...] = v.astype(r.dtype)

    blk = pl.BlockSpec((tr, tc), lambda i, j: (i, j))
    res = _pallas(
        body, name=name, grid=(R // tr, C // tc), in_specs=[blk] * n_in + [_ANY] * n_dep,
        out_specs=[blk] * len(out_dtypes),
        out_shape=[jax.ShapeDtypeStruct((R, C), d) for d in out_dtypes],
        compiler_params=_params(("parallel", "parallel")),
    )(*ins, *deps)
    return res


def _adamw_vals(w, g, m, v):
    m = ADAM_B1 * m + (1.0 - ADAM_B1) * g
    v = ADAM_B2 * v + (1.0 - ADAM_B2) * (g * g)
    m_hat = m / (1.0 - ADAM_B1 ** ADAM_STEP)
    v_hat = v / (1.0 - ADAM_B2 ** ADAM_STEP)
    delta = -ADAM_LR * (m_hat / (jnp.sqrt(v_hat) + ADAM_EPS) + ADAM_WD * w)
    return delta, m, v


def _sum_pair(p, theirs, place, *, name):
    _, R, C = p.shape
    R2 = R // 2
    tr, tc = _blk(R2, C)
    p4 = p.reshape(N_CHIPS, 2, R2, C)

    def body(place_ref, a_ref, b_ref, o_ref):
        o_ref[...] = (a_ref[...].astype(F32) + b_ref[...].astype(F32)).astype(BF16)

    spec = pltpu.PrefetchScalarGridSpec(
        num_scalar_prefetch=1, grid=(N_CHIPS, R2 // tr, C // tc),
        in_specs=[pl.BlockSpec((None, None, tr, tc), lambda q, i, j, pr: (q, pr[0], i, j)),
                  pl.BlockSpec((None, tr, tc), lambda q, i, j, pr: (q, i, j))],
        out_specs=pl.BlockSpec((None, tr, tc), lambda q, i, j, pr: (q, i, j)))
    return _pallas(body, name=name, grid_spec=spec, out_shape=jax.ShapeDtypeStruct((N_CHIPS, R2, C), BF16),
                   compiler_params=_params(("parallel", "parallel", "parallel")))(place, p4, theirs)


def _sum_chips(p, theirs, recv, place, *, name):
    _, R, C = p.shape
    R2 = R // 2
    tr, tc = _blk(R2, C)
    p4 = p.reshape(N_CHIPS, 2, R2, C)

    def body(place_ref, a_ref, b_ref, r0_ref, r1_ref, r2_ref, o_ref):
        own = a_ref[...].astype(F32) + b_ref[...].astype(F32)
        o_ref[...] = ((own + r0_ref[...].astype(F32)) + r1_ref[...].astype(F32)) + r2_ref[...].astype(F32)

    def slot(k):
        return pl.BlockSpec((None, tr, tc), lambda i, j, pr: (k, i, j))

    spec = pltpu.PrefetchScalarGridSpec(
        num_scalar_prefetch=1, grid=(R2 // tr, C // tc),
        in_specs=[pl.BlockSpec((None, None, tr, tc), lambda i, j, pr: (pr[1], pr[0], i, j)),
                  pl.BlockSpec((None, tr, tc), lambda i, j, pr: (pr[1], i, j)), slot(0), slot(1), slot(2)],
        out_specs=pl.BlockSpec((None, tr, tc), lambda i, j, pr: (pr[0], i, j)))
    return _pallas(body, name=name, grid_spec=spec, out_shape=jax.ShapeDtypeStruct((2, R2, C), F32),
                   compiler_params=_params(("parallel", "parallel")))(place, p4, theirs, recv, recv, recv)


def _me():
    return lax.axis_index("x"), lax.axis_index("y"), lax.axis_index("c")


def _other_chips(x, y):
    return [(1 - x, y), (x, 1 - y), (1 - x, 1 - y)]


def _rcopy(src, dst, ssem, rsem, dev):
    return pltpu.make_async_remote_copy(src_ref=src, dst_ref=dst, send_sem=ssem, recv_sem=rsem,
                                        device_id=dev, device_id_type=MESH)


def _cast_into_slot(w, place, *, name, rows=None, deps=()):
    R, C = w.shape
    rows = R if rows is None else rows
    tr, tc = _blk(R, C)

    def body(place_ref, w_ref, *rest):
        rest[-1][...] = w_ref[...].astype(BF16)

    spec = pltpu.PrefetchScalarGridSpec(
        num_scalar_prefetch=1, grid=(R // tr, C // tc),
        in_specs=[pl.BlockSpec((tr, tc), lambda i, j, pr: (i, j))] + [_ANY] * len(deps),
        out_specs=pl.BlockSpec((None, tr, tc), lambda i, j, pr: (pr[1], i, j)))
    out = _pallas(body, name=name, grid_spec=spec, out_shape=jax.ShapeDtypeStruct((N_CHIPS, rows, C), BF16),
                  compiler_params=_params(("parallel", "parallel")))(place, w, *deps)
    return out.reshape(N_CHIPS, 2, rows // 2, C)


def _gather_ici_plan(bufs):
    x, y, c = _me()
    j = 2 * x + y
    plan = []
    for i, buf in enumerate(bufs):
        for k, (px, py) in enumerate(_other_chips(x, y)):
            plan.append((3 * i + k, buf.at[j, c], buf.at[j, c], (px, py, c)))
    return plan


def _forward_halves(bufs, *, name):
    n = len(bufs)

    def body(*refs):
        outs = refs[n:2 * n]
        ssem, rsem = refs[2 * n:]
        x, y, c = _me()
        sib = (x, y, 1 - c)
        cps = []
        for i in range(n):
            for k, (px, py) in enumerate(_other_chips(x, y)):
                slot = outs[i].at[2 * px + py, c]
                r = _rcopy(slot, slot, ssem.at[3 * i + k], rsem.at[3 * i + k], sib)
                r.start()
                cps.append(r)
        for r in cps:
            r.wait()

    return _pallas(
        body, name=name, in_specs=[_ANY] * n, out_specs=[_ANY] * n,
        out_shape=[jax.ShapeDtypeStruct(b.shape, b.dtype) for b in bufs],
        scratch_shapes=[pltpu.SemaphoreType.DMA((3 * n,))] * 2,
        input_output_aliases={i: i for i in range(n)},
        compiler_params=pltpu.CompilerParams(has_side_effects=True),
    )(*bufs)


_HBM = pl.BlockSpec(memory_space=pltpu.HBM)
_SEM = pl.BlockSpec(memory_space=pltpu.SEMAPHORE)
_EFFECT = pltpu.SideEffectType.DATAFLOW_SIDE_EFFECTING


def _split_start(bufs, plan, n_copies, *, name):
    n = len(bufs)

    def body(*refs):
        ssem, rsem = refs[n], refs[n + 1]
        for s, src, dst, dev in plan(refs[:n]):
            _rcopy(src, dst, ssem.at[s], rsem.at[s], dev).start()
        refs[-1][...] = jnp.zeros_like(refs[-1])

    res = _pallas(
        body, name=name, in_specs=[_HBM] * n,
        out_specs=(_SEM, _SEM, *[_HBM] * n, pl.BlockSpec(memory_space=pltpu.VMEM)),
        out_shape=(pltpu.SemaphoreType.DMA((n_copies,)), pltpu.SemaphoreType.DMA((n_copies,)),
                   *[pltpu.HBM(b.shape, b.dtype) for b in bufs], jax.ShapeDtypeStruct((8, LANES), F32)),
        input_output_aliases={i: 2 + i for i in range(n)},
        compiler_params=pltpu.CompilerParams(has_side_effects=_EFFECT),
    )(*[pltpu.with_memory_space_constraint(b, pltpu.HBM) for b in bufs])
    return res[0], res[1], list(res[2:2 + n]), res[-1]


def _split_wait(ssem, rsem, bufs, after, plan, *, name):
    n = len(bufs)

    def body(*refs):
        ssem_ref, rsem_ref = refs[n], refs[n + 1]
        for s, src, dst, dev in plan(refs[:n]):
            cp = _rcopy(src, dst, ssem_ref.at[s], rsem_ref.at[s], dev)
            cp.wait_send()
            cp.wait_recv()

    return list(_pallas(
        body, name=name, in_specs=[_HBM] * n + [_SEM, _SEM, _ANY], out_specs=[_HBM] * n,
        out_shape=[pltpu.HBM(b.shape, b.dtype) for b in bufs],
        input_output_aliases={i: i for i in range(n)},
        compiler_params=pltpu.CompilerParams(has_side_effects=_EFFECT),
    )(*bufs, ssem, rsem, after))


def _forward_plan(bufs):
    x, y, c = _me()
    plan = []
    for i, buf in enumerate(bufs):
        for k, (px, py) in enumerate(_other_chips(x, y)):
            plan.append((3 * i + k, buf.at[2 * px + py, c], buf.at[2 * px + py, c], (x, y, 1 - c)))
    return plan


def _join_plan(bufs):
    x, y, c = _me()
    return [(i, buf.at[c], buf.at[c], (x, y, 1 - c)) for i, buf in enumerate(bufs)]


def _swap_plan(n):
    def plan(bufs):
        x, y, c = _me()
        return [(i, bufs[i].at[:, 1 - c], bufs[n + i], (x, y, 1 - c)) for i in range(n)]
    return plan


def _scatter_plan(n):
    def plan(bufs):
        x, y, c = _me()
        out = []
        for i in range(n):
            for k, (px, py) in enumerate(_other_chips(x, y)):
                out.append((3 * i + k, bufs[i].at[2 * px + py], bufs[n + i].at[k], (px, py, c)))
        return out
    return plan


def _allreduce_small(parts, loss11):
    n = len(parts)
    widths = [p.shape[1] for p in parts]
    total = sum(widths) + LANES

    def body(*refs):
        o_ref, mine, buf, ssem, rsem = refs[n + 1:]
        x, y, c = _me()
        me = 4 * x + 2 * y + c
        off = 0
        for r, w in zip(refs[:n], widths):
            mine[:, off:off + w] = r[...]
            off += w
        mine[:, off:] = jnp.broadcast_to(refs[n][...], (1, LANES))
        buf[me] = mine[...]
        cps = []
        for k in range(1, 8):
            peer = (x ^ (k >> 2), y ^ ((k >> 1) & 1), c ^ (k & 1))
            r = _rcopy(mine, buf.at[me], ssem.at[k - 1], rsem.at[k - 1], peer)
            r.start()
            cps.append(r)
        for k in range(1, 8):
            peer = (x ^ (k >> 2), y ^ ((k >> 1) & 1), c ^ (k & 1))
            pid = 4 * peer[0] + 2 * peer[1] + peer[2]
            _rcopy(mine, buf.at[pid], ssem.at[k - 1], rsem.at[k - 1], peer).wait_recv()
        for r in cps:
            r.wait_send()
        tot = buf[0]
        for d in range(1, 8):
            tot = tot + buf[d]
        o_ref[...] = tot

    vm = pl.BlockSpec(memory_space=pltpu.VMEM)
    return _pallas(
        body, name="allreduce_small", in_specs=[vm] * (n + 1), out_specs=vm,
        out_shape=jax.ShapeDtypeStruct((1, total), F32),
        scratch_shapes=[pltpu.VMEM((1, total), F32), pltpu.VMEM((8, 1, total), F32),
                        pltpu.SemaphoreType.DMA((7,)), pltpu.SemaphoreType.DMA((7,))],
        compiler_params=pltpu.CompilerParams(has_side_effects=True),
    )(*parts, loss11)


def _adamw_small(red, ws, ms, vs):
    n = len(ws)

    def body(*refs):
        red_ref = refs[0]
        outs = refs[1 + 3 * n:]
        off = 0
        for i in range(n):
            w = refs[1 + i].shape[1]
            g = red_ref[:, off:off + w]
            d, m, v = _adamw_vals(refs[1 + i][...], g, refs[1 + n + i][...], refs[1 + 2 * n + i][...])
            for o, val in zip(outs[4 * i:4 * i + 4], (g, d, m, v)):
                o[...] = val
            off += w

    vm = pl.BlockSpec(memory_space=pltpu.VMEM)
    res = _pallas(
        body, name="adamw_small", in_specs=[vm] * (1 + 3 * n), out_specs=[vm] * (4 * n),
        out_shape=[jax.ShapeDtypeStruct(w.shape, F32) for w in ws for _ in range(4)],
    )(red, *ws, *ms, *vs)
    return [res[4 * i:4 * i + 4] for i in range(n)]


def _rope_tables(positions, S):
    pos = positions.reshape(S, 1).astype(F32)
    half = RET_QK // 2
    inv = ROPE_THETA ** (-jnp.arange(half, dtype=F32) / half)
    ang = pos * inv
    cosr = jnp.concatenate([jnp.cos(ang), jnp.cos(ang)], axis=1)
    sinr = jnp.concatenate([-jnp.sin(ang), jnp.sin(ang)], axis=1)
    half = QK_ROPE // 2
    inv = ROPE_THETA ** (-jnp.arange(half, dtype=F32) / half)
    ang = pos * inv
    z = jnp.zeros((S, half), F32)
    c = jnp.concatenate([jnp.cos(ang), jnp.cos(ang), z, z], axis=1)
    s1 = jnp.concatenate([-jnp.sin(ang), z, z, z], axis=1)
    s2 = jnp.concatenate([z, jnp.sin(ang), z, z], axis=1)
    return cosr, sinr, (c, s1, s2)


def _cat_cols(g):
    return jnp.concatenate([g[j] for j in range(N_CHIPS)], axis=1)


def _split_cols(w):
    return jnp.stack(jnp.split(w, N_CHIPS, axis=1))


def kernel(x, positions, norm_mix_g, w_in, ret_norm_g, w_ret_o, q_a_norm_g, w_q_b, kv_a_norm_g, w_kv_b, w_mla_o, w_out, norm_mlp_g, w_up, w_down, norm_f_g, loss_target, m_norm_mix_g, m_w_in, m_ret_norm_g, m_w_ret_o, m_q_a_norm_g, m_w_q_b, m_kv_a_norm_g, m_w_kv_b, m_w_mla_o, m_w_out, m_norm_mlp_g, m_w_up, m_w_down, m_norm_f_g, v_norm_mix_g, v_w_in, v_ret_norm_g, v_w_ret_o, v_q_a_norm_g, v_w_q_b, v_kv_a_norm_g, v_w_kv_b, v_w_mla_o, v_w_out, v_norm_mlp_g, v_w_up, v_w_down, v_norm_f_g):
    S, D = x.shape[1], x.shape[2]
    RVW = w_ret_o.shape[1] * N_CHIPS
    RH = RVW // RET_V
    RQW = RH * RET_QK
    MVW = w_mla_o.shape[1] * N_CHIPS
    MH = MVW // V_HEAD
    QL, KVL = w_q_b.shape[1], w_kv_b.shape[1]
    T_RET = _tile(S, 256)
    T_ATT = _tile(S, 512)

    xs = x.reshape(S, D)
    tgt = loss_target.reshape(S, D)
    cosr, sinr, pe_tabs = _rope_tables(positions, S)
    lgam = jnp.log(1.0 - 2.0 ** (-5.0 - jnp.arange(RH, dtype=F32)))
    lgam = jnp.broadcast_to(lgam[:, None, None], (RH, 8, LANES))

    big = ("w_in", "w_ret_o", "w_q_b", "w_kv_b", "w_mla_o", "w_out", "w_up", "w_down")
    w_sh = dict(w_in=w_in[0].T, w_ret_o=w_ret_o[0], w_q_b=w_q_b[0], w_kv_b=w_kv_b[0], w_mla_o=w_mla_o[0],
                w_out=w_out[0], w_up=w_up[0], w_down=w_down[0])
    m_sh = dict(w_in=m_w_in[0].T, w_ret_o=m_w_ret_o[0], w_q_b=m_w_q_b[0], w_kv_b=m_w_kv_b[0],
                w_mla_o=m_w_mla_o[0], w_out=m_w_out[0], w_up=m_w_up[0], w_down=m_w_down[0])
    v_sh = dict(w_in=v_w_in[0].T, w_ret_o=v_w_ret_o[0], w_q_b=v_w_q_b[0], w_kv_b=v_w_kv_b[0],
                w_mla_o=v_w_mla_o[0], w_out=v_w_out[0], w_up=v_w_up[0], w_down=v_w_down[0])
    col_sharded = ("w_q_b", "w_kv_b", "w_up")
    c_sh = w_in.shape[2]
    c_pad = -(-c_sh // 64) * 64
    place = jnp.stack([lax.axis_index("c"), 2 * lax.axis_index("x") + lax.axis_index("y")]).astype(jnp.int32)

    def whole(k, g):
        g = g.reshape(N_CHIPS, w_sh[k].shape[0], w_sh[k].shape[1])
        if k == "w_up":
            return g
        return _cat_cols(g) if k in col_sharded else g.reshape(-1, g.shape[2])

    first = ("w_in", "w_q_b", "w_kv_b")
    later = ("w_ret_o", "w_mla_o", "w_out", "w_up", "w_down")
    first_bufs = [_cast_into_slot(w_sh[k], place, name="cast_" + k, rows=c_pad if k == "w_in" else None)
                  for k in first]
    first_ssem, first_rsem, first_bufs, first_token = _split_start(
        first_bufs, _gather_ici_plan, 3 * len(first), name="gather_first_start")
    later_bufs = [_cast_into_slot(w_sh[k], place, name="cast_" + k, deps=(first_token,)) for k in later[:-1]]
    first_bufs = _split_wait(first_ssem, first_rsem, first_bufs, later_bufs[-1], _gather_ici_plan,
                             name="gather_first_wait")
    got = _forward_halves(first_bufs, name="gather_first_forward")
    full = {k: whole(k, g) for k, g in zip(first[1:], got[1:])}
    later_bufs.append(_cast_into_slot(w_sh[later[-1]], place, name="cast_" + later[-1], deps=(got[0],)))
    later_ssem, later_rsem, later_bufs, later_token = _split_start(
        later_bufs, _gather_ici_plan, 3 * len(later), name="gather_later_start")

    o_rq, o_rk, o_rv, o_rg = 0, RQW, 2 * RQW, 2 * RQW + RVW
    o_cq = 2 * RQW + 2 * RVW
    o_ckv, o_kpe = o_cq + QL, o_cq + QL + KVL
    o_gr = o_kpe + QK_ROPE
    o_gm = o_gr + D
    n_ret = RH * RET_HEAD_COLS
    off_gate, off_cq, off_ckv = n_ret, n_ret + 2 * D, n_ret + 2 * D + QL
    gate_tile = _tile(D, 1024)
    n_a = off_ckv + KVL
    runs = []
    for h in range(RH):
        base = h * RET_HEAD_COLS
        runs += [(o_rq + h * RET_QK, RET_QK, base), (o_rk + h * RET_QK, RET_QK, base + RET_QK),
                 (o_rv + h * RET_V, RET_V, base + 2 * RET_QK), (o_rg + h * RET_V, RET_V, base + 2 * RET_QK + RET_V)]
    for t in range(D // gate_tile):
        runs += [(o_gr + t * gate_tile, gate_tile, off_gate + 2 * t * gate_tile),
                 (o_gm + t * gate_tile, gate_tile, off_gate + (2 * t + 1) * gate_tile)]
    runs += [(o_cq, QL, off_cq), (o_ckv, KVL, off_ckv), (o_kpe, QK_ROPE, n_a)]

    def take(parts, start, width):
        out, lo = [], 0
        for p in parts:
            hi = lo + p.shape[0]
            a, b = max(start, lo), min(start + width, hi)
            if a < b:
                out.append(p[a - lo:b - lo])
            lo = hi
        return out

    wi = [got[0].reshape(N_CHIPS, c_pad, D)[jj, :c_sh] for jj in range(N_CHIPS)]
    here = sorted(runs, key=lambda r: r[2])
    wa = jnp.concatenate([p for s0, w, _ in here[:-1] for p in take(wi, s0, w)], axis=0)
    wkpe = jnp.concatenate(take(wi, o_kpe, QK_ROPE) + [jnp.zeros((LANES - QK_ROPE, D), BF16)], axis=0)
    wq = jnp.pad(full["w_q_b"].reshape(QL, MH, QK_NOPE + QK_ROPE),
                 ((0, 0), (0, 0), (0, LANES - QK_ROPE))).reshape(QL, MH * 2 * LANES)
    wkv = full["w_kv_b"]

    u, rstd0 = _rmsnorm_fwd(xs, norm_mix_g, name="norm_mix")
    proj = _mm(u, wa, mode="nt", outs=[F32], name="in_proj", deps=(later_token,))
    kpe = _mm(u, wkpe, mode="nt", outs=[F32], name="kpe_proj")
    ry, gated, states = _ret_fwd(proj, cosr, sinr, lgam, ret_norm_g, RH, T=T_RET)
    cqn, ckvn, rstd_q, rstd_kv = _norm_pair_fwd(proj, q_a_norm_g, kv_a_norm_g, off_cq)
    qf, kf, vb = _qkv_proj(cqn, ckvn, wq, wkv, kpe, pe_tabs, MH)
    first_half = _attn_fwd(qf, kf, vb, MH, T=T_ATT, heads=(0, MH // 2), name="attn_fwd_a")
    later_bufs = _split_wait(later_ssem, later_rsem, later_bufs, first_half[0], _gather_ici_plan,
                             name="gather_later_wait")
    fwd_ssem, fwd_rsem, later_bufs, fwd_token = _split_start(
        later_bufs, _forward_plan, 3 * len(later), name="gather_later_forward_start")
    my, my_b, lse2 = _attn_fwd(qf, kf, vb, MH, T=T_ATT, heads=(MH // 2, MH), name="attn_fwd_b",
                               prev=first_half, deps=(fwd_token,))
    later_bufs = _split_wait(fwd_ssem, fwd_rsem, later_bufs, my, _forward_plan, name="gather_later_forward_wait")
    full.update({k: whole(k, g) for k, g in zip(later, later_bufs)})
    y_ret = _mm(gated, full["w_ret_o"], mode="nn", outs=[BF16], name="ret_o")
    y_mla, merged = _mm(my_b, full["w_mla_o"], mode="nn", outs=[BF16, BF16], name="mla_o",
                        epi=lambda acc, gr, gm, yr: (acc, _sigmoid(gr) * yr + _sigmoid(gm) * acc),
                        extras=((proj, off_gate, 2), (proj, off_gate + gate_tile, 2), y_ret), tn=gate_tile)
    h1 = _mm(merged, full["w_out"], mode="nn", outs=[F32], name="out_proj",
             epi=lambda acc, r: (acc + r,), extras=(xs,))
    n1, rstd1 = _rmsnorm_fwd(h1, norm_mlp_g, name="norm_mlp")

    def up_epi(acc):
        r = jnp.maximum(acc, 0.0)
        return acc, r * r

    z, act = _mm(n1, full["w_up"], mode="nn", outs=[F32, BF16], name="up_proj", epi=up_epi)
    h2 = _mm(act, full["w_down"], mode="nn", outs=[F32], name="down_proj",
             epi=lambda acc, r: (acc + r,), extras=(h1,))
    loss11, dh2, dh2_b, g_norm_f = _final_loss(h2, norm_f_g.reshape(1, D), tgt)

    dz = _mm(dh2_b, full["w_down"], mode="nt", outs=[BF16], name="down_bwd_x",
             epi=lambda acc, zz: (acc * (2.0 * jnp.maximum(zz, 0.0)),), extras=(z,))
    g_w_down = _mm(act, dh2_b, mode="tn", outs=[BF16], name="down_bwd_w")
    dn1 = _mm(dz, full["w_up"], mode="nt", outs=[F32], name="up_bwd_x")
    g_w_up = _mm(n1, dz, mode="tn", outs=[BF16], name="up_bwd_w", out_shards=True)

    def scatter_begin(tag, sums):
        lands = [lax.empty((3,) + s.shape[1:], s.dtype) for s in sums]
        return _split_start(sums + lands, _scatter_plan(len(sums)), 3 * len(sums), name="scatter_" + tag + "_start")

    def swap_begin(tag, grads):
        views = [g if g.ndim == 3 else g.reshape(N_CHIPS, g.shape[0] // N_CHIPS, g.shape[1]) for g in grads]
        views = [v.reshape(N_CHIPS, 2, v.shape[1] // 2, v.shape[2]) for v in views]
        lands = [lax.empty((N_CHIPS,) + v.shape[2:], v.dtype) for v in views]
        return _split_start(views + lands, _swap_plan(len(views)), len(views), name="swap_" + tag + "_start")

    def swap_end(tag, names, handle, after):
        n = len(names)
        bufs = _split_wait(handle[0], handle[1], handle[2], after, _swap_plan(n), name="swap_" + tag + "_wait")
        pcs = [b.reshape(N_CHIPS, 2 * b.shape[2], b.shape[3]) for b in bufs[:n]]
        sums = [_sum_pair(p, t, place, name="sum_pair_" + k) for k, p, t in zip(names, pcs, bufs[n:])]
        return pcs, bufs[n:], sums

    g1 = ("w_up", "w_down")
    swap1 = swap_begin("g1", (g_w_up, g_w_down))
    dh1, g_norm_mlp, dh1_b = _rmsnorm_bwd(dn1, h1, rstd1, norm_mlp_g, name="norm_mlp_bwd", res=dh2,
                                          deps=(swap1[3],), bf16_copy=1)
    def merge_bwd_epi(dm, gr, gm, yr, ym):
        sr, sm = _sigmoid(gr), _sigmoid(gm)
        return dm * sr, dm * sm, jnp.concatenate([dm * yr * sr * (1.0 - sr), dm * ym * sm * (1.0 - sm)], axis=1)

    dy_ret, dy_mla, dproj = _mm(
        dh1_b, full["w_out"], mode="nt", outs=[BF16, BF16], name="out_bwd_x", epi=merge_bwd_epi,
        extras=((proj, off_gate, 2), (proj, off_gate + gate_tile, 2), y_ret, y_mla), tm=512, tn=gate_tile,
        more_outs=lambda tm, tn: [(jax.ShapeDtypeStruct(proj.shape, BF16),
                                   pl.BlockSpec((tm, 2 * tn), lambda i, j, k: (i, off_gate // (2 * tn) + j)))])
    pcs1, theirs1, sums1 = swap_end("g1", g1, swap1, dy_ret)
    ssem1, rsem1, bufs1, token1 = scatter_begin("g1", sums1)
    g_w_out = _mm(merged, dh1_b, mode="tn", outs=[BF16], name="out_bwd_w", deps=(token1,))
    dgated = _mm(dy_ret, full["w_ret_o"], mode="nt", outs=[F32], name="ret_o_bwd_x")
    g_w_ret_o = _mm(gated, dy_ret, mode="tn", outs=[BF16], name="ret_o_bwd_w")
    dproj, g_ret_norm = _ret_bwd(proj, cosr, sinr, lgam, ret_norm_g, ry, dgated, states, dproj, RH, T=T_RET)
    def delta_epi(acc, o):
        rows = acc.shape[0]
        return acc, [jnp.broadcast_to(jnp.sum(acc[:, lo:lo + V_HEAD] * o[:, lo:lo + V_HEAD], axis=-1, keepdims=True),
                                      (rows, LANES)) for lo in range(0, acc.shape[1], V_HEAD)]

    dob, delta = _mm(dy_mla, full["w_mla_o"], mode="nt", outs=[BF16], name="mla_o_bwd_x", epi=delta_epi,
                     extras=(my,), more_outs=lambda tm, tn: [
                         (jax.ShapeDtypeStruct((MH, S, LANES), F32),
                          pl.BlockSpec((tn // V_HEAD, tm, LANES), lambda i, j, k: (j, i, 0)))])
    g_w_mla_o = _mm(my_b, dy_mla, mode="tn", outs=[BF16], name="mla_o_bwd_w")
    g2 = ("w_out", "w_ret_o", "w_mla_o")
    swap2 = swap_begin("g2", (g_w_out, g_w_ret_o, g_w_mla_o))
    dq_all, dkv_all, dkpe_h = _attn_bwd(qf, kf, vb, dob, lse2, delta, pe_tabs, MH, T=T_ATT, deps=(swap2[3],))
    pcs2, theirs2, sums2 = swap_end("g2", g2, swap2, dkv_all)
    ssem2, rsem2, bufs2, token2 = scatter_begin("g2", sums2)
    dkpe = _kpe_sum(dkpe_h, pe_tabs, MH)
    dcqn = _mm(dq_all, wq, mode="nt", outs=[F32], name="q_bwd_x", deps=(token2,))
    g_wq = _mm(cqn, dq_all, mode="tn", outs=[BF16], name="q_bwd_w")
    dckvn = _mm(dkv_all, wkv, mode="nt", outs=[F32], name="kv_bwd_x")
    g_wkv = _mm(ckvn, dkv_all, mode="tn", outs=[BF16], name="kv_bwd_w")
    dproj, g_q_a, g_kv_a = _norm_pair_bwd(dcqn, dckvn, proj, rstd_q, rstd_kv, q_a_norm_g, kv_a_norm_g, dproj, off_cq)
    g_wa = _mm(dproj, u, mode="tn", outs=[BF16], name="in_bwd_w")
    g_wkpe = _mm(dkpe, u, mode="tn", outs=[BF16], name="kpe_bwd_w")

    there = sorted(runs)
    g_parts = [g_wa, g_wkpe]
    g_w_in = jnp.stack([jnp.concatenate(
        [p for s0, w, d0 in there for a, b in [(max(s0, jj * c_sh), min(s0 + w, (jj + 1) * c_sh))] if a < b
         for p in take(g_parts, d0 + a - s0, b - a)] + [jnp.zeros((c_pad - c_sh, D), BF16)], axis=0)
        for jj in range(N_CHIPS)])
    gq = g_wq.reshape(QL, MH, 2 * LANES)[:, :, :QK_NOPE + QK_ROPE].reshape(QL, MH * (QK_NOPE + QK_ROPE))
    g3 = ("w_in", "w_q_b", "w_kv_b")
    def chip_sums(names, pcs, theirs, recv):
        return [_sum_chips(p, t, r, place, name="sum_chips_" + k) for k, p, t, r in zip(names, pcs, theirs, recv)]

    swap3 = swap_begin("g3", (g_w_in, _split_cols(gq), _split_cols(g_wkv)))
    bufs1 = _split_wait(ssem1, rsem1, bufs1, swap3[3], _scatter_plan(len(g1)), name="scatter_g1_wait")
    halves1 = chip_sums(g1, pcs1, theirs1, bufs1[len(g1):])
    pcs3, theirs3, sums3 = swap_end("g3", g3, swap3, halves1[-1])
    ssem3, rsem3, bufs3, token3 = scatter_begin("g3", sums3)
    bufs2 = _split_wait(ssem2, rsem2, bufs2, token3, _scatter_plan(len(g2)), name="scatter_g2_wait")
    halves12 = halves1 + chip_sums(g2, pcs2, theirs2, bufs2[len(g2):])
    jssem, jrsem, halves12, join_token = _split_start(halves12, _join_plan, len(halves12), name="join_g12_start")
    du = _mm(dproj, wa, mode="nn", outs=[F32], name="in_bwd_x", tk=2816, tail=(dkpe, wkpe), deps=(join_token,))
    dx, g_norm_mix = _rmsnorm_bwd(du, xs, rstd0, norm_mix_g, name="norm_mix_bwd", res=dh1)

    bufs3 = _split_wait(ssem3, rsem3, bufs3, dx, _scatter_plan(len(g3)), name="scatter_g3_wait")
    halves12 = _split_wait(jssem, jrsem, halves12, dx, _join_plan, name="join_g12_wait")
    j3ssem, j3rsem, halves3, join3_token = _split_start(
        chip_sums(g3, pcs3, theirs3, bufs3[len(g3):]), _join_plan, len(g3), name="join_g3_start")

    small = ("norm_mix_g", "ret_norm_g", "q_a_norm_g", "kv_a_norm_g", "norm_mlp_g", "norm_f_g")
    g_small = [g_norm_mix, g_ret_norm, g_q_a, g_kv_a, g_norm_mlp, g_norm_f]
    red = _allreduce_small(g_small, loss11)
    loss = red[0, red.shape[1] - 1]
    w_small = [norm_mix_g, ret_norm_g, q_a_norm_g, kv_a_norm_g, norm_mlp_g, norm_f_g]
    m_small = [m_norm_mix_g, m_ret_norm_g, m_q_a_norm_g, m_kv_a_norm_g, m_norm_mlp_g, m_norm_f_g]
    v_small = [v_norm_mix_g, v_ret_norm_g, v_q_a_norm_g, v_kv_a_norm_g, v_norm_mlp_g, v_norm_f_g]
    row = lambda a: a.reshape(1, -1)
    upd = _adamw_small(red, [row(a) for a in w_small], [row(a) for a in m_small], [row(a) for a in v_small])
    out_g, out_d, out_m, out_v = {}, {}, {}, {}
    for k, wv, (g_, d_, m_, v_) in zip(small, w_small, upd):
        out_g[k], out_d[k], out_m[k], out_v[k] = [a.reshape(wv.shape) for a in (g_, d_, m_, v_)]

    def adamw_shard(k, joined, deps=()):
        g = joined.reshape(2 * joined.shape[1], joined.shape[2])
        res = _rows_call(lambda w, g, m, v: (g,) + _adamw_vals(w, g, m, v),
                         [w_sh[k], g, m_sh[k], v_sh[k]], [F32] * 4, name="adamw_" + k, deps=deps)
        if k == "w_in":
            res = [r.T for r in res]
        out_g[k], out_d[k], out_m[k], out_v[k] = [r[None] for r in res]
        return res[0]

    for k, joined in zip(g1 + g2, halves12):
        last = adamw_shard(k, joined, deps=(join3_token,))
    halves3 = _split_wait(j3ssem, j3rsem, halves3, last, _join_plan, name="join_g3_wait")
    for k, joined in zip(g3, halves3):
        adamw_shard(k, joined)

    order = ("norm_mix_g", "w_in", "ret_norm_g", "w_ret_o", "q_a_norm_g", "w_q_b", "kv_a_norm_g", "w_kv_b",
             "w_mla_o", "w_out", "norm_mlp_g", "w_up", "w_down", "norm_f_g")
    return (loss, dx.reshape(1, S, D), *[out_g[k] for k in order], *[out_d[k] for k in order],
            *[out_m[k] for k in order], *[out_v[k] for k in order])
```

```python
import math

import jax
import jax.numpy as jnp
from jax import lax
from jax.experimental import pallas as pl
from jax.experimental.pallas import tpu as pltpu

F32 = jnp.float32
BF16 = jnp.bfloat16

EPS = 1e-6
ROPE_THETA = 10000.0
CHUNK = 64
RET_QK = 128
RET_V = 256
RET_HEAD_COLS = 2 * RET_QK + 2 * RET_V
QK_NOPE = 128
QK_ROPE = 64
V_HEAD = 128
LANES = 128
LOG2E = math.log2(math.e)

ADAM_LR = 0.001
ADAM_B1 = 0.9
ADAM_B2 = 0.999
ADAM_EPS = 1e-08
ADAM_WD = 0.01
ADAM_STEP = 10

N_CHIPS = 4
VMEM_LIMIT = 56 * 1024 * 1024
MESH = pl.DeviceIdType.MESH
NEG = -1e30


def _pallas(body, **kw):
    return pl.pallas_call(body, **kw)


def _params(sem=None):
    return pltpu.CompilerParams(dimension_semantics=sem, vmem_limit_bytes=VMEM_LIMIT)


def _tile(n, want):
    t = min(n, want)
    while n % t:
        t //= 2
    return t


_ANY = pl.BlockSpec(memory_space=pl.ANY)
TN_BF16_TK = 4096


def _mm(a, b, *, mode, outs, name, epi=None, extras=(), deps=(), out_shards=False, more_outs=None, tail=None,
        tm=1024, tn=1024, tk=2048):
    shards = b.shape[0] if b.ndim == 3 else 1
    brows, bcols = b.shape[-2], b.shape[-1] * shards
    if mode == "nn":
        (M, K), N = a.shape, bcols
    elif mode == "nt":
        (M, K), N = a.shape, brows
    else:
        (K, M), N = a.shape, bcols
    if mode == "tn" and a.dtype == BF16 and b.dtype == BF16:
        tk = max(tk, TN_BF16_TK)
    tm = _tile(M, tm)
    tn = _tile(N // (shards if mode == "nn" else 1) // (N_CHIPS if out_shards else 1), tn)
    tk = _tile(K // (shards if mode == "nt" else 1), tk)
    nk = K // tk
    if mode == "nn":
        a_spec = pl.BlockSpec((tm, tk), lambda i, j, k: (i, k))
        dims = (((1,), (0,)), ((), ()))
        if shards > 1:
            per = N // shards // tn
            b_spec = pl.BlockSpec((None, tk, tn), lambda i, j, k: (j // per, k, j % per))
        else:
            b_spec = pl.BlockSpec((tk, tn), lambda i, j, k: (k, j))
    elif mode == "nt":
        a_spec = pl.BlockSpec((tm, tk), lambda i, j, k: (i, k))
        dims = (((1,), (1,)), ((), ()))
        if shards > 1:
            per = K // shards // tk
            b_spec = pl.BlockSpec((None, tn, tk), lambda i, j, k: (k // per, j, k % per))
        else:
            b_spec = pl.BlockSpec((tn, tk), lambda i, j, k: (j, k))
    else:
        assert shards == 1
        a_spec = pl.BlockSpec((tk, tm), lambda i, j, k: (k, i))
        b_spec = pl.BlockSpec((tk, tn), lambda i, j, k: (k, j))
        dims = (((0,), (0,)), ((), ()))
    if out_shards:
        assert not extras
        oper = N // N_CHIPS // tn
        o_spec = pl.BlockSpec((None, tm, tn), lambda i, j, k: (j // oper, i, j % oper))
        o_shape = (N_CHIPS, M, N // N_CHIPS)
    else:
        o_spec = pl.BlockSpec((tm, tn), lambda i, j, k: (i, j))
        o_shape = (M, N)
    more = [] if more_outs is None else more_outs(tm, tn)
    ex_arrays = [e[0] if isinstance(e, tuple) else e for e in extras]
    ex_specs = [pl.BlockSpec((tm, tn), lambda i, j, k, off=e[1] // tn, st=e[2]: (i, off + st * j))
                if isinstance(e, tuple) else o_spec for e in extras]
    n_ex, n_out, n_dep = len(extras), len(outs) + len(more), len(deps)
    if epi is None:
        epi = lambda acc: (acc,)
    tails, tail_specs = [], []
    if tail is not None:
        assert mode == "nn"
        tails = list(tail)
        k2 = tail[0].shape[1]
        tail_specs = [pl.BlockSpec((tm, k2), lambda i, j, k: (i, 0)), pl.BlockSpec((k2, tn), lambda i, j, k: (0, j))]
    n_tail = len(tails)

    def body(*refs):
        a_ref, b_ref = refs[0], refs[1]
        ex_refs = refs[2:2 + n_ex]
        t_refs = refs[2 + n_ex:2 + n_ex + n_tail]
        first_out = 2 + n_ex + n_tail + n_dep
        o_refs = refs[first_out:first_out + n_out]
        part = lax.dot_general(a_ref[...].astype(BF16), b_ref[...].astype(BF16), dims,
                               preferred_element_type=F32)

        def finish(acc):
            if n_tail:
                acc = acc + lax.dot_general(t_refs[0][...].astype(BF16), t_refs[1][...].astype(BF16), dims,
                                            preferred_element_type=F32)
            vals = epi(acc, *[r[...] for r in ex_refs])
            for r, v in zip(o_refs, vals):
                if isinstance(v, (list, tuple)):
                    for lead, piece in enumerate(v):
                        r[lead] = piece.astype(r.dtype)
                else:
                    r[...] = v.astype(r.dtype)

        if nk == 1:
            finish(part)
        else:
            acc_ref = refs[-1]
            k = pl.program_id(2)

            @pl.when(k == 0)
            def _():
                acc_ref[...] = part

            @pl.when(k > 0)
            def _():
                acc_ref[...] += part

            @pl.when(k == nk - 1)
            def _():
                finish(acc_ref[...])

    res = _pallas(
        body, name=name, grid=(M // tm, N // tn, nk),
        in_specs=[a_spec, b_spec] + ex_specs + tail_specs + [_ANY] * n_dep,
        out_specs=[o_spec] * len(outs) + [spec for _, spec in more],
        out_shape=[jax.ShapeDtypeStruct(o_shape, d) for d in outs] + [shape for shape, _ in more],
        scratch_shapes=[pltpu.VMEM((tm, tn), F32)] if nk > 1 else [],
        compiler_params=_params(("parallel", "parallel", "arbitrary")),
    )(a, b, *ex_arrays, *tails, *deps)
    return res[0] if n_out == 1 else res


def _rmsnorm_fwd(x, g, *, name, tr=512):
    S, W = x.shape
    tr = _tile(S, tr)

    def body(x_ref, g_ref, y_ref, r_ref):
        xv = x_ref[...]
        rstd = lax.rsqrt(jnp.mean(xv * xv, axis=-1, keepdims=True) + EPS)
        y_ref[...] = (xv * rstd * g_ref[...]).astype(BF16)
        r_ref[...] = rstd

    return _pallas(
        body, name=name, grid=(S // tr,),
        in_specs=[pl.BlockSpec((tr, W), lambda i: (i, 0)), pl.BlockSpec((1, W), lambda i: (0, 0))],
        out_specs=[pl.BlockSpec((tr, W), lambda i: (i, 0)), pl.BlockSpec((tr, 1), lambda i: (i, 0))],
        out_shape=[jax.ShapeDtypeStruct((S, W), BF16), jax.ShapeDtypeStruct((S, 1), F32)],
        compiler_params=_params(("parallel",)),
    )(x, g)


def _rmsnorm_bwd(dy, x, rstd, g, *, name, res=None, deps=(), bf16_copy=0, tr=512):
    S, W = x.shape
    tr = _tile(S, tr)
    has_res = res is not None

    def body(*refs):
        dy_ref, x_ref, r_ref, g_ref = refs[:4]
        dx_ref, dg_ref = refs[-2 - bf16_copy], refs[-1 - bf16_copy]
        rstd_v = r_ref[...]
        xhat = x_ref[...] * rstd_v
        dyv = dy_ref[...].astype(F32)
        dyg = dyv * g_ref[...]
        dx = rstd_v * (dyg - xhat * jnp.mean(dyg * xhat, axis=-1, keepdims=True))
        if has_res:
            dx = dx + refs[4][...]
        dx_ref[...] = dx.astype(dx_ref.dtype)
        if bf16_copy:
            refs[-1][...] = dx.astype(BF16)
        part = jnp.sum(dyv * xhat, axis=0, keepdims=True)

        @pl.when(pl.program_id(0) == 0)
        def _():
            dg_ref[...] = part

        @pl.when(pl.program_id(0) > 0)
        def _():
            dg_ref[...] += part

    row = pl.BlockSpec((tr, W), lambda i: (i, 0))
    ins = [dy, x, rstd, g] + ([res] if has_res else [])
    in_specs = [row, row, pl.BlockSpec((tr, 1), lambda i: (i, 0)),
                pl.BlockSpec((1, W), lambda i: (0, 0))] + ([row] if has_res else [])
    ins += list(deps)
    in_specs += [_ANY] * len(deps)
    return _pallas(
        body, name=name, grid=(S // tr,), in_specs=in_specs,
        out_specs=[row, pl.BlockSpec((1, W), lambda i: (0, 0))] + [row] * bf16_copy,
        out_shape=[jax.ShapeDtypeStruct((S, W), F32), jax.ShapeDtypeStruct((1, W), F32)]
        + [jax.ShapeDtypeStruct((S, W), BF16)] * bf16_copy,
        compiler_params=_params(("arbitrary",)),
    )(*ins)


def _norm_pair_fwd(proj, g_a, g_b, off, *, tr=512):
    S = proj.shape[0]
    wa_, wb_ = g_a.shape[1], g_b.shape[1]
    W = wa_ + wb_
    tr = _tile(S, tr)

    def body(x_ref, ga_ref, gb_ref, ya_ref, yb_ref, ra_ref, rb_ref):
        for lo, hi, g_ref, y_ref, r_ref in ((0, wa_, ga_ref, ya_ref, ra_ref), (wa_, W, gb_ref, yb_ref, rb_ref)):
            xv = x_ref[:, lo:hi]
            rstd = lax.rsqrt(jnp.mean(xv * xv, axis=-1, keepdims=True) + EPS)
            y_ref[...] = (xv * rstd * g_ref[...]).astype(BF16)
            r_ref[...] = rstd

    one = pl.BlockSpec((tr, 1), lambda i: (i, 0))
    return _pallas(
        body, name="norm_qkv", grid=(S // tr,),
        in_specs=[pl.BlockSpec((tr, W), lambda i: (i, off // W)), pl.BlockSpec((1, wa_), lambda i: (0, 0)),
                  pl.BlockSpec((1, wb_), lambda i: (0, 0))],
        out_specs=[pl.BlockSpec((tr, wa_), lambda i: (i, 0)), pl.BlockSpec((tr, wb_), lambda i: (i, 0)), one, one],
        out_shape=[jax.ShapeDtypeStruct((S, wa_), BF16), jax.ShapeDtypeStruct((S, wb_), BF16),
                   jax.ShapeDtypeStruct((S, 1), F32), jax.ShapeDtypeStruct((S, 1), F32)],
        compiler_params=_params(("parallel",)),
    )(proj, g_a, g_b)


def _norm_pair_bwd(dy_a, dy_b, proj, r_a, r_b, g_a, g_b, dproj, off, *, tr=512):
    S = proj.shape[0]
    wa_, wb_ = g_a.shape[1], g_b.shape[1]
    W = wa_ + wb_
    tr = _tile(S, tr)

    def body(dya_ref, dyb_ref, x_ref, ra_ref, rb_ref, ga_ref, gb_ref, _, dx_ref, dga_ref, dgb_ref):
        first = pl.program_id(0) == 0
        for lo, hi, dy_ref, r_ref, g_ref, dg_ref in ((0, wa_, dya_ref, ra_ref, ga_ref, dga_ref),
                                                    (wa_, W, dyb_ref, rb_ref, gb_ref, dgb_ref)):
            rstd = r_ref[...]
            xhat = x_ref[:, lo:hi] * rstd
            dyv = dy_ref[...]
            dyg = dyv * g_ref[...]
            dx_ref[:, lo:hi] = (rstd * (dyg - xhat * jnp.mean(dyg * xhat, axis=-1, keepdims=True))).astype(dx_ref.dtype)
            part = jnp.sum(dyv * xhat, axis=0, keepdims=True)

            @pl.when(first)
            def _():
                dg_ref[...] = part

            @pl.when(jnp.logical_not(first))
            def _():
                dg_ref[...] += part

    one = pl.BlockSpec((tr, 1), lambda i: (i, 0))
    cols = pl.BlockSpec((tr, W), lambda i: (i, off // W))
    va, vb = pl.BlockSpec((1, wa_), lambda i: (0, 0)), pl.BlockSpec((1, wb_), lambda i: (0, 0))
    return _pallas(
        body, name="norm_qkv_bwd", grid=(S // tr,),
        in_specs=[pl.BlockSpec((tr, wa_), lambda i: (i, 0)), pl.BlockSpec((tr, wb_), lambda i: (i, 0)), cols,
                  one, one, va, vb, _ANY],
        out_specs=[cols, va, vb],
        out_shape=[jax.ShapeDtypeStruct(dproj.shape, dproj.dtype), jax.ShapeDtypeStruct((1, wa_), F32),
                   jax.ShapeDtypeStruct((1, wb_), F32)],
        input_output_aliases={7: 0},
        compiler_params=_params(("arbitrary",)),
    )(dy_a, dy_b, proj, r_a, r_b, g_a, g_b, dproj)


def _final_loss(h2, g, target, *, tr=512):
    S, D = h2.shape
    tr = _tile(S, tr)

    def body(h_ref, g_ref, t_ref, loss_ref, dh_ref, dhb_ref, dg_ref):
        hv = h_ref[...]
        rstd = lax.rsqrt(jnp.mean(hv * hv, axis=-1, keepdims=True) + EPS)
        xhat = hv * rstd
        e = xhat * g_ref[...] - t_ref[...]
        lpart = (0.5 / D) * jnp.sum(jnp.sum(e * e, axis=-1, keepdims=True), axis=0, keepdims=True)
        dy = e * (1.0 / D)
        dyg = dy * g_ref[...]
        dh = rstd * (dyg - xhat * jnp.mean(dyg * xhat, axis=-1, keepdims=True))
        dh_ref[...] = dh
        dhb_ref[...] = dh.astype(BF16)
        gpart = jnp.sum(dy * xhat, axis=0, keepdims=True)

        @pl.when(pl.program_id(0) == 0)
        def _():
            loss_ref[...] = lpart
            dg_ref[...] = gpart

        @pl.when(pl.program_id(0) > 0)
        def _():
            loss_ref[...] += lpart
            dg_ref[...] += gpart

    row = pl.BlockSpec((tr, D), lambda i: (i, 0))
    vec = pl.BlockSpec((1, D), lambda i: (0, 0))
    return _pallas(
        body, name="final_loss", grid=(S // tr,), in_specs=[row, vec, row],
        out_specs=[pl.BlockSpec((1, 1), lambda i: (0, 0)), row, row, vec],
        out_shape=[jax.ShapeDtypeStruct((1, 1), F32), jax.ShapeDtypeStruct((S, D), F32),
                   jax.ShapeDtypeStruct((S, D), BF16), jax.ShapeDtypeStruct((1, D), F32)],
        compiler_params=_params(("arbitrary",)),
    )(h2, g, target)


def _sigmoid(v):
    return 1.0 / (1.0 + jnp.exp(-v))


def _rope128(t, cos_full, sin_signed):
    return t * cos_full + pltpu.roll(t, RET_QK // 2, 1) * sin_signed


def _rope128_t(d, cos_full, sin_signed):
    return d * cos_full + pltpu.roll(d * sin_signed, RET_QK // 2, 1)


def _ret_consts(lg, T):
    pos = lax.broadcasted_iota(jnp.int32, (T, 1), 0).astype(F32)
    qd = jnp.exp(lg * (pos + 1.0))
    kd = jnp.exp(lg * (T - 1.0 - pos))
    n = lax.broadcasted_iota(jnp.int32, (T, T), 0)
    m = lax.broadcasted_iota(jnp.int32, (T, T), 1)
    vis = (m // CHUNK) <= (n // CHUNK)
    dist = jnp.abs(n - m).astype(F32)
    decay = jnp.where(vis, jnp.exp(lg * dist), 0.0)
    cdec = jnp.exp(lg * float(T))
    return qd, kd, decay, cdec


def _dot(a, b, dims):
    return lax.dot_general(a.astype(BF16), b.astype(BF16), (dims, ((), ())), preferred_element_type=F32)


NN = ((1,), (0,))
NT = ((1,), (1,))
TN = ((0,), (0,))
_RQ = slice(0, RET_QK)
_RK = slice(RET_QK, 2 * RET_QK)
_RV = slice(2 * RET_QK, 2 * RET_QK + RET_V)
_RG = slice(2 * RET_QK + RET_V, RET_HEAD_COLS)


RET_GROUP = 8


def _head_cols(h, part):
    return slice(h * RET_HEAD_COLS + part.start, h * RET_HEAD_COLS + part.stop)


def _ret_fwd(proj, cosr, sinr, lgam, gain, RH, *, T):
    S = proj.shape[0]
    nb = S // T
    G = _tile(RH, RET_GROUP)
    heads = range(G)
    scale = RET_QK ** -0.5

    def body(p_ref, cos_ref, sin_ref, lg_ref, gain_ref, ry_ref, gated_ref, st_ref, state):
        b = pl.program_id(1)

        @pl.when(b == 0)
        def _():
            state[...] = jnp.zeros_like(state)

        consts = [_ret_consts(lg_ref[h, 0:1, 0:1], T) for h in heads]
        cosv, sinv = cos_ref[...], sin_ref[...]
        q = [_rope128(p_ref[:, _head_cols(h, _RQ)], cosv, sinv) for h in heads]
        k = [_rope128(p_ref[:, _head_cols(h, _RK)], cosv, sinv) * scale for h in heads]
        v = [p_ref[:, _head_cols(h, _RV)] for h in heads]
        sprev = [state[h] for h in heads]
        for h in heads:
            st_ref[h] = sprev[h]
        a = [_dot(q[h], k[h], NT) for h in heads]
        qs = [_dot(q[h] * consts[h][0], sprev[h], NN) for h in heads]
        kv = [_dot(k[h] * consts[h][1], v[h], TN) for h in heads]
        o = [_dot(a[h] * consts[h][2], v[h], NN) + qs[h] for h in heads]
        for h in heads:
            state[h] = sprev[h] * consts[h][3] + kv[h]
            vals = slice(h * RET_V, (h + 1) * RET_V)
            ry_ref[:, vals] = o[h]
            mu = jnp.mean(o[h], axis=-1, keepdims=True)
            oc = o[h] - mu
            var = jnp.mean(oc * oc, axis=-1, keepdims=True)
            t = oc * lax.rsqrt(var + EPS) * gain_ref[:, vals]
            gv = p_ref[:, _head_cols(h, _RG)]
            gated_ref[:, vals] = (t * (gv * _sigmoid(gv))).astype(BF16)

    return _pallas(
        body, name="ret_fwd", grid=(RH // G, nb),
        in_specs=[pl.BlockSpec((T, G * RET_HEAD_COLS), lambda h, b: (b, h)),
                  pl.BlockSpec((T, RET_QK), lambda h, b: (b, 0)),
                  pl.BlockSpec((T, RET_QK), lambda h, b: (b, 0)),
                  pl.BlockSpec((G, 8, LANES), lambda h, b: (h, 0, 0)),
                  pl.BlockSpec((1, G * RET_V), lambda h, b: (0, h))],
        out_specs=[pl.BlockSpec((T, G * RET_V), lambda h, b: (b, h)),
                   pl.BlockSpec((T, G * RET_V), lambda h, b: (b, h)),
                   pl.BlockSpec((G, None, RET_QK, RET_V), lambda h, b: (h, b, 0, 0))],
        out_shape=[jax.ShapeDtypeStruct((S, RH * RET_V), F32), jax.ShapeDtypeStruct((S, RH * RET_V), BF16),
                   jax.ShapeDtypeStruct((RH, nb, RET_QK, RET_V), F32)],
        scratch_shapes=[pltpu.VMEM((G, RET_QK, RET_V), F32)],
        compiler_params=_params(("parallel", "arbitrary")),
    )(proj, cosr, sinr, lgam, gain)


def _ret_bwd(proj, cosr, sinr, lgam, gain, ry, dgated, states, dproj, RH, *, T):
    S = proj.shape[0]
    nb = S // T
    G = _tile(RH, RET_GROUP)
    heads = range(G)
    scale = RET_QK ** -0.5

    def body(p_ref, cos_ref, sin_ref, lg_ref, gain_ref, ry_ref, dg_ref, st_ref, _, dp_ref, dgain_ref, dstate):
        b = pl.program_id(1)

        @pl.when(b == 0)
        def _():
            dstate[...] = jnp.zeros_like(dstate)

        consts = [_ret_consts(lg_ref[h, 0:1, 0:1], T) for h in heads]
        qd, kd, decay, cdec = [[c[i] for c in consts] for i in range(4)]
        cosv, sinv = cos_ref[...], sin_ref[...]
        q = [_rope128(p_ref[:, _head_cols(h, _RQ)], cosv, sinv) for h in heads]
        k = [_rope128(p_ref[:, _head_cols(h, _RK)], cosv, sinv) * scale for h in heads]
        v = [p_ref[:, _head_cols(h, _RV)] for h in heads]
        sprev = [st_ref[h] for h in heads]
        ds_new = [dstate[h] for h in heads]
        a = [_dot(q[h], k[h], NT) for h in heads]
        do, gparts = [], []
        for h in heads:
            vals = slice(h * RET_V, (h + 1) * RET_V)
            o = ry_ref[:, vals]
            mu = jnp.mean(o, axis=-1, keepdims=True)
            oc = o - mu
            rstd = lax.rsqrt(jnp.mean(oc * oc, axis=-1, keepdims=True) + EPS)
            ryn = oc * rstd
            gainv = gain_ref[:, vals]
            gv = p_ref[:, _head_cols(h, _RG)]
            sg = _sigmoid(gv)
            dgt = dg_ref[:, vals]
            dt = dgt * (gv * sg)
            dp_ref[:, _head_cols(h, _RG)] = (dgt * (ryn * gainv) * (sg * (1.0 + gv * (1.0 - sg)))).astype(BF16)
            gparts.append(jnp.sum(dt * ryn, axis=0, keepdims=True))
            dryn = dt * gainv
            do.append(rstd * (dryn - jnp.mean(dryn, axis=-1, keepdims=True)
                              - ryn * jnp.mean(dryn * ryn, axis=-1, keepdims=True)))
        gpart = jnp.concatenate(gparts, axis=1)

        @pl.when(b == 0)
        def _():
            dgain_ref[...] = gpart

        @pl.when(b > 0)
        def _():
            dgain_ref[...] += gpart

        dpm = [_dot(do[h], v[h], NT) for h in heads]
        dq_s = [_dot(do[h], sprev[h], NT) for h in heads]
        dk_s = [_dot(v[h], ds_new[h], NT) for h in heads]
        dv_s = [_dot(k[h] * kd[h], ds_new[h], NN) for h in heads]
        dst = [_dot(q[h] * qd[h], do[h], TN) for h in heads]
        a = [a[h] * decay[h] for h in heads]
        dpm = [dpm[h] * decay[h] for h in heads]
        dv = [_dot(a[h], do[h], TN) + dv_s[h] for h in heads]
        dq = [_dot(dpm[h], k[h], NN) + dq_s[h] * qd[h] for h in heads]
        dk = [(_dot(dpm[h], q[h], TN) + dk_s[h] * kd[h]) * scale for h in heads]
        for h in heads:
            dstate[h] = ds_new[h] * cdec[h] + dst[h]
            dp_ref[:, _head_cols(h, _RV)] = dv[h].astype(BF16)
            dp_ref[:, _head_cols(h, _RQ)] = _rope128_t(dq[h], cosv, sinv).astype(BF16)
            dp_ref[:, _head_cols(h, _RK)] = _rope128_t(dk[h], cosv, sinv).astype(BF16)

    rb = lambda b: nb - 1 - b
    return _pallas(
        body, name="ret_bwd", grid=(RH // G, nb),
        in_specs=[pl.BlockSpec((T, G * RET_HEAD_COLS), lambda h, b: (rb(b), h)),
                  pl.BlockSpec((T, RET_QK), lambda h, b: (rb(b), 0)),
                  pl.BlockSpec((T, RET_QK), lambda h, b: (rb(b), 0)),
                  pl.BlockSpec((G, 8, LANES), lambda h, b: (h, 0, 0)),
                  pl.BlockSpec((1, G * RET_V), lambda h, b: (0, h)),
                  pl.BlockSpec((T, G * RET_V), lambda h, b: (rb(b), h)),
                  pl.BlockSpec((T, G * RET_V), lambda h, b: (rb(b), h)),
                  pl.BlockSpec((G, None, RET_QK, RET_V), lambda h, b: (h, rb(b), 0, 0)),
                  _ANY],
        out_specs=[pl.BlockSpec((T, G * RET_HEAD_COLS), lambda h, b: (rb(b), h)),
                   pl.BlockSpec((1, G * RET_V), lambda h, b: (0, h))],
        out_shape=[jax.ShapeDtypeStruct(dproj.shape, dproj.dtype), jax.ShapeDtypeStruct((1, RH * RET_V), F32)],
        scratch_shapes=[pltpu.VMEM((G, RET_QK, RET_V), F32)],
        input_output_aliases={8: 0},
        compiler_params=_params(("parallel", "arbitrary")),
    )(proj, cosr, sinr, lgam, gain, ry, dgated, states, dproj)


def _rope_pe(t, c, s1, s2):
    return t * c + pltpu.roll(t, LANES - QK_ROPE // 2, 1) * s1 + pltpu.roll(t, QK_ROPE // 2, 1) * s2


def _rope_pe_t(d, c, s1, s2):
    return d * c + pltpu.roll(d * s1, QK_ROPE // 2, 1) + pltpu.roll(d * s2, LANES - QK_ROPE // 2, 1)


ATTN_C2 = (QK_NOPE + QK_ROPE) ** -0.5 * LOG2E


def _qkv_proj(cqn, ckvn, wq, wkv, kpe, tabs, MH, *, tm=512, heads=4):
    S = cqn.shape[0]
    tm = _tile(S, tm)
    hb = _tile(MH, heads)
    W = 2 * LANES
    c_t, s1_t, s2_t = tabs

    def body(cq_ref, ckv_ref, wq_ref, wkv_ref, kpe_ref, c_ref, s1_ref, s2_ref, qf_ref, kf_ref, v_ref):
        c, s1, s2 = c_ref[...], s1_ref[...], s2_ref[...]
        q = _dot(cq_ref[...], wq_ref[...], NN)
        kv = _dot(ckv_ref[...], wkv_ref[...], NN)
        kper = _rope_pe(kpe_ref[...], c, s1, s2).astype(BF16)
        for h in range(hb):
            lo, mid, hi = h * W, h * W + QK_NOPE, (h + 1) * W
            qf_ref[:, lo:mid] = (q[:, lo:mid] * ATTN_C2).astype(BF16)
            qf_ref[:, mid:hi] = (_rope_pe(q[:, mid:hi], c, s1, s2) * ATTN_C2).astype(BF16)
            kf_ref[:, lo:mid] = kv[:, lo:mid].astype(BF16)
            kf_ref[:, mid:hi] = kper
            v_ref[:, h * V_HEAD:(h + 1) * V_HEAD] = kv[:, mid:hi].astype(BF16)

    tab = pl.BlockSpec((tm, LANES), lambda i, j: (i, 0))
    grp = pl.BlockSpec((tm, hb * W), lambda i, j: (i, j))
    return _pallas(
        body, name="qkv_proj", grid=(S // tm, MH // hb),
        in_specs=[pl.BlockSpec((tm, cqn.shape[1]), lambda i, j: (i, 0)),
                  pl.BlockSpec((tm, ckvn.shape[1]), lambda i, j: (i, 0)),
                  pl.BlockSpec((wq.shape[0], hb * W), lambda i, j: (0, j)),
                  pl.BlockSpec((wkv.shape[0], hb * W), lambda i, j: (0, j)), tab, tab, tab, tab],
        out_specs=[grp, grp, pl.BlockSpec((tm, hb * V_HEAD), lambda i, j: (i, j))],
        out_shape=[jax.ShapeDtypeStruct((S, MH * W), BF16)] * 2 + [jax.ShapeDtypeStruct((S, MH * V_HEAD), BF16)],
        compiler_params=_params(("parallel", "parallel")),
    )(cqn, ckvn, wq, wkv, kpe, c_t, s1_t, s2_t)


def _chunk_mask(T):
    n = lax.broadcasted_iota(jnp.int32, (T, T), 0)
    m = lax.broadcasted_iota(jnp.int32, (T, T), 1)
    return (m // CHUNK) <= (n // CHUNK)


def _lanes_to(v, width):
    return jnp.tile(v, (1, width // LANES))


def _attn_fwd(qf, kf, vb, MH, *, T, heads, name, prev=(), deps=()):
    S = qf.shape[0]
    nt = S // T
    n_skip = len(prev) + len(deps)

    def body(q_ref, k_ref, v_ref, *rest):
        o_ref, ob_ref, lse_ref, m_sc, l_sc, acc_sc, s_a, s_b = rest[n_skip:]
        qi = pl.program_id(1)
        m_sc[...] = jnp.full_like(m_sc, NEG)
        l_sc[...] = jnp.zeros_like(l_sc)
        acc_sc[...] = jnp.zeros_like(acc_sc)

        def rows_of(kt):
            return pl.ds(pl.multiple_of(kt * T, T), T)

        def scores(kt):
            return _dot(q_ref[...], k_ref[rows_of(kt), :], NT)

        def update(s, kt):
            m_prev = m_sc[...]
            m_new = jnp.maximum(m_prev, jnp.max(s, axis=-1, keepdims=True))
            alpha = jnp.exp2(m_prev - m_new)
            p = jnp.exp2(s - _lanes_to(m_new, T))
            l_sc[...] = alpha * l_sc[...] + jnp.sum(p, axis=-1, keepdims=True)
            acc_sc[...] = alpha * acc_sc[...] + _dot(p, v_ref[rows_of(kt), :], NN)
            m_sc[...] = m_new

        def masked(s):
            return jnp.where(_chunk_mask(T), s, NEG)

        @pl.when(qi == 0)
        def _():
            update(masked(scores(0)), 0)

        @pl.when(qi > 0)
        def _():
            s_a[...] = masked(scores(qi))
            s_b[...] = scores(0)
            update(s_a[...], qi)
            s_a[...] = scores(jnp.minimum(1, qi - 1))
            update(s_b[...], 0)

            def pair(j, carry):
                s_b[...] = scores(2 * j)
                update(s_a[...], 2 * j - 1)
                s_a[...] = scores(jnp.minimum(2 * j + 1, qi - 1))
                update(s_b[...], 2 * j)
                return carry

            lax.fori_loop(1, (qi + 1) // 2, pair, 0)

            @pl.when(qi % 2 == 0)
            def _():
                update(s_a[...], qi - 1)
        l = l_sc[...]
        o = acc_sc[...] / l
        o_ref[...] = o
        ob_ref[...] = o.astype(BF16)
        lse_ref[...] = m_sc[...] + jnp.log(l) * LOG2E

    h0, h1 = heads
    out_shape = [jax.ShapeDtypeStruct((S, MH * LANES), F32), jax.ShapeDtypeStruct((S, MH * LANES), BF16),
                 jax.ShapeDtypeStruct((MH, S, LANES), F32)]
    row = pl.BlockSpec((T, LANES), lambda h, i: (i, h0 + h))
    return _pallas(
        body, name=name, grid=(h1 - h0, nt),
        in_specs=[pl.BlockSpec((T, 2 * LANES), lambda h, i: (i, h0 + h)),
                  pl.BlockSpec((S, 2 * LANES), lambda h, i: (0, h0 + h)),
                  pl.BlockSpec((S, LANES), lambda h, i: (0, h0 + h))] + [_ANY] * (len(prev) + len(deps)),
        out_specs=[row, row, pl.BlockSpec((None, T, LANES), lambda h, i: (h0 + h, i, 0))],
        out_shape=out_shape,
        scratch_shapes=[pltpu.VMEM((T, LANES), F32), pltpu.VMEM((T, LANES), F32), pltpu.VMEM((T, LANES), F32),
                        pltpu.VMEM((T, T), F32), pltpu.VMEM((T, T), F32)],
        input_output_aliases={3 + i: i for i in range(len(prev))},
        compiler_params=_params(("parallel", "parallel")),
    )(qf, kf, vb, *prev, *deps)


def _attn_bwd(qf, kf, vb, dob, lse2, delta, tabs, MH, *, T, deps=()):
    S = qf.shape[0]
    nt = S // T
    scale = (QK_NOPE + QK_ROPE) ** -0.5
    n_dep = len(deps)

    def body(q_ref, k_ref, v_ref, do_ref, lse_ref, dl_ref, c_ref, s1_ref, s2_ref, *rest):
        dqa_ref, dkv_ref, dkpe_ref, dq_ref, dk_sc, dv_sc, s_a, dp_a, s_b, dp_b = rest[n_dep:]
        kj = pl.program_id(1)

        @pl.when(kj == 0)
        def _():
            dq_ref[...] = jnp.zeros_like(dq_ref)

        dk_sc[...] = jnp.zeros_like(dk_sc)
        dv_sc[...] = jnp.zeros_like(dv_sc)

        def rows_of(qt):
            return pl.ds(pl.multiple_of(qt * T, T), T)

        def products(qt):
            rows = rows_of(qt)
            return _dot(q_ref[rows, :], k_ref[...], NT), _dot(do_ref[rows, :], v_ref[...], NT)

        def update(s, dp, qt):
            rows = rows_of(qt)
            q, dov = q_ref[rows, :], do_ref[rows, :]
            p = jnp.exp2(s - _lanes_to(lse_ref[rows, :], T))
            ds = p * (dp - _lanes_to(dl_ref[rows, :], T))
            dv_sc[...] += _dot(p, dov, TN)
            dk_sc[...] += _dot(ds, q, TN)
            dq_ref[rows, :] += _dot(ds, k_ref[...], NN)

        def masked(s):
            return jnp.where(_chunk_mask(T), s, NEG)

        @pl.when(kj == nt - 1)
        def _():
            s, dp = products(kj)
            update(masked(s), dp, kj)

        @pl.when(kj < nt - 1)
        def _():
            s, dp = products(kj)
            s_a[...], dp_a[...] = masked(s), dp
            s_b[...], dp_b[...] = products(kj + 1)
            update(s_a[...], dp_a[...], kj)
            s_a[...], dp_a[...] = products(jnp.minimum(kj + 2, nt - 1))
            update(s_b[...], dp_b[...], kj + 1)

            def pair(j, carry):
                t0 = kj + 2 * j
                s_b[...], dp_b[...] = products(t0 + 1)
                update(s_a[...], dp_a[...], t0)
                s_a[...], dp_a[...] = products(jnp.minimum(t0 + 2, nt - 1))
                update(s_b[...], dp_b[...], t0 + 1)
                return carry

            lax.fori_loop(1, (nt - kj) // 2, pair, 0)

            @pl.when((nt - kj) % 2 == 1)
            def _():
                update(s_a[...], dp_a[...], nt - 1)
        dkv_ref[:, :QK_NOPE] = (dk_sc[:, :QK_NOPE] * (1.0 / LOG2E)).astype(BF16)
        dkv_ref[:, QK_NOPE:] = dv_sc[...].astype(BF16)
        dkpe_ref[...] = dk_sc[:, QK_NOPE:] * (1.0 / LOG2E)

        @pl.when(kj == nt - 1)
        def _():
            dqa_ref[:, :QK_NOPE] = (dq_ref[:, :QK_NOPE] * scale).astype(BF16)
            dqa_ref[:, QK_NOPE:] = (_rope_pe_t(dq_ref[:, QK_NOPE:], c_ref[...], s1_ref[...], s2_ref[...])
                                    * scale).astype(BF16)

    stat = pl.BlockSpec((None, S, LANES), lambda h, j: (h, 0, 0))
    tab = pl.BlockSpec((S, LANES), lambda h, j: (0, 0))
    return _pallas(
        body, name="attn_bwd", grid=(MH, nt),
        in_specs=[pl.BlockSpec((S, 2 * LANES), lambda h, j: (0, h)),
                  pl.BlockSpec((T, 2 * LANES), lambda h, j: (j, h)),
                  pl.BlockSpec((T, LANES), lambda h, j: (j, h)),
                  pl.BlockSpec((S, LANES), lambda h, j: (0, h)), stat, stat, tab, tab, tab] + [_ANY] * n_dep,
        out_specs=[pl.BlockSpec((S, 2 * LANES), lambda h, j: (0, h)),
                   pl.BlockSpec((T, 2 * LANES), lambda h, j: (j, h)),
                   pl.BlockSpec((T, LANES), lambda h, j: (j, h))],
        out_shape=[jax.ShapeDtypeStruct((S, MH * 2 * LANES), BF16), jax.ShapeDtypeStruct((S, MH * 2 * LANES), BF16),
                   jax.ShapeDtypeStruct((S, MH * LANES), F32)],
        scratch_shapes=[pltpu.VMEM((S, 2 * LANES), F32), pltpu.VMEM((T, 2 * LANES), F32), pltpu.VMEM((T, LANES), F32)]
        + [pltpu.VMEM((T, T), F32)] * 4,
        compiler_params=_params(("parallel", "arbitrary")),
    )(qf, kf, vb, dob, lse2, delta, *tabs, *deps)


def _kpe_sum(dkpe_h, tabs, MH, *, tr=256):
    S = dkpe_h.shape[0]
    tr = _tile(S, tr)

    def body(dk_ref, c_ref, s1_ref, s2_ref, dkpe_ref):
        tot = dk_ref[:, :LANES]
        for h in range(1, MH):
            tot = tot + dk_ref[:, h * LANES:(h + 1) * LANES]
        dkpe_ref[...] = _rope_pe_t(tot, c_ref[...], s1_ref[...], s2_ref[...]).astype(BF16)

    tab = pl.BlockSpec((tr, LANES), lambda i: (i, 0))
    return _pallas(
        body, name="kpe_sum", grid=(S // tr,),
        in_specs=[pl.BlockSpec((tr, MH * LANES), lambda i: (i, 0)), tab, tab, tab],
        out_specs=tab, out_shape=jax.ShapeDtypeStruct((S, LANES), BF16),
        compiler_params=_params(("parallel",)),
    )(dkpe_h, *tabs)


ROW_ALIGN = 16


def _blk(R, C, block_bytes=2 << 20):
    cap = max(ROW_ALIGN, block_bytes // (C * 4))
    for t in range(min(R, cap) // ROW_ALIGN * ROW_ALIGN, LANES - 1, -ROW_ALIGN):
        if R % t == 0:
            return t, C
    if R <= cap:
        return R, C
    tc = C
    while R * tc * 4 > block_bytes and tc % (2 * LANES) == 0:
        tc //= 2
    return R, tc


def _rows_call(fn, ins, out_dtypes, *, name, deps=()):
    R, C = ins[0].shape
    tr, tc = _blk(R, C)
    n_in, n_dep = len(ins), len(deps)

    def body(*refs):
        vals = fn(*[r[...] for r in refs[:n_in]])
        for r, v in zip(refs[n_in + n_dep:], vals):
            r[...] = v.astype(r.dtype)

    blk = pl.BlockSpec((tr, tc), lambda i, j: (i, j))
    res = _pallas(
        body, name=name, grid=(R // tr, C // tc), in_specs=[blk] * n_in + [_ANY] * n_dep,
        out_specs=[blk] * len(out_dtypes),
        out_shape=[jax.ShapeDtypeStruct((R, C), d) for d in out_dtypes],
        compiler_params=_params(("parallel", "parallel")),
    )(*ins, *deps)
    return res


STREAM_BUFS = 3


def _stream_call(fn, ins, n_out, *, name, deps=()):
    R, C = ins[0].shape
    tr, tc = _blk(R, C, 1 << 20)
    assert tc == C
    n_in, n_dep, steps = len(ins), len(deps), R // tr

    def body(*refs):
        in_refs = refs[:n_in]
        out_refs = refs[n_in + n_dep:n_in + n_dep + n_out]
        ibuf, obuf, isem, osem = refs[n_in + n_dep + n_out:]

        def rows(i):
            return pl.ds(pl.multiple_of(i * tr, tr), tr)

        def read(i, slot, k):
            return pltpu.make_async_copy(in_refs[k].at[rows(i)], ibuf.at[slot, k], isem.at[slot, k])

        def write(i, slot, k):
            return pltpu.make_async_copy(obuf.at[slot, k], out_refs[k].at[rows(i)], osem.at[slot, k])

        for s in range(min(STREAM_BUFS, steps)):
            for k in range(n_in):
                read(s, s, k).start()

        def step(i, carry):
            slot, oslot = i % STREAM_BUFS, i % 2
            for k in range(n_in):
                read(i, slot, k).wait()

            @pl.when(i >= 2)
            def _():
                for k in range(n_out):
                    write(i - 2, oslot, k).wait()

            vals = fn(*[ibuf[slot, k] for k in range(n_in)])
            for k, val in enumerate(vals):
                obuf[oslot, k] = val
            for k in range(n_out):
                write(i, oslot, k).start()

            @pl.when(i + STREAM_BUFS < steps)
            def _():
                for k in range(n_in):
                    read(i + STREAM_BUFS, slot, k).start()

            return carry

        lax.fori_loop(0, steps, step, 0)
        for i in range(max(steps - 2, 0), steps):
            for k in range(n_out):
                write(i, i % 2, k).wait()

    return _pallas(
        body, name=name, in_specs=[_ANY] * (n_in + n_dep), out_specs=[_ANY] * n_out,
        out_shape=[jax.ShapeDtypeStruct((R, C), F32)] * n_out,
        scratch_shapes=[pltpu.VMEM((STREAM_BUFS, n_in, tr, C), F32), pltpu.VMEM((2, n_out, tr, C), F32),
                        pltpu.SemaphoreType.DMA((STREAM_BUFS, n_in)), pltpu.SemaphoreType.DMA((2, n_out))],
        compiler_params=pltpu.CompilerParams(vmem_limit_bytes=VMEM_LIMIT),
    )(*ins, *deps)


def _adamw_vals(w, g, m, v):
    m = ADAM_B1 * m + (1.0 - ADAM_B1) * g
    v = ADAM_B2 * v + (1.0 - ADAM_B2) * (g * g)
    m_hat = m / (1.0 - ADAM_B1 ** ADAM_STEP)
    v_hat = v / (1.0 - ADAM_B2 ** ADAM_STEP)
    delta = -ADAM_LR * (m_hat / (jnp.sqrt(v_hat) + ADAM_EPS) + ADAM_WD * w)
    return delta, m, v


def _sum_pair(p, theirs, place, *, name):
    _, R, C = p.shape
    R2 = R // 2
    tr, tc = _blk(R2, C)
    p4 = p.reshape(N_CHIPS, 2, R2, C)

    def body(place_ref, a_ref, b_ref, o_ref):
        o_ref[...] = (a_ref[...].astype(F32) + b_ref[...].astype(F32)).astype(BF16)

    spec = pltpu.PrefetchScalarGridSpec(
        num_scalar_prefetch=1, grid=(N_CHIPS, R2 // tr, C // tc),
        in_specs=[pl.BlockSpec((None, None, tr, tc), lambda q, i, j, pr: (q, pr[0], i, j)),
                  pl.BlockSpec((None, tr, tc), lambda q, i, j, pr: (q, i, j))],
        out_specs=pl.BlockSpec((None, tr, tc), lambda q, i, j, pr: (q, i, j)))
    return _pallas(body, name=name, grid_spec=spec, out_shape=jax.ShapeDtypeStruct((N_CHIPS, R2, C), BF16),
                   compiler_params=_params(("parallel", "parallel", "parallel")))(place, p4, theirs)


def _sum_chips(p, theirs, recv, place, *, name):
    _, R, C = p.shape
    R2 = R // 2
    tr, tc = _blk(R2, C)
    p4 = p.reshape(N_CHIPS, 2, R2, C)

    def body(place_ref, a_ref, b_ref, r0_ref, r1_ref, r2_ref, o_ref):
        own = a_ref[...].astype(F32) + b_ref[...].astype(F32)
        o_ref[...] = ((own + r0_ref[...].astype(F32)) + r1_ref[...].astype(F32)) + r2_ref[...].astype(F32)

    def slot(k):
        return pl.BlockSpec((None, tr, tc), lambda i, j, pr: (k, i, j))

    spec = pltpu.PrefetchScalarGridSpec(
        num_scalar_prefetch=1, grid=(R2 // tr, C // tc),
        in_specs=[pl.BlockSpec((None, None, tr, tc), lambda i, j, pr: (pr[1], pr[0], i, j)),
                  pl.BlockSpec((None, tr, tc), lambda i, j, pr: (pr[1], i, j)), slot(0), slot(1), slot(2)],
        out_specs=pl.BlockSpec((None, tr, tc), lambda i, j, pr: (pr[0], i, j)))
    return _pallas(body, name=name, grid_spec=spec, out_shape=jax.ShapeDtypeStruct((2, R2, C), F32),
                   compiler_params=_params(("parallel", "parallel")))(place, p4, theirs, recv, recv, recv)


def _me():
    return lax.axis_index("x"), lax.axis_index("y"), lax.axis_index("c")


def _other_chips(x, y):
    return [(1 - x, y), (x, 1 - y), (1 - x, 1 - y)]


def _rcopy(src, dst, ssem, rsem, dev):
    return pltpu.make_async_remote_copy(src_ref=src, dst_ref=dst, send_sem=ssem, recv_sem=rsem,
                                        device_id=dev, device_id_type=MESH)


def _cast_into_slot(w, place, *, name, rows=None, deps=()):
    R, C = w.shape
    rows = R if rows is None else rows
    tr, tc = _blk(R, C)

    def body(place_ref, w_ref, *rest):
        rest[-1][...] = w_ref[...].astype(BF16)

    spec = pltpu.PrefetchScalarGridSpec(
        num_scalar_prefetch=1, grid=(R // tr, C // tc),
        in_specs=[pl.BlockSpec((tr, tc), lambda i, j, pr: (i, j))] + [_ANY] * len(deps),
        out_specs=pl.BlockSpec((None, tr, tc), lambda i, j, pr: (pr[1], i, j)))
    out = _pallas(body, name=name, grid_spec=spec, out_shape=jax.ShapeDtypeStruct((N_CHIPS, rows, C), BF16),
                  compiler_params=_params(("parallel", "parallel")))(place, w, *deps)
    return out.reshape(N_CHIPS, 2, rows // 2, C)


def _gather_ici_plan(bufs):
    x, y, c = _me()
    j = 2 * x + y
    plan = []
    for i, buf in enumerate(bufs):
        for k, (px, py) in enumerate(_other_chips(x, y)):
            plan.append((3 * i + k, buf.at[j, c], buf.at[j, c], (px, py, c)))
    return plan


def _forward_halves(bufs, *, name):
    n = len(bufs)

    def body(*refs):
        outs = refs[n:2 * n]
        ssem, rsem = refs[2 * n:]
        x, y, c = _me()
        sib = (x, y, 1 - c)
        cps = []
        for i in range(n):
            for k, (px, py) in enumerate(_other_chips(x, y)):
                slot = outs[i].at[2 * px + py, c]
                r = _rcopy(slot, slot, ssem.at[3 * i + k], rsem.at[3 * i + k], sib)
                r.start()
                cps.append(r)
        for r in cps:
            r.wait()

    return _pallas(
        body, name=name, in_specs=[_ANY] * n, out_specs=[_ANY] * n,
        out_shape=[jax.ShapeDtypeStruct(b.shape, b.dtype) for b in bufs],
        scratch_shapes=[pltpu.SemaphoreType.DMA((3 * n,))] * 2,
        input_output_aliases={i: i for i in range(n)},
        compiler_params=pltpu.CompilerParams(has_side_effects=True),
    )(*bufs)


_HBM = pl.BlockSpec(memory_space=pltpu.HBM)
_SEM = pl.BlockSpec(memory_space=pltpu.SEMAPHORE)
_EFFECT = pltpu.SideEffectType.DATAFLOW_SIDE_EFFECTING


def _split_start(bufs, plan, n_copies, *, name):
    n = len(bufs)

    def body(*refs):
        ssem, rsem = refs[n], refs[n + 1]
        for s, src, dst, dev in plan(refs[:n]):
            _rcopy(src, dst, ssem.at[s], rsem.at[s], dev).start()
        refs[-1][...] = jnp.zeros_like(refs[-1])

    res = _pallas(
        body, name=name, in_specs=[_HBM] * n,
        out_specs=(_SEM, _SEM, *[_HBM] * n, pl.BlockSpec(memory_space=pltpu.VMEM)),
        out_shape=(pltpu.SemaphoreType.DMA((n_copies,)), pltpu.SemaphoreType.DMA((n_copies,)),
                   *[pltpu.HBM(b.shape, b.dtype) for b in bufs], jax.ShapeDtypeStruct((8, LANES), F32)),
        input_output_aliases={i: 2 + i for i in range(n)},
        compiler_params=pltpu.CompilerParams(has_side_effects=_EFFECT),
    )(*[pltpu.with_memory_space_constraint(b, pltpu.HBM) for b in bufs])
    return res[0], res[1], list(res[2:2 + n]), res[-1]


def _split_wait(ssem, rsem, bufs, after, plan, *, name):
    n = len(bufs)

    def body(*refs):
        ssem_ref, rsem_ref = refs[n], refs[n + 1]
        for s, src, dst, dev in plan(refs[:n]):
            cp = _rcopy(src, dst, ssem_ref.at[s], rsem_ref.at[s], dev)
            cp.wait_send()
            cp.wait_recv()

    return list(_pallas(
        body, name=name, in_specs=[_HBM] * n + [_SEM, _SEM, _ANY], out_specs=[_HBM] * n,
        out_shape=[pltpu.HBM(b.shape, b.dtype) for b in bufs],
        input_output_aliases={i: i for i in range(n)},
        compiler_params=pltpu.CompilerParams(has_side_effects=_EFFECT),
    )(*bufs, ssem, rsem, after))


def _forward_plan(bufs):
    x, y, c = _me()
    plan = []
    for i, buf in enumerate(bufs):
        for k, (px, py) in enumerate(_other_chips(x, y)):
            plan.append((3 * i + k, buf.at[2 * px + py, c], buf.at[2 * px + py, c], (x, y, 1 - c)))
    return plan


def _join_plan(bufs):
    x, y, c = _me()
    return [(i, buf.at[c], buf.at[c], (x, y, 1 - c)) for i, buf in enumerate(bufs)]


def _swap_plan(n):
    def plan(bufs):
        x, y, c = _me()
        return [(i, bufs[i].at[:, 1 - c], bufs[n + i], (x, y, 1 - c)) for i in range(n)]
    return plan


def _scatter_plan(n):
    def plan(bufs):
        x, y, c = _me()
        out = []
        for i in range(n):
            for k, (px, py) in enumerate(_other_chips(x, y)):
                out.append((3 * i + k, bufs[i].at[2 * px + py], bufs[n + i].at[k], (px, py, c)))
        return out
    return plan


def _allreduce_small(parts, loss11):
    n = len(parts)
    widths = [p.shape[1] for p in parts]
    total = sum(widths) + LANES

    def body(*refs):
        o_ref, mine, buf, ssem, rsem = refs[n + 1:]
        x, y, c = _me()
        me = 4 * x + 2 * y + c
        off = 0
        for r, w in zip(refs[:n], widths):
            mine[:, off:off + w] = r[...]
            off += w
        mine[:, off:] = jnp.broadcast_to(refs[n][...], (1, LANES))
        buf[me] = mine[...]
        cps = []
        for k in range(1, 8):
            peer = (x ^ (k >> 2), y ^ ((k >> 1) & 1), c ^ (k & 1))
            r = _rcopy(mine, buf.at[me], ssem.at[k - 1], rsem.at[k - 1], peer)
            r.start()
            cps.append(r)
        for k in range(1, 8):
            peer = (x ^ (k >> 2), y ^ ((k >> 1) & 1), c ^ (k & 1))
            pid = 4 * peer[0] + 2 * peer[1] + peer[2]
            _rcopy(mine, buf.at[pid], ssem.at[k - 1], rsem.at[k - 1], peer).wait_recv()
        for r in cps:
            r.wait_send()
        tot = buf[0]
        for d in range(1, 8):
            tot = tot + buf[d]
        o_ref[...] = tot

    vm = pl.BlockSpec(memory_space=pltpu.VMEM)
    return _pallas(
        body, name="allreduce_small", in_specs=[vm] * (n + 1), out_specs=vm,
        out_shape=jax.ShapeDtypeStruct((1, total), F32),
        scratch_shapes=[pltpu.VMEM((1, total), F32), pltpu.VMEM((8, 1, total), F32),
                        pltpu.SemaphoreType.DMA((7,)), pltpu.SemaphoreType.DMA((7,))],
        compiler_params=pltpu.CompilerParams(has_side_effects=True),
    )(*parts, loss11)


def _adamw_small(red, ws, ms, vs):
    n = len(ws)

    def body(*refs):
        red_ref = refs[0]
        outs = refs[1 + 3 * n:]
        off = 0
        for i in range(n):
            w = refs[1 + i].shape[1]
            g = red_ref[:, off:off + w]
            d, m, v = _adamw_vals(refs[1 + i][...], g, refs[1 + n + i][...], refs[1 + 2 * n + i][...])
            for o, val in zip(outs[4 * i:4 * i + 4], (g, d, m, v)):
                o[...] = val
            off += w

    vm = pl.BlockSpec(memory_space=pltpu.VMEM)
    res = _pallas(
        body, name="adamw_small", in_specs=[vm] * (1 + 3 * n), out_specs=[vm] * (4 * n),
        out_shape=[jax.ShapeDtypeStruct(w.shape, F32) for w in ws for _ in range(4)],
    )(red, *ws, *ms, *vs)
    return [res[4 * i:4 * i + 4] for i in range(n)]


def _rope_tables(positions, S):
    pos = positions.reshape(S, 1).astype(F32)
    half = RET_QK // 2
    inv = ROPE_THETA ** (-jnp.arange(half, dtype=F32) / half)
    ang = pos * inv
    cosr = jnp.concatenate([jnp.cos(ang), jnp.cos(ang)], axis=1)
    sinr = jnp.concatenate([-jnp.sin(ang), jnp.sin(ang)], axis=1)
    half = QK_ROPE // 2
    inv = ROPE_THETA ** (-jnp.arange(half, dtype=F32) / half)
    ang = pos * inv
    z = jnp.zeros((S, half), F32)
    c = jnp.concatenate([jnp.cos(ang), jnp.cos(ang), z, z], axis=1)
    s1 = jnp.concatenate([-jnp.sin(ang), z, z, z], axis=1)
    s2 = jnp.concatenate([z, jnp.sin(ang), z, z], axis=1)
    return cosr, sinr, (c, s1, s2)


def _cat_cols(g):
    return jnp.concatenate([g[j] for j in range(N_CHIPS)], axis=1)


def _split_cols(w):
    return jnp.stack(jnp.split(w, N_CHIPS, axis=1))


def kernel(x, positions, norm_mix_g, w_in, ret_norm_g, w_ret_o, q_a_norm_g, w_q_b, kv_a_norm_g, w_kv_b, w_mla_o, w_out, norm_mlp_g, w_up, w_down, norm_f_g, loss_target, m_norm_mix_g, m_w_in, m_ret_norm_g, m_w_ret_o, m_q_a_norm_g, m_w_q_b, m_kv_a_norm_g, m_w_kv_b, m_w_mla_o, m_w_out, m_norm_mlp_g, m_w_up, m_w_down, m_norm_f_g, v_norm_mix_g, v_w_in, v_ret_norm_g, v_w_ret_o, v_q_a_norm_g, v_w_q_b, v_kv_a_norm_g, v_w_kv_b, v_w_mla_o, v_w_out, v_norm_mlp_g, v_w_up, v_w_down, v_norm_f_g):
    S, D = x.shape[1], x.shape[2]
    RVW = w_ret_o.shape[1] * N_CHIPS
    RH = RVW // RET_V
    RQW = RH * RET_QK
    MVW = w_mla_o.shape[1] * N_CHIPS
    MH = MVW // V_HEAD
    QL, KVL = w_q_b.shape[1], w_kv_b.shape[1]
    T_RET = _tile(S, 256)
    T_ATT = _tile(S, 512)

    xs = x.reshape(S, D)
    tgt = loss_target.reshape(S, D)
    cosr, sinr, pe_tabs = _rope_tables(positions, S)
    lgam = jnp.log(1.0 - 2.0 ** (-5.0 - jnp.arange(RH, dtype=F32)))
    lgam = jnp.broadcast_to(lgam[:, None, None], (RH, 8, LANES))

    big = ("w_in", "w_ret_o", "w_q_b", "w_kv_b", "w_mla_o", "w_out", "w_up", "w_down")
    w_sh = dict(w_in=w_in[0].T, w_ret_o=w_ret_o[0], w_q_b=w_q_b[0], w_kv_b=w_kv_b[0], w_mla_o=w_mla_o[0],
                w_out=w_out[0], w_up=w_up[0], w_down=w_down[0])
    m_sh = dict(w_in=m_w_in[0].T, w_ret_o=m_w_ret_o[0], w_q_b=m_w_q_b[0], w_kv_b=m_w_kv_b[0],
                w_mla_o=m_w_mla_o[0], w_out=m_w_out[0], w_up=m_w_up[0], w_down=m_w_down[0])
    v_sh = dict(w_in=v_w_in[0].T, w_ret_o=v_w_ret_o[0], w_q_b=v_w_q_b[0], w_kv_b=v_w_kv_b[0],
                w_mla_o=v_w_mla_o[0], w_out=v_w_out[0], w_up=v_w_up[0], w_down=v_w_down[0])
    col_sharded = ("w_q_b", "w_kv_b", "w_up")
    c_sh = w_in.shape[2]
    c_pad = -(-c_sh // 64) * 64
    place = jnp.stack([lax.axis_index("c"), 2 * lax.axis_index("x") + lax.axis_index("y")]).astype(jnp.int32)

    def whole(k, g):
        g = g.reshape(N_CHIPS, w_sh[k].shape[0], w_sh[k].shape[1])
        if k == "w_up":
            return g
        return _cat_cols(g) if k in col_sharded else g.reshape(-1, g.shape[2])

    first = ("w_in", "w_q_b", "w_kv_b")
    later = ("w_ret_o", "w_mla_o", "w_out", "w_up", "w_down")
    first_bufs = [_cast_into_slot(w_sh[k], place, name="cast_" + k, rows=c_pad if k == "w_in" else None)
                  for k in first]
    first_ssem, first_rsem, first_bufs, first_token = _split_start(
        first_bufs, _gather_ici_plan, 3 * len(first), name="gather_first_start")
    later_bufs = [_cast_into_slot(w_sh[k], place, name="cast_" + k, deps=(first_token,)) for k in later[:-1]]
    first_bufs = _split_wait(first_ssem, first_rsem, first_bufs, later_bufs[-1], _gather_ici_plan,
                             name="gather_first_wait")
    got = _forward_halves(first_bufs, name="gather_first_forward")
    full = {k: whole(k, g) for k, g in zip(first[1:], got[1:])}
    later_bufs.append(_cast_into_slot(w_sh[later[-1]], place, name="cast_" + later[-1], deps=(got[0],)))
    later_ssem, later_rsem, later_bufs, later_token = _split_start(
        later_bufs, _gather_ici_plan, 3 * len(later), name="gather_later_start")

    o_rq, o_rk, o_rv, o_rg = 0, RQW, 2 * RQW, 2 * RQW + RVW
    o_cq = 2 * RQW + 2 * RVW
    o_ckv, o_kpe = o_cq + QL, o_cq + QL + KVL
    o_gr = o_kpe + QK_ROPE
    o_gm = o_gr + D
    n_ret = RH * RET_HEAD_COLS
    off_gate, off_cq, off_ckv = n_ret, n_ret + 2 * D, n_ret + 2 * D + QL
    gate_tile = _tile(D, 1024)
    n_a = off_ckv + KVL
    runs = []
    for h in range(RH):
        base = h * RET_HEAD_COLS
        runs += [(o_rq + h * RET_QK, RET_QK, base), (o_rk + h * RET_QK, RET_QK, base + RET_QK),
                 (o_rv + h * RET_V, RET_V, base + 2 * RET_QK), (o_rg + h * RET_V, RET_V, base + 2 * RET_QK + RET_V)]
    for t in range(D // gate_tile):
        runs += [(o_gr + t * gate_tile, gate_tile, off_gate + 2 * t * gate_tile),
                 (o_gm + t * gate_tile, gate_tile, off_gate + (2 * t + 1) * gate_tile)]
    runs += [(o_cq, QL, off_cq), (o_ckv, KVL, off_ckv), (o_kpe, QK_ROPE, n_a)]

    def take(parts, start, width):
        out, lo = [], 0
        for p in parts:
            hi = lo + p.shape[0]
            a, b = max(start, lo), min(start + width, hi)
            if a < b:
                out.append(p[a - lo:b - lo])
            lo = hi
        return out

    wi = [got[0].reshape(N_CHIPS, c_pad, D)[jj, :c_sh] for jj in range(N_CHIPS)]
    here = sorted(runs, key=lambda r: r[2])
    wa = jnp.concatenate([p for s0, w, _ in here[:-1] for p in take(wi, s0, w)], axis=0)
    wkpe = jnp.concatenate(take(wi, o_kpe, QK_ROPE) + [jnp.zeros((LANES - QK_ROPE, D), BF16)], axis=0)
    wq = jnp.pad(full["w_q_b"].reshape(QL, MH, QK_NOPE + QK_ROPE),
                 ((0, 0), (0, 0), (0, LANES - QK_ROPE))).reshape(QL, MH * 2 * LANES)
    wkv = full["w_kv_b"]

    u, rstd0 = _rmsnorm_fwd(xs, norm_mix_g, name="norm_mix")
    proj = _mm(u, wa, mode="nt", outs=[F32], name="in_proj", deps=(later_token,))
    kpe = _mm(u, wkpe, mode="nt", outs=[F32], name="kpe_proj")
    ry, gated, states = _ret_fwd(proj, cosr, sinr, lgam, ret_norm_g, RH, T=T_RET)
    cqn, ckvn, rstd_q, rstd_kv = _norm_pair_fwd(proj, q_a_norm_g, kv_a_norm_g, off_cq)
    qf, kf, vb = _qkv_proj(cqn, ckvn, wq, wkv, kpe, pe_tabs, MH)
    first_half = _attn_fwd(qf, kf, vb, MH, T=T_ATT, heads=(0, MH // 2), name="attn_fwd_a")
    later_bufs = _split_wait(later_ssem, later_rsem, later_bufs, first_half[0], _gather_ici_plan,
                             name="gather_later_wait")
    fwd_ssem, fwd_rsem, later_bufs, fwd_token = _split_start(
        later_bufs, _forward_plan, 3 * len(later), name="gather_later_forward_start")
    my, my_b, lse2 = _attn_fwd(qf, kf, vb, MH, T=T_ATT, heads=(MH // 2, MH), name="attn_fwd_b",
                               prev=first_half, deps=(fwd_token,))
    later_bufs = _split_wait(fwd_ssem, fwd_rsem, later_bufs, my, _forward_plan, name="gather_later_forward_wait")
    full.update({k: whole(k, g) for k, g in zip(later, later_bufs)})
    y_ret = _mm(gated, full["w_ret_o"], mode="nn", outs=[BF16], name="ret_o")
    y_mla, merged = _mm(my_b, full["w_mla_o"], mode="nn", outs=[BF16, BF16], name="mla_o",
                        epi=lambda acc, gr, gm, yr: (acc, _sigmoid(gr) * yr + _sigmoid(gm) * acc),
                        extras=((proj, off_gate, 2), (proj, off_gate + gate_tile, 2), y_ret), tn=gate_tile)
    h1 = _mm(merged, full["w_out"], mode="nn", outs=[F32], name="out_proj",
             epi=lambda acc, r: (acc + r,), extras=(xs,))
    n1, rstd1 = _rmsnorm_fwd(h1, norm_mlp_g, name="norm_mlp")

    def up_epi(acc):
        r = jnp.maximum(acc, 0.0)
        return acc, r * r

    z, act = _mm(n1, full["w_up"], mode="nn", outs=[F32, BF16], name="up_proj", epi=up_epi)
    h2 = _mm(act, full["w_down"], mode="nn", outs=[F32], name="down_proj",
             epi=lambda acc, r: (acc + r,), extras=(h1,))
    loss11, dh2, dh2_b, g_norm_f = _final_loss(h2, norm_f_g.reshape(1, D), tgt)

    dz = _mm(dh2_b, full["w_down"], mode="nt", outs=[BF16], name="down_bwd_x",
             epi=lambda acc, zz: (acc * (2.0 * jnp.maximum(zz, 0.0)),), extras=(z,))
    g_w_down = _mm(act, dh2_b, mode="tn", outs=[BF16], name="down_bwd_w")
    dn1 = _mm(dz, full["w_up"], mode="nt", outs=[F32], name="up_bwd_x")
    g_w_up = _mm(n1, dz, mode="tn", outs=[BF16], name="up_bwd_w", out_shards=True)

    def scatter_begin(tag, sums):
        lands = [lax.empty((3,) + s.shape[1:], s.dtype) for s in sums]
        return _split_start(sums + lands, _scatter_plan(len(sums)), 3 * len(sums), name="scatter_" + tag + "_start")

    def swap_begin(tag, grads):
        views = [g if g.ndim == 3 else g.reshape(N_CHIPS, g.shape[0] // N_CHIPS, g.shape[1]) for g in grads]
        views = [v.reshape(N_CHIPS, 2, v.shape[1] // 2, v.shape[2]) for v in views]
        lands = [lax.empty((N_CHIPS,) + v.shape[2:], v.dtype) for v in views]
        return _split_start(views + lands, _swap_plan(len(views)), len(views), name="swap_" + tag + "_start")

    def swap_end(tag, names, handle, after):
        n = len(names)
        bufs = _split_wait(handle[0], handle[1], handle[2], after, _swap_plan(n), name="swap_" + tag + "_wait")
        pcs = [b.reshape(N_CHIPS, 2 * b.shape[2], b.shape[3]) for b in bufs[:n]]
        sums = [_sum_pair(p, t, place, name="sum_pair_" + k) for k, p, t in zip(names, pcs, bufs[n:])]
        return pcs, bufs[n:], sums

    g1 = ("w_up", "w_down")
    swap1 = swap_begin("g1", (g_w_up, g_w_down))
    dh1, g_norm_mlp, dh1_b = _rmsnorm_bwd(dn1, h1, rstd1, norm_mlp_g, name="norm_mlp_bwd", res=dh2,
                                          deps=(swap1[3],), bf16_copy=1)
    def merge_bwd_epi(dm, gr, gm, yr, ym):
        sr, sm = _sigmoid(gr), _sigmoid(gm)
        return dm * sr, dm * sm, jnp.concatenate([dm * yr * sr * (1.0 - sr), dm * ym * sm * (1.0 - sm)], axis=1)

    dy_ret, dy_mla, dproj = _mm(
        dh1_b, full["w_out"], mode="nt", outs=[BF16, BF16], name="out_bwd_x", epi=merge_bwd_epi,
        extras=((proj, off_gate, 2), (proj, off_gate + gate_tile, 2), y_ret, y_mla), tm=512, tn=gate_tile,
        more_outs=lambda tm, tn: [(jax.ShapeDtypeStruct(proj.shape, BF16),
                                   pl.BlockSpec((tm, 2 * tn), lambda i, j, k: (i, off_gate // (2 * tn) + j)))])
    pcs1, theirs1, sums1 = swap_end("g1", g1, swap1, dy_ret)
    ssem1, rsem1, bufs1, token1 = scatter_begin("g1", sums1)
    g_w_out = _mm(merged, dh1_b, mode="tn", outs=[BF16], name="out_bwd_w", deps=(token1,))
    dgated = _mm(dy_ret, full["w_ret_o"], mode="nt", outs=[F32], name="ret_o_bwd_x")
    g_w_ret_o = _mm(gated, dy_ret, mode="tn", outs=[BF16], name="ret_o_bwd_w")
    dproj, g_ret_norm = _ret_bwd(proj, cosr, sinr, lgam, ret_norm_g, ry, dgated, states, dproj, RH, T=T_RET)
    def delta_epi(acc, o):
        rows = acc.shape[0]
        return acc, [jnp.broadcast_to(jnp.sum(acc[:, lo:lo + V_HEAD] * o[:, lo:lo + V_HEAD], axis=-1, keepdims=True),
                                      (rows, LANES)) for lo in range(0, acc.shape[1], V_HEAD)]

    dob, delta = _mm(dy_mla, full["w_mla_o"], mode="nt", outs=[BF16], name="mla_o_bwd_x", epi=delta_epi,
                     extras=(my,), more_outs=lambda tm, tn: [
                         (jax.ShapeDtypeStruct((MH, S, LANES), F32),
                          pl.BlockSpec((tn // V_HEAD, tm, LANES), lambda i, j, k: (j, i, 0)))])
    g_w_mla_o = _mm(my_b, dy_mla, mode="tn", outs=[BF16], name="mla_o_bwd_w")
    g2 = ("w_out", "w_ret_o", "w_mla_o")
    swap2 = swap_begin("g2", (g_w_out, g_w_ret_o, g_w_mla_o))
    dq_all, dkv_all, dkpe_h = _attn_bwd(qf, kf, vb, dob, lse2, delta, pe_tabs, MH, T=T_ATT, deps=(swap2[3],))
    pcs2, theirs2, sums2 = swap_end("g2", g2, swap2, dkv_all)
    ssem2, rsem2, bufs2, token2 = scatter_begin("g2", sums2)
    dkpe = _kpe_sum(dkpe_h, pe_tabs, MH)
    dcqn = _mm(dq_all, wq, mode="nt", outs=[F32], name="q_bwd_x", deps=(token2,))
    g_wq = _mm(cqn, dq_all, mode="tn", outs=[BF16], name="q_bwd_w")
    dckvn = _mm(dkv_all, wkv, mode="nt", outs=[F32], name="kv_bwd_x")
    g_wkv = _mm(ckvn, dkv_all, mode="tn", outs=[BF16], name="kv_bwd_w")
    dproj, g_q_a, g_kv_a = _norm_pair_bwd(dcqn, dckvn, proj, rstd_q, rstd_kv, q_a_norm_g, kv_a_norm_g, dproj, off_cq)
    g_wa = _mm(dproj, u, mode="tn", outs=[BF16], name="in_bwd_w")
    g_wkpe = _mm(dkpe, u, mode="tn", outs=[BF16], name="kpe_bwd_w")

    there = sorted(runs)
    g_parts = [g_wa, g_wkpe]
    g_w_in = jnp.stack([jnp.concatenate(
        [p for s0, w, d0 in there for a, b in [(max(s0, jj * c_sh), min(s0 + w, (jj + 1) * c_sh))] if a < b
         for p in take(g_parts, d0 + a - s0, b - a)] + [jnp.zeros((c_pad - c_sh, D), BF16)], axis=0)
        for jj in range(N_CHIPS)])
    gq = g_wq.reshape(QL, MH, 2 * LANES)[:, :, :QK_NOPE + QK_ROPE].reshape(QL, MH * (QK_NOPE + QK_ROPE))
    g3 = ("w_in", "w_q_b", "w_kv_b")
    def chip_sums(names, pcs, theirs, recv):
        return [_sum_chips(p, t, r, place, name="sum_chips_" + k) for k, p, t, r in zip(names, pcs, theirs, recv)]

    swap3 = swap_begin("g3", (g_w_in, _split_cols(gq), _split_cols(g_wkv)))
    bufs1 = _split_wait(ssem1, rsem1, bufs1, swap3[3], _scatter_plan(len(g1)), name="scatter_g1_wait")
    halves1 = chip_sums(g1, pcs1, theirs1, bufs1[len(g1):])
    pcs3, theirs3, sums3 = swap_end("g3", g3, swap3, halves1[-1])
    ssem3, rsem3, bufs3, token3 = scatter_begin("g3", sums3)
    bufs2 = _split_wait(ssem2, rsem2, bufs2, token3, _scatter_plan(len(g2)), name="scatter_g2_wait")
    halves12 = halves1 + chip_sums(g2, pcs2, theirs2, bufs2[len(g2):])
    jssem, jrsem, halves12, join_token = _split_start(halves12, _join_plan, len(halves12), name="join_g12_start")
    du = _mm(dproj, wa, mode="nn", outs=[F32], name="in_bwd_x", tk=2816, tail=(dkpe, wkpe), deps=(join_token,))
    dx, g_norm_mix = _rmsnorm_bwd(du, xs, rstd0, norm_mix_g, name="norm_mix_bwd", res=dh1)

    bufs3 = _split_wait(ssem3, rsem3, bufs3, dx, _scatter_plan(len(g3)), name="scatter_g3_wait")
    halves12 = _split_wait(jssem, jrsem, halves12, dx, _join_plan, name="join_g12_wait")
    j3ssem, j3rsem, halves3, join3_token = _split_start(
        chip_sums(g3, pcs3, theirs3, bufs3[len(g3):]), _join_plan, len(g3), name="join_g3_start")

    small = ("norm_mix_g", "ret_norm_g", "q_a_norm_g", "kv_a_norm_g", "norm_mlp_g", "norm_f_g")
    g_small = [g_norm_mix, g_ret_norm, g_q_a, g_kv_a, g_norm_mlp, g_norm_f]
    red = _allreduce_small(g_small, loss11)
    loss = red[0, red.shape[1] - 1]
    w_small = [norm_mix_g, ret_norm_g, q_a_norm_g, kv_a_norm_g, norm_mlp_g, norm_f_g]
    m_small = [m_norm_mix_g, m_ret_norm_g, m_q_a_norm_g, m_kv_a_norm_g, m_norm_mlp_g, m_norm_f_g]
    v_small = [v_norm_mix_g, v_ret_norm_g, v_q_a_norm_g, v_kv_a_norm_g, v_norm_mlp_g, v_norm_f_g]
    row = lambda a: a.reshape(1, -1)
    upd = _adamw_small(red, [row(a) for a in w_small], [row(a) for a in m_small], [row(a) for a in v_small])
    out_g, out_d, out_m, out_v = {}, {}, {}, {}
    for k, wv, (g_, d_, m_, v_) in zip(small, w_small, upd):
        out_g[k], out_d[k], out_m[k], out_v[k] = [a.reshape(wv.shape) for a in (g_, d_, m_, v_)]

    def adamw_shard(k, joined, deps=()):
        g = joined.reshape(2 * joined.shape[1], joined.shape[2])
        step_fn = lambda w, g, m, v: (g,) + _adamw_vals(w, g, m, v)
        if k == "w_in":
            res = [r.T for r in _rows_call(step_fn, [w_sh[k], g, m_sh[k], v_sh[k]], [F32] * 4,
                                           name="adamw_" + k, deps=deps)]
        else:
            res = _stream_call(step_fn, [w_sh[k], g, m_sh[k], v_sh[k]], 4, name="adamw_" + k, deps=deps)
        out_g[k], out_d[k], out_m[k], out_v[k] = [r[None] for r in res]
        return res[0]

    for k, joined in zip(g1 + g2, halves12):
        last = adamw_shard(k, joined, deps=(join3_token,))
    halves3 = _split_wait(j3ssem, j3rsem, halves3, last, _join_plan, name="join_g3_wait")
    for k, joined in zip(g3, halves3):
        adamw_shard(k, joined)

    order = ("norm_mix_g", "w_in", "ret_norm_g", "w_ret_o", "q_a_norm_g", "w_q_b", "kv_a_norm_g", "w_kv_b",
             "w_mla_o", "w_out", "norm_mlp_g", "w_up", "w_down", "norm_f_g")
    return (loss, dx.reshape(1, S, D), *[out_g[k] for k in order], *[out_d[k] for k in order],
            *[out_m[k] for k in order], *[out_v[k] for k in order])
```
